```python
import math
import jax, jax.numpy as jnp
from jax import lax
import numpy as np

D_MODEL = 1024
BATCH = 16
SEQ = 4096
DEPTH = 2

N_A = DEPTH // 2
N_B = DEPTH - N_A

E_A = D_MODEL
CONV_WIDTH = 3
CONV_GROUPS = 16

N_HEADS = 8
QK_NOPE = 64
QK_ROPE = 32
V_HEAD = 64
KV_RANK = 256
Q_RANK = 384
E_B = N_HEADS * V_HEAD
ROPE_THETA = 10000.0
Q_BLOCK = 128
SOFTMAX_SCALE = 1.0 / math.sqrt(QK_NOPE + QK_ROPE)

EPS = 1e-6

kernel_name = "yoco_shortconv_mla_hybrid"


def rms_norm(x, g):
    xf = x.astype(jnp.float32)
    y = xf * lax.rsqrt(jnp.mean(xf * xf, axis=-1, keepdims=True) + EPS)
    return (y * g.astype(jnp.float32)).astype(x.dtype)


def rope_tables(positions, dtype):
    inv_freq = ROPE_THETA ** (-jnp.arange(0, QK_ROPE, 2, dtype=jnp.float32) / QK_ROPE)
    ang = positions.astype(jnp.float32)[..., None] * inv_freq
    return jnp.cos(ang).astype(dtype), jnp.sin(ang).astype(dtype)


def apply_rope(x, cos, sin):
    x1, x2 = jnp.split(x, 2, axis=-1)
    return jnp.concatenate([x1 * cos - x2 * sin, x1 * sin + x2 * cos], axis=-1)


def conv_mixer(x, norm_g, w_in, conv_w, w_out):
    S = x.shape[1]
    h = rms_norm(x, norm_g)
    proj = h @ w_in
    b, c, u, g = jnp.split(proj, 4, axis=-1)
    v = c * u
    vp = jnp.pad(v, ((0, 0), (CONV_WIDTH - 1, 0), (0, 0)))
    conv = conv_w[0] * vp[:, 0:S]
    for k in range(1, CONV_WIDTH):
        conv = conv + conv_w[k] * vp[:, k:k + S]
    y = jax.nn.silu(g) * b * conv
    return y @ w_out


def shared_kv(s, kv_norm, w_dkv, ckv_norm, w_ukv, cos, sin):
    Bsz, S, _ = s.shape
    h = rms_norm(s, kv_norm)
    ckr = h @ w_dkv
    c_kv, k_rope = ckr[..., :KV_RANK], ckr[..., KV_RANK:]
    c_kv = rms_norm(c_kv, ckv_norm)
    kv = (c_kv @ w_ukv).reshape(Bsz, S, N_HEADS, QK_NOPE + V_HEAD)
    k_nope, v = kv[..., :QK_NOPE], kv[..., QK_NOPE:]
    k_rope = apply_rope(k_rope, cos, sin)
    return k_nope, k_rope, v


def causal_attention(q_nope, q_rope, k_nope, k_rope, v):
    Bsz, S, H, _ = q_nope.shape
    nb = S // Q_BLOCK
    kpos = jnp.arange(S)

    def one_block(args):
        qn, qr, i = args
        sc = (jnp.einsum('bqhd,bkhd->bhqk', qn, k_nope)
              + jnp.einsum('bqhr,bkr->bhqk', qr, k_rope))
        sc = sc.astype(jnp.float32) * SOFTMAX_SCALE
        qpos = i * Q_BLOCK + jnp.arange(Q_BLOCK)
        mask = kpos[None, :] <= qpos[:, None]
        sc = jnp.where(mask[None, None], sc, -jnp.inf)
        p = jax.nn.softmax(sc, axis=-1).astype(v.dtype)
        return jnp.einsum('bhqk,bkhv->bqhv', p, v)

    qn_b = q_nope.reshape(Bsz, nb, Q_BLOCK, H, QK_NOPE).transpose(1, 0, 2, 3, 4)
    qr_b = q_rope.reshape(Bsz, nb, Q_BLOCK, H, QK_ROPE).transpose(1, 0, 2, 3, 4)
    out = lax.map(one_block, (qn_b, qr_b, jnp.arange(nb)))
    return out.transpose(1, 0, 2, 3, 4).reshape(Bsz, S, H, V_HEAD)


def mla_layer(x, norm_g, w_in, q_norm, w_uq, w_out, k_nope, k_rope, v, cos, sin):
    Bsz, S, _ = x.shape
    h = rms_norm(x, norm_g)
    proj = h @ w_in
    c_q, g = proj[..., :Q_RANK], proj[..., Q_RANK:]
    q = (rms_norm(c_q, q_norm) @ w_uq).reshape(Bsz, S, N_HEADS, QK_NOPE + QK_ROPE)
    q_nope, q_rope = q[..., :QK_NOPE], q[..., QK_NOPE:]
    q_rope = apply_rope(q_rope, cos[:, :, None, :], sin[:, :, None, :])
    o = causal_attention(q_nope, q_rope, k_nope, k_rope, v).reshape(Bsz, S, E_B)
    return (o * jax.nn.silu(g)) @ w_out


def _fwd_setup_inputs(seed: int = 0) -> dict:
    key = jax.random.key(seed)
    ks = jax.random.split(key, 16)
    f32 = jnp.float32

    def w(k, shape, fan_in):
        return jax.random.normal(k, shape, f32) * (fan_in ** -0.5)

    def gain(k, shape):
        return 1.0 + 0.1 * jax.random.normal(k, shape, f32)

    x = jax.random.normal(ks[0], (BATCH, SEQ, D_MODEL), f32)
    positions = jnp.broadcast_to(jnp.arange(SEQ, dtype=jnp.int32), (BATCH, SEQ))
    return {
        "x": x,
        "positions": positions,
        "a_norm": gain(ks[1], (N_A, D_MODEL)),
        "a_w_in": w(ks[2], (N_A, D_MODEL, 4 * E_A), D_MODEL),
        "a_conv": w(ks[3], (N_A, CONV_WIDTH, E_A), CONV_WIDTH),
        "a_w_out": w(ks[4], (N_A, E_A, D_MODEL), E_A),
        "kv_norm": gain(ks[5], (D_MODEL,)),
        "w_dkv": w(ks[6], (D_MODEL, KV_RANK + QK_ROPE), D_MODEL),
        "ckv_norm": gain(ks[7], (KV_RANK,)),
        "w_ukv": w(ks[8], (KV_RANK, N_HEADS * (QK_NOPE + V_HEAD)), KV_RANK),
        "b_norm": gain(ks[9], (N_B, D_MODEL)),
        "b_w_in": w(ks[10], (N_B, D_MODEL, Q_RANK + E_B), D_MODEL),
        "b_q_norm": gain(ks[11], (N_B, Q_RANK)),
        "b_w_uq": w(ks[12], (N_B, Q_RANK, N_HEADS * (QK_NOPE + QK_ROPE)), Q_RANK),
        "b_w_out": w(ks[13], (N_B, E_B, D_MODEL), E_B),
        "final_norm": gain(ks[14], (D_MODEL,)),
    }


def _fwd_reference(x, positions, a_norm, a_w_in, a_conv, a_w_out, kv_norm, w_dkv, ckv_norm,
              w_ukv, b_norm, b_w_in, b_q_norm, b_w_uq, b_w_out, final_norm):
    cos, sin = rope_tables(positions, x.dtype)
    k_nope = k_rope = v = None
    for layer in range(DEPTH):
        if layer < N_A:
            x = x + conv_mixer(x, a_norm[layer], a_w_in[layer], a_conv[layer], a_w_out[layer])
            if layer == N_A - 1:
                k_nope, k_rope, v = shared_kv(x, kv_norm, w_dkv, ckv_norm, w_ukv, cos, sin)
        else:
            j = layer - N_A
            x = x + mla_layer(x, b_norm[j], b_w_in[j], b_q_norm[j], b_w_uq[j], b_w_out[j],
                              k_nope, k_rope, v, cos, sin)
    return rms_norm(x, final_norm)


import jax as _jax
import jax.numpy as _jnp

TWIN_FORMAT = 'train_step'
FWD_PARAMS = ['x', 'positions', 'a_norm', 'a_w_in', 'a_conv', 'a_w_out', 'kv_norm', 'w_dkv', 'ckv_norm', 'w_ukv', 'b_norm', 'b_w_in', 'b_q_norm', 'b_w_uq', 'b_w_out', 'final_norm']
TWIN_WEIGHTS = ['a_norm', 'a_w_in', 'a_conv', 'a_w_out', 'kv_norm', 'w_dkv', 'ckv_norm', 'w_ukv', 'b_norm', 'b_w_in', 'b_q_norm', 'b_w_uq', 'b_w_out', 'final_norm']
TWIN_DIFF_INPUT = 'x'
TWIN_INPUTS = ['x', 'positions', 'a_norm', 'a_w_in', 'a_conv', 'a_w_out', 'kv_norm', 'w_dkv', 'ckv_norm', 'w_ukv', 'b_norm', 'b_w_in', 'b_q_norm', 'b_w_uq', 'b_w_out', 'final_norm', 'loss_target', 'm_a_norm', 'm_a_w_in', 'm_a_conv', 'm_a_w_out', 'm_kv_norm', 'm_w_dkv', 'm_ckv_norm', 'm_w_ukv', 'm_b_norm', 'm_b_w_in', 'm_b_q_norm', 'm_b_w_uq', 'm_b_w_out', 'm_final_norm', 'v_a_norm', 'v_a_w_in', 'v_a_conv', 'v_a_w_out', 'v_kv_norm', 'v_w_dkv', 'v_ckv_norm', 'v_w_ukv', 'v_b_norm', 'v_b_w_in', 'v_b_q_norm', 'v_b_w_uq', 'v_b_w_out', 'v_final_norm']
TWIN_OUTPUTS = ['loss', 'grad_x', 'grad_a_norm', 'grad_a_w_in', 'grad_a_conv', 'grad_a_w_out', 'grad_kv_norm', 'grad_w_dkv', 'grad_ckv_norm', 'grad_w_ukv', 'grad_b_norm', 'grad_b_w_in', 'grad_b_q_norm', 'grad_b_w_uq', 'grad_b_w_out', 'grad_final_norm', 'delta_a_norm', 'delta_a_w_in', 'delta_a_conv', 'delta_a_w_out', 'delta_kv_norm', 'delta_w_dkv', 'delta_ckv_norm', 'delta_w_ukv', 'delta_b_norm', 'delta_b_w_in', 'delta_b_q_norm', 'delta_b_w_uq', 'delta_b_w_out', 'delta_final_norm', 'new_m_a_norm', 'new_m_a_w_in', 'new_m_a_conv', 'new_m_a_w_out', 'new_m_kv_norm', 'new_m_w_dkv', 'new_m_ckv_norm', 'new_m_w_ukv', 'new_m_b_norm', 'new_m_b_w_in', 'new_m_b_q_norm', 'new_m_b_w_uq', 'new_m_b_w_out', 'new_m_final_norm', 'new_v_a_norm', 'new_v_a_w_in', 'new_v_a_conv', 'new_v_a_w_out', 'new_v_kv_norm', 'new_v_w_dkv', 'new_v_ckv_norm', 'new_v_w_ukv', 'new_v_b_norm', 'new_v_b_w_in', 'new_v_b_q_norm', 'new_v_b_w_uq', 'new_v_b_w_out', 'new_v_final_norm']
TWIN_LEAF_KINDS = {'loss': 'loss', 'grad_x': 'grad_x', 'grad_a_norm': 'grad_w', 'grad_a_w_in': 'grad_w', 'grad_a_conv': 'grad_w', 'grad_a_w_out': 'grad_w', 'grad_kv_norm': 'grad_w', 'grad_w_dkv': 'grad_w', 'grad_ckv_norm': 'grad_w', 'grad_w_ukv': 'grad_w', 'grad_b_norm': 'grad_w', 'grad_b_w_in': 'grad_w', 'grad_b_q_norm': 'grad_w', 'grad_b_w_uq': 'grad_w', 'grad_b_w_out': 'grad_w', 'grad_final_norm': 'grad_w', 'delta_a_norm': 'delta_w', 'delta_a_w_in': 'delta_w', 'delta_a_conv': 'delta_w', 'delta_a_w_out': 'delta_w', 'delta_kv_norm': 'delta_w', 'delta_w_dkv': 'delta_w', 'delta_ckv_norm': 'delta_w', 'delta_w_ukv': 'delta_w', 'delta_b_norm': 'delta_w', 'delta_b_w_in': 'delta_w', 'delta_b_q_norm': 'delta_w', 'delta_b_w_uq': 'delta_w', 'delta_b_w_out': 'delta_w', 'delta_final_norm': 'delta_w', 'new_m_a_norm': 'new_m', 'new_m_a_w_in': 'new_m', 'new_m_a_conv': 'new_m', 'new_m_a_w_out': 'new_m', 'new_m_kv_norm': 'new_m', 'new_m_w_dkv': 'new_m', 'new_m_ckv_norm': 'new_m', 'new_m_w_ukv': 'new_m', 'new_m_b_norm': 'new_m', 'new_m_b_w_in': 'new_m', 'new_m_b_q_norm': 'new_m', 'new_m_b_w_uq': 'new_m', 'new_m_b_w_out': 'new_m', 'new_m_final_norm': 'new_m', 'new_v_a_norm': 'new_v', 'new_v_a_w_in': 'new_v', 'new_v_a_conv': 'new_v', 'new_v_a_w_out': 'new_v', 'new_v_kv_norm': 'new_v', 'new_v_w_dkv': 'new_v', 'new_v_ckv_norm': 'new_v', 'new_v_w_ukv': 'new_v', 'new_v_b_norm': 'new_v', 'new_v_b_w_in': 'new_v', 'new_v_b_q_norm': 'new_v', 'new_v_b_w_uq': 'new_v', 'new_v_b_w_out': 'new_v', 'new_v_final_norm': 'new_v'}


def _forward(args):
    return _fwd_reference(*[args[k] for k in FWD_PARAMS])


def _output_shape():
    out = _jax.eval_shape(lambda: _forward(_fwd_setup_inputs(0)))
    return out.shape, out.dtype

N_MICROBATCH = 1
ADAM_LR = 0.001
ADAM_B1 = 0.9
ADAM_B2 = 0.999
ADAM_EPS = 1e-08
ADAM_WD = 0.01
ADAM_STEP = 10
PER_EXAMPLE_BATCH_AXIS = {'x': 0, 'positions': 0, 'loss_target': 0}
SHARED_INPUTS = []
_WEIGHT_DTYPES = {'a_norm': _jnp.float32, 'a_w_in': _jnp.float32, 'a_conv': _jnp.float32, 'a_w_out': _jnp.float32, 'kv_norm': _jnp.float32, 'w_dkv': _jnp.float32, 'ckv_norm': _jnp.float32, 'w_ukv': _jnp.float32, 'b_norm': _jnp.float32, 'b_w_in': _jnp.float32, 'b_q_norm': _jnp.float32, 'b_w_uq': _jnp.float32, 'b_w_out': _jnp.float32, 'final_norm': _jnp.float32}
MOMENT_SCALE = {'a_norm': 2.686845e-01, 'a_w_in': 1.318099e-01, 'a_conv': 1.341426e-01, 'a_w_out': 1.370906e-01, 'kv_norm': 3.832719e-02, 'w_dkv': 6.942250e-02, 'ckv_norm': 7.248649e-02, 'w_ukv': 3.548447e-02, 'b_norm': 3.623126e-02, 'b_w_in': 4.061323e-02, 'b_q_norm': 3.795120e-02, 'b_w_uq': 2.779597e-02, 'b_w_out': 2.837796e-02, 'final_norm': 6.424022e+01}


def _to_microbatches(a, axis):
    t = _jnp.moveaxis(a, axis, 0)
    t = t.reshape((N_MICROBATCH, t.shape[0] // N_MICROBATCH) + t.shape[1:])
    return _jnp.moveaxis(t, 1, axis + 1)


def setup_inputs(seed: int = 0) -> dict:
    inp = _fwd_setup_inputs(seed)
    key = _jax.random.fold_in(_jax.random.key(seed), 7919)
    shape, _ = _output_shape()
    out = dict(inp)
    out["loss_target"] = _jax.random.normal(_jax.random.fold_in(key, 0), shape, _jnp.float32)
    for i, name in enumerate(TWIN_WEIGHTS):
        w = inp[name].astype(_jnp.float32)
        if MOMENT_SCALE is None:
            s = _jnp.sqrt(_jnp.mean(_jnp.square(w)) + 1e-30)
        else:
            s = MOMENT_SCALE[name]
        km, kv = _jax.random.split(_jax.random.fold_in(key, i + 1))
        out[name] = w
        out["m_" + name] = s * _jax.random.normal(km, w.shape, _jnp.float32)
        out["v_" + name] = (s * s) * _jax.random.uniform(kv, w.shape, _jnp.float32, 0.5, 1.5)
    if N_MICROBATCH > 1:
        for name, axis in PER_EXAMPLE_BATCH_AXIS.items():
            out[name] = _to_microbatches(out[name], axis)
    return {'x': out['x'], 'positions': out['positions'], 'a_norm': out['a_norm'], 'a_w_in': out['a_w_in'], 'a_conv': out['a_conv'], 'a_w_out': out['a_w_out'], 'kv_norm': out['kv_norm'], 'w_dkv': out['w_dkv'], 'ckv_norm': out['ckv_norm'], 'w_ukv': out['w_ukv'], 'b_norm': out['b_norm'], 'b_w_in': out['b_w_in'], 'b_q_norm': out['b_q_norm'], 'b_w_uq': out['b_w_uq'], 'b_w_out': out['b_w_out'], 'final_norm': out['final_norm'], 'loss_target': out['loss_target'], 'm_a_norm': out['m_a_norm'], 'm_a_w_in': out['m_a_w_in'], 'm_a_conv': out['m_a_conv'], 'm_a_w_out': out['m_a_w_out'], 'm_kv_norm': out['m_kv_norm'], 'm_w_dkv': out['m_w_dkv'], 'm_ckv_norm': out['m_ckv_norm'], 'm_w_ukv': out['m_w_ukv'], 'm_b_norm': out['m_b_norm'], 'm_b_w_in': out['m_b_w_in'], 'm_b_q_norm': out['m_b_q_norm'], 'm_b_w_uq': out['m_b_w_uq'], 'm_b_w_out': out['m_b_w_out'], 'm_final_norm': out['m_final_norm'], 'v_a_norm': out['v_a_norm'], 'v_a_w_in': out['v_a_w_in'], 'v_a_conv': out['v_a_conv'], 'v_a_w_out': out['v_a_w_out'], 'v_kv_norm': out['v_kv_norm'], 'v_w_dkv': out['v_w_dkv'], 'v_ckv_norm': out['v_ckv_norm'], 'v_w_ukv': out['v_w_ukv'], 'v_b_norm': out['v_b_norm'], 'v_b_w_in': out['v_b_w_in'], 'v_b_q_norm': out['v_b_q_norm'], 'v_b_w_uq': out['v_b_w_uq'], 'v_b_w_out': out['v_b_w_out'], 'v_final_norm': out['v_final_norm']}


def _loss(weights, diff, rest, loss_target):
    with _jax.named_scope("forward"):
        args = {**rest, TWIN_DIFF_INPUT: diff, **{k: w.astype(_WEIGHT_DTYPES[k]) for k, w in weights.items()}}
        y = _forward(args)
    with _jax.named_scope("loss_head"):
        err = _jnp.square(y.astype(_jnp.float32) - loss_target)
        return 0.5 * _jnp.sum(_jnp.mean(err, axis=-1)) if err.ndim else 0.5 * err


def _adamw(w, g, m, v):
    m = ADAM_B1 * m + (1.0 - ADAM_B1) * g
    v = ADAM_B2 * v + (1.0 - ADAM_B2) * _jnp.square(g)
    m_hat = m / (1.0 - ADAM_B1 ** ADAM_STEP)
    v_hat = v / (1.0 - ADAM_B2 ** ADAM_STEP)
    delta = -ADAM_LR * (m_hat / (_jnp.sqrt(v_hat) + ADAM_EPS) + ADAM_WD * w)
    return delta, m, v


def reference(x, positions, a_norm, a_w_in, a_conv, a_w_out, kv_norm, w_dkv, ckv_norm, w_ukv, b_norm, b_w_in, b_q_norm, b_w_uq, b_w_out, final_norm, loss_target, m_a_norm, m_a_w_in, m_a_conv, m_a_w_out, m_kv_norm, m_w_dkv, m_ckv_norm, m_w_ukv, m_b_norm, m_b_w_in, m_b_q_norm, m_b_w_uq, m_b_w_out, m_final_norm, v_a_norm, v_a_w_in, v_a_conv, v_a_w_out, v_kv_norm, v_w_dkv, v_ckv_norm, v_w_ukv, v_b_norm, v_b_w_in, v_b_q_norm, v_b_w_uq, v_b_w_out, v_final_norm):
    given = dict(x=x, positions=positions, a_norm=a_norm, a_w_in=a_w_in, a_conv=a_conv, a_w_out=a_w_out, kv_norm=kv_norm, w_dkv=w_dkv, ckv_norm=ckv_norm, w_ukv=w_ukv, b_norm=b_norm, b_w_in=b_w_in, b_q_norm=b_q_norm, b_w_uq=b_w_uq, b_w_out=b_w_out, final_norm=final_norm, loss_target=loss_target, m_a_norm=m_a_norm, m_a_w_in=m_a_w_in, m_a_conv=m_a_conv, m_a_w_out=m_a_w_out, m_kv_norm=m_kv_norm, m_w_dkv=m_w_dkv, m_ckv_norm=m_ckv_norm, m_w_ukv=m_w_ukv, m_b_norm=m_b_norm, m_b_w_in=m_b_w_in, m_b_q_norm=m_b_q_norm, m_b_w_uq=m_b_w_uq, m_b_w_out=m_b_w_out, m_final_norm=m_final_norm, v_a_norm=v_a_norm, v_a_w_in=v_a_w_in, v_a_conv=v_a_conv, v_a_w_out=v_a_w_out, v_kv_norm=v_kv_norm, v_w_dkv=v_w_dkv, v_ckv_norm=v_ckv_norm, v_w_ukv=v_w_ukv, v_b_norm=v_b_norm, v_b_w_in=v_b_w_in, v_b_q_norm=v_b_q_norm, v_b_w_uq=v_b_w_uq, v_b_w_out=v_b_w_out, v_final_norm=v_final_norm)
    weights = {n: given[n] for n in TWIN_WEIGHTS}
    shared = {n: given[n] for n in SHARED_INPUTS}
    per_example = {n: given[n] for n in ['x', 'positions']}
    grad_fn = _jax.value_and_grad(_loss, argnums=(0, 1))

    def one_microbatch(ex, loss_target):
        ex = dict(ex)
        diff = ex.pop(TWIN_DIFF_INPUT)
        return grad_fn(weights, diff, {**shared, **ex}, loss_target)

    if N_MICROBATCH == 1:
        loss, (grad_w, grad_x) = one_microbatch(per_example, given["loss_target"])
    else:
        def body(carry, xs):
            loss_sum, grad_sum = carry
            l_k, (gw_k, gx_k) = one_microbatch(xs[0], xs[1])
            with _jax.named_scope("update"):
                return (loss_sum + l_k, _jax.tree.map(_jnp.add, grad_sum, gw_k)), gx_k

        init = (_jnp.zeros((), _jnp.float32), _jax.tree.map(_jnp.zeros_like, weights))
        (loss, grad_w), grad_x = _jax.lax.scan(body, init, (per_example, given["loss_target"]))
    with _jax.named_scope("update"):
        delta_w, new_m, new_v = {}, {}, {}
        for n in TWIN_WEIGHTS:
            delta_w[n], new_m[n], new_v[n] = _adamw(weights[n], grad_w[n], given["m_" + n], given["v_" + n])
    return (loss, grad_x, *[grad_w[n] for n in TWIN_WEIGHTS], *[delta_w[n] for n in TWIN_WEIGHTS],
            *[new_m[n] for n in TWIN_WEIGHTS], *[new_v[n] for n in TWIN_WEIGHTS])
```

```python
import functools
import math

import numpy as np
import jax
import jax.numpy as jnp
from jax import lax
from jax.experimental import pallas as pl
from jax.experimental.pallas import tpu as pltpu

F32 = jnp.float32
BF16 = jnp.bfloat16

D_MODEL = 1024
E_A = 1024
CONV_WIDTH = 3
N_HEADS = 8
QK_NOPE = 64
QK_ROPE = 32
V_HEAD = 64
KV_RANK = 256
Q_RANK = 384
E_B = N_HEADS * V_HEAD
ROPE_THETA = 10000.0
SOFTMAX_SCALE = 1.0 / math.sqrt(QK_NOPE + QK_ROPE)
EPS = 1e-6
HEAD_PAD = 128
QK_PAD = N_HEADS * HEAD_PAD
ROPE_LO = QK_NOPE
ROPE_HALF = QK_ROPE // 2
KR_PAD = KV_RANK + HEAD_PAD

ADAM_LR = 0.001
ADAM_B1 = 0.9
ADAM_B2 = 0.999
ADAM_EPS = 1e-08
ADAM_WD = 0.01
ADAM_STEP = 10

VMEM_LIMIT = 56 * 1024 * 1024
ROW_TILE = 256
ATT_TILE = 512
LANES = 128
PACK_W = 1024

N_CHIPS = 4
N_DEV = 8


def _dot(a, b):
    return jnp.dot(a, b, preferred_element_type=F32)


def _dot_nt(a, b):
    return lax.dot_general(a, b, (((1,), (1,)), ((), ())), preferred_element_type=F32)


def _dot_tn(a, b):
    return lax.dot_general(a, b, (((0,), (0,)), ((), ())), preferred_element_type=F32)


def _rms(x):
    r = lax.rsqrt(jnp.mean(x * x, axis=-1, keepdims=True) + EPS)
    return r, x * r


def _rms_bwd(dxh, xh, r):
    return r * (dxh - xh * jnp.mean(dxh * xh, axis=-1, keepdims=True))


def _rope_fwd(a, c, s1, s2):
    return a * c + pltpu.roll(a, HEAD_PAD - ROPE_HALF, 1) * s1 + pltpu.roll(a, ROPE_HALF, 1) * s2


def _rope_bwd(g, c, s1, s2):
    return g * c + pltpu.roll(g * s1, ROPE_HALF, 1) + pltpu.roll(g * s2, HEAD_PAD - ROPE_HALF, 1)


def _sigmoid(x):
    return 1.0 / (1.0 + jnp.exp(-x))


def _row_spec(tm, n):
    return pl.BlockSpec((tm, n), lambda i: (i, 0))


def _const_spec(shape):
    nd = len(shape)
    return pl.BlockSpec(shape, lambda i: (0,) * nd, pipeline_mode=pl.Buffered(1))


def _acc_spec(shape):
    nd = len(shape)
    return pl.BlockSpec(shape, lambda i: (0,) * nd, pipeline_mode=pl.Buffered(1))


def _params(*sem):
    return pltpu.CompilerParams(dimension_semantics=sem, vmem_limit_bytes=VMEM_LIMIT)


def _rope_consts():
    lane = np.arange(HEAD_PAD)
    first = (lane >= ROPE_LO) & (lane < ROPE_LO + ROPE_HALF)
    second = (lane >= ROPE_LO + ROPE_HALF) & (lane < ROPE_LO + QK_ROPE)
    f = np.where(first, lane - ROPE_LO, np.where(second, lane - ROPE_LO - ROPE_HALF, 0))
    inv = np.float32(ROPE_THETA) ** (-(2 * f).astype(np.float32) / np.float32(QK_ROPE))
    out = np.zeros((8, HEAD_PAD), np.float32)
    out[0] = inv
    out[1] = first
    out[2] = second
    out[3] = lane < ROPE_LO
    return jnp.asarray(out)


def _rope_tables(pos_col):
    t = pos_col.shape[0]
    tm = min(ROW_TILE, t)

    def body(p_ref, k_ref, c_ref, s1_ref, s2_ref):
        inv, first, second, nope = k_ref[0:1, :], k_ref[1:2, :], k_ref[2:3, :], k_ref[3:4, :]
        ang = p_ref[...].astype(F32) * inv
        cs, sn = jnp.cos(ang), jnp.sin(ang)
        c_ref[...] = cs * (first + second) + nope
        s1_ref[...] = -sn * first
        s2_ref[...] = sn * second

    out = jax.ShapeDtypeStruct((t, HEAD_PAD), F32)
    return pl.pallas_call(
        body, grid=(t // tm,), name="rope_tables",
        in_specs=[_row_spec(tm, 1), _const_spec((8, HEAD_PAD))],
        out_specs=[_row_spec(tm, HEAD_PAD)] * 3, out_shape=[out] * 3,
        compiler_params=_params("parallel"),
    )(pos_col, _rope_consts())


def _shift_down(v, prev, row):
    p1, p2 = prev[7:8, :], prev[6:7, :]
    v1 = jnp.where(row == 0, p1, pltpu.roll(v, 1, 0))
    v2 = jnp.where(row == 0, p2, jnp.where(row == 1, p1, pltpu.roll(v, 2, 0)))
    return v1, v2


def _conv_fwd(x, seq, ga, w_in4, cw):
    t = x.shape[0]
    tm = min(ROW_TILE, seq)
    tiles_per_seq = seq // tm

    def body(x_ref, ga_ref, w_ref, cw_ref, b_ref, c_ref, u_ref, g_ref, ym_ref, carry_ref):
        i = pl.program_id(0)

        @pl.when(i % tiles_per_seq == 0)
        def _():
            carry_ref[...] = jnp.zeros_like(carry_ref)

        _, xh = _rms(x_ref[...])
        h = (xh * ga_ref[...]).astype(BF16)
        b, c, u, g = (_dot(h, w_ref[j]) for j in range(4))
        v = c * u
        row = lax.broadcasted_iota(jnp.int32, (tm, 1), 0)
        v1, v2 = _shift_down(v, carry_ref[...], row)
        carry_ref[...] = v[tm - 8:tm, :]
        cv = cw_ref[2:3, :] * v + cw_ref[1:2, :] * v1 + cw_ref[0:1, :] * v2
        b_ref[...] = b.astype(BF16)
        c_ref[...] = c.astype(BF16)
        u_ref[...] = u.astype(BF16)
        g_ref[...] = g.astype(BF16)
        ym_ref[...] = (g * _sigmoid(g) * b * cv).astype(BF16)

    out = jax.ShapeDtypeStruct((t, E_A), BF16)
    return pl.pallas_call(
        body, grid=(t // tm,), name="conv_fwd",
        in_specs=[_row_spec(tm, D_MODEL), _const_spec((1, D_MODEL)), _const_spec((4, D_MODEL, E_A)),
                  _const_spec((8, E_A))],
        out_specs=[_row_spec(tm, E_A)] * 5, out_shape=[out] * 5,
        scratch_shapes=[pltpu.VMEM((8, E_A), F32)],
        compiler_params=_params("arbitrary"),
    )(x, ga, w_in4, cw)


def _mid_fwd(x, ym, w_out, gk, gb, w_dkv, gc, w_uk, w_uv, w_bin, gq, w_uq, rc, rs1, rs2):
    t = x.shape[0]
    tm = min(ROW_TILE, t)

    def body(x_ref, ym_ref, wo_ref, gk_ref, gb_ref, wd_ref, gc_ref, wk_ref, wv_ref, wi_ref, gq_ref, wq_ref,
             c_ref, s1_ref, s2_ref, x1_ref, q_ref, k_ref, v_ref, gate_ref, cq_ref, ckv_ref):
        cb, s1b, s2b = c_ref[...], s1_ref[...], s2_ref[...]
        x1 = x_ref[...] + _dot(ym_ref[...], wo_ref[...])
        x1_ref[...] = x1
        _, xh = _rms(x1)
        hk = (xh * gk_ref[...]).astype(BF16)
        h1 = (xh * gb_ref[...]).astype(BF16)

        ckr = _dot(hk, wd_ref[...])
        ckv_raw = ckr[:, :KV_RANK]
        ckv_ref[...] = ckv_raw.astype(BF16)
        _, ch = _rms(ckv_raw)
        ckv = (ch * gc_ref[...]).astype(BF16)
        kr = _rope_fwd(ckr[:, KV_RANK:], cb, s1b, s2b)
        kn = _dot(ckv, wk_ref[...])
        for h in range(N_HEADS):
            sl = slice(h * HEAD_PAD, (h + 1) * HEAD_PAD)
            k_ref[:, sl] = (kn[:, sl] + kr).astype(BF16)
        v_ref[...] = _dot(ckv, wv_ref[...]).astype(BF16)

        pb = _dot(h1, wi_ref[...])
        cq = pb[:, :Q_RANK]
        cq_ref[...] = cq.astype(BF16)
        gate_ref[...] = pb[:, Q_RANK:].astype(BF16)
        _, cqh = _rms(cq)
        q = _dot((cqh * gq_ref[...]).astype(BF16), wq_ref[...])
        for h in range(N_HEADS):
            sl = slice(h * HEAD_PAD, (h + 1) * HEAD_PAD)
            q_ref[:, sl] = _rope_fwd(q[:, sl], cb, s1b, s2b).astype(BF16)

    def sds(n, dt):
        return jax.ShapeDtypeStruct((t, n), dt)

    return pl.pallas_call(
        body, grid=(t // tm,), name="mid_fwd",
        in_specs=[_row_spec(tm, D_MODEL), _row_spec(tm, E_A), _const_spec((E_A, D_MODEL)),
                  _const_spec((1, D_MODEL)), _const_spec((1, D_MODEL)), _const_spec((D_MODEL, KR_PAD)),
                  _const_spec((1, KV_RANK)), _const_spec((KV_RANK, QK_PAD)), _const_spec((KV_RANK, E_B)),
                  _const_spec((D_MODEL, Q_RANK + E_B)), _const_spec((1, Q_RANK)), _const_spec((Q_RANK, QK_PAD)),
                  _row_spec(tm, HEAD_PAD), _row_spec(tm, HEAD_PAD), _row_spec(tm, HEAD_PAD)],
        out_specs=[_row_spec(tm, D_MODEL), _row_spec(tm, QK_PAD), _row_spec(tm, QK_PAD), _row_spec(tm, E_B),
                   _row_spec(tm, E_B), _row_spec(tm, Q_RANK), _row_spec(tm, KV_RANK)],
        out_shape=[sds(D_MODEL, F32), sds(QK_PAD, BF16), sds(QK_PAD, BF16), sds(E_B, BF16), sds(E_B, BF16),
                   sds(Q_RANK, BF16), sds(KV_RANK, BF16)],
        compiler_params=_params("parallel"),
    )(x, ym, w_out, gk, gb, w_dkv, gc, w_uk, w_uv, w_bin, gq, w_uq, rc, rs1, rs2)


def _pair_specs(seq):
    qk = pl.BlockSpec((seq, 2 * HEAD_PAD), lambda b, p: (b, p))
    vo = pl.BlockSpec((seq, 2 * V_HEAD), lambda b, p: (b, p))
    st = pl.BlockSpec((2, seq, 1), lambda b, p: (p, b, 0))
    return qk, vo, st


def _attn_fwd(q, k, v, seq):
    t = q.shape[0]
    tq = min(ATT_TILE, seq)
    nq = seq // tq

    def body(q_ref, k_ref, v_ref, o_ref, lse_ref):
        row = lax.broadcasted_iota(jnp.int32, (tq, tq), 0)
        col = lax.broadcasted_iota(jnp.int32, (tq, tq), 1)
        lane = lax.broadcasted_iota(jnp.int32, (tq, 2 * V_HEAD), 1)

        def q_step(qi, _):
            q0 = pl.multiple_of(qi * tq, tq)
            outs = []
            for hh in range(2):
                hs = slice(hh * HEAD_PAD, (hh + 1) * HEAD_PAD)
                qt = q_ref[pl.ds(q0, tq), hs]

                def k_step(ki, carry, masked):
                    m, l, acc = carry
                    k0 = pl.multiple_of(ki * tq, tq)
                    s = _dot_nt(qt, k_ref[pl.ds(k0, tq), hs]) * SOFTMAX_SCALE
                    if masked:
                        s = jnp.where(col <= row, s, -jnp.inf)
                    m_new = jnp.maximum(m, jnp.max(s, axis=-1, keepdims=True))
                    p = jnp.exp(s - m_new)
                    alpha = jnp.exp(m - m_new)
                    l = alpha * l + jnp.sum(p, axis=-1, keepdims=True)
                    acc = alpha * acc + _dot(p.astype(BF16), v_ref[pl.ds(k0, tq), :])
                    return m_new, l, acc

                init = (jnp.full((tq, 1), -jnp.inf, F32), jnp.zeros((tq, 1), F32),
                        jnp.zeros((tq, 2 * V_HEAD), F32))
                carry = lax.fori_loop(0, qi, functools.partial(k_step, masked=False), init)
                m, l, acc = k_step(qi, carry, True)
                outs.append(acc / l)
                lse_ref[hh, pl.ds(q0, tq), :] = m + jnp.log(l)
            o_ref[pl.ds(q0, tq), :] = jnp.where(lane < V_HEAD, outs[0], outs[1]).astype(BF16)
            return 0

        lax.fori_loop(0, nq, q_step, 0)

    qk, vo, st = _pair_specs(seq)
    return pl.pallas_call(
        body, grid=(t // seq, N_HEADS // 2), name="attn_fwd",
        in_specs=[qk, qk, vo], out_specs=[vo, st],
        out_shape=[jax.ShapeDtypeStruct((t, E_B), BF16), jax.ShapeDtypeStruct((N_HEADS, t, 1), F32)],
        compiler_params=_params("parallel", "parallel"),
    )(q, k, v)


def _head_fwd_bwd(o, gate, x1, tgt, w_bout, gf):
    t = o.shape[0]
    tm = min(ROW_TILE, t)

    def body(o_ref, gate_ref, x1_ref, tgt_ref, w_ref, gf_ref,
             dx2_ref, do_ref, dgate_ref, dd_ref, loss_ref, dgf_ref, dw_ref):
        @pl.when(pl.program_id(0) == 0)
        def _():
            loss_ref[...] = jnp.zeros_like(loss_ref)
            dgf_ref[...] = jnp.zeros_like(dgf_ref)
            dw_ref[...] = jnp.zeros_like(dw_ref)

        o = o_ref[...].astype(F32)
        gt = gate_ref[...].astype(F32)
        sg = _sigmoid(gt)
        silu = gt * sg
        z = (o * silu).astype(BF16)
        x2 = x1_ref[...] + _dot(z, w_ref[...])
        r2, xh2 = _rms(x2)
        gf = gf_ref[...]
        err = xh2 * gf - tgt_ref[...]
        loss_ref[...] += 0.5 * jnp.sum(jnp.mean(err * err, axis=-1, keepdims=True))
        dy = err * (1.0 / D_MODEL)
        dgf_ref[...] += jnp.sum(dy * xh2, axis=0, keepdims=True)
        dx2 = _rms_bwd(dy * gf, xh2, r2)
        dx2_ref[...] = dx2
        dx2b = dx2.astype(BF16)
        dw_ref[...] += _dot_tn(z, dx2b)
        dz = _dot_nt(dx2b, w_ref[...])
        do = dz * silu
        do_ref[...] = do.astype(BF16)
        dgate_ref[...] = (dz * o * (sg * (1.0 + gt * (1.0 - sg)))).astype(BF16)
        prod = do * o
        lane = lax.broadcasted_iota(jnp.int32, (tm, 2 * V_HEAD), 1)
        for p in range(N_HEADS // 2):
            blk = prod[:, p * 2 * V_HEAD:(p + 1) * 2 * V_HEAD]
            dd_ref[2 * p] = jnp.sum(jnp.where(lane < V_HEAD, blk, 0.0), axis=-1, keepdims=True)
            dd_ref[2 * p + 1] = jnp.sum(jnp.where(lane < V_HEAD, 0.0, blk), axis=-1, keepdims=True)

    return pl.pallas_call(
        body, grid=(t // tm,), name="head_fwd_bwd",
        in_specs=[_row_spec(tm, E_B), _row_spec(tm, E_B), _row_spec(tm, D_MODEL), _row_spec(tm, D_MODEL),
                  _const_spec((E_B, D_MODEL)), _const_spec((1, D_MODEL))],
        out_specs=[_row_spec(tm, D_MODEL), _row_spec(tm, E_B), _row_spec(tm, E_B),
                   pl.BlockSpec((N_HEADS, tm, 1), lambda i: (0, i, 0)),
                   _acc_spec((1, 1)), _acc_spec((1, D_MODEL)), _acc_spec((E_B, D_MODEL))],
        out_shape=[jax.ShapeDtypeStruct((t, D_MODEL), F32), jax.ShapeDtypeStruct((t, E_B), BF16),
                   jax.ShapeDtypeStruct((t, E_B), BF16), jax.ShapeDtypeStruct((N_HEADS, t, 1), F32),
                   jax.ShapeDtypeStruct((1, 1), F32), jax.ShapeDtypeStruct((1, D_MODEL), F32),
                   jax.ShapeDtypeStruct((E_B, D_MODEL), F32)],
        compiler_params=_params("arbitrary"),
    )(o, gate, x1, tgt, w_bout, gf)


def _attn_bwd(q, k, v, do, lse, dd, seq):
    t = q.shape[0]
    tq = min(ATT_TILE, seq)
    nq = seq // tq

    def body(q_ref, k_ref, v_ref, do_ref, lse_ref, dd_ref, dq_ref, dk_ref, dv_ref, dq_acc):
        row = lax.broadcasted_iota(jnp.int32, (tq, tq), 0)
        col = lax.broadcasted_iota(jnp.int32, (tq, tq), 1)
        lane = lax.broadcasted_iota(jnp.int32, (tq, 2 * V_HEAD), 1)
        dq_acc[...] = jnp.zeros_like(dq_acc)

        def k_step(ki, _):
            k0 = pl.multiple_of(ki * tq, tq)
            vt = v_ref[pl.ds(k0, tq), :]
            dv_tot = jnp.zeros((tq, 2 * V_HEAD), F32)
            for hh in range(2):
                hs = slice(hh * HEAD_PAD, (hh + 1) * HEAD_PAD)
                kt = k_ref[pl.ds(k0, tq), hs]
                mine = (lane < V_HEAD) if hh == 0 else (lane >= V_HEAD)

                def q_step(qi, carry, masked):
                    dk, dv = carry
                    q0 = pl.multiple_of(qi * tq, tq)
                    qt = q_ref[pl.ds(q0, tq), hs]
                    dot = jnp.where(mine, do_ref[pl.ds(q0, tq), :], jnp.zeros((), BF16))
                    s = _dot_nt(qt, kt) * SOFTMAX_SCALE
                    if masked:
                        s = jnp.where(col <= row, s, -jnp.inf)
                    p = jnp.exp(s - lse_ref[hh, pl.ds(q0, tq), :])
                    dv = dv + _dot_tn(p.astype(BF16), dot)
                    dp = _dot_nt(dot, vt)
                    ds = (p * (dp - dd_ref[hh, pl.ds(q0, tq), :]) * SOFTMAX_SCALE).astype(BF16)
                    dk = dk + _dot_tn(ds, qt)
                    dq_acc[pl.ds(q0, tq), hs] += _dot(ds, kt)
                    return dk, dv

                init = (jnp.zeros((tq, HEAD_PAD), F32), jnp.zeros((tq, 2 * V_HEAD), F32))
                carry = q_step(ki, init, True)
                dk, dv = lax.fori_loop(ki + 1, nq, functools.partial(q_step, masked=False), carry)
                dk_ref[pl.ds(k0, tq), hs] = dk.astype(BF16)
                dv_tot = dv_tot + dv
            dv_ref[pl.ds(k0, tq), :] = dv_tot.astype(BF16)
            return 0

        lax.fori_loop(0, nq, k_step, 0)
        dq_ref[...] = dq_acc[...].astype(BF16)

    qk, vo, st = _pair_specs(seq)
    return pl.pallas_call(
        body, grid=(t // seq, N_HEADS // 2), name="attn_bwd",
        in_specs=[qk, qk, vo, vo, st, st], out_specs=[qk, qk, vo],
        out_shape=[jax.ShapeDtypeStruct((t, QK_PAD), BF16), jax.ShapeDtypeStruct((t, QK_PAD), BF16),
                   jax.ShapeDtypeStruct((t, E_B), BF16)],
        scratch_shapes=[pltpu.VMEM((seq, 2 * HEAD_PAD), F32)],
        compiler_params=_params("parallel", "parallel"),
    )(q, k, v, do, lse, dd)


def _mid_bwd(dq, dk, dv, dgate, dx2, x1, cq, ckv, rc, rs1, rs2, w_uq, w_bin, w_uk, w_uv, w_dkv, gq, gc, gk, gb):
    t = dq.shape[0]
    tm = min(ROW_TILE, t)

    def body(dq_ref, dk_ref, dv_ref, dgate_ref, dx2_ref, x1_ref, cq_ref, ckv_ref, c_ref, s1_ref, s2_ref,
             wq_ref, wi_ref, wk_ref, wv_ref, wd_ref, gq_ref, gc_ref, gk_ref, gb_ref,
             dx1_ref, dwq_ref, dwi_ref, dwk_ref, dwv_ref, dwd_ref, dgq_ref, dgc_ref, dgk_ref, dgb_ref):
        @pl.when(pl.program_id(0) == 0)
        def _():
            for ref in (dwq_ref, dwi_ref, dwk_ref, dwv_ref, dwd_ref, dgq_ref, dgc_ref, dgk_ref, dgb_ref):
                ref[...] = jnp.zeros_like(ref)

        cb, s1b, s2b = c_ref[...], s1_ref[...], s2_ref[...]
        r1, xh = _rms(x1_ref[...])
        gk, gb, gq, gc = gk_ref[...], gb_ref[...], gq_ref[...], gc_ref[...]
        hk = (xh * gk).astype(BF16)
        h1 = (xh * gb).astype(BF16)

        dqs = [_rope_bwd(dq_ref[:, h * HEAD_PAD:(h + 1) * HEAD_PAD].astype(F32), cb, s1b, s2b)
               for h in range(N_HEADS)]
        dqb = jnp.concatenate(dqs, axis=-1).astype(BF16)
        rq, cqh = _rms(cq_ref[...].astype(F32))
        dwq_ref[...] += _dot_tn((cqh * gq).astype(BF16), dqb)
        dcqn = _dot_nt(dqb, wq_ref[...])
        dgq_ref[...] += jnp.sum(dcqn * cqh, axis=0, keepdims=True)
        dcq = _rms_bwd(dcqn * gq, cqh, rq)
        dpb = jnp.concatenate([dcq.astype(BF16), dgate_ref[...]], axis=-1)
        dwi_ref[...] += _dot_tn(h1, dpb)
        dh1 = _dot_nt(dpb, wi_ref[...])

        rcv, ch = _rms(ckv_ref[...].astype(F32))
        ckvn = (ch * gc).astype(BF16)
        dkb, dvb = dk_ref[...], dv_ref[...]
        dwk_ref[...] += _dot_tn(ckvn, dkb)
        dwv_ref[...] += _dot_tn(ckvn, dvb)
        dckv = _dot_nt(dkb, wk_ref[...]) + _dot_nt(dvb, wv_ref[...])
        dgc_ref[...] += jnp.sum(dckv * ch, axis=0, keepdims=True)
        dckv_raw = _rms_bwd(dckv * gc, ch, rcv)
        dkr = dk_ref[:, 0:HEAD_PAD].astype(F32)
        for h in range(1, N_HEADS):
            dkr = dkr + dk_ref[:, h * HEAD_PAD:(h + 1) * HEAD_PAD].astype(F32)
        dkr = _rope_bwd(dkr, cb, s1b, s2b)
        dckr = jnp.concatenate([dckv_raw, dkr], axis=-1).astype(BF16)
        dwd_ref[...] += _dot_tn(hk, dckr)
        dhk = _dot_nt(dckr, wd_ref[...])

        dgb_ref[...] += jnp.sum(dh1 * xh, axis=0, keepdims=True)
        dgk_ref[...] += jnp.sum(dhk * xh, axis=0, keepdims=True)
        dx1_ref[...] = dx2_ref[...] + _rms_bwd(dh1 * gb + dhk * gk, xh, r1)

    acc_shapes = [(Q_RANK, QK_PAD), (D_MODEL, Q_RANK + E_B), (KV_RANK, QK_PAD), (KV_RANK, E_B), (D_MODEL, KR_PAD),
                  (1, Q_RANK), (1, KV_RANK), (1, D_MODEL), (1, D_MODEL)]
    return pl.pallas_call(
        body, grid=(t // tm,), name="mid_bwd",
        in_specs=[_row_spec(tm, QK_PAD), _row_spec(tm, QK_PAD), _row_spec(tm, E_B), _row_spec(tm, E_B),
                  _row_spec(tm, D_MODEL), _row_spec(tm, D_MODEL), _row_spec(tm, Q_RANK), _row_spec(tm, KV_RANK),
                  _row_spec(tm, HEAD_PAD), _row_spec(tm, HEAD_PAD), _row_spec(tm, HEAD_PAD),
                  _const_spec((Q_RANK, QK_PAD)), _const_spec((D_MODEL, Q_RANK + E_B)),
                  _const_spec((KV_RANK, QK_PAD)), _const_spec((KV_RANK, E_B)), _const_spec((D_MODEL, KR_PAD)),
                  _const_spec((1, Q_RANK)), _const_spec((1, KV_RANK)), _const_spec((1, D_MODEL)),
                  _const_spec((1, D_MODEL))],
        out_specs=[_row_spec(tm, D_MODEL)] + [_acc_spec(s) for s in acc_shapes],
        out_shape=[jax.ShapeDtypeStruct((t, D_MODEL), F32)] + [jax.ShapeDtypeStruct(s, F32) for s in acc_shapes],
        compiler_params=_params("arbitrary"),
    )(dq, dk, dv, dgate, dx2, x1, cq, ckv, rc, rs1, rs2, w_uq, w_bin, w_uk, w_uv, w_dkv, gq, gc, gk, gb)


def _conv_bwd(dx1, x, b, c, u, g, seq, w_out, w_in4, ga, cw):
    t = x.shape[0]
    tm = min(ROW_TILE, seq)
    tiles_per_seq = seq // tm
    n = t // tm
    halo = tm // 8

    def tile(i):
        return n - 1 - i

    def rev(width):
        return pl.BlockSpec((tm, width), lambda i: (tile(i), 0))

    def prev8(width):
        return pl.BlockSpec((8, width), lambda i: (jnp.maximum(tile(i) * halo - 1, 0), 0))

    def body(dx1_ref, x_ref, b_ref, c_ref, u_ref, g_ref, cp_ref, up_ref, wo_ref, wi_ref, ga_ref, cw_ref,
             dx_ref, dwi_ref, dwo_ref, dcw_ref, dga_ref, carry_ref):
        i = pl.program_id(0)
        j = tile(i)

        @pl.when(i == 0)
        def _():
            for ref in (dwi_ref, dwo_ref, dcw_ref, dga_ref):
                ref[...] = jnp.zeros_like(ref)

        @pl.when(j % tiles_per_seq == tiles_per_seq - 1)
        def _():
            carry_ref[...] = jnp.zeros_like(carry_ref)

        dx1 = dx1_ref[...]
        dx1b = dx1.astype(BF16)
        b, c, u, g = (r[...].astype(F32) for r in (b_ref, c_ref, u_ref, g_ref))
        v = c * u
        first = (j % tiles_per_seq == 0).astype(F32)
        vprev = cp_ref[...].astype(F32) * up_ref[...].astype(F32) * (1.0 - first)
        row = lax.broadcasted_iota(jnp.int32, (tm, 1), 0)
        v1, v2 = _shift_down(v, vprev, row)
        w0, w1, w2 = cw_ref[0:1, :], cw_ref[1:2, :], cw_ref[2:3, :]
        cv = w2 * v + w1 * v1 + w0 * v2
        sg = _sigmoid(g)
        silu = g * sg
        ym = (silu * b * cv).astype(BF16)
        dwo_ref[...] += _dot_tn(ym, dx1b)
        dym = _dot_nt(dx1b, wo_ref[...])
        db = dym * silu * cv
        dcv = dym * silu * b
        dg = dym * b * cv * (sg * (1.0 + g * (1.0 - sg)))

        nxt = carry_ref[...]
        n0, n1 = nxt[0:1, :], nxt[1:2, :]
        d1 = jnp.where(row == tm - 1, n0, pltpu.roll(dcv, tm - 1, 0))
        d2 = jnp.where(row == tm - 1, n1, jnp.where(row == tm - 2, n0, pltpu.roll(dcv, tm - 2, 0)))
        carry_ref[...] = dcv[0:8, :]
        dv = w2 * dcv + w1 * d1 + w0 * d2
        dcw_ref[0:1, :] += jnp.sum(dcv * v2, axis=0, keepdims=True)
        dcw_ref[1:2, :] += jnp.sum(dcv * v1, axis=0, keepdims=True)
        dcw_ref[2:3, :] += jnp.sum(dcv * v, axis=0, keepdims=True)

        r0, xh = _rms(x_ref[...])
        ga = ga_ref[...]
        h = (xh * ga).astype(BF16)
        dh = jnp.zeros((tm, D_MODEL), F32)
        for idx, dpart in enumerate((db, dv * u, dv * c, dg)):
            dpb = dpart.astype(BF16)
            dwi_ref[idx] += _dot_tn(h, dpb)
            dh = dh + _dot_nt(dpb, wi_ref[idx])
        dga_ref[...] += jnp.sum(dh * xh, axis=0, keepdims=True)
        dx_ref[...] = dx1 + _rms_bwd(dh * ga, xh, r0)

    acc_shapes = [(4, D_MODEL, E_A), (E_A, D_MODEL), (8, E_A), (1, D_MODEL)]
    return pl.pallas_call(
        body, grid=(n,), name="conv_bwd",
        in_specs=[rev(D_MODEL), rev(D_MODEL), rev(E_A), rev(E_A), rev(E_A), rev(E_A), prev8(E_A), prev8(E_A),
                  _const_spec((E_A, D_MODEL)), _const_spec((4, D_MODEL, E_A)), _const_spec((1, D_MODEL)),
                  _const_spec((8, E_A))],
        out_specs=[rev(D_MODEL)] + [_acc_spec(s) for s in acc_shapes],
        out_shape=[jax.ShapeDtypeStruct((t, D_MODEL), F32)] + [jax.ShapeDtypeStruct(s, F32) for s in acc_shapes],
        scratch_shapes=[pltpu.VMEM((8, E_A), F32)],
        compiler_params=_params("arbitrary"),
    )(dx1, x, b, c, u, g, c, u, w_out, w_in4, ga, cw)


def _pad_heads(w, width):
    kdim = w.shape[0]
    w = w.reshape(kdim, N_HEADS, width)
    return jnp.pad(w, ((0, 0), (0, 0), (0, HEAD_PAD - width))).reshape(kdim, QK_PAD)


def _unpad_heads(w, width):
    kdim = w.shape[0]
    return w.reshape(kdim, N_HEADS, HEAD_PAD)[:, :, :width].reshape(kdim, N_HEADS * width)


def _local_step(x, positions, tgt, wts):
    bsz, seq, _ = x.shape
    t = bsz * seq
    x2d = x.reshape(t, D_MODEL)
    row = lambda a: a.reshape(1, -1).astype(F32)

    w_in4 = wts["a_w_in4"]
    w_out = wts["a_w_out"]
    cw = jnp.pad(wts["a_conv"].astype(F32), ((0, 8 - CONV_WIDTH), (0, 0)))
    w_dkv = wts["w_dkv"]
    w_dkv_p = jnp.concatenate(
        [w_dkv[:, :KV_RANK], jnp.zeros((D_MODEL, ROPE_LO), BF16), w_dkv[:, KV_RANK:],
         jnp.zeros((D_MODEL, HEAD_PAD - ROPE_LO - QK_ROPE), BF16)], axis=1)
    w_ukv = wts["w_ukv"].reshape(KV_RANK, N_HEADS, 2, QK_NOPE)
    w_uk = _pad_heads(w_ukv[:, :, 0, :].reshape(KV_RANK, N_HEADS * QK_NOPE), QK_NOPE)
    w_uv = w_ukv[:, :, 1, :].reshape(KV_RANK, E_B)
    w_bin = wts["b_w_in"]
    w_uq = _pad_heads(wts["b_w_uq"], QK_NOPE + QK_ROPE)
    w_bout = wts["b_w_out"]
    ga, gk, gc, gb, gq, gf = (row(wts[n]) for n in ("a_norm", "kv_norm", "ckv_norm", "b_norm", "b_q_norm",
                                                   "final_norm"))

    rc, rs1, rs2 = _rope_tables(positions.reshape(t, 1))
    b, c, u, g, ym = _conv_fwd(x2d, seq, ga, w_in4, cw)
    x1, q, k, v, gate, cq, ckv = _mid_fwd(x2d, ym, w_out, gk, gb, w_dkv_p, gc, w_uk, w_uv, w_bin, gq, w_uq,
                                          rc, rs1, rs2)
    o, lse = _attn_fwd(q, k, v, seq)
    dx2, do, dgate, dd, loss, dgf, dw_bout = _head_fwd_bwd(o, gate, x1, tgt.reshape(t, D_MODEL), w_bout, gf)
    dq, dk, dv = _attn_bwd(q, k, v, do, lse, dd, seq)
    dx1, dwq, dw_bin, dwk, dwv, dwd, dgq, dgc, dgk, dgb = _mid_bwd(
        dq, dk, dv, dgate, dx2, x1, cq, ckv, rc, rs1, rs2, w_uq, w_bin, w_uk, w_uv, w_dkv_p, gq, gc, gk, gb)
    dx, dw_in4, dw_out, dcw, dga = _conv_bwd(dx1, x2d, b, c, u, g, seq, w_out, w_in4, ga, cw)

    dw_ukv = jnp.stack([_unpad_heads(dwk, QK_NOPE).reshape(KV_RANK, N_HEADS, QK_NOPE),
                        dwv.reshape(KV_RANK, N_HEADS, V_HEAD)], axis=2).reshape(KV_RANK, 2 * E_B)
    grads = {
        "a_norm": dga[0], "a_w_in4": dw_in4, "a_conv": dcw[:CONV_WIDTH], "a_w_out": dw_out,
        "kv_norm": dgk[0],
        "w_dkv": jnp.concatenate([dwd[:, :KV_RANK], dwd[:, KV_RANK + ROPE_LO:KV_RANK + ROPE_LO + QK_ROPE]], axis=1),
        "ckv_norm": dgc[0], "w_ukv": dw_ukv, "b_norm": dgb[0], "b_w_in": dw_bin, "b_q_norm": dgq[0],
        "b_w_uq": _unpad_heads(dwq, QK_NOPE + QK_ROPE), "b_w_out": dw_bout, "final_norm": dgf[0],
    }
    return loss[0, 0], dx.reshape(bsz, seq, D_MODEL), grads


MATS = ("a_w_in", "a_w_out", "w_dkv", "w_ukv", "b_w_in", "b_w_uq", "b_w_out")
SHARDED_SMALL = ("a_norm", "a_conv")
REPLICATED = ("kv_norm", "ckv_norm", "b_norm", "b_q_norm", "final_norm")
WEIGHTS = ("a_norm", "a_w_in", "a_conv", "a_w_out", "kv_norm", "w_dkv", "ckv_norm", "w_ukv", "b_norm", "b_w_in",
           "b_q_norm", "b_w_uq", "b_w_out", "final_norm")
SHARD_SHAPES = {
    "a_norm": (1, 256), "a_w_in": (1, 1024, 1024), "a_conv": (1, 3, 256), "a_w_out": (1, 256, 1024),
    "kv_norm": (1024,), "w_dkv": (256, 288), "ckv_norm": (256,), "w_ukv": (256, 256), "b_norm": (1, 1024),
    "b_w_in": (1, 256, 896), "b_q_norm": (1, 384), "b_w_uq": (1, 384, 192), "b_w_out": (1, 512, 256),
    "final_norm": (1024,),
}
PACK_ORDER = MATS + SHARDED_SMALL + REPLICATED


def _layout():
    off, table = 0, {}
    for name in PACK_ORDER:
        size = int(np.prod(SHARD_SHAPES[name]))
        table[name] = (off, size)
        off += size
        if name == MATS[-1] or name == SHARDED_SMALL[-1]:
            off = -(-off // PACK_W) * PACK_W
    rows = -(-off // PACK_W)
    return table, -(-rows // 32) * 32


PACK_TABLE, PACK_ROWS = _layout()
HALF_ROWS = PACK_ROWS // 2


def _pack(parts, dtype):
    flat = []
    off = 0
    for name in PACK_ORDER:
        start, size = PACK_TABLE[name]
        if start > off:
            flat.append(jnp.zeros((start - off,), dtype))
        flat.append(parts[name].reshape(-1).astype(dtype))
        off = start + size
    flat.append(jnp.zeros((PACK_ROWS * PACK_W - off,), dtype))
    return jnp.concatenate(flat).reshape(PACK_ROWS, PACK_W)


def _unpack(buf, names=PACK_ORDER):
    flat = buf.reshape(-1)
    return {n: flat[PACK_TABLE[n][0]:PACK_TABLE[n][0] + PACK_TABLE[n][1]].reshape(SHARD_SHAPES[n]) for n in names}


def _shard_of(name, full, j):
    if name == "a_w_in4":
        return full[j]
    if name in ("a_w_out", "w_dkv", "b_w_in"):
        rows = full.shape[0] // N_CHIPS
        return full[j * rows:(j + 1) * rows]
    if name in REPLICATED:
        return full
    cols = full.shape[-1] // N_CHIPS
    return full[..., j * cols:(j + 1) * cols]


def _full_weights(gath, small):
    def mat(name):
        start, size = PACK_TABLE[name]
        r0, r1 = start // PACK_W, (start + size) // PACK_W
        return gath[:, r0:r1, :].reshape((N_CHIPS,) + SHARD_SHAPES[name])

    def cols(w):
        return jnp.transpose(w, (1, 0, 2)).reshape(w.shape[1], -1)

    sm = small.reshape(N_CHIPS, -1)
    return {
        "a_w_in4": mat("a_w_in")[:, 0],
        "a_w_out": mat("a_w_out")[:, 0].reshape(E_A, D_MODEL),
        "w_dkv": mat("w_dkv").reshape(D_MODEL, KV_RANK + QK_ROPE),
        "w_ukv": cols(mat("w_ukv")),
        "b_w_in": mat("b_w_in")[:, 0].reshape(D_MODEL, Q_RANK + E_B),
        "b_w_uq": cols(mat("b_w_uq")[:, 0]),
        "b_w_out": cols(mat("b_w_out")[:, 0]),
        "a_norm": sm[:, :256].reshape(-1),
        "a_conv": jnp.transpose(sm[:, 256:1024].reshape(N_CHIPS, CONV_WIDTH, 256), (1, 0, 2)).reshape(CONV_WIDTH, -1),
    }


MESH = pl.DeviceIdType.MESH
HBM = pl.BlockSpec(memory_space=pl.ANY)


def _place():
    return lax.axis_index("x"), lax.axis_index("y"), lax.axis_index("c")


def _gather_weights(shard, small):
    flips = ((1, 0), (0, 1), (1, 1))

    def body(sh_ref, sm_ref, out_ref, osm_ref, send, recv, send_s, recv_s, local):
        x, y, c = _place()
        mine = 2 * x + y
        own = [pltpu.make_async_copy(sh_ref, out_ref.at[mine], local.at[0]),
               pltpu.make_async_copy(sm_ref, osm_ref.at[mine], local.at[1])]
        for cp in own:
            cp.start()

        def copies(i, fx, fy):
            px, py = x ^ fx, y ^ fy
            peer = 2 * px + py
            big = pltpu.make_async_remote_copy(src_ref=sh_ref, dst_ref=out_ref.at[mine], send_sem=send.at[i],
                                               recv_sem=recv.at[i], device_id=(px, py, c), device_id_type=MESH)
            sml = pltpu.make_async_remote_copy(src_ref=sm_ref, dst_ref=osm_ref.at[mine], send_sem=send_s.at[i],
                                               recv_sem=recv_s.at[i], device_id=(px, py, c), device_id_type=MESH)
            got = pltpu.make_async_remote_copy(src_ref=sh_ref, dst_ref=out_ref.at[peer], send_sem=send.at[i],
                                               recv_sem=recv.at[i], device_id=(px, py, c), device_id_type=MESH)
            got_s = pltpu.make_async_remote_copy(src_ref=sm_ref, dst_ref=osm_ref.at[peer], send_sem=send_s.at[i],
                                                 recv_sem=recv_s.at[i], device_id=(px, py, c), device_id_type=MESH)
            return big, sml, got, got_s

        all_copies = [copies(i, fx, fy) for i, (fx, fy) in enumerate(flips)]
        for big, sml, _, _ in all_copies:
            big.start()
            sml.start()
        for big, sml, got, got_s in all_copies:
            got.wait_recv()
            got_s.wait_recv()
            big.wait_send()
            sml.wait_send()
        for cp in own:
            cp.wait()

    return pl.pallas_call(
        body, name="gather_weights",
        in_specs=[HBM, HBM], out_specs=[HBM, HBM],
        out_shape=[jax.ShapeDtypeStruct((N_CHIPS,) + shard.shape, shard.dtype),
                   jax.ShapeDtypeStruct((N_CHIPS,) + small.shape, small.dtype)],
        scratch_shapes=[pltpu.SemaphoreType.DMA((3,)), pltpu.SemaphoreType.DMA((3,)),
                        pltpu.SemaphoreType.DMA((3,)), pltpu.SemaphoreType.DMA((3,)),
                        pltpu.SemaphoreType.DMA((2,))],
    )(shard, small)


def _scatter_partials(parts):
    def body(p_ref, out_ref, send, recv, local):
        x, y, c = _place()
        me = 4 * x + 2 * y + c
        own = pltpu.make_async_copy(p_ref.at[me], out_ref.at[me], local)
        own.start()
        sends, recvs = [], []
        for d in range(1, N_DEV):
            px, py, pc = x ^ (d >> 2), y ^ ((d >> 1) & 1), c ^ (d & 1)
            peer = 4 * px + 2 * py + pc
            sends.append(pltpu.make_async_remote_copy(
                src_ref=p_ref.at[peer], dst_ref=out_ref.at[me], send_sem=send.at[d - 1], recv_sem=recv.at[d - 1],
                device_id=(px, py, pc), device_id_type=MESH))
            recvs.append(pltpu.make_async_remote_copy(
                src_ref=p_ref.at[peer], dst_ref=out_ref.at[peer], send_sem=send.at[d - 1], recv_sem=recv.at[d - 1],
                device_id=(px, py, pc), device_id_type=MESH))
        for cp in sends:
            cp.start()
        for cp in recvs:
            cp.wait_recv()
        for cp in sends:
            cp.wait_send()
        own.wait()

    return pl.pallas_call(
        body, name="scatter_partials",
        in_specs=[HBM], out_specs=HBM, out_shape=jax.ShapeDtypeStruct(parts.shape, parts.dtype),
        scratch_shapes=[pltpu.SemaphoreType.DMA((N_DEV - 1,)), pltpu.SemaphoreType.DMA((N_DEV - 1,)),
                        pltpu.SemaphoreType.DMA],
    )(parts)


def _sum_partials(parts):
    _, rows, width = parts.shape
    tm = rows // 4

    def body(p_ref, o_ref):
        acc = p_ref[0]
        for kk in range(1, N_DEV):
            acc = acc + p_ref[kk]
        o_ref[...] = acc

    return pl.pallas_call(
        body, grid=(rows // tm,), name="sum_partials",
        in_specs=[pl.BlockSpec((N_DEV, tm, width), lambda i: (0, i, 0))],
        out_specs=pl.BlockSpec((tm, width), lambda i: (i, 0)),
        out_shape=jax.ShapeDtypeStruct((rows, width), parts.dtype),
        compiler_params=_params("parallel"),
    )(parts)


def _swap_halves(half):
    def body(h_ref, out_ref, send, recv, local):
        x, y, c = _place()
        own = pltpu.make_async_copy(h_ref, out_ref.at[c], local)
        own.start()
        give = pltpu.make_async_remote_copy(src_ref=h_ref, dst_ref=out_ref.at[c], send_sem=send, recv_sem=recv,
                                            device_id=(x, y, 1 - c), device_id_type=MESH)
        take = pltpu.make_async_remote_copy(src_ref=h_ref, dst_ref=out_ref.at[1 - c], send_sem=send, recv_sem=recv,
                                            device_id=(x, y, 1 - c), device_id_type=MESH)
        give.start()
        take.wait_recv()
        give.wait_send()
        own.wait()

    return pl.pallas_call(
        body, name="swap_halves",
        in_specs=[HBM], out_specs=HBM, out_shape=jax.ShapeDtypeStruct((2,) + half.shape, half.dtype),
        scratch_shapes=[pltpu.SemaphoreType.DMA, pltpu.SemaphoreType.DMA, pltpu.SemaphoreType.DMA],
    )(half)


def _adamw(w, g, m, v):
    rows, width = w.shape
    tm = rows // 8

    def body(w_ref, g_ref, m_ref, v_ref, d_ref, mo_ref, vo_ref):
        g = g_ref[...]
        m = ADAM_B1 * m_ref[...] + (1.0 - ADAM_B1) * g
        v = ADAM_B2 * v_ref[...] + (1.0 - ADAM_B2) * (g * g)
        m_hat = m / (1.0 - ADAM_B1 ** ADAM_STEP)
        v_hat = v / (1.0 - ADAM_B2 ** ADAM_STEP)
        d_ref[...] = -ADAM_LR * (m_hat / (jnp.sqrt(v_hat) + ADAM_EPS) + ADAM_WD * w_ref[...])
        mo_ref[...] = m
        vo_ref[...] = v

    spec = pl.BlockSpec((tm, width), lambda i: (i, 0))
    out = jax.ShapeDtypeStruct((rows, width), F32)
    return pl.pallas_call(
        body, grid=(rows // tm,), name="adamw",
        in_specs=[spec] * 4, out_specs=[spec] * 3, out_shape=[out] * 3,
        compiler_params=_params("parallel"),
    )(w, g, m, v)


def kernel(x, positions, a_norm, a_w_in, a_conv, a_w_out, kv_norm, w_dkv, ckv_norm, w_ukv, b_norm, b_w_in, b_q_norm, b_w_uq, b_w_out, final_norm, loss_target, m_a_norm, m_a_w_in, m_a_conv, m_a_w_out, m_kv_norm, m_w_dkv, m_ckv_norm, m_w_ukv, m_b_norm, m_b_w_in, m_b_q_norm, m_b_w_uq, m_b_w_out, m_final_norm, v_a_norm, v_a_w_in, v_a_conv, v_a_w_out, v_kv_norm, v_w_dkv, v_ckv_norm, v_w_ukv, v_b_norm, v_b_w_in, v_b_q_norm, v_b_w_uq, v_b_w_out, v_final_norm):
    w = dict(a_norm=a_norm, a_w_in=a_w_in, a_conv=a_conv, a_w_out=a_w_out, kv_norm=kv_norm, w_dkv=w_dkv,
             ckv_norm=ckv_norm, w_ukv=w_ukv, b_norm=b_norm, b_w_in=b_w_in, b_q_norm=b_q_norm, b_w_uq=b_w_uq,
             b_w_out=b_w_out, final_norm=final_norm)
    m = dict(a_norm=m_a_norm, a_w_in=m_a_w_in, a_conv=m_a_conv, a_w_out=m_a_w_out, kv_norm=m_kv_norm, w_dkv=m_w_dkv,
             ckv_norm=m_ckv_norm, w_ukv=m_w_ukv, b_norm=m_b_norm, b_w_in=m_b_w_in, b_q_norm=m_b_q_norm,
             b_w_uq=m_b_w_uq, b_w_out=m_b_w_out, final_norm=m_final_norm)
    v = dict(a_norm=v_a_norm, a_w_in=v_a_w_in, a_conv=v_a_conv, a_w_out=v_a_w_out, kv_norm=v_kv_norm, w_dkv=v_w_dkv,
             ckv_norm=v_ckv_norm, w_ukv=v_w_ukv, b_norm=v_b_norm, b_w_in=v_b_w_in, b_q_norm=v_b_q_norm,
             b_w_uq=v_b_w_uq, b_w_out=v_b_w_out, final_norm=v_final_norm)

    w_pack = _pack(w, F32)
    small = jnp.concatenate([a_norm.reshape(-1), a_conv.reshape(-1)]).reshape(8, LANES)
    gath, gsmall = _gather_weights(w_pack.astype(BF16), small)
    full = _full_weights(gath, gsmall)
    for name in REPLICATED:
        full[name] = w[name].reshape(-1)

    loss, dx, grads = _local_step(x, positions, loss_target, full)
    loss = lax.psum(loss, ("x", "y", "c"))

    names = {"a_w_in": "a_w_in4"}
    parts = jnp.stack([_pack({n: _shard_of(names.get(n, n), grads[names.get(n, n)], j) for n in PACK_ORDER}, F32)
                       for j in range(N_CHIPS)])
    got = _scatter_partials(parts.reshape(N_DEV, HALF_ROWS, PACK_W))
    g_pack = _swap_halves(_sum_partials(got)).reshape(PACK_ROWS, PACK_W)

    d_pack, m_pack, v_pack = _adamw(w_pack, g_pack, _pack(m, F32), _pack(v, F32))
    outs = [_unpack(buf) for buf in (g_pack, d_pack, m_pack, v_pack)]
    return (loss, dx) + tuple(o[n] for o in outs for n in WEIGHTS)
```

```python
import functools
import math

import numpy as np
import jax
import jax.numpy as jnp
from jax import lax
from jax.experimental import pallas as pl
from jax.experimental.pallas import tpu as pltpu

F32 = jnp.float32
BF16 = jnp.bfloat16

D_MODEL = 1024
E_A = 1024
CONV_WIDTH = 3
N_HEADS = 8
QK_NOPE = 64
QK_ROPE = 32
V_HEAD = 64
KV_RANK = 256
Q_RANK = 384
E_B = N_HEADS * V_HEAD
ROPE_THETA = 10000.0
SOFTMAX_SCALE = 1.0 / math.sqrt(QK_NOPE + QK_ROPE)
LOG2E = math.log2(math.e)
LN2 = math.log(2.0)
Q_PRESCALE = SOFTMAX_SCALE * LOG2E
EPS = 1e-6
HEAD_PAD = 128
QK_PAD = N_HEADS * HEAD_PAD
ROPE_LO = QK_NOPE
ROPE_HALF = QK_ROPE // 2
KR_PAD = KV_RANK + HEAD_PAD

ADAM_LR = 0.001
ADAM_B1 = 0.9
ADAM_B2 = 0.999
ADAM_EPS = 1e-08
ADAM_WD = 0.01
ADAM_STEP = 10

VMEM_LIMIT = 56 * 1024 * 1024
ROW_TILE = 256
ATT_TILE = 512
LANES = 128
PACK_W = 1024

N_CHIPS = 4
N_DEV = 8


def _dot(a, b):
    return jnp.dot(a, b, preferred_element_type=F32)


def _dot_nt(a, b):
    return lax.dot_general(a, b, (((1,), (1,)), ((), ())), preferred_element_type=F32)


def _dot_tn(a, b):
    return lax.dot_general(a, b, (((0,), (0,)), ((), ())), preferred_element_type=F32)


def _rms(x):
    r = lax.rsqrt(jnp.mean(x * x, axis=-1, keepdims=True) + EPS)
    return r, x * r


def _rms_bwd(dxh, xh, r):
    return r * (dxh - xh * jnp.mean(dxh * xh, axis=-1, keepdims=True))


def _rope_fwd(a, c, s1, s2):
    return a * c + pltpu.roll(a, HEAD_PAD - ROPE_HALF, 1) * s1 + pltpu.roll(a, ROPE_HALF, 1) * s2


def _rope_bwd(g, c, s1, s2):
    return g * c + pltpu.roll(g * s1, ROPE_HALF, 1) + pltpu.roll(g * s2, HEAD_PAD - ROPE_HALF, 1)


def _sigmoid(x):
    return 1.0 / (1.0 + jnp.exp(-x))


def _row_spec(tm, n):
    return pl.BlockSpec((tm, n), lambda i: (i, 0))


def _const_spec(shape):
    nd = len(shape)
    return pl.BlockSpec(shape, lambda i: (0,) * nd, pipeline_mode=pl.Buffered(1))


def _acc_spec(shape):
    nd = len(shape)
    return pl.BlockSpec(shape, lambda i: (0,) * nd, pipeline_mode=pl.Buffered(1))


def _params(*sem):
    return pltpu.CompilerParams(dimension_semantics=sem, vmem_limit_bytes=VMEM_LIMIT)


def _rope_consts():
    lane = np.arange(HEAD_PAD)
    first = (lane >= ROPE_LO) & (lane < ROPE_LO + ROPE_HALF)
    second = (lane >= ROPE_LO + ROPE_HALF) & (lane < ROPE_LO + QK_ROPE)
    f = np.where(first, lane - ROPE_LO, np.where(second, lane - ROPE_LO - ROPE_HALF, 0))
    inv = np.float32(ROPE_THETA) ** (-(2 * f).astype(np.float32) / np.float32(QK_ROPE))
    out = np.zeros((8, HEAD_PAD), np.float32)
    out[0] = inv
    out[1] = first
    out[2] = second
    out[3] = lane < ROPE_LO
    return jnp.asarray(out)


def _rope_tables(pos_col):
    t = pos_col.shape[0]
    tm = min(ROW_TILE, t)

    def body(p_ref, k_ref, c_ref, s1_ref, s2_ref):
        inv, first, second, nope = k_ref[0:1, :], k_ref[1:2, :], k_ref[2:3, :], k_ref[3:4, :]
        ang = p_ref[...].astype(F32) * inv
        cs, sn = jnp.cos(ang), jnp.sin(ang)
        c_ref[...] = cs * (first + second) + nope
        s1_ref[...] = -sn * first
        s2_ref[...] = sn * second

    out = jax.ShapeDtypeStruct((t, HEAD_PAD), F32)
    return pl.pallas_call(
        body, grid=(t // tm,), name="rope_tables",
        in_specs=[_row_spec(tm, 1), _const_spec((8, HEAD_PAD))],
        out_specs=[_row_spec(tm, HEAD_PAD)] * 3, out_shape=[out] * 3,
        compiler_params=_params("parallel"),
    )(pos_col, _rope_consts())


def _shift_down(v, prev, row):
    p1, p2 = prev[7:8, :], prev[6:7, :]
    v1 = jnp.where(row == 0, p1, pltpu.roll(v, 1, 0))
    v2 = jnp.where(row == 0, p2, jnp.where(row == 1, p1, pltpu.roll(v, 2, 0)))
    return v1, v2


def _conv_fwd(x, seq, ga, w_in4, cw):
    t = x.shape[0]
    tm = min(ROW_TILE, seq)
    tiles_per_seq = seq // tm

    def body(x_ref, ga_ref, w_ref, cw_ref, b_ref, c_ref, u_ref, g_ref, ym_ref, carry_ref):
        i = pl.program_id(0)

        @pl.when(i % tiles_per_seq == 0)
        def _():
            carry_ref[...] = jnp.zeros_like(carry_ref)

        _, xh = _rms(x_ref[...])
        h = (xh * ga_ref[...]).astype(BF16)
        b, c, u, g = (_dot(h, w_ref[j]) for j in range(4))
        v = c * u
        row = lax.broadcasted_iota(jnp.int32, (tm, 1), 0)
        v1, v2 = _shift_down(v, carry_ref[...], row)
        carry_ref[...] = v[tm - 8:tm, :]
        cv = cw_ref[2:3, :] * v + cw_ref[1:2, :] * v1 + cw_ref[0:1, :] * v2
        b_ref[...] = b.astype(BF16)
        c_ref[...] = c.astype(BF16)
        u_ref[...] = u.astype(BF16)
        g_ref[...] = g.astype(BF16)
        ym_ref[...] = (g * _sigmoid(g) * b * cv).astype(BF16)

    out = jax.ShapeDtypeStruct((t, E_A), BF16)
    return pl.pallas_call(
        body, grid=(t // tm,), name="conv_fwd",
        in_specs=[_row_spec(tm, D_MODEL), _const_spec((1, D_MODEL)), _const_spec((4, D_MODEL, E_A)),
                  _const_spec((8, E_A))],
        out_specs=[_row_spec(tm, E_A)] * 5, out_shape=[out] * 5,
        scratch_shapes=[pltpu.VMEM((8, E_A), F32)],
        compiler_params=_params("arbitrary"),
    )(x, ga, w_in4, cw)


def _mid_fwd(x, ym, w_out, gk, gb, w_dkv, gc, w_uk, w_uv, w_bin, gq, w_uq, rc, rs1, rs2):
    t = x.shape[0]
    tm = min(ROW_TILE, t)

    def body(x_ref, ym_ref, wo_ref, gk_ref, gb_ref, wd_ref, gc_ref, wk_ref, wv_ref, wi_ref, gq_ref, wq_ref,
             c_ref, s1_ref, s2_ref, x1_ref, q_ref, k_ref, v_ref, gate_ref, cq_ref, ckv_ref):
        cb, s1b, s2b = c_ref[...], s1_ref[...], s2_ref[...]
        x1 = x_ref[...] + _dot(ym_ref[...], wo_ref[...])
        x1_ref[...] = x1
        _, xh = _rms(x1)
        hk = (xh * gk_ref[...]).astype(BF16)
        h1 = (xh * gb_ref[...]).astype(BF16)

        ckr = _dot(hk, wd_ref[...])
        ckv_raw = ckr[:, :KV_RANK]
        ckv_ref[...] = ckv_raw.astype(BF16)
        _, ch = _rms(ckv_raw)
        ckv = (ch * gc_ref[...]).astype(BF16)
        kr = _rope_fwd(ckr[:, KV_RANK:], cb, s1b, s2b)
        kn = _dot(ckv, wk_ref[...])
        for h in range(N_HEADS):
            sl = slice(h * HEAD_PAD, (h + 1) * HEAD_PAD)
            k_ref[:, sl] = (kn[:, sl] + kr).astype(BF16)
        v_ref[...] = _dot(ckv, wv_ref[...]).astype(BF16)

        pb = _dot(h1, wi_ref[...])
        cq = pb[:, :Q_RANK]
        cq_ref[...] = cq.astype(BF16)
        gate_ref[...] = pb[:, Q_RANK:].astype(BF16)
        _, cqh = _rms(cq)
        q = _dot((cqh * gq_ref[...]).astype(BF16), wq_ref[...])
        for h in range(N_HEADS):
            sl = slice(h * HEAD_PAD, (h + 1) * HEAD_PAD)
            q_ref[:, sl] = (_rope_fwd(q[:, sl], cb, s1b, s2b) * Q_PRESCALE).astype(BF16)

    def sds(n, dt):
        return jax.ShapeDtypeStruct((t, n), dt)

    return pl.pallas_call(
        body, grid=(t // tm,), name="mid_fwd",
        in_specs=[_row_spec(tm, D_MODEL), _row_spec(tm, E_A), _const_spec((E_A, D_MODEL)),
                  _const_spec((1, D_MODEL)), _const_spec((1, D_MODEL)), _const_spec((D_MODEL, KR_PAD)),
                  _const_spec((1, KV_RANK)), _const_spec((KV_RANK, QK_PAD)), _const_spec((KV_RANK, E_B)),
                  _const_spec((D_MODEL, Q_RANK + E_B)), _const_spec((1, Q_RANK)), _const_spec((Q_RANK, QK_PAD)),
                  _row_spec(tm, HEAD_PAD), _row_spec(tm, HEAD_PAD), _row_spec(tm, HEAD_PAD)],
        out_specs=[_row_spec(tm, D_MODEL), _row_spec(tm, QK_PAD), _row_spec(tm, QK_PAD), _row_spec(tm, E_B),
                   _row_spec(tm, E_B), _row_spec(tm, Q_RANK), _row_spec(tm, KV_RANK)],
        out_shape=[sds(D_MODEL, F32), sds(QK_PAD, BF16), sds(QK_PAD, BF16), sds(E_B, BF16), sds(E_B, BF16),
                   sds(Q_RANK, BF16), sds(KV_RANK, BF16)],
        compiler_params=_params("parallel"),
    )(x, ym, w_out, gk, gb, w_dkv, gc, w_uk, w_uv, w_bin, gq, w_uq, rc, rs1, rs2)


def _pair_specs(seq):
    qk = pl.BlockSpec((seq, 2 * HEAD_PAD), lambda b, p: (b, p))
    vo = pl.BlockSpec((seq, 2 * V_HEAD), lambda b, p: (b, p))
    st = pl.BlockSpec((None, 2, seq), lambda b, p: (p, 0, b))
    return qk, vo, st


def _attn_fwd(q, k, v, seq):
    t = q.shape[0]
    tq = min(ATT_TILE, seq)
    nq = seq // tq

    def body(q_ref, k_ref, v_ref, o_ref, lse_ref, m_scr, l_scr, acc_scr):
        row = lax.broadcasted_iota(jnp.int32, (tq, tq), 0)
        col = lax.broadcasted_iota(jnp.int32, (tq, tq), 1)
        lane = lax.broadcasted_iota(jnp.int32, (tq, 2 * V_HEAD), 1)

        def q_step(qi, _):
            q0 = pl.multiple_of(qi * tq, tq)
            m_scr[...] = jnp.full(m_scr.shape, -jnp.inf, F32)
            l_scr[...] = jnp.zeros_like(l_scr)
            acc_scr[...] = jnp.zeros_like(acc_scr)

            def k_step(ki, _, masked):
                k0 = pl.multiple_of(ki * tq, tq)
                vt = v_ref[pl.ds(k0, tq), :]
                for hh in range(2):
                    hs = slice(hh * HEAD_PAD, (hh + 1) * HEAD_PAD)
                    s = _dot_nt(q_ref[pl.ds(q0, tq), hs], k_ref[pl.ds(k0, tq), hs])
                    if masked:
                        s = jnp.where(col <= row, s, -jnp.inf)
                    m_old = m_scr[hh]
                    m_new = jnp.maximum(m_old, jnp.max(s, axis=-1, keepdims=True))
                    alpha = jnp.exp2(m_old - m_new)
                    ps = [jnp.exp2(s[:, j * LANES:(j + 1) * LANES] - m_new) for j in range(tq // LANES)]
                    l_scr[hh] = alpha * l_scr[hh] + functools.reduce(lambda a, b: a + b, ps)
                    p = jnp.concatenate(ps, axis=-1).astype(BF16)
                    acc_scr[hh] = alpha * acc_scr[hh] + _dot(p, vt)
                    m_scr[hh] = m_new
                return 0

            lax.fori_loop(0, qi, functools.partial(k_step, masked=False), 0)
            k_step(qi, 0, True)
            l0 = jnp.sum(l_scr[0], axis=-1, keepdims=True)
            l1 = jnp.sum(l_scr[1], axis=-1, keepdims=True)
            o_ref[pl.ds(q0, tq), :] = jnp.where(lane < V_HEAD, acc_scr[0] / l0, acc_scr[1] / l1).astype(BF16)
            stats = jnp.where(lane == 0, m_scr[0] + jnp.log2(l0), m_scr[1] + jnp.log2(l1)).T
            lse_ref[:, pl.ds(q0, tq)] = stats[0:2, :]
            return 0

        lax.fori_loop(0, nq, q_step, 0)

    qk, vo, st = _pair_specs(seq)
    return pl.pallas_call(
        body, grid=(t // seq, N_HEADS // 2), name="attn_fwd",
        in_specs=[qk, qk, vo], out_specs=[vo, st],
        out_shape=[jax.ShapeDtypeStruct((t, E_B), BF16), jax.ShapeDtypeStruct((N_HEADS // 2, 2, t), F32)],
        scratch_shapes=[pltpu.VMEM((2, tq, LANES), F32), pltpu.VMEM((2, tq, LANES), F32),
                        pltpu.VMEM((2, tq, 2 * V_HEAD), F32)],
        compiler_params=_params("parallel", "parallel"),
    )(q, k, v)


def _head_fwd_bwd(o, gate, x1, tgt, w_bout, gf):
    t = o.shape[0]
    tm = min(ROW_TILE, t)

    def body(o_ref, gate_ref, x1_ref, tgt_ref, w_ref, gf_ref,
             dx2_ref, do_ref, dgate_ref, dd_ref, loss_ref, dgf_ref, dw_ref):
        @pl.when(pl.program_id(0) == 0)
        def _():
            loss_ref[...] = jnp.zeros_like(loss_ref)
            dgf_ref[...] = jnp.zeros_like(dgf_ref)
            dw_ref[...] = jnp.zeros_like(dw_ref)

        o = o_ref[...].astype(F32)
        gt = gate_ref[...].astype(F32)
        sg = _sigmoid(gt)
        silu = gt * sg
        z = (o * silu).astype(BF16)
        x2 = x1_ref[...] + _dot(z, w_ref[...])
        r2, xh2 = _rms(x2)
        gf = gf_ref[...]
        err = xh2 * gf - tgt_ref[...]
        loss_ref[...] += 0.5 * jnp.sum(jnp.mean(err * err, axis=-1, keepdims=True))
        dy = err * (1.0 / D_MODEL)
        dgf_ref[...] += jnp.sum(dy * xh2, axis=0, keepdims=True)
        dx2 = _rms_bwd(dy * gf, xh2, r2)
        dx2_ref[...] = dx2
        dx2b = dx2.astype(BF16)
        dw_ref[...] += _dot_tn(z, dx2b)
        dz = _dot_nt(dx2b, w_ref[...])
        do = dz * silu
        do_ref[...] = do.astype(BF16)
        dgate_ref[...] = (dz * o * (sg * (1.0 + gt * (1.0 - sg)))).astype(BF16)
        prod = do * o
        lane = lax.broadcasted_iota(jnp.int32, (tm, 2 * V_HEAD), 1)
        cols = jnp.zeros((tm, LANES), F32)
        for p in range(N_HEADS // 2):
            blk = prod[:, p * 2 * V_HEAD:(p + 1) * 2 * V_HEAD]
            d0 = jnp.sum(jnp.where(lane < V_HEAD, blk, 0.0), axis=-1, keepdims=True)
            d1 = jnp.sum(jnp.where(lane < V_HEAD, 0.0, blk), axis=-1, keepdims=True)
            cols = jnp.where(lane == 2 * p, d0, jnp.where(lane == 2 * p + 1, d1, cols))
        rows = cols.T
        for h in range(N_HEADS):
            dd_ref[h // 2, h % 2:h % 2 + 1, :] = rows[h:h + 1, :]

    return pl.pallas_call(
        body, grid=(t // tm,), name="head_fwd_bwd",
        in_specs=[_row_spec(tm, E_B), _row_spec(tm, E_B), _row_spec(tm, D_MODEL), _row_spec(tm, D_MODEL),
                  _const_spec((E_B, D_MODEL)), _const_spec((1, D_MODEL))],
        out_specs=[_row_spec(tm, D_MODEL), _row_spec(tm, E_B), _row_spec(tm, E_B),
                   pl.BlockSpec((N_HEADS // 2, 2, tm), lambda i: (0, 0, i)),
                   _acc_spec((1, 1)), _acc_spec((1, D_MODEL)), _acc_spec((E_B, D_MODEL))],
        out_shape=[jax.ShapeDtypeStruct((t, D_MODEL), F32), jax.ShapeDtypeStruct((t, E_B), BF16),
                   jax.ShapeDtypeStruct((t, E_B), BF16), jax.ShapeDtypeStruct((N_HEADS // 2, 2, t), F32),
                   jax.ShapeDtypeStruct((1, 1), F32), jax.ShapeDtypeStruct((1, D_MODEL), F32),
                   jax.ShapeDtypeStruct((E_B, D_MODEL), F32)],
        compiler_params=_params("arbitrary"),
    )(o, gate, x1, tgt, w_bout, gf)


def _attn_bwd(q, k, v, do, lse, dd, seq):
    t = q.shape[0]
    tq = min(ATT_TILE, seq)
    nq = seq // tq

    def body(q_ref, k_ref, v_ref, do_ref, lse_ref, dd_ref, dq_ref, dk_ref, dv_ref, dq_acc, dk_acc, dv_acc):
        krow = lax.broadcasted_iota(jnp.int32, (tq, tq), 0)
        qcol = lax.broadcasted_iota(jnp.int32, (tq, tq), 1)
        lane = lax.broadcasted_iota(jnp.int32, (tq, 2 * V_HEAD), 1)
        dq_acc[...] = jnp.zeros_like(dq_acc)

        def k_step(ki, _):
            k0 = pl.multiple_of(ki * tq, tq)
            vt = v_ref[pl.ds(k0, tq), :]
            dk_acc[...] = jnp.zeros_like(dk_acc)
            dv_acc[...] = jnp.zeros_like(dv_acc)

            def q_step(qi, _, masked):
                q0 = pl.multiple_of(qi * tq, tq)
                do_pair = do_ref[pl.ds(q0, tq), :]
                for hh in range(2):
                    hs = slice(hh * HEAD_PAD, (hh + 1) * HEAD_PAD)
                    kt = k_ref[pl.ds(k0, tq), hs]
                    qt = q_ref[pl.ds(q0, tq), hs]
                    mine = (lane < V_HEAD) if hh == 0 else (lane >= V_HEAD)
                    do_h = jnp.where(mine, do_pair, jnp.zeros((), BF16))
                    st = _dot_nt(kt, qt)
                    if masked:
                        st = jnp.where(krow <= qcol, st, -jnp.inf)
                    pt = jnp.exp2(st - lse_ref[hh:hh + 1, pl.ds(q0, tq)])
                    dpt = _dot_nt(vt, do_h)
                    dst = (pt * (dpt - dd_ref[hh:hh + 1, pl.ds(q0, tq)])).astype(BF16)
                    dv_acc[...] += _dot(pt.astype(BF16), do_h)
                    dk_acc[:, hs] += _dot(dst, qt)
                    dq_acc[pl.ds(q0, tq), hs] += _dot_tn(dst, kt)
                return 0

            q_step(ki, 0, True)
            lax.fori_loop(ki + 1, nq, functools.partial(q_step, masked=False), 0)
            dk_ref[pl.ds(k0, tq), :] = (dk_acc[...] * LN2).astype(BF16)
            dv_ref[pl.ds(k0, tq), :] = dv_acc[...].astype(BF16)
            return 0

        lax.fori_loop(0, nq, k_step, 0)
        dq_ref[...] = (dq_acc[...] * SOFTMAX_SCALE).astype(BF16)

    qk, vo, st = _pair_specs(seq)
    return pl.pallas_call(
        body, grid=(t // seq, N_HEADS // 2), name="attn_bwd",
        in_specs=[qk, qk, vo, vo, st, st], out_specs=[qk, qk, vo],
        out_shape=[jax.ShapeDtypeStruct((t, QK_PAD), BF16), jax.ShapeDtypeStruct((t, QK_PAD), BF16),
                   jax.ShapeDtypeStruct((t, E_B), BF16)],
        scratch_shapes=[pltpu.VMEM((seq, 2 * HEAD_PAD), F32), pltpu.VMEM((tq, 2 * HEAD_PAD), F32),
                        pltpu.VMEM((tq, 2 * V_HEAD), F32)],
        compiler_params=_params("parallel", "parallel"),
    )(q, k, v, do, lse, dd)


def _mid_bwd(dq, dk, dv, dgate, dx2, x1, cq, ckv, rc, rs1, rs2, w_uq, w_bin, w_uk, w_uv, w_dkv, gq, gc, gk, gb):
    t = dq.shape[0]
    tm = min(ROW_TILE, t)

    def body(dq_ref, dk_ref, dv_ref, dgate_ref, dx2_ref, x1_ref, cq_ref, ckv_ref, c_ref, s1_ref, s2_ref,
             wq_ref, wi_ref, wk_ref, wv_ref, wd_ref, gq_ref, gc_ref, gk_ref, gb_ref,
             dx1_ref, dwq_ref, dwi_ref, dwk_ref, dwv_ref, dwd_ref, dgq_ref, dgc_ref, dgk_ref, dgb_ref):
        @pl.when(pl.program_id(0) == 0)
        def _():
            for ref in (dwq_ref, dwi_ref, dwk_ref, dwv_ref, dwd_ref, dgq_ref, dgc_ref, dgk_ref, dgb_ref):
                ref[...] = jnp.zeros_like(ref)

        cb, s1b, s2b = c_ref[...], s1_ref[...], s2_ref[...]
        r1, xh = _rms(x1_ref[...])
        gk, gb, gq, gc = gk_ref[...], gb_ref[...], gq_ref[...], gc_ref[...]
        hk = (xh * gk).astype(BF16)
        h1 = (xh * gb).astype(BF16)

        dqs = [_rope_bwd(dq_ref[:, h * HEAD_PAD:(h + 1) * HEAD_PAD].astype(F32), cb, s1b, s2b)
               for h in range(N_HEADS)]
        dqb = jnp.concatenate(dqs, axis=-1).astype(BF16)
        rq, cqh = _rms(cq_ref[...].astype(F32))
        dwq_ref[...] += _dot_tn((cqh * gq).astype(BF16), dqb)
        dcqn = _dot_nt(dqb, wq_ref[...])
        dgq_ref[...] += jnp.sum(dcqn * cqh, axis=0, keepdims=True)
        dcq = _rms_bwd(dcqn * gq, cqh, rq)
        dpb = jnp.concatenate([dcq.astype(BF16), dgate_ref[...]], axis=-1)
        dwi_ref[...] += _dot_tn(h1, dpb)
        dh1 = _dot_nt(dpb, wi_ref[...])

        rcv, ch = _rms(ckv_ref[...].astype(F32))
        ckvn = (ch * gc).astype(BF16)
        dkb, dvb = dk_ref[...], dv_ref[...]
        dwk_ref[...] += _dot_tn(ckvn, dkb)
        dwv_ref[...] += _dot_tn(ckvn, dvb)
        dckv = _dot_nt(dkb, wk_ref[...]) + _dot_nt(dvb, wv_ref[...])
        dgc_ref[...] += jnp.sum(dckv * ch, axis=0, keepdims=True)
        dckv_raw = _rms_bwd(dckv * gc, ch, rcv)
        dkr = dk_ref[:, 0:HEAD_PAD].astype(F32)
        for h in range(1, N_HEADS):
            dkr = dkr + dk_ref[:, h * HEAD_PAD:(h + 1) * HEAD_PAD].astype(F32)
        dkr = _rope_bwd(dkr, cb, s1b, s2b)
        dckr = jnp.concatenate([dckv_raw, dkr], axis=-1).astype(BF16)
        dwd_ref[...] += _dot_tn(hk, dckr)
        dhk = _dot_nt(dckr, wd_ref[...])

        dgb_ref[...] += jnp.sum(dh1 * xh, axis=0, keepdims=True)
        dgk_ref[...] += jnp.sum(dhk * xh, axis=0, keepdims=True)
        dx1_ref[...] = dx2_ref[...] + _rms_bwd(dh1 * gb + dhk * gk, xh, r1)

    acc_shapes = [(Q_RANK, QK_PAD), (D_MODEL, Q_RANK + E_B), (KV_RANK, QK_PAD), (KV_RANK, E_B), (D_MODEL, KR_PAD),
                  (1, Q_RANK), (1, KV_RANK), (1, D_MODEL), (1, D_MODEL)]
    return pl.pallas_call(
        body, grid=(t // tm,), name="mid_bwd",
        in_specs=[_row_spec(tm, QK_PAD), _row_spec(tm, QK_PAD), _row_spec(tm, E_B), _row_spec(tm, E_B),
                  _row_spec(tm, D_MODEL), _row_spec(tm, D_MODEL), _row_spec(tm, Q_RANK), _row_spec(tm, KV_RANK),
                  _row_spec(tm, HEAD_PAD), _row_spec(tm, HEAD_PAD), _row_spec(tm, HEAD_PAD),
                  _const_spec((Q_RANK, QK_PAD)), _const_spec((D_MODEL, Q_RANK + E_B)),
                  _const_spec((KV_RANK, QK_PAD)), _const_spec((KV_RANK, E_B)), _const_spec((D_MODEL, KR_PAD)),
                  _const_spec((1, Q_RANK)), _const_spec((1, KV_RANK)), _const_spec((1, D_MODEL)),
                  _const_spec((1, D_MODEL))],
        out_specs=[_row_spec(tm, D_MODEL)] + [_acc_spec(s) for s in acc_shapes],
        out_shape=[jax.ShapeDtypeStruct((t, D_MODEL), F32)] + [jax.ShapeDtypeStruct(s, F32) for s in acc_shapes],
        compiler_params=_params("arbitrary"),
    )(dq, dk, dv, dgate, dx2, x1, cq, ckv, rc, rs1, rs2, w_uq, w_bin, w_uk, w_uv, w_dkv, gq, gc, gk, gb)


def _conv_bwd(dx1, x, b, c, u, g, seq, w_out, w_in4, ga, cw):
    t = x.shape[0]
    tm = min(ROW_TILE, seq)
    tiles_per_seq = seq // tm
    n = t // tm
    halo = tm // 8

    def tile(i):
        return n - 1 - i

    def rev(width):
        return pl.BlockSpec((tm, width), lambda i: (tile(i), 0))

    def prev8(width):
        return pl.BlockSpec((8, width), lambda i: (jnp.maximum(tile(i) * halo - 1, 0), 0))

    def body(dx1_ref, x_ref, b_ref, c_ref, u_ref, g_ref, cp_ref, up_ref, wo_ref, wi_ref, ga_ref, cw_ref,
             dx_ref, dwi_ref, dwo_ref, dcw_ref, dga_ref, carry_ref):
        i = pl.program_id(0)
        j = tile(i)

        @pl.when(i == 0)
        def _():
            for ref in (dwi_ref, dwo_ref, dcw_ref, dga_ref):
                ref[...] = jnp.zeros_like(ref)

        @pl.when(j % tiles_per_seq == tiles_per_seq - 1)
        def _():
            carry_ref[...] = jnp.zeros_like(carry_ref)

        dx1 = dx1_ref[...]
        dx1b = dx1.astype(BF16)
        b, c, u, g = (r[...].astype(F32) for r in (b_ref, c_ref, u_ref, g_ref))
        v = c * u
        first = (j % tiles_per_seq == 0).astype(F32)
        vprev = cp_ref[...].astype(F32) * up_ref[...].astype(F32) * (1.0 - first)
        row = lax.broadcasted_iota(jnp.int32, (tm, 1), 0)
        v1, v2 = _shift_down(v, vprev, row)
        w0, w1, w2 = cw_ref[0:1, :], cw_ref[1:2, :], cw_ref[2:3, :]
        cv = w2 * v + w1 * v1 + w0 * v2
        sg = _sigmoid(g)
        silu = g * sg
        ym = (silu * b * cv).astype(BF16)
        dwo_ref[...] += _dot_tn(ym, dx1b)
        dym = _dot_nt(dx1b, wo_ref[...])
        db = dym * silu * cv
        dcv = dym * silu * b
        dg = dym * b * cv * (sg * (1.0 + g * (1.0 - sg)))

        nxt = carry_ref[...]
        n0, n1 = nxt[0:1, :], nxt[1:2, :]
        d1 = jnp.where(row == tm - 1, n0, pltpu.roll(dcv, tm - 1, 0))
        d2 = jnp.where(row == tm - 1, n1, jnp.where(row == tm - 2, n0, pltpu.roll(dcv, tm - 2, 0)))
        carry_ref[...] = dcv[0:8, :]
        dv = w2 * dcv + w1 * d1 + w0 * d2
        dcw_ref[0:1, :] += jnp.sum(dcv * v2, axis=0, keepdims=True)
        dcw_ref[1:2, :] += jnp.sum(dcv * v1, axis=0, keepdims=True)
        dcw_ref[2:3, :] += jnp.sum(dcv * v, axis=0, keepdims=True)

        r0, xh = _rms(x_ref[...])
        ga = ga_ref[...]
        h = (xh * ga).astype(BF16)
        dh = jnp.zeros((tm, D_MODEL), F32)
        for idx, dpart in enumerate((db, dv * u, dv * c, dg)):
            dpb = dpart.astype(BF16)
            dwi_ref[idx] += _dot_tn(h, dpb)
            dh = dh + _dot_nt(dpb, wi_ref[idx])
        dga_ref[...] += jnp.sum(dh * xh, axis=0, keepdims=True)
        dx_ref[...] = dx1 + _rms_bwd(dh * ga, xh, r0)

    acc_shapes = [(4, D_MODEL, E_A), (E_A, D_MODEL), (8, E_A), (1, D_MODEL)]
    return pl.pallas_call(
        body, grid=(n,), name="conv_bwd",
        in_specs=[rev(D_MODEL), rev(D_MODEL), rev(E_A), rev(E_A), rev(E_A), rev(E_A), prev8(E_A), prev8(E_A),
                  _const_spec((E_A, D_MODEL)), _const_spec((4, D_MODEL, E_A)), _const_spec((1, D_MODEL)),
                  _const_spec((8, E_A))],
        out_specs=[rev(D_MODEL)] + [_acc_spec(s) for s in acc_shapes],
        out_shape=[jax.ShapeDtypeStruct((t, D_MODEL), F32)] + [jax.ShapeDtypeStruct(s, F32) for s in acc_shapes],
        scratch_shapes=[pltpu.VMEM((8, E_A), F32)],
        compiler_params=_params("arbitrary"),
    )(dx1, x, b, c, u, g, c, u, w_out, w_in4, ga, cw)


def _pad_heads(w, width):
    kdim = w.shape[0]
    w = w.reshape(kdim, N_HEADS, width)
    return jnp.pad(w, ((0, 0), (0, 0), (0, HEAD_PAD - width))).reshape(kdim, QK_PAD)


def _unpad_heads(w, width):
    kdim = w.shape[0]
    return w.reshape(kdim, N_HEADS, HEAD_PAD)[:, :, :width].reshape(kdim, N_HEADS * width)


def _local_step(x, positions, tgt, wts):
    bsz, seq, _ = x.shape
    t = bsz * seq
    x2d = x.reshape(t, D_MODEL)
    row = lambda a: a.reshape(1, -1).astype(F32)

    w_in4 = wts["a_w_in4"]
    w_out = wts["a_w_out"]
    cw = jnp.pad(wts["a_conv"].astype(F32), ((0, 8 - CONV_WIDTH), (0, 0)))
    w_dkv = wts["w_dkv"]
    w_dkv_p = jnp.concatenate(
        [w_dkv[:, :KV_RANK], jnp.zeros((D_MODEL, ROPE_LO), BF16), w_dkv[:, KV_RANK:],
         jnp.zeros((D_MODEL, HEAD_PAD - ROPE_LO - QK_ROPE), BF16)], axis=1)
    w_ukv = wts["w_ukv"].reshape(KV_RANK, N_HEADS, 2, QK_NOPE)
    w_uk = _pad_heads(w_ukv[:, :, 0, :].reshape(KV_RANK, N_HEADS * QK_NOPE), QK_NOPE)
    w_uv = w_ukv[:, :, 1, :].reshape(KV_RANK, E_B)
    w_bin = wts["b_w_in"]
    w_uq = _pad_heads(wts["b_w_uq"], QK_NOPE + QK_ROPE)
    w_bout = wts["b_w_out"]
    ga, gk, gc, gb, gq, gf = (row(wts[n]) for n in ("a_norm", "kv_norm", "ckv_norm", "b_norm", "b_q_norm",
                                                   "final_norm"))

    rc, rs1, rs2 = _rope_tables(positions.reshape(t, 1))
    b, c, u, g, ym = _conv_fwd(x2d, seq, ga, w_in4, cw)
    x1, q, k, v, gate, cq, ckv = _mid_fwd(x2d, ym, w_out, gk, gb, w_dkv_p, gc, w_uk, w_uv, w_bin, gq, w_uq,
                                          rc, rs1, rs2)
    o, lse = _attn_fwd(q, k, v, seq)
    dx2, do, dgate, dd, loss, dgf, dw_bout = _head_fwd_bwd(o, gate, x1, tgt.reshape(t, D_MODEL), w_bout, gf)
    dq, dk, dv = _attn_bwd(q, k, v, do, lse, dd, seq)
    dx1, dwq, dw_bin, dwk, dwv, dwd, dgq, dgc, dgk, dgb = _mid_bwd(
        dq, dk, dv, dgate, dx2, x1, cq, ckv, rc, rs1, rs2, w_uq, w_bin, w_uk, w_uv, w_dkv_p, gq, gc, gk, gb)
    dx, dw_in4, dw_out, dcw, dga = _conv_bwd(dx1, x2d, b, c, u, g, seq, w_out, w_in4, ga, cw)

    dw_ukv = jnp.stack([_unpad_heads(dwk, QK_NOPE).reshape(KV_RANK, N_HEADS, QK_NOPE),
                        dwv.reshape(KV_RANK, N_HEADS, V_HEAD)], axis=2).reshape(KV_RANK, 2 * E_B)
    grads = {
        "a_norm": dga[0], "a_w_in4": dw_in4, "a_conv": dcw[:CONV_WIDTH], "a_w_out": dw_out,
        "kv_norm": dgk[0],
        "w_dkv": jnp.concatenate([dwd[:, :KV_RANK], dwd[:, KV_RANK + ROPE_LO:KV_RANK + ROPE_LO + QK_ROPE]], axis=1),
        "ckv_norm": dgc[0], "w_ukv": dw_ukv, "b_norm": dgb[0], "b_w_in": dw_bin, "b_q_norm": dgq[0],
        "b_w_uq": _unpad_heads(dwq, QK_NOPE + QK_ROPE), "b_w_out": dw_bout, "final_norm": dgf[0],
    }
    return loss[0, 0], dx.reshape(bsz, seq, D_MODEL), grads


MATS = ("a_w_in", "a_w_out", "w_dkv", "w_ukv", "b_w_in", "b_w_uq", "b_w_out")
SHARDED_SMALL = ("a_norm", "a_conv")
REPLICATED = ("kv_norm", "ckv_norm", "b_norm", "b_q_norm", "final_norm")
WEIGHTS = ("a_norm", "a_w_in", "a_conv", "a_w_out", "kv_norm", "w_dkv", "ckv_norm", "w_ukv", "b_norm", "b_w_in",
           "b_q_norm", "b_w_uq", "b_w_out", "final_norm")
SHARD_SHAPES = {
    "a_norm": (1, 256), "a_w_in": (1, 1024, 1024), "a_conv": (1, 3, 256), "a_w_out": (1, 256, 1024),
    "kv_norm": (1024,), "w_dkv": (256, 288), "ckv_norm": (256,), "w_ukv": (256, 256), "b_norm": (1, 1024),
    "b_w_in": (1, 256, 896), "b_q_norm": (1, 384), "b_w_uq": (1, 384, 192), "b_w_out": (1, 512, 256),
    "final_norm": (1024,),
}
PACK_ORDER = MATS + SHARDED_SMALL + REPLICATED


def _layout():
    off, table = 0, {}
    for name in PACK_ORDER:
        size = int(np.prod(SHARD_SHAPES[name]))
        table[name] = (off, size)
        off += size
        if name == MATS[-1] or name == SHARDED_SMALL[-1]:
            off = -(-off // PACK_W) * PACK_W
    rows = -(-off // PACK_W)
    return table, -(-rows // 32) * 32


PACK_TABLE, PACK_ROWS = _layout()
HALF_ROWS = PACK_ROWS // 2


def _pack(parts, dtype):
    flat = []
    off = 0
    for name in PACK_ORDER:
        start, size = PACK_TABLE[name]
        if start > off:
            flat.append(jnp.zeros((start - off,), dtype))
        flat.append(parts[name].reshape(-1).astype(dtype))
        off = start + size
    flat.append(jnp.zeros((PACK_ROWS * PACK_W - off,), dtype))
    return jnp.concatenate(flat).reshape(PACK_ROWS, PACK_W)


def _unpack(buf, names=PACK_ORDER):
    flat = buf.reshape(-1)
    return {n: flat[PACK_TABLE[n][0]:PACK_TABLE[n][0] + PACK_TABLE[n][1]].reshape(SHARD_SHAPES[n]) for n in names}


def _shard_of(name, full, j):
    if name == "a_w_in4":
        return full[j]
    if name in ("a_w_out", "w_dkv", "b_w_in"):
        rows = full.shape[0] // N_CHIPS
        return full[j * rows:(j + 1) * rows]
    if name in REPLICATED:
        return full
    cols = full.shape[-1] // N_CHIPS
    return full[..., j * cols:(j + 1) * cols]


def _full_weights(gath, small):
    def mat(name):
        start, size = PACK_TABLE[name]
        r0, r1 = start // PACK_W, (start + size) // PACK_W
        return gath[:, r0:r1, :].reshape((N_CHIPS,) + SHARD_SHAPES[name])

    def cols(w):
        return jnp.transpose(w, (1, 0, 2)).reshape(w.shape[1], -1)

    sm = small.reshape(N_CHIPS, -1)
    return {
        "a_w_in4": mat("a_w_in")[:, 0],
        "a_w_out": mat("a_w_out")[:, 0].reshape(E_A, D_MODEL),
        "w_dkv": mat("w_dkv").reshape(D_MODEL, KV_RANK + QK_ROPE),
        "w_ukv": cols(mat("w_ukv")),
        "b_w_in": mat("b_w_in")[:, 0].reshape(D_MODEL, Q_RANK + E_B),
        "b_w_uq": cols(mat("b_w_uq")[:, 0]),
        "b_w_out": cols(mat("b_w_out")[:, 0]),
        "a_norm": sm[:, :256].reshape(-1),
        "a_conv": jnp.transpose(sm[:, 256:1024].reshape(N_CHIPS, CONV_WIDTH, 256), (1, 0, 2)).reshape(CONV_WIDTH, -1),
    }


MESH = pl.DeviceIdType.MESH
HBM = pl.BlockSpec(memory_space=pl.ANY)


def _place():
    return lax.axis_index("x"), lax.axis_index("y"), lax.axis_index("c")


def _gather_weights(shard, small):
    flips = ((1, 0), (0, 1), (1, 1))

    def body(sh_ref, sm_ref, out_ref, osm_ref, send, recv, send_s, recv_s, local):
        x, y, c = _place()
        mine = 2 * x + y
        own = [pltpu.make_async_copy(sh_ref, out_ref.at[mine], local.at[0]),
               pltpu.make_async_copy(sm_ref, osm_ref.at[mine], local.at[1])]
        for cp in own:
            cp.start()

        def copies(i, fx, fy):
            px, py = x ^ fx, y ^ fy
            peer = 2 * px + py
            big = pltpu.make_async_remote_copy(src_ref=sh_ref, dst_ref=out_ref.at[mine], send_sem=send.at[i],
                                               recv_sem=recv.at[i], device_id=(px, py, c), device_id_type=MESH)
            sml = pltpu.make_async_remote_copy(src_ref=sm_ref, dst_ref=osm_ref.at[mine], send_sem=send_s.at[i],
                                               recv_sem=recv_s.at[i], device_id=(px, py, c), device_id_type=MESH)
            got = pltpu.make_async_remote_copy(src_ref=sh_ref, dst_ref=out_ref.at[peer], send_sem=send.at[i],
                                               recv_sem=recv.at[i], device_id=(px, py, c), device_id_type=MESH)
            got_s = pltpu.make_async_remote_copy(src_ref=sm_ref, dst_ref=osm_ref.at[peer], send_sem=send_s.at[i],
                                                 recv_sem=recv_s.at[i], device_id=(px, py, c), device_id_type=MESH)
            return big, sml, got, got_s

        all_copies = [copies(i, fx, fy) for i, (fx, fy) in enumerate(flips)]
        for big, sml, _, _ in all_copies:
            big.start()
            sml.start()
        for big, sml, got, got_s in all_copies:
            got.wait_recv()
            got_s.wait_recv()
            big.wait_send()
            sml.wait_send()
        for cp in own:
            cp.wait()

    return pl.pallas_call(
        body, name="gather_weights",
        in_specs=[HBM, HBM], out_specs=[HBM, HBM],
        out_shape=[jax.ShapeDtypeStruct((N_CHIPS,) + shard.shape, shard.dtype),
                   jax.ShapeDtypeStruct((N_CHIPS,) + small.shape, small.dtype)],
        scratch_shapes=[pltpu.SemaphoreType.DMA((3,)), pltpu.SemaphoreType.DMA((3,)),
                        pltpu.SemaphoreType.DMA((3,)), pltpu.SemaphoreType.DMA((3,)),
                        pltpu.SemaphoreType.DMA((2,))],
    )(shard, small)


def _scatter_partials(parts):
    def body(p_ref, out_ref, send, recv, local):
        x, y, c = _place()
        me = 4 * x + 2 * y + c
        own = pltpu.make_async_copy(p_ref.at[me], out_ref.at[me], local)
        own.start()
        sends, recvs = [], []
        for d in range(1, N_DEV):
            px, py, pc = x ^ (d >> 2), y ^ ((d >> 1) & 1), c ^ (d & 1)
            peer = 4 * px + 2 * py + pc
            sends.append(pltpu.make_async_remote_copy(
                src_ref=p_ref.at[peer], dst_ref=out_ref.at[me], send_sem=send.at[d - 1], recv_sem=recv.at[d - 1],
                device_id=(px, py, pc), device_id_type=MESH))
            recvs.append(pltpu.make_async_remote_copy(
                src_ref=p_ref.at[peer], dst_ref=out_ref.at[peer], send_sem=send.at[d - 1], recv_sem=recv.at[d - 1],
                device_id=(px, py, pc), device_id_type=MESH))
        for cp in sends:
            cp.start()
        for cp in recvs:
            cp.wait_recv()
        for cp in sends:
            cp.wait_send()
        own.wait()

    return pl.pallas_call(
        body, name="scatter_partials",
        in_specs=[HBM], out_specs=HBM, out_shape=jax.ShapeDtypeStruct(parts.shape, parts.dtype),
        scratch_shapes=[pltpu.SemaphoreType.DMA((N_DEV - 1,)), pltpu.SemaphoreType.DMA((N_DEV - 1,)),
                        pltpu.SemaphoreType.DMA],
    )(parts)


def _sum_partials(parts):
    _, rows, width = parts.shape
    tm = rows // 4

    def body(p_ref, o_ref):
        acc = p_ref[0]
        for kk in range(1, N_DEV):
            acc = acc + p_ref[kk]
        o_ref[...] = acc

    return pl.pallas_call(
        body, grid=(rows // tm,), name="sum_partials",
        in_specs=[pl.BlockSpec((N_DEV, tm, width), lambda i: (0, i, 0))],
        out_specs=pl.BlockSpec((tm, width), lambda i: (i, 0)),
        out_shape=jax.ShapeDtypeStruct((rows, width), parts.dtype),
        compiler_params=_params("parallel"),
    )(parts)


def _swap_halves(half):
    def body(h_ref, out_ref, send, recv, local):
        x, y, c = _place()
        own = pltpu.make_async_copy(h_ref, out_ref.at[c], local)
        own.start()
        give = pltpu.make_async_remote_copy(src_ref=h_ref, dst_ref=out_ref.at[c], send_sem=send, recv_sem=recv,
                                            device_id=(x, y, 1 - c), device_id_type=MESH)
        take = pltpu.make_async_remote_copy(src_ref=h_ref, dst_ref=out_ref.at[1 - c], send_sem=send, recv_sem=recv,
                                            device_id=(x, y, 1 - c), device_id_type=MESH)
        give.start()
        take.wait_recv()
        give.wait_send()
        own.wait()

    return pl.pallas_call(
        body, name="swap_halves",
        in_specs=[HBM], out_specs=HBM, out_shape=jax.ShapeDtypeStruct((2,) + half.shape, half.dtype),
        scratch_shapes=[pltpu.SemaphoreType.DMA, pltpu.SemaphoreType.DMA, pltpu.SemaphoreType.DMA],
    )(half)


def _adamw(w, g, m, v):
    rows, width = w.shape
    tm = rows // 8

    def body(w_ref, g_ref, m_ref, v_ref, d_ref, mo_ref, vo_ref):
        g = g_ref[...]
        m = ADAM_B1 * m_ref[...] + (1.0 - ADAM_B1) * g
        v = ADAM_B2 * v_ref[...] + (1.0 - ADAM_B2) * (g * g)
        m_hat = m / (1.0 - ADAM_B1 ** ADAM_STEP)
        v_hat = v / (1.0 - ADAM_B2 ** ADAM_STEP)
        d_ref[...] = -ADAM_LR * (m_hat / (jnp.sqrt(v_hat) + ADAM_EPS) + ADAM_WD * w_ref[...])
        mo_ref[...] = m
        vo_ref[...] = v

    spec = pl.BlockSpec((tm, width), lambda i: (i, 0))
    out = jax.ShapeDtypeStruct((rows, width), F32)
    return pl.pallas_call(
        body, grid=(rows // tm,), name="adamw",
        in_specs=[spec] * 4, out_specs=[spec] * 3, out_shape=[out] * 3,
        compiler_params=_params("parallel"),
    )(w, g, m, v)


def kernel(x, positions, a_norm, a_w_in, a_conv, a_w_out, kv_norm, w_dkv, ckv_norm, w_ukv, b_norm, b_w_in, b_q_norm, b_w_uq, b_w_out, final_norm, loss_target, m_a_norm, m_a_w_in, m_a_conv, m_a_w_out, m_kv_norm, m_w_dkv, m_ckv_norm, m_w_ukv, m_b_norm, m_b_w_in, m_b_q_norm, m_b_w_uq, m_b_w_out, m_final_norm, v_a_norm, v_a_w_in, v_a_conv, v_a_w_out, v_kv_norm, v_w_dkv, v_ckv_norm, v_w_ukv, v_b_norm, v_b_w_in, v_b_q_norm, v_b_w_uq, v_b_w_out, v_final_norm):
    w = dict(a_norm=a_norm, a_w_in=a_w_in, a_conv=a_conv, a_w_out=a_w_out, kv_norm=kv_norm, w_dkv=w_dkv,
             ckv_norm=ckv_norm, w_ukv=w_ukv, b_norm=b_norm, b_w_in=b_w_in, b_q_norm=b_q_norm, b_w_uq=b_w_uq,
             b_w_out=b_w_out, final_norm=final_norm)
    m = dict(a_norm=m_a_norm, a_w_in=m_a_w_in, a_conv=m_a_conv, a_w_out=m_a_w_out, kv_norm=m_kv_norm, w_dkv=m_w_dkv,
             ckv_norm=m_ckv_norm, w_ukv=m_w_ukv, b_norm=m_b_norm, b_w_in=m_b_w_in, b_q_norm=m_b_q_norm,
             b_w_uq=m_b_w_uq, b_w_out=m_b_w_out, final_norm=m_final_norm)
    v = dict(a_norm=v_a_norm, a_w_in=v_a_w_in, a_conv=v_a_conv, a_w_out=v_a_w_out, kv_norm=v_kv_norm, w_dkv=v_w_dkv,
             ckv_norm=v_ckv_norm, w_ukv=v_w_ukv, b_norm=v_b_norm, b_w_in=v_b_w_in, b_q_norm=v_b_q_norm,
             b_w_uq=v_b_w_uq, b_w_out=v_b_w_out, final_norm=v_final_norm)

    w_pack = _pack(w, F32)
    small = jnp.concatenate([a_norm.reshape(-1), a_conv.reshape(-1)]).reshape(8, LANES)
    gath, gsmall = _gather_weights(w_pack.astype(BF16), small)
    full = _full_weights(gath, gsmall)
    for name in REPLICATED:
        full[name] = w[name].reshape(-1)

    loss, dx, grads = _local_step(x, positions, loss_target, full)
    loss = lax.psum(loss, ("x", "y", "c"))

    names = {"a_w_in": "a_w_in4"}
    parts = jnp.stack([_pack({n: _shard_of(names.get(n, n), grads[names.get(n, n)], j) for n in PACK_ORDER}, F32)
                       for j in range(N_CHIPS)])
    got = _scatter_partials(parts.reshape(N_DEV, HALF_ROWS, PACK_W))
    g_pack = _swap_halves(_sum_partials(got)).reshape(PACK_ROWS, PACK_W)

    d_pack, m_pack, v_pack = _adamw(w_pack, g_pack, _pack(m, F32), _pack(v, F32))
    outs = [_unpack(buf) for buf in (g_pack, d_pack, m_pack, v_pack)]
    return (loss, dx) + tuple(o[n] for o in outs for n in WEIGHTS)
```

```python
import functools
import math

import numpy as np
import jax
import jax.numpy as jnp
from jax import lax
from jax.experimental import pallas as pl
from jax.experimental.pallas import tpu as pltpu

F32 = jnp.float32
BF16 = jnp.bfloat16

D_MODEL = 1024
E_A = 1024
CONV_WIDTH = 3
N_HEADS = 8
QK_NOPE = 64
QK_ROPE = 32
V_HEAD = 64
KV_RANK = 256
Q_RANK = 384
E_B = N_HEADS * V_HEAD
ROPE_THETA = 10000.0
SOFTMAX_SCALE = 1.0 / math.sqrt(QK_NOPE + QK_ROPE)
LOG2E = math.log2(math.e)
LN2 = math.log(2.0)
Q_PRESCALE = SOFTMAX_SCALE * LOG2E
EPS = 1e-6
HEAD_PAD = 128
QK_PAD = N_HEADS * HEAD_PAD
ROPE_LO = QK_NOPE
ROPE_HALF = QK_ROPE // 2
KR_PAD = KV_RANK + HEAD_PAD

ADAM_LR = 0.001
ADAM_B1 = 0.9
ADAM_B2 = 0.999
ADAM_EPS = 1e-08
ADAM_WD = 0.01
ADAM_STEP = 10

VMEM_LIMIT = 56 * 1024 * 1024
ROW_TILE = 256
ATT_TILE = 512
LANES = 128
PACK_W = 1024

N_CHIPS = 4
N_DEV = 8


def _dot(a, b):
    return jnp.dot(a, b, preferred_element_type=F32)


def _dot_nt(a, b):
    return lax.dot_general(a, b, (((1,), (1,)), ((), ())), preferred_element_type=F32)


def _dot_tn(a, b):
    return lax.dot_general(a, b, (((0,), (0,)), ((), ())), preferred_element_type=F32)


def _rms(x):
    r = lax.rsqrt(jnp.mean(x * x, axis=-1, keepdims=True) + EPS)
    return r, x * r


def _rms_bwd(dxh, xh, r):
    return r * (dxh - xh * jnp.mean(dxh * xh, axis=-1, keepdims=True))


def _rope_fwd(a, c, s1, s2):
    return a * c + pltpu.roll(a, HEAD_PAD - ROPE_HALF, 1) * s1 + pltpu.roll(a, ROPE_HALF, 1) * s2


def _rope_bwd(g, c, s1, s2):
    return g * c + pltpu.roll(g * s1, ROPE_HALF, 1) + pltpu.roll(g * s2, HEAD_PAD - ROPE_HALF, 1)


def _sigmoid(x):
    return 1.0 / (1.0 + jnp.exp(-x))


def _row_spec(tm, n):
    return pl.BlockSpec((tm, n), lambda i: (i, 0))


def _const_spec(shape):
    nd = len(shape)
    return pl.BlockSpec(shape, lambda i: (0,) * nd, pipeline_mode=pl.Buffered(1))


def _acc_spec(shape):
    nd = len(shape)
    return pl.BlockSpec(shape, lambda i: (0,) * nd, pipeline_mode=pl.Buffered(1))


def _params(*sem):
    return pltpu.CompilerParams(dimension_semantics=sem, vmem_limit_bytes=VMEM_LIMIT)


def _rope_consts():
    lane = np.arange(HEAD_PAD)
    first = (lane >= ROPE_LO) & (lane < ROPE_LO + ROPE_HALF)
    second = (lane >= ROPE_LO + ROPE_HALF) & (lane < ROPE_LO + QK_ROPE)
    f = np.where(first, lane - ROPE_LO, np.where(second, lane - ROPE_LO - ROPE_HALF, 0))
    inv = np.float32(ROPE_THETA) ** (-(2 * f).astype(np.float32) / np.float32(QK_ROPE))
    out = np.zeros((8, HEAD_PAD), np.float32)
    out[0] = inv
    out[1] = first
    out[2] = second
    out[3] = lane < ROPE_LO
    return jnp.asarray(out)


def _rope_tables(pos_col):
    t = pos_col.shape[0]
    tm = min(ROW_TILE, t)

    def body(p_ref, k_ref, c_ref, s1_ref, s2_ref):
        inv, first, second, nope = k_ref[0:1, :], k_ref[1:2, :], k_ref[2:3, :], k_ref[3:4, :]
        ang = p_ref[...].astype(F32) * inv
        cs, sn = jnp.cos(ang), jnp.sin(ang)
        c_ref[...] = cs * (first + second) + nope
        s1_ref[...] = -sn * first
        s2_ref[...] = sn * second

    out = jax.ShapeDtypeStruct((t, HEAD_PAD), F32)
    return pl.pallas_call(
        body, grid=(t // tm,), name="rope_tables",
        in_specs=[_row_spec(tm, 1), _const_spec((8, HEAD_PAD))],
        out_specs=[_row_spec(tm, HEAD_PAD)] * 3, out_shape=[out] * 3,
        compiler_params=_params("parallel"),
    )(pos_col, _rope_consts())


def _shift_down(v, prev, row):
    p1, p2 = prev[7:8, :], prev[6:7, :]
    v1 = jnp.where(row == 0, p1, pltpu.roll(v, 1, 0))
    v2 = jnp.where(row == 0, p2, jnp.where(row == 1, p1, pltpu.roll(v, 2, 0)))
    return v1, v2


def _conv_fwd(x, seq, ga, w_in4, cw):
    t = x.shape[0]
    tm = min(ROW_TILE, seq)
    tiles_per_seq = seq // tm

    def body(x_ref, ga_ref, w_ref, cw_ref, b_ref, c_ref, u_ref, g_ref, ym_ref, carry_ref):
        i = pl.program_id(0)

        @pl.when(i % tiles_per_seq == 0)
        def _():
            carry_ref[...] = jnp.zeros_like(carry_ref)

        _, xh = _rms(x_ref[...])
        h = (xh * ga_ref[...]).astype(BF16)
        b, c, u, g = (_dot(h, w_ref[j]) for j in range(4))
        v = c * u
        row = lax.broadcasted_iota(jnp.int32, (tm, 1), 0)
        v1, v2 = _shift_down(v, carry_ref[...], row)
        carry_ref[...] = v[tm - 8:tm, :]
        cv = cw_ref[2:3, :] * v + cw_ref[1:2, :] * v1 + cw_ref[0:1, :] * v2
        b_ref[...] = b.astype(BF16)
        c_ref[...] = c.astype(BF16)
        u_ref[...] = u.astype(BF16)
        g_ref[...] = g.astype(BF16)
        ym_ref[...] = (g * _sigmoid(g) * b * cv).astype(BF16)

    out = jax.ShapeDtypeStruct((t, E_A), BF16)
    return pl.pallas_call(
        body, grid=(t // tm,), name="conv_fwd",
        in_specs=[_row_spec(tm, D_MODEL), _const_spec((1, D_MODEL)), _const_spec((4, D_MODEL, E_A)),
                  _const_spec((8, E_A))],
        out_specs=[_row_spec(tm, E_A)] * 5, out_shape=[out] * 5,
        scratch_shapes=[pltpu.VMEM((8, E_A), F32)],
        compiler_params=_params("arbitrary"),
    )(x, ga, w_in4, cw)


def _mid_fwd(x, ym, w_out, gk, gb, w_dkv, gc, w_uk, w_uv, w_bin, gq, w_uq, rc, rs1, rs2):
    t = x.shape[0]
    tm = min(ROW_TILE, t)

    def body(x_ref, ym_ref, wo_ref, gk_ref, gb_ref, wd_ref, gc_ref, wk_ref, wv_ref, wi_ref, gq_ref, wq_ref,
             c_ref, s1_ref, s2_ref, x1_ref, q_ref, k_ref, v_ref, gate_ref, cq_ref, ckv_ref):
        cb, s1b, s2b = c_ref[...], s1_ref[...], s2_ref[...]
        x1 = x_ref[...] + _dot(ym_ref[...], wo_ref[...])
        x1_ref[...] = x1
        _, xh = _rms(x1)
        hk = (xh * gk_ref[...]).astype(BF16)
        h1 = (xh * gb_ref[...]).astype(BF16)

        ckr = _dot(hk, wd_ref[...])
        ckv_raw = ckr[:, :KV_RANK]
        ckv_ref[...] = ckv_raw.astype(BF16)
        _, ch = _rms(ckv_raw)
        ckv = (ch * gc_ref[...]).astype(BF16)
        kr = _rope_fwd(ckr[:, KV_RANK:], cb, s1b, s2b)
        kn = _dot(ckv, wk_ref[...])
        for h in range(N_HEADS):
            sl = slice(h * HEAD_PAD, (h + 1) * HEAD_PAD)
            k_ref[:, sl] = (kn[:, sl] + kr).astype(BF16)
        v_ref[...] = _dot(ckv, wv_ref[...]).astype(BF16)

        pb = _dot(h1, wi_ref[...])
        cq = pb[:, :Q_RANK]
        cq_ref[...] = cq.astype(BF16)
        gate_ref[...] = pb[:, Q_RANK:].astype(BF16)
        _, cqh = _rms(cq)
        q = _dot((cqh * gq_ref[...]).astype(BF16), wq_ref[...])
        for h in range(N_HEADS):
            sl = slice(h * HEAD_PAD, (h + 1) * HEAD_PAD)
            q_ref[:, sl] = (_rope_fwd(q[:, sl], cb, s1b, s2b) * Q_PRESCALE).astype(BF16)

    def sds(n, dt):
        return jax.ShapeDtypeStruct((t, n), dt)

    return pl.pallas_call(
        body, grid=(t // tm,), name="mid_fwd",
        in_specs=[_row_spec(tm, D_MODEL), _row_spec(tm, E_A), _const_spec((E_A, D_MODEL)),
                  _const_spec((1, D_MODEL)), _const_spec((1, D_MODEL)), _const_spec((D_MODEL, KR_PAD)),
                  _const_spec((1, KV_RANK)), _const_spec((KV_RANK, QK_PAD)), _const_spec((KV_RANK, E_B)),
                  _const_spec((D_MODEL, Q_RANK + E_B)), _const_spec((1, Q_RANK)), _const_spec((Q_RANK, QK_PAD)),
                  _row_spec(tm, HEAD_PAD), _row_spec(tm, HEAD_PAD), _row_spec(tm, HEAD_PAD)],
        out_specs=[_row_spec(tm, D_MODEL), _row_spec(tm, QK_PAD), _row_spec(tm, QK_PAD), _row_spec(tm, E_B),
                   _row_spec(tm, E_B), _row_spec(tm, Q_RANK), _row_spec(tm, KV_RANK)],
        out_shape=[sds(D_MODEL, F32), sds(QK_PAD, BF16), sds(QK_PAD, BF16), sds(E_B, BF16), sds(E_B, BF16),
                   sds(Q_RANK, BF16), sds(KV_RANK, BF16)],
        compiler_params=_params("parallel"),
    )(x, ym, w_out, gk, gb, w_dkv, gc, w_uk, w_uv, w_bin, gq, w_uq, rc, rs1, rs2)


def _pair_specs(seq):
    qk = pl.BlockSpec((seq, 2 * HEAD_PAD), lambda b, p: (b, p))
    vo = pl.BlockSpec((seq, 2 * V_HEAD), lambda b, p: (b, p))
    st = pl.BlockSpec((None, 2, seq), lambda b, p: (p, 0, b))
    return qk, vo, st


def _attn_fwd(q, k, v, seq):
    t = q.shape[0]
    tq = min(ATT_TILE, seq)
    nq = seq // tq

    def body(q_ref, k_ref, v_ref, o_ref, lse_ref, m_scr, l_scr, acc_scr):
        row = lax.broadcasted_iota(jnp.int32, (tq, tq), 0)
        col = lax.broadcasted_iota(jnp.int32, (tq, tq), 1)
        lane = lax.broadcasted_iota(jnp.int32, (tq, 2 * V_HEAD), 1)

        def q_step(qi, _):
            q0 = pl.multiple_of(qi * tq, tq)
            m_scr[...] = jnp.full(m_scr.shape, -jnp.inf, F32)
            l_scr[...] = jnp.zeros_like(l_scr)
            acc_scr[...] = jnp.zeros_like(acc_scr)

            def k_step(ki, _, masked):
                k0 = pl.multiple_of(ki * tq, tq)
                vt = v_ref[pl.ds(k0, tq), :]
                for hh in range(2):
                    hs = slice(hh * HEAD_PAD, (hh + 1) * HEAD_PAD)
                    s = _dot_nt(q_ref[pl.ds(q0, tq), hs], k_ref[pl.ds(k0, tq), hs])
                    if masked:
                        s = jnp.where(col <= row, s, -jnp.inf)
                    m_old = m_scr[hh]
                    m_new = jnp.maximum(m_old, jnp.max(s, axis=-1, keepdims=True))
                    alpha = jnp.exp2(m_old - m_new)
                    ps = [jnp.exp2(s[:, j * LANES:(j + 1) * LANES] - m_new) for j in range(tq // LANES)]
                    l_scr[hh] = alpha * l_scr[hh] + functools.reduce(lambda a, b: a + b, ps)
                    p = jnp.concatenate(ps, axis=-1).astype(BF16)
                    acc_scr[hh] = alpha * acc_scr[hh] + _dot(p, vt)
                    m_scr[hh] = m_new
                return 0

            lax.fori_loop(0, qi, functools.partial(k_step, masked=False), 0)
            k_step(qi, 0, True)
            l0 = jnp.sum(l_scr[0], axis=-1, keepdims=True)
            l1 = jnp.sum(l_scr[1], axis=-1, keepdims=True)
            o_ref[pl.ds(q0, tq), :] = jnp.where(lane < V_HEAD, acc_scr[0] / l0, acc_scr[1] / l1).astype(BF16)
            stats = jnp.where(lane == 0, m_scr[0] + jnp.log2(l0), m_scr[1] + jnp.log2(l1)).T
            lse_ref[:, pl.ds(q0, tq)] = stats[0:2, :]
            return 0

        lax.fori_loop(0, nq, q_step, 0)

    qk, vo, st = _pair_specs(seq)
    return pl.pallas_call(
        body, grid=(t // seq, N_HEADS // 2), name="attn_fwd",
        in_specs=[qk, qk, vo], out_specs=[vo, st],
        out_shape=[jax.ShapeDtypeStruct((t, E_B), BF16), jax.ShapeDtypeStruct((N_HEADS // 2, 2, t), F32)],
        scratch_shapes=[pltpu.VMEM((2, tq, LANES), F32), pltpu.VMEM((2, tq, LANES), F32),
                        pltpu.VMEM((2, tq, 2 * V_HEAD), F32)],
        compiler_params=_params("parallel", "parallel"),
    )(q, k, v)


def _head_fwd_bwd(o, gate, x1, tgt, w_bout, gf):
    t = o.shape[0]
    tm = min(ROW_TILE, t)

    def body(o_ref, gate_ref, x1_ref, tgt_ref, w_ref, gf_ref,
             dx2_ref, do_ref, dgate_ref, dd_ref, loss_ref, dgf_ref, dw_ref):
        @pl.when(pl.program_id(0) == 0)
        def _():
            loss_ref[...] = jnp.zeros_like(loss_ref)
            dgf_ref[...] = jnp.zeros_like(dgf_ref)
            dw_ref[...] = jnp.zeros_like(dw_ref)

        o = o_ref[...].astype(F32)
        gt = gate_ref[...].astype(F32)
        sg = _sigmoid(gt)
        silu = gt * sg
        z = (o * silu).astype(BF16)
        x2 = x1_ref[...] + _dot(z, w_ref[...])
        r2, xh2 = _rms(x2)
        gf = gf_ref[...]
        err = xh2 * gf - tgt_ref[...]
        loss_ref[...] += 0.5 * jnp.sum(jnp.mean(err * err, axis=-1, keepdims=True))
        dy = err * (1.0 / D_MODEL)
        dgf_ref[...] += jnp.sum(dy * xh2, axis=0, keepdims=True)
        dx2 = _rms_bwd(dy * gf, xh2, r2)
        dx2_ref[...] = dx2
        dx2b = dx2.astype(BF16)
        dw_ref[...] += _dot_tn(z, dx2b)
        dz = _dot_nt(dx2b, w_ref[...])
        do = dz * silu
        do_ref[...] = do.astype(BF16)
        dgate_ref[...] = (dz * o * (sg * (1.0 + gt * (1.0 - sg)))).astype(BF16)
        prod = do * o
        lane = lax.broadcasted_iota(jnp.int32, (tm, 2 * V_HEAD), 1)
        cols = jnp.zeros((tm, LANES), F32)
        for p in range(N_HEADS // 2):
            blk = prod[:, p * 2 * V_HEAD:(p + 1) * 2 * V_HEAD]
            d0 = jnp.sum(jnp.where(lane < V_HEAD, blk, 0.0), axis=-1, keepdims=True)
            d1 = jnp.sum(jnp.where(lane < V_HEAD, 0.0, blk), axis=-1, keepdims=True)
            cols = jnp.where(lane == 2 * p, d0, jnp.where(lane == 2 * p + 1, d1, cols))
        rows = cols.T
        for h in range(N_HEADS):
            dd_ref[h // 2, h % 2:h % 2 + 1, :] = rows[h:h + 1, :]

    return pl.pallas_call(
        body, grid=(t // tm,), name="head_fwd_bwd",
        in_specs=[_row_spec(tm, E_B), _row_spec(tm, E_B), _row_spec(tm, D_MODEL), _row_spec(tm, D_MODEL),
                  _const_spec((E_B, D_MODEL)), _const_spec((1, D_MODEL))],
        out_specs=[_row_spec(tm, D_MODEL), _row_spec(tm, E_B), _row_spec(tm, E_B),
                   pl.BlockSpec((N_HEADS // 2, 2, tm), lambda i: (0, 0, i)),
                   _acc_spec((1, 1)), _acc_spec((1, D_MODEL)), _acc_spec((E_B, D_MODEL))],
        out_shape=[jax.ShapeDtypeStruct((t, D_MODEL), F32), jax.ShapeDtypeStruct((t, E_B), BF16),
                   jax.ShapeDtypeStruct((t, E_B), BF16), jax.ShapeDtypeStruct((N_HEADS // 2, 2, t), F32),
                   jax.ShapeDtypeStruct((1, 1), F32), jax.ShapeDtypeStruct((1, D_MODEL), F32),
                   jax.ShapeDtypeStruct((E_B, D_MODEL), F32)],
        compiler_params=_params("arbitrary"),
    )(o, gate, x1, tgt, w_bout, gf)


def _attn_bwd(q, k, v, do, lse, dd, seq):
    t = q.shape[0]
    tq = min(ATT_TILE, seq)
    nq = seq // tq

    def body(q_ref, k_ref, v_ref, do_ref, lse_ref, dd_ref, dq_ref, dk_ref, dv_ref, dq_acc, dk_acc, dv_acc):
        krow = lax.broadcasted_iota(jnp.int32, (tq, tq), 0)
        qcol = lax.broadcasted_iota(jnp.int32, (tq, tq), 1)
        lane = lax.broadcasted_iota(jnp.int32, (tq, 2 * V_HEAD), 1)
        dq_acc[...] = jnp.zeros_like(dq_acc)

        def k_step(ki, _):
            k0 = pl.multiple_of(ki * tq, tq)
            vt = v_ref[pl.ds(k0, tq), :]
            dk_acc[...] = jnp.zeros_like(dk_acc)
            dv_acc[...] = jnp.zeros_like(dv_acc)

            def q_step(qi, _, masked):
                q0 = pl.multiple_of(qi * tq, tq)
                do_pair = do_ref[pl.ds(q0, tq), :]
                for hh in range(2):
                    hs = slice(hh * HEAD_PAD, (hh + 1) * HEAD_PAD)
                    kt = k_ref[pl.ds(k0, tq), hs]
                    qt = q_ref[pl.ds(q0, tq), hs]
                    mine = (lane < V_HEAD) if hh == 0 else (lane >= V_HEAD)
                    do_h = jnp.where(mine, do_pair, jnp.zeros((), BF16))
                    st = _dot_nt(kt, qt)
                    if masked:
                        st = jnp.where(krow <= qcol, st, -jnp.inf)
                    pt = jnp.exp2(st - lse_ref[hh:hh + 1, pl.ds(q0, tq)])
                    dpt = _dot_nt(vt, do_h)
                    dst = (pt * (dpt - dd_ref[hh:hh + 1, pl.ds(q0, tq)])).astype(BF16)
                    dv_acc[...] += _dot(pt.astype(BF16), do_h)
                    dk_acc[:, hs] += _dot(dst, qt)
                    dq_acc[pl.ds(q0, tq), hs] += _dot_tn(dst, kt)
                return 0

            q_step(ki, 0, True)
            lax.fori_loop(ki + 1, nq, functools.partial(q_step, masked=False), 0)
            dk_ref[pl.ds(k0, tq), :] = (dk_acc[...] * LN2).astype(BF16)
            dv_ref[pl.ds(k0, tq), :] = dv_acc[...].astype(BF16)
            return 0

        lax.fori_loop(0, nq, k_step, 0)
        dq_ref[...] = (dq_acc[...] * SOFTMAX_SCALE).astype(BF16)

    qk, vo, st = _pair_specs(seq)
    return pl.pallas_call(
        body, grid=(t // seq, N_HEADS // 2), name="attn_bwd",
        in_specs=[qk, qk, vo, vo, st, st], out_specs=[qk, qk, vo],
        out_shape=[jax.ShapeDtypeStruct((t, QK_PAD), BF16), jax.ShapeDtypeStruct((t, QK_PAD), BF16),
                   jax.ShapeDtypeStruct((t, E_B), BF16)],
        scratch_shapes=[pltpu.VMEM((seq, 2 * HEAD_PAD), F32), pltpu.VMEM((tq, 2 * HEAD_PAD), F32),
                        pltpu.VMEM((tq, 2 * V_HEAD), F32)],
        compiler_params=_params("parallel", "parallel"),
    )(q, k, v, do, lse, dd)


def _mid_bwd(dq, dk, dv, dgate, dx2, x1, cq, ckv, rc, rs1, rs2, w_uq, w_bin, w_uk, w_uv, w_dkv, gq, gc, gk, gb):
    t = dq.shape[0]
    tm = min(ROW_TILE, t)

    def body(dq_ref, dk_ref, dv_ref, dgate_ref, dx2_ref, x1_ref, cq_ref, ckv_ref, c_ref, s1_ref, s2_ref,
             wq_ref, wi_ref, wk_ref, wv_ref, wd_ref, gq_ref, gc_ref, gk_ref, gb_ref,
             dx1_ref, dwq_ref, dwi_ref, dwk_ref, dwv_ref, dwd_ref, dgq_ref, dgc_ref, dgk_ref, dgb_ref):
        @pl.when(pl.program_id(0) == 0)
        def _():
            for ref in (dwq_ref, dwi_ref, dwk_ref, dwv_ref, dwd_ref, dgq_ref, dgc_ref, dgk_ref, dgb_ref):
                ref[...] = jnp.zeros_like(ref)

        cb, s1b, s2b = c_ref[...], s1_ref[...], s2_ref[...]
        r1, xh = _rms(x1_ref[...])
        gk, gb, gq, gc = gk_ref[...], gb_ref[...], gq_ref[...], gc_ref[...]
        hk = (xh * gk).astype(BF16)
        h1 = (xh * gb).astype(BF16)

        dqs = [_rope_bwd(dq_ref[:, h * HEAD_PAD:(h + 1) * HEAD_PAD].astype(F32), cb, s1b, s2b)
               for h in range(N_HEADS)]
        dqb = jnp.concatenate(dqs, axis=-1).astype(BF16)
        rq, cqh = _rms(cq_ref[...].astype(F32))
        dwq_ref[...] += _dot_tn((cqh * gq).astype(BF16), dqb)
        dcqn = _dot_nt(dqb, wq_ref[...])
        dgq_ref[...] += jnp.sum(dcqn * cqh, axis=0, keepdims=True)
        dcq = _rms_bwd(dcqn * gq, cqh, rq)
        dpb = jnp.concatenate([dcq.astype(BF16), dgate_ref[...]], axis=-1)
        dwi_ref[...] += _dot_tn(h1, dpb)
        dh1 = _dot_nt(dpb, wi_ref[...])

        rcv, ch = _rms(ckv_ref[...].astype(F32))
        ckvn = (ch * gc).astype(BF16)
        dkb, dvb = dk_ref[...], dv_ref[...]
        dwk_ref[...] += _dot_tn(ckvn, dkb)
        dwv_ref[...] += _dot_tn(ckvn, dvb)
        dckv = _dot_nt(dkb, wk_ref[...]) + _dot_nt(dvb, wv_ref[...])
        dgc_ref[...] += jnp.sum(dckv * ch, axis=0, keepdims=True)
        dckv_raw = _rms_bwd(dckv * gc, ch, rcv)
        dkr = dk_ref[:, 0:HEAD_PAD].astype(F32)
        for h in range(1, N_HEADS):
            dkr = dkr + dk_ref[:, h * HEAD_PAD:(h + 1) * HEAD_PAD].astype(F32)
        dkr = _rope_bwd(dkr, cb, s1b, s2b)
        dckr = jnp.concatenate([dckv_raw, dkr], axis=-1).astype(BF16)
        dwd_ref[...] += _dot_tn(hk, dckr)
        dhk = _dot_nt(dckr, wd_ref[...])

        dgb_ref[...] += jnp.sum(dh1 * xh, axis=0, keepdims=True)
        dgk_ref[...] += jnp.sum(dhk * xh, axis=0, keepdims=True)
        dx1_ref[...] = dx2_ref[...] + _rms_bwd(dh1 * gb + dhk * gk, xh, r1)

    acc_shapes = [(Q_RANK, QK_PAD), (D_MODEL, Q_RANK + E_B), (KV_RANK, QK_PAD), (KV_RANK, E_B), (D_MODEL, KR_PAD),
                  (1, Q_RANK), (1, KV_RANK), (1, D_MODEL), (1, D_MODEL)]
    return pl.pallas_call(
        body, grid=(t // tm,), name="mid_bwd",
        in_specs=[_row_spec(tm, QK_PAD), _row_spec(tm, QK_PAD), _row_spec(tm, E_B), _row_spec(tm, E_B),
                  _row_spec(tm, D_MODEL), _row_spec(tm, D_MODEL), _row_spec(tm, Q_RANK), _row_spec(tm, KV_RANK),
                  _row_spec(tm, HEAD_PAD), _row_spec(tm, HEAD_PAD), _row_spec(tm, HEAD_PAD),
                  _const_spec((Q_RANK, QK_PAD)), _const_spec((D_MODEL, Q_RANK + E_B)),
                  _const_spec((KV_RANK, QK_PAD)), _const_spec((KV_RANK, E_B)), _const_spec((D_MODEL, KR_PAD)),
                  _const_spec((1, Q_RANK)), _const_spec((1, KV_RANK)), _const_spec((1, D_MODEL)),
                  _const_spec((1, D_MODEL))],
        out_specs=[_row_spec(tm, D_MODEL)] + [_acc_spec(s) for s in acc_shapes],
        out_shape=[jax.ShapeDtypeStruct((t, D_MODEL), F32)] + [jax.ShapeDtypeStruct(s, F32) for s in acc_shapes],
        compiler_params=_params("arbitrary"),
    )(dq, dk, dv, dgate, dx2, x1, cq, ckv, rc, rs1, rs2, w_uq, w_bin, w_uk, w_uv, w_dkv, gq, gc, gk, gb)


def _conv_bwd(dx1, x, b, c, u, g, seq, w_out, w_in4, ga, cw):
    t = x.shape[0]
    tm = min(ROW_TILE, seq)
    tiles_per_seq = seq // tm
    n = t // tm
    halo = tm // 8

    def tile(i):
        return n - 1 - i

    def rev(width):
        return pl.BlockSpec((tm, width), lambda i: (tile(i), 0))

    def prev8(width):
        return pl.BlockSpec((8, width), lambda i: (jnp.maximum(tile(i) * halo - 1, 0), 0))

    def body(dx1_ref, x_ref, b_ref, c_ref, u_ref, g_ref, cp_ref, up_ref, wo_ref, wi_ref, ga_ref, cw_ref,
             dx_ref, dwi_ref, dwo_ref, dcw_ref, dga_ref, carry_ref):
        i = pl.program_id(0)
        j = tile(i)

        @pl.when(i == 0)
        def _():
            for ref in (dwi_ref, dwo_ref, dcw_ref, dga_ref):
                ref[...] = jnp.zeros_like(ref)

        @pl.when(j % tiles_per_seq == tiles_per_seq - 1)
        def _():
            carry_ref[...] = jnp.zeros_like(carry_ref)

        dx1 = dx1_ref[...]
        dx1b = dx1.astype(BF16)
        b, c, u, g = (r[...].astype(F32) for r in (b_ref, c_ref, u_ref, g_ref))
        v = c * u
        first = (j % tiles_per_seq == 0).astype(F32)
        vprev = cp_ref[...].astype(F32) * up_ref[...].astype(F32) * (1.0 - first)
        row = lax.broadcasted_iota(jnp.int32, (tm, 1), 0)
        v1, v2 = _shift_down(v, vprev, row)
        w0, w1, w2 = cw_ref[0:1, :], cw_ref[1:2, :], cw_ref[2:3, :]
        cv = w2 * v + w1 * v1 + w0 * v2
        sg = _sigmoid(g)
        silu = g * sg
        ym = (silu * b * cv).astype(BF16)
        dwo_ref[...] += _dot_tn(ym, dx1b)
        dym = _dot_nt(dx1b, wo_ref[...])
        db = dym * silu * cv
        dcv = dym * silu * b
        dg = dym * b * cv * (sg * (1.0 + g * (1.0 - sg)))

        nxt = carry_ref[...]
        n0, n1 = nxt[0:1, :], nxt[1:2, :]
        d1 = jnp.where(row == tm - 1, n0, pltpu.roll(dcv, tm - 1, 0))
        d2 = jnp.where(row == tm - 1, n1, jnp.where(row == tm - 2, n0, pltpu.roll(dcv, tm - 2, 0)))
        carry_ref[...] = dcv[0:8, :]
        dv = w2 * dcv + w1 * d1 + w0 * d2
        dcw_ref[0:1, :] += jnp.sum(dcv * v2, axis=0, keepdims=True)
        dcw_ref[1:2, :] += jnp.sum(dcv * v1, axis=0, keepdims=True)
        dcw_ref[2:3, :] += jnp.sum(dcv * v, axis=0, keepdims=True)

        r0, xh = _rms(x_ref[...])
        ga = ga_ref[...]
        h = (xh * ga).astype(BF16)
        dh = jnp.zeros((tm, D_MODEL), F32)
        for idx, dpart in enumerate((db, dv * u, dv * c, dg)):
            dpb = dpart.astype(BF16)
            dwi_ref[idx] += _dot_tn(h, dpb)
            dh = dh + _dot_nt(dpb, wi_ref[idx])
        dga_ref[...] += jnp.sum(dh * xh, axis=0, keepdims=True)
        dx_ref[...] = dx1 + _rms_bwd(dh * ga, xh, r0)

    acc_shapes = [(4, D_MODEL, E_A), (E_A, D_MODEL), (8, E_A), (1, D_MODEL)]
    return pl.pallas_call(
        body, grid=(n,), name="conv_bwd",
        in_specs=[rev(D_MODEL), rev(D_MODEL), rev(E_A), rev(E_A), rev(E_A), rev(E_A), prev8(E_A), prev8(E_A),
                  _const_spec((E_A, D_MODEL)), _const_spec((4, D_MODEL, E_A)), _const_spec((1, D_MODEL)),
                  _const_spec((8, E_A))],
        out_specs=[rev(D_MODEL)] + [_acc_spec(s) for s in acc_shapes],
        out_shape=[jax.ShapeDtypeStruct((t, D_MODEL), F32)] + [jax.ShapeDtypeStruct(s, F32) for s in acc_shapes],
        scratch_shapes=[pltpu.VMEM((8, E_A), F32)],
        compiler_params=_params("arbitrary"),
    )(dx1, x, b, c, u, g, c, u, w_out, w_in4, ga, cw)


WEIGHTS = ("a_norm", "a_w_in", "a_conv", "a_w_out", "kv_norm", "w_dkv", "ckv_norm", "w_ukv", "b_norm", "b_w_in",
           "b_q_norm", "b_w_uq", "b_w_out", "final_norm")
SHARD_SHAPES = {
    "a_norm": (1, 256), "a_w_in": (1, 1024, 1024), "a_conv": (1, 3, 256), "a_w_out": (1, 256, 1024),
    "kv_norm": (1024,), "w_dkv": (256, 288), "ckv_norm": (256,), "w_ukv": (256, 256), "b_norm": (1, 1024),
    "b_w_in": (1, 256, 896), "b_q_norm": (1, 384), "b_w_uq": (1, 384, 192), "b_w_out": (1, 512, 256),
    "final_norm": (1024,),
}
MATS = ("a_w_in", "a_w_out", "w_dkv", "w_ukv", "b_w_in", "b_w_uq", "b_w_out")
SMALL = ("a_norm", "a_conv", "kv_norm", "ckv_norm", "b_norm", "b_q_norm", "final_norm")
SMALL_FULL = {"a_norm": 1024, "a_conv": 3072, "kv_norm": 1024, "ckv_norm": 256, "b_norm": 1024, "b_q_norm": 384,
              "final_norm": 1024}
SMALL_ROWS = 8


def _mat2d(name, a):
    return a.reshape(SHARD_SHAPES[name][-2:])


def _prep_weights(gath, gsmall, w):
    def cols(a):
        return jnp.transpose(a, (1, 0, 2)).reshape(a.shape[1], -1)

    def pad_heads(a, width):
        a = a.reshape(a.shape[0], N_HEADS, width)
        return jnp.pad(a, ((0, 0), (0, 0), (0, HEAD_PAD - width))).reshape(a.shape[0], QK_PAD)

    row = lambda a: a.reshape(1, -1).astype(F32)
    w_dkv = gath["w_dkv"].reshape(D_MODEL, KV_RANK + QK_ROPE)
    w_ukv = cols(gath["w_ukv"]).reshape(KV_RANK, N_HEADS, 2, QK_NOPE)
    sm = gsmall.reshape(N_CHIPS, -1)
    a_conv = jnp.transpose(sm[:, 256:1024].reshape(N_CHIPS, CONV_WIDTH, 256), (1, 0, 2)).reshape(CONV_WIDTH, -1)
    return {
        "w_in4": gath["a_w_in"],
        "w_out": gath["a_w_out"].reshape(E_A, D_MODEL),
        "w_dkv": jnp.concatenate([w_dkv[:, :KV_RANK], jnp.zeros((D_MODEL, ROPE_LO), BF16), w_dkv[:, KV_RANK:],
                                  jnp.zeros((D_MODEL, HEAD_PAD - ROPE_LO - QK_ROPE), BF16)], axis=1),
        "w_uk": pad_heads(w_ukv[:, :, 0, :].reshape(KV_RANK, N_HEADS * QK_NOPE), QK_NOPE),
        "w_uv": w_ukv[:, :, 1, :].reshape(KV_RANK, E_B),
        "w_bin": gath["b_w_in"].reshape(D_MODEL, Q_RANK + E_B),
        "w_uq": pad_heads(cols(gath["b_w_uq"]), QK_NOPE + QK_ROPE),
        "w_bout": cols(gath["b_w_out"]),
        "cw": jnp.pad(a_conv, ((0, 8 - CONV_WIDTH), (0, 0))),
        "ga": row(sm[:, :256]), "gk": row(w["kv_norm"]), "gc": row(w["ckv_norm"]), "gb": row(w["b_norm"]),
        "gq": row(w["b_q_norm"]), "gf": row(w["final_norm"]),
    }


GRAD_MATS = ("w_in4", "w_out", "w_dkv", "w_uk", "w_uv", "w_bin", "w_uq", "w_bout")
GRAD_KIND = {"w_in4": "lead", "w_out": "row", "w_dkv": "row", "w_uk": "col", "w_uv": "col", "w_bin": "row",
             "w_uq": "col", "w_bout": "col"}


def _local_step(x, positions, tgt, wk):
    bsz, seq, _ = x.shape
    t = bsz * seq
    x2d = x.reshape(t, D_MODEL)
    rc, rs1, rs2 = _rope_tables(positions.reshape(t, 1))
    b, c, u, g, ym = _conv_fwd(x2d, seq, wk["ga"], wk["w_in4"], wk["cw"])
    x1, q, k, v, gate, cq, ckv = _mid_fwd(x2d, ym, wk["w_out"], wk["gk"], wk["gb"], wk["w_dkv"], wk["gc"], wk["w_uk"],
                                          wk["w_uv"], wk["w_bin"], wk["gq"], wk["w_uq"], rc, rs1, rs2)
    o, lse = _attn_fwd(q, k, v, seq)
    dx2, do, dgate, dd, loss, dgf, dw_bout = _head_fwd_bwd(o, gate, x1, tgt.reshape(t, D_MODEL), wk["w_bout"], wk["gf"])
    dq, dk, dv = _attn_bwd(q, k, v, do, lse, dd, seq)
    dx1, dwq, dw_bin, dwk, dwv, dwd, dgq, dgc, dgk, dgb = _mid_bwd(
        dq, dk, dv, dgate, dx2, x1, cq, ckv, rc, rs1, rs2, wk["w_uq"], wk["w_bin"], wk["w_uk"], wk["w_uv"], wk["w_dkv"],
        wk["gq"], wk["gc"], wk["gk"], wk["gb"])
    dx, dw_in4, dw_out, dcw, dga = _conv_bwd(dx1, x2d, b, c, u, g, seq, wk["w_out"], wk["w_in4"], wk["ga"], wk["cw"])
    mats = {"w_in4": dw_in4, "w_out": dw_out, "w_dkv": dwd, "w_uk": dwk, "w_uv": dwv, "w_bin": dw_bin, "w_uq": dwq,
            "w_bout": dw_bout}
    small = {"a_norm": dga, "a_conv": dcw[:CONV_WIDTH], "kv_norm": dgk, "ckv_norm": dgc, "b_norm": dgb,
             "b_q_norm": dgq, "final_norm": dgf}
    return loss[0, 0], dx.reshape(bsz, seq, D_MODEL), mats, small


def _shard_grads(sh, svec):
    j0 = 2 * lax.axis_index("x") + lax.axis_index("y")
    flat = svec.reshape(-1)
    off, small = 0, {}
    for n in SMALL:
        small[n] = flat[off:off + SMALL_FULL[n]]
        off += SMALL_FULL[n]
    dwd, dwk, dwv, dwq = sh["w_dkv"], sh["w_uk"], sh["w_uv"], sh["w_uq"]
    w_ukv = jnp.stack([dwk.reshape(KV_RANK, 2, HEAD_PAD)[:, :, :QK_NOPE], dwv.reshape(KV_RANK, 2, V_HEAD)], axis=2)
    return {
        "a_norm": lax.dynamic_slice(small["a_norm"], (j0 * 256,), (256,)),
        "a_conv": lax.dynamic_slice(small["a_conv"].reshape(CONV_WIDTH, E_A), (0, j0 * 256), (CONV_WIDTH, 256)),
        "kv_norm": small["kv_norm"], "ckv_norm": small["ckv_norm"], "b_norm": small["b_norm"],
        "b_q_norm": small["b_q_norm"], "final_norm": small["final_norm"],
        "a_w_in": sh["w_in4"], "a_w_out": sh["w_out"],
        "w_dkv": jnp.concatenate([dwd[:, :KV_RANK], dwd[:, KV_RANK + ROPE_LO:KV_RANK + ROPE_LO + QK_ROPE]], axis=1),
        "w_ukv": w_ukv.reshape(KV_RANK, 2 * (QK_NOPE + V_HEAD)),
        "b_w_in": sh["w_bin"],
        "b_w_uq": dwq.reshape(Q_RANK, 2, HEAD_PAD)[:, :, :QK_NOPE + QK_ROPE].reshape(Q_RANK, -1),
        "b_w_out": sh["w_bout"],
    }


MESH = pl.DeviceIdType.MESH
HBM = pl.BlockSpec(memory_space=pl.ANY)
WHOLE = pl.BlockSpec(memory_space=pltpu.VMEM)
FLIPS = ((1, 0), (0, 1), (1, 1))


def _place():
    return lax.axis_index("x"), lax.axis_index("y"), lax.axis_index("c")


def _remote(src, dst, send, recv, peer):
    return pltpu.make_async_remote_copy(src_ref=src, dst_ref=dst, send_sem=send, recv_sem=recv, device_id=peer,
                                        device_id_type=MESH)


def _comm_params():
    return pltpu.CompilerParams(vmem_limit_bytes=VMEM_LIMIT)


def _gather_weights(shards, small):
    n = len(shards)

    def body(*refs):
        w, sm = refs[:n], refs[n]
        outs, osm = refs[n + 1:2 * n + 1], refs[2 * n + 1]
        bf = refs[2 * n + 2:3 * n + 2]
        send_i, recv_i, send_d, recv_d, send_s, recv_s, local = refs[3 * n + 2:]
        x, y, c = _place()
        j0 = 2 * x + y
        sibling = (x, y, 1 - c)
        own = []
        for k in range(n):
            bf[k][...] = w[k][...].astype(BF16)
            own.append(pltpu.make_async_copy(bf[k], outs[k].at[j0], local.at[k]))
        own.append(pltpu.make_async_copy(sm, osm.at[j0], local.at[n]))
        for cp in own:
            cp.start()

        def half(k, cc):
            h = w[k].shape[0] // 2
            return pl.ds(pl.multiple_of(cc * h, 16), h)

        sends, waits, forwards, fwaits = [], [], [], []
        for i, (fx, fy) in enumerate(FLIPS):
            px, py = x ^ fx, y ^ fy
            pj = 2 * px + py
            for k in range(n):
                s = i * n + k
                sends.append(_remote(bf[k].at[half(k, c)], outs[k].at[j0, half(k, c)], send_i.at[s], recv_i.at[s],
                                     (px, py, c)))
                waits.append(_remote(bf[k].at[half(k, c)], outs[k].at[pj, half(k, c)], send_i.at[s], recv_i.at[s],
                                     (px, py, c)))
                forwards.append(_remote(outs[k].at[pj, half(k, c)], outs[k].at[pj, half(k, c)], send_d.at[s],
                                        recv_d.at[s], sibling))
                fwaits.append(_remote(outs[k].at[pj, half(k, 1 - c)], outs[k].at[pj, half(k, 1 - c)], send_d.at[s],
                                      recv_d.at[s], sibling))
            sends.append(_remote(sm, osm.at[j0], send_s.at[i], recv_s.at[i], (px, py, c)))
            waits.append(_remote(sm, osm.at[pj], send_s.at[i], recv_s.at[i], (px, py, c)))
        for cp in sends:
            cp.start()
        fi = 0
        for idx, cp in enumerate(waits):
            cp.wait_recv()
            if idx % (n + 1) != n:
                forwards[fi].start()
                fi += 1
        for cp in fwaits:
            cp.wait_recv()
        for cp in sends + forwards:
            cp.wait_send()
        for cp in own:
            cp.wait()

    return pl.pallas_call(
        body, name="gather_weights",
        in_specs=[WHOLE] * (n + 1), out_specs=[HBM] * (n + 1),
        out_shape=[jax.ShapeDtypeStruct((N_CHIPS,) + s.shape, BF16) for s in shards]
        + [jax.ShapeDtypeStruct((N_CHIPS,) + small.shape, small.dtype)],
        scratch_shapes=[pltpu.VMEM(s.shape, BF16) for s in shards]
        + [pltpu.SemaphoreType.DMA((3 * n,))] * 4 + [pltpu.SemaphoreType.DMA((3,))] * 2
        + [pltpu.SemaphoreType.DMA((n + 1,))],
        compiler_params=_comm_params(),
    )(*shards, small)


def _sub(ref, kind, j, cc):
    if kind == "lead":
        h = ref.shape[1] // 2
        return ref.at[j, pl.ds(pl.multiple_of(cc * h, 8), h), :]
    if kind == "row":
        rows = ref.shape[0] // N_CHIPS
        h = rows // 2
        return ref.at[pl.ds(pl.multiple_of(j * rows + cc * h, 8), h), :]
    cols = ref.shape[1] // N_CHIPS
    h = ref.shape[0] // 2
    return ref.at[pl.ds(pl.multiple_of(cc * h, 8), h), pl.ds(j * cols, cols)]


def _sub_shape(shape, kind):
    if kind == "lead":
        return (shape[1] // 2, shape[2])
    if kind == "row":
        return (shape[0] // N_CHIPS // 2, shape[1])
    return (shape[0] // 2, shape[1] // N_CHIPS)


def _pair_exchange(grads, kinds):
    n = len(grads)

    def body(*refs):
        g, out = refs[:n], refs[n:2 * n]
        send, recv = refs[2 * n:]
        x, y, c = _place()
        sibling = (x, y, 1 - c)
        copies = [_remote(_sub(g[k], kinds[k], j, 1 - c), out[k].at[j], send.at[k], recv.at[k], sibling)
                  for k in range(n) for j in range(N_CHIPS)]
        for cp in copies:
            cp.start()
        for k in range(n):
            done = _remote(out[k], out[k], send.at[k], recv.at[k], sibling)
            done.wait_recv()
            done.wait_send()

    return pl.pallas_call(
        body, name="pair_exchange",
        in_specs=[HBM] * n, out_specs=[HBM] * n,
        out_shape=[jax.ShapeDtypeStruct((N_CHIPS,) + _sub_shape(a.shape, kd), F32) for a, kd in zip(grads, kinds)],
        scratch_shapes=[pltpu.SemaphoreType.DMA((n,)), pltpu.SemaphoreType.DMA((n,))],
    )(*grads)


def _chip_partials(grads, theirs, kinds):
    n = len(grads)

    def body(*refs):
        g, other, out, mine = refs[:n], refs[n:2 * n], refs[2 * n:3 * n], refs[3 * n:4 * n]
        sem = refs[4 * n]
        c = lax.axis_index("c")
        for k in range(n):
            for j in range(N_CHIPS):
                pltpu.make_async_copy(_sub(g[k], kinds[k], j, c), mine[k].at[j], sem.at[k]).start()
        for k in range(n):
            pltpu.make_async_copy(mine[k], mine[k], sem.at[k]).wait()
            out[k][...] = (mine[k][...] + other[k][...]).astype(BF16)

    shapes = [(N_CHIPS,) + _sub_shape(a.shape, kd) for a, kd in zip(grads, kinds)]
    return pl.pallas_call(
        body, name="chip_partials",
        in_specs=[HBM] * n + [WHOLE] * n, out_specs=[WHOLE] * n,
        out_shape=[jax.ShapeDtypeStruct(s, BF16) for s in shapes],
        scratch_shapes=[pltpu.VMEM(s, F32) for s in shapes] + [pltpu.SemaphoreType.DMA((n,))],
        compiler_params=_comm_params(),
    )(*grads, *theirs)


def _chip_exchange(parts):
    n = len(parts)

    def body(*refs):
        p, out = refs[:n], refs[n:2 * n]
        send, recv, local = refs[2 * n:]
        x, y, c = _place()
        j0 = 2 * x + y
        own = [pltpu.make_async_copy(p[k].at[j0], out[k].at[j0], local.at[k]) for k in range(n)]
        for cp in own:
            cp.start()
        sends, waits = [], []
        for i, (fx, fy) in enumerate(FLIPS):
            px, py = x ^ fx, y ^ fy
            pj = 2 * px + py
            for k in range(n):
                s = i * n + k
                sends.append(_remote(p[k].at[pj], out[k].at[j0], send.at[s], recv.at[s], (px, py, c)))
                waits.append(_remote(p[k].at[pj], out[k].at[pj], send.at[s], recv.at[s], (px, py, c)))
        for cp in sends:
            cp.start()
        for cp in waits:
            cp.wait_recv()
        for cp in sends:
            cp.wait_send()
        for cp in own:
            cp.wait()

    return pl.pallas_call(
        body, name="chip_exchange",
        in_specs=[HBM] * n, out_specs=[HBM] * n,
        out_shape=[jax.ShapeDtypeStruct(a.shape, a.dtype) for a in parts],
        scratch_shapes=[pltpu.SemaphoreType.DMA((3 * n,)), pltpu.SemaphoreType.DMA((3 * n,)),
                        pltpu.SemaphoreType.DMA((n,))],
    )(*parts)


def _reduce_swap(parts):
    n = len(parts)

    def body(*refs):
        p, out, red = refs[:n], refs[n:2 * n], refs[2 * n:3 * n]
        send, recv, local = refs[3 * n:]
        x, y, c = _place()
        sibling = (x, y, 1 - c)
        own, give, take = [], [], []
        for k in range(n):
            red[k][...] = ((p[k][0].astype(F32) + p[k][1].astype(F32)) + p[k][2].astype(F32)) + p[k][3].astype(F32)
            own.append(pltpu.make_async_copy(red[k], out[k].at[c], local.at[k]))
            give.append(_remote(red[k], out[k].at[c], send.at[k], recv.at[k], sibling))
            take.append(_remote(red[k], out[k].at[1 - c], send.at[k], recv.at[k], sibling))
            own[k].start()
            give[k].start()
        for k in range(n):
            take[k].wait_recv()
            give[k].wait_send()
            own[k].wait()

    return pl.pallas_call(
        body, name="reduce_swap",
        in_specs=[WHOLE] * n, out_specs=[HBM] * n,
        out_shape=[jax.ShapeDtypeStruct((2,) + a.shape[1:], F32) for a in parts],
        scratch_shapes=[pltpu.VMEM(a.shape[1:], F32) for a in parts]
        + [pltpu.SemaphoreType.DMA((n,))] * 3,
        compiler_params=_comm_params(),
    )(*parts)


def _allreduce_small(vec):
    def body(v_ref, o_ref, got, send, recv):
        x, y, c = _place()
        me = 4 * x + 2 * y + c
        got[me] = v_ref[...]
        sends, waits = [], []
        for d in range(1, N_DEV):
            px, py, pc = x ^ (d >> 2), y ^ ((d >> 1) & 1), c ^ (d & 1)
            peer = 4 * px + 2 * py + pc
            sends.append(_remote(v_ref, got.at[me], send.at[d - 1], recv.at[d - 1], (px, py, pc)))
            waits.append(_remote(v_ref, got.at[peer], send.at[d - 1], recv.at[d - 1], (px, py, pc)))
        for cp in sends:
            cp.start()
        for cp in waits:
            cp.wait_recv()
        for cp in sends:
            cp.wait_send()
        acc = got[0]
        for k in range(1, N_DEV):
            acc = acc + got[k]
        o_ref[...] = acc

    return pl.pallas_call(
        body, name="allreduce_small",
        in_specs=[WHOLE], out_specs=WHOLE, out_shape=jax.ShapeDtypeStruct(vec.shape, vec.dtype),
        scratch_shapes=[pltpu.VMEM((N_DEV,) + vec.shape, vec.dtype), pltpu.SemaphoreType.DMA((N_DEV - 1,)),
                        pltpu.SemaphoreType.DMA((N_DEV - 1,))],
    )(vec)


def _adamw_math(w, g, m, v):
    m = ADAM_B1 * m + (1.0 - ADAM_B1) * g
    v = ADAM_B2 * v + (1.0 - ADAM_B2) * (g * g)
    m_hat = m / (1.0 - ADAM_B1 ** ADAM_STEP)
    v_hat = v / (1.0 - ADAM_B2 ** ADAM_STEP)
    return -ADAM_LR * (m_hat / (jnp.sqrt(v_hat) + ADAM_EPS) + ADAM_WD * w), m, v


def _adamw_tiled(w, g, m, v):
    rows, width = w.shape
    tm = rows // 4

    def body(w_ref, g_ref, m_ref, v_ref, d_ref, mo_ref, vo_ref):
        d_ref[...], mo_ref[...], vo_ref[...] = _adamw_math(w_ref[...], g_ref[...], m_ref[...], v_ref[...])

    spec = pl.BlockSpec((tm, width), lambda i: (i, 0))
    out = jax.ShapeDtypeStruct((rows, width), F32)
    return pl.pallas_call(
        body, grid=(rows // tm,), name="adamw_tiled",
        in_specs=[spec] * 4, out_specs=[spec] * 3, out_shape=[out] * 3,
        compiler_params=_params("parallel"),
    )(w, g, m, v)


def _adamw_many(ws, gs, ms, vs):
    n = len(ws)

    def body(*refs):
        for k in range(n):
            w_ref, g_ref, m_ref, v_ref = (refs[i * n + k] for i in range(4))
            d_ref, mo_ref, vo_ref = (refs[(4 + i) * n + k] for i in range(3))
            d_ref[...], mo_ref[...], vo_ref[...] = _adamw_math(w_ref[...], g_ref[...], m_ref[...], v_ref[...])

    outs = [jax.ShapeDtypeStruct(a.shape, F32) for a in ws]
    res = pl.pallas_call(
        body, name="adamw_many",
        in_specs=[WHOLE] * (4 * n), out_specs=[WHOLE] * (3 * n), out_shape=outs * 3,
        compiler_params=_comm_params(),
    )(*ws, *gs, *ms, *vs)
    return res[:n], res[n:2 * n], res[2 * n:]


def kernel(x, positions, a_norm, a_w_in, a_conv, a_w_out, kv_norm, w_dkv, ckv_norm, w_ukv, b_norm, b_w_in, b_q_norm, b_w_uq, b_w_out, final_norm, loss_target, m_a_norm, m_a_w_in, m_a_conv, m_a_w_out, m_kv_norm, m_w_dkv, m_ckv_norm, m_w_ukv, m_b_norm, m_b_w_in, m_b_q_norm, m_b_w_uq, m_b_w_out, m_final_norm, v_a_norm, v_a_w_in, v_a_conv, v_a_w_out, v_kv_norm, v_w_dkv, v_ckv_norm, v_w_ukv, v_b_norm, v_b_w_in, v_b_q_norm, v_b_w_uq, v_b_w_out, v_final_norm):
    w = dict(a_norm=a_norm, a_w_in=a_w_in, a_conv=a_conv, a_w_out=a_w_out, kv_norm=kv_norm, w_dkv=w_dkv,
             ckv_norm=ckv_norm, w_ukv=w_ukv, b_norm=b_norm, b_w_in=b_w_in, b_q_norm=b_q_norm, b_w_uq=b_w_uq,
             b_w_out=b_w_out, final_norm=final_norm)
    m = dict(a_norm=m_a_norm, a_w_in=m_a_w_in, a_conv=m_a_conv, a_w_out=m_a_w_out, kv_norm=m_kv_norm, w_dkv=m_w_dkv,
             ckv_norm=m_ckv_norm, w_ukv=m_w_ukv, b_norm=m_b_norm, b_w_in=m_b_w_in, b_q_norm=m_b_q_norm,
             b_w_uq=m_b_w_uq, b_w_out=m_b_w_out, final_norm=m_final_norm)
    v = dict(a_norm=v_a_norm, a_w_in=v_a_w_in, a_conv=v_a_conv, a_w_out=v_a_w_out, kv_norm=v_kv_norm, w_dkv=v_w_dkv,
             ckv_norm=v_ckv_norm, w_ukv=v_w_ukv, b_norm=v_b_norm, b_w_in=v_b_w_in, b_q_norm=v_b_q_norm,
             b_w_uq=v_b_w_uq, b_w_out=v_b_w_out, final_norm=v_final_norm)

    small = jnp.concatenate([a_norm.reshape(-1), a_conv.reshape(-1)]).reshape(8, LANES)
    gathered = _gather_weights([_mat2d(n, w[n]) for n in MATS], small)
    wk = _prep_weights(dict(zip(MATS, gathered[:-1])), gathered[-1], w)

    loss, dx, gmat, gsmall = _local_step(x, positions, loss_target, wk)
    loss = lax.psum(loss, ("x", "y", "c"))

    kinds = [GRAD_KIND[n] for n in GRAD_MATS]
    grads = [gmat[n] for n in GRAD_MATS]
    theirs = _pair_exchange(grads, kinds)
    mine = _reduce_swap(_chip_exchange(_chip_partials(grads, theirs, kinds)))
    flat = jnp.concatenate([gsmall[n].reshape(-1) for n in SMALL])
    svec = _allreduce_small(jnp.pad(flat, (0, SMALL_ROWS * PACK_W - flat.shape[0])).reshape(SMALL_ROWS, PACK_W))
    g = _shard_grads({n: a.reshape(-1, a.shape[-1]) for n, a in zip(GRAD_MATS, mine)}, svec)

    two_d = lambda n, a: a.reshape((-1, a.shape[-1]))
    big = "a_w_in"
    rest = [n for n in WEIGHTS if n != big]
    d_big, m_big, v_big = _adamw_tiled(*(two_d(big, t[big]) for t in (w, g, m, v)))
    d_r, m_r, v_r = _adamw_many(*([two_d(n, t[n]) for n in rest] for t in (w, g, m, v)))
    out = {"g": g, "d": dict(zip(rest, d_r)), "m": dict(zip(rest, m_r)), "v": dict(zip(rest, v_r))}
    out["d"][big], out["m"][big], out["v"][big] = d_big, m_big, v_big
    return (loss, dx) + tuple(out[kind][n].reshape(SHARD_SHAPES[n]) for kind in "gdmv" for n in WEIGHTS)
```

```python
import functools
import math

import numpy as np
import jax
import jax.numpy as jnp
from jax import lax
from jax.experimental import pallas as pl
from jax.experimental.pallas import tpu as pltpu

F32 = jnp.float32
BF16 = jnp.bfloat16

D_MODEL = 1024
E_A = 1024
CONV_WIDTH = 3
N_HEADS = 8
QK_NOPE = 64
QK_ROPE = 32
V_HEAD = 64
KV_RANK = 256
Q_RANK = 384
E_B = N_HEADS * V_HEAD
ROPE_THETA = 10000.0
SOFTMAX_SCALE = 1.0 / math.sqrt(QK_NOPE + QK_ROPE)
LOG2E = math.log2(math.e)
LN2 = math.log(2.0)
Q_PRESCALE = SOFTMAX_SCALE * LOG2E
EPS = 1e-6
HEAD_PAD = 128
QK_PAD = N_HEADS * HEAD_PAD
ROPE_LO = QK_NOPE
ROPE_HALF = QK_ROPE // 2
KR_PAD = KV_RANK + HEAD_PAD

ADAM_LR = 0.001
ADAM_B1 = 0.9
ADAM_B2 = 0.999
ADAM_EPS = 1e-08
ADAM_WD = 0.01
ADAM_STEP = 10

VMEM_LIMIT = 56 * 1024 * 1024
ROW_TILE = 512
CONV_BWD_TILE = 256
ATT_TILE_FWD = 1024
ATT_TILE = 512
LANES = 128
PACK_W = 1024

N_CHIPS = 4
N_DEV = 8


def _dot(a, b):
    return jnp.dot(a, b, preferred_element_type=F32)


def _dot_nt(a, b):
    return lax.dot_general(a, b, (((1,), (1,)), ((), ())), preferred_element_type=F32)


def _dot_tn(a, b):
    return lax.dot_general(a, b, (((0,), (0,)), ((), ())), preferred_element_type=F32)


def _rms(x):
    r = lax.rsqrt(jnp.mean(x * x, axis=-1, keepdims=True) + EPS)
    return r, x * r


def _rms_bwd(dxh, xh, r):
    return r * (dxh - xh * jnp.mean(dxh * xh, axis=-1, keepdims=True))


def _rope_fwd(a, c, s1, s2):
    return a * c + pltpu.roll(a, HEAD_PAD - ROPE_HALF, 1) * s1 + pltpu.roll(a, ROPE_HALF, 1) * s2


def _rope_bwd(g, c, s1, s2):
    return g * c + pltpu.roll(g * s1, ROPE_HALF, 1) + pltpu.roll(g * s2, HEAD_PAD - ROPE_HALF, 1)


def _sigmoid(x):
    return 1.0 / (1.0 + jnp.exp(-x))


def _row_spec(tm, n):
    return pl.BlockSpec((tm, n), lambda i: (i, 0))


def _const_spec(shape):
    nd = len(shape)
    return pl.BlockSpec(shape, lambda i: (0,) * nd, pipeline_mode=pl.Buffered(1))


def _acc_spec(shape):
    nd = len(shape)
    return pl.BlockSpec(shape, lambda i: (0,) * nd, pipeline_mode=pl.Buffered(1))


def _params(*sem):
    return pltpu.CompilerParams(dimension_semantics=sem, vmem_limit_bytes=VMEM_LIMIT)


def _rope_consts():
    lane = np.arange(HEAD_PAD)
    first = (lane >= ROPE_LO) & (lane < ROPE_LO + ROPE_HALF)
    second = (lane >= ROPE_LO + ROPE_HALF) & (lane < ROPE_LO + QK_ROPE)
    f = np.where(first, lane - ROPE_LO, np.where(second, lane - ROPE_LO - ROPE_HALF, 0))
    inv = np.float32(ROPE_THETA) ** (-(2 * f).astype(np.float32) / np.float32(QK_ROPE))
    out = np.zeros((8, HEAD_PAD), np.float32)
    out[0] = inv
    out[1] = first
    out[2] = second
    out[3] = lane < ROPE_LO
    return jnp.asarray(out)


def _rope_tables(pos_col):
    t = pos_col.shape[0]
    tm = min(ROW_TILE, t)

    def body(p_ref, k_ref, c_ref, s1_ref, s2_ref):
        inv, first, second, nope = k_ref[0:1, :], k_ref[1:2, :], k_ref[2:3, :], k_ref[3:4, :]
        ang = p_ref[...].astype(F32) * inv
        cs, sn = jnp.cos(ang), jnp.sin(ang)
        c_ref[...] = cs * (first + second) + nope
        s1_ref[...] = -sn * first
        s2_ref[...] = sn * second

    out = jax.ShapeDtypeStruct((t, HEAD_PAD), F32)
    return pl.pallas_call(
        body, grid=(t // tm,), name="rope_tables",
        in_specs=[_row_spec(tm, 1), _const_spec((8, HEAD_PAD))],
        out_specs=[_row_spec(tm, HEAD_PAD)] * 3, out_shape=[out] * 3,
        compiler_params=_params("parallel"),
    )(pos_col, _rope_consts())


def _shift_down(v, prev, row):
    p1, p2 = prev[7:8, :], prev[6:7, :]
    v1 = jnp.where(row == 0, p1, pltpu.roll(v, 1, 0))
    v2 = jnp.where(row == 0, p2, jnp.where(row == 1, p1, pltpu.roll(v, 2, 0)))
    return v1, v2


def _conv_fwd(x, seq, ga, w_in4, cw):
    t = x.shape[0]
    tm = min(ROW_TILE, seq)
    tiles_per_seq = seq // tm

    def body(x_ref, ga_ref, w_ref, cw_ref, b_ref, c_ref, u_ref, g_ref, ym_ref, carry_ref):
        i = pl.program_id(0)

        @pl.when(i % tiles_per_seq == 0)
        def _():
            carry_ref[...] = jnp.zeros_like(carry_ref)

        _, xh = _rms(x_ref[...])
        h = (xh * ga_ref[...]).astype(BF16)
        b, c, u, g = (_dot(h, w_ref[j]) for j in range(4))
        v = c * u
        row = lax.broadcasted_iota(jnp.int32, (tm, 1), 0)
        v1, v2 = _shift_down(v, carry_ref[...], row)
        carry_ref[...] = v[tm - 8:tm, :]
        cv = cw_ref[2:3, :] * v + cw_ref[1:2, :] * v1 + cw_ref[0:1, :] * v2
        b_ref[...] = b.astype(BF16)
        c_ref[...] = c.astype(BF16)
        u_ref[...] = u.astype(BF16)
        g_ref[...] = g.astype(BF16)
        ym_ref[...] = (g * _sigmoid(g) * b * cv).astype(BF16)

    out = jax.ShapeDtypeStruct((t, E_A), BF16)
    return pl.pallas_call(
        body, grid=(t // tm,), name="conv_fwd",
        in_specs=[_row_spec(tm, D_MODEL), _const_spec((1, D_MODEL)), _const_spec((4, D_MODEL, E_A)),
                  _const_spec((8, E_A))],
        out_specs=[_row_spec(tm, E_A)] * 5, out_shape=[out] * 5,
        scratch_shapes=[pltpu.VMEM((8, E_A), F32)],
        compiler_params=_params("arbitrary"),
    )(x, ga, w_in4, cw)


def _mid_fwd(x, ym, w_out, gk, gb, w_dkv, gc, w_uk, w_uv, w_bin, gq, w_uq, rc, rs1, rs2):
    t = x.shape[0]
    tm = min(ROW_TILE, t)

    def body(x_ref, ym_ref, wo_ref, gk_ref, gb_ref, wd_ref, gc_ref, wk_ref, wv_ref, wi_ref, gq_ref, wq_ref,
             c_ref, s1_ref, s2_ref, x1_ref, q_ref, k_ref, v_ref, gate_ref, cq_ref, ckv_ref):
        cb, s1b, s2b = c_ref[...], s1_ref[...], s2_ref[...]
        x1 = x_ref[...] + _dot(ym_ref[...], wo_ref[...])
        x1_ref[...] = x1
        _, xh = _rms(x1)
        hk = (xh * gk_ref[...]).astype(BF16)
        h1 = (xh * gb_ref[...]).astype(BF16)

        ckr = _dot(hk, wd_ref[...])
        ckv_raw = ckr[:, :KV_RANK]
        ckv_ref[...] = ckv_raw.astype(BF16)
        _, ch = _rms(ckv_raw)
        ckv = (ch * gc_ref[...]).astype(BF16)
        kr = _rope_fwd(ckr[:, KV_RANK:], cb, s1b, s2b)
        kn = _dot(ckv, wk_ref[...])
        for h in range(N_HEADS):
            sl = slice(h * HEAD_PAD, (h + 1) * HEAD_PAD)
            k_ref[:, sl] = (kn[:, sl] + kr).astype(BF16)
        v_ref[...] = _dot(ckv, wv_ref[...]).astype(BF16)

        pb = _dot(h1, wi_ref[...])
        cq = pb[:, :Q_RANK]
        cq_ref[...] = cq.astype(BF16)
        gate_ref[...] = pb[:, Q_RANK:].astype(BF16)
        _, cqh = _rms(cq)
        q = _dot((cqh * gq_ref[...]).astype(BF16), wq_ref[...])
        for h in range(N_HEADS):
            sl = slice(h * HEAD_PAD, (h + 1) * HEAD_PAD)
            q_ref[:, sl] = (_rope_fwd(q[:, sl], cb, s1b, s2b) * Q_PRESCALE).astype(BF16)

    def sds(n, dt):
        return jax.ShapeDtypeStruct((t, n), dt)

    return pl.pallas_call(
        body, grid=(t // tm,), name="mid_fwd",
        in_specs=[_row_spec(tm, D_MODEL), _row_spec(tm, E_A), _const_spec((E_A, D_MODEL)),
                  _const_spec((1, D_MODEL)), _const_spec((1, D_MODEL)), _const_spec((D_MODEL, KR_PAD)),
                  _const_spec((1, KV_RANK)), _const_spec((KV_RANK, QK_PAD)), _const_spec((KV_RANK, E_B)),
                  _const_spec((D_MODEL, Q_RANK + E_B)), _const_spec((1, Q_RANK)), _const_spec((Q_RANK, QK_PAD)),
                  _row_spec(tm, HEAD_PAD), _row_spec(tm, HEAD_PAD), _row_spec(tm, HEAD_PAD)],
        out_specs=[_row_spec(tm, D_MODEL), _row_spec(tm, QK_PAD), _row_spec(tm, QK_PAD), _row_spec(tm, E_B),
                   _row_spec(tm, E_B), _row_spec(tm, Q_RANK), _row_spec(tm, KV_RANK)],
        out_shape=[sds(D_MODEL, F32), sds(QK_PAD, BF16), sds(QK_PAD, BF16), sds(E_B, BF16), sds(E_B, BF16),
                   sds(Q_RANK, BF16), sds(KV_RANK, BF16)],
        compiler_params=_params("parallel"),
    )(x, ym, w_out, gk, gb, w_dkv, gc, w_uk, w_uv, w_bin, gq, w_uq, rc, rs1, rs2)


def _pair_specs(seq):
    qk = pl.BlockSpec((seq, 2 * HEAD_PAD), lambda b, p: (b, p))
    vo = pl.BlockSpec((seq, 2 * V_HEAD), lambda b, p: (b, p))
    st = pl.BlockSpec((None, 2, seq), lambda b, p: (p, 0, b))
    return qk, vo, st


def _attn_fwd(q, k, v, seq):
    t = q.shape[0]
    tq = min(ATT_TILE_FWD, seq)
    nq = seq // tq

    def body(q_ref, k_ref, v_ref, o_ref, lse_ref, m_scr, l_scr, acc_scr):
        row = lax.broadcasted_iota(jnp.int32, (tq, tq), 0)
        col = lax.broadcasted_iota(jnp.int32, (tq, tq), 1)
        lane = lax.broadcasted_iota(jnp.int32, (tq, 2 * V_HEAD), 1)

        def q_step(qi, _):
            q0 = pl.multiple_of(qi * tq, tq)
            m_scr[...] = jnp.full(m_scr.shape, -jnp.inf, F32)
            l_scr[...] = jnp.zeros_like(l_scr)
            acc_scr[...] = jnp.zeros_like(acc_scr)

            def k_step(ki, _, masked):
                k0 = pl.multiple_of(ki * tq, tq)
                vt = v_ref[pl.ds(k0, tq), :]
                for hh in range(2):
                    hs = slice(hh * HEAD_PAD, (hh + 1) * HEAD_PAD)
                    s = _dot_nt(q_ref[pl.ds(q0, tq), hs], k_ref[pl.ds(k0, tq), hs])
                    if masked:
                        s = jnp.where(col <= row, s, -jnp.inf)
                    m_old = m_scr[hh]
                    m_new = jnp.maximum(m_old, jnp.max(s, axis=-1, keepdims=True))
                    alpha = jnp.exp2(m_old - m_new)
                    ps = [jnp.exp2(s[:, j * LANES:(j + 1) * LANES] - m_new) for j in range(tq // LANES)]
                    l_scr[hh] = alpha * l_scr[hh] + functools.reduce(lambda a, b: a + b, ps)
                    p = jnp.concatenate(ps, axis=-1).astype(BF16)
                    acc_scr[hh] = alpha * acc_scr[hh] + _dot(p, vt)
                    m_scr[hh] = m_new
                return 0

            lax.fori_loop(0, qi, functools.partial(k_step, masked=False), 0)
            k_step(qi, 0, True)
            l0 = jnp.sum(l_scr[0], axis=-1, keepdims=True)
            l1 = jnp.sum(l_scr[1], axis=-1, keepdims=True)
            o_ref[pl.ds(q0, tq), :] = jnp.where(lane < V_HEAD, acc_scr[0] / l0, acc_scr[1] / l1).astype(BF16)
            stats = jnp.where(lane == 0, m_scr[0] + jnp.log2(l0), m_scr[1] + jnp.log2(l1)).T
            lse_ref[:, pl.ds(q0, tq)] = stats[0:2, :]
            return 0

        lax.fori_loop(0, nq, q_step, 0)

    qk, vo, st = _pair_specs(seq)
    return pl.pallas_call(
        body, grid=(t // seq, N_HEADS // 2), name="attn_fwd",
        in_specs=[qk, qk, vo], out_specs=[vo, st],
        out_shape=[jax.ShapeDtypeStruct((t, E_B), BF16), jax.ShapeDtypeStruct((N_HEADS // 2, 2, t), F32)],
        scratch_shapes=[pltpu.VMEM((2, tq, LANES), F32), pltpu.VMEM((2, tq, LANES), F32),
                        pltpu.VMEM((2, tq, 2 * V_HEAD), F32)],
        compiler_params=_params("parallel", "parallel"),
    )(q, k, v)


def _head_fwd_bwd(o, gate, x1, tgt, w_bout, gf):
    t = o.shape[0]
    tm = min(ROW_TILE, t)

    def body(o_ref, gate_ref, x1_ref, tgt_ref, w_ref, gf_ref,
             dx2_ref, do_ref, dgate_ref, dd_ref, loss_ref, dgf_ref, dw_ref):
        @pl.when(pl.program_id(0) == 0)
        def _():
            loss_ref[...] = jnp.zeros_like(loss_ref)
            dgf_ref[...] = jnp.zeros_like(dgf_ref)
            dw_ref[...] = jnp.zeros_like(dw_ref)

        o = o_ref[...].astype(F32)
        gt = gate_ref[...].astype(F32)
        sg = _sigmoid(gt)
        silu = gt * sg
        z = (o * silu).astype(BF16)
        x2 = x1_ref[...] + _dot(z, w_ref[...])
        r2, xh2 = _rms(x2)
        gf = gf_ref[...]
        err = xh2 * gf - tgt_ref[...]
        loss_ref[...] += 0.5 * jnp.sum(jnp.mean(err * err, axis=-1, keepdims=True))
        dy = err * (1.0 / D_MODEL)
        dgf_ref[...] += jnp.sum(dy * xh2, axis=0, keepdims=True)
        dx2 = _rms_bwd(dy * gf, xh2, r2)
        dx2_ref[...] = dx2
        dx2b = dx2.astype(BF16)
        dw_ref[...] += _dot_tn(z, dx2b)
        dz = _dot_nt(dx2b, w_ref[...])
        do = dz * silu
        do_ref[...] = do.astype(BF16)
        dgate_ref[...] = (dz * o * (sg * (1.0 + gt * (1.0 - sg)))).astype(BF16)
        prod = do * o
        lane = lax.broadcasted_iota(jnp.int32, (tm, 2 * V_HEAD), 1)
        cols = jnp.zeros((tm, LANES), F32)
        for p in range(N_HEADS // 2):
            blk = prod[:, p * 2 * V_HEAD:(p + 1) * 2 * V_HEAD]
            d0 = jnp.sum(jnp.where(lane < V_HEAD, blk, 0.0), axis=-1, keepdims=True)
            d1 = jnp.sum(jnp.where(lane < V_HEAD, 0.0, blk), axis=-1, keepdims=True)
            cols = jnp.where(lane == 2 * p, d0, jnp.where(lane == 2 * p + 1, d1, cols))
        rows = cols.T
        for h in range(N_HEADS):
            dd_ref[h // 2, h % 2:h % 2 + 1, :] = rows[h:h + 1, :]

    return pl.pallas_call(
        body, grid=(t // tm,), name="head_fwd_bwd",
        in_specs=[_row_spec(tm, E_B), _row_spec(tm, E_B), _row_spec(tm, D_MODEL), _row_spec(tm, D_MODEL),
                  _const_spec((E_B, D_MODEL)), _const_spec((1, D_MODEL))],
        out_specs=[_row_spec(tm, D_MODEL), _row_spec(tm, E_B), _row_spec(tm, E_B),
                   pl.BlockSpec((N_HEADS // 2, 2, tm), lambda i: (0, 0, i)),
                   _acc_spec((1, 1)), _acc_spec((1, D_MODEL)), _acc_spec((E_B, D_MODEL))],
        out_shape=[jax.ShapeDtypeStruct((t, D_MODEL), F32), jax.ShapeDtypeStruct((t, E_B), BF16),
                   jax.ShapeDtypeStruct((t, E_B), BF16), jax.ShapeDtypeStruct((N_HEADS // 2, 2, t), F32),
                   jax.ShapeDtypeStruct((1, 1), F32), jax.ShapeDtypeStruct((1, D_MODEL), F32),
                   jax.ShapeDtypeStruct((E_B, D_MODEL), F32)],
        compiler_params=_params("arbitrary"),
    )(o, gate, x1, tgt, w_bout, gf)


def _attn_bwd(q, k, v, do, lse, dd, seq):
    t = q.shape[0]
    tq = min(ATT_TILE, seq)
    nq = seq // tq

    def body(q_ref, k_ref, v_ref, do_ref, lse_ref, dd_ref, dq_ref, dk_ref, dv_ref, dq_acc, dk_acc, dv_acc):
        krow = lax.broadcasted_iota(jnp.int32, (tq, tq), 0)
        qcol = lax.broadcasted_iota(jnp.int32, (tq, tq), 1)
        lane = lax.broadcasted_iota(jnp.int32, (tq, 2 * V_HEAD), 1)
        dq_acc[...] = jnp.zeros_like(dq_acc)

        def k_step(ki, _):
            k0 = pl.multiple_of(ki * tq, tq)
            vt = v_ref[pl.ds(k0, tq), :]
            dk_acc[...] = jnp.zeros_like(dk_acc)
            dv_acc[...] = jnp.zeros_like(dv_acc)

            def q_step(qi, _, masked):
                q0 = pl.multiple_of(qi * tq, tq)
                do_pair = do_ref[pl.ds(q0, tq), :]
                for hh in range(2):
                    hs = slice(hh * HEAD_PAD, (hh + 1) * HEAD_PAD)
                    kt = k_ref[pl.ds(k0, tq), hs]
                    qt = q_ref[pl.ds(q0, tq), hs]
                    mine = (lane < V_HEAD) if hh == 0 else (lane >= V_HEAD)
                    do_h = jnp.where(mine, do_pair, jnp.zeros((), BF16))
                    st = _dot_nt(kt, qt)
                    if masked:
                        st = jnp.where(krow <= qcol, st, -jnp.inf)
                    pt = jnp.exp2(st - lse_ref[hh:hh + 1, pl.ds(q0, tq)])
                    dpt = _dot_nt(vt, do_h)
                    dst = (pt * (dpt - dd_ref[hh:hh + 1, pl.ds(q0, tq)])).astype(BF16)
                    dv_acc[...] += _dot(pt.astype(BF16), do_h)
                    dk_acc[:, hs] += _dot(dst, qt)
                    dq_acc[pl.ds(q0, tq), hs] += _dot_tn(dst, kt)
                return 0

            q_step(ki, 0, True)
            lax.fori_loop(ki + 1, nq, functools.partial(q_step, masked=False), 0)
            dk_ref[pl.ds(k0, tq), :] = (dk_acc[...] * LN2).astype(BF16)
            dv_ref[pl.ds(k0, tq), :] = dv_acc[...].astype(BF16)
            return 0

        lax.fori_loop(0, nq, k_step, 0)
        dq_ref[...] = (dq_acc[...] * SOFTMAX_SCALE).astype(BF16)

    qk, vo, st = _pair_specs(seq)
    return pl.pallas_call(
        body, grid=(t // seq, N_HEADS // 2), name="attn_bwd",
        in_specs=[qk, qk, vo, vo, st, st], out_specs=[qk, qk, vo],
        out_shape=[jax.ShapeDtypeStruct((t, QK_PAD), BF16), jax.ShapeDtypeStruct((t, QK_PAD), BF16),
                   jax.ShapeDtypeStruct((t, E_B), BF16)],
        scratch_shapes=[pltpu.VMEM((seq, 2 * HEAD_PAD), F32), pltpu.VMEM((tq, 2 * HEAD_PAD), F32),
                        pltpu.VMEM((tq, 2 * V_HEAD), F32)],
        compiler_params=_params("parallel", "parallel"),
    )(q, k, v, do, lse, dd)


def _mid_bwd(dq, dk, dv, dgate, dx2, x1, cq, ckv, rc, rs1, rs2, w_uq, w_bin, w_uk, w_uv, w_dkv, gq, gc, gk, gb):
    t = dq.shape[0]
    tm = min(ROW_TILE, t)

    def body(dq_ref, dk_ref, dv_ref, dgate_ref, dx2_ref, x1_ref, cq_ref, ckv_ref, c_ref, s1_ref, s2_ref,
             wq_ref, wi_ref, wk_ref, wv_ref, wd_ref, gq_ref, gc_ref, gk_ref, gb_ref,
             dx1_ref, dwq_ref, dwi_ref, dwk_ref, dwv_ref, dwd_ref, dgq_ref, dgc_ref, dgk_ref, dgb_ref):
        @pl.when(pl.program_id(0) == 0)
        def _():
            for ref in (dwq_ref, dwi_ref, dwk_ref, dwv_ref, dwd_ref, dgq_ref, dgc_ref, dgk_ref, dgb_ref):
                ref[...] = jnp.zeros_like(ref)

        cb, s1b, s2b = c_ref[...], s1_ref[...], s2_ref[...]
        r1, xh = _rms(x1_ref[...])
        gk, gb, gq, gc = gk_ref[...], gb_ref[...], gq_ref[...], gc_ref[...]
        hk = (xh * gk).astype(BF16)
        h1 = (xh * gb).astype(BF16)

        dqs = [_rope_bwd(dq_ref[:, h * HEAD_PAD:(h + 1) * HEAD_PAD].astype(F32), cb, s1b, s2b)
               for h in range(N_HEADS)]
        dqb = jnp.concatenate(dqs, axis=-1).astype(BF16)
        rq, cqh = _rms(cq_ref[...].astype(F32))
        dwq_ref[...] += _dot_tn((cqh * gq).astype(BF16), dqb)
        dcqn = _dot_nt(dqb, wq_ref[...])
        dgq_ref[...] += jnp.sum(dcqn * cqh, axis=0, keepdims=True)
        dcq = _rms_bwd(dcqn * gq, cqh, rq)
        dpb = jnp.concatenate([dcq.astype(BF16), dgate_ref[...]], axis=-1)
        dwi_ref[...] += _dot_tn(h1, dpb)
        dh1 = _dot_nt(dpb, wi_ref[...])

        rcv, ch = _rms(ckv_ref[...].astype(F32))
        ckvn = (ch * gc).astype(BF16)
        dkb, dvb = dk_ref[...], dv_ref[...]
        dwk_ref[...] += _dot_tn(ckvn, dkb)
        dwv_ref[...] += _dot_tn(ckvn, dvb)
        dckv = _dot_nt(dkb, wk_ref[...]) + _dot_nt(dvb, wv_ref[...])
        dgc_ref[...] += jnp.sum(dckv * ch, axis=0, keepdims=True)
        dckv_raw = _rms_bwd(dckv * gc, ch, rcv)
        dkr = dk_ref[:, 0:HEAD_PAD].astype(F32)
        for h in range(1, N_HEADS):
            dkr = dkr + dk_ref[:, h * HEAD_PAD:(h + 1) * HEAD_PAD].astype(F32)
        dkr = _rope_bwd(dkr, cb, s1b, s2b)
        dckr = jnp.concatenate([dckv_raw, dkr], axis=-1).astype(BF16)
        dwd_ref[...] += _dot_tn(hk, dckr)
        dhk = _dot_nt(dckr, wd_ref[...])

        dgb_ref[...] += jnp.sum(dh1 * xh, axis=0, keepdims=True)
        dgk_ref[...] += jnp.sum(dhk * xh, axis=0, keepdims=True)
        dx1_ref[...] = dx2_ref[...] + _rms_bwd(dh1 * gb + dhk * gk, xh, r1)

    acc_shapes = [(Q_RANK, QK_PAD), (D_MODEL, Q_RANK + E_B), (KV_RANK, QK_PAD), (KV_RANK, E_B), (D_MODEL, KR_PAD),
                  (1, Q_RANK), (1, KV_RANK), (1, D_MODEL), (1, D_MODEL)]
    return pl.pallas_call(
        body, grid=(t // tm,), name="mid_bwd",
        in_specs=[_row_spec(tm, QK_PAD), _row_spec(tm, QK_PAD), _row_spec(tm, E_B), _row_spec(tm, E_B),
                  _row_spec(tm, D_MODEL), _row_spec(tm, D_MODEL), _row_spec(tm, Q_RANK), _row_spec(tm, KV_RANK),
                  _row_spec(tm, HEAD_PAD), _row_spec(tm, HEAD_PAD), _row_spec(tm, HEAD_PAD),
                  _const_spec((Q_RANK, QK_PAD)), _const_spec((D_MODEL, Q_RANK + E_B)),
                  _const_spec((KV_RANK, QK_PAD)), _const_spec((KV_RANK, E_B)), _const_spec((D_MODEL, KR_PAD)),
                  _const_spec((1, Q_RANK)), _const_spec((1, KV_RANK)), _const_spec((1, D_MODEL)),
                  _const_spec((1, D_MODEL))],
        out_specs=[_row_spec(tm, D_MODEL)] + [_acc_spec(s) for s in acc_shapes],
        out_shape=[jax.ShapeDtypeStruct((t, D_MODEL), F32)] + [jax.ShapeDtypeStruct(s, F32) for s in acc_shapes],
        compiler_params=_params("arbitrary"),
    )(dq, dk, dv, dgate, dx2, x1, cq, ckv, rc, rs1, rs2, w_uq, w_bin, w_uk, w_uv, w_dkv, gq, gc, gk, gb)


def _conv_bwd(dx1, x, b, c, u, g, seq, w_out, w_in4, ga, cw):
    t = x.shape[0]
    tm = min(CONV_BWD_TILE, seq)
    tiles_per_seq = seq // tm
    n = t // tm
    halo = tm // 8

    def tile(i):
        return n - 1 - i

    def rev(width):
        return pl.BlockSpec((tm, width), lambda i: (tile(i), 0))

    def prev8(width):
        return pl.BlockSpec((8, width), lambda i: (jnp.maximum(tile(i) * halo - 1, 0), 0))

    def body(dx1_ref, x_ref, b_ref, c_ref, u_ref, g_ref, cp_ref, up_ref, wo_ref, wi_ref, ga_ref, cw_ref,
             dx_ref, dwi_ref, dwo_ref, dcw_ref, dga_ref, carry_ref):
        i = pl.program_id(0)
        j = tile(i)

        @pl.when(i == 0)
        def _():
            for ref in (dwi_ref, dwo_ref, dcw_ref, dga_ref):
                ref[...] = jnp.zeros_like(ref)

        @pl.when(j % tiles_per_seq == tiles_per_seq - 1)
        def _():
            carry_ref[...] = jnp.zeros_like(carry_ref)

        dx1 = dx1_ref[...]
        dx1b = dx1.astype(BF16)
        b, c, u, g = (r[...].astype(F32) for r in (b_ref, c_ref, u_ref, g_ref))
        v = c * u
        first = (j % tiles_per_seq == 0).astype(F32)
        vprev = cp_ref[...].astype(F32) * up_ref[...].astype(F32) * (1.0 - first)
        row = lax.broadcasted_iota(jnp.int32, (tm, 1), 0)
        v1, v2 = _shift_down(v, vprev, row)
        w0, w1, w2 = cw_ref[0:1, :], cw_ref[1:2, :], cw_ref[2:3, :]
        cv = w2 * v + w1 * v1 + w0 * v2
        sg = _sigmoid(g)
        silu = g * sg
        ym = (silu * b * cv).astype(BF16)
        dwo_ref[...] += _dot_tn(ym, dx1b)
        dym = _dot_nt(dx1b, wo_ref[...])
        db = dym * silu * cv
        dcv = dym * silu * b
        dg = dym * b * cv * (sg * (1.0 + g * (1.0 - sg)))

        nxt = carry_ref[...]
        n0, n1 = nxt[0:1, :], nxt[1:2, :]
        d1 = jnp.where(row == tm - 1, n0, pltpu.roll(dcv, tm - 1, 0))
        d2 = jnp.where(row == tm - 1, n1, jnp.where(row == tm - 2, n0, pltpu.roll(dcv, tm - 2, 0)))
        carry_ref[...] = dcv[0:8, :]
        dv = w2 * dcv + w1 * d1 + w0 * d2
        dcw_ref[0:1, :] += jnp.sum(dcv * v2, axis=0, keepdims=True)
        dcw_ref[1:2, :] += jnp.sum(dcv * v1, axis=0, keepdims=True)
        dcw_ref[2:3, :] += jnp.sum(dcv * v, axis=0, keepdims=True)

        r0, xh = _rms(x_ref[...])
        ga = ga_ref[...]
        h = (xh * ga).astype(BF16)
        dh = jnp.zeros((tm, D_MODEL), F32)
        for idx, dpart in enumerate((db, dv * u, dv * c, dg)):
            dpb = dpart.astype(BF16)
            dwi_ref[idx] += _dot_tn(h, dpb)
            dh = dh + _dot_nt(dpb, wi_ref[idx])
        dga_ref[...] += jnp.sum(dh * xh, axis=0, keepdims=True)
        dx_ref[...] = dx1 + _rms_bwd(dh * ga, xh, r0)

    acc_shapes = [(4, D_MODEL, E_A), (E_A, D_MODEL), (8, E_A), (1, D_MODEL)]
    return pl.pallas_call(
        body, grid=(n,), name="conv_bwd",
        in_specs=[rev(D_MODEL), rev(D_MODEL), rev(E_A), rev(E_A), rev(E_A), rev(E_A), prev8(E_A), prev8(E_A),
                  _const_spec((E_A, D_MODEL)), _const_spec((4, D_MODEL, E_A)), _const_spec((1, D_MODEL)),
                  _const_spec((8, E_A))],
        out_specs=[rev(D_MODEL)] + [_acc_spec(s) for s in acc_shapes],
        out_shape=[jax.ShapeDtypeStruct((t, D_MODEL), F32)] + [jax.ShapeDtypeStruct(s, F32) for s in acc_shapes],
        scratch_shapes=[pltpu.VMEM((8, E_A), F32)],
        compiler_params=_params("arbitrary"),
    )(dx1, x, b, c, u, g, c, u, w_out, w_in4, ga, cw)


WEIGHTS = ("a_norm", "a_w_in", "a_conv", "a_w_out", "kv_norm", "w_dkv", "ckv_norm", "w_ukv", "b_norm", "b_w_in",
           "b_q_norm", "b_w_uq", "b_w_out", "final_norm")
SHARD_SHAPES = {
    "a_norm": (1, 256), "a_w_in": (1, 1024, 1024), "a_conv": (1, 3, 256), "a_w_out": (1, 256, 1024),
    "kv_norm": (1024,), "w_dkv": (256, 288), "ckv_norm": (256,), "w_ukv": (256, 256), "b_norm": (1, 1024),
    "b_w_in": (1, 256, 896), "b_q_norm": (1, 384), "b_w_uq": (1, 384, 192), "b_w_out": (1, 512, 256),
    "final_norm": (1024,),
}
MATS = ("a_w_in", "a_w_out", "w_dkv", "w_ukv", "b_w_in", "b_w_uq", "b_w_out")
SMALL = ("a_norm", "a_conv", "kv_norm", "ckv_norm", "b_norm", "b_q_norm", "final_norm")
SMALL_FULL = {"a_norm": 1024, "a_conv": 3072, "kv_norm": 1024, "ckv_norm": 256, "b_norm": 1024, "b_q_norm": 384,
              "final_norm": 1024}
SMALL_ROWS = 8


def _mat2d(name, a):
    return a.reshape(SHARD_SHAPES[name][-2:])


def _prep_weights(gath, gsmall, w):
    def cols(a):
        return jnp.transpose(a, (1, 0, 2)).reshape(a.shape[1], -1)

    def pad_heads(a, width):
        a = a.reshape(a.shape[0], N_HEADS, width)
        return jnp.pad(a, ((0, 0), (0, 0), (0, HEAD_PAD - width))).reshape(a.shape[0], QK_PAD)

    row = lambda a: a.reshape(1, -1).astype(F32)
    w_dkv = gath["w_dkv"].reshape(D_MODEL, KV_RANK + QK_ROPE)
    w_ukv = cols(gath["w_ukv"]).reshape(KV_RANK, N_HEADS, 2, QK_NOPE)
    sm = gsmall.reshape(N_CHIPS, -1)
    a_conv = jnp.transpose(sm[:, 256:1024].reshape(N_CHIPS, CONV_WIDTH, 256), (1, 0, 2)).reshape(CONV_WIDTH, -1)
    return {
        "w_in4": gath["a_w_in"],
        "w_out": gath["a_w_out"].reshape(E_A, D_MODEL),
        "w_dkv": jnp.concatenate([w_dkv[:, :KV_RANK], jnp.zeros((D_MODEL, ROPE_LO), BF16), w_dkv[:, KV_RANK:],
                                  jnp.zeros((D_MODEL, HEAD_PAD - ROPE_LO - QK_ROPE), BF16)], axis=1),
        "w_uk": pad_heads(w_ukv[:, :, 0, :].reshape(KV_RANK, N_HEADS * QK_NOPE), QK_NOPE),
        "w_uv": w_ukv[:, :, 1, :].reshape(KV_RANK, E_B),
        "w_bin": gath["b_w_in"].reshape(D_MODEL, Q_RANK + E_B),
        "w_uq": pad_heads(cols(gath["b_w_uq"]), QK_NOPE + QK_ROPE),
        "w_bout": cols(gath["b_w_out"]),
        "cw": jnp.pad(a_conv, ((0, 8 - CONV_WIDTH), (0, 0))),
        "ga": row(sm[:, :256]), "gk": row(w["kv_norm"]), "gc": row(w["ckv_norm"]), "gb": row(w["b_norm"]),
        "gq": row(w["b_q_norm"]), "gf": row(w["final_norm"]),
    }


GRAD_MATS = ("w_in4", "w_out", "w_dkv", "w_uk", "w_uv", "w_bin", "w_uq", "w_bout")
GRAD_KIND = {"w_in4": "lead", "w_out": "row", "w_dkv": "row", "w_uk": "col", "w_uv": "col", "w_bin": "row",
             "w_uq": "col", "w_bout": "col"}


def _local_step(x, positions, tgt, wk):
    bsz, seq, _ = x.shape
    t = bsz * seq
    x2d = x.reshape(t, D_MODEL)
    rc, rs1, rs2 = _rope_tables(positions.reshape(t, 1))
    b, c, u, g, ym = _conv_fwd(x2d, seq, wk["ga"], wk["w_in4"], wk["cw"])
    x1, q, k, v, gate, cq, ckv = _mid_fwd(x2d, ym, wk["w_out"], wk["gk"], wk["gb"], wk["w_dkv"], wk["gc"], wk["w_uk"],
                                          wk["w_uv"], wk["w_bin"], wk["gq"], wk["w_uq"], rc, rs1, rs2)
    o, lse = _attn_fwd(q, k, v, seq)
    dx2, do, dgate, dd, loss, dgf, dw_bout = _head_fwd_bwd(o, gate, x1, tgt.reshape(t, D_MODEL), wk["w_bout"], wk["gf"])
    dq, dk, dv = _attn_bwd(q, k, v, do, lse, dd, seq)
    dx1, dwq, dw_bin, dwk, dwv, dwd, dgq, dgc, dgk, dgb = _mid_bwd(
        dq, dk, dv, dgate, dx2, x1, cq, ckv, rc, rs1, rs2, wk["w_uq"], wk["w_bin"], wk["w_uk"], wk["w_uv"], wk["w_dkv"],
        wk["gq"], wk["gc"], wk["gk"], wk["gb"])
    dx, dw_in4, dw_out, dcw, dga = _conv_bwd(dx1, x2d, b, c, u, g, seq, wk["w_out"], wk["w_in4"], wk["ga"], wk["cw"])
    mats = {"w_in4": dw_in4, "w_out": dw_out, "w_dkv": dwd, "w_uk": dwk, "w_uv": dwv, "w_bin": dw_bin, "w_uq": dwq,
            "w_bout": dw_bout}
    small = {"a_norm": dga, "a_conv": dcw[:CONV_WIDTH], "kv_norm": dgk, "ckv_norm": dgc, "b_norm": dgb,
             "b_q_norm": dgq, "final_norm": dgf}
    return loss[0, 0], dx.reshape(bsz, seq, D_MODEL), mats, small


def _shard_grads(sh, svec):
    j0 = 2 * lax.axis_index("x") + lax.axis_index("y")
    flat = svec.reshape(-1)
    off, small = 0, {}
    for n in SMALL:
        small[n] = flat[off:off + SMALL_FULL[n]]
        off += SMALL_FULL[n]
    dwd, dwk, dwv, dwq = sh["w_dkv"], sh["w_uk"], sh["w_uv"], sh["w_uq"]
    w_ukv = jnp.stack([dwk.reshape(KV_RANK, 2, HEAD_PAD)[:, :, :QK_NOPE], dwv.reshape(KV_RANK, 2, V_HEAD)], axis=2)
    return {
        "a_norm": lax.dynamic_slice(small["a_norm"], (j0 * 256,), (256,)),
        "a_conv": lax.dynamic_slice(small["a_conv"].reshape(CONV_WIDTH, E_A), (0, j0 * 256), (CONV_WIDTH, 256)),
        "kv_norm": small["kv_norm"], "ckv_norm": small["ckv_norm"], "b_norm": small["b_norm"],
        "b_q_norm": small["b_q_norm"], "final_norm": small["final_norm"],
        "a_w_in": sh["w_in4"], "a_w_out": sh["w_out"],
        "w_dkv": jnp.concatenate([dwd[:, :KV_RANK], dwd[:, KV_RANK + ROPE_LO:KV_RANK + ROPE_LO + QK_ROPE]], axis=1),
        "w_ukv": w_ukv.reshape(KV_RANK, 2 * (QK_NOPE + V_HEAD)),
        "b_w_in": sh["w_bin"],
        "b_w_uq": dwq.reshape(Q_RANK, 2, HEAD_PAD)[:, :, :QK_NOPE + QK_ROPE].reshape(Q_RANK, -1),
        "b_w_out": sh["w_bout"],
    }


MESH = pl.DeviceIdType.MESH
HBM = pl.BlockSpec(memory_space=pl.ANY)
WHOLE = pl.BlockSpec(memory_space=pltpu.VMEM)
FLIPS = ((1, 0), (0, 1), (1, 1))


def _place():
    return lax.axis_index("x"), lax.axis_index("y"), lax.axis_index("c")


def _remote(src, dst, send, recv, peer):
    return pltpu.make_async_remote_copy(src_ref=src, dst_ref=dst, send_sem=send, recv_sem=recv, device_id=peer,
                                        device_id_type=MESH)


def _comm_params():
    return pltpu.CompilerParams(vmem_limit_bytes=VMEM_LIMIT)


def _gather_weights(shards, small):
    n = len(shards)

    def body(*refs):
        w, sm = refs[:n], refs[n]
        outs, osm = refs[n + 1:2 * n + 1], refs[2 * n + 1]
        bf = refs[2 * n + 2:3 * n + 2]
        send_i, recv_i, send_d, recv_d, send_s, recv_s, local = refs[3 * n + 2:]
        x, y, c = _place()
        j0 = 2 * x + y
        sibling = (x, y, 1 - c)
        own = []
        for k in range(n):
            bf[k][...] = w[k][...].astype(BF16)
            own.append(pltpu.make_async_copy(bf[k], outs[k].at[j0], local.at[k]))
        own.append(pltpu.make_async_copy(sm, osm.at[j0], local.at[n]))
        for cp in own:
            cp.start()

        def half(k, cc):
            h = w[k].shape[0] // 2
            return pl.ds(pl.multiple_of(cc * h, 16), h)

        sends, waits, forwards, fwaits = [], [], [], []
        for i, (fx, fy) in enumerate(FLIPS):
            px, py = x ^ fx, y ^ fy
            pj = 2 * px + py
            for k in range(n):
                s = i * n + k
                sends.append(_remote(bf[k].at[half(k, c)], outs[k].at[j0, half(k, c)], send_i.at[s], recv_i.at[s],
                                     (px, py, c)))
                waits.append(_remote(bf[k].at[half(k, c)], outs[k].at[pj, half(k, c)], send_i.at[s], recv_i.at[s],
                                     (px, py, c)))
                forwards.append(_remote(outs[k].at[pj, half(k, c)], outs[k].at[pj, half(k, c)], send_d.at[s],
                                        recv_d.at[s], sibling))
                fwaits.append(_remote(outs[k].at[pj, half(k, 1 - c)], outs[k].at[pj, half(k, 1 - c)], send_d.at[s],
                                      recv_d.at[s], sibling))
            sends.append(_remote(sm, osm.at[j0], send_s.at[i], recv_s.at[i], (px, py, c)))
            waits.append(_remote(sm, osm.at[pj], send_s.at[i], recv_s.at[i], (px, py, c)))
        for cp in sends:
            cp.start()
        fi = 0
        for idx, cp in enumerate(waits):
            cp.wait_recv()
            if idx % (n + 1) != n:
                forwards[fi].start()
                fi += 1
        for cp in fwaits:
            cp.wait_recv()
        for cp in sends + forwards:
            cp.wait_send()
        for cp in own:
            cp.wait()

    return pl.pallas_call(
        body, name="gather_weights",
        in_specs=[WHOLE] * (n + 1), out_specs=[HBM] * (n + 1),
        out_shape=[jax.ShapeDtypeStruct((N_CHIPS,) + s.shape, BF16) for s in shards]
        + [jax.ShapeDtypeStruct((N_CHIPS,) + small.shape, small.dtype)],
        scratch_shapes=[pltpu.VMEM(s.shape, BF16) for s in shards]
        + [pltpu.SemaphoreType.DMA((3 * n,))] * 4 + [pltpu.SemaphoreType.DMA((3,))] * 2
        + [pltpu.SemaphoreType.DMA((n + 1,))],
        compiler_params=_comm_params(),
    )(*shards, small)


def _sub(ref, kind, j, cc):
    if kind == "lead":
        h = ref.shape[1] // 2
        return ref.at[j, pl.ds(pl.multiple_of(cc * h, 8), h), :]
    if kind == "row":
        rows = ref.shape[0] // N_CHIPS
        h = rows // 2
        return ref.at[pl.ds(pl.multiple_of(j * rows + cc * h, 8), h), :]
    cols = ref.shape[1] // N_CHIPS
    h = ref.shape[0] // 2
    return ref.at[pl.ds(pl.multiple_of(cc * h, 8), h), pl.ds(j * cols, cols)]


def _sub_shape(shape, kind):
    if kind == "lead":
        return (shape[1] // 2, shape[2])
    if kind == "row":
        return (shape[0] // N_CHIPS // 2, shape[1])
    return (shape[0] // 2, shape[1] // N_CHIPS)


def _pair_exchange(grads, kinds):
    n = len(grads)

    def body(*refs):
        g, out = refs[:n], refs[n:2 * n]
        send, recv = refs[2 * n:]
        x, y, c = _place()
        sibling = (x, y, 1 - c)
        copies = [_remote(_sub(g[k], kinds[k], j, 1 - c), out[k].at[j], send.at[k], recv.at[k], sibling)
                  for k in range(n) for j in range(N_CHIPS)]
        for cp in copies:
            cp.start()
        for k in range(n):
            done = _remote(out[k], out[k], send.at[k], recv.at[k], sibling)
            done.wait_recv()
            done.wait_send()

    return pl.pallas_call(
        body, name="pair_exchange",
        in_specs=[HBM] * n, out_specs=[HBM] * n,
        out_shape=[jax.ShapeDtypeStruct((N_CHIPS,) + _sub_shape(a.shape, kd), F32) for a, kd in zip(grads, kinds)],
        scratch_shapes=[pltpu.SemaphoreType.DMA((n,)), pltpu.SemaphoreType.DMA((n,))],
    )(*grads)


def _chip_partials(grads, theirs, kinds):
    n = len(grads)

    def body(*refs):
        g, other, out, mine = refs[:n], refs[n:2 * n], refs[2 * n:3 * n], refs[3 * n:4 * n]
        sem = refs[4 * n]
        c = lax.axis_index("c")
        for k in range(n):
            for j in range(N_CHIPS):
                pltpu.make_async_copy(_sub(g[k], kinds[k], j, c), mine[k].at[j], sem.at[k]).start()
        for k in range(n):
            pltpu.make_async_copy(mine[k], mine[k], sem.at[k]).wait()
            out[k][...] = (mine[k][...] + other[k][...]).astype(BF16)

    shapes = [(N_CHIPS,) + _sub_shape(a.shape, kd) for a, kd in zip(grads, kinds)]
    return pl.pallas_call(
        body, name="chip_partials",
        in_specs=[HBM] * n + [WHOLE] * n, out_specs=[WHOLE] * n,
        out_shape=[jax.ShapeDtypeStruct(s, BF16) for s in shapes],
        scratch_shapes=[pltpu.VMEM(s, F32) for s in shapes] + [pltpu.SemaphoreType.DMA((n,))],
        compiler_params=_comm_params(),
    )(*grads, *theirs)


def _chip_exchange(parts):
    n = len(parts)

    def body(*refs):
        p, out = refs[:n], refs[n:2 * n]
        send, recv, local = refs[2 * n:]
        x, y, c = _place()
        j0 = 2 * x + y
        own = [pltpu.make_async_copy(p[k].at[j0], out[k].at[j0], local.at[k]) for k in range(n)]
        for cp in own:
            cp.start()
        sends, waits = [], []
        for i, (fx, fy) in enumerate(FLIPS):
            px, py = x ^ fx, y ^ fy
            pj = 2 * px + py
            for k in range(n):
                s = i * n + k
                sends.append(_remote(p[k].at[pj], out[k].at[j0], send.at[s], recv.at[s], (px, py, c)))
                waits.append(_remote(p[k].at[pj], out[k].at[pj], send.at[s], recv.at[s], (px, py, c)))
        for cp in sends:
            cp.start()
        for cp in waits:
            cp.wait_recv()
        for cp in sends:
            cp.wait_send()
        for cp in own:
            cp.wait()

    return pl.pallas_call(
        body, name="chip_exchange",
        in_specs=[HBM] * n, out_specs=[HBM] * n,
        out_shape=[jax.ShapeDtypeStruct(a.shape, a.dtype) for a in parts],
        scratch_shapes=[pltpu.SemaphoreType.DMA((3 * n,)), pltpu.SemaphoreType.DMA((3 * n,)),
                        pltpu.SemaphoreType.DMA((n,))],
    )(*parts)


def _reduce_swap(parts):
    n = len(parts)

    def body(*refs):
        p, out, red = refs[:n], refs[n:2 * n], refs[2 * n:3 * n]
        send, recv, local = refs[3 * n:]
        x, y, c = _place()
        sibling = (x, y, 1 - c)
        own, give, take = [], [], []
        for k in range(n):
            red[k][...] = ((p[k][0].astype(F32) + p[k][1].astype(F32)) + p[k][2].astype(F32)) + p[k][3].astype(F32)
            own.append(pltpu.make_async_copy(red[k], out[k].at[c], local.at[k]))
            give.append(_remote(red[k], out[k].at[c], send.at[k], recv.at[k], sibling))
            take.append(_remote(red[k], out[k].at[1 - c], send.at[k], recv.at[k], sibling))
            own[k].start()
            give[k].start()
        for k in range(n):
            take[k].wait_recv()
            give[k].wait_send()
            own[k].wait()

    return pl.pallas_call(
        body, name="reduce_swap",
        in_specs=[WHOLE] * n, out_specs=[HBM] * n,
        out_shape=[jax.ShapeDtypeStruct((2,) + a.shape[1:], F32) for a in parts],
        scratch_shapes=[pltpu.VMEM(a.shape[1:], F32) for a in parts]
        + [pltpu.SemaphoreType.DMA((n,))] * 3,
        compiler_params=_comm_params(),
    )(*parts)


def _allreduce_small(vec):
    def body(v_ref, o_ref, got, send, recv):
        x, y, c = _place()
        me = 4 * x + 2 * y + c
        got[me] = v_ref[...]
        sends, waits = [], []
        for d in range(1, N_DEV):
            px, py, pc = x ^ (d >> 2), y ^ ((d >> 1) & 1), c ^ (d & 1)
            peer = 4 * px + 2 * py + pc
            sends.append(_remote(v_ref, got.at[me], send.at[d - 1], recv.at[d - 1], (px, py, pc)))
            waits.append(_remote(v_ref, got.at[peer], send.at[d - 1], recv.at[d - 1], (px, py, pc)))
        for cp in sends:
            cp.start()
        for cp in waits:
            cp.wait_recv()
        for cp in sends:
            cp.wait_send()
        acc = got[0]
        for k in range(1, N_DEV):
            acc = acc + got[k]
        o_ref[...] = acc

    return pl.pallas_call(
        body, name="allreduce_small",
        in_specs=[WHOLE], out_specs=WHOLE, out_shape=jax.ShapeDtypeStruct(vec.shape, vec.dtype),
        scratch_shapes=[pltpu.VMEM((N_DEV,) + vec.shape, vec.dtype), pltpu.SemaphoreType.DMA((N_DEV - 1,)),
                        pltpu.SemaphoreType.DMA((N_DEV - 1,))],
    )(vec)


def _adamw_math(w, g, m, v):
    m = ADAM_B1 * m + (1.0 - ADAM_B1) * g
    v = ADAM_B2 * v + (1.0 - ADAM_B2) * (g * g)
    m_hat = m / (1.0 - ADAM_B1 ** ADAM_STEP)
    v_hat = v / (1.0 - ADAM_B2 ** ADAM_STEP)
    return -ADAM_LR * (m_hat / (jnp.sqrt(v_hat) + ADAM_EPS) + ADAM_WD * w), m, v


def _adamw_tiled(w, g, m, v):
    rows, width = w.shape
    tm = rows // 4

    def body(w_ref, g_ref, m_ref, v_ref, d_ref, mo_ref, vo_ref):
        d_ref[...], mo_ref[...], vo_ref[...] = _adamw_math(w_ref[...], g_ref[...], m_ref[...], v_ref[...])

    spec = pl.BlockSpec((tm, width), lambda i: (i, 0))
    out = jax.ShapeDtypeStruct((rows, width), F32)
    return pl.pallas_call(
        body, grid=(rows // tm,), name="adamw_tiled",
        in_specs=[spec] * 4, out_specs=[spec] * 3, out_shape=[out] * 3,
        compiler_params=_params("parallel"),
    )(w, g, m, v)


def _adamw_many(ws, gs, ms, vs):
    n = len(ws)

    def body(*refs):
        for k in range(n):
            w_ref, g_ref, m_ref, v_ref = (refs[i * n + k] for i in range(4))
            d_ref, mo_ref, vo_ref = (refs[(4 + i) * n + k] for i in range(3))
            d_ref[...], mo_ref[...], vo_ref[...] = _adamw_math(w_ref[...], g_ref[...], m_ref[...], v_ref[...])

    outs = [jax.ShapeDtypeStruct(a.shape, F32) for a in ws]
    res = pl.pallas_call(
        body, name="adamw_many",
        in_specs=[WHOLE] * (4 * n), out_specs=[WHOLE] * (3 * n), out_shape=outs * 3,
        compiler_params=_comm_params(),
    )(*ws, *gs, *ms, *vs)
    return res[:n], res[n:2 * n], res[2 * n:]


def kernel(x, positions, a_norm, a_w_in, a_conv, a_w_out, kv_norm, w_dkv, ckv_norm, w_ukv, b_norm, b_w_in, b_q_norm, b_w_uq, b_w_out, final_norm, loss_target, m_a_norm, m_a_w_in, m_a_conv, m_a_w_out, m_kv_norm, m_w_dkv, m_ckv_norm, m_w_ukv, m_b_norm, m_b_w_in, m_b_q_norm, m_b_w_uq, m_b_w_out, m_final_norm, v_a_norm, v_a_w_in, v_a_conv, v_a_w_out, v_kv_norm, v_w_dkv, v_ckv_norm, v_w_ukv, v_b_norm, v_b_w_in, v_b_q_norm, v_b_w_uq, v_b_w_out, v_final_norm):
    w = dict(a_norm=a_norm, a_w_in=a_w_in, a_conv=a_conv, a_w_out=a_w_out, kv_norm=kv_norm, w_dkv=w_dkv,
             ckv_norm=ckv_norm, w_ukv=w_ukv, b_norm=b_norm, b_w_in=b_w_in, b_q_norm=b_q_norm, b_w_uq=b_w_uq,
             b_w_out=b_w_out, final_norm=final_norm)
    m = dict(a_norm=m_a_norm, a_w_in=m_a_w_in, a_conv=m_a_conv, a_w_out=m_a_w_out, kv_norm=m_kv_norm, w_dkv=m_w_dkv,
             ckv_norm=m_ckv_norm, w_ukv=m_w_ukv, b_norm=m_b_norm, b_w_in=m_b_w_in, b_q_norm=m_b_q_norm,
             b_w_uq=m_b_w_uq, b_w_out=m_b_w_out, final_norm=m_final_norm)
    v = dict(a_norm=v_a_norm, a_w_in=v_a_w_in, a_conv=v_a_conv, a_w_out=v_a_w_out, kv_norm=v_kv_norm, w_dkv=v_w_dkv,
             ckv_norm=v_ckv_norm, w_ukv=v_w_ukv, b_norm=v_b_norm, b_w_in=v_b_w_in, b_q_norm=v_b_q_norm,
             b_w_uq=v_b_w_uq, b_w_out=v_b_w_out, final_norm=v_final_norm)

    small = jnp.concatenate([a_norm.reshape(-1), a_conv.reshape(-1)]).reshape(8, LANES)
    gathered = _gather_weights([_mat2d(n, w[n]) for n in MATS], small)
    wk = _prep_weights(dict(zip(MATS, gathered[:-1])), gathered[-1], w)

    loss, dx, gmat, gsmall = _local_step(x, positions, loss_target, wk)
    loss = lax.psum(loss, ("x", "y", "c"))

    kinds = [GRAD_KIND[n] for n in GRAD_MATS]
    grads = [gmat[n] for n in GRAD_MATS]
    theirs = _pair_exchange(grads, kinds)
    mine = _reduce_swap(_chip_exchange(_chip_partials(grads, theirs, kinds)))
    flat = jnp.concatenate([gsmall[n].reshape(-1) for n in SMALL])
    svec = _allreduce_small(jnp.pad(flat, (0, SMALL_ROWS * PACK_W - flat.shape[0])).reshape(SMALL_ROWS, PACK_W))
    g = _shard_grads({n: a.reshape(-1, a.shape[-1]) for n, a in zip(GRAD_MATS, mine)}, svec)

    two_d = lambda n, a: a.reshape((-1, a.shape[-1]))
    big = "a_w_in"
    rest = [n for n in WEIGHTS if n != big]
    d_big, m_big, v_big = _adamw_tiled(*(two_d(big, t[big]) for t in (w, g, m, v)))
    d_r, m_r, v_r = _adamw_many(*([two_d(n, t[n]) for n in rest] for t in (w, g, m, v)))
    out = {"g": g, "d": dict(zip(rest, d_r)), "m": dict(zip(rest, m_r)), "v": dict(zip(rest, v_r))}
    out["d"][big], out["m"][big], out["v"][big] = d_big, m_big, v_big
    return (loss, dx) + tuple(out[kind][n].reshape(SHARD_SHAPES[n]) for kind in "gdmv" for n in WEIGHTS)
```

```python
import functools
import math

import numpy as np
import jax
import jax.numpy as jnp
from jax import lax
from jax.experimental import pallas as pl
from jax.experimental.pallas import tpu as pltpu

F32 = jnp.float32
BF16 = jnp.bfloat16

D_MODEL = 1024
E_A = 1024
CONV_WIDTH = 3
N_HEADS = 8
QK_NOPE = 64
QK_ROPE = 32
V_HEAD = 64
KV_RANK = 256
Q_RANK = 384
E_B = N_HEADS * V_HEAD
ROPE_THETA = 10000.0
SOFTMAX_SCALE = 1.0 / math.sqrt(QK_NOPE + QK_ROPE)
LOG2E = math.log2(math.e)
LN2 = math.log(2.0)
Q_PRESCALE = SOFTMAX_SCALE * LOG2E
EPS = 1e-6
HEAD_PAD = 128
QK_PAD = N_HEADS * HEAD_PAD
ROPE_LO = QK_NOPE
ROPE_HALF = QK_ROPE // 2
KR_PAD = KV_RANK + HEAD_PAD

ADAM_LR = 0.001
ADAM_B1 = 0.9
ADAM_B2 = 0.999
ADAM_EPS = 1e-08
ADAM_WD = 0.01
ADAM_STEP = 10

VMEM_LIMIT = 56 * 1024 * 1024
ROW_TILE = 512
CONV_BWD_TILE = 256
ATT_TILE_FWD = 1024
ATT_TILE = 512
LANES = 128
PACK_W = 1024

N_CHIPS = 4
N_DEV = 8


def _dot(a, b):
    return jnp.dot(a, b, preferred_element_type=F32)


def _dot_nt(a, b):
    return lax.dot_general(a, b, (((1,), (1,)), ((), ())), preferred_element_type=F32)


def _dot_tn(a, b):
    return lax.dot_general(a, b, (((0,), (0,)), ((), ())), preferred_element_type=F32)


def _rms(x):
    r = lax.rsqrt(jnp.mean(x * x, axis=-1, keepdims=True) + EPS)
    return r, x * r


def _rms_bwd(dxh, xh, r):
    return r * (dxh - xh * jnp.mean(dxh * xh, axis=-1, keepdims=True))


def _rope_fwd(a, c, s1, s2):
    return a * c + pltpu.roll(a, HEAD_PAD - ROPE_HALF, 1) * s1 + pltpu.roll(a, ROPE_HALF, 1) * s2


def _rope_bwd(g, c, s1, s2):
    return g * c + pltpu.roll(g * s1, ROPE_HALF, 1) + pltpu.roll(g * s2, HEAD_PAD - ROPE_HALF, 1)


def _sigmoid(x):
    return 1.0 / (1.0 + jnp.exp(-x))


def _row_spec(tm, n):
    return pl.BlockSpec((tm, n), lambda i: (i, 0))


def _const_spec(shape):
    nd = len(shape)
    return pl.BlockSpec(shape, lambda i: (0,) * nd, pipeline_mode=pl.Buffered(1))


def _acc_spec(shape):
    nd = len(shape)
    return pl.BlockSpec(shape, lambda i: (0,) * nd, pipeline_mode=pl.Buffered(1))


def _params(*sem):
    return pltpu.CompilerParams(dimension_semantics=sem, vmem_limit_bytes=VMEM_LIMIT)


MESH = pl.DeviceIdType.MESH
HBM = pl.BlockSpec(memory_space=pl.ANY)
WHOLE = pl.BlockSpec(memory_space=pltpu.VMEM)
FLIPS = ((1, 0), (0, 1), (1, 1))


def _place():
    return lax.axis_index("x"), lax.axis_index("y"), lax.axis_index("c")


def _remote(src, dst, send, recv, peer):
    return pltpu.make_async_remote_copy(src_ref=src, dst_ref=dst, send_sem=send, recv_sem=recv, device_id=peer,
                                        device_id_type=MESH)


def _comm_params():
    return pltpu.CompilerParams(vmem_limit_bytes=VMEM_LIMIT)


def _rope_consts():
    lane = np.arange(HEAD_PAD)
    first = (lane >= ROPE_LO) & (lane < ROPE_LO + ROPE_HALF)
    second = (lane >= ROPE_LO + ROPE_HALF) & (lane < ROPE_LO + QK_ROPE)
    f = np.where(first, lane - ROPE_LO, np.where(second, lane - ROPE_LO - ROPE_HALF, 0))
    inv = np.float32(ROPE_THETA) ** (-(2 * f).astype(np.float32) / np.float32(QK_ROPE))
    out = np.zeros((8, HEAD_PAD), np.float32)
    out[0] = inv
    out[1] = first
    out[2] = second
    out[3] = lane < ROPE_LO
    return jnp.asarray(out)


def _gather_plan(w, sm, outs, osm, bf, sems):
    n = len(w)
    send_i, recv_i, send_d, recv_d, send_s, recv_s, local = sems
    x, y, c = _place()
    j0 = 2 * x + y
    sibling = (x, y, 1 - c)
    own = [pltpu.make_async_copy(bf[k], outs[k].at[j0], local.at[k]) for k in range(n)]
    if sm is not None:
        own.append(pltpu.make_async_copy(sm, osm.at[j0], local.at[n]))

    def half(k, cc):
        h = w[k].shape[0] // 2
        return pl.ds(pl.multiple_of(cc * h, 16), h)

    sends, arrivals, forwards, fwaits = [], [], [], []
    for i, (fx, fy) in enumerate(FLIPS):
        px, py = x ^ fx, y ^ fy
        pj = 2 * px + py
        for k in range(n):
            s = i * n + k
            sends.append(_remote(bf[k].at[half(k, c)], outs[k].at[j0, half(k, c)], send_i.at[s], recv_i.at[s],
                                 (px, py, c)))
            arrivals.append(_remote(bf[k].at[half(k, c)], outs[k].at[pj, half(k, c)], send_i.at[s], recv_i.at[s],
                                    (px, py, c)))
            forwards.append(_remote(outs[k].at[pj, half(k, c)], outs[k].at[pj, half(k, c)], send_d.at[s],
                                    recv_d.at[s], sibling))
            fwaits.append(_remote(outs[k].at[pj, half(k, 1 - c)], outs[k].at[pj, half(k, 1 - c)], send_d.at[s],
                                  recv_d.at[s], sibling))
        if sm is not None:
            sends.append(_remote(sm, osm.at[j0], send_s.at[i], recv_s.at[i], (px, py, c)))
            fwaits.append(_remote(sm, osm.at[pj], send_s.at[i], recv_s.at[i], (px, py, c)))
    return own, sends, arrivals, forwards, fwaits


def _gather_begin(w, bf, plan):
    own, sends, _, _, _ = plan
    for k in range(len(w)):
        bf[k][...] = w[k][...].astype(BF16)
    for cp in own + sends:
        cp.start()


def _gather_end(plan):
    own, sends, arrivals, forwards, fwaits = plan
    for cp, fwd in zip(arrivals, forwards):
        cp.wait_recv()
        fwd.start()
    for cp in fwaits:
        cp.wait_recv()
    for cp in sends + forwards:
        cp.wait_send()
    for cp in own:
        cp.wait()


def _gather_sems(n, with_small):
    return ([pltpu.SemaphoreType.DMA((3 * n,))] * 4 + [pltpu.SemaphoreType.DMA((3,))] * 2
            + [pltpu.SemaphoreType.DMA((n + (1 if with_small else 0),))])


def _rope_tables_gather(pos_col, w_in, small):
    t = pos_col.shape[0]
    tm = min(ROW_TILE, t)
    steps = t // tm

    def body(p_ref, k_ref, w_ref, sm_ref, c_ref, s1_ref, s2_ref, out_ref, osm_ref, bf_ref, *sems):
        plan = _gather_plan([w_ref], sm_ref, [out_ref], osm_ref, [bf_ref], sems)
        i = pl.program_id(0)

        @pl.when(i == 0)
        def _():
            _gather_begin([w_ref], [bf_ref], plan)

        inv, first, second, nope = k_ref[0:1, :], k_ref[1:2, :], k_ref[2:3, :], k_ref[3:4, :]
        ang = p_ref[...].astype(F32) * inv
        cs, sn = jnp.cos(ang), jnp.sin(ang)
        c_ref[...] = cs * (first + second) + nope
        s1_ref[...] = -sn * first
        s2_ref[...] = sn * second

        @pl.when(i == steps - 1)
        def _():
            _gather_end(plan)

    out = jax.ShapeDtypeStruct((t, HEAD_PAD), F32)
    return pl.pallas_call(
        body, grid=(steps,), name="rope_tables_gather",
        in_specs=[_row_spec(tm, 1), _const_spec((8, HEAD_PAD)), WHOLE, WHOLE],
        out_specs=[_row_spec(tm, HEAD_PAD)] * 3 + [HBM, HBM],
        out_shape=[out] * 3 + [jax.ShapeDtypeStruct((N_CHIPS,) + w_in.shape, BF16),
                               jax.ShapeDtypeStruct((N_CHIPS,) + small.shape, small.dtype)],
        scratch_shapes=[pltpu.VMEM(w_in.shape, BF16)] + _gather_sems(1, True),
        compiler_params=_params("arbitrary"),
    )(pos_col, _rope_consts(), w_in, small)


def _shift_down(v, prev, row):
    p1, p2 = prev[7:8, :], prev[6:7, :]
    v1 = jnp.where(row == 0, p1, pltpu.roll(v, 1, 0))
    v2 = jnp.where(row == 0, p2, jnp.where(row == 1, p1, pltpu.roll(v, 2, 0)))
    return v1, v2


def _conv_fwd(x, seq, ga, w_in4, cw, shards):
    t = x.shape[0]
    tm = min(ROW_TILE, seq)
    tiles_per_seq = seq // tm
    steps = t // tm
    n = len(shards)

    def body(x_ref, ga_ref, w_ref, cw_ref, *rest):
        sh, (b_ref, c_ref, u_ref, g_ref, ym_ref) = rest[:n], rest[n:n + 5]
        outs, carry_ref, bf, sems = rest[n + 5:2 * n + 5], rest[2 * n + 5], rest[2 * n + 6:3 * n + 6], rest[3 * n + 6:]
        plan = _gather_plan(sh, None, outs, None, bf, sems)
        i = pl.program_id(0)

        @pl.when(i == 0)
        def _():
            _gather_begin(sh, bf, plan)

        @pl.when(i % tiles_per_seq == 0)
        def _():
            carry_ref[...] = jnp.zeros_like(carry_ref)

        _, xh = _rms(x_ref[...])
        h = (xh * ga_ref[...]).astype(BF16)
        b, c, u, g = (_dot(h, w_ref[j]) for j in range(4))
        v = c * u
        row = lax.broadcasted_iota(jnp.int32, (tm, 1), 0)
        v1, v2 = _shift_down(v, carry_ref[...], row)
        carry_ref[...] = v[tm - 8:tm, :]
        cv = cw_ref[2:3, :] * v + cw_ref[1:2, :] * v1 + cw_ref[0:1, :] * v2
        b_ref[...] = b.astype(BF16)
        c_ref[...] = c.astype(BF16)
        u_ref[...] = u.astype(BF16)
        g_ref[...] = g.astype(BF16)
        ym_ref[...] = (g * _sigmoid(g) * b * cv).astype(BF16)

        @pl.when(i == steps - 1)
        def _():
            _gather_end(plan)

    out = jax.ShapeDtypeStruct((t, E_A), BF16)
    res = pl.pallas_call(
        body, grid=(steps,), name="conv_fwd",
        in_specs=[_row_spec(tm, D_MODEL), _const_spec((1, D_MODEL)), _const_spec((4, D_MODEL, E_A)),
                  _const_spec((8, E_A))] + [WHOLE] * n,
        out_specs=[_row_spec(tm, E_A)] * 5 + [HBM] * n,
        out_shape=[out] * 5 + [jax.ShapeDtypeStruct((N_CHIPS,) + a.shape, BF16) for a in shards],
        scratch_shapes=[pltpu.VMEM((8, E_A), F32)] + [pltpu.VMEM(a.shape, BF16) for a in shards]
        + _gather_sems(n, False),
        compiler_params=_params("arbitrary"),
    )(x, ga, w_in4, cw, *shards)
    return res[:5], res[5:]


def _mid_fwd(x, ym, w_out, gk, gb, w_dkv, gc, w_uk, w_uv, w_bin, gq, w_uq, rc, rs1, rs2):
    t = x.shape[0]
    tm = min(ROW_TILE, t)

    def body(x_ref, ym_ref, wo_ref, gk_ref, gb_ref, wd_ref, gc_ref, wk_ref, wv_ref, wi_ref, gq_ref, wq_ref,
             c_ref, s1_ref, s2_ref, x1_ref, q_ref, k_ref, v_ref, gate_ref, cq_ref, ckv_ref):
        cb, s1b, s2b = c_ref[...], s1_ref[...], s2_ref[...]
        x1 = x_ref[...] + _dot(ym_ref[...], wo_ref[...])
        x1_ref[...] = x1
        _, xh = _rms(x1)
        hk = (xh * gk_ref[...]).astype(BF16)
        h1 = (xh * gb_ref[...]).astype(BF16)

        ckr = _dot(hk, wd_ref[...])
        ckv_raw = ckr[:, :KV_RANK]
        ckv_ref[...] = ckv_raw.astype(BF16)
        _, ch = _rms(ckv_raw)
        ckv = (ch * gc_ref[...]).astype(BF16)
        kr = _rope_fwd(ckr[:, KV_RANK:], cb, s1b, s2b)
        kn = _dot(ckv, wk_ref[...])
        for h in range(N_HEADS):
            sl = slice(h * HEAD_PAD, (h + 1) * HEAD_PAD)
            k_ref[:, sl] = (kn[:, sl] + kr).astype(BF16)
        v_ref[...] = _dot(ckv, wv_ref[...]).astype(BF16)

        pb = _dot(h1, wi_ref[...])
        cq = pb[:, :Q_RANK]
        cq_ref[...] = cq.astype(BF16)
        gate_ref[...] = pb[:, Q_RANK:].astype(BF16)
        _, cqh = _rms(cq)
        q = _dot((cqh * gq_ref[...]).astype(BF16), wq_ref[...])
        for h in range(N_HEADS):
            sl = slice(h * HEAD_PAD, (h + 1) * HEAD_PAD)
            q_ref[:, sl] = (_rope_fwd(q[:, sl], cb, s1b, s2b) * Q_PRESCALE).astype(BF16)

    def sds(n, dt):
        return jax.ShapeDtypeStruct((t, n), dt)

    return pl.pallas_call(
        body, grid=(t // tm,), name="mid_fwd",
        in_specs=[_row_spec(tm, D_MODEL), _row_spec(tm, E_A), _const_spec((E_A, D_MODEL)),
                  _const_spec((1, D_MODEL)), _const_spec((1, D_MODEL)), _const_spec((D_MODEL, KR_PAD)),
                  _const_spec((1, KV_RANK)), _const_spec((KV_RANK, QK_PAD)), _const_spec((KV_RANK, E_B)),
                  _const_spec((D_MODEL, Q_RANK + E_B)), _const_spec((1, Q_RANK)), _const_spec((Q_RANK, QK_PAD)),
                  _row_spec(tm, HEAD_PAD), _row_spec(tm, HEAD_PAD), _row_spec(tm, HEAD_PAD)],
        out_specs=[_row_spec(tm, D_MODEL), _row_spec(tm, QK_PAD), _row_spec(tm, QK_PAD), _row_spec(tm, E_B),
                   _row_spec(tm, E_B), _row_spec(tm, Q_RANK), _row_spec(tm, KV_RANK)],
        out_shape=[sds(D_MODEL, F32), sds(QK_PAD, BF16), sds(QK_PAD, BF16), sds(E_B, BF16), sds(E_B, BF16),
                   sds(Q_RANK, BF16), sds(KV_RANK, BF16)],
        compiler_params=_params("parallel"),
    )(x, ym, w_out, gk, gb, w_dkv, gc, w_uk, w_uv, w_bin, gq, w_uq, rc, rs1, rs2)


def _pair_specs(seq):
    qk = pl.BlockSpec((seq, 2 * HEAD_PAD), lambda b, p: (b, p))
    vo = pl.BlockSpec((seq, 2 * V_HEAD), lambda b, p: (b, p))
    st = pl.BlockSpec((None, 2, seq), lambda b, p: (p, 0, b))
    return qk, vo, st


def _attn_fwd(q, k, v, seq):
    t = q.shape[0]
    tq = min(ATT_TILE_FWD, seq)
    nq = seq // tq

    def body(q_ref, k_ref, v_ref, o_ref, lse_ref, m_scr, l_scr, acc_scr):
        row = lax.broadcasted_iota(jnp.int32, (tq, tq), 0)
        col = lax.broadcasted_iota(jnp.int32, (tq, tq), 1)
        lane = lax.broadcasted_iota(jnp.int32, (tq, 2 * V_HEAD), 1)

        def q_step(qi, _):
            q0 = pl.multiple_of(qi * tq, tq)
            m_scr[...] = jnp.full(m_scr.shape, -jnp.inf, F32)
            l_scr[...] = jnp.zeros_like(l_scr)
            acc_scr[...] = jnp.zeros_like(acc_scr)

            def k_step(ki, _, masked):
                k0 = pl.multiple_of(ki * tq, tq)
                vt = v_ref[pl.ds(k0, tq), :]
                for hh in range(2):
                    hs = slice(hh * HEAD_PAD, (hh + 1) * HEAD_PAD)
                    s = _dot_nt(q_ref[pl.ds(q0, tq), hs], k_ref[pl.ds(k0, tq), hs])
                    if masked:
                        s = jnp.where(col <= row, s, -jnp.inf)
                    m_old = m_scr[hh]
                    m_new = jnp.maximum(m_old, jnp.max(s, axis=-1, keepdims=True))
                    alpha = jnp.exp2(m_old - m_new)
                    ps = [jnp.exp2(s[:, j * LANES:(j + 1) * LANES] - m_new) for j in range(tq // LANES)]
                    l_scr[hh] = alpha * l_scr[hh] + functools.reduce(lambda a, b: a + b, ps)
                    p = jnp.concatenate(ps, axis=-1).astype(BF16)
                    acc_scr[hh] = alpha * acc_scr[hh] + _dot(p, vt)
                    m_scr[hh] = m_new
                return 0

            lax.fori_loop(0, qi, functools.partial(k_step, masked=False), 0)
            k_step(qi, 0, True)
            l0 = jnp.sum(l_scr[0], axis=-1, keepdims=True)
            l1 = jnp.sum(l_scr[1], axis=-1, keepdims=True)
            o_ref[pl.ds(q0, tq), :] = jnp.where(lane < V_HEAD, acc_scr[0] / l0, acc_scr[1] / l1).astype(BF16)
            stats = jnp.where(lane == 0, m_scr[0] + jnp.log2(l0), m_scr[1] + jnp.log2(l1)).T
            lse_ref[:, pl.ds(q0, tq)] = stats[0:2, :]
            return 0

        lax.fori_loop(0, nq, q_step, 0)

    qk, vo, st = _pair_specs(seq)
    return pl.pallas_call(
        body, grid=(t // seq, N_HEADS // 2), name="attn_fwd",
        in_specs=[qk, qk, vo], out_specs=[vo, st],
        out_shape=[jax.ShapeDtypeStruct((t, E_B), BF16), jax.ShapeDtypeStruct((N_HEADS // 2, 2, t), F32)],
        scratch_shapes=[pltpu.VMEM((2, tq, LANES), F32), pltpu.VMEM((2, tq, LANES), F32),
                        pltpu.VMEM((2, tq, 2 * V_HEAD), F32)],
        compiler_params=_params("parallel", "parallel"),
    )(q, k, v)


def _head_fwd_bwd(o, gate, x1, tgt, w_bout, gf):
    t = o.shape[0]
    tm = min(ROW_TILE, t)

    def body(o_ref, gate_ref, x1_ref, tgt_ref, w_ref, gf_ref,
             dx2_ref, do_ref, dgate_ref, dd_ref, loss_ref, dgf_ref, dw_ref):
        @pl.when(pl.program_id(0) == 0)
        def _():
            loss_ref[...] = jnp.zeros_like(loss_ref)
            dgf_ref[...] = jnp.zeros_like(dgf_ref)
            dw_ref[...] = jnp.zeros_like(dw_ref)

        o = o_ref[...].astype(F32)
        gt = gate_ref[...].astype(F32)
        sg = _sigmoid(gt)
        silu = gt * sg
        z = (o * silu).astype(BF16)
        x2 = x1_ref[...] + _dot(z, w_ref[...])
        r2, xh2 = _rms(x2)
        gf = gf_ref[...]
        err = xh2 * gf - tgt_ref[...]
        loss_ref[...] += 0.5 * jnp.sum(jnp.mean(err * err, axis=-1, keepdims=True))
        dy = err * (1.0 / D_MODEL)
        dgf_ref[...] += jnp.sum(dy * xh2, axis=0, keepdims=True)
        dx2 = _rms_bwd(dy * gf, xh2, r2)
        dx2_ref[...] = dx2
        dx2b = dx2.astype(BF16)
        dw_ref[...] += _dot_tn(z, dx2b)
        dz = _dot_nt(dx2b, w_ref[...])
        do = dz * silu
        do_ref[...] = do.astype(BF16)
        dgate_ref[...] = (dz * o * (sg * (1.0 + gt * (1.0 - sg)))).astype(BF16)
        prod = do * o
        lane = lax.broadcasted_iota(jnp.int32, (tm, 2 * V_HEAD), 1)
        cols = jnp.zeros((tm, LANES), F32)
        for p in range(N_HEADS // 2):
            blk = prod[:, p * 2 * V_HEAD:(p + 1) * 2 * V_HEAD]
            d0 = jnp.sum(jnp.where(lane < V_HEAD, blk, 0.0), axis=-1, keepdims=True)
            d1 = jnp.sum(jnp.where(lane < V_HEAD, 0.0, blk), axis=-1, keepdims=True)
            cols = jnp.where(lane == 2 * p, d0, jnp.where(lane == 2 * p + 1, d1, cols))
        rows = cols.T
        for h in range(N_HEADS):
            dd_ref[h // 2, h % 2:h % 2 + 1, :] = rows[h:h + 1, :]

    return pl.pallas_call(
        body, grid=(t // tm,), name="head_fwd_bwd",
        in_specs=[_row_spec(tm, E_B), _row_spec(tm, E_B), _row_spec(tm, D_MODEL), _row_spec(tm, D_MODEL),
                  _const_spec((E_B, D_MODEL)), _const_spec((1, D_MODEL))],
        out_specs=[_row_spec(tm, D_MODEL), _row_spec(tm, E_B), _row_spec(tm, E_B),
                   pl.BlockSpec((N_HEADS // 2, 2, tm), lambda i: (0, 0, i)),
                   _acc_spec((1, 1)), _acc_spec((1, D_MODEL)), _acc_spec((E_B, D_MODEL))],
        out_shape=[jax.ShapeDtypeStruct((t, D_MODEL), F32), jax.ShapeDtypeStruct((t, E_B), BF16),
                   jax.ShapeDtypeStruct((t, E_B), BF16), jax.ShapeDtypeStruct((N_HEADS // 2, 2, t), F32),
                   jax.ShapeDtypeStruct((1, 1), F32), jax.ShapeDtypeStruct((1, D_MODEL), F32),
                   jax.ShapeDtypeStruct((E_B, D_MODEL), F32)],
        compiler_params=_params("arbitrary"),
    )(o, gate, x1, tgt, w_bout, gf)


def _attn_bwd(q, k, v, do, lse, dd, seq):
    t = q.shape[0]
    tq = min(ATT_TILE, seq)
    nq = seq // tq

    def body(q_ref, k_ref, v_ref, do_ref, lse_ref, dd_ref, dq_ref, dk_ref, dv_ref, dq_acc, dk_acc, dv_acc):
        krow = lax.broadcasted_iota(jnp.int32, (tq, tq), 0)
        qcol = lax.broadcasted_iota(jnp.int32, (tq, tq), 1)
        lane = lax.broadcasted_iota(jnp.int32, (tq, 2 * V_HEAD), 1)
        dq_acc[...] = jnp.zeros_like(dq_acc)

        def k_step(ki, _):
            k0 = pl.multiple_of(ki * tq, tq)
            vt = v_ref[pl.ds(k0, tq), :]
            dk_acc[...] = jnp.zeros_like(dk_acc)
            dv_acc[...] = jnp.zeros_like(dv_acc)

            def q_step(qi, _, masked):
                q0 = pl.multiple_of(qi * tq, tq)
                do_pair = do_ref[pl.ds(q0, tq), :]
                for hh in range(2):
                    hs = slice(hh * HEAD_PAD, (hh + 1) * HEAD_PAD)
                    kt = k_ref[pl.ds(k0, tq), hs]
                    qt = q_ref[pl.ds(q0, tq), hs]
                    mine = (lane < V_HEAD) if hh == 0 else (lane >= V_HEAD)
                    do_h = jnp.where(mine, do_pair, jnp.zeros((), BF16))
                    st = _dot_nt(kt, qt)
                    if masked:
                        st = jnp.where(krow <= qcol, st, -jnp.inf)
                    pt = jnp.exp2(st - lse_ref[hh:hh + 1, pl.ds(q0, tq)])
                    dpt = _dot_nt(vt, do_h)
                    dst = (pt * (dpt - dd_ref[hh:hh + 1, pl.ds(q0, tq)])).astype(BF16)
                    dv_acc[...] += _dot(pt.astype(BF16), do_h)
                    dk_acc[:, hs] += _dot(dst, qt)
                    dq_acc[pl.ds(q0, tq), hs] += _dot_tn(dst, kt)
                return 0

            q_step(ki, 0, True)
            lax.fori_loop(ki + 1, nq, functools.partial(q_step, masked=False), 0)
            dk_ref[pl.ds(k0, tq), :] = (dk_acc[...] * LN2).astype(BF16)
            dv_ref[pl.ds(k0, tq), :] = dv_acc[...].astype(BF16)
            return 0

        lax.fori_loop(0, nq, k_step, 0)
        dq_ref[...] = (dq_acc[...] * SOFTMAX_SCALE).astype(BF16)

    qk, vo, st = _pair_specs(seq)
    return pl.pallas_call(
        body, grid=(t // seq, N_HEADS // 2), name="attn_bwd",
        in_specs=[qk, qk, vo, vo, st, st], out_specs=[qk, qk, vo],
        out_shape=[jax.ShapeDtypeStruct((t, QK_PAD), BF16), jax.ShapeDtypeStruct((t, QK_PAD), BF16),
                   jax.ShapeDtypeStruct((t, E_B), BF16)],
        scratch_shapes=[pltpu.VMEM((seq, 2 * HEAD_PAD), F32), pltpu.VMEM((tq, 2 * HEAD_PAD), F32),
                        pltpu.VMEM((tq, 2 * V_HEAD), F32)],
        compiler_params=_params("parallel", "parallel"),
    )(q, k, v, do, lse, dd)


def _mid_bwd(dq, dk, dv, dgate, dx2, x1, cq, ckv, rc, rs1, rs2, w_uq, w_bin, w_uk, w_uv, w_dkv, gq, gc, gk, gb):
    t = dq.shape[0]
    tm = min(ROW_TILE, t)

    def body(dq_ref, dk_ref, dv_ref, dgate_ref, dx2_ref, x1_ref, cq_ref, ckv_ref, c_ref, s1_ref, s2_ref,
             wq_ref, wi_ref, wk_ref, wv_ref, wd_ref, gq_ref, gc_ref, gk_ref, gb_ref,
             dx1_ref, dwq_ref, dwi_ref, dwk_ref, dwv_ref, dwd_ref, dgq_ref, dgc_ref, dgk_ref, dgb_ref):
        @pl.when(pl.program_id(0) == 0)
        def _():
            for ref in (dwq_ref, dwi_ref, dwk_ref, dwv_ref, dwd_ref, dgq_ref, dgc_ref, dgk_ref, dgb_ref):
                ref[...] = jnp.zeros_like(ref)

        cb, s1b, s2b = c_ref[...], s1_ref[...], s2_ref[...]
        r1, xh = _rms(x1_ref[...])
        gk, gb, gq, gc = gk_ref[...], gb_ref[...], gq_ref[...], gc_ref[...]
        hk = (xh * gk).astype(BF16)
        h1 = (xh * gb).astype(BF16)

        dqs = [_rope_bwd(dq_ref[:, h * HEAD_PAD:(h + 1) * HEAD_PAD].astype(F32), cb, s1b, s2b)
               for h in range(N_HEADS)]
        dqb = jnp.concatenate(dqs, axis=-1).astype(BF16)
        rq, cqh = _rms(cq_ref[...].astype(F32))
        dwq_ref[...] += _dot_tn((cqh * gq).astype(BF16), dqb)
        dcqn = _dot_nt(dqb, wq_ref[...])
        dgq_ref[...] += jnp.sum(dcqn * cqh, axis=0, keepdims=True)
        dcq = _rms_bwd(dcqn * gq, cqh, rq)
        dpb = jnp.concatenate([dcq.astype(BF16), dgate_ref[...]], axis=-1)
        dwi_ref[...] += _dot_tn(h1, dpb)
        dh1 = _dot_nt(dpb, wi_ref[...])

        rcv, ch = _rms(ckv_ref[...].astype(F32))
        ckvn = (ch * gc).astype(BF16)
        dkb, dvb = dk_ref[...], dv_ref[...]
        dwk_ref[...] += _dot_tn(ckvn, dkb)
        dwv_ref[...] += _dot_tn(ckvn, dvb)
        dckv = _dot_nt(dkb, wk_ref[...]) + _dot_nt(dvb, wv_ref[...])
        dgc_ref[...] += jnp.sum(dckv * ch, axis=0, keepdims=True)
        dckv_raw = _rms_bwd(dckv * gc, ch, rcv)
        dkr = dk_ref[:, 0:HEAD_PAD].astype(F32)
        for h in range(1, N_HEADS):
            dkr = dkr + dk_ref[:, h * HEAD_PAD:(h + 1) * HEAD_PAD].astype(F32)
        dkr = _rope_bwd(dkr, cb, s1b, s2b)
        dckr = jnp.concatenate([dckv_raw, dkr], axis=-1).astype(BF16)
        dwd_ref[...] += _dot_tn(hk, dckr)
        dhk = _dot_nt(dckr, wd_ref[...])

        dgb_ref[...] += jnp.sum(dh1 * xh, axis=0, keepdims=True)
        dgk_ref[...] += jnp.sum(dhk * xh, axis=0, keepdims=True)
        dx1_ref[...] = dx2_ref[...] + _rms_bwd(dh1 * gb + dhk * gk, xh, r1)

    acc_shapes = [(Q_RANK, QK_PAD), (D_MODEL, Q_RANK + E_B), (KV_RANK, QK_PAD), (KV_RANK, E_B), (D_MODEL, KR_PAD),
                  (1, Q_RANK), (1, KV_RANK), (1, D_MODEL), (1, D_MODEL)]
    return pl.pallas_call(
        body, grid=(t // tm,), name="mid_bwd",
        in_specs=[_row_spec(tm, QK_PAD), _row_spec(tm, QK_PAD), _row_spec(tm, E_B), _row_spec(tm, E_B),
                  _row_spec(tm, D_MODEL), _row_spec(tm, D_MODEL), _row_spec(tm, Q_RANK), _row_spec(tm, KV_RANK),
                  _row_spec(tm, HEAD_PAD), _row_spec(tm, HEAD_PAD), _row_spec(tm, HEAD_PAD),
                  _const_spec((Q_RANK, QK_PAD)), _const_spec((D_MODEL, Q_RANK + E_B)),
                  _const_spec((KV_RANK, QK_PAD)), _const_spec((KV_RANK, E_B)), _const_spec((D_MODEL, KR_PAD)),
                  _const_spec((1, Q_RANK)), _const_spec((1, KV_RANK)), _const_spec((1, D_MODEL)),
                  _const_spec((1, D_MODEL))],
        out_specs=[_row_spec(tm, D_MODEL)] + [_acc_spec(s) for s in acc_shapes],
        out_shape=[jax.ShapeDtypeStruct((t, D_MODEL), F32)] + [jax.ShapeDtypeStruct(s, F32) for s in acc_shapes],
        compiler_params=_params("arbitrary"),
    )(dq, dk, dv, dgate, dx2, x1, cq, ckv, rc, rs1, rs2, w_uq, w_bin, w_uk, w_uv, w_dkv, gq, gc, gk, gb)


def _conv_bwd(dx1, x, b, c, u, g, seq, w_out, w_in4, ga, cw):
    t = x.shape[0]
    tm = min(CONV_BWD_TILE, seq)
    tiles_per_seq = seq // tm
    n = t // tm
    halo = tm // 8

    def tile(i):
        return n - 1 - i

    def rev(width):
        return pl.BlockSpec((tm, width), lambda i: (tile(i), 0))

    def prev8(width):
        return pl.BlockSpec((8, width), lambda i: (jnp.maximum(tile(i) * halo - 1, 0), 0))

    def body(dx1_ref, x_ref, b_ref, c_ref, u_ref, g_ref, cp_ref, up_ref, wo_ref, wi_ref, ga_ref, cw_ref,
             dx_ref, dwi_ref, dwo_ref, dcw_ref, dga_ref, carry_ref):
        i = pl.program_id(0)
        j = tile(i)

        @pl.when(i == 0)
        def _():
            for ref in (dwi_ref, dwo_ref, dcw_ref, dga_ref):
                ref[...] = jnp.zeros_like(ref)

        @pl.when(j % tiles_per_seq == tiles_per_seq - 1)
        def _():
            carry_ref[...] = jnp.zeros_like(carry_ref)

        dx1 = dx1_ref[...]
        dx1b = dx1.astype(BF16)
        b, c, u, g = (r[...].astype(F32) for r in (b_ref, c_ref, u_ref, g_ref))
        v = c * u
        first = (j % tiles_per_seq == 0).astype(F32)
        vprev = cp_ref[...].astype(F32) * up_ref[...].astype(F32) * (1.0 - first)
        row = lax.broadcasted_iota(jnp.int32, (tm, 1), 0)
        v1, v2 = _shift_down(v, vprev, row)
        w0, w1, w2 = cw_ref[0:1, :], cw_ref[1:2, :], cw_ref[2:3, :]
        cv = w2 * v + w1 * v1 + w0 * v2
        sg = _sigmoid(g)
        silu = g * sg
        ym = (silu * b * cv).astype(BF16)
        dwo_ref[...] += _dot_tn(ym, dx1b)
        dym = _dot_nt(dx1b, wo_ref[...])
        db = dym * silu * cv
        dcv = dym * silu * b
        dg = dym * b * cv * (sg * (1.0 + g * (1.0 - sg)))

        nxt = carry_ref[...]
        n0, n1 = nxt[0:1, :], nxt[1:2, :]
        d1 = jnp.where(row == tm - 1, n0, pltpu.roll(dcv, tm - 1, 0))
        d2 = jnp.where(row == tm - 1, n1, jnp.where(row == tm - 2, n0, pltpu.roll(dcv, tm - 2, 0)))
        carry_ref[...] = dcv[0:8, :]
        dv = w2 * dcv + w1 * d1 + w0 * d2
        dcw_ref[0:1, :] += jnp.sum(dcv * v2, axis=0, keepdims=True)
        dcw_ref[1:2, :] += jnp.sum(dcv * v1, axis=0, keepdims=True)
        dcw_ref[2:3, :] += jnp.sum(dcv * v, axis=0, keepdims=True)

        r0, xh = _rms(x_ref[...])
        ga = ga_ref[...]
        h = (xh * ga).astype(BF16)
        dh = jnp.zeros((tm, D_MODEL), F32)
        for idx, dpart in enumerate((db, dv * u, dv * c, dg)):
            dpb = dpart.astype(BF16)
            dwi_ref[idx] += _dot_tn(h, dpb)
            dh = dh + _dot_nt(dpb, wi_ref[idx])
        dga_ref[...] += jnp.sum(dh * xh, axis=0, keepdims=True)
        dx_ref[...] = dx1 + _rms_bwd(dh * ga, xh, r0)

    acc_shapes = [(4, D_MODEL, E_A), (E_A, D_MODEL), (8, E_A), (1, D_MODEL)]
    return pl.pallas_call(
        body, grid=(n,), name="conv_bwd",
        in_specs=[rev(D_MODEL), rev(D_MODEL), rev(E_A), rev(E_A), rev(E_A), rev(E_A), prev8(E_A), prev8(E_A),
                  _const_spec((E_A, D_MODEL)), _const_spec((4, D_MODEL, E_A)), _const_spec((1, D_MODEL)),
                  _const_spec((8, E_A))],
        out_specs=[rev(D_MODEL)] + [_acc_spec(s) for s in acc_shapes],
        out_shape=[jax.ShapeDtypeStruct((t, D_MODEL), F32)] + [jax.ShapeDtypeStruct(s, F32) for s in acc_shapes],
        scratch_shapes=[pltpu.VMEM((8, E_A), F32)],
        compiler_params=_params("arbitrary"),
    )(dx1, x, b, c, u, g, c, u, w_out, w_in4, ga, cw)


WEIGHTS = ("a_norm", "a_w_in", "a_conv", "a_w_out", "kv_norm", "w_dkv", "ckv_norm", "w_ukv", "b_norm", "b_w_in",
           "b_q_norm", "b_w_uq", "b_w_out", "final_norm")
SHARD_SHAPES = {
    "a_norm": (1, 256), "a_w_in": (1, 1024, 1024), "a_conv": (1, 3, 256), "a_w_out": (1, 256, 1024),
    "kv_norm": (1024,), "w_dkv": (256, 288), "ckv_norm": (256,), "w_ukv": (256, 256), "b_norm": (1, 1024),
    "b_w_in": (1, 256, 896), "b_q_norm": (1, 384), "b_w_uq": (1, 384, 192), "b_w_out": (1, 512, 256),
    "final_norm": (1024,),
}
MATS = ("a_w_in", "a_w_out", "w_dkv", "w_ukv", "b_w_in", "b_w_uq", "b_w_out")
SMALL = ("a_norm", "a_conv", "kv_norm", "ckv_norm", "b_norm", "b_q_norm", "final_norm")
SMALL_FULL = {"a_norm": 1024, "a_conv": 3072, "kv_norm": 1024, "ckv_norm": 256, "b_norm": 1024, "b_q_norm": 384,
              "final_norm": 1024}
SMALL_ROWS = 8
LOSS_SLOT = sum(SMALL_FULL.values())


def _mat2d(name, a):
    return a.reshape(SHARD_SHAPES[name][-2:])


def _prep_first(g_win, gsmall):
    sm = gsmall.reshape(N_CHIPS, -1)
    a_conv = jnp.transpose(sm[:, 256:1024].reshape(N_CHIPS, CONV_WIDTH, 256), (1, 0, 2)).reshape(CONV_WIDTH, -1)
    return {"w_in4": g_win, "ga": sm[:, :256].reshape(1, -1), "cw": jnp.pad(a_conv, ((0, 8 - CONV_WIDTH), (0, 0)))}


def _prep_rest(gath, w):
    def cols(a):
        return jnp.transpose(a, (1, 0, 2)).reshape(a.shape[1], -1)

    def pad_heads(a, width):
        a = a.reshape(a.shape[0], N_HEADS, width)
        return jnp.pad(a, ((0, 0), (0, 0), (0, HEAD_PAD - width))).reshape(a.shape[0], QK_PAD)

    row = lambda a: a.reshape(1, -1).astype(F32)
    w_dkv = gath["w_dkv"].reshape(D_MODEL, KV_RANK + QK_ROPE)
    w_ukv = cols(gath["w_ukv"]).reshape(KV_RANK, N_HEADS, 2, QK_NOPE)
    return {
        "w_out": gath["a_w_out"].reshape(E_A, D_MODEL),
        "w_dkv": jnp.concatenate([w_dkv[:, :KV_RANK], jnp.zeros((D_MODEL, ROPE_LO), BF16), w_dkv[:, KV_RANK:],
                                  jnp.zeros((D_MODEL, HEAD_PAD - ROPE_LO - QK_ROPE), BF16)], axis=1),
        "w_uk": pad_heads(w_ukv[:, :, 0, :].reshape(KV_RANK, N_HEADS * QK_NOPE), QK_NOPE),
        "w_uv": w_ukv[:, :, 1, :].reshape(KV_RANK, E_B),
        "w_bin": gath["b_w_in"].reshape(D_MODEL, Q_RANK + E_B),
        "w_uq": pad_heads(cols(gath["b_w_uq"]), QK_NOPE + QK_ROPE),
        "w_bout": cols(gath["b_w_out"]),
        "gk": row(w["kv_norm"]), "gc": row(w["ckv_norm"]), "gb": row(w["b_norm"]),
        "gq": row(w["b_q_norm"]), "gf": row(w["final_norm"]),
    }


GRAD_MATS = ("w_in4", "w_out", "w_dkv", "w_uk", "w_uv", "w_bin", "w_uq", "w_bout")
GRAD_KIND = {"w_in4": "lead", "w_out": "row", "w_dkv": "row", "w_uk": "col", "w_uv": "col", "w_bin": "row",
             "w_uq": "col", "w_bout": "col"}


def _local_step(x, positions, tgt, w):
    bsz, seq, _ = x.shape
    t = bsz * seq
    x2d = x.reshape(t, D_MODEL)
    small = jnp.concatenate([w["a_norm"].reshape(-1), w["a_conv"].reshape(-1)]).reshape(8, LANES)
    rc, rs1, rs2, g_win, gsmall = _rope_tables_gather(positions.reshape(t, 1), _mat2d(MATS[0], w[MATS[0]]), small)
    wk = _prep_first(g_win, gsmall)
    (b, c, u, g, ym), gathered = _conv_fwd(x2d, seq, wk["ga"], wk["w_in4"], wk["cw"],
                                           [_mat2d(n, w[n]) for n in MATS[1:]])
    wk.update(_prep_rest(dict(zip(MATS[1:], gathered)), w))
    x1, q, k, v, gate, cq, ckv = _mid_fwd(x2d, ym, wk["w_out"], wk["gk"], wk["gb"], wk["w_dkv"], wk["gc"], wk["w_uk"],
                                          wk["w_uv"], wk["w_bin"], wk["gq"], wk["w_uq"], rc, rs1, rs2)
    o, lse = _attn_fwd(q, k, v, seq)
    dx2, do, dgate, dd, loss, dgf, dw_bout = _head_fwd_bwd(o, gate, x1, tgt.reshape(t, D_MODEL), wk["w_bout"], wk["gf"])
    dq, dk, dv = _attn_bwd(q, k, v, do, lse, dd, seq)
    dx1, dwq, dw_bin, dwk, dwv, dwd, dgq, dgc, dgk, dgb = _mid_bwd(
        dq, dk, dv, dgate, dx2, x1, cq, ckv, rc, rs1, rs2, wk["w_uq"], wk["w_bin"], wk["w_uk"], wk["w_uv"], wk["w_dkv"],
        wk["gq"], wk["gc"], wk["gk"], wk["gb"])
    dx, dw_in4, dw_out, dcw, dga = _conv_bwd(dx1, x2d, b, c, u, g, seq, wk["w_out"], wk["w_in4"], wk["ga"], wk["cw"])
    mats = {"w_in4": dw_in4, "w_out": dw_out, "w_dkv": dwd, "w_uk": dwk, "w_uv": dwv, "w_bin": dw_bin, "w_uq": dwq,
            "w_bout": dw_bout}
    small = {"a_norm": dga, "a_conv": dcw[:CONV_WIDTH], "kv_norm": dgk, "ckv_norm": dgc, "b_norm": dgb,
             "b_q_norm": dgq, "final_norm": dgf}
    return loss[0, 0], dx.reshape(bsz, seq, D_MODEL), mats, small


def _shard_grads(sh, svec):
    j0 = 2 * lax.axis_index("x") + lax.axis_index("y")
    flat = svec.reshape(-1)
    off, small = 0, {}
    for n in SMALL:
        small[n] = flat[off:off + SMALL_FULL[n]]
        off += SMALL_FULL[n]
    dwd, dwk, dwv, dwq = sh["w_dkv"], sh["w_uk"], sh["w_uv"], sh["w_uq"]
    w_ukv = jnp.stack([dwk.reshape(KV_RANK, 2, HEAD_PAD)[:, :, :QK_NOPE], dwv.reshape(KV_RANK, 2, V_HEAD)], axis=2)
    return {
        "a_norm": lax.dynamic_slice(small["a_norm"], (j0 * 256,), (256,)),
        "a_conv": lax.dynamic_slice(small["a_conv"].reshape(CONV_WIDTH, E_A), (0, j0 * 256), (CONV_WIDTH, 256)),
        "kv_norm": small["kv_norm"], "ckv_norm": small["ckv_norm"], "b_norm": small["b_norm"],
        "b_q_norm": small["b_q_norm"], "final_norm": small["final_norm"],
        "a_w_in": sh["w_in4"], "a_w_out": sh["w_out"],
        "w_dkv": jnp.concatenate([dwd[:, :KV_RANK], dwd[:, KV_RANK + ROPE_LO:KV_RANK + ROPE_LO + QK_ROPE]], axis=1),
        "w_ukv": w_ukv.reshape(KV_RANK, 2 * (QK_NOPE + V_HEAD)),
        "b_w_in": sh["w_bin"],
        "b_w_uq": dwq.reshape(Q_RANK, 2, HEAD_PAD)[:, :, :QK_NOPE + QK_ROPE].reshape(Q_RANK, -1),
        "b_w_out": sh["w_bout"],
    }


def _sub(ref, kind, j, cc):
    if kind == "lead":
        h = ref.shape[1] // 2
        return ref.at[j, pl.ds(pl.multiple_of(cc * h, 8), h), :]
    if kind == "row":
        rows = ref.shape[0] // N_CHIPS
        h = rows // 2
        return ref.at[pl.ds(pl.multiple_of(j * rows + cc * h, 8), h), :]
    cols = ref.shape[1] // N_CHIPS
    h = ref.shape[0] // 2
    return ref.at[pl.ds(pl.multiple_of(cc * h, 8), h), pl.ds(j * cols, cols)]


def _sub_shape(shape, kind):
    if kind == "lead":
        return (shape[1] // 2, shape[2])
    if kind == "row":
        return (shape[0] // N_CHIPS // 2, shape[1])
    return (shape[0] // 2, shape[1] // N_CHIPS)


def _pair_exchange(grads, kinds):
    n = len(grads)

    def body(*refs):
        g, out = refs[:n], refs[n:2 * n]
        send, recv = refs[2 * n:]
        x, y, c = _place()
        sibling = (x, y, 1 - c)
        copies = [_remote(_sub(g[k], kinds[k], j, 1 - c), out[k].at[j], send.at[k], recv.at[k], sibling)
                  for k in range(n) for j in range(N_CHIPS)]
        for cp in copies:
            cp.start()
        for k in range(n):
            done = _remote(out[k], out[k], send.at[k], recv.at[k], sibling)
            done.wait_recv()
            done.wait_send()

    return pl.pallas_call(
        body, name="pair_exchange",
        in_specs=[HBM] * n, out_specs=[HBM] * n,
        out_shape=[jax.ShapeDtypeStruct((N_CHIPS,) + _sub_shape(a.shape, kd), F32) for a, kd in zip(grads, kinds)],
        scratch_shapes=[pltpu.SemaphoreType.DMA((n,)), pltpu.SemaphoreType.DMA((n,))],
    )(*grads)


def _chip_partials(grads, theirs, kinds):
    n = len(grads)

    def body(*refs):
        g, other, out, mine = refs[:n], refs[n:2 * n], refs[2 * n:3 * n], refs[3 * n:4 * n]
        sem = refs[4 * n]
        c = lax.axis_index("c")
        for k in range(n):
            for j in range(N_CHIPS):
                pltpu.make_async_copy(_sub(g[k], kinds[k], j, c), mine[k].at[j], sem.at[k]).start()
        for k in range(n):
            pltpu.make_async_copy(mine[k], mine[k], sem.at[k]).wait()
            out[k][...] = (mine[k][...] + other[k][...]).astype(BF16)

    shapes = [(N_CHIPS,) + _sub_shape(a.shape, kd) for a, kd in zip(grads, kinds)]
    return pl.pallas_call(
        body, name="chip_partials",
        in_specs=[HBM] * n + [WHOLE] * n, out_specs=[WHOLE] * n,
        out_shape=[jax.ShapeDtypeStruct(s, BF16) for s in shapes],
        scratch_shapes=[pltpu.VMEM(s, F32) for s in shapes] + [pltpu.SemaphoreType.DMA((n,))],
        compiler_params=_comm_params(),
    )(*grads, *theirs)


def _chip_exchange(parts):
    n = len(parts)

    def body(*refs):
        p, out = refs[:n], refs[n:2 * n]
        send, recv, local = refs[2 * n:]
        x, y, c = _place()
        j0 = 2 * x + y
        own = [pltpu.make_async_copy(p[k].at[j0], out[k].at[j0], local.at[k]) for k in range(n)]
        for cp in own:
            cp.start()
        sends, waits = [], []
        for i, (fx, fy) in enumerate(FLIPS):
            px, py = x ^ fx, y ^ fy
            pj = 2 * px + py
            for k in range(n):
                s = i * n + k
                sends.append(_remote(p[k].at[pj], out[k].at[j0], send.at[s], recv.at[s], (px, py, c)))
                waits.append(_remote(p[k].at[pj], out[k].at[pj], send.at[s], recv.at[s], (px, py, c)))
        for cp in sends:
            cp.start()
        for cp in waits:
            cp.wait_recv()
        for cp in sends:
            cp.wait_send()
        for cp in own:
            cp.wait()

    return pl.pallas_call(
        body, name="chip_exchange",
        in_specs=[HBM] * n, out_specs=[HBM] * n,
        out_shape=[jax.ShapeDtypeStruct(a.shape, a.dtype) for a in parts],
        scratch_shapes=[pltpu.SemaphoreType.DMA((3 * n,)), pltpu.SemaphoreType.DMA((3 * n,)),
                        pltpu.SemaphoreType.DMA((n,))],
    )(*parts)


def _reduce_swap(parts):
    n = len(parts)

    def body(*refs):
        p, out, red = refs[:n], refs[n:2 * n], refs[2 * n:3 * n]
        send, recv, local = refs[3 * n:]
        x, y, c = _place()
        sibling = (x, y, 1 - c)
        own, give, take = [], [], []
        for k in range(n):
            red[k][...] = ((p[k][0].astype(F32) + p[k][1].astype(F32)) + p[k][2].astype(F32)) + p[k][3].astype(F32)
            own.append(pltpu.make_async_copy(red[k], out[k].at[c], local.at[k]))
            give.append(_remote(red[k], out[k].at[c], send.at[k], recv.at[k], sibling))
            take.append(_remote(red[k], out[k].at[1 - c], send.at[k], recv.at[k], sibling))
            own[k].start()
            give[k].start()
        for k in range(n):
            take[k].wait_recv()
            give[k].wait_send()
            own[k].wait()

    return pl.pallas_call(
        body, name="reduce_swap",
        in_specs=[WHOLE] * n, out_specs=[HBM] * n,
        out_shape=[jax.ShapeDtypeStruct((2,) + a.shape[1:], F32) for a in parts],
        scratch_shapes=[pltpu.VMEM(a.shape[1:], F32) for a in parts]
        + [pltpu.SemaphoreType.DMA((n,))] * 3,
        compiler_params=_comm_params(),
    )(*parts)


def _allreduce_small(vec):
    def body(v_ref, o_ref, got, send, recv):
        x, y, c = _place()
        me = 4 * x + 2 * y + c
        got[me] = v_ref[...]
        sends, waits = [], []
        for d in range(1, N_DEV):
            px, py, pc = x ^ (d >> 2), y ^ ((d >> 1) & 1), c ^ (d & 1)
            peer = 4 * px + 2 * py + pc
            sends.append(_remote(v_ref, got.at[me], send.at[d - 1], recv.at[d - 1], (px, py, pc)))
            waits.append(_remote(v_ref, got.at[peer], send.at[d - 1], recv.at[d - 1], (px, py, pc)))
        for cp in sends:
            cp.start()
        for cp in waits:
            cp.wait_recv()
        for cp in sends:
            cp.wait_send()
        acc = got[0]
        for k in range(1, N_DEV):
            acc = acc + got[k]
        o_ref[...] = acc

    return pl.pallas_call(
        body, name="allreduce_small",
        in_specs=[WHOLE], out_specs=WHOLE, out_shape=jax.ShapeDtypeStruct(vec.shape, vec.dtype),
        scratch_shapes=[pltpu.VMEM((N_DEV,) + vec.shape, vec.dtype), pltpu.SemaphoreType.DMA((N_DEV - 1,)),
                        pltpu.SemaphoreType.DMA((N_DEV - 1,))],
    )(vec)


def _adamw_math(w, g, m, v):
    m = ADAM_B1 * m + (1.0 - ADAM_B1) * g
    v = ADAM_B2 * v + (1.0 - ADAM_B2) * (g * g)
    m_hat = m / (1.0 - ADAM_B1 ** ADAM_STEP)
    v_hat = v / (1.0 - ADAM_B2 ** ADAM_STEP)
    return -ADAM_LR * (m_hat / (jnp.sqrt(v_hat) + ADAM_EPS) + ADAM_WD * w), m, v


def _adamw_tiled(w, g, m, v):
    rows, width = w.shape
    tm = rows // 4

    def body(w_ref, g_ref, m_ref, v_ref, d_ref, mo_ref, vo_ref):
        d_ref[...], mo_ref[...], vo_ref[...] = _adamw_math(w_ref[...], g_ref[...], m_ref[...], v_ref[...])

    spec = pl.BlockSpec((tm, width), lambda i: (i, 0))
    out = jax.ShapeDtypeStruct((rows, width), F32)
    return pl.pallas_call(
        body, grid=(rows // tm,), name="adamw_tiled",
        in_specs=[spec] * 4, out_specs=[spec] * 3, out_shape=[out] * 3,
        compiler_params=_params("parallel"),
    )(w, g, m, v)


def _adamw_many(ws, gs, ms, vs):
    n = len(ws)

    def body(*refs):
        for k in range(n):
            w_ref, g_ref, m_ref, v_ref = (refs[i * n + k] for i in range(4))
            d_ref, mo_ref, vo_ref = (refs[(4 + i) * n + k] for i in range(3))
            d_ref[...], mo_ref[...], vo_ref[...] = _adamw_math(w_ref[...], g_ref[...], m_ref[...], v_ref[...])

    outs = [jax.ShapeDtypeStruct(a.shape, F32) for a in ws]
    res = pl.pallas_call(
        body, name="adamw_many",
        in_specs=[WHOLE] * (4 * n), out_specs=[WHOLE] * (3 * n), out_shape=outs * 3,
        compiler_params=_comm_params(),
    )(*ws, *gs, *ms, *vs)
    return res[:n], res[n:2 * n], res[2 * n:]


def kernel(x, positions, a_norm, a_w_in, a_conv, a_w_out, kv_norm, w_dkv, ckv_norm, w_ukv, b_norm, b_w_in, b_q_norm, b_w_uq, b_w_out, final_norm, loss_target, m_a_norm, m_a_w_in, m_a_conv, m_a_w_out, m_kv_norm, m_w_dkv, m_ckv_norm, m_w_ukv, m_b_norm, m_b_w_in, m_b_q_norm, m_b_w_uq, m_b_w_out, m_final_norm, v_a_norm, v_a_w_in, v_a_conv, v_a_w_out, v_kv_norm, v_w_dkv, v_ckv_norm, v_w_ukv, v_b_norm, v_b_w_in, v_b_q_norm, v_b_w_uq, v_b_w_out, v_final_norm):
    w = dict(a_norm=a_norm, a_w_in=a_w_in, a_conv=a_conv, a_w_out=a_w_out, kv_norm=kv_norm, w_dkv=w_dkv,
             ckv_norm=ckv_norm, w_ukv=w_ukv, b_norm=b_norm, b_w_in=b_w_in, b_q_norm=b_q_norm, b_w_uq=b_w_uq,
             b_w_out=b_w_out, final_norm=final_norm)
    m = dict(a_norm=m_a_norm, a_w_in=m_a_w_in, a_conv=m_a_conv, a_w_out=m_a_w_out, kv_norm=m_kv_norm, w_dkv=m_w_dkv,
             ckv_norm=m_ckv_norm, w_ukv=m_w_ukv, b_norm=m_b_norm, b_w_in=m_b_w_in, b_q_norm=m_b_q_norm,
             b_w_uq=m_b_w_uq, b_w_out=m_b_w_out, final_norm=m_final_norm)
    v = dict(a_norm=v_a_norm, a_w_in=v_a_w_in, a_conv=v_a_conv, a_w_out=v_a_w_out, kv_norm=v_kv_norm, w_dkv=v_w_dkv,
             ckv_norm=v_ckv_norm, w_ukv=v_w_ukv, b_norm=v_b_norm, b_w_in=v_b_w_in, b_q_norm=v_b_q_norm,
             b_w_uq=v_b_w_uq, b_w_out=v_b_w_out, final_norm=v_final_norm)

    loss, dx, gmat, gsmall = _local_step(x, positions, loss_target, w)

    kinds = [GRAD_KIND[n] for n in GRAD_MATS]
    grads = [gmat[n] for n in GRAD_MATS]
    theirs = _pair_exchange(grads, kinds)
    mine = _reduce_swap(_chip_exchange(_chip_partials(grads, theirs, kinds)))
    flat = jnp.concatenate([gsmall[n].reshape(-1) for n in SMALL] + [loss.reshape(1)])
    svec = _allreduce_small(jnp.pad(flat, (0, SMALL_ROWS * PACK_W - flat.shape[0])).reshape(SMALL_ROWS, PACK_W))
    loss = svec.reshape(-1)[LOSS_SLOT]
    g = _shard_grads({n: a.reshape(-1, a.shape[-1]) for n, a in zip(GRAD_MATS, mine)}, svec)

    two_d = lambda n, a: a.reshape((-1, a.shape[-1]))
    big = "a_w_in"
    rest = [n for n in WEIGHTS if n != big]
    d_big, m_big, v_big = _adamw_tiled(*(two_d(big, t[big]) for t in (w, g, m, v)))
    d_r, m_r, v_r = _adamw_many(*([two_d(n, t[n]) for n in rest] for t in (w, g, m, v)))
    out = {"g": g, "d": dict(zip(rest, d_r)), "m": dict(zip(rest, m_r)), "v": dict(zip(rest, v_r))}
    out["d"][big], out["m"][big], out["v"][big] = d_big, m_big, v_big
    return (loss, dx) + tuple(out[kind][n].reshape(SHARD_SHAPES[n]) for kind in "gdmv" for n in WEIGHTS)
```

```python
import functools
import math

import numpy as np
import jax
import jax.numpy as jnp
from jax import lax
from jax.experimental import pallas as pl
from jax.experimental.pallas import tpu as pltpu

F32 = jnp.float32
BF16 = jnp.bfloat16

D_MODEL = 1024
E_A = 1024
CONV_WIDTH = 3
N_HEADS = 8
QK_NOPE = 64
QK_ROPE = 32
V_HEAD = 64
KV_RANK = 256
Q_RANK = 384
E_B = N_HEADS * V_HEAD
ROPE_THETA = 10000.0
SOFTMAX_SCALE = 1.0 / math.sqrt(QK_NOPE + QK_ROPE)
LOG2E = math.log2(math.e)
LN2 = math.log(2.0)
Q_PRESCALE = SOFTMAX_SCALE * LOG2E
EPS = 1e-6
HEAD_PAD = 128
QK_PAD = N_HEADS * HEAD_PAD
ROPE_LO = QK_NOPE
ROPE_HALF = QK_ROPE // 2
KR_PAD = KV_RANK + HEAD_PAD

ADAM_LR = 0.001
ADAM_B1 = 0.9
ADAM_B2 = 0.999
ADAM_EPS = 1e-08
ADAM_WD = 0.01
ADAM_STEP = 10

VMEM_LIMIT = 56 * 1024 * 1024
ROW_TILE = 512
CONV_BWD_TILE = 256
ATT_TILE_FWD = 1024
ATT_TILE = 512
LANES = 128
PACK_W = 1024

N_CHIPS = 4
N_DEV = 8


def _dot(a, b):
    return jnp.dot(a, b, preferred_element_type=F32)


def _dot_nt(a, b):
    return lax.dot_general(a, b, (((1,), (1,)), ((), ())), preferred_element_type=F32)


def _dot_tn(a, b):
    return lax.dot_general(a, b, (((0,), (0,)), ((), ())), preferred_element_type=F32)


def _rms(x):
    r = lax.rsqrt(jnp.mean(x * x, axis=-1, keepdims=True) + EPS)
    return r, x * r


def _rms_bwd(dxh, xh, r):
    return r * (dxh - xh * jnp.mean(dxh * xh, axis=-1, keepdims=True))


def _rope_fwd(a, c, s1, s2):
    return a * c + pltpu.roll(a, HEAD_PAD - ROPE_HALF, 1) * s1 + pltpu.roll(a, ROPE_HALF, 1) * s2


def _rope_bwd(g, c, s1, s2):
    return g * c + pltpu.roll(g * s1, ROPE_HALF, 1) + pltpu.roll(g * s2, HEAD_PAD - ROPE_HALF, 1)


def _sigmoid(x):
    return 1.0 / (1.0 + jnp.exp(-x))


def _row_spec(tm, n):
    return pl.BlockSpec((tm, n), lambda i: (i, 0))


def _const_spec(shape):
    nd = len(shape)
    return pl.BlockSpec(shape, lambda i: (0,) * nd, pipeline_mode=pl.Buffered(1))


def _acc_spec(shape):
    nd = len(shape)
    return pl.BlockSpec(shape, lambda i: (0,) * nd, pipeline_mode=pl.Buffered(1))


def _params(*sem):
    return pltpu.CompilerParams(dimension_semantics=sem, vmem_limit_bytes=VMEM_LIMIT)


MESH = pl.DeviceIdType.MESH
HBM = pl.BlockSpec(memory_space=pl.ANY)
WHOLE = pl.BlockSpec(memory_space=pltpu.VMEM)
FLIPS = ((1, 0), (0, 1), (1, 1))


def _place():
    return lax.axis_index("x"), lax.axis_index("y"), lax.axis_index("c")


def _remote(src, dst, send, recv, peer):
    return pltpu.make_async_remote_copy(src_ref=src, dst_ref=dst, send_sem=send, recv_sem=recv, device_id=peer,
                                        device_id_type=MESH)


def _comm_params():
    return pltpu.CompilerParams(vmem_limit_bytes=VMEM_LIMIT)


def _rope_consts():
    lane = np.arange(HEAD_PAD)
    first = (lane >= ROPE_LO) & (lane < ROPE_LO + ROPE_HALF)
    second = (lane >= ROPE_LO + ROPE_HALF) & (lane < ROPE_LO + QK_ROPE)
    f = np.where(first, lane - ROPE_LO, np.where(second, lane - ROPE_LO - ROPE_HALF, 0))
    inv = np.float32(ROPE_THETA) ** (-(2 * f).astype(np.float32) / np.float32(QK_ROPE))
    out = np.zeros((8, HEAD_PAD), np.float32)
    out[0] = inv
    out[1] = first
    out[2] = second
    out[3] = lane < ROPE_LO
    return jnp.asarray(out)


def _gather_plan(w, sm, outs, osm, bf, sems):
    n = len(w)
    send_i, recv_i, send_d, recv_d, send_s, recv_s, local = sems
    x, y, c = _place()
    j0 = 2 * x + y
    sibling = (x, y, 1 - c)
    own = [pltpu.make_async_copy(bf[k], outs[k].at[j0], local.at[k]) for k in range(n)]
    if sm is not None:
        own.append(pltpu.make_async_copy(sm, osm.at[j0], local.at[n]))

    def half(k, cc):
        h = w[k].shape[0] // 2
        return pl.ds(pl.multiple_of(cc * h, 16), h)

    sends, arrivals, forwards, fwaits = [], [], [], []
    for i, (fx, fy) in enumerate(FLIPS):
        px, py = x ^ fx, y ^ fy
        pj = 2 * px + py
        for k in range(n):
            s = i * n + k
            sends.append(_remote(bf[k].at[half(k, c)], outs[k].at[j0, half(k, c)], send_i.at[s], recv_i.at[s],
                                 (px, py, c)))
            arrivals.append(_remote(bf[k].at[half(k, c)], outs[k].at[pj, half(k, c)], send_i.at[s], recv_i.at[s],
                                    (px, py, c)))
            forwards.append(_remote(outs[k].at[pj, half(k, c)], outs[k].at[pj, half(k, c)], send_d.at[s],
                                    recv_d.at[s], sibling))
            fwaits.append(_remote(outs[k].at[pj, half(k, 1 - c)], outs[k].at[pj, half(k, 1 - c)], send_d.at[s],
                                  recv_d.at[s], sibling))
        if sm is not None:
            sends.append(_remote(sm, osm.at[j0], send_s.at[i], recv_s.at[i], (px, py, c)))
            fwaits.append(_remote(sm, osm.at[pj], send_s.at[i], recv_s.at[i], (px, py, c)))
    return own, sends, arrivals, forwards, fwaits


def _gather_begin(w, bf, plan):
    own, sends, _, _, _ = plan
    for k in range(len(w)):
        bf[k][...] = w[k][...].astype(BF16)
    for cp in own + sends:
        cp.start()


def _gather_end(plan):
    own, sends, arrivals, forwards, fwaits = plan
    for cp, fwd in zip(arrivals, forwards):
        cp.wait_recv()
        fwd.start()
    for cp in fwaits:
        cp.wait_recv()
    for cp in sends + forwards:
        cp.wait_send()
    for cp in own:
        cp.wait()


def _gather_sems(n, with_small):
    return ([pltpu.SemaphoreType.DMA((3 * n,))] * 4 + [pltpu.SemaphoreType.DMA((3,))] * 2
            + [pltpu.SemaphoreType.DMA((n + (1 if with_small else 0),))])


def _rope_tables_gather(pos_col, w_in, small):
    t = pos_col.shape[0]
    tm = min(ROW_TILE, t)
    steps = t // tm

    def body(p_ref, k_ref, w_ref, sm_ref, c_ref, s1_ref, s2_ref, out_ref, osm_ref, bf_ref, *sems):
        plan = _gather_plan([w_ref], sm_ref, [out_ref], osm_ref, [bf_ref], sems)
        i = pl.program_id(0)

        @pl.when(i == 0)
        def _():
            _gather_begin([w_ref], [bf_ref], plan)

        inv, first, second, nope = k_ref[0:1, :], k_ref[1:2, :], k_ref[2:3, :], k_ref[3:4, :]
        ang = p_ref[...].astype(F32) * inv
        cs, sn = jnp.cos(ang), jnp.sin(ang)
        c_ref[...] = cs * (first + second) + nope
        s1_ref[...] = -sn * first
        s2_ref[...] = sn * second

        @pl.when(i == steps - 1)
        def _():
            _gather_end(plan)

    out = jax.ShapeDtypeStruct((t, HEAD_PAD), F32)
    return pl.pallas_call(
        body, grid=(steps,), name="rope_tables_gather",
        in_specs=[_row_spec(tm, 1), _const_spec((8, HEAD_PAD)), WHOLE, WHOLE],
        out_specs=[_row_spec(tm, HEAD_PAD)] * 3 + [HBM, HBM],
        out_shape=[out] * 3 + [jax.ShapeDtypeStruct((N_CHIPS,) + w_in.shape, BF16),
                               jax.ShapeDtypeStruct((N_CHIPS,) + small.shape, small.dtype)],
        scratch_shapes=[pltpu.VMEM(w_in.shape, BF16)] + _gather_sems(1, True),
        compiler_params=_params("arbitrary"),
    )(pos_col, _rope_consts(), w_in, small)


def _shift_down(v, prev, row):
    p1, p2 = prev[7:8, :], prev[6:7, :]
    v1 = jnp.where(row == 0, p1, pltpu.roll(v, 1, 0))
    v2 = jnp.where(row == 0, p2, jnp.where(row == 1, p1, pltpu.roll(v, 2, 0)))
    return v1, v2


def _conv_fwd(x, seq, ga, w_in4, cw, shards):
    t = x.shape[0]
    tm = min(ROW_TILE, seq)
    tiles_per_seq = seq // tm
    steps = t // tm
    n = len(shards)

    def body(x_ref, ga_ref, w_ref, cw_ref, *rest):
        sh, (b_ref, c_ref, u_ref, g_ref, ym_ref) = rest[:n], rest[n:n + 5]
        outs, carry_ref, bf, sems = rest[n + 5:2 * n + 5], rest[2 * n + 5], rest[2 * n + 6:3 * n + 6], rest[3 * n + 6:]
        plan = _gather_plan(sh, None, outs, None, bf, sems)
        i = pl.program_id(0)

        @pl.when(i == 0)
        def _():
            _gather_begin(sh, bf, plan)

        @pl.when(i % tiles_per_seq == 0)
        def _():
            carry_ref[...] = jnp.zeros_like(carry_ref)

        _, xh = _rms(x_ref[...])
        h = (xh * ga_ref[...]).astype(BF16)
        b, c, u, g = (_dot(h, w_ref[j]) for j in range(4))
        v = c * u
        row = lax.broadcasted_iota(jnp.int32, (tm, 1), 0)
        v1, v2 = _shift_down(v, carry_ref[...], row)
        carry_ref[...] = v[tm - 8:tm, :]
        cv = cw_ref[2:3, :] * v + cw_ref[1:2, :] * v1 + cw_ref[0:1, :] * v2
        b_ref[...] = b.astype(BF16)
        c_ref[...] = c.astype(BF16)
        u_ref[...] = u.astype(BF16)
        g_ref[...] = g.astype(BF16)
        ym_ref[...] = (g * _sigmoid(g) * b * cv).astype(BF16)

        @pl.when(i == steps - 1)
        def _():
            _gather_end(plan)

    out = jax.ShapeDtypeStruct((t, E_A), BF16)
    res = pl.pallas_call(
        body, grid=(steps,), name="conv_fwd",
        in_specs=[_row_spec(tm, D_MODEL), _const_spec((1, D_MODEL)), _const_spec((4, D_MODEL, E_A)),
                  _const_spec((8, E_A))] + [WHOLE] * n,
        out_specs=[_row_spec(tm, E_A)] * 5 + [HBM] * n,
        out_shape=[out] * 5 + [jax.ShapeDtypeStruct((N_CHIPS,) + a.shape, BF16) for a in shards],
        scratch_shapes=[pltpu.VMEM((8, E_A), F32)] + [pltpu.VMEM(a.shape, BF16) for a in shards]
        + _gather_sems(n, False),
        compiler_params=_params("arbitrary"),
    )(x, ga, w_in4, cw, *shards)
    return res[:5], res[5:]


def _mid_fwd(x, ym, w_out, gk, gb, w_dkv, gc, w_uk, w_uv, w_bin, gq, w_uq, rc, rs1, rs2):
    t = x.shape[0]
    tm = min(ROW_TILE, t)

    def body(x_ref, ym_ref, wo_ref, gk_ref, gb_ref, wd_ref, gc_ref, wk_ref, wv_ref, wi_ref, gq_ref, wq_ref,
             c_ref, s1_ref, s2_ref, x1_ref, q_ref, k_ref, v_ref, gate_ref, cq_ref, ckv_ref):
        cb, s1b, s2b = c_ref[...], s1_ref[...], s2_ref[...]
        x1 = x_ref[...] + _dot(ym_ref[...], wo_ref[...])
        x1_ref[...] = x1
        _, xh = _rms(x1)
        hk = (xh * gk_ref[...]).astype(BF16)
        h1 = (xh * gb_ref[...]).astype(BF16)

        ckr = _dot(hk, wd_ref[...])
        ckv_raw = ckr[:, :KV_RANK]
        ckv_ref[...] = ckv_raw.astype(BF16)
        _, ch = _rms(ckv_raw)
        ckv = (ch * gc_ref[...]).astype(BF16)
        kr = _rope_fwd(ckr[:, KV_RANK:], cb, s1b, s2b)
        kn = _dot(ckv, wk_ref[...])
        for h in range(N_HEADS):
            sl = slice(h * HEAD_PAD, (h + 1) * HEAD_PAD)
            k_ref[:, sl] = (kn[:, sl] + kr).astype(BF16)
        v_ref[...] = _dot(ckv, wv_ref[...]).astype(BF16)

        pb = _dot(h1, wi_ref[...])
        cq = pb[:, :Q_RANK]
        cq_ref[...] = cq.astype(BF16)
        gate_ref[...] = pb[:, Q_RANK:].astype(BF16)
        _, cqh = _rms(cq)
        q = _dot((cqh * gq_ref[...]).astype(BF16), wq_ref[...])
        for h in range(N_HEADS):
            sl = slice(h * HEAD_PAD, (h + 1) * HEAD_PAD)
            q_ref[:, sl] = (_rope_fwd(q[:, sl], cb, s1b, s2b) * Q_PRESCALE).astype(BF16)

    def sds(n, dt):
        return jax.ShapeDtypeStruct((t, n), dt)

    return pl.pallas_call(
        body, grid=(t // tm,), name="mid_fwd",
        in_specs=[_row_spec(tm, D_MODEL), _row_spec(tm, E_A), _const_spec((E_A, D_MODEL)),
                  _const_spec((1, D_MODEL)), _const_spec((1, D_MODEL)), _const_spec((D_MODEL, KR_PAD)),
                  _const_spec((1, KV_RANK)), _const_spec((KV_RANK, QK_PAD)), _const_spec((KV_RANK, E_B)),
                  _const_spec((D_MODEL, Q_RANK + E_B)), _const_spec((1, Q_RANK)), _const_spec((Q_RANK, QK_PAD)),
                  _row_spec(tm, HEAD_PAD), _row_spec(tm, HEAD_PAD), _row_spec(tm, HEAD_PAD)],
        out_specs=[_row_spec(tm, D_MODEL), _row_spec(tm, QK_PAD), _row_spec(tm, QK_PAD), _row_spec(tm, E_B),
                   _row_spec(tm, E_B), _row_spec(tm, Q_RANK), _row_spec(tm, KV_RANK)],
        out_shape=[sds(D_MODEL, F32), sds(QK_PAD, BF16), sds(QK_PAD, BF16), sds(E_B, BF16), sds(E_B, BF16),
                   sds(Q_RANK, BF16), sds(KV_RANK, BF16)],
        compiler_params=_params("parallel"),
    )(x, ym, w_out, gk, gb, w_dkv, gc, w_uk, w_uv, w_bin, gq, w_uq, rc, rs1, rs2)


def _pair_specs(seq):
    qk = pl.BlockSpec((seq, 2 * HEAD_PAD), lambda b, p: (b, p))
    vo = pl.BlockSpec((seq, 2 * V_HEAD), lambda b, p: (b, p))
    st = pl.BlockSpec((None, 2, seq), lambda b, p: (p, 0, b))
    return qk, vo, st


def _attn_fwd(q, k, v, seq):
    t = q.shape[0]
    tq = min(ATT_TILE_FWD, seq)
    nq = seq // tq

    def body(q_ref, k_ref, v_ref, o_ref, lse_ref, m_scr, l_scr, acc_scr):
        lane = lax.broadcasted_iota(jnp.int32, (tq, 2 * V_HEAD), 1)

        def q_step(qi, _):
            q0 = pl.multiple_of(qi * tq, tq)
            m_scr[...] = jnp.full(m_scr.shape, -jnp.inf, F32)
            l_scr[...] = jnp.zeros_like(l_scr)
            acc_scr[...] = jnp.zeros_like(acc_scr)

            def block(q_lo, q_n, k0, k_n, masked):
                rows = slice(q_lo, q_lo + q_n)
                vt = v_ref[pl.ds(k0, k_n), :]
                for hh in range(2):
                    hs = slice(hh * HEAD_PAD, (hh + 1) * HEAD_PAD)
                    s = _dot_nt(q_ref[pl.ds(q0 + q_lo, q_n), hs], k_ref[pl.ds(k0, k_n), hs])
                    if masked:
                        row = lax.broadcasted_iota(jnp.int32, (q_n, k_n), 0)
                        col = lax.broadcasted_iota(jnp.int32, (q_n, k_n), 1)
                        s = jnp.where(col <= row, s, -jnp.inf)
                    m_old = m_scr[hh, rows]
                    m_new = jnp.maximum(m_old, jnp.max(s, axis=-1, keepdims=True))
                    alpha = jnp.exp2(m_old - m_new)
                    ps = [jnp.exp2(s[:, j * LANES:(j + 1) * LANES] - m_new) for j in range(k_n // LANES)]
                    l_scr[hh, rows] = alpha * l_scr[hh, rows] + functools.reduce(lambda a, b: a + b, ps)
                    p = jnp.concatenate(ps, axis=-1).astype(BF16)
                    acc_scr[hh, rows] = alpha * acc_scr[hh, rows] + _dot(p, vt)
                    m_scr[hh, rows] = m_new

            def k_step(ki, _):
                block(0, tq, pl.multiple_of(ki * tq, tq), tq, False)
                return 0

            lax.fori_loop(0, qi, k_step, 0)
            half = tq // 2
            block(0, tq, q0, half, True)
            block(half, half, q0 + half, half, True)
            l0 = jnp.sum(l_scr[0], axis=-1, keepdims=True)
            l1 = jnp.sum(l_scr[1], axis=-1, keepdims=True)
            o_ref[pl.ds(q0, tq), :] = jnp.where(lane < V_HEAD, acc_scr[0] / l0, acc_scr[1] / l1).astype(BF16)
            stats = jnp.where(lane == 0, m_scr[0] + jnp.log2(l0), m_scr[1] + jnp.log2(l1)).T
            lse_ref[:, pl.ds(q0, tq)] = stats[0:2, :]
            return 0

        lax.fori_loop(0, nq, q_step, 0)

    qk, vo, st = _pair_specs(seq)
    return pl.pallas_call(
        body, grid=(t // seq, N_HEADS // 2), name="attn_fwd",
        in_specs=[qk, qk, vo], out_specs=[vo, st],
        out_shape=[jax.ShapeDtypeStruct((t, E_B), BF16), jax.ShapeDtypeStruct((N_HEADS // 2, 2, t), F32)],
        scratch_shapes=[pltpu.VMEM((2, tq, LANES), F32), pltpu.VMEM((2, tq, LANES), F32),
                        pltpu.VMEM((2, tq, 2 * V_HEAD), F32)],
        compiler_params=_params("parallel", "parallel"),
    )(q, k, v)


def _head_fwd_bwd(o, gate, x1, tgt, w_bout, gf):
    t = o.shape[0]
    tm = min(ROW_TILE, t)

    def body(o_ref, gate_ref, x1_ref, tgt_ref, w_ref, gf_ref,
             dx2_ref, do_ref, dgate_ref, dd_ref, loss_ref, dgf_ref, dw_ref):
        @pl.when(pl.program_id(0) == 0)
        def _():
            loss_ref[...] = jnp.zeros_like(loss_ref)
            dgf_ref[...] = jnp.zeros_like(dgf_ref)
            dw_ref[...] = jnp.zeros_like(dw_ref)

        o = o_ref[...].astype(F32)
        gt = gate_ref[...].astype(F32)
        sg = _sigmoid(gt)
        silu = gt * sg
        z = (o * silu).astype(BF16)
        x2 = x1_ref[...] + _dot(z, w_ref[...])
        r2, xh2 = _rms(x2)
        gf = gf_ref[...]
        err = xh2 * gf - tgt_ref[...]
        loss_ref[...] += 0.5 * jnp.sum(jnp.mean(err * err, axis=-1, keepdims=True))
        dy = err * (1.0 / D_MODEL)
        dgf_ref[...] += jnp.sum(dy * xh2, axis=0, keepdims=True)
        dx2 = _rms_bwd(dy * gf, xh2, r2)
        dx2_ref[...] = dx2
        dx2b = dx2.astype(BF16)
        dw_ref[...] += _dot_tn(z, dx2b)
        dz = _dot_nt(dx2b, w_ref[...])
        do = dz * silu
        do_ref[...] = do.astype(BF16)
        dgate_ref[...] = (dz * o * (sg * (1.0 + gt * (1.0 - sg)))).astype(BF16)
        prod = do * o
        lane = lax.broadcasted_iota(jnp.int32, (tm, 2 * V_HEAD), 1)
        cols = jnp.zeros((tm, LANES), F32)
        for p in range(N_HEADS // 2):
            blk = prod[:, p * 2 * V_HEAD:(p + 1) * 2 * V_HEAD]
            d0 = jnp.sum(jnp.where(lane < V_HEAD, blk, 0.0), axis=-1, keepdims=True)
            d1 = jnp.sum(jnp.where(lane < V_HEAD, 0.0, blk), axis=-1, keepdims=True)
            cols = jnp.where(lane == 2 * p, d0, jnp.where(lane == 2 * p + 1, d1, cols))
        rows = cols.T
        for h in range(N_HEADS):
            dd_ref[h // 2, h % 2:h % 2 + 1, :] = rows[h:h + 1, :]

    return pl.pallas_call(
        body, grid=(t // tm,), name="head_fwd_bwd",
        in_specs=[_row_spec(tm, E_B), _row_spec(tm, E_B), _row_spec(tm, D_MODEL), _row_spec(tm, D_MODEL),
                  _const_spec((E_B, D_MODEL)), _const_spec((1, D_MODEL))],
        out_specs=[_row_spec(tm, D_MODEL), _row_spec(tm, E_B), _row_spec(tm, E_B),
                   pl.BlockSpec((N_HEADS // 2, 2, tm), lambda i: (0, 0, i)),
                   _acc_spec((1, 1)), _acc_spec((1, D_MODEL)), _acc_spec((E_B, D_MODEL))],
        out_shape=[jax.ShapeDtypeStruct((t, D_MODEL), F32), jax.ShapeDtypeStruct((t, E_B), BF16),
                   jax.ShapeDtypeStruct((t, E_B), BF16), jax.ShapeDtypeStruct((N_HEADS // 2, 2, t), F32),
                   jax.ShapeDtypeStruct((1, 1), F32), jax.ShapeDtypeStruct((1, D_MODEL), F32),
                   jax.ShapeDtypeStruct((E_B, D_MODEL), F32)],
        compiler_params=_params("arbitrary"),
    )(o, gate, x1, tgt, w_bout, gf)


def _attn_bwd(q, k, v, do, lse, dd, seq):
    t = q.shape[0]
    tq = min(ATT_TILE, seq)
    nq = seq // tq

    def body(q_ref, k_ref, v_ref, do_ref, lse_ref, dd_ref, dq_ref, dk_ref, dv_ref, dq_acc, dk_acc, dv_acc):
        dq_acc[...] = jnp.zeros_like(dq_acc)

        def k_step(ki, _):
            k0 = pl.multiple_of(ki * tq, tq)
            dk_acc[...] = jnp.zeros_like(dk_acc)
            dv_acc[...] = jnp.zeros_like(dv_acc)

            def block(k_lo, k_n, q0, q_n, masked):
                rows = slice(k_lo, k_lo + k_n)
                lane = lax.broadcasted_iota(jnp.int32, (q_n, 2 * V_HEAD), 1)
                vt = v_ref[pl.ds(k0 + k_lo, k_n), :]
                do_pair = do_ref[pl.ds(q0, q_n), :]
                for hh in range(2):
                    hs = slice(hh * HEAD_PAD, (hh + 1) * HEAD_PAD)
                    kt = k_ref[pl.ds(k0 + k_lo, k_n), hs]
                    qt = q_ref[pl.ds(q0, q_n), hs]
                    mine = (lane < V_HEAD) if hh == 0 else (lane >= V_HEAD)
                    do_h = jnp.where(mine, do_pair, jnp.zeros((), BF16))
                    st = _dot_nt(kt, qt)
                    if masked:
                        krow = lax.broadcasted_iota(jnp.int32, (k_n, q_n), 0)
                        qcol = lax.broadcasted_iota(jnp.int32, (k_n, q_n), 1)
                        st = jnp.where(krow <= qcol, st, -jnp.inf)
                    pt = jnp.exp2(st - lse_ref[hh:hh + 1, pl.ds(q0, q_n)])
                    dpt = _dot_nt(vt, do_h)
                    dst = (pt * (dpt - dd_ref[hh:hh + 1, pl.ds(q0, q_n)])).astype(BF16)
                    dv_acc[rows, :] += _dot(pt.astype(BF16), do_h)
                    dk_acc[rows, hs] += _dot(dst, qt)
                    dq_acc[pl.ds(q0, q_n), hs] += _dot_tn(dst, kt)

            def q_step(qi, _):
                block(0, tq, pl.multiple_of(qi * tq, tq), tq, False)
                return 0

            half = tq // 2
            block(0, half, k0, tq, True)
            block(half, half, pl.multiple_of(k0 + half, half), half, True)
            lax.fori_loop(ki + 1, nq, q_step, 0)
            dk_ref[pl.ds(k0, tq), :] = (dk_acc[...] * LN2).astype(BF16)
            dv_ref[pl.ds(k0, tq), :] = dv_acc[...].astype(BF16)
            return 0

        lax.fori_loop(0, nq, k_step, 0)
        dq_ref[...] = (dq_acc[...] * SOFTMAX_SCALE).astype(BF16)

    qk, vo, st = _pair_specs(seq)
    return pl.pallas_call(
        body, grid=(t // seq, N_HEADS // 2), name="attn_bwd",
        in_specs=[qk, qk, vo, vo, st, st], out_specs=[qk, qk, vo],
        out_shape=[jax.ShapeDtypeStruct((t, QK_PAD), BF16), jax.ShapeDtypeStruct((t, QK_PAD), BF16),
                   jax.ShapeDtypeStruct((t, E_B), BF16)],
        scratch_shapes=[pltpu.VMEM((seq, 2 * HEAD_PAD), F32), pltpu.VMEM((tq, 2 * HEAD_PAD), F32),
                        pltpu.VMEM((tq, 2 * V_HEAD), F32)],
        compiler_params=_params("parallel", "parallel"),
    )(q, k, v, do, lse, dd)


def _mid_bwd(dq, dk, dv, dgate, dx2, x1, cq, ckv, rc, rs1, rs2, w_uq, w_bin, w_uk, w_uv, w_dkv, gq, gc, gk, gb):
    t = dq.shape[0]
    tm = min(ROW_TILE, t)

    def body(dq_ref, dk_ref, dv_ref, dgate_ref, dx2_ref, x1_ref, cq_ref, ckv_ref, c_ref, s1_ref, s2_ref,
             wq_ref, wi_ref, wk_ref, wv_ref, wd_ref, gq_ref, gc_ref, gk_ref, gb_ref,
             dx1_ref, dwq_ref, dwi_ref, dwk_ref, dwv_ref, dwd_ref, dgq_ref, dgc_ref, dgk_ref, dgb_ref):
        @pl.when(pl.program_id(0) == 0)
        def _():
            for ref in (dwq_ref, dwi_ref, dwk_ref, dwv_ref, dwd_ref, dgq_ref, dgc_ref, dgk_ref, dgb_ref):
                ref[...] = jnp.zeros_like(ref)

        cb, s1b, s2b = c_ref[...], s1_ref[...], s2_ref[...]
        r1, xh = _rms(x1_ref[...])
        gk, gb, gq, gc = gk_ref[...], gb_ref[...], gq_ref[...], gc_ref[...]
        hk = (xh * gk).astype(BF16)
        h1 = (xh * gb).astype(BF16)

        dqs = [_rope_bwd(dq_ref[:, h * HEAD_PAD:(h + 1) * HEAD_PAD].astype(F32), cb, s1b, s2b)
               for h in range(N_HEADS)]
        dqb = jnp.concatenate(dqs, axis=-1).astype(BF16)
        rq, cqh = _rms(cq_ref[...].astype(F32))
        dwq_ref[...] += _dot_tn((cqh * gq).astype(BF16), dqb)
        dcqn = _dot_nt(dqb, wq_ref[...])
        dgq_ref[...] += jnp.sum(dcqn * cqh, axis=0, keepdims=True)
        dcq = _rms_bwd(dcqn * gq, cqh, rq)
        dpb = jnp.concatenate([dcq.astype(BF16), dgate_ref[...]], axis=-1)
        dwi_ref[...] += _dot_tn(h1, dpb)
        dh1 = _dot_nt(dpb, wi_ref[...])

        rcv, ch = _rms(ckv_ref[...].astype(F32))
        ckvn = (ch * gc).astype(BF16)
        dkb, dvb = dk_ref[...], dv_ref[...]
        dwk_ref[...] += _dot_tn(ckvn, dkb)
        dwv_ref[...] += _dot_tn(ckvn, dvb)
        dckv = _dot_nt(dkb, wk_ref[...]) + _dot_nt(dvb, wv_ref[...])
        dgc_ref[...] += jnp.sum(dckv * ch, axis=0, keepdims=True)
        dckv_raw = _rms_bwd(dckv * gc, ch, rcv)
        dkr = dk_ref[:, 0:HEAD_PAD].astype(F32)
        for h in range(1, N_HEADS):
            dkr = dkr + dk_ref[:, h * HEAD_PAD:(h + 1) * HEAD_PAD].astype(F32)
        dkr = _rope_bwd(dkr, cb, s1b, s2b)
        dckr = jnp.concatenate([dckv_raw, dkr], axis=-1).astype(BF16)
        dwd_ref[...] += _dot_tn(hk, dckr)
        dhk = _dot_nt(dckr, wd_ref[...])

        dgb_ref[...] += jnp.sum(dh1 * xh, axis=0, keepdims=True)
        dgk_ref[...] += jnp.sum(dhk * xh, axis=0, keepdims=True)
        dx1_ref[...] = dx2_ref[...] + _rms_bwd(dh1 * gb + dhk * gk, xh, r1)

    acc_shapes = [(Q_RANK, QK_PAD), (D_MODEL, Q_RANK + E_B), (KV_RANK, QK_PAD), (KV_RANK, E_B), (D_MODEL, KR_PAD),
                  (1, Q_RANK), (1, KV_RANK), (1, D_MODEL), (1, D_MODEL)]
    return pl.pallas_call(
        body, grid=(t // tm,), name="mid_bwd",
        in_specs=[_row_spec(tm, QK_PAD), _row_spec(tm, QK_PAD), _row_spec(tm, E_B), _row_spec(tm, E_B),
                  _row_spec(tm, D_MODEL), _row_spec(tm, D_MODEL), _row_spec(tm, Q_RANK), _row_spec(tm, KV_RANK),
                  _row_spec(tm, HEAD_PAD), _row_spec(tm, HEAD_PAD), _row_spec(tm, HEAD_PAD),
                  _const_spec((Q_RANK, QK_PAD)), _const_spec((D_MODEL, Q_RANK + E_B)),
                  _const_spec((KV_RANK, QK_PAD)), _const_spec((KV_RANK, E_B)), _const_spec((D_MODEL, KR_PAD)),
                  _const_spec((1, Q_RANK)), _const_spec((1, KV_RANK)), _const_spec((1, D_MODEL)),
                  _const_spec((1, D_MODEL))],
        out_specs=[_row_spec(tm, D_MODEL)] + [_acc_spec(s) for s in acc_shapes],
        out_shape=[jax.ShapeDtypeStruct((t, D_MODEL), F32)] + [jax.ShapeDtypeStruct(s, F32) for s in acc_shapes],
        compiler_params=_params("arbitrary"),
    )(dq, dk, dv, dgate, dx2, x1, cq, ckv, rc, rs1, rs2, w_uq, w_bin, w_uk, w_uv, w_dkv, gq, gc, gk, gb)


def _conv_bwd(dx1, x, b, c, u, g, seq, w_out, w_in4, ga, cw):
    t = x.shape[0]
    tm = min(CONV_BWD_TILE, seq)
    tiles_per_seq = seq // tm
    n = t // tm
    halo = tm // 8

    def tile(i):
        return n - 1 - i

    def rev(width):
        return pl.BlockSpec((tm, width), lambda i: (tile(i), 0))

    def prev8(width):
        return pl.BlockSpec((8, width), lambda i: (jnp.maximum(tile(i) * halo - 1, 0), 0))

    def body(dx1_ref, x_ref, b_ref, c_ref, u_ref, g_ref, cp_ref, up_ref, wo_ref, wi_ref, ga_ref, cw_ref,
             dx_ref, dwi_ref, dwo_ref, dcw_ref, dga_ref, carry_ref):
        i = pl.program_id(0)
        j = tile(i)

        @pl.when(i == 0)
        def _():
            for ref in (dwi_ref, dwo_ref, dcw_ref, dga_ref):
                ref[...] = jnp.zeros_like(ref)

        @pl.when(j % tiles_per_seq == tiles_per_seq - 1)
        def _():
            carry_ref[...] = jnp.zeros_like(carry_ref)

        dx1 = dx1_ref[...]
        dx1b = dx1.astype(BF16)
        b, c, u, g = (r[...].astype(F32) for r in (b_ref, c_ref, u_ref, g_ref))
        v = c * u
        first = (j % tiles_per_seq == 0).astype(F32)
        vprev = cp_ref[...].astype(F32) * up_ref[...].astype(F32) * (1.0 - first)
        row = lax.broadcasted_iota(jnp.int32, (tm, 1), 0)
        v1, v2 = _shift_down(v, vprev, row)
        w0, w1, w2 = cw_ref[0:1, :], cw_ref[1:2, :], cw_ref[2:3, :]
        cv = w2 * v + w1 * v1 + w0 * v2
        sg = _sigmoid(g)
        silu = g * sg
        ym = (silu * b * cv).astype(BF16)
        dwo_ref[...] += _dot_tn(ym, dx1b)
        dym = _dot_nt(dx1b, wo_ref[...])
        db = dym * silu * cv
        dcv = dym * silu * b
        dg = dym * b * cv * (sg * (1.0 + g * (1.0 - sg)))

        nxt = carry_ref[...]
        n0, n1 = nxt[0:1, :], nxt[1:2, :]
        d1 = jnp.where(row == tm - 1, n0, pltpu.roll(dcv, tm - 1, 0))
        d2 = jnp.where(row == tm - 1, n1, jnp.where(row == tm - 2, n0, pltpu.roll(dcv, tm - 2, 0)))
        carry_ref[...] = dcv[0:8, :]
        dv = w2 * dcv + w1 * d1 + w0 * d2
        dcw_ref[0:1, :] += jnp.sum(dcv * v2, axis=0, keepdims=True)
        dcw_ref[1:2, :] += jnp.sum(dcv * v1, axis=0, keepdims=True)
        dcw_ref[2:3, :] += jnp.sum(dcv * v, axis=0, keepdims=True)

        r0, xh = _rms(x_ref[...])
        ga = ga_ref[...]
        h = (xh * ga).astype(BF16)
        dh = jnp.zeros((tm, D_MODEL), F32)
        for idx, dpart in enumerate((db, dv * u, dv * c, dg)):
            dpb = dpart.astype(BF16)
            dwi_ref[idx] += _dot_tn(h, dpb)
            dh = dh + _dot_nt(dpb, wi_ref[idx])
        dga_ref[...] += jnp.sum(dh * xh, axis=0, keepdims=True)
        dx_ref[...] = dx1 + _rms_bwd(dh * ga, xh, r0)

    acc_shapes = [(4, D_MODEL, E_A), (E_A, D_MODEL), (8, E_A), (1, D_MODEL)]
    return pl.pallas_call(
        body, grid=(n,), name="conv_bwd",
        in_specs=[rev(D_MODEL), rev(D_MODEL), rev(E_A), rev(E_A), rev(E_A), rev(E_A), prev8(E_A), prev8(E_A),
                  _const_spec((E_A, D_MODEL)), _const_spec((4, D_MODEL, E_A)), _const_spec((1, D_MODEL)),
                  _const_spec((8, E_A))],
        out_specs=[rev(D_MODEL)] + [_acc_spec(s) for s in acc_shapes],
        out_shape=[jax.ShapeDtypeStruct((t, D_MODEL), F32)] + [jax.ShapeDtypeStruct(s, F32) for s in acc_shapes],
        scratch_shapes=[pltpu.VMEM((8, E_A), F32)],
        compiler_params=_params("arbitrary"),
    )(dx1, x, b, c, u, g, c, u, w_out, w_in4, ga, cw)


WEIGHTS = ("a_norm", "a_w_in", "a_conv", "a_w_out", "kv_norm", "w_dkv", "ckv_norm", "w_ukv", "b_norm", "b_w_in",
           "b_q_norm", "b_w_uq", "b_w_out", "final_norm")
SHARD_SHAPES = {
    "a_norm": (1, 256), "a_w_in": (1, 1024, 1024), "a_conv": (1, 3, 256), "a_w_out": (1, 256, 1024),
    "kv_norm": (1024,), "w_dkv": (256, 288), "ckv_norm": (256,), "w_ukv": (256, 256), "b_norm": (1, 1024),
    "b_w_in": (1, 256, 896), "b_q_norm": (1, 384), "b_w_uq": (1, 384, 192), "b_w_out": (1, 512, 256),
    "final_norm": (1024,),
}
MATS = ("a_w_in", "a_w_out", "w_dkv", "w_ukv", "b_w_in", "b_w_uq", "b_w_out")
SMALL = ("a_norm", "a_conv", "kv_norm", "ckv_norm", "b_norm", "b_q_norm", "final_norm")
SMALL_FULL = {"a_norm": 1024, "a_conv": 3072, "kv_norm": 1024, "ckv_norm": 256, "b_norm": 1024, "b_q_norm": 384,
              "final_norm": 1024}
SMALL_ROWS = 8
LOSS_SLOT = sum(SMALL_FULL.values())


def _mat2d(name, a):
    return a.reshape(SHARD_SHAPES[name][-2:])


def _prep_first(g_win, gsmall):
    sm = gsmall.reshape(N_CHIPS, -1)
    a_conv = jnp.transpose(sm[:, 256:1024].reshape(N_CHIPS, CONV_WIDTH, 256), (1, 0, 2)).reshape(CONV_WIDTH, -1)
    return {"w_in4": g_win, "ga": sm[:, :256].reshape(1, -1), "cw": jnp.pad(a_conv, ((0, 8 - CONV_WIDTH), (0, 0)))}


def _prep_rest(gath, w):
    def cols(a):
        return jnp.transpose(a, (1, 0, 2)).reshape(a.shape[1], -1)

    def pad_heads(a, width):
        a = a.reshape(a.shape[0], N_HEADS, width)
        return jnp.pad(a, ((0, 0), (0, 0), (0, HEAD_PAD - width))).reshape(a.shape[0], QK_PAD)

    row = lambda a: a.reshape(1, -1).astype(F32)
    w_dkv = gath["w_dkv"].reshape(D_MODEL, KV_RANK + QK_ROPE)
    w_ukv = cols(gath["w_ukv"]).reshape(KV_RANK, N_HEADS, 2, QK_NOPE)
    return {
        "w_out": gath["a_w_out"].reshape(E_A, D_MODEL),
        "w_dkv": jnp.concatenate([w_dkv[:, :KV_RANK], jnp.zeros((D_MODEL, ROPE_LO), BF16), w_dkv[:, KV_RANK:],
                                  jnp.zeros((D_MODEL, HEAD_PAD - ROPE_LO - QK_ROPE), BF16)], axis=1),
        "w_uk": pad_heads(w_ukv[:, :, 0, :].reshape(KV_RANK, N_HEADS * QK_NOPE), QK_NOPE),
        "w_uv": w_ukv[:, :, 1, :].reshape(KV_RANK, E_B),
        "w_bin": gath["b_w_in"].reshape(D_MODEL, Q_RANK + E_B),
        "w_uq": pad_heads(cols(gath["b_w_uq"]), QK_NOPE + QK_ROPE),
        "w_bout": cols(gath["b_w_out"]),
        "gk": row(w["kv_norm"]), "gc": row(w["ckv_norm"]), "gb": row(w["b_norm"]),
        "gq": row(w["b_q_norm"]), "gf": row(w["final_norm"]),
    }


GRAD_MATS = ("w_in4", "w_out", "w_dkv", "w_uk", "w_uv", "w_bin", "w_uq", "w_bout")
GRAD_KIND = {"w_in4": "lead", "w_out": "row", "w_dkv": "row", "w_uk": "col", "w_uv": "col", "w_bin": "row",
             "w_uq": "col", "w_bout": "col"}


def _local_step(x, positions, tgt, w):
    bsz, seq, _ = x.shape
    t = bsz * seq
    x2d = x.reshape(t, D_MODEL)
    small = jnp.concatenate([w["a_norm"].reshape(-1), w["a_conv"].reshape(-1)]).reshape(8, LANES)
    rc, rs1, rs2, g_win, gsmall = _rope_tables_gather(positions.reshape(t, 1), _mat2d(MATS[0], w[MATS[0]]), small)
    wk = _prep_first(g_win, gsmall)
    (b, c, u, g, ym), gathered = _conv_fwd(x2d, seq, wk["ga"], wk["w_in4"], wk["cw"],
                                           [_mat2d(n, w[n]) for n in MATS[1:]])
    wk.update(_prep_rest(dict(zip(MATS[1:], gathered)), w))
    x1, q, k, v, gate, cq, ckv = _mid_fwd(x2d, ym, wk["w_out"], wk["gk"], wk["gb"], wk["w_dkv"], wk["gc"], wk["w_uk"],
                                          wk["w_uv"], wk["w_bin"], wk["gq"], wk["w_uq"], rc, rs1, rs2)
    o, lse = _attn_fwd(q, k, v, seq)
    dx2, do, dgate, dd, loss, dgf, dw_bout = _head_fwd_bwd(o, gate, x1, tgt.reshape(t, D_MODEL), wk["w_bout"], wk["gf"])
    dq, dk, dv = _attn_bwd(q, k, v, do, lse, dd, seq)
    dx1, dwq, dw_bin, dwk, dwv, dwd, dgq, dgc, dgk, dgb = _mid_bwd(
        dq, dk, dv, dgate, dx2, x1, cq, ckv, rc, rs1, rs2, wk["w_uq"], wk["w_bin"], wk["w_uk"], wk["w_uv"], wk["w_dkv"],
        wk["gq"], wk["gc"], wk["gk"], wk["gb"])
    dx, dw_in4, dw_out, dcw, dga = _conv_bwd(dx1, x2d, b, c, u, g, seq, wk["w_out"], wk["w_in4"], wk["ga"], wk["cw"])
    mats = {"w_in4": dw_in4, "w_out": dw_out, "w_dkv": dwd, "w_uk": dwk, "w_uv": dwv, "w_bin": dw_bin, "w_uq": dwq,
            "w_bout": dw_bout}
    small = {"a_norm": dga, "a_conv": dcw[:CONV_WIDTH], "kv_norm": dgk, "ckv_norm": dgc, "b_norm": dgb,
             "b_q_norm": dgq, "final_norm": dgf}
    return loss[0, 0], dx.reshape(bsz, seq, D_MODEL), mats, small


def _shard_grads(sh, svec):
    j0 = 2 * lax.axis_index("x") + lax.axis_index("y")
    flat = svec.reshape(-1)
    off, small = 0, {}
    for n in SMALL:
        small[n] = flat[off:off + SMALL_FULL[n]]
        off += SMALL_FULL[n]
    dwd, dwk, dwv, dwq = sh["w_dkv"], sh["w_uk"], sh["w_uv"], sh["w_uq"]
    w_ukv = jnp.stack([dwk.reshape(KV_RANK, 2, HEAD_PAD)[:, :, :QK_NOPE], dwv.reshape(KV_RANK, 2, V_HEAD)], axis=2)
    return {
        "a_norm": lax.dynamic_slice(small["a_norm"], (j0 * 256,), (256,)),
        "a_conv": lax.dynamic_slice(small["a_conv"].reshape(CONV_WIDTH, E_A), (0, j0 * 256), (CONV_WIDTH, 256)),
        "kv_norm": small["kv_norm"], "ckv_norm": small["ckv_norm"], "b_norm": small["b_norm"],
        "b_q_norm": small["b_q_norm"], "final_norm": small["final_norm"],
        "a_w_in": sh["w_in4"], "a_w_out": sh["w_out"],
        "w_dkv": jnp.concatenate([dwd[:, :KV_RANK], dwd[:, KV_RANK + ROPE_LO:KV_RANK + ROPE_LO + QK_ROPE]], axis=1),
        "w_ukv": w_ukv.reshape(KV_RANK, 2 * (QK_NOPE + V_HEAD)),
        "b_w_in": sh["w_bin"],
        "b_w_uq": dwq.reshape(Q_RANK, 2, HEAD_PAD)[:, :, :QK_NOPE + QK_ROPE].reshape(Q_RANK, -1),
        "b_w_out": sh["w_bout"],
    }


def _sub(ref, kind, j, cc):
    if kind == "lead":
        h = ref.shape[1] // 2
        return ref.at[j, pl.ds(pl.multiple_of(cc * h, 8), h), :]
    if kind == "row":
        rows = ref.shape[0] // N_CHIPS
        h = rows // 2
        return ref.at[pl.ds(pl.multiple_of(j * rows + cc * h, 8), h), :]
    cols = ref.shape[1] // N_CHIPS
    h = ref.shape[0] // 2
    return ref.at[pl.ds(pl.multiple_of(cc * h, 8), h), pl.ds(j * cols, cols)]


def _sub_shape(shape, kind):
    if kind == "lead":
        return (shape[1] // 2, shape[2])
    if kind == "row":
        return (shape[0] // N_CHIPS // 2, shape[1])
    return (shape[0] // 2, shape[1] // N_CHIPS)


def _pair_exchange(grads, kinds):
    n = len(grads)

    def body(*refs):
        g, out = refs[:n], refs[n:2 * n]
        send, recv = refs[2 * n:]
        x, y, c = _place()
        sibling = (x, y, 1 - c)
        copies = [_remote(_sub(g[k], kinds[k], j, 1 - c), out[k].at[j], send.at[k], recv.at[k], sibling)
                  for k in range(n) for j in range(N_CHIPS)]
        for cp in copies:
            cp.start()
        for k in range(n):
            done = _remote(out[k], out[k], send.at[k], recv.at[k], sibling)
            done.wait_recv()
            done.wait_send()

    return pl.pallas_call(
        body, name="pair_exchange",
        in_specs=[HBM] * n, out_specs=[HBM] * n,
        out_shape=[jax.ShapeDtypeStruct((N_CHIPS,) + _sub_shape(a.shape, kd), F32) for a, kd in zip(grads, kinds)],
        scratch_shapes=[pltpu.SemaphoreType.DMA((n,)), pltpu.SemaphoreType.DMA((n,))],
    )(*grads)


def _chip_partials(grads, theirs, kinds):
    n = len(grads)

    def body(*refs):
        g, other, out, mine = refs[:n], refs[n:2 * n], refs[2 * n:3 * n], refs[3 * n:4 * n]
        sem = refs[4 * n]
        c = lax.axis_index("c")
        for k in range(n):
            for j in range(N_CHIPS):
                pltpu.make_async_copy(_sub(g[k], kinds[k], j, c), mine[k].at[j], sem.at[k]).start()
        for k in range(n):
            pltpu.make_async_copy(mine[k], mine[k], sem.at[k]).wait()
            out[k][...] = (mine[k][...] + other[k][...]).astype(BF16)

    shapes = [(N_CHIPS,) + _sub_shape(a.shape, kd) for a, kd in zip(grads, kinds)]
    return pl.pallas_call(
        body, name="chip_partials",
        in_specs=[HBM] * n + [WHOLE] * n, out_specs=[WHOLE] * n,
        out_shape=[jax.ShapeDtypeStruct(s, BF16) for s in shapes],
        scratch_shapes=[pltpu.VMEM(s, F32) for s in shapes] + [pltpu.SemaphoreType.DMA((n,))],
        compiler_params=_comm_params(),
    )(*grads, *theirs)


def _chip_exchange(parts):
    n = len(parts)

    def body(*refs):
        p, out = refs[:n], refs[n:2 * n]
        send, recv, local = refs[2 * n:]
        x, y, c = _place()
        j0 = 2 * x + y
        own = [pltpu.make_async_copy(p[k].at[j0], out[k].at[j0], local.at[k]) for k in range(n)]
        for cp in own:
            cp.start()
        sends, waits = [], []
        for i, (fx, fy) in enumerate(FLIPS):
            px, py = x ^ fx, y ^ fy
            pj = 2 * px + py
            for k in range(n):
                s = i * n + k
                sends.append(_remote(p[k].at[pj], out[k].at[j0], send.at[s], recv.at[s], (px, py, c)))
                waits.append(_remote(p[k].at[pj], out[k].at[pj], send.at[s], recv.at[s], (px, py, c)))
        for cp in sends:
            cp.start()
        for cp in waits:
            cp.wait_recv()
        for cp in sends:
            cp.wait_send()
        for cp in own:
            cp.wait()

    return pl.pallas_call(
        body, name="chip_exchange",
        in_specs=[HBM] * n, out_specs=[HBM] * n,
        out_shape=[jax.ShapeDtypeStruct(a.shape, a.dtype) for a in parts],
        scratch_shapes=[pltpu.SemaphoreType.DMA((3 * n,)), pltpu.SemaphoreType.DMA((3 * n,)),
                        pltpu.SemaphoreType.DMA((n,))],
    )(*parts)


def _reduce_swap(parts):
    n = len(parts)

    def body(*refs):
        p, out, red = refs[:n], refs[n:2 * n], refs[2 * n:3 * n]
        send, recv, local = refs[3 * n:]
        x, y, c = _place()
        sibling = (x, y, 1 - c)
        own, give, take = [], [], []
        for k in range(n):
            red[k][...] = ((p[k][0].astype(F32) + p[k][1].astype(F32)) + p[k][2].astype(F32)) + p[k][3].astype(F32)
            own.append(pltpu.make_async_copy(red[k], out[k].at[c], local.at[k]))
            give.append(_remote(red[k], out[k].at[c], send.at[k], recv.at[k], sibling))
            take.append(_remote(red[k], out[k].at[1 - c], send.at[k], recv.at[k], sibling))
            own[k].start()
            give[k].start()
        for k in range(n):
            take[k].wait_recv()
            give[k].wait_send()
            own[k].wait()

    return pl.pallas_call(
        body, name="reduce_swap",
        in_specs=[WHOLE] * n, out_specs=[HBM] * n,
        out_shape=[jax.ShapeDtypeStruct((2,) + a.shape[1:], F32) for a in parts],
        scratch_shapes=[pltpu.VMEM(a.shape[1:], F32) for a in parts]
        + [pltpu.SemaphoreType.DMA((n,))] * 3,
        compiler_params=_comm_params(),
    )(*parts)


def _allreduce_small(vec):
    def body(v_ref, o_ref, got, send, recv):
        x, y, c = _place()
        me = 4 * x + 2 * y + c
        got[me] = v_ref[...]
        sends, waits = [], []
        for d in range(1, N_DEV):
            px, py, pc = x ^ (d >> 2), y ^ ((d >> 1) & 1), c ^ (d & 1)
            peer = 4 * px + 2 * py + pc
            sends.append(_remote(v_ref, got.at[me], send.at[d - 1], recv.at[d - 1], (px, py, pc)))
            waits.append(_remote(v_ref, got.at[peer], send.at[d - 1], recv.at[d - 1], (px, py, pc)))
        for cp in sends:
            cp.start()
        for cp in waits:
            cp.wait_recv()
        for cp in sends:
            cp.wait_send()
        acc = got[0]
        for k in range(1, N_DEV):
            acc = acc + got[k]
        o_ref[...] = acc

    return pl.pallas_call(
        body, name="allreduce_small",
        in_specs=[WHOLE], out_specs=WHOLE, out_shape=jax.ShapeDtypeStruct(vec.shape, vec.dtype),
        scratch_shapes=[pltpu.VMEM((N_DEV,) + vec.shape, vec.dtype), pltpu.SemaphoreType.DMA((N_DEV - 1,)),
                        pltpu.SemaphoreType.DMA((N_DEV - 1,))],
    )(vec)


def _adamw_math(w, g, m, v):
    m = ADAM_B1 * m + (1.0 - ADAM_B1) * g
    v = ADAM_B2 * v + (1.0 - ADAM_B2) * (g * g)
    m_hat = m / (1.0 - ADAM_B1 ** ADAM_STEP)
    v_hat = v / (1.0 - ADAM_B2 ** ADAM_STEP)
    return -ADAM_LR * (m_hat / (jnp.sqrt(v_hat) + ADAM_EPS) + ADAM_WD * w), m, v


def _adamw_tiled(w, g, m, v):
    rows, width = w.shape
    tm = rows // 4

    def body(w_ref, g_ref, m_ref, v_ref, d_ref, mo_ref, vo_ref):
        d_ref[...], mo_ref[...], vo_ref[...] = _adamw_math(w_ref[...], g_ref[...], m_ref[...], v_ref[...])

    spec = pl.BlockSpec((tm, width), lambda i: (i, 0))
    out = jax.ShapeDtypeStruct((rows, width), F32)
    return pl.pallas_call(
        body, grid=(rows // tm,), name="adamw_tiled",
        in_specs=[spec] * 4, out_specs=[spec] * 3, out_shape=[out] * 3,
        compiler_params=_params("parallel"),
    )(w, g, m, v)


def _adamw_many(ws, gs, ms, vs):
    n = len(ws)

    def body(*refs):
        for k in range(n):
            w_ref, g_ref, m_ref, v_ref = (refs[i * n + k] for i in range(4))
            d_ref, mo_ref, vo_ref = (refs[(4 + i) * n + k] for i in range(3))
            d_ref[...], mo_ref[...], vo_ref[...] = _adamw_math(w_ref[...], g_ref[...], m_ref[...], v_ref[...])

    outs = [jax.ShapeDtypeStruct(a.shape, F32) for a in ws]
    res = pl.pallas_call(
        body, name="adamw_many",
        in_specs=[WHOLE] * (4 * n), out_specs=[WHOLE] * (3 * n), out_shape=outs * 3,
        compiler_params=_comm_params(),
    )(*ws, *gs, *ms, *vs)
    return res[:n], res[n:2 * n], res[2 * n:]


def kernel(x, positions, a_norm, a_w_in, a_conv, a_w_out, kv_norm, w_dkv, ckv_norm, w_ukv, b_norm, b_w_in, b_q_norm, b_w_uq, b_w_out, final_norm, loss_target, m_a_norm, m_a_w_in, m_a_conv, m_a_w_out, m_kv_norm, m_w_dkv, m_ckv_norm, m_w_ukv, m_b_norm, m_b_w_in, m_b_q_norm, m_b_w_uq, m_b_w_out, m_final_norm, v_a_norm, v_a_w_in, v_a_conv, v_a_w_out, v_kv_norm, v_w_dkv, v_ckv_norm, v_w_ukv, v_b_norm, v_b_w_in, v_b_q_norm, v_b_w_uq, v_b_w_out, v_final_norm):
    w = dict(a_norm=a_norm, a_w_in=a_w_in, a_conv=a_conv, a_w_out=a_w_out, kv_norm=kv_norm, w_dkv=w_dkv,
             ckv_norm=ckv_norm, w_ukv=w_ukv, b_norm=b_norm, b_w_in=b_w_in, b_q_norm=b_q_norm, b_w_uq=b_w_uq,
             b_w_out=b_w_out, final_norm=final_norm)
    m = dict(a_norm=m_a_norm, a_w_in=m_a_w_in, a_conv=m_a_conv, a_w_out=m_a_w_out, kv_norm=m_kv_norm, w_dkv=m_w_dkv,
             ckv_norm=m_ckv_norm, w_ukv=m_w_ukv, b_norm=m_b_norm, b_w_in=m_b_w_in, b_q_norm=m_b_q_norm,
             b_w_uq=m_b_w_uq, b_w_out=m_b_w_out, final_norm=m_final_norm)
    v = dict(a_norm=v_a_norm, a_w_in=v_a_w_in, a_conv=v_a_conv, a_w_out=v_a_w_out, kv_norm=v_kv_norm, w_dkv=v_w_dkv,
             ckv_norm=v_ckv_norm, w_ukv=v_w_ukv, b_norm=v_b_norm, b_w_in=v_b_w_in, b_q_norm=v_b_q_norm,
             b_w_uq=v_b_w_uq, b_w_out=v_b_w_out, final_norm=v_final_norm)

    loss, dx, gmat, gsmall = _local_step(x, positions, loss_target, w)

    kinds = [GRAD_KIND[n] for n in GRAD_MATS]
    grads = [gmat[n] for n in GRAD_MATS]
    theirs = _pair_exchange(grads, kinds)
    mine = _reduce_swap(_chip_exchange(_chip_partials(grads, theirs, kinds)))
    flat = jnp.concatenate([gsmall[n].reshape(-1) for n in SMALL] + [loss.reshape(1)])
    svec = _allreduce_small(jnp.pad(flat, (0, SMALL_ROWS * PACK_W - flat.shape[0])).reshape(SMALL_ROWS, PACK_W))
    loss = svec.reshape(-1)[LOSS_SLOT]
    g = _shard_grads({n: a.reshape(-1, a.shape[-1]) for n, a in zip(GRAD_MATS, mine)}, svec)

    two_d = lambda n, a: a.reshape((-1, a.shape[-1]))
    big = "a_w_in"
    rest = [n for n in WEIGHTS if n != big]
    d_big, m_big, v_big = _adamw_tiled(*(two_d(big, t[big]) for t in (w, g, m, v)))
    d_r, m_r, v_r = _adamw_many(*([two_d(n, t[n]) for n in rest] for t in (w, g, m, v)))
    out = {"g": g, "d": dict(zip(rest, d_r)), "m": dict(zip(rest, m_r)), "v": dict(zip(rest, v_r))}
    out["d"][big], out["m"][big], out["v"][big] = d_big, m_big, v_big
    return (loss, dx) + tuple(out[kind][n].reshape(SHARD_SHAPES[n]) for kind in "gdmv" for n in WEIGHTS)
```

```python
import functools
import math

import numpy as np
import jax
import jax.numpy as jnp
from jax import lax
from jax.experimental import pallas as pl
from jax.experimental.pallas import tpu as pltpu

F32 = jnp.float32
BF16 = jnp.bfloat16

D_MODEL = 1024
E_A = 1024
CONV_WIDTH = 3
N_HEADS = 8
QK_NOPE = 64
QK_ROPE = 32
V_HEAD = 64
KV_RANK = 256
Q_RANK = 384
E_B = N_HEADS * V_HEAD
ROPE_THETA = 10000.0
SOFTMAX_SCALE = 1.0 / math.sqrt(QK_NOPE + QK_ROPE)
LOG2E = math.log2(math.e)
LN2 = math.log(2.0)
Q_PRESCALE = SOFTMAX_SCALE * LOG2E
EPS = 1e-6
HEAD_PAD = 128
QK_PAD = N_HEADS * HEAD_PAD
ROPE_LO = QK_NOPE
ROPE_HALF = QK_ROPE // 2
KR_PAD = KV_RANK + HEAD_PAD

ADAM_LR = 0.001
ADAM_B1 = 0.9
ADAM_B2 = 0.999
ADAM_EPS = 1e-08
ADAM_WD = 0.01
ADAM_STEP = 10

VMEM_LIMIT = 56 * 1024 * 1024
ROW_TILE = 512
CONV_BWD_TILE = 256
ATT_TILE_FWD = 1024
ATT_TILE = 512
LANES = 128
PACK_W = 1024

N_CHIPS = 4
N_DEV = 8


def _dot(a, b):
    return jnp.dot(a, b, preferred_element_type=F32)


def _dot_nt(a, b):
    return lax.dot_general(a, b, (((1,), (1,)), ((), ())), preferred_element_type=F32)


def _dot_tn(a, b):
    return lax.dot_general(a, b, (((0,), (0,)), ((), ())), preferred_element_type=F32)


def _rms(x):
    r = lax.rsqrt(jnp.mean(x * x, axis=-1, keepdims=True) + EPS)
    return r, x * r


def _rms_bwd(dxh, xh, r):
    return r * (dxh - xh * jnp.mean(dxh * xh, axis=-1, keepdims=True))


def _rope_fwd(a, c, s1, s2):
    return a * c + pltpu.roll(a, HEAD_PAD - ROPE_HALF, 1) * s1 + pltpu.roll(a, ROPE_HALF, 1) * s2


def _rope_bwd(g, c, s1, s2):
    return g * c + pltpu.roll(g * s1, ROPE_HALF, 1) + pltpu.roll(g * s2, HEAD_PAD - ROPE_HALF, 1)


def _sigmoid(x):
    return 1.0 / (1.0 + jnp.exp(-x))


def _row_spec(tm, n):
    return pl.BlockSpec((tm, n), lambda i: (i, 0))


def _const_spec(shape):
    nd = len(shape)
    return pl.BlockSpec(shape, lambda i: (0,) * nd, pipeline_mode=pl.Buffered(1))


def _acc_spec(shape):
    nd = len(shape)
    return pl.BlockSpec(shape, lambda i: (0,) * nd, pipeline_mode=pl.Buffered(1))


def _params(*sem):
    return pltpu.CompilerParams(dimension_semantics=sem, vmem_limit_bytes=VMEM_LIMIT)


MESH = pl.DeviceIdType.MESH
HBM = pl.BlockSpec(memory_space=pl.ANY)
WHOLE = pl.BlockSpec(memory_space=pltpu.VMEM)
FLIPS = ((1, 0), (0, 1), (1, 1))


def _place():
    return lax.axis_index("x"), lax.axis_index("y"), lax.axis_index("c")


def _remote(src, dst, send, recv, peer):
    return pltpu.make_async_remote_copy(src_ref=src, dst_ref=dst, send_sem=send, recv_sem=recv, device_id=peer,
                                        device_id_type=MESH)


def _comm_params():
    return pltpu.CompilerParams(vmem_limit_bytes=VMEM_LIMIT)


def _rope_consts():
    lane = np.arange(HEAD_PAD)
    first = (lane >= ROPE_LO) & (lane < ROPE_LO + ROPE_HALF)
    second = (lane >= ROPE_LO + ROPE_HALF) & (lane < ROPE_LO + QK_ROPE)
    f = np.where(first, lane - ROPE_LO, np.where(second, lane - ROPE_LO - ROPE_HALF, 0))
    inv = np.float32(ROPE_THETA) ** (-(2 * f).astype(np.float32) / np.float32(QK_ROPE))
    out = np.zeros((8, HEAD_PAD), np.float32)
    out[0] = inv
    out[1] = first
    out[2] = second
    out[3] = lane < ROPE_LO
    return jnp.asarray(out)


def _gather_plan(w, sm, outs, osm, bf, sems):
    n = len(w)
    send_i, recv_i, send_d, recv_d, send_s, recv_s, local = sems
    x, y, c = _place()
    j0 = 2 * x + y
    sibling = (x, y, 1 - c)
    own = [pltpu.make_async_copy(bf[k], outs[k].at[j0], local.at[k]) for k in range(n)]
    if sm is not None:
        own.append(pltpu.make_async_copy(sm, osm.at[j0], local.at[n]))

    def half(k, cc):
        h = w[k].shape[0] // 2
        return pl.ds(pl.multiple_of(cc * h, 16), h)

    sends, arrivals, forwards, fwaits = [], [], [], []
    for i, (fx, fy) in enumerate(FLIPS):
        px, py = x ^ fx, y ^ fy
        pj = 2 * px + py
        for k in range(n):
            s = i * n + k
            sends.append(_remote(bf[k].at[half(k, c)], outs[k].at[j0, half(k, c)], send_i.at[s], recv_i.at[s],
                                 (px, py, c)))
            arrivals.append(_remote(bf[k].at[half(k, c)], outs[k].at[pj, half(k, c)], send_i.at[s], recv_i.at[s],
                                    (px, py, c)))
            forwards.append(_remote(outs[k].at[pj, half(k, c)], outs[k].at[pj, half(k, c)], send_d.at[s],
                                    recv_d.at[s], sibling))
            fwaits.append(_remote(outs[k].at[pj, half(k, 1 - c)], outs[k].at[pj, half(k, 1 - c)], send_d.at[s],
                                  recv_d.at[s], sibling))
        if sm is not None:
            sends.append(_remote(sm, osm.at[j0], send_s.at[i], recv_s.at[i], (px, py, c)))
            fwaits.append(_remote(sm, osm.at[pj], send_s.at[i], recv_s.at[i], (px, py, c)))
    return own, sends, arrivals, forwards, fwaits


def _gather_begin(w, bf, plan):
    own, sends, _, _, _ = plan
    for k in range(len(w)):
        bf[k][...] = w[k][...].astype(BF16)
    for cp in own + sends:
        cp.start()


def _gather_end(plan):
    own, sends, arrivals, forwards, fwaits = plan
    for cp, fwd in zip(arrivals, forwards):
        cp.wait_recv()
        fwd.start()
    for cp in fwaits:
        cp.wait_recv()
    for cp in sends + forwards:
        cp.wait_send()
    for cp in own:
        cp.wait()


def _gather_sems(n, with_small):
    return ([pltpu.SemaphoreType.DMA((3 * n,))] * 4 + [pltpu.SemaphoreType.DMA((3,))] * 2
            + [pltpu.SemaphoreType.DMA((n + (1 if with_small else 0),))])


def _rope_tables_gather(pos_col, w_in, small):
    t = pos_col.shape[0]
    tm = min(ROW_TILE, t)
    steps = t // tm

    def body(p_ref, k_ref, w_ref, sm_ref, c_ref, s1_ref, s2_ref, out_ref, osm_ref, bf_ref, *sems):
        plan = _gather_plan([w_ref], sm_ref, [out_ref], osm_ref, [bf_ref], sems)
        i = pl.program_id(0)

        @pl.when(i == 0)
        def _():
            _gather_begin([w_ref], [bf_ref], plan)

        inv, first, second, nope = k_ref[0:1, :], k_ref[1:2, :], k_ref[2:3, :], k_ref[3:4, :]
        ang = p_ref[...].astype(F32) * inv
        cs, sn = jnp.cos(ang), jnp.sin(ang)
        c_ref[...] = cs * (first + second) + nope
        s1_ref[...] = -sn * first
        s2_ref[...] = sn * second

        @pl.when(i == steps - 1)
        def _():
            _gather_end(plan)

    out = jax.ShapeDtypeStruct((t, HEAD_PAD), F32)
    return pl.pallas_call(
        body, grid=(steps,), name="rope_tables_gather",
        in_specs=[_row_spec(tm, 1), _const_spec((8, HEAD_PAD)), WHOLE, WHOLE],
        out_specs=[_row_spec(tm, HEAD_PAD)] * 3 + [HBM, HBM],
        out_shape=[out] * 3 + [jax.ShapeDtypeStruct((N_CHIPS,) + w_in.shape, BF16),
                               jax.ShapeDtypeStruct((N_CHIPS,) + small.shape, small.dtype)],
        scratch_shapes=[pltpu.VMEM(w_in.shape, BF16)] + _gather_sems(1, True),
        compiler_params=_params("arbitrary"),
    )(pos_col, _rope_consts(), w_in, small)


def _shift_down(v, prev, row):
    p1, p2 = prev[7:8, :], prev[6:7, :]
    v1 = jnp.where(row == 0, p1, pltpu.roll(v, 1, 0))
    v2 = jnp.where(row == 0, p2, jnp.where(row == 1, p1, pltpu.roll(v, 2, 0)))
    return v1, v2


def _conv_fwd(x, seq, ga, w_in4, cw, shards):
    t = x.shape[0]
    tm = min(ROW_TILE, seq)
    tiles_per_seq = seq // tm
    steps = t // tm
    n = len(shards)

    def body(x_ref, ga_ref, w_ref, cw_ref, *rest):
        sh, (b_ref, c_ref, u_ref, g_ref, ym_ref) = rest[:n], rest[n:n + 5]
        outs, carry_ref, bf, sems = rest[n + 5:2 * n + 5], rest[2 * n + 5], rest[2 * n + 6:3 * n + 6], rest[3 * n + 6:]
        plan = _gather_plan(sh, None, outs, None, bf, sems)
        i = pl.program_id(0)

        @pl.when(i == 0)
        def _():
            _gather_begin(sh, bf, plan)

        @pl.when(i % tiles_per_seq == 0)
        def _():
            carry_ref[...] = jnp.zeros_like(carry_ref)

        _, xh = _rms(x_ref[...])
        h = (xh * ga_ref[...]).astype(BF16)
        b, c, u, g = (_dot(h, w_ref[j]) for j in range(4))
        v = c * u
        row = lax.broadcasted_iota(jnp.int32, (tm, 1), 0)
        v1, v2 = _shift_down(v, carry_ref[...], row)
        carry_ref[...] = v[tm - 8:tm, :]
        cv = cw_ref[2:3, :] * v + cw_ref[1:2, :] * v1 + cw_ref[0:1, :] * v2
        b_ref[...] = b.astype(BF16)
        c_ref[...] = c.astype(BF16)
        u_ref[...] = u.astype(BF16)
        g_ref[...] = g.astype(BF16)
        ym_ref[...] = (g * _sigmoid(g) * b * cv).astype(BF16)

        @pl.when(i == steps - 1)
        def _():
            _gather_end(plan)

    out = jax.ShapeDtypeStruct((t, E_A), BF16)
    res = pl.pallas_call(
        body, grid=(steps,), name="conv_fwd",
        in_specs=[_row_spec(tm, D_MODEL), _const_spec((1, D_MODEL)), _const_spec((4, D_MODEL, E_A)),
                  _const_spec((8, E_A))] + [WHOLE] * n,
        out_specs=[_row_spec(tm, E_A)] * 5 + [HBM] * n,
        out_shape=[out] * 5 + [jax.ShapeDtypeStruct((N_CHIPS,) + a.shape, BF16) for a in shards],
        scratch_shapes=[pltpu.VMEM((8, E_A), F32)] + [pltpu.VMEM(a.shape, BF16) for a in shards]
        + _gather_sems(n, False),
        compiler_params=_params("arbitrary"),
    )(x, ga, w_in4, cw, *shards)
    return res[:5], res[5:]


def _mid_fwd(x, ym, w_out, gk, gb, w_dkv, gc, w_uk, w_uv, w_bin, gq, w_uq, rc, rs1, rs2):
    t = x.shape[0]
    tm = min(ROW_TILE, t)

    def body(x_ref, ym_ref, wo_ref, gk_ref, gb_ref, wd_ref, gc_ref, wk_ref, wv_ref, wi_ref, gq_ref, wq_ref,
             c_ref, s1_ref, s2_ref, x1_ref, q_ref, k_ref, v_ref, gate_ref, cq_ref, ckv_ref):
        cb, s1b, s2b = c_ref[...], s1_ref[...], s2_ref[...]
        x1 = x_ref[...] + _dot(ym_ref[...], wo_ref[...])
        x1_ref[...] = x1
        _, xh = _rms(x1)
        hk = (xh * gk_ref[...]).astype(BF16)
        h1 = (xh * gb_ref[...]).astype(BF16)

        ckr = _dot(hk, wd_ref[...])
        ckv_raw = ckr[:, :KV_RANK]
        ckv_ref[...] = ckv_raw.astype(BF16)
        _, ch = _rms(ckv_raw)
        ckv = (ch * gc_ref[...]).astype(BF16)
        kr = _rope_fwd(ckr[:, KV_RANK:], cb, s1b, s2b)
        kn = _dot(ckv, wk_ref[...])
        for h in range(N_HEADS):
            sl = slice(h * HEAD_PAD, (h + 1) * HEAD_PAD)
            k_ref[:, sl] = (kn[:, sl] + kr).astype(BF16)
        v_ref[...] = _dot(ckv, wv_ref[...]).astype(BF16)

        pb = _dot(h1, wi_ref[...])
        cq = pb[:, :Q_RANK]
        cq_ref[...] = cq.astype(BF16)
        gate_ref[...] = pb[:, Q_RANK:].astype(BF16)
        _, cqh = _rms(cq)
        q = _dot((cqh * gq_ref[...]).astype(BF16), wq_ref[...])
        for h in range(N_HEADS):
            sl = slice(h * HEAD_PAD, (h + 1) * HEAD_PAD)
            q_ref[:, sl] = (_rope_fwd(q[:, sl], cb, s1b, s2b) * Q_PRESCALE).astype(BF16)

    def sds(n, dt):
        return jax.ShapeDtypeStruct((t, n), dt)

    return pl.pallas_call(
        body, grid=(t // tm,), name="mid_fwd",
        in_specs=[_row_spec(tm, D_MODEL), _row_spec(tm, E_A), _const_spec((E_A, D_MODEL)),
                  _const_spec((1, D_MODEL)), _const_spec((1, D_MODEL)), _const_spec((D_MODEL, KR_PAD)),
                  _const_spec((1, KV_RANK)), _const_spec((KV_RANK, QK_PAD)), _const_spec((KV_RANK, E_B)),
                  _const_spec((D_MODEL, Q_RANK + E_B)), _const_spec((1, Q_RANK)), _const_spec((Q_RANK, QK_PAD)),
                  _row_spec(tm, HEAD_PAD), _row_spec(tm, HEAD_PAD), _row_spec(tm, HEAD_PAD)],
        out_specs=[_row_spec(tm, D_MODEL), _row_spec(tm, QK_PAD), _row_spec(tm, QK_PAD), _row_spec(tm, E_B),
                   _row_spec(tm, E_B), _row_spec(tm, Q_RANK), _row_spec(tm, KV_RANK)],
        out_shape=[sds(D_MODEL, F32), sds(QK_PAD, BF16), sds(QK_PAD, BF16), sds(E_B, BF16), sds(E_B, BF16),
                   sds(Q_RANK, BF16), sds(KV_RANK, BF16)],
        compiler_params=_params("parallel"),
    )(x, ym, w_out, gk, gb, w_dkv, gc, w_uk, w_uv, w_bin, gq, w_uq, rc, rs1, rs2)


def _pair_specs(seq):
    qk = pl.BlockSpec((seq, 2 * HEAD_PAD), lambda b, p: (b, p))
    vo = pl.BlockSpec((seq, 2 * V_HEAD), lambda b, p: (b, p))
    st = pl.BlockSpec((None, 2, seq), lambda b, p: (p, 0, b))
    return qk, vo, st


def _attn_fwd(q, k, v, seq):
    t = q.shape[0]
    tq = min(ATT_TILE_FWD, seq)
    nq = seq // tq

    def body(q_ref, k_ref, v_ref, o_ref, lse_ref, m_scr, l_scr, acc_scr):
        lane = lax.broadcasted_iota(jnp.int32, (tq, 2 * V_HEAD), 1)

        def q_step(qi, _):
            q0 = pl.multiple_of(qi * tq, tq)
            m_scr[...] = jnp.full(m_scr.shape, -jnp.inf, F32)
            l_scr[...] = jnp.zeros_like(l_scr)
            acc_scr[...] = jnp.zeros_like(acc_scr)

            def block(q_lo, q_n, k0, k_n, masked):
                rows = slice(q_lo, q_lo + q_n)
                vt = v_ref[pl.ds(k0, k_n), :]
                for hh in range(2):
                    hs = slice(hh * HEAD_PAD, (hh + 1) * HEAD_PAD)
                    s = _dot_nt(q_ref[pl.ds(q0 + q_lo, q_n), hs], k_ref[pl.ds(k0, k_n), hs])
                    if masked:
                        row = lax.broadcasted_iota(jnp.int32, (q_n, k_n), 0)
                        col = lax.broadcasted_iota(jnp.int32, (q_n, k_n), 1)
                        s = jnp.where(col <= row, s, -jnp.inf)
                    m_old = m_scr[hh, rows]
                    m_new = jnp.maximum(m_old, jnp.max(s, axis=-1, keepdims=True))
                    alpha = jnp.exp2(m_old - m_new)
                    ps = [jnp.exp2(s[:, j * LANES:(j + 1) * LANES] - m_new) for j in range(k_n // LANES)]
                    l_scr[hh, rows] = alpha * l_scr[hh, rows] + functools.reduce(lambda a, b: a + b, ps)
                    p = jnp.concatenate(ps, axis=-1).astype(BF16)
                    acc_scr[hh, rows] = alpha * acc_scr[hh, rows] + _dot(p, vt)
                    m_scr[hh, rows] = m_new

            def k_step(ki, _):
                block(0, tq, pl.multiple_of(ki * tq, tq), tq, False)
                return 0

            lax.fori_loop(0, qi, k_step, 0)
            half = tq // 2
            block(0, tq, q0, half, True)
            block(half, half, q0 + half, half, True)
            l0 = jnp.sum(l_scr[0], axis=-1, keepdims=True)
            l1 = jnp.sum(l_scr[1], axis=-1, keepdims=True)
            o_ref[pl.ds(q0, tq), :] = jnp.where(lane < V_HEAD, acc_scr[0] / l0, acc_scr[1] / l1).astype(BF16)
            stats = jnp.where(lane == 0, m_scr[0] + jnp.log2(l0), m_scr[1] + jnp.log2(l1)).T
            lse_ref[:, pl.ds(q0, tq)] = stats[0:2, :]
            return 0

        lax.fori_loop(0, nq, q_step, 0)

    qk, vo, st = _pair_specs(seq)
    return pl.pallas_call(
        body, grid=(t // seq, N_HEADS // 2), name="attn_fwd",
        in_specs=[qk, qk, vo], out_specs=[vo, st],
        out_shape=[jax.ShapeDtypeStruct((t, E_B), BF16), jax.ShapeDtypeStruct((N_HEADS // 2, 2, t), F32)],
        scratch_shapes=[pltpu.VMEM((2, tq, LANES), F32), pltpu.VMEM((2, tq, LANES), F32),
                        pltpu.VMEM((2, tq, 2 * V_HEAD), F32)],
        compiler_params=_params("parallel", "parallel"),
    )(q, k, v)


def _head_fwd_bwd(o, gate, x1, tgt, w_bout, gf):
    t = o.shape[0]
    tm = min(ROW_TILE, t)

    def body(o_ref, gate_ref, x1_ref, tgt_ref, w_ref, gf_ref,
             dx2_ref, do_ref, dgate_ref, dd_ref, loss_ref, dgf_ref, dw_ref):
        @pl.when(pl.program_id(0) == 0)
        def _():
            loss_ref[...] = jnp.zeros_like(loss_ref)
            dgf_ref[...] = jnp.zeros_like(dgf_ref)
            dw_ref[...] = jnp.zeros_like(dw_ref)

        o = o_ref[...].astype(F32)
        gt = gate_ref[...].astype(F32)
        sg = _sigmoid(gt)
        silu = gt * sg
        z = (o * silu).astype(BF16)
        x2 = x1_ref[...] + _dot(z, w_ref[...])
        r2, xh2 = _rms(x2)
        gf = gf_ref[...]
        err = xh2 * gf - tgt_ref[...]
        loss_ref[...] += 0.5 * jnp.sum(jnp.mean(err * err, axis=-1, keepdims=True))
        dy = err * (1.0 / D_MODEL)
        dgf_ref[...] += jnp.sum(dy * xh2, axis=0, keepdims=True)
        dx2 = _rms_bwd(dy * gf, xh2, r2)
        dx2_ref[...] = dx2
        dx2b = dx2.astype(BF16)
        dw_ref[...] += _dot_tn(z, dx2b)
        dz = _dot_nt(dx2b, w_ref[...])
        do = dz * silu
        do_ref[...] = do.astype(BF16)
        dgate_ref[...] = (dz * o * (sg * (1.0 + gt * (1.0 - sg)))).astype(BF16)
        prod = do * o
        lane = lax.broadcasted_iota(jnp.int32, (tm, 2 * V_HEAD), 1)
        cols = jnp.zeros((tm, LANES), F32)
        for p in range(N_HEADS // 2):
            blk = prod[:, p * 2 * V_HEAD:(p + 1) * 2 * V_HEAD]
            d0 = jnp.sum(jnp.where(lane < V_HEAD, blk, 0.0), axis=-1, keepdims=True)
            d1 = jnp.sum(jnp.where(lane < V_HEAD, 0.0, blk), axis=-1, keepdims=True)
            cols = jnp.where(lane == 2 * p, d0, jnp.where(lane == 2 * p + 1, d1, cols))
        rows = cols.T
        for h in range(N_HEADS):
            dd_ref[h // 2, h % 2:h % 2 + 1, :] = rows[h:h + 1, :]

    return pl.pallas_call(
        body, grid=(t // tm,), name="head_fwd_bwd",
        in_specs=[_row_spec(tm, E_B), _row_spec(tm, E_B), _row_spec(tm, D_MODEL), _row_spec(tm, D_MODEL),
                  _const_spec((E_B, D_MODEL)), _const_spec((1, D_MODEL))],
        out_specs=[_row_spec(tm, D_MODEL), _row_spec(tm, E_B), _row_spec(tm, E_B),
                   pl.BlockSpec((N_HEADS // 2, 2, tm), lambda i: (0, 0, i)),
                   _acc_spec((1, 1)), _acc_spec((1, D_MODEL)), _acc_spec((E_B, D_MODEL))],
        out_shape=[jax.ShapeDtypeStruct((t, D_MODEL), F32), jax.ShapeDtypeStruct((t, E_B), BF16),
                   jax.ShapeDtypeStruct((t, E_B), BF16), jax.ShapeDtypeStruct((N_HEADS // 2, 2, t), F32),
                   jax.ShapeDtypeStruct((1, 1), F32), jax.ShapeDtypeStruct((1, D_MODEL), F32),
                   jax.ShapeDtypeStruct((E_B, D_MODEL), F32)],
        compiler_params=_params("arbitrary"),
    )(o, gate, x1, tgt, w_bout, gf)


def _attn_bwd(q, k, v, do, lse, dd, seq):
    t = q.shape[0]
    tq = min(ATT_TILE, seq)
    nq = seq // tq

    def body(q_ref, k_ref, v_ref, do_ref, lse_ref, dd_ref, dq_ref, dk_ref, dv_ref, dq_acc, dk_acc, dv_acc):
        dq_acc[...] = jnp.zeros_like(dq_acc)

        def k_step(ki, _):
            k0 = pl.multiple_of(ki * tq, tq)
            dk_acc[...] = jnp.zeros_like(dk_acc)
            dv_acc[...] = jnp.zeros_like(dv_acc)

            def block(k_lo, k_n, q0, q_n, masked):
                rows = slice(k_lo, k_lo + k_n)
                lane = lax.broadcasted_iota(jnp.int32, (q_n, 2 * V_HEAD), 1)
                vt = v_ref[pl.ds(k0 + k_lo, k_n), :]
                do_pair = do_ref[pl.ds(q0, q_n), :]
                for hh in range(2):
                    hs = slice(hh * HEAD_PAD, (hh + 1) * HEAD_PAD)
                    kt = k_ref[pl.ds(k0 + k_lo, k_n), hs]
                    qt = q_ref[pl.ds(q0, q_n), hs]
                    mine = (lane < V_HEAD) if hh == 0 else (lane >= V_HEAD)
                    do_h = jnp.where(mine, do_pair, jnp.zeros((), BF16))
                    st = _dot_nt(kt, qt)
                    if masked:
                        krow = lax.broadcasted_iota(jnp.int32, (k_n, q_n), 0)
                        qcol = lax.broadcasted_iota(jnp.int32, (k_n, q_n), 1)
                        st = jnp.where(krow <= qcol, st, -jnp.inf)
                    pt = jnp.exp2(st - lse_ref[hh:hh + 1, pl.ds(q0, q_n)])
                    dpt = _dot_nt(vt, do_h)
                    dst = (pt * (dpt - dd_ref[hh:hh + 1, pl.ds(q0, q_n)])).astype(BF16)
                    dv_acc[rows, :] += _dot(pt.astype(BF16), do_h)
                    dk_acc[rows, hs] += _dot(dst, qt)
                    dq_acc[pl.ds(q0, q_n), hs] += _dot_tn(dst, kt)

            def q_step(qi, _):
                block(0, tq, pl.multiple_of(qi * tq, tq), tq, False)
                return 0

            half = tq // 2
            block(0, half, k0, tq, True)
            block(half, half, pl.multiple_of(k0 + half, half), half, True)
            lax.fori_loop(ki + 1, nq, q_step, 0)
            dk_ref[pl.ds(k0, tq), :] = (dk_acc[...] * LN2).astype(BF16)
            dv_ref[pl.ds(k0, tq), :] = dv_acc[...].astype(BF16)
            return 0

        lax.fori_loop(0, nq, k_step, 0)
        dq_ref[...] = (dq_acc[...] * SOFTMAX_SCALE).astype(BF16)

    qk, vo, st = _pair_specs(seq)
    return pl.pallas_call(
        body, grid=(t // seq, N_HEADS // 2), name="attn_bwd",
        in_specs=[qk, qk, vo, vo, st, st], out_specs=[qk, qk, vo],
        out_shape=[jax.ShapeDtypeStruct((t, QK_PAD), BF16), jax.ShapeDtypeStruct((t, QK_PAD), BF16),
                   jax.ShapeDtypeStruct((t, E_B), BF16)],
        scratch_shapes=[pltpu.VMEM((seq, 2 * HEAD_PAD), F32), pltpu.VMEM((tq, 2 * HEAD_PAD), F32),
                        pltpu.VMEM((tq, 2 * V_HEAD), F32)],
        compiler_params=_params("parallel", "parallel"),
    )(q, k, v, do, lse, dd)


def _mid_bwd(dq, dk, dv, dgate, dx2, x1, cq, ckv, rc, rs1, rs2, w_uq, w_bin, w_uk, w_uv, w_dkv, gq, gc, gk, gb):
    t = dq.shape[0]
    tm = min(ROW_TILE, t)

    def body(dq_ref, dk_ref, dv_ref, dgate_ref, dx2_ref, x1_ref, cq_ref, ckv_ref, c_ref, s1_ref, s2_ref,
             wq_ref, wi_ref, wk_ref, wv_ref, wd_ref, gq_ref, gc_ref, gk_ref, gb_ref,
             dx1_ref, dwq_ref, dwi_ref, dwk_ref, dwv_ref, dwd_ref, dgq_ref, dgc_ref, dgk_ref, dgb_ref):
        @pl.when(pl.program_id(0) == 0)
        def _():
            for ref in (dwq_ref, dwi_ref, dwk_ref, dwv_ref, dwd_ref, dgq_ref, dgc_ref, dgk_ref, dgb_ref):
                ref[...] = jnp.zeros_like(ref)

        cb, s1b, s2b = c_ref[...], s1_ref[...], s2_ref[...]
        r1, xh = _rms(x1_ref[...])
        gk, gb, gq, gc = gk_ref[...], gb_ref[...], gq_ref[...], gc_ref[...]
        hk = (xh * gk).astype(BF16)
        h1 = (xh * gb).astype(BF16)

        dqs = [_rope_bwd(dq_ref[:, h * HEAD_PAD:(h + 1) * HEAD_PAD].astype(F32), cb, s1b, s2b)
               for h in range(N_HEADS)]
        dqb = jnp.concatenate(dqs, axis=-1).astype(BF16)
        rq, cqh = _rms(cq_ref[...].astype(F32))
        dwq_ref[...] += _dot_tn((cqh * gq).astype(BF16), dqb)
        dcqn = _dot_nt(dqb, wq_ref[...])
        dgq_ref[...] += jnp.sum(dcqn * cqh, axis=0, keepdims=True)
        dcq = _rms_bwd(dcqn * gq, cqh, rq)
        dpb = jnp.concatenate([dcq.astype(BF16), dgate_ref[...]], axis=-1)
        dwi_ref[...] += _dot_tn(h1, dpb)
        dh1 = _dot_nt(dpb, wi_ref[...])

        rcv, ch = _rms(ckv_ref[...].astype(F32))
        ckvn = (ch * gc).astype(BF16)
        dkb, dvb = dk_ref[...], dv_ref[...]
        dwk_ref[...] += _dot_tn(ckvn, dkb)
        dwv_ref[...] += _dot_tn(ckvn, dvb)
        dckv = _dot_nt(dkb, wk_ref[...]) + _dot_nt(dvb, wv_ref[...])
        dgc_ref[...] += jnp.sum(dckv * ch, axis=0, keepdims=True)
        dckv_raw = _rms_bwd(dckv * gc, ch, rcv)
        dkr = dk_ref[:, 0:HEAD_PAD].astype(F32)
        for h in range(1, N_HEADS):
            dkr = dkr + dk_ref[:, h * HEAD_PAD:(h + 1) * HEAD_PAD].astype(F32)
        dkr = _rope_bwd(dkr, cb, s1b, s2b)
        dckr = jnp.concatenate([dckv_raw, dkr], axis=-1).astype(BF16)
        dwd_ref[...] += _dot_tn(hk, dckr)
        dhk = _dot_nt(dckr, wd_ref[...])

        dgb_ref[...] += jnp.sum(dh1 * xh, axis=0, keepdims=True)
        dgk_ref[...] += jnp.sum(dhk * xh, axis=0, keepdims=True)
        dx1_ref[...] = dx2_ref[...] + _rms_bwd(dh1 * gb + dhk * gk, xh, r1)

    acc_shapes = [(Q_RANK, QK_PAD), (D_MODEL, Q_RANK + E_B), (KV_RANK, QK_PAD), (KV_RANK, E_B), (D_MODEL, KR_PAD),
                  (1, Q_RANK), (1, KV_RANK), (1, D_MODEL), (1, D_MODEL)]
    return pl.pallas_call(
        body, grid=(t // tm,), name="mid_bwd",
        in_specs=[_row_spec(tm, QK_PAD), _row_spec(tm, QK_PAD), _row_spec(tm, E_B), _row_spec(tm, E_B),
                  _row_spec(tm, D_MODEL), _row_spec(tm, D_MODEL), _row_spec(tm, Q_RANK), _row_spec(tm, KV_RANK),
                  _row_spec(tm, HEAD_PAD), _row_spec(tm, HEAD_PAD), _row_spec(tm, HEAD_PAD),
                  _const_spec((Q_RANK, QK_PAD)), _const_spec((D_MODEL, Q_RANK + E_B)),
                  _const_spec((KV_RANK, QK_PAD)), _const_spec((KV_RANK, E_B)), _const_spec((D_MODEL, KR_PAD)),
                  _const_spec((1, Q_RANK)), _const_spec((1, KV_RANK)), _const_spec((1, D_MODEL)),
                  _const_spec((1, D_MODEL))],
        out_specs=[_row_spec(tm, D_MODEL)] + [_acc_spec(s) for s in acc_shapes],
        out_shape=[jax.ShapeDtypeStruct((t, D_MODEL), F32)] + [jax.ShapeDtypeStruct(s, F32) for s in acc_shapes],
        compiler_params=_params("arbitrary"),
    )(dq, dk, dv, dgate, dx2, x1, cq, ckv, rc, rs1, rs2, w_uq, w_bin, w_uk, w_uv, w_dkv, gq, gc, gk, gb)


def _conv_bwd(dx1, x, b, c, u, g, seq, w_out, w_in4, ga, cw):
    t = x.shape[0]
    tm = min(CONV_BWD_TILE, seq)
    tiles_per_seq = seq // tm
    n = t // tm
    halo = tm // 8

    def tile(i):
        return n - 1 - i

    def rev(width):
        return pl.BlockSpec((tm, width), lambda i: (tile(i), 0))

    def prev8(width):
        return pl.BlockSpec((8, width), lambda i: (jnp.maximum(tile(i) * halo - 1, 0), 0))

    def body(dx1_ref, x_ref, b_ref, c_ref, u_ref, g_ref, cp_ref, up_ref, wo_ref, wi_ref, ga_ref, cw_ref,
             dx_ref, dwi_ref, dwo_ref, dcw_ref, dga_ref, carry_ref):
        i = pl.program_id(0)
        j = tile(i)

        @pl.when(i == 0)
        def _():
            for ref in (dwi_ref, dwo_ref, dcw_ref, dga_ref):
                ref[...] = jnp.zeros_like(ref)

        @pl.when(j % tiles_per_seq == tiles_per_seq - 1)
        def _():
            carry_ref[...] = jnp.zeros_like(carry_ref)

        dx1 = dx1_ref[...]
        dx1b = dx1.astype(BF16)
        b, c, u, g = (r[...].astype(F32) for r in (b_ref, c_ref, u_ref, g_ref))
        v = c * u
        first = (j % tiles_per_seq == 0).astype(F32)
        vprev = cp_ref[...].astype(F32) * up_ref[...].astype(F32) * (1.0 - first)
        row = lax.broadcasted_iota(jnp.int32, (tm, 1), 0)
        v1, v2 = _shift_down(v, vprev, row)
        w0, w1, w2 = cw_ref[0:1, :], cw_ref[1:2, :], cw_ref[2:3, :]
        cv = w2 * v + w1 * v1 + w0 * v2
        sg = _sigmoid(g)
        silu = g * sg
        ym = (silu * b * cv).astype(BF16)
        dwo_ref[...] += _dot_tn(ym, dx1b)
        dym = _dot_nt(dx1b, wo_ref[...])
        db = dym * silu * cv
        dcv = dym * silu * b
        dg = dym * b * cv * (sg * (1.0 + g * (1.0 - sg)))

        nxt = carry_ref[...]
        n0, n1 = nxt[0:1, :], nxt[1:2, :]
        d1 = jnp.where(row == tm - 1, n0, pltpu.roll(dcv, tm - 1, 0))
        d2 = jnp.where(row == tm - 1, n1, jnp.where(row == tm - 2, n0, pltpu.roll(dcv, tm - 2, 0)))
        carry_ref[...] = dcv[0:8, :]
        dv = w2 * dcv + w1 * d1 + w0 * d2
        dcw_ref[0:1, :] += jnp.sum(dcv * v2, axis=0, keepdims=True)
        dcw_ref[1:2, :] += jnp.sum(dcv * v1, axis=0, keepdims=True)
        dcw_ref[2:3, :] += jnp.sum(dcv * v, axis=0, keepdims=True)

        r0, xh = _rms(x_ref[...])
        ga = ga_ref[...]
        h = (xh * ga).astype(BF16)
        dh = jnp.zeros((tm, D_MODEL), F32)
        for idx, dpart in enumerate((db, dv * u, dv * c, dg)):
            dpb = dpart.astype(BF16)
            dwi_ref[idx] += _dot_tn(h, dpb)
            dh = dh + _dot_nt(dpb, wi_ref[idx])
        dga_ref[...] += jnp.sum(dh * xh, axis=0, keepdims=True)
        dx_ref[...] = dx1 + _rms_bwd(dh * ga, xh, r0)

    acc_shapes = [(4, D_MODEL, E_A), (E_A, D_MODEL), (8, E_A), (1, D_MODEL)]
    return pl.pallas_call(
        body, grid=(n,), name="conv_bwd",
        in_specs=[rev(D_MODEL), rev(D_MODEL), rev(E_A), rev(E_A), rev(E_A), rev(E_A), prev8(E_A), prev8(E_A),
                  _const_spec((E_A, D_MODEL)), _const_spec((4, D_MODEL, E_A)), _const_spec((1, D_MODEL)),
                  _const_spec((8, E_A))],
        out_specs=[rev(D_MODEL)] + [_acc_spec(s) for s in acc_shapes],
        out_shape=[jax.ShapeDtypeStruct((t, D_MODEL), F32)] + [jax.ShapeDtypeStruct(s, F32) for s in acc_shapes],
        scratch_shapes=[pltpu.VMEM((8, E_A), F32)],
        compiler_params=_params("arbitrary"),
    )(dx1, x, b, c, u, g, c, u, w_out, w_in4, ga, cw)


WEIGHTS = ("a_norm", "a_w_in", "a_conv", "a_w_out", "kv_norm", "w_dkv", "ckv_norm", "w_ukv", "b_norm", "b_w_in",
           "b_q_norm", "b_w_uq", "b_w_out", "final_norm")
SHARD_SHAPES = {
    "a_norm": (1, 256), "a_w_in": (1, 1024, 1024), "a_conv": (1, 3, 256), "a_w_out": (1, 256, 1024),
    "kv_norm": (1024,), "w_dkv": (256, 288), "ckv_norm": (256,), "w_ukv": (256, 256), "b_norm": (1, 1024),
    "b_w_in": (1, 256, 896), "b_q_norm": (1, 384), "b_w_uq": (1, 384, 192), "b_w_out": (1, 512, 256),
    "final_norm": (1024,),
}
MATS = ("a_w_in", "a_w_out", "w_dkv", "w_ukv", "b_w_in", "b_w_uq", "b_w_out")
SMALL = ("a_norm", "a_conv", "kv_norm", "ckv_norm", "b_norm", "b_q_norm", "final_norm")
SMALL_FULL = {"a_norm": 1024, "a_conv": 3072, "kv_norm": 1024, "ckv_norm": 256, "b_norm": 1024, "b_q_norm": 384,
              "final_norm": 1024}
SMALL_ROWS = 8
LOSS_SLOT = sum(SMALL_FULL.values())


def _mat2d(name, a):
    return a.reshape(SHARD_SHAPES[name][-2:])


def _prep_first(g_win, gsmall):
    sm = gsmall.reshape(N_CHIPS, -1)
    a_conv = jnp.transpose(sm[:, 256:1024].reshape(N_CHIPS, CONV_WIDTH, 256), (1, 0, 2)).reshape(CONV_WIDTH, -1)
    return {"w_in4": g_win, "ga": sm[:, :256].reshape(1, -1), "cw": jnp.pad(a_conv, ((0, 8 - CONV_WIDTH), (0, 0)))}


def _prep_rest(gath, w):
    def cols(a):
        return jnp.transpose(a, (1, 0, 2)).reshape(a.shape[1], -1)

    def pad_heads(a, width):
        a = a.reshape(a.shape[0], N_HEADS, width)
        return jnp.pad(a, ((0, 0), (0, 0), (0, HEAD_PAD - width))).reshape(a.shape[0], QK_PAD)

    row = lambda a: a.reshape(1, -1).astype(F32)
    w_dkv = gath["w_dkv"].reshape(D_MODEL, KV_RANK + QK_ROPE)
    w_ukv = cols(gath["w_ukv"]).reshape(KV_RANK, N_HEADS, 2, QK_NOPE)
    return {
        "w_out": gath["a_w_out"].reshape(E_A, D_MODEL),
        "w_dkv": jnp.concatenate([w_dkv[:, :KV_RANK], jnp.zeros((D_MODEL, ROPE_LO), BF16), w_dkv[:, KV_RANK:],
                                  jnp.zeros((D_MODEL, HEAD_PAD - ROPE_LO - QK_ROPE), BF16)], axis=1),
        "w_uk": pad_heads(w_ukv[:, :, 0, :].reshape(KV_RANK, N_HEADS * QK_NOPE), QK_NOPE),
        "w_uv": w_ukv[:, :, 1, :].reshape(KV_RANK, E_B),
        "w_bin": gath["b_w_in"].reshape(D_MODEL, Q_RANK + E_B),
        "w_uq": pad_heads(cols(gath["b_w_uq"]), QK_NOPE + QK_ROPE),
        "w_bout": cols(gath["b_w_out"]),
        "gk": row(w["kv_norm"]), "gc": row(w["ckv_norm"]), "gb": row(w["b_norm"]),
        "gq": row(w["b_q_norm"]), "gf": row(w["final_norm"]),
    }


GRAD_MATS = ("w_in4", "w_out", "w_dkv", "w_uk", "w_uv", "w_bin", "w_uq", "w_bout")
GRAD_KIND = {"w_in4": "lead", "w_out": "row", "w_dkv": "row", "w_uk": "col", "w_uv": "col", "w_bin": "row",
             "w_uq": "col", "w_bout": "col"}


def _local_step(x, positions, tgt, w):
    bsz, seq, _ = x.shape
    t = bsz * seq
    x2d = x.reshape(t, D_MODEL)
    small = jnp.concatenate([w["a_norm"].reshape(-1), w["a_conv"].reshape(-1)]).reshape(8, LANES)
    rc, rs1, rs2, g_win, gsmall = _rope_tables_gather(positions.reshape(t, 1), _mat2d(MATS[0], w[MATS[0]]), small)
    wk = _prep_first(g_win, gsmall)
    (b, c, u, g, ym), gathered = _conv_fwd(x2d, seq, wk["ga"], wk["w_in4"], wk["cw"],
                                           [_mat2d(n, w[n]) for n in MATS[1:]])
    wk.update(_prep_rest(dict(zip(MATS[1:], gathered)), w))
    x1, q, k, v, gate, cq, ckv = _mid_fwd(x2d, ym, wk["w_out"], wk["gk"], wk["gb"], wk["w_dkv"], wk["gc"], wk["w_uk"],
                                          wk["w_uv"], wk["w_bin"], wk["gq"], wk["w_uq"], rc, rs1, rs2)
    o, lse = _attn_fwd(q, k, v, seq)
    dx2, do, dgate, dd, loss, dgf, dw_bout = _head_fwd_bwd(o, gate, x1, tgt.reshape(t, D_MODEL), wk["w_bout"], wk["gf"])
    dq, dk, dv = _attn_bwd(q, k, v, do, lse, dd, seq)
    dx1, dwq, dw_bin, dwk, dwv, dwd, dgq, dgc, dgk, dgb = _mid_bwd(
        dq, dk, dv, dgate, dx2, x1, cq, ckv, rc, rs1, rs2, wk["w_uq"], wk["w_bin"], wk["w_uk"], wk["w_uv"], wk["w_dkv"],
        wk["gq"], wk["gc"], wk["gk"], wk["gb"])
    dx, dw_in4, dw_out, dcw, dga = _conv_bwd(dx1, x2d, b, c, u, g, seq, wk["w_out"], wk["w_in4"], wk["ga"], wk["cw"])
    mats = {"w_in4": dw_in4, "w_out": dw_out, "w_dkv": dwd, "w_uk": dwk, "w_uv": dwv, "w_bin": dw_bin, "w_uq": dwq,
            "w_bout": dw_bout}
    small = {"a_norm": dga, "a_conv": dcw[:CONV_WIDTH], "kv_norm": dgk, "ckv_norm": dgc, "b_norm": dgb,
             "b_q_norm": dgq, "final_norm": dgf}
    return loss[0, 0], dx.reshape(bsz, seq, D_MODEL), mats, small


def _shard_grads(sh, svec):
    j0 = 2 * lax.axis_index("x") + lax.axis_index("y")
    flat = svec.reshape(-1)
    off, small = 0, {}
    for n in SMALL:
        small[n] = flat[off:off + SMALL_FULL[n]]
        off += SMALL_FULL[n]
    dwd, dwk, dwv, dwq = sh["w_dkv"], sh["w_uk"], sh["w_uv"], sh["w_uq"]
    w_ukv = jnp.stack([dwk.reshape(KV_RANK, 2, HEAD_PAD)[:, :, :QK_NOPE], dwv.reshape(KV_RANK, 2, V_HEAD)], axis=2)
    return {
        "a_norm": lax.dynamic_slice(small["a_norm"], (j0 * 256,), (256,)),
        "a_conv": lax.dynamic_slice(small["a_conv"].reshape(CONV_WIDTH, E_A), (0, j0 * 256), (CONV_WIDTH, 256)),
        "kv_norm": small["kv_norm"], "ckv_norm": small["ckv_norm"], "b_norm": small["b_norm"],
        "b_q_norm": small["b_q_norm"], "final_norm": small["final_norm"],
        "a_w_in": sh["w_in4"], "a_w_out": sh["w_out"],
        "w_dkv": jnp.concatenate([dwd[:, :KV_RANK], dwd[:, KV_RANK + ROPE_LO:KV_RANK + ROPE_LO + QK_ROPE]], axis=1),
        "w_ukv": w_ukv.reshape(KV_RANK, 2 * (QK_NOPE + V_HEAD)),
        "b_w_in": sh["w_bin"],
        "b_w_uq": dwq.reshape(Q_RANK, 2, HEAD_PAD)[:, :, :QK_NOPE + QK_ROPE].reshape(Q_RANK, -1),
        "b_w_out": sh["w_bout"],
    }


def _sub(ref, kind, j, cc):
    if kind == "lead":
        h = ref.shape[1] // 2
        return ref.at[j, pl.ds(pl.multiple_of(cc * h, 8), h), :]
    if kind == "row":
        rows = ref.shape[0] // N_CHIPS
        h = rows // 2
        return ref.at[pl.ds(pl.multiple_of(j * rows + cc * h, 8), h), :]
    cols = ref.shape[1] // N_CHIPS
    h = ref.shape[0] // 2
    return ref.at[pl.ds(pl.multiple_of(cc * h, 8), h), pl.ds(j * cols, cols)]


def _sub_shape(shape, kind):
    if kind == "lead":
        return (shape[1] // 2, shape[2])
    if kind == "row":
        return (shape[0] // N_CHIPS // 2, shape[1])
    return (shape[0] // 2, shape[1] // N_CHIPS)


def _pair_exchange(grads, kinds, vec):
    n = len(grads)

    def body(*refs):
        g, v_ref, out, o_ref = refs[:n], refs[n], refs[n + 1:2 * n + 1], refs[2 * n + 1]
        got, send, recv, send_v, recv_v = refs[2 * n + 2:]
        x, y, c = _place()
        me = 4 * x + 2 * y + c
        sibling = (x, y, 1 - c)
        copies = [_remote(_sub(g[k], kinds[k], j, 1 - c), out[k].at[j], send.at[k], recv.at[k], sibling)
                  for k in range(n) for j in range(N_CHIPS)]
        got[me] = v_ref[...]
        arrivals = []
        for d in range(1, N_DEV):
            px, py, pc = x ^ (d >> 2), y ^ ((d >> 1) & 1), c ^ (d & 1)
            copies.append(_remote(v_ref, got.at[me], send_v.at[d - 1], recv_v.at[d - 1], (px, py, pc)))
            arrivals.append(_remote(v_ref, got.at[4 * px + 2 * py + pc], send_v.at[d - 1], recv_v.at[d - 1],
                                    (px, py, pc)))
        for cp in copies:
            cp.start()
        for k in range(n):
            done = _remote(out[k], out[k], send.at[k], recv.at[k], sibling)
            done.wait_recv()
            done.wait_send()
        for cp in arrivals:
            cp.wait_recv()
            cp.wait_send()
        acc = got[0]
        for k in range(1, N_DEV):
            acc = acc + got[k]
        o_ref[...] = acc

    res = pl.pallas_call(
        body, name="pair_exchange",
        in_specs=[HBM] * n + [WHOLE], out_specs=[HBM] * n + [WHOLE],
        out_shape=[jax.ShapeDtypeStruct((N_CHIPS,) + _sub_shape(a.shape, kd), F32) for a, kd in zip(grads, kinds)]
        + [jax.ShapeDtypeStruct(vec.shape, vec.dtype)],
        scratch_shapes=[pltpu.VMEM((N_DEV,) + vec.shape, vec.dtype), pltpu.SemaphoreType.DMA((n,)),
                        pltpu.SemaphoreType.DMA((n,)), pltpu.SemaphoreType.DMA((N_DEV - 1,)),
                        pltpu.SemaphoreType.DMA((N_DEV - 1,))],
    )(*grads, vec)
    return res[:n], res[n]


def _chip_partials(grads, theirs, kinds):
    n = len(grads)

    def body(*refs):
        g, other, out, mine = refs[:n], refs[n:2 * n], refs[2 * n:3 * n], refs[3 * n:4 * n]
        sem = refs[4 * n]
        c = lax.axis_index("c")
        for k in range(n):
            for j in range(N_CHIPS):
                pltpu.make_async_copy(_sub(g[k], kinds[k], j, c), mine[k].at[j], sem.at[k]).start()
        for k in range(n):
            pltpu.make_async_copy(mine[k], mine[k], sem.at[k]).wait()
            out[k][...] = (mine[k][...] + other[k][...]).astype(BF16)

    shapes = [(N_CHIPS,) + _sub_shape(a.shape, kd) for a, kd in zip(grads, kinds)]
    return pl.pallas_call(
        body, name="chip_partials",
        in_specs=[HBM] * n + [WHOLE] * n, out_specs=[WHOLE] * n,
        out_shape=[jax.ShapeDtypeStruct(s, BF16) for s in shapes],
        scratch_shapes=[pltpu.VMEM(s, F32) for s in shapes] + [pltpu.SemaphoreType.DMA((n,))],
        compiler_params=_comm_params(),
    )(*grads, *theirs)


def _chip_exchange(parts):
    n = len(parts)

    def body(*refs):
        p, out = refs[:n], refs[n:2 * n]
        send, recv, local = refs[2 * n:]
        x, y, c = _place()
        j0 = 2 * x + y
        own = [pltpu.make_async_copy(p[k].at[j0], out[k].at[j0], local.at[k]) for k in range(n)]
        for cp in own:
            cp.start()
        sends, waits = [], []
        for i, (fx, fy) in enumerate(FLIPS):
            px, py = x ^ fx, y ^ fy
            pj = 2 * px + py
            for k in range(n):
                s = i * n + k
                sends.append(_remote(p[k].at[pj], out[k].at[j0], send.at[s], recv.at[s], (px, py, c)))
                waits.append(_remote(p[k].at[pj], out[k].at[pj], send.at[s], recv.at[s], (px, py, c)))
        for cp in sends:
            cp.start()
        for cp in waits:
            cp.wait_recv()
        for cp in sends:
            cp.wait_send()
        for cp in own:
            cp.wait()

    return pl.pallas_call(
        body, name="chip_exchange",
        in_specs=[HBM] * n, out_specs=[HBM] * n,
        out_shape=[jax.ShapeDtypeStruct(a.shape, a.dtype) for a in parts],
        scratch_shapes=[pltpu.SemaphoreType.DMA((3 * n,)), pltpu.SemaphoreType.DMA((3 * n,)),
                        pltpu.SemaphoreType.DMA((n,))],
    )(*parts)


def _reduce_swap(parts):
    n = len(parts)

    def body(*refs):
        p, out, red = refs[:n], refs[n:2 * n], refs[2 * n:3 * n]
        send, recv, local = refs[3 * n:]
        x, y, c = _place()
        sibling = (x, y, 1 - c)
        own, give, take = [], [], []
        for k in range(n):
            red[k][...] = ((p[k][0].astype(F32) + p[k][1].astype(F32)) + p[k][2].astype(F32)) + p[k][3].astype(F32)
            own.append(pltpu.make_async_copy(red[k], out[k].at[c], local.at[k]))
            give.append(_remote(red[k], out[k].at[c], send.at[k], recv.at[k], sibling))
            take.append(_remote(red[k], out[k].at[1 - c], send.at[k], recv.at[k], sibling))
            own[k].start()
            give[k].start()
        for k in range(n):
            take[k].wait_recv()
            give[k].wait_send()
            own[k].wait()

    return pl.pallas_call(
        body, name="reduce_swap",
        in_specs=[WHOLE] * n, out_specs=[HBM] * n,
        out_shape=[jax.ShapeDtypeStruct((2,) + a.shape[1:], F32) for a in parts],
        scratch_shapes=[pltpu.VMEM(a.shape[1:], F32) for a in parts]
        + [pltpu.SemaphoreType.DMA((n,))] * 3,
        compiler_params=_comm_params(),
    )(*parts)


def _adamw_math(w, g, m, v):
    m = ADAM_B1 * m + (1.0 - ADAM_B1) * g
    v = ADAM_B2 * v + (1.0 - ADAM_B2) * (g * g)
    m_hat = m / (1.0 - ADAM_B1 ** ADAM_STEP)
    v_hat = v / (1.0 - ADAM_B2 ** ADAM_STEP)
    return -ADAM_LR * (m_hat / (jnp.sqrt(v_hat) + ADAM_EPS) + ADAM_WD * w), m, v


def _adamw_tiled(w, g, m, v):
    rows, width = w.shape
    tm = rows // 4

    def body(w_ref, g_ref, m_ref, v_ref, go_ref, d_ref, mo_ref, vo_ref):
        g = g_ref[...]
        go_ref[...] = g
        d_ref[...], mo_ref[...], vo_ref[...] = _adamw_math(w_ref[...], g, m_ref[...], v_ref[...])

    spec = pl.BlockSpec((tm, width), lambda i: (i, 0))
    out = jax.ShapeDtypeStruct((rows, width), F32)
    return pl.pallas_call(
        body, grid=(rows // tm,), name="adamw_tiled",
        in_specs=[spec] * 4, out_specs=[spec] * 4, out_shape=[out] * 4,
        compiler_params=_params("parallel"),
    )(w, g, m, v)


def _adamw_many(ws, gs, ms, vs):
    n = len(ws)

    def body(*refs):
        for k in range(n):
            w_ref, g_ref, m_ref, v_ref = (refs[i * n + k] for i in range(4))
            go_ref, d_ref, mo_ref, vo_ref = (refs[(4 + i) * n + k] for i in range(4))
            g = g_ref[...]
            go_ref[...] = g
            d_ref[...], mo_ref[...], vo_ref[...] = _adamw_math(w_ref[...], g, m_ref[...], v_ref[...])

    outs = [jax.ShapeDtypeStruct(a.shape, F32) for a in ws]
    res = pl.pallas_call(
        body, name="adamw_many",
        in_specs=[WHOLE] * (4 * n), out_specs=[WHOLE] * (4 * n), out_shape=outs * 4,
        compiler_params=_comm_params(),
    )(*ws, *gs, *ms, *vs)
    return res[:n], res[n:2 * n], res[2 * n:3 * n], res[3 * n:]


def kernel(x, positions, a_norm, a_w_in, a_conv, a_w_out, kv_norm, w_dkv, ckv_norm, w_ukv, b_norm, b_w_in, b_q_norm, b_w_uq, b_w_out, final_norm, loss_target, m_a_norm, m_a_w_in, m_a_conv, m_a_w_out, m_kv_norm, m_w_dkv, m_ckv_norm, m_w_ukv, m_b_norm, m_b_w_in, m_b_q_norm, m_b_w_uq, m_b_w_out, m_final_norm, v_a_norm, v_a_w_in, v_a_conv, v_a_w_out, v_kv_norm, v_w_dkv, v_ckv_norm, v_w_ukv, v_b_norm, v_b_w_in, v_b_q_norm, v_b_w_uq, v_b_w_out, v_final_norm):
    w = dict(a_norm=a_norm, a_w_in=a_w_in, a_conv=a_conv, a_w_out=a_w_out, kv_norm=kv_norm, w_dkv=w_dkv,
             ckv_norm=ckv_norm, w_ukv=w_ukv, b_norm=b_norm, b_w_in=b_w_in, b_q_norm=b_q_norm, b_w_uq=b_w_uq,
             b_w_out=b_w_out, final_norm=final_norm)
    m = dict(a_norm=m_a_norm, a_w_in=m_a_w_in, a_conv=m_a_conv, a_w_out=m_a_w_out, kv_norm=m_kv_norm, w_dkv=m_w_dkv,
             ckv_norm=m_ckv_norm, w_ukv=m_w_ukv, b_norm=m_b_norm, b_w_in=m_b_w_in, b_q_norm=m_b_q_norm,
             b_w_uq=m_b_w_uq, b_w_out=m_b_w_out, final_norm=m_final_norm)
    v = dict(a_norm=v_a_norm, a_w_in=v_a_w_in, a_conv=v_a_conv, a_w_out=v_a_w_out, kv_norm=v_kv_norm, w_dkv=v_w_dkv,
             ckv_norm=v_ckv_norm, w_ukv=v_w_ukv, b_norm=v_b_norm, b_w_in=v_b_w_in, b_q_norm=v_b_q_norm,
             b_w_uq=v_b_w_uq, b_w_out=v_b_w_out, final_norm=v_final_norm)

    loss, dx, gmat, gsmall = _local_step(x, positions, loss_target, w)

    kinds = [GRAD_KIND[n] for n in GRAD_MATS]
    grads = [gmat[n] for n in GRAD_MATS]
    flat = jnp.concatenate([gsmall[n].reshape(-1) for n in SMALL] + [loss.reshape(1)])
    flat = jnp.pad(flat, (0, SMALL_ROWS * PACK_W - flat.shape[0])).reshape(SMALL_ROWS, PACK_W)
    theirs, svec = _pair_exchange(grads, kinds, flat)
    mine = _reduce_swap(_chip_exchange(_chip_partials(grads, theirs, kinds)))
    loss = svec.reshape(-1)[LOSS_SLOT]
    g = _shard_grads({n: a.reshape(-1, a.shape[-1]) for n, a in zip(GRAD_MATS, mine)}, svec)

    def two_d(a):
        return a.reshape(-1, PACK_W) if a.size % PACK_W == 0 else a.reshape(-1, a.shape[-1])

    big = "a_w_in"
    rest = [n for n in WEIGHTS if n != big]
    res_big = _adamw_tiled(*(two_d(t[big]) for t in (w, g, m, v)))
    res_rest = _adamw_many(*([two_d(t[n]) for n in rest] for t in (w, g, m, v)))
    out = {kind: dict(zip(rest, res_rest[i])) for i, kind in enumerate("gdmv")}
    for i, kind in enumerate("gdmv"):
        out[kind][big] = res_big[i]
    return (loss, dx) + tuple(out[kind][n].reshape(SHARD_SHAPES[n]) for kind in "gdmv" for n in WEIGHTS)
```

```python
import functools
import math

import numpy as np
import jax
import jax.numpy as jnp
from jax import lax
from jax.experimental import pallas as pl
from jax.experimental.pallas import tpu as pltpu

F32 = jnp.float32
BF16 = jnp.bfloat16

D_MODEL = 1024
E_A = 1024
CONV_WIDTH = 3
N_HEADS = 8
QK_NOPE = 64
QK_ROPE = 32
V_HEAD = 64
KV_RANK = 256
Q_RANK = 384
E_B = N_HEADS * V_HEAD
ROPE_THETA = 10000.0
SOFTMAX_SCALE = 1.0 / math.sqrt(QK_NOPE + QK_ROPE)
LOG2E = math.log2(math.e)
LN2 = math.log(2.0)
Q_PRESCALE = SOFTMAX_SCALE * LOG2E
EPS = 1e-6
HEAD_PAD = 128
QK_PAD = N_HEADS * HEAD_PAD
ROPE_LO = QK_NOPE
ROPE_HALF = QK_ROPE // 2
KR_PAD = KV_RANK + HEAD_PAD

ADAM_LR = 0.001
ADAM_B1 = 0.9
ADAM_B2 = 0.999
ADAM_EPS = 1e-08
ADAM_WD = 0.01
ADAM_STEP = 10

VMEM_LIMIT = 56 * 1024 * 1024
ROW_TILE = 512
CONV_BWD_TILE = 256
ATT_TILE_FWD = 1024
ATT_TILE = 512
LANES = 128
PACK_W = 1024

N_CHIPS = 4
N_DEV = 8


def _dot(a, b):
    return jnp.dot(a, b, preferred_element_type=F32)


def _dot_nt(a, b):
    return lax.dot_general(a, b, (((1,), (1,)), ((), ())), preferred_element_type=F32)


def _dot_tn(a, b):
    return lax.dot_general(a, b, (((0,), (0,)), ((), ())), preferred_element_type=F32)


def _rms(x):
    r = lax.rsqrt(jnp.mean(x * x, axis=-1, keepdims=True) + EPS)
    return r, x * r


def _rms_bwd(dxh, xh, r):
    return r * (dxh - xh * jnp.mean(dxh * xh, axis=-1, keepdims=True))


def _rope_fwd(a, c, s1, s2):
    return a * c + pltpu.roll(a, HEAD_PAD - ROPE_HALF, 1) * s1 + pltpu.roll(a, ROPE_HALF, 1) * s2


def _rope_bwd(g, c, s1, s2):
    return g * c + pltpu.roll(g * s1, ROPE_HALF, 1) + pltpu.roll(g * s2, HEAD_PAD - ROPE_HALF, 1)


def _sigmoid(x):
    return 1.0 / (1.0 + jnp.exp(-x))


def _row_spec(tm, n):
    return pl.BlockSpec((tm, n), lambda i: (i, 0))


def _const_spec(shape):
    nd = len(shape)
    return pl.BlockSpec(shape, lambda i: (0,) * nd, pipeline_mode=pl.Buffered(1))


def _acc_spec(shape):
    nd = len(shape)
    return pl.BlockSpec(shape, lambda i: (0,) * nd, pipeline_mode=pl.Buffered(1))


def _params(*sem):
    return pltpu.CompilerParams(dimension_semantics=sem, vmem_limit_bytes=VMEM_LIMIT)


MESH = pl.DeviceIdType.MESH
HBM = pl.BlockSpec(memory_space=pl.ANY)
WHOLE = pl.BlockSpec(memory_space=pltpu.VMEM)
FLIPS = ((1, 0), (0, 1), (1, 1))


def _place():
    return lax.axis_index("x"), lax.axis_index("y"), lax.axis_index("c")


def _remote(src, dst, send, recv, peer):
    return pltpu.make_async_remote_copy(src_ref=src, dst_ref=dst, send_sem=send, recv_sem=recv, device_id=peer,
                                        device_id_type=MESH)


def _comm_params():
    return pltpu.CompilerParams(vmem_limit_bytes=VMEM_LIMIT)


def _rope_consts():
    lane = np.arange(HEAD_PAD)
    first = (lane >= ROPE_LO) & (lane < ROPE_LO + ROPE_HALF)
    second = (lane >= ROPE_LO + ROPE_HALF) & (lane < ROPE_LO + QK_ROPE)
    f = np.where(first, lane - ROPE_LO, np.where(second, lane - ROPE_LO - ROPE_HALF, 0))
    inv = np.float32(ROPE_THETA) ** (-(2 * f).astype(np.float32) / np.float32(QK_ROPE))
    out = np.zeros((8, HEAD_PAD), np.float32)
    out[0] = inv
    out[1] = first
    out[2] = second
    out[3] = lane < ROPE_LO
    return jnp.asarray(out)


def _gather_plan(w, sm, outs, osm, bf, sems):
    n = len(w)
    send_i, recv_i, send_d, recv_d, send_s, recv_s, local = sems
    x, y, c = _place()
    j0 = 2 * x + y
    sibling = (x, y, 1 - c)
    own = [pltpu.make_async_copy(bf[k], outs[k].at[j0], local.at[k]) for k in range(n)]
    if sm is not None:
        own.append(pltpu.make_async_copy(sm, osm.at[j0], local.at[n]))

    def half(k, cc):
        h = w[k].shape[0] // 2
        return pl.ds(pl.multiple_of(cc * h, 16), h)

    sends, arrivals, forwards, fwaits = [], [], [], []
    for i, (fx, fy) in enumerate(FLIPS):
        px, py = x ^ fx, y ^ fy
        pj = 2 * px + py
        for k in range(n):
            s = i * n + k
            sends.append(_remote(bf[k].at[half(k, c)], outs[k].at[j0, half(k, c)], send_i.at[s], recv_i.at[s],
                                 (px, py, c)))
            arrivals.append(_remote(bf[k].at[half(k, c)], outs[k].at[pj, half(k, c)], send_i.at[s], recv_i.at[s],
                                    (px, py, c)))
            forwards.append(_remote(outs[k].at[pj, half(k, c)], outs[k].at[pj, half(k, c)], send_d.at[s],
                                    recv_d.at[s], sibling))
            fwaits.append(_remote(outs[k].at[pj, half(k, 1 - c)], outs[k].at[pj, half(k, 1 - c)], send_d.at[s],
                                  recv_d.at[s], sibling))
        if sm is not None:
            sends.append(_remote(sm, osm.at[j0], send_s.at[i], recv_s.at[i], (px, py, c)))
            fwaits.append(_remote(sm, osm.at[pj], send_s.at[i], recv_s.at[i], (px, py, c)))
    return own, sends, arrivals, forwards, fwaits


def _gather_begin(w, bf, plan):
    own, sends, _, _, _ = plan
    for k in range(len(w)):
        bf[k][...] = w[k][...].astype(BF16)
    for cp in own + sends:
        cp.start()


def _gather_end(plan):
    own, sends, arrivals, forwards, fwaits = plan
    for cp, fwd in zip(arrivals, forwards):
        cp.wait_recv()
        fwd.start()
    for cp in fwaits:
        cp.wait_recv()
    for cp in sends + forwards:
        cp.wait_send()
    for cp in own:
        cp.wait()


def _gather_sems(n, with_small):
    return ([pltpu.SemaphoreType.DMA((3 * n,))] * 4 + [pltpu.SemaphoreType.DMA((3,))] * 2
            + [pltpu.SemaphoreType.DMA((n + (1 if with_small else 0),))])


def _rope_tables_gather(pos_col, w_in, small):
    t = pos_col.shape[0]
    tm = min(ROW_TILE, t)
    steps = t // tm

    def body(p_ref, k_ref, w_ref, sm_ref, c_ref, s1_ref, s2_ref, out_ref, osm_ref, bf_ref, *sems):
        plan = _gather_plan([w_ref], sm_ref, [out_ref], osm_ref, [bf_ref], sems)
        i = pl.program_id(0)

        @pl.when(i == 0)
        def _():
            _gather_begin([w_ref], [bf_ref], plan)

        inv, first, second, nope = k_ref[0:1, :], k_ref[1:2, :], k_ref[2:3, :], k_ref[3:4, :]
        ang = p_ref[...].astype(F32) * inv
        cs, sn = jnp.cos(ang), jnp.sin(ang)
        c_ref[...] = cs * (first + second) + nope
        s1_ref[...] = -sn * first
        s2_ref[...] = sn * second

        @pl.when(i == steps - 1)
        def _():
            _gather_end(plan)

    out = jax.ShapeDtypeStruct((t, HEAD_PAD), F32)
    return pl.pallas_call(
        body, grid=(steps,), name="rope_tables_gather",
        in_specs=[_row_spec(tm, 1), _const_spec((8, HEAD_PAD)), WHOLE, WHOLE],
        out_specs=[_row_spec(tm, HEAD_PAD)] * 3 + [HBM, HBM],
        out_shape=[out] * 3 + [jax.ShapeDtypeStruct((N_CHIPS,) + w_in.shape, BF16),
                               jax.ShapeDtypeStruct((N_CHIPS,) + small.shape, small.dtype)],
        scratch_shapes=[pltpu.VMEM(w_in.shape, BF16)] + _gather_sems(1, True),
        compiler_params=_params("arbitrary"),
    )(pos_col, _rope_consts(), w_in, small)


def _shift_down(v, prev, row):
    p1, p2 = prev[7:8, :], prev[6:7, :]
    v1 = jnp.where(row == 0, p1, pltpu.roll(v, 1, 0))
    v2 = jnp.where(row == 0, p2, jnp.where(row == 1, p1, pltpu.roll(v, 2, 0)))
    return v1, v2


def _conv_fwd(x, seq, ga, w_in4, cw, shards):
    t = x.shape[0]
    tm = min(ROW_TILE, seq)
    tiles_per_seq = seq // tm
    steps = t // tm
    n = len(shards)

    def body(x_ref, ga_ref, w_ref, cw_ref, *rest):
        sh, (b_ref, c_ref, u_ref, g_ref, ym_ref) = rest[:n], rest[n:n + 5]
        outs, carry_ref, bf, sems = rest[n + 5:2 * n + 5], rest[2 * n + 5], rest[2 * n + 6:3 * n + 6], rest[3 * n + 6:]
        plan = _gather_plan(sh, None, outs, None, bf, sems)
        i = pl.program_id(0)

        @pl.when(i == 0)
        def _():
            _gather_begin(sh, bf, plan)

        @pl.when(i % tiles_per_seq == 0)
        def _():
            carry_ref[...] = jnp.zeros_like(carry_ref)

        _, xh = _rms(x_ref[...])
        h = (xh * ga_ref[...]).astype(BF16)
        b, c, u, g = (_dot(h, w_ref[j]) for j in range(4))
        v = c * u
        row = lax.broadcasted_iota(jnp.int32, (tm, 1), 0)
        v1, v2 = _shift_down(v, carry_ref[...], row)
        carry_ref[...] = v[tm - 8:tm, :]
        cv = cw_ref[2:3, :] * v + cw_ref[1:2, :] * v1 + cw_ref[0:1, :] * v2
        b_ref[...] = b.astype(BF16)
        c_ref[...] = c.astype(BF16)
        u_ref[...] = u.astype(BF16)
        g_ref[...] = g.astype(BF16)
        ym_ref[...] = (g * _sigmoid(g) * b * cv).astype(BF16)

        @pl.when(i == steps - 1)
        def _():
            _gather_end(plan)

    out = jax.ShapeDtypeStruct((t, E_A), BF16)
    res = pl.pallas_call(
        body, grid=(steps,), name="conv_fwd",
        in_specs=[_row_spec(tm, D_MODEL), _const_spec((1, D_MODEL)), _const_spec((4, D_MODEL, E_A)),
                  _const_spec((8, E_A))] + [WHOLE] * n,
        out_specs=[_row_spec(tm, E_A)] * 5 + [HBM] * n,
        out_shape=[out] * 5 + [jax.ShapeDtypeStruct((N_CHIPS,) + a.shape, BF16) for a in shards],
        scratch_shapes=[pltpu.VMEM((8, E_A), F32)] + [pltpu.VMEM(a.shape, BF16) for a in shards]
        + _gather_sems(n, False),
        compiler_params=_params("arbitrary"),
    )(x, ga, w_in4, cw, *shards)
    return res[:5], res[5:]


def _mid_fwd(x, ym, w_out, gk, gb, w_dkv, gc, w_uk, w_uv, w_bin, gq, w_uq, rc, rs1, rs2):
    t = x.shape[0]
    tm = min(ROW_TILE, t)

    def body(x_ref, ym_ref, wo_ref, gk_ref, gb_ref, wd_ref, gc_ref, wk_ref, wv_ref, wi_ref, gq_ref, wq_ref,
             c_ref, s1_ref, s2_ref, x1_ref, q_ref, k_ref, v_ref, gate_ref, cq_ref, ckv_ref):
        cb, s1b, s2b = c_ref[...], s1_ref[...], s2_ref[...]
        x1 = x_ref[...] + _dot(ym_ref[...], wo_ref[...])
        x1_ref[...] = x1
        _, xh = _rms(x1)
        hk = (xh * gk_ref[...]).astype(BF16)
        h1 = (xh * gb_ref[...]).astype(BF16)

        ckr = _dot(hk, wd_ref[...])
        ckv_raw = ckr[:, :KV_RANK]
        ckv_ref[...] = ckv_raw.astype(BF16)
        _, ch = _rms(ckv_raw)
        ckv = (ch * gc_ref[...]).astype(BF16)
        kr = _rope_fwd(ckr[:, KV_RANK:], cb, s1b, s2b)
        kn = _dot(ckv, wk_ref[...])
        for h in range(N_HEADS):
            sl = slice(h * HEAD_PAD, (h + 1) * HEAD_PAD)
            k_ref[:, sl] = (kn[:, sl] + kr).astype(BF16)
        v_ref[...] = _dot(ckv, wv_ref[...]).astype(BF16)

        pb = _dot(h1, wi_ref[...])
        cq = pb[:, :Q_RANK]
        cq_ref[...] = cq.astype(BF16)
        gate_ref[...] = pb[:, Q_RANK:].astype(BF16)
        _, cqh = _rms(cq)
        q = _dot((cqh * gq_ref[...]).astype(BF16), wq_ref[...])
        for h in range(N_HEADS):
            sl = slice(h * HEAD_PAD, (h + 1) * HEAD_PAD)
            q_ref[:, sl] = (_rope_fwd(q[:, sl], cb, s1b, s2b) * Q_PRESCALE).astype(BF16)

    def sds(n, dt):
        return jax.ShapeDtypeStruct((t, n), dt)

    return pl.pallas_call(
        body, grid=(t // tm,), name="mid_fwd",
        in_specs=[_row_spec(tm, D_MODEL), _row_spec(tm, E_A), _const_spec((E_A, D_MODEL)),
                  _const_spec((1, D_MODEL)), _const_spec((1, D_MODEL)), _const_spec((D_MODEL, KR_PAD)),
                  _const_spec((1, KV_RANK)), _const_spec((KV_RANK, QK_PAD)), _const_spec((KV_RANK, E_B)),
                  _const_spec((D_MODEL, Q_RANK + E_B)), _const_spec((1, Q_RANK)), _const_spec((Q_RANK, QK_PAD)),
                  _row_spec(tm, HEAD_PAD), _row_spec(tm, HEAD_PAD), _row_spec(tm, HEAD_PAD)],
        out_specs=[_row_spec(tm, D_MODEL), _row_spec(tm, QK_PAD), _row_spec(tm, QK_PAD), _row_spec(tm, E_B),
                   _row_spec(tm, E_B), _row_spec(tm, Q_RANK), _row_spec(tm, KV_RANK)],
        out_shape=[sds(D_MODEL, F32), sds(QK_PAD, BF16), sds(QK_PAD, BF16), sds(E_B, BF16), sds(E_B, BF16),
                   sds(Q_RANK, BF16), sds(KV_RANK, BF16)],
        compiler_params=_params("parallel"),
    )(x, ym, w_out, gk, gb, w_dkv, gc, w_uk, w_uv, w_bin, gq, w_uq, rc, rs1, rs2)


def _pair_specs(seq):
    qk = pl.BlockSpec((seq, 2 * HEAD_PAD), lambda b, p: (b, p))
    vo = pl.BlockSpec((seq, 2 * V_HEAD), lambda b, p: (b, p))
    st = pl.BlockSpec((None, 2, seq), lambda b, p: (p, 0, b))
    return qk, vo, st


def _attn_fwd(q, k, v, seq):
    t = q.shape[0]
    tq = min(ATT_TILE_FWD, seq)
    nq = seq // tq

    def body(q_ref, k_ref, v_ref, o_ref, lse_ref, m_scr, l_scr, acc_scr):
        lane = lax.broadcasted_iota(jnp.int32, (tq, 2 * V_HEAD), 1)

        def q_step(qi, _):
            q0 = pl.multiple_of(qi * tq, tq)
            m_scr[...] = jnp.full(m_scr.shape, -jnp.inf, F32)
            l_scr[...] = jnp.zeros_like(l_scr)
            acc_scr[...] = jnp.zeros_like(acc_scr)

            def block(q_lo, q_n, k0, k_n, masked):
                rows = slice(q_lo, q_lo + q_n)
                vt = v_ref[pl.ds(k0, k_n), :]
                for hh in range(2):
                    hs = slice(hh * HEAD_PAD, (hh + 1) * HEAD_PAD)
                    s = _dot_nt(q_ref[pl.ds(q0 + q_lo, q_n), hs], k_ref[pl.ds(k0, k_n), hs])
                    if masked:
                        row = lax.broadcasted_iota(jnp.int32, (q_n, k_n), 0)
                        col = lax.broadcasted_iota(jnp.int32, (q_n, k_n), 1)
                        s = jnp.where(col <= row, s, -jnp.inf)
                    m_old = m_scr[hh, rows]
                    m_new = jnp.maximum(m_old, jnp.max(s, axis=-1, keepdims=True))
                    alpha = jnp.exp2(m_old - m_new)
                    ps = [jnp.exp2(s[:, j * LANES:(j + 1) * LANES] - m_new) for j in range(k_n // LANES)]
                    l_scr[hh, rows] = alpha * l_scr[hh, rows] + functools.reduce(lambda a, b: a + b, ps)
                    p = jnp.concatenate(ps, axis=-1).astype(BF16)
                    acc_scr[hh, rows] = alpha * acc_scr[hh, rows] + _dot(p, vt)
                    m_scr[hh, rows] = m_new

            def k_step(ki, _):
                block(0, tq, pl.multiple_of(ki * tq, tq), tq, False)
                return 0

            lax.fori_loop(0, qi, k_step, 0)
            half = tq // 2
            block(0, tq, q0, half, True)
            block(half, half, q0 + half, half, True)
            l0 = jnp.sum(l_scr[0], axis=-1, keepdims=True)
            l1 = jnp.sum(l_scr[1], axis=-1, keepdims=True)
            o_ref[pl.ds(q0, tq), :] = jnp.where(lane < V_HEAD, acc_scr[0] / l0, acc_scr[1] / l1).astype(BF16)
            stats = jnp.where(lane == 0, m_scr[0] + jnp.log2(l0), m_scr[1] + jnp.log2(l1)).T
            lse_ref[:, pl.ds(q0, tq)] = stats[0:2, :]
            return 0

        lax.fori_loop(0, nq, q_step, 0)

    qk, vo, st = _pair_specs(seq)
    return pl.pallas_call(
        body, grid=(t // seq, N_HEADS // 2), name="attn_fwd",
        in_specs=[qk, qk, vo], out_specs=[vo, st],
        out_shape=[jax.ShapeDtypeStruct((t, E_B), BF16), jax.ShapeDtypeStruct((N_HEADS // 2, 2, t), F32)],
        scratch_shapes=[pltpu.VMEM((2, tq, LANES), F32), pltpu.VMEM((2, tq, LANES), F32),
                        pltpu.VMEM((2, tq, 2 * V_HEAD), F32)],
        compiler_params=_params("parallel", "parallel"),
    )(q, k, v)


def _head_fwd_bwd(o, gate, x1, tgt, w_bout, gf):
    t = o.shape[0]
    tm = min(ROW_TILE, t)

    def body(o_ref, gate_ref, x1_ref, tgt_ref, w_ref, gf_ref,
             dx2_ref, do_ref, dgate_ref, dd_ref, loss_ref, dgf_ref, dw_ref):
        @pl.when(pl.program_id(0) == 0)
        def _():
            loss_ref[...] = jnp.zeros_like(loss_ref)
            dgf_ref[...] = jnp.zeros_like(dgf_ref)
            dw_ref[...] = jnp.zeros_like(dw_ref)

        o = o_ref[...].astype(F32)
        gt = gate_ref[...].astype(F32)
        sg = _sigmoid(gt)
        silu = gt * sg
        z = (o * silu).astype(BF16)
        x2 = x1_ref[...] + _dot(z, w_ref[...])
        r2, xh2 = _rms(x2)
        gf = gf_ref[...]
        err = xh2 * gf - tgt_ref[...]
        loss_ref[...] += 0.5 * jnp.sum(jnp.mean(err * err, axis=-1, keepdims=True))
        dy = err * (1.0 / D_MODEL)
        dgf_ref[...] += jnp.sum(dy * xh2, axis=0, keepdims=True)
        dx2 = _rms_bwd(dy * gf, xh2, r2)
        dx2_ref[...] = dx2
        dx2b = dx2.astype(BF16)
        dw_ref[...] += _dot_tn(z, dx2b)
        dz = _dot_nt(dx2b, w_ref[...])
        do = dz * silu
        do_ref[...] = do.astype(BF16)
        dgate_ref[...] = (dz * o * (sg * (1.0 + gt * (1.0 - sg)))).astype(BF16)
        prod = do * o
        lane = lax.broadcasted_iota(jnp.int32, (tm, 2 * V_HEAD), 1)
        cols = jnp.zeros((tm, LANES), F32)
        for p in range(N_HEADS // 2):
            blk = prod[:, p * 2 * V_HEAD:(p + 1) * 2 * V_HEAD]
            d0 = jnp.sum(jnp.where(lane < V_HEAD, blk, 0.0), axis=-1, keepdims=True)
            d1 = jnp.sum(jnp.where(lane < V_HEAD, 0.0, blk), axis=-1, keepdims=True)
            cols = jnp.where(lane == 2 * p, d0, jnp.where(lane == 2 * p + 1, d1, cols))
        rows = cols.T
        for h in range(N_HEADS):
            dd_ref[h // 2, h % 2:h % 2 + 1, :] = rows[h:h + 1, :]

    return pl.pallas_call(
        body, grid=(t // tm,), name="head_fwd_bwd",
        in_specs=[_row_spec(tm, E_B), _row_spec(tm, E_B), _row_spec(tm, D_MODEL), _row_spec(tm, D_MODEL),
                  _const_spec((E_B, D_MODEL)), _const_spec((1, D_MODEL))],
        out_specs=[_row_spec(tm, D_MODEL), _row_spec(tm, E_B), _row_spec(tm, E_B),
                   pl.BlockSpec((N_HEADS // 2, 2, tm), lambda i: (0, 0, i)),
                   _acc_spec((1, 1)), _acc_spec((1, D_MODEL)), _acc_spec((E_B, D_MODEL))],
        out_shape=[jax.ShapeDtypeStruct((t, D_MODEL), F32), jax.ShapeDtypeStruct((t, E_B), BF16),
                   jax.ShapeDtypeStruct((t, E_B), BF16), jax.ShapeDtypeStruct((N_HEADS // 2, 2, t), F32),
                   jax.ShapeDtypeStruct((1, 1), F32), jax.ShapeDtypeStruct((1, D_MODEL), F32),
                   jax.ShapeDtypeStruct((E_B, D_MODEL), F32)],
        compiler_params=_params("arbitrary"),
    )(o, gate, x1, tgt, w_bout, gf)


def _attn_bwd(q, k, v, do, lse, dd, seq):
    t = q.shape[0]
    tq = min(ATT_TILE, seq)
    nq = seq // tq

    def body(q_ref, k_ref, v_ref, do_ref, lse_ref, dd_ref, dq_ref, dk_ref, dv_ref, dq_acc, dk_acc, dv_acc):
        dq_acc[...] = jnp.zeros_like(dq_acc)

        def k_step(ki, _):
            k0 = pl.multiple_of(ki * tq, tq)
            dk_acc[...] = jnp.zeros_like(dk_acc)
            dv_acc[...] = jnp.zeros_like(dv_acc)

            def block(k_lo, k_n, q0, q_n, masked):
                rows = slice(k_lo, k_lo + k_n)
                lane = lax.broadcasted_iota(jnp.int32, (q_n, 2 * V_HEAD), 1)
                vt = v_ref[pl.ds(k0 + k_lo, k_n), :]
                do_pair = do_ref[pl.ds(q0, q_n), :]
                for hh in range(2):
                    hs = slice(hh * HEAD_PAD, (hh + 1) * HEAD_PAD)
                    kt = k_ref[pl.ds(k0 + k_lo, k_n), hs]
                    qt = q_ref[pl.ds(q0, q_n), hs]
                    mine = (lane < V_HEAD) if hh == 0 else (lane >= V_HEAD)
                    do_h = jnp.where(mine, do_pair, jnp.zeros((), BF16))
                    st = _dot_nt(kt, qt)
                    if masked:
                        krow = lax.broadcasted_iota(jnp.int32, (k_n, q_n), 0)
                        qcol = lax.broadcasted_iota(jnp.int32, (k_n, q_n), 1)
                        st = jnp.where(krow <= qcol, st, -jnp.inf)
                    pt = jnp.exp2(st - lse_ref[hh:hh + 1, pl.ds(q0, q_n)])
                    dpt = _dot_nt(vt, do_h)
                    dst = (pt * (dpt - dd_ref[hh:hh + 1, pl.ds(q0, q_n)])).astype(BF16)
                    dv_acc[rows, :] += _dot(pt.astype(BF16), do_h)
                    dk_acc[rows, hs] += _dot(dst, qt)
                    dq_acc[pl.ds(q0, q_n), hs] += _dot_tn(dst, kt)

            def q_step(qi, _):
                block(0, tq, pl.multiple_of(qi * tq, tq), tq, False)
                return 0

            half = tq // 2
            block(0, half, k0, tq, True)
            block(half, half, pl.multiple_of(k0 + half, half), half, True)
            lax.fori_loop(ki + 1, nq, q_step, 0)
            dk_ref[pl.ds(k0, tq), :] = (dk_acc[...] * LN2).astype(BF16)
            dv_ref[pl.ds(k0, tq), :] = dv_acc[...].astype(BF16)
            return 0

        lax.fori_loop(0, nq, k_step, 0)
        dq_ref[...] = (dq_acc[...] * SOFTMAX_SCALE).astype(BF16)

    qk, vo, st = _pair_specs(seq)
    return pl.pallas_call(
        body, grid=(t // seq, N_HEADS // 2), name="attn_bwd",
        in_specs=[qk, qk, vo, vo, st, st], out_specs=[qk, qk, vo],
        out_shape=[jax.ShapeDtypeStruct((t, QK_PAD), BF16), jax.ShapeDtypeStruct((t, QK_PAD), BF16),
                   jax.ShapeDtypeStruct((t, E_B), BF16)],
        scratch_shapes=[pltpu.VMEM((seq, 2 * HEAD_PAD), F32), pltpu.VMEM((tq, 2 * HEAD_PAD), F32),
                        pltpu.VMEM((tq, 2 * V_HEAD), F32)],
        compiler_params=_params("parallel", "parallel"),
    )(q, k, v, do, lse, dd)


def _mid_bwd(dq, dk, dv, dgate, dx2, x1, cq, ckv, rc, rs1, rs2, w_uq, w_bin, w_uk, w_uv, w_dkv, gq, gc, gk, gb):
    t = dq.shape[0]
    tm = min(ROW_TILE, t)

    def body(dq_ref, dk_ref, dv_ref, dgate_ref, dx2_ref, x1_ref, cq_ref, ckv_ref, c_ref, s1_ref, s2_ref,
             wq_ref, wi_ref, wk_ref, wv_ref, wd_ref, gq_ref, gc_ref, gk_ref, gb_ref,
             dx1_ref, dwq_ref, dwi_ref, dwk_ref, dwv_ref, dwd_ref, dgq_ref, dgc_ref, dgk_ref, dgb_ref):
        @pl.when(pl.program_id(0) == 0)
        def _():
            for ref in (dwq_ref, dwi_ref, dwk_ref, dwv_ref, dwd_ref, dgq_ref, dgc_ref, dgk_ref, dgb_ref):
                ref[...] = jnp.zeros_like(ref)

        cb, s1b, s2b = c_ref[...], s1_ref[...], s2_ref[...]
        r1, xh = _rms(x1_ref[...])
        gk, gb, gq, gc = gk_ref[...], gb_ref[...], gq_ref[...], gc_ref[...]
        hk = (xh * gk).astype(BF16)
        h1 = (xh * gb).astype(BF16)

        dqs = [_rope_bwd(dq_ref[:, h * HEAD_PAD:(h + 1) * HEAD_PAD].astype(F32), cb, s1b, s2b)
               for h in range(N_HEADS)]
        dqb = jnp.concatenate(dqs, axis=-1).astype(BF16)
        rq, cqh = _rms(cq_ref[...].astype(F32))
        dwq_ref[...] += _dot_tn((cqh * gq).astype(BF16), dqb)
        dcqn = _dot_nt(dqb, wq_ref[...])
        dgq_ref[...] += jnp.sum(dcqn * cqh, axis=0, keepdims=True)
        dcq = _rms_bwd(dcqn * gq, cqh, rq)
        dpb = jnp.concatenate([dcq.astype(BF16), dgate_ref[...]], axis=-1)
        dwi_ref[...] += _dot_tn(h1, dpb)
        dh1 = _dot_nt(dpb, wi_ref[...])

        rcv, ch = _rms(ckv_ref[...].astype(F32))
        ckvn = (ch * gc).astype(BF16)
        dkb, dvb = dk_ref[...], dv_ref[...]
        dwk_ref[...] += _dot_tn(ckvn, dkb)
        dwv_ref[...] += _dot_tn(ckvn, dvb)
        dckv = _dot_nt(dkb, wk_ref[...]) + _dot_nt(dvb, wv_ref[...])
        dgc_ref[...] += jnp.sum(dckv * ch, axis=0, keepdims=True)
        dckv_raw = _rms_bwd(dckv * gc, ch, rcv)
        dkr = dk_ref[:, 0:HEAD_PAD].astype(F32)
        for h in range(1, N_HEADS):
            dkr = dkr + dk_ref[:, h * HEAD_PAD:(h + 1) * HEAD_PAD].astype(F32)
        dkr = _rope_bwd(dkr, cb, s1b, s2b)
        dckr = jnp.concatenate([dckv_raw, dkr], axis=-1).astype(BF16)
        dwd_ref[...] += _dot_tn(hk, dckr)
        dhk = _dot_nt(dckr, wd_ref[...])

        dgb_ref[...] += jnp.sum(dh1 * xh, axis=0, keepdims=True)
        dgk_ref[...] += jnp.sum(dhk * xh, axis=0, keepdims=True)
        dx1_ref[...] = dx2_ref[...] + _rms_bwd(dh1 * gb + dhk * gk, xh, r1)

    acc_shapes = [(Q_RANK, QK_PAD), (D_MODEL, Q_RANK + E_B), (KV_RANK, QK_PAD), (KV_RANK, E_B), (D_MODEL, KR_PAD),
                  (1, Q_RANK), (1, KV_RANK), (1, D_MODEL), (1, D_MODEL)]
    return pl.pallas_call(
        body, grid=(t // tm,), name="mid_bwd",
        in_specs=[_row_spec(tm, QK_PAD), _row_spec(tm, QK_PAD), _row_spec(tm, E_B), _row_spec(tm, E_B),
                  _row_spec(tm, D_MODEL), _row_spec(tm, D_MODEL), _row_spec(tm, Q_RANK), _row_spec(tm, KV_RANK),
                  _row_spec(tm, HEAD_PAD), _row_spec(tm, HEAD_PAD), _row_spec(tm, HEAD_PAD),
                  _const_spec((Q_RANK, QK_PAD)), _const_spec((D_MODEL, Q_RANK + E_B)),
                  _const_spec((KV_RANK, QK_PAD)), _const_spec((KV_RANK, E_B)), _const_spec((D_MODEL, KR_PAD)),
                  _const_spec((1, Q_RANK)), _const_spec((1, KV_RANK)), _const_spec((1, D_MODEL)),
                  _const_spec((1, D_MODEL))],
        out_specs=[_row_spec(tm, D_MODEL)] + [_acc_spec(s) for s in acc_shapes],
        out_shape=[jax.ShapeDtypeStruct((t, D_MODEL), F32)] + [jax.ShapeDtypeStruct(s, F32) for s in acc_shapes],
        compiler_params=_params("arbitrary"),
    )(dq, dk, dv, dgate, dx2, x1, cq, ckv, rc, rs1, rs2, w_uq, w_bin, w_uk, w_uv, w_dkv, gq, gc, gk, gb)


def _conv_bwd(dx1, x, b, c, u, g, seq, w_out, w_in4, ga, cw):
    t = x.shape[0]
    tm = min(CONV_BWD_TILE, seq)
    tiles_per_seq = seq // tm
    n = t // tm
    halo = tm // 8

    def tile(i):
        return n - 1 - i

    def rev(width):
        return pl.BlockSpec((tm, width), lambda i: (tile(i), 0))

    def prev8(width):
        return pl.BlockSpec((8, width), lambda i: (jnp.maximum(tile(i) * halo - 1, 0), 0))

    def body(dx1_ref, x_ref, b_ref, c_ref, u_ref, g_ref, cp_ref, up_ref, wo_ref, wi_ref, ga_ref, cw_ref,
             dx_ref, dwi_ref, dwo_ref, dcw_ref, dga_ref, carry_ref):
        i = pl.program_id(0)
        j = tile(i)

        @pl.when(i == 0)
        def _():
            for ref in (dwi_ref, dwo_ref, dcw_ref, dga_ref):
                ref[...] = jnp.zeros_like(ref)

        @pl.when(j % tiles_per_seq == tiles_per_seq - 1)
        def _():
            carry_ref[...] = jnp.zeros_like(carry_ref)

        dx1 = dx1_ref[...]
        dx1b = dx1.astype(BF16)
        b, c, u, g = (r[...].astype(F32) for r in (b_ref, c_ref, u_ref, g_ref))
        v = c * u
        first = (j % tiles_per_seq == 0).astype(F32)
        vprev = cp_ref[...].astype(F32) * up_ref[...].astype(F32) * (1.0 - first)
        row = lax.broadcasted_iota(jnp.int32, (tm, 1), 0)
        v1, v2 = _shift_down(v, vprev, row)
        w0, w1, w2 = cw_ref[0:1, :], cw_ref[1:2, :], cw_ref[2:3, :]
        cv = w2 * v + w1 * v1 + w0 * v2
        sg = _sigmoid(g)
        silu = g * sg
        ym = (silu * b * cv).astype(BF16)
        dwo_ref[...] += _dot_tn(ym, dx1b)
        dym = _dot_nt(dx1b, wo_ref[...])
        db = dym * silu * cv
        dcv = dym * silu * b
        dg = dym * b * cv * (sg * (1.0 + g * (1.0 - sg)))

        nxt = carry_ref[...]
        n0, n1 = nxt[0:1, :], nxt[1:2, :]
        d1 = jnp.where(row == tm - 1, n0, pltpu.roll(dcv, tm - 1, 0))
        d2 = jnp.where(row == tm - 1, n1, jnp.where(row == tm - 2, n0, pltpu.roll(dcv, tm - 2, 0)))
        carry_ref[...] = dcv[0:8, :]
        dv = w2 * dcv + w1 * d1 + w0 * d2
        dcw_ref[0:1, :] += jnp.sum(dcv * v2, axis=0, keepdims=True)
        dcw_ref[1:2, :] += jnp.sum(dcv * v1, axis=0, keepdims=True)
        dcw_ref[2:3, :] += jnp.sum(dcv * v, axis=0, keepdims=True)

        r0, xh = _rms(x_ref[...])
        ga = ga_ref[...]
        h = (xh * ga).astype(BF16)
        dh = jnp.zeros((tm, D_MODEL), F32)
        for idx, dpart in enumerate((db, dv * u, dv * c, dg)):
            dpb = dpart.astype(BF16)
            dwi_ref[idx] += _dot_tn(h, dpb)
            dh = dh + _dot_nt(dpb, wi_ref[idx])
        dga_ref[...] += jnp.sum(dh * xh, axis=0, keepdims=True)
        dx_ref[...] = dx1 + _rms_bwd(dh * ga, xh, r0)

    acc_shapes = [(4, D_MODEL, E_A), (E_A, D_MODEL), (8, E_A), (1, D_MODEL)]
    return pl.pallas_call(
        body, grid=(n,), name="conv_bwd",
        in_specs=[rev(D_MODEL), rev(D_MODEL), rev(E_A), rev(E_A), rev(E_A), rev(E_A), prev8(E_A), prev8(E_A),
                  _const_spec((E_A, D_MODEL)), _const_spec((4, D_MODEL, E_A)), _const_spec((1, D_MODEL)),
                  _const_spec((8, E_A))],
        out_specs=[rev(D_MODEL)] + [_acc_spec(s) for s in acc_shapes],
        out_shape=[jax.ShapeDtypeStruct((t, D_MODEL), F32)] + [jax.ShapeDtypeStruct(s, F32) for s in acc_shapes],
        scratch_shapes=[pltpu.VMEM((8, E_A), F32)],
        compiler_params=_params("arbitrary"),
    )(dx1, x, b, c, u, g, c, u, w_out, w_in4, ga, cw)


WEIGHTS = ("a_norm", "a_w_in", "a_conv", "a_w_out", "kv_norm", "w_dkv", "ckv_norm", "w_ukv", "b_norm", "b_w_in",
           "b_q_norm", "b_w_uq", "b_w_out", "final_norm")
SHARD_SHAPES = {
    "a_norm": (1, 256), "a_w_in": (1, 1024, 1024), "a_conv": (1, 3, 256), "a_w_out": (1, 256, 1024),
    "kv_norm": (1024,), "w_dkv": (256, 288), "ckv_norm": (256,), "w_ukv": (256, 256), "b_norm": (1, 1024),
    "b_w_in": (1, 256, 896), "b_q_norm": (1, 384), "b_w_uq": (1, 384, 192), "b_w_out": (1, 512, 256),
    "final_norm": (1024,),
}
MATS = ("a_w_in", "a_w_out", "w_dkv", "w_ukv", "b_w_in", "b_w_uq", "b_w_out")
SMALL = ("a_norm", "a_conv", "kv_norm", "ckv_norm", "b_norm", "b_q_norm", "final_norm")
SMALL_FULL = {"a_norm": 1024, "a_conv": 3072, "kv_norm": 1024, "ckv_norm": 256, "b_norm": 1024, "b_q_norm": 384,
              "final_norm": 1024}
SMALL_ROWS = 8
LOSS_SLOT = sum(SMALL_FULL.values())


def _mat2d(name, a):
    return a.reshape(SHARD_SHAPES[name][-2:])


def _prep_first(g_win, gsmall):
    sm = gsmall.reshape(N_CHIPS, -1)
    a_conv = jnp.transpose(sm[:, 256:1024].reshape(N_CHIPS, CONV_WIDTH, 256), (1, 0, 2)).reshape(CONV_WIDTH, -1)
    return {"w_in4": g_win, "ga": sm[:, :256].reshape(1, -1), "cw": jnp.pad(a_conv, ((0, 8 - CONV_WIDTH), (0, 0)))}


def _prep_rest(gath, w):
    def cols(a):
        return jnp.transpose(a, (1, 0, 2)).reshape(a.shape[1], -1)

    def pad_heads(a, width):
        a = a.reshape(a.shape[0], N_HEADS, width)
        return jnp.pad(a, ((0, 0), (0, 0), (0, HEAD_PAD - width))).reshape(a.shape[0], QK_PAD)

    row = lambda a: a.reshape(1, -1).astype(F32)
    w_dkv = gath["w_dkv"].reshape(D_MODEL, KV_RANK + QK_ROPE)
    w_ukv = cols(gath["w_ukv"]).reshape(KV_RANK, N_HEADS, 2, QK_NOPE)
    return {
        "w_out": gath["a_w_out"].reshape(E_A, D_MODEL),
        "w_dkv": jnp.concatenate([w_dkv[:, :KV_RANK], jnp.zeros((D_MODEL, ROPE_LO), BF16), w_dkv[:, KV_RANK:],
                                  jnp.zeros((D_MODEL, HEAD_PAD - ROPE_LO - QK_ROPE), BF16)], axis=1),
        "w_uk": pad_heads(w_ukv[:, :, 0, :].reshape(KV_RANK, N_HEADS * QK_NOPE), QK_NOPE),
        "w_uv": w_ukv[:, :, 1, :].reshape(KV_RANK, E_B),
        "w_bin": gath["b_w_in"].reshape(D_MODEL, Q_RANK + E_B),
        "w_uq": pad_heads(cols(gath["b_w_uq"]), QK_NOPE + QK_ROPE),
        "w_bout": cols(gath["b_w_out"]),
        "gk": row(w["kv_norm"]), "gc": row(w["ckv_norm"]), "gb": row(w["b_norm"]),
        "gq": row(w["b_q_norm"]), "gf": row(w["final_norm"]),
    }


GRAD_MATS = ("w_in4", "w_out", "w_dkv", "w_uk", "w_uv", "w_bin", "w_uq", "w_bout")
GRAD_KIND = {"w_in4": "lead", "w_out": "row", "w_dkv": "row", "w_uk": "col", "w_uv": "col", "w_bin": "row",
             "w_uq": "col", "w_bout": "col"}


def _local_step(x, positions, tgt, w):
    bsz, seq, _ = x.shape
    t = bsz * seq
    x2d = x.reshape(t, D_MODEL)
    small = jnp.concatenate([w["a_norm"].reshape(-1), w["a_conv"].reshape(-1)]).reshape(8, LANES)
    rc, rs1, rs2, g_win, gsmall = _rope_tables_gather(positions.reshape(t, 1), _mat2d(MATS[0], w[MATS[0]]), small)
    wk = _prep_first(g_win, gsmall)
    (b, c, u, g, ym), gathered = _conv_fwd(x2d, seq, wk["ga"], wk["w_in4"], wk["cw"],
                                           [_mat2d(n, w[n]) for n in MATS[1:]])
    wk.update(_prep_rest(dict(zip(MATS[1:], gathered)), w))
    x1, q, k, v, gate, cq, ckv = _mid_fwd(x2d, ym, wk["w_out"], wk["gk"], wk["gb"], wk["w_dkv"], wk["gc"], wk["w_uk"],
                                          wk["w_uv"], wk["w_bin"], wk["gq"], wk["w_uq"], rc, rs1, rs2)
    o, lse = _attn_fwd(q, k, v, seq)
    dx2, do, dgate, dd, loss, dgf, dw_bout = _head_fwd_bwd(o, gate, x1, tgt.reshape(t, D_MODEL), wk["w_bout"], wk["gf"])
    dq, dk, dv = _attn_bwd(q, k, v, do, lse, dd, seq)
    dx1, dwq, dw_bin, dwk, dwv, dwd, dgq, dgc, dgk, dgb = _mid_bwd(
        dq, dk, dv, dgate, dx2, x1, cq, ckv, rc, rs1, rs2, wk["w_uq"], wk["w_bin"], wk["w_uk"], wk["w_uv"], wk["w_dkv"],
        wk["gq"], wk["gc"], wk["gk"], wk["gb"])
    dx, dw_in4, dw_out, dcw, dga = _conv_bwd(dx1, x2d, b, c, u, g, seq, wk["w_out"], wk["w_in4"], wk["ga"], wk["cw"])
    mats = {"w_in4": dw_in4, "w_out": dw_out, "w_dkv": dwd, "w_uk": dwk, "w_uv": dwv, "w_bin": dw_bin, "w_uq": dwq,
            "w_bout": dw_bout}
    small = {"a_norm": dga, "a_conv": dcw[:CONV_WIDTH], "kv_norm": dgk, "ckv_norm": dgc, "b_norm": dgb,
             "b_q_norm": dgq, "final_norm": dgf}
    return loss[0, 0], dx.reshape(bsz, seq, D_MODEL), mats, small


def _shard_grads(sh, svec):
    j0 = 2 * lax.axis_index("x") + lax.axis_index("y")
    flat = svec.reshape(-1)
    off, small = 0, {}
    for n in SMALL:
        small[n] = flat[off:off + SMALL_FULL[n]]
        off += SMALL_FULL[n]
    dwd, dwk, dwv, dwq = sh["w_dkv"], sh["w_uk"], sh["w_uv"], sh["w_uq"]
    w_ukv = jnp.stack([dwk.reshape(KV_RANK, 2, HEAD_PAD)[:, :, :QK_NOPE], dwv.reshape(KV_RANK, 2, V_HEAD)], axis=2)
    return {
        "a_norm": lax.dynamic_slice(small["a_norm"], (j0 * 256,), (256,)),
        "a_conv": lax.dynamic_slice(small["a_conv"].reshape(CONV_WIDTH, E_A), (0, j0 * 256), (CONV_WIDTH, 256)),
        "kv_norm": small["kv_norm"], "ckv_norm": small["ckv_norm"], "b_norm": small["b_norm"],
        "b_q_norm": small["b_q_norm"], "final_norm": small["final_norm"],
        "a_w_in": sh["w_in4"], "a_w_out": sh["w_out"],
        "w_dkv": jnp.concatenate([dwd[:, :KV_RANK], dwd[:, KV_RANK + ROPE_LO:KV_RANK + ROPE_LO + QK_ROPE]], axis=1),
        "w_ukv": w_ukv.reshape(KV_RANK, 2 * (QK_NOPE + V_HEAD)),
        "b_w_in": sh["w_bin"],
        "b_w_uq": dwq.reshape(Q_RANK, 2, HEAD_PAD)[:, :, :QK_NOPE + QK_ROPE].reshape(Q_RANK, -1),
        "b_w_out": sh["w_bout"],
    }


def _sub(ref, kind, j, cc):
    if kind == "lead":
        h = ref.shape[1] // 2
        return ref.at[j, pl.ds(pl.multiple_of(cc * h, 8), h), :]
    if kind == "row":
        rows = ref.shape[0] // N_CHIPS
        h = rows // 2
        return ref.at[pl.ds(pl.multiple_of(j * rows + cc * h, 8), h), :]
    cols = ref.shape[1] // N_CHIPS
    h = ref.shape[0] // 2
    return ref.at[pl.ds(pl.multiple_of(cc * h, 8), h), pl.ds(j * cols, cols)]


def _sub_shape(shape, kind):
    if kind == "lead":
        return (shape[1] // 2, shape[2])
    if kind == "row":
        return (shape[0] // N_CHIPS // 2, shape[1])
    return (shape[0] // 2, shape[1] // N_CHIPS)


def _reduce_grads(grads, kinds, vec):
    n = len(grads)
    shapes = [_sub_shape(a.shape, kd) for a, kd in zip(grads, kinds)]
    units = [(k, j) for k in range(n) for j in range(N_CHIPS)]
    big = (max(s[0] for s in shapes), max(s[1] for s in shapes))

    def body(*refs):
        g, v_ref = refs[:n], refs[n]
        out, o_ref = refs[n + 1:2 * n + 1], refs[2 * n + 1]
        theirs, part, recd, red = (refs[(2 + i) * n + 2:(3 + i) * n + 2] for i in range(4))
        mine, got = refs[6 * n + 2], refs[6 * n + 3]
        send1, recv1, send3, recv3, send5, recv5, load, local, send_v, recv_v = refs[6 * n + 4:]
        x, y, c = _place()
        j0 = 2 * x + y
        me = 2 * j0 + c
        sibling = (x, y, 1 - c)

        got[me] = v_ref[...]
        first, small_in = [], []
        for d in range(1, N_DEV):
            px, py, pc = x ^ (d >> 2), y ^ ((d >> 1) & 1), c ^ (d & 1)
            first.append(_remote(v_ref, got.at[me], send_v.at[d - 1], recv_v.at[d - 1], (px, py, pc)))
            small_in.append(_remote(v_ref, got.at[4 * px + 2 * py + pc], send_v.at[d - 1], recv_v.at[d - 1],
                                    (px, py, pc)))
        halves = [_remote(_sub(g[k], kinds[k], j, 1 - c), theirs[k].at[j], send1.at[u], recv1.at[u], sibling)
                  for u, (k, j) in enumerate(units)]
        for cp in first + halves:
            cp.start()

        def mine_load(u):
            k, j = units[u]
            h, cols = shapes[k]
            return pltpu.make_async_copy(_sub(g[k], kinds[k], j, c), mine.at[u % 2, pl.ds(0, h), pl.ds(0, cols)],
                                         load.at[u % 2])

        mine_load(0).start()
        for u, (k, j) in enumerate(units):
            h, cols = shapes[k]
            if u + 1 < len(units):
                mine_load(u + 1).start()
            mine_load(u).wait()
            halves[u].wait_recv()
            part[k][j] = (mine[u % 2, 0:h, 0:cols] + theirs[k][j]).astype(BF16)
            to_owner = _remote(part[k].at[j], recd[k].at[j0], send3.at[u], recv3.at[4 * k + j0], (j // 2, j % 2, c))

            @pl.when(j != j0)
            def _():
                to_owner.start()

            @pl.when(j == j0)
            def _():
                recd[k][j] = part[k][j]

        swaps = []
        for k in range(n):
            for j in range(N_CHIPS):
                arrived = _remote(part[k].at[j], recd[k].at[j], send3.at[4 * k + j], recv3.at[4 * k + j],
                                  (j // 2, j % 2, c))

                @pl.when(j != j0)
                def _():
                    arrived.wait_recv()

            r = recd[k]
            red[k][...] = ((r[0].astype(F32) + r[1].astype(F32)) + r[2].astype(F32)) + r[3].astype(F32)
            own = pltpu.make_async_copy(red[k], out[k].at[c], local.at[k])
            give = _remote(red[k], out[k].at[c], send5.at[k], recv5.at[k], sibling)
            take = _remote(red[k], out[k].at[1 - c], send5.at[k], recv5.at[k], sibling)
            own.start()
            give.start()
            swaps.append((own, give, take))

        for cp in small_in:
            cp.wait_recv()
            cp.wait_send()
        acc = got[0]
        for d in range(1, N_DEV):
            acc = acc + got[d]
        o_ref[...] = acc
        for u, (k, j) in enumerate(units):
            halves[u].wait_send()
            sent = _remote(part[k].at[j], recd[k].at[j0], send3.at[u], recv3.at[u], (j // 2, j % 2, c))

            @pl.when(j != j0)
            def _():
                sent.wait_send()
        for own, give, take in swaps:
            take.wait_recv()
            give.wait_send()
            own.wait()

    sems = [pltpu.SemaphoreType.DMA((len(units),))] * 4 + [pltpu.SemaphoreType.DMA((n,))] * 2 \
        + [pltpu.SemaphoreType.DMA((2,)), pltpu.SemaphoreType.DMA((n,))] + [pltpu.SemaphoreType.DMA((N_DEV - 1,))] * 2
    res = pl.pallas_call(
        body, name="reduce_grads",
        in_specs=[HBM] * n + [WHOLE], out_specs=[HBM] * n + [WHOLE],
        out_shape=[jax.ShapeDtypeStruct((2,) + s, F32) for s in shapes] + [jax.ShapeDtypeStruct(vec.shape, vec.dtype)],
        scratch_shapes=[pltpu.VMEM((N_CHIPS,) + s, F32) for s in shapes]
        + [pltpu.VMEM((N_CHIPS,) + s, BF16) for s in shapes] * 2
        + [pltpu.VMEM(s, F32) for s in shapes]
        + [pltpu.VMEM((2,) + big, F32), pltpu.VMEM((N_DEV,) + vec.shape, vec.dtype)] + sems,
        compiler_params=_comm_params(),
    )(*grads, vec)
    return res[:n], res[n]


def _adamw_math(w, g, m, v):
    m = ADAM_B1 * m + (1.0 - ADAM_B1) * g
    v = ADAM_B2 * v + (1.0 - ADAM_B2) * (g * g)
    m_hat = m / (1.0 - ADAM_B1 ** ADAM_STEP)
    v_hat = v / (1.0 - ADAM_B2 ** ADAM_STEP)
    return -ADAM_LR * (m_hat / (jnp.sqrt(v_hat) + ADAM_EPS) + ADAM_WD * w), m, v


def _adamw_tiled(w, g, m, v):
    rows, width = w.shape
    tm = rows // 4

    def body(w_ref, g_ref, m_ref, v_ref, go_ref, d_ref, mo_ref, vo_ref):
        g = g_ref[...]
        go_ref[...] = g
        d_ref[...], mo_ref[...], vo_ref[...] = _adamw_math(w_ref[...], g, m_ref[...], v_ref[...])

    spec = pl.BlockSpec((tm, width), lambda i: (i, 0))
    out = jax.ShapeDtypeStruct((rows, width), F32)
    return pl.pallas_call(
        body, grid=(rows // tm,), name="adamw_tiled",
        in_specs=[spec] * 4, out_specs=[spec] * 4, out_shape=[out] * 4,
        compiler_params=_params("parallel"),
    )(w, g, m, v)


def _adamw_many(ws, gs, ms, vs):
    n = len(ws)

    def body(*refs):
        for k in range(n):
            w_ref, g_ref, m_ref, v_ref = (refs[i * n + k] for i in range(4))
            go_ref, d_ref, mo_ref, vo_ref = (refs[(4 + i) * n + k] for i in range(4))
            g = g_ref[...]
            go_ref[...] = g
            d_ref[...], mo_ref[...], vo_ref[...] = _adamw_math(w_ref[...], g, m_ref[...], v_ref[...])

    outs = [jax.ShapeDtypeStruct(a.shape, F32) for a in ws]
    res = pl.pallas_call(
        body, name="adamw_many",
        in_specs=[WHOLE] * (4 * n), out_specs=[WHOLE] * (4 * n), out_shape=outs * 4,
        compiler_params=_comm_params(),
    )(*ws, *gs, *ms, *vs)
    return res[:n], res[n:2 * n], res[2 * n:3 * n], res[3 * n:]


def kernel(x, positions, a_norm, a_w_in, a_conv, a_w_out, kv_norm, w_dkv, ckv_norm, w_ukv, b_norm, b_w_in, b_q_norm, b_w_uq, b_w_out, final_norm, loss_target, m_a_norm, m_a_w_in, m_a_conv, m_a_w_out, m_kv_norm, m_w_dkv, m_ckv_norm, m_w_ukv, m_b_norm, m_b_w_in, m_b_q_norm, m_b_w_uq, m_b_w_out, m_final_norm, v_a_norm, v_a_w_in, v_a_conv, v_a_w_out, v_kv_norm, v_w_dkv, v_ckv_norm, v_w_ukv, v_b_norm, v_b_w_in, v_b_q_norm, v_b_w_uq, v_b_w_out, v_final_norm):
    w = dict(a_norm=a_norm, a_w_in=a_w_in, a_conv=a_conv, a_w_out=a_w_out, kv_norm=kv_norm, w_dkv=w_dkv,
             ckv_norm=ckv_norm, w_ukv=w_ukv, b_norm=b_norm, b_w_in=b_w_in, b_q_norm=b_q_norm, b_w_uq=b_w_uq,
             b_w_out=b_w_out, final_norm=final_norm)
    m = dict(a_norm=m_a_norm, a_w_in=m_a_w_in, a_conv=m_a_conv, a_w_out=m_a_w_out, kv_norm=m_kv_norm, w_dkv=m_w_dkv,
             ckv_norm=m_ckv_norm, w_ukv=m_w_ukv, b_norm=m_b_norm, b_w_in=m_b_w_in, b_q_norm=m_b_q_norm,
             b_w_uq=m_b_w_uq, b_w_out=m_b_w_out, final_norm=m_final_norm)
    v = dict(a_norm=v_a_norm, a_w_in=v_a_w_in, a_conv=v_a_conv, a_w_out=v_a_w_out, kv_norm=v_kv_norm, w_dkv=v_w_dkv,
             ckv_norm=v_ckv_norm, w_ukv=v_w_ukv, b_norm=v_b_norm, b_w_in=v_b_w_in, b_q_norm=v_b_q_norm,
             b_w_uq=v_b_w_uq, b_w_out=v_b_w_out, final_norm=v_final_norm)

    loss, dx, gmat, gsmall = _local_step(x, positions, loss_target, w)

    kinds = [GRAD_KIND[n] for n in GRAD_MATS]
    grads = [gmat[n] for n in GRAD_MATS]
    flat = jnp.concatenate([gsmall[n].reshape(-1) for n in SMALL] + [loss.reshape(1)])
    flat = jnp.pad(flat, (0, SMALL_ROWS * PACK_W - flat.shape[0])).reshape(SMALL_ROWS, PACK_W)
    mine, svec = _reduce_grads(grads, kinds, flat)
    loss = svec.reshape(-1)[LOSS_SLOT]
    g = _shard_grads({n: a.reshape(-1, a.shape[-1]) for n, a in zip(GRAD_MATS, mine)}, svec)

    def two_d(a):
        return a.reshape(-1, a.shape[-1])

    big = "a_w_in"
    rest = [n for n in WEIGHTS if n != big]
    res_big = _adamw_tiled(*(two_d(t[big]) for t in (w, g, m, v)))
    res_rest = _adamw_many(*([two_d(t[n]) for n in rest] for t in (w, g, m, v)))
    out = {kind: dict(zip(rest, res_rest[i])) for i, kind in enumerate("gdmv")}
    for i, kind in enumerate("gdmv"):
        out[kind][big] = res_big[i]
    return (loss, dx) + tuple(out[kind][n].reshape(SHARD_SHAPES[n]) for kind in "gdmv" for n in WEIGHTS)
```

```python
import functools
import math

import numpy as np
import jax
import jax.numpy as jnp
from jax import lax
from jax.experimental import pallas as pl
from jax.experimental.pallas import tpu as pltpu

F32 = jnp.float32
BF16 = jnp.bfloat16

D_MODEL = 1024
E_A = 1024
CONV_WIDTH = 3
N_HEADS = 8
QK_NOPE = 64
QK_ROPE = 32
V_HEAD = 64
KV_RANK = 256
Q_RANK = 384
E_B = N_HEADS * V_HEAD
ROPE_THETA = 10000.0
SOFTMAX_SCALE = 1.0 / math.sqrt(QK_NOPE + QK_ROPE)
LOG2E = math.log2(math.e)
LN2 = math.log(2.0)
Q_PRESCALE = SOFTMAX_SCALE * LOG2E
EPS = 1e-6
HEAD_PAD = 128
QK_PAD = N_HEADS * HEAD_PAD
ROPE_LO = QK_NOPE
ROPE_HALF = QK_ROPE // 2
KR_PAD = KV_RANK + HEAD_PAD

ADAM_LR = 0.001
ADAM_B1 = 0.9
ADAM_B2 = 0.999
ADAM_EPS = 1e-08
ADAM_WD = 0.01
ADAM_STEP = 10

VMEM_LIMIT = 56 * 1024 * 1024
ROW_TILE = 512
CONV_BWD_TILE = 256
ATT_TILE_FWD = 1024
ATT_TILE = 512
LANES = 128
PACK_W = 1024

N_CHIPS = 4
N_DEV = 8


def _dot(a, b):
    return jnp.dot(a, b, preferred_element_type=F32)


def _dot_nt(a, b):
    return lax.dot_general(a, b, (((1,), (1,)), ((), ())), preferred_element_type=F32)


def _dot_tn(a, b):
    return lax.dot_general(a, b, (((0,), (0,)), ((), ())), preferred_element_type=F32)


def _rms(x):
    r = lax.rsqrt(jnp.mean(x * x, axis=-1, keepdims=True) + EPS)
    return r, x * r


def _rms_bwd(dxh, xh, r):
    return r * (dxh - xh * jnp.mean(dxh * xh, axis=-1, keepdims=True))


def _rope_fwd(a, c, s1, s2):
    return a * c + pltpu.roll(a, HEAD_PAD - ROPE_HALF, 1) * s1 + pltpu.roll(a, ROPE_HALF, 1) * s2


def _rope_bwd(g, c, s1, s2):
    return g * c + pltpu.roll(g * s1, ROPE_HALF, 1) + pltpu.roll(g * s2, HEAD_PAD - ROPE_HALF, 1)


def _sigmoid(x):
    return 1.0 / (1.0 + jnp.exp(-x))


def _row_spec(tm, n):
    return pl.BlockSpec((tm, n), lambda i: (i, 0))


def _const_spec(shape):
    nd = len(shape)
    return pl.BlockSpec(shape, lambda i: (0,) * nd, pipeline_mode=pl.Buffered(1))


def _acc_spec(shape):
    nd = len(shape)
    return pl.BlockSpec(shape, lambda i: (0,) * nd, pipeline_mode=pl.Buffered(1))


def _params(*sem):
    return pltpu.CompilerParams(dimension_semantics=sem, vmem_limit_bytes=VMEM_LIMIT)


MESH = pl.DeviceIdType.MESH
HBM = pl.BlockSpec(memory_space=pl.ANY)
WHOLE = pl.BlockSpec(memory_space=pltpu.VMEM)
FLIPS = ((1, 0), (0, 1), (1, 1))


def _place():
    return lax.axis_index("x"), lax.axis_index("y"), lax.axis_index("c")


def _remote(src, dst, send, recv, peer):
    return pltpu.make_async_remote_copy(src_ref=src, dst_ref=dst, send_sem=send, recv_sem=recv, device_id=peer,
                                        device_id_type=MESH)


def _comm_params():
    return pltpu.CompilerParams(vmem_limit_bytes=VMEM_LIMIT)


def _rope_consts():
    lane = np.arange(HEAD_PAD)
    first = (lane >= ROPE_LO) & (lane < ROPE_LO + ROPE_HALF)
    second = (lane >= ROPE_LO + ROPE_HALF) & (lane < ROPE_LO + QK_ROPE)
    f = np.where(first, lane - ROPE_LO, np.where(second, lane - ROPE_LO - ROPE_HALF, 0))
    inv = np.float32(ROPE_THETA) ** (-(2 * f).astype(np.float32) / np.float32(QK_ROPE))
    out = np.zeros((8, HEAD_PAD), np.float32)
    out[0] = inv
    out[1] = first
    out[2] = second
    out[3] = lane < ROPE_LO
    return jnp.asarray(out)


def _gather_plan(w, sm, outs, osm, bf, sems):
    n = len(w)
    send_i, recv_i, send_d, recv_d, send_s, recv_s, local = sems
    x, y, c = _place()
    j0 = 2 * x + y
    sibling = (x, y, 1 - c)
    own = [pltpu.make_async_copy(bf[k], outs[k].at[j0], local.at[k]) for k in range(n)]
    if sm is not None:
        own.append(pltpu.make_async_copy(sm, osm.at[j0], local.at[n]))

    def half(k, cc):
        h = w[k].shape[0] // 2
        return pl.ds(pl.multiple_of(cc * h, 16), h)

    sends, arrivals, forwards, fwaits = [], [], [], []
    for i, (fx, fy) in enumerate(FLIPS):
        px, py = x ^ fx, y ^ fy
        pj = 2 * px + py
        for k in range(n):
            s = i * n + k
            sends.append(_remote(bf[k].at[half(k, c)], outs[k].at[j0, half(k, c)], send_i.at[s], recv_i.at[s],
                                 (px, py, c)))
            arrivals.append(_remote(bf[k].at[half(k, c)], outs[k].at[pj, half(k, c)], send_i.at[s], recv_i.at[s],
                                    (px, py, c)))
            forwards.append(_remote(outs[k].at[pj, half(k, c)], outs[k].at[pj, half(k, c)], send_d.at[s],
                                    recv_d.at[s], sibling))
            fwaits.append(_remote(outs[k].at[pj, half(k, 1 - c)], outs[k].at[pj, half(k, 1 - c)], send_d.at[s],
                                  recv_d.at[s], sibling))
        if sm is not None:
            sends.append(_remote(sm, osm.at[j0], send_s.at[i], recv_s.at[i], (px, py, c)))
            fwaits.append(_remote(sm, osm.at[pj], send_s.at[i], recv_s.at[i], (px, py, c)))
    return own, sends, arrivals, forwards, fwaits


def _gather_begin(w, bf, plan):
    own, sends, _, _, _ = plan
    for k in range(len(w)):
        bf[k][...] = w[k][...].astype(BF16)
    for cp in own + sends:
        cp.start()


def _gather_end(plan):
    own, sends, arrivals, forwards, fwaits = plan
    for cp, fwd in zip(arrivals, forwards):
        cp.wait_recv()
        fwd.start()
    for cp in fwaits:
        cp.wait_recv()
    for cp in sends + forwards:
        cp.wait_send()
    for cp in own:
        cp.wait()


def _gather_sems(n, with_small):
    return ([pltpu.SemaphoreType.DMA((3 * n,))] * 4 + [pltpu.SemaphoreType.DMA((3,))] * 2
            + [pltpu.SemaphoreType.DMA((n + (1 if with_small else 0),))])


def _rope_tables_gather(pos_col, w_in, small):
    t = pos_col.shape[0]
    tm = min(ROW_TILE, t)
    steps = t // tm

    def body(p_ref, k_ref, w_ref, sm_ref, c_ref, s1_ref, s2_ref, out_ref, osm_ref, bf_ref, *sems):
        plan = _gather_plan([w_ref], sm_ref, [out_ref], osm_ref, [bf_ref], sems)
        i = pl.program_id(0)

        @pl.when(i == 0)
        def _():
            _gather_begin([w_ref], [bf_ref], plan)

        inv, first, second, nope = k_ref[0:1, :], k_ref[1:2, :], k_ref[2:3, :], k_ref[3:4, :]
        ang = p_ref[...].astype(F32) * inv
        cs, sn = jnp.cos(ang), jnp.sin(ang)
        c_ref[...] = cs * (first + second) + nope
        s1_ref[...] = -sn * first
        s2_ref[...] = sn * second

        @pl.when(i == steps - 1)
        def _():
            _gather_end(plan)

    out = jax.ShapeDtypeStruct((t, HEAD_PAD), F32)
    return pl.pallas_call(
        body, grid=(steps,), name="rope_tables_gather",
        in_specs=[_row_spec(tm, 1), _const_spec((8, HEAD_PAD)), WHOLE, WHOLE],
        out_specs=[_row_spec(tm, HEAD_PAD)] * 3 + [HBM, HBM],
        out_shape=[out] * 3 + [jax.ShapeDtypeStruct((N_CHIPS,) + w_in.shape, BF16),
                               jax.ShapeDtypeStruct((N_CHIPS,) + small.shape, small.dtype)],
        scratch_shapes=[pltpu.VMEM(w_in.shape, BF16)] + _gather_sems(1, True),
        compiler_params=_params("arbitrary"),
    )(pos_col, _rope_consts(), w_in, small)


def _shift_down(v, prev, row):
    p1, p2 = prev[7:8, :], prev[6:7, :]
    v1 = jnp.where(row == 0, p1, pltpu.roll(v, 1, 0))
    v2 = jnp.where(row == 0, p2, jnp.where(row == 1, p1, pltpu.roll(v, 2, 0)))
    return v1, v2


def _conv_fwd(x, seq, ga, w_in4, cw, shards):
    t = x.shape[0]
    tm = min(ROW_TILE, seq)
    tiles_per_seq = seq // tm
    steps = t // tm
    n = len(shards)

    def body(x_ref, ga_ref, w_ref, cw_ref, *rest):
        sh, (b_ref, c_ref, u_ref, g_ref, ym_ref) = rest[:n], rest[n:n + 5]
        outs, carry_ref, bf, sems = rest[n + 5:2 * n + 5], rest[2 * n + 5], rest[2 * n + 6:3 * n + 6], rest[3 * n + 6:]
        plan = _gather_plan(sh, None, outs, None, bf, sems)
        i = pl.program_id(0)

        @pl.when(i == 0)
        def _():
            _gather_begin(sh, bf, plan)

        @pl.when(i % tiles_per_seq == 0)
        def _():
            carry_ref[...] = jnp.zeros_like(carry_ref)

        _, xh = _rms(x_ref[...])
        h = (xh * ga_ref[...]).astype(BF16)
        b, c, u, g = (_dot(h, w_ref[j]) for j in range(4))
        v = c * u
        row = lax.broadcasted_iota(jnp.int32, (tm, 1), 0)
        v1, v2 = _shift_down(v, carry_ref[...], row)
        carry_ref[...] = v[tm - 8:tm, :]
        cv = cw_ref[2:3, :] * v + cw_ref[1:2, :] * v1 + cw_ref[0:1, :] * v2
        b_ref[...] = b.astype(BF16)
        c_ref[...] = c.astype(BF16)
        u_ref[...] = u.astype(BF16)
        g_ref[...] = g.astype(BF16)
        ym_ref[...] = (g * _sigmoid(g) * b * cv).astype(BF16)

        @pl.when(i == steps - 1)
        def _():
            _gather_end(plan)

    out = jax.ShapeDtypeStruct((t, E_A), BF16)
    res = pl.pallas_call(
        body, grid=(steps,), name="conv_fwd",
        in_specs=[_row_spec(tm, D_MODEL), _const_spec((1, D_MODEL)), _const_spec((4, D_MODEL, E_A)),
                  _const_spec((8, E_A))] + [WHOLE] * n,
        out_specs=[_row_spec(tm, E_A)] * 5 + [HBM] * n,
        out_shape=[out] * 5 + [jax.ShapeDtypeStruct((N_CHIPS,) + a.shape, BF16) for a in shards],
        scratch_shapes=[pltpu.VMEM((8, E_A), F32)] + [pltpu.VMEM(a.shape, BF16) for a in shards]
        + _gather_sems(n, False),
        compiler_params=_params("arbitrary"),
    )(x, ga, w_in4, cw, *shards)
    return res[:5], res[5:]


def _mid_fwd(x, ym, w_out, gk, gb, w_dkv, gc, w_uk, w_uv, w_bin, gq, w_uq, rc, rs1, rs2):
    t = x.shape[0]
    tm = min(ROW_TILE, t)

    def body(x_ref, ym_ref, wo_ref, gk_ref, gb_ref, wd_ref, gc_ref, wk_ref, wv_ref, wi_ref, gq_ref, wq_ref,
             c_ref, s1_ref, s2_ref, x1_ref, q_ref, k_ref, v_ref, gate_ref, cq_ref, ckv_ref):
        cb, s1b, s2b = c_ref[...], s1_ref[...], s2_ref[...]
        x1 = x_ref[...] + _dot(ym_ref[...], wo_ref[...])
        x1_ref[...] = x1
        _, xh = _rms(x1)
        hk = (xh * gk_ref[...]).astype(BF16)
        h1 = (xh * gb_ref[...]).astype(BF16)

        pb = _dot(h1, wi_ref[...])
        cq = pb[:, :Q_RANK]
        cq_ref[...] = cq.astype(BF16)
        gate_ref[...] = pb[:, Q_RANK:].astype(BF16)
        _, cqh = _rms(cq)
        q = _dot((cqh * gq_ref[...]).astype(BF16), wq_ref[...])
        for h in range(N_HEADS):
            sl = slice(h * HEAD_PAD, (h + 1) * HEAD_PAD)
            q_ref[:, sl] = (_rope_fwd(q[:, sl], cb, s1b, s2b) * Q_PRESCALE).astype(BF16)

        ckr = _dot(hk, wd_ref[...])
        ckv_raw = ckr[:, :KV_RANK]
        ckv_ref[...] = ckv_raw.astype(BF16)
        _, ch = _rms(ckv_raw)
        ckv = (ch * gc_ref[...]).astype(BF16)
        kr = _rope_fwd(ckr[:, KV_RANK:], cb, s1b, s2b)
        kn = _dot(ckv, wk_ref[...])
        for h in range(N_HEADS):
            sl = slice(h * HEAD_PAD, (h + 1) * HEAD_PAD)
            k_ref[:, sl] = (kn[:, sl] + kr).astype(BF16)
        v_ref[...] = _dot(ckv, wv_ref[...]).astype(BF16)

    def sds(n, dt):
        return jax.ShapeDtypeStruct((t, n), dt)

    return pl.pallas_call(
        body, grid=(t // tm,), name="mid_fwd",
        in_specs=[_row_spec(tm, D_MODEL), _row_spec(tm, E_A), _const_spec((E_A, D_MODEL)),
                  _const_spec((1, D_MODEL)), _const_spec((1, D_MODEL)), _const_spec((D_MODEL, KR_PAD)),
                  _const_spec((1, KV_RANK)), _const_spec((KV_RANK, QK_PAD)), _const_spec((KV_RANK, E_B)),
                  _const_spec((D_MODEL, Q_RANK + E_B)), _const_spec((1, Q_RANK)), _const_spec((Q_RANK, QK_PAD)),
                  _row_spec(tm, HEAD_PAD), _row_spec(tm, HEAD_PAD), _row_spec(tm, HEAD_PAD)],
        out_specs=[_row_spec(tm, D_MODEL), _row_spec(tm, QK_PAD), _row_spec(tm, QK_PAD), _row_spec(tm, E_B),
                   _row_spec(tm, E_B), _row_spec(tm, Q_RANK), _row_spec(tm, KV_RANK)],
        out_shape=[sds(D_MODEL, F32), sds(QK_PAD, BF16), sds(QK_PAD, BF16), sds(E_B, BF16), sds(E_B, BF16),
                   sds(Q_RANK, BF16), sds(KV_RANK, BF16)],
        compiler_params=_params("parallel"),
    )(x, ym, w_out, gk, gb, w_dkv, gc, w_uk, w_uv, w_bin, gq, w_uq, rc, rs1, rs2)


def _pair_specs(seq):
    qk = pl.BlockSpec((seq, 2 * HEAD_PAD), lambda b, p: (b, p))
    vo = pl.BlockSpec((seq, 2 * V_HEAD), lambda b, p: (b, p))
    st = pl.BlockSpec((None, 2, seq), lambda b, p: (p, 0, b))
    return qk, vo, st


def _attn_fwd(q, k, v, seq):
    t = q.shape[0]
    tq = min(ATT_TILE_FWD, seq)
    nq = seq // tq

    def body(q_ref, k_ref, v_ref, o_ref, lse_ref, m_scr, l_scr, acc_scr):
        lane = lax.broadcasted_iota(jnp.int32, (tq, 2 * V_HEAD), 1)

        def q_step(qi, _):
            q0 = pl.multiple_of(qi * tq, tq)
            m_scr[...] = jnp.full(m_scr.shape, -jnp.inf, F32)
            l_scr[...] = jnp.zeros_like(l_scr)
            acc_scr[...] = jnp.zeros_like(acc_scr)

            def block(q_lo, q_n, k0, k_n, masked):
                rows = slice(q_lo, q_lo + q_n)
                vt = v_ref[pl.ds(k0, k_n), :]
                for hh in range(2):
                    hs = slice(hh * HEAD_PAD, (hh + 1) * HEAD_PAD)
                    s = _dot_nt(q_ref[pl.ds(q0 + q_lo, q_n), hs], k_ref[pl.ds(k0, k_n), hs])
                    if masked:
                        row = lax.broadcasted_iota(jnp.int32, (q_n, k_n), 0)
                        col = lax.broadcasted_iota(jnp.int32, (q_n, k_n), 1)
                        s = jnp.where(col <= row, s, -jnp.inf)
                    m_old = m_scr[hh, rows]
                    m_new = jnp.maximum(m_old, jnp.max(s, axis=-1, keepdims=True))
                    alpha = jnp.exp2(m_old - m_new)
                    ps = [jnp.exp2(s[:, j * LANES:(j + 1) * LANES] - m_new) for j in range(k_n // LANES)]
                    l_scr[hh, rows] = alpha * l_scr[hh, rows] + functools.reduce(lambda a, b: a + b, ps)
                    p = jnp.concatenate(ps, axis=-1).astype(BF16)
                    acc_scr[hh, rows] = alpha * acc_scr[hh, rows] + _dot(p, vt)
                    m_scr[hh, rows] = m_new

            def k_step(ki, _):
                block(0, tq, pl.multiple_of(ki * tq, tq), tq, False)
                return 0

            lax.fori_loop(0, qi, k_step, 0)
            half = tq // 2
            block(0, tq, q0, half, True)
            block(half, half, q0 + half, half, True)
            l0 = jnp.sum(l_scr[0], axis=-1, keepdims=True)
            l1 = jnp.sum(l_scr[1], axis=-1, keepdims=True)
            o_ref[pl.ds(q0, tq), :] = jnp.where(lane < V_HEAD, acc_scr[0] / l0, acc_scr[1] / l1).astype(BF16)
            stats = jnp.where(lane == 0, m_scr[0] + jnp.log2(l0), m_scr[1] + jnp.log2(l1)).T
            lse_ref[:, pl.ds(q0, tq)] = stats[0:2, :]
            return 0

        lax.fori_loop(0, nq, q_step, 0)

    qk, vo, st = _pair_specs(seq)
    return pl.pallas_call(
        body, grid=(t // seq, N_HEADS // 2), name="attn_fwd",
        in_specs=[qk, qk, vo], out_specs=[vo, st],
        out_shape=[jax.ShapeDtypeStruct((t, E_B), BF16), jax.ShapeDtypeStruct((N_HEADS // 2, 2, t), F32)],
        scratch_shapes=[pltpu.VMEM((2, tq, LANES), F32), pltpu.VMEM((2, tq, LANES), F32),
                        pltpu.VMEM((2, tq, 2 * V_HEAD), F32)],
        compiler_params=_params("parallel", "parallel"),
    )(q, k, v)


def _head_fwd_bwd(o, gate, x1, tgt, w_bout, gf):
    t = o.shape[0]
    tm = min(ROW_TILE, t)

    def body(o_ref, gate_ref, x1_ref, tgt_ref, w_ref, gf_ref,
             dx2_ref, do_ref, dgate_ref, dd_ref, loss_ref, dgf_ref, dw_ref):
        @pl.when(pl.program_id(0) == 0)
        def _():
            loss_ref[...] = jnp.zeros_like(loss_ref)
            dgf_ref[...] = jnp.zeros_like(dgf_ref)
            dw_ref[...] = jnp.zeros_like(dw_ref)

        o = o_ref[...].astype(F32)
        gt = gate_ref[...].astype(F32)
        sg = _sigmoid(gt)
        silu = gt * sg
        z = (o * silu).astype(BF16)
        x2 = x1_ref[...] + _dot(z, w_ref[...])
        r2, xh2 = _rms(x2)
        gf = gf_ref[...]
        err = xh2 * gf - tgt_ref[...]
        loss_ref[...] += 0.5 * jnp.sum(jnp.mean(err * err, axis=-1, keepdims=True))
        dy = err * (1.0 / D_MODEL)
        dgf_ref[...] += jnp.sum(dy * xh2, axis=0, keepdims=True)
        dx2 = _rms_bwd(dy * gf, xh2, r2)
        dx2_ref[...] = dx2
        dx2b = dx2.astype(BF16)
        dw_ref[...] += _dot_tn(z, dx2b)
        dz = _dot_nt(dx2b, w_ref[...])
        do = dz * silu
        do_ref[...] = do.astype(BF16)
        dgate_ref[...] = (dz * o * (sg * (1.0 + gt * (1.0 - sg)))).astype(BF16)
        prod = do * o
        lane = lax.broadcasted_iota(jnp.int32, (tm, 2 * V_HEAD), 1)
        cols = jnp.zeros((tm, LANES), F32)
        for p in range(N_HEADS // 2):
            blk = prod[:, p * 2 * V_HEAD:(p + 1) * 2 * V_HEAD]
            d0 = jnp.sum(jnp.where(lane < V_HEAD, blk, 0.0), axis=-1, keepdims=True)
            d1 = jnp.sum(jnp.where(lane < V_HEAD, 0.0, blk), axis=-1, keepdims=True)
            cols = jnp.where(lane == 2 * p, d0, jnp.where(lane == 2 * p + 1, d1, cols))
        rows = cols.T
        for h in range(N_HEADS):
            dd_ref[h // 2, h % 2:h % 2 + 1, :] = rows[h:h + 1, :]

    return pl.pallas_call(
        body, grid=(t // tm,), name="head_fwd_bwd",
        in_specs=[_row_spec(tm, E_B), _row_spec(tm, E_B), _row_spec(tm, D_MODEL), _row_spec(tm, D_MODEL),
                  _const_spec((E_B, D_MODEL)), _const_spec((1, D_MODEL))],
        out_specs=[_row_spec(tm, D_MODEL), _row_spec(tm, E_B), _row_spec(tm, E_B),
                   pl.BlockSpec((N_HEADS // 2, 2, tm), lambda i: (0, 0, i)),
                   _acc_spec((1, 1)), _acc_spec((1, D_MODEL)), _acc_spec((E_B, D_MODEL))],
        out_shape=[jax.ShapeDtypeStruct((t, D_MODEL), F32), jax.ShapeDtypeStruct((t, E_B), BF16),
                   jax.ShapeDtypeStruct((t, E_B), BF16), jax.ShapeDtypeStruct((N_HEADS // 2, 2, t), F32),
                   jax.ShapeDtypeStruct((1, 1), F32), jax.ShapeDtypeStruct((1, D_MODEL), F32),
                   jax.ShapeDtypeStruct((E_B, D_MODEL), F32)],
        compiler_params=_params("arbitrary"),
    )(o, gate, x1, tgt, w_bout, gf)


def _attn_bwd(q, k, v, do, lse, dd, seq):
    t = q.shape[0]
    tq = min(ATT_TILE, seq)
    nq = seq // tq

    def body(q_ref, k_ref, v_ref, do_ref, lse_ref, dd_ref, dq_ref, dk_ref, dv_ref, dq_acc, dk_acc, dv_acc):
        dq_acc[...] = jnp.zeros_like(dq_acc)

        def k_step(ki, _):
            k0 = pl.multiple_of(ki * tq, tq)
            dk_acc[...] = jnp.zeros_like(dk_acc)
            dv_acc[...] = jnp.zeros_like(dv_acc)

            def block(k_lo, k_n, q0, q_n, masked):
                rows = slice(k_lo, k_lo + k_n)
                lane = lax.broadcasted_iota(jnp.int32, (q_n, 2 * V_HEAD), 1)
                vt = v_ref[pl.ds(k0 + k_lo, k_n), :]
                do_pair = do_ref[pl.ds(q0, q_n), :]
                for hh in range(2):
                    hs = slice(hh * HEAD_PAD, (hh + 1) * HEAD_PAD)
                    kt = k_ref[pl.ds(k0 + k_lo, k_n), hs]
                    qt = q_ref[pl.ds(q0, q_n), hs]
                    mine = (lane < V_HEAD) if hh == 0 else (lane >= V_HEAD)
                    do_h = jnp.where(mine, do_pair, jnp.zeros((), BF16))
                    st = _dot_nt(kt, qt)
                    if masked:
                        krow = lax.broadcasted_iota(jnp.int32, (k_n, q_n), 0)
                        qcol = lax.broadcasted_iota(jnp.int32, (k_n, q_n), 1)
                        st = jnp.where(krow <= qcol, st, -jnp.inf)
                    pt = jnp.exp2(st - lse_ref[hh:hh + 1, pl.ds(q0, q_n)])
                    dpt = _dot_nt(vt, do_h)
                    dst = (pt * (dpt - dd_ref[hh:hh + 1, pl.ds(q0, q_n)])).astype(BF16)
                    dv_acc[rows, :] += _dot(pt.astype(BF16), do_h)
                    dk_acc[rows, hs] += _dot(dst, qt)
                    dq_acc[pl.ds(q0, q_n), hs] += _dot_tn(dst, kt)

            def q_step(qi, _):
                block(0, tq, pl.multiple_of(qi * tq, tq), tq, False)
                return 0

            half = tq // 2
            block(0, half, k0, tq, True)
            block(half, half, pl.multiple_of(k0 + half, half), half, True)
            lax.fori_loop(ki + 1, nq, q_step, 0)
            dk_ref[pl.ds(k0, tq), :] = (dk_acc[...] * LN2).astype(BF16)
            dv_ref[pl.ds(k0, tq), :] = dv_acc[...].astype(BF16)
            return 0

        lax.fori_loop(0, nq, k_step, 0)
        dq_ref[...] = (dq_acc[...] * SOFTMAX_SCALE).astype(BF16)

    qk, vo, st = _pair_specs(seq)
    return pl.pallas_call(
        body, grid=(t // seq, N_HEADS // 2), name="attn_bwd",
        in_specs=[qk, qk, vo, vo, st, st], out_specs=[qk, qk, vo],
        out_shape=[jax.ShapeDtypeStruct((t, QK_PAD), BF16), jax.ShapeDtypeStruct((t, QK_PAD), BF16),
                   jax.ShapeDtypeStruct((t, E_B), BF16)],
        scratch_shapes=[pltpu.VMEM((seq, 2 * HEAD_PAD), F32), pltpu.VMEM((tq, 2 * HEAD_PAD), F32),
                        pltpu.VMEM((tq, 2 * V_HEAD), F32)],
        compiler_params=_params("parallel", "parallel"),
    )(q, k, v, do, lse, dd)


def _mid_bwd(dq, dk, dv, dgate, dx2, x1, cq, ckv, rc, rs1, rs2, w_uq, w_bin, w_uk, w_uv, w_dkv, gq, gc, gk, gb):
    t = dq.shape[0]
    tm = min(ROW_TILE, t)

    def body(dq_ref, dk_ref, dv_ref, dgate_ref, dx2_ref, x1_ref, cq_ref, ckv_ref, c_ref, s1_ref, s2_ref,
             wq_ref, wi_ref, wk_ref, wv_ref, wd_ref, gq_ref, gc_ref, gk_ref, gb_ref,
             dx1_ref, dwq_ref, dwi_ref, dwk_ref, dwv_ref, dwd_ref, dgq_ref, dgc_ref, dgk_ref, dgb_ref):
        @pl.when(pl.program_id(0) == 0)
        def _():
            for ref in (dwq_ref, dwi_ref, dwk_ref, dwv_ref, dwd_ref, dgq_ref, dgc_ref, dgk_ref, dgb_ref):
                ref[...] = jnp.zeros_like(ref)

        cb, s1b, s2b = c_ref[...], s1_ref[...], s2_ref[...]
        r1, xh = _rms(x1_ref[...])
        gk, gb, gq, gc = gk_ref[...], gb_ref[...], gq_ref[...], gc_ref[...]
        hk = (xh * gk).astype(BF16)
        h1 = (xh * gb).astype(BF16)

        rcv, ch = _rms(ckv_ref[...].astype(F32))
        ckvn = (ch * gc).astype(BF16)
        dkb, dvb = dk_ref[...], dv_ref[...]
        dwk_ref[...] += _dot_tn(ckvn, dkb)
        dwv_ref[...] += _dot_tn(ckvn, dvb)
        dckv = _dot_nt(dkb, wk_ref[...]) + _dot_nt(dvb, wv_ref[...])
        dgc_ref[...] += jnp.sum(dckv * ch, axis=0, keepdims=True)
        dckv_raw = _rms_bwd(dckv * gc, ch, rcv)
        dkr = dk_ref[:, 0:HEAD_PAD].astype(F32)
        for h in range(1, N_HEADS):
            dkr = dkr + dk_ref[:, h * HEAD_PAD:(h + 1) * HEAD_PAD].astype(F32)
        dkr = _rope_bwd(dkr, cb, s1b, s2b)
        dckr = jnp.concatenate([dckv_raw, dkr], axis=-1).astype(BF16)
        dwd_ref[...] += _dot_tn(hk, dckr)
        dhk = _dot_nt(dckr, wd_ref[...])

        dqs = [_rope_bwd(dq_ref[:, h * HEAD_PAD:(h + 1) * HEAD_PAD].astype(F32), cb, s1b, s2b)
               for h in range(N_HEADS)]
        dqb = jnp.concatenate(dqs, axis=-1).astype(BF16)
        rq, cqh = _rms(cq_ref[...].astype(F32))
        dwq_ref[...] += _dot_tn((cqh * gq).astype(BF16), dqb)
        dcqn = _dot_nt(dqb, wq_ref[...])
        dgq_ref[...] += jnp.sum(dcqn * cqh, axis=0, keepdims=True)
        dcq = _rms_bwd(dcqn * gq, cqh, rq)
        dpb = jnp.concatenate([dcq.astype(BF16), dgate_ref[...]], axis=-1)
        dwi_ref[...] += _dot_tn(h1, dpb)
        dh1 = _dot_nt(dpb, wi_ref[...])

        dgb_ref[...] += jnp.sum(dh1 * xh, axis=0, keepdims=True)
        dgk_ref[...] += jnp.sum(dhk * xh, axis=0, keepdims=True)
        dx1_ref[...] = dx2_ref[...] + _rms_bwd(dh1 * gb + dhk * gk, xh, r1)

    acc_shapes = [(Q_RANK, QK_PAD), (D_MODEL, Q_RANK + E_B), (KV_RANK, QK_PAD), (KV_RANK, E_B), (D_MODEL, KR_PAD),
                  (1, Q_RANK), (1, KV_RANK), (1, D_MODEL), (1, D_MODEL)]
    return pl.pallas_call(
        body, grid=(t // tm,), name="mid_bwd",
        in_specs=[_row_spec(tm, QK_PAD), _row_spec(tm, QK_PAD), _row_spec(tm, E_B), _row_spec(tm, E_B),
                  _row_spec(tm, D_MODEL), _row_spec(tm, D_MODEL), _row_spec(tm, Q_RANK), _row_spec(tm, KV_RANK),
                  _row_spec(tm, HEAD_PAD), _row_spec(tm, HEAD_PAD), _row_spec(tm, HEAD_PAD),
                  _const_spec((Q_RANK, QK_PAD)), _const_spec((D_MODEL, Q_RANK + E_B)),
                  _const_spec((KV_RANK, QK_PAD)), _const_spec((KV_RANK, E_B)), _const_spec((D_MODEL, KR_PAD)),
                  _const_spec((1, Q_RANK)), _const_spec((1, KV_RANK)), _const_spec((1, D_MODEL)),
                  _const_spec((1, D_MODEL))],
        out_specs=[_row_spec(tm, D_MODEL)] + [_acc_spec(s) for s in acc_shapes],
        out_shape=[jax.ShapeDtypeStruct((t, D_MODEL), F32)] + [jax.ShapeDtypeStruct(s, F32) for s in acc_shapes],
        compiler_params=_params("arbitrary"),
    )(dq, dk, dv, dgate, dx2, x1, cq, ckv, rc, rs1, rs2, w_uq, w_bin, w_uk, w_uv, w_dkv, gq, gc, gk, gb)


def _conv_bwd(dx1, x, b, c, u, g, seq, w_out, w_in4, ga, cw):
    t = x.shape[0]
    tm = min(CONV_BWD_TILE, seq)
    tiles_per_seq = seq // tm
    n = t // tm
    halo = tm // 8

    def tile(i):
        return n - 1 - i

    def rev(width):
        return pl.BlockSpec((tm, width), lambda i: (tile(i), 0))

    def prev8(width):
        return pl.BlockSpec((8, width), lambda i: (jnp.maximum(tile(i) * halo - 1, 0), 0))

    def body(dx1_ref, x_ref, b_ref, c_ref, u_ref, g_ref, cp_ref, up_ref, wo_ref, wi_ref, ga_ref, cw_ref,
             dx_ref, dwi_ref, dwo_ref, dcw_ref, dga_ref, carry_ref):
        i = pl.program_id(0)
        j = tile(i)

        @pl.when(i == 0)
        def _():
            for ref in (dwi_ref, dwo_ref, dcw_ref, dga_ref):
                ref[...] = jnp.zeros_like(ref)

        @pl.when(j % tiles_per_seq == tiles_per_seq - 1)
        def _():
            carry_ref[...] = jnp.zeros_like(carry_ref)

        dx1 = dx1_ref[...]
        dx1b = dx1.astype(BF16)
        b, c, u, g = (r[...].astype(F32) for r in (b_ref, c_ref, u_ref, g_ref))
        v = c * u
        first = (j % tiles_per_seq == 0).astype(F32)
        vprev = cp_ref[...].astype(F32) * up_ref[...].astype(F32) * (1.0 - first)
        row = lax.broadcasted_iota(jnp.int32, (tm, 1), 0)
        v1, v2 = _shift_down(v, vprev, row)
        w0, w1, w2 = cw_ref[0:1, :], cw_ref[1:2, :], cw_ref[2:3, :]
        cv = w2 * v + w1 * v1 + w0 * v2
        sg = _sigmoid(g)
        silu = g * sg
        ym = (silu * b * cv).astype(BF16)
        dwo_ref[...] += _dot_tn(ym, dx1b)
        dym = _dot_nt(dx1b, wo_ref[...])
        db = dym * silu * cv
        dcv = dym * silu * b
        dg = dym * b * cv * (sg * (1.0 + g * (1.0 - sg)))

        nxt = carry_ref[...]
        n0, n1 = nxt[0:1, :], nxt[1:2, :]
        d1 = jnp.where(row == tm - 1, n0, pltpu.roll(dcv, tm - 1, 0))
        d2 = jnp.where(row == tm - 1, n1, jnp.where(row == tm - 2, n0, pltpu.roll(dcv, tm - 2, 0)))
        carry_ref[...] = dcv[0:8, :]
        dv = w2 * dcv + w1 * d1 + w0 * d2
        dcw_ref[0:1, :] += jnp.sum(dcv * v2, axis=0, keepdims=True)
        dcw_ref[1:2, :] += jnp.sum(dcv * v1, axis=0, keepdims=True)
        dcw_ref[2:3, :] += jnp.sum(dcv * v, axis=0, keepdims=True)

        r0, xh = _rms(x_ref[...])
        ga = ga_ref[...]
        h = (xh * ga).astype(BF16)
        dh = jnp.zeros((tm, D_MODEL), F32)
        for idx, dpart in enumerate((db, dv * u, dv * c, dg)):
            dpb = dpart.astype(BF16)
            dwi_ref[idx] += _dot_tn(h, dpb)
            dh = dh + _dot_nt(dpb, wi_ref[idx])
        dga_ref[...] += jnp.sum(dh * xh, axis=0, keepdims=True)
        dx_ref[...] = dx1 + _rms_bwd(dh * ga, xh, r0)

    acc_shapes = [(4, D_MODEL, E_A), (E_A, D_MODEL), (8, E_A), (1, D_MODEL)]
    return pl.pallas_call(
        body, grid=(n,), name="conv_bwd",
        in_specs=[rev(D_MODEL), rev(D_MODEL), rev(E_A), rev(E_A), rev(E_A), rev(E_A), prev8(E_A), prev8(E_A),
                  _const_spec((E_A, D_MODEL)), _const_spec((4, D_MODEL, E_A)), _const_spec((1, D_MODEL)),
                  _const_spec((8, E_A))],
        out_specs=[rev(D_MODEL)] + [_acc_spec(s) for s in acc_shapes],
        out_shape=[jax.ShapeDtypeStruct((t, D_MODEL), F32)] + [jax.ShapeDtypeStruct(s, F32) for s in acc_shapes],
        scratch_shapes=[pltpu.VMEM((8, E_A), F32)],
        compiler_params=_params("arbitrary"),
    )(dx1, x, b, c, u, g, c, u, w_out, w_in4, ga, cw)


WEIGHTS = ("a_norm", "a_w_in", "a_conv", "a_w_out", "kv_norm", "w_dkv", "ckv_norm", "w_ukv", "b_norm", "b_w_in",
           "b_q_norm", "b_w_uq", "b_w_out", "final_norm")
SHARD_SHAPES = {
    "a_norm": (1, 256), "a_w_in": (1, 1024, 1024), "a_conv": (1, 3, 256), "a_w_out": (1, 256, 1024),
    "kv_norm": (1024,), "w_dkv": (256, 288), "ckv_norm": (256,), "w_ukv": (256, 256), "b_norm": (1, 1024),
    "b_w_in": (1, 256, 896), "b_q_norm": (1, 384), "b_w_uq": (1, 384, 192), "b_w_out": (1, 512, 256),
    "final_norm": (1024,),
}
MATS = ("a_w_in", "a_w_out", "w_dkv", "w_ukv", "b_w_in", "b_w_uq", "b_w_out")
SMALL = ("a_norm", "a_conv", "kv_norm", "ckv_norm", "b_norm", "b_q_norm", "final_norm")
SMALL_FULL = {"a_norm": 1024, "a_conv": 3072, "kv_norm": 1024, "ckv_norm": 256, "b_norm": 1024, "b_q_norm": 384,
              "final_norm": 1024}
SMALL_ROWS = 8
LOSS_SLOT = sum(SMALL_FULL.values())


def _mat2d(name, a):
    return a.reshape(SHARD_SHAPES[name][-2:])


def _prep_first(g_win, gsmall):
    sm = gsmall.reshape(N_CHIPS, -1)
    a_conv = jnp.transpose(sm[:, 256:1024].reshape(N_CHIPS, CONV_WIDTH, 256), (1, 0, 2)).reshape(CONV_WIDTH, -1)
    return {"w_in4": g_win, "ga": sm[:, :256].reshape(1, -1), "cw": jnp.pad(a_conv, ((0, 8 - CONV_WIDTH), (0, 0)))}


def _prep_rest(gath, w):
    def cols(a):
        return jnp.transpose(a, (1, 0, 2)).reshape(a.shape[1], -1)

    def pad_heads(a, width):
        a = a.reshape(a.shape[0], N_HEADS, width)
        return jnp.pad(a, ((0, 0), (0, 0), (0, HEAD_PAD - width))).reshape(a.shape[0], QK_PAD)

    row = lambda a: a.reshape(1, -1).astype(F32)
    w_dkv = gath["w_dkv"].reshape(D_MODEL, KV_RANK + QK_ROPE)
    w_ukv = cols(gath["w_ukv"]).reshape(KV_RANK, N_HEADS, 2, QK_NOPE)
    return {
        "w_out": gath["a_w_out"].reshape(E_A, D_MODEL),
        "w_dkv": jnp.concatenate([w_dkv[:, :KV_RANK], jnp.zeros((D_MODEL, ROPE_LO), BF16), w_dkv[:, KV_RANK:],
                                  jnp.zeros((D_MODEL, HEAD_PAD - ROPE_LO - QK_ROPE), BF16)], axis=1),
        "w_uk": pad_heads(w_ukv[:, :, 0, :].reshape(KV_RANK, N_HEADS * QK_NOPE), QK_NOPE),
        "w_uv": w_ukv[:, :, 1, :].reshape(KV_RANK, E_B),
        "w_bin": gath["b_w_in"].reshape(D_MODEL, Q_RANK + E_B),
        "w_uq": pad_heads(cols(gath["b_w_uq"]), QK_NOPE + QK_ROPE),
        "w_bout": cols(gath["b_w_out"]),
        "gk": row(w["kv_norm"]), "gc": row(w["ckv_norm"]), "gb": row(w["b_norm"]),
        "gq": row(w["b_q_norm"]), "gf": row(w["final_norm"]),
    }


GRAD_MATS = ("w_in4", "w_out", "w_dkv", "w_uk", "w_uv", "w_bin", "w_uq", "w_bout")
GRAD_KIND = {"w_in4": "lead", "w_out": "row", "w_dkv": "row", "w_uk": "col", "w_uv": "col", "w_bin": "row",
             "w_uq": "col", "w_bout": "col"}


def _local_step(x, positions, tgt, w):
    bsz, seq, _ = x.shape
    t = bsz * seq
    x2d = x.reshape(t, D_MODEL)
    small = jnp.concatenate([w["a_norm"].reshape(-1), w["a_conv"].reshape(-1)]).reshape(8, LANES)
    rc, rs1, rs2, g_win, gsmall = _rope_tables_gather(positions.reshape(t, 1), _mat2d(MATS[0], w[MATS[0]]), small)
    wk = _prep_first(g_win, gsmall)
    (b, c, u, g, ym), gathered = _conv_fwd(x2d, seq, wk["ga"], wk["w_in4"], wk["cw"],
                                           [_mat2d(n, w[n]) for n in MATS[1:]])
    wk.update(_prep_rest(dict(zip(MATS[1:], gathered)), w))
    x1, q, k, v, gate, cq, ckv = _mid_fwd(x2d, ym, wk["w_out"], wk["gk"], wk["gb"], wk["w_dkv"], wk["gc"], wk["w_uk"],
                                          wk["w_uv"], wk["w_bin"], wk["gq"], wk["w_uq"], rc, rs1, rs2)
    o, lse = _attn_fwd(q, k, v, seq)
    dx2, do, dgate, dd, loss, dgf, dw_bout = _head_fwd_bwd(o, gate, x1, tgt.reshape(t, D_MODEL), wk["w_bout"], wk["gf"])
    dq, dk, dv = _attn_bwd(q, k, v, do, lse, dd, seq)
    dx1, dwq, dw_bin, dwk, dwv, dwd, dgq, dgc, dgk, dgb = _mid_bwd(
        dq, dk, dv, dgate, dx2, x1, cq, ckv, rc, rs1, rs2, wk["w_uq"], wk["w_bin"], wk["w_uk"], wk["w_uv"], wk["w_dkv"],
        wk["gq"], wk["gc"], wk["gk"], wk["gb"])
    dx, dw_in4, dw_out, dcw, dga = _conv_bwd(dx1, x2d, b, c, u, g, seq, wk["w_out"], wk["w_in4"], wk["ga"], wk["cw"])
    mats = {"w_in4": dw_in4, "w_out": dw_out, "w_dkv": dwd, "w_uk": dwk, "w_uv": dwv, "w_bin": dw_bin, "w_uq": dwq,
            "w_bout": dw_bout}
    small = {"a_norm": dga, "a_conv": dcw[:CONV_WIDTH], "kv_norm": dgk, "ckv_norm": dgc, "b_norm": dgb,
             "b_q_norm": dgq, "final_norm": dgf}
    return loss[0, 0], dx.reshape(bsz, seq, D_MODEL), mats, small


def _shard_grads(sh, svec):
    j0 = 2 * lax.axis_index("x") + lax.axis_index("y")
    flat = svec.reshape(-1)
    off, small = 0, {}
    for n in SMALL:
        small[n] = flat[off:off + SMALL_FULL[n]]
        off += SMALL_FULL[n]
    dwd, dwk, dwv, dwq = sh["w_dkv"], sh["w_uk"], sh["w_uv"], sh["w_uq"]
    w_ukv = jnp.stack([dwk.reshape(KV_RANK, 2, HEAD_PAD)[:, :, :QK_NOPE], dwv.reshape(KV_RANK, 2, V_HEAD)], axis=2)
    return {
        "a_norm": lax.dynamic_slice(small["a_norm"], (j0 * 256,), (256,)),
        "a_conv": lax.dynamic_slice(small["a_conv"].reshape(CONV_WIDTH, E_A), (0, j0 * 256), (CONV_WIDTH, 256)),
        "kv_norm": small["kv_norm"], "ckv_norm": small["ckv_norm"], "b_norm": small["b_norm"],
        "b_q_norm": small["b_q_norm"], "final_norm": small["final_norm"],
        "a_w_in": sh["w_in4"], "a_w_out": sh["w_out"],
        "w_dkv": jnp.concatenate([dwd[:, :KV_RANK], dwd[:, KV_RANK + ROPE_LO:KV_RANK + ROPE_LO + QK_ROPE]], axis=1),
        "w_ukv": w_ukv.reshape(KV_RANK, 2 * (QK_NOPE + V_HEAD)),
        "b_w_in": sh["w_bin"],
        "b_w_uq": dwq.reshape(Q_RANK, 2, HEAD_PAD)[:, :, :QK_NOPE + QK_ROPE].reshape(Q_RANK, -1),
        "b_w_out": sh["w_bout"],
    }


def _sub(ref, kind, j, cc):
    if kind == "lead":
        h = ref.shape[1] // 2
        return ref.at[j, pl.ds(pl.multiple_of(cc * h, 8), h), :]
    if kind == "row":
        rows = ref.shape[0] // N_CHIPS
        h = rows // 2
        return ref.at[pl.ds(pl.multiple_of(j * rows + cc * h, 8), h), :]
    cols = ref.shape[1] // N_CHIPS
    h = ref.shape[0] // 2
    return ref.at[pl.ds(pl.multiple_of(cc * h, 8), h), pl.ds(j * cols, cols)]


def _sub_shape(shape, kind):
    if kind == "lead":
        return (shape[1] // 2, shape[2])
    if kind == "row":
        return (shape[0] // N_CHIPS // 2, shape[1])
    return (shape[0] // 2, shape[1] // N_CHIPS)


def _reduce_grads(grads, kinds, vec):
    n = len(grads)
    shapes = [_sub_shape(a.shape, kd) for a, kd in zip(grads, kinds)]
    units = [(k, j) for k in range(n) for j in range(N_CHIPS)]
    big = (max(s[0] for s in shapes), max(s[1] for s in shapes))

    def body(*refs):
        g, v_ref = refs[:n], refs[n]
        out, o_ref = refs[n + 1:2 * n + 1], refs[2 * n + 1]
        theirs, part, recd, red = (refs[(2 + i) * n + 2:(3 + i) * n + 2] for i in range(4))
        mine, got = refs[6 * n + 2], refs[6 * n + 3]
        send1, recv1, send3, recv3, send5, recv5, load, local, send_v, recv_v = refs[6 * n + 4:]
        x, y, c = _place()
        j0 = 2 * x + y
        me = 2 * j0 + c
        sibling = (x, y, 1 - c)

        got[me] = v_ref[...]
        first, small_in = [], []
        for d in range(1, N_DEV):
            px, py, pc = x ^ (d >> 2), y ^ ((d >> 1) & 1), c ^ (d & 1)
            first.append(_remote(v_ref, got.at[me], send_v.at[d - 1], recv_v.at[d - 1], (px, py, pc)))
            small_in.append(_remote(v_ref, got.at[4 * px + 2 * py + pc], send_v.at[d - 1], recv_v.at[d - 1],
                                    (px, py, pc)))
        halves = [_remote(_sub(g[k], kinds[k], j, 1 - c), theirs[k].at[j], send1.at[u], recv1.at[u], sibling)
                  for u, (k, j) in enumerate(units)]
        for cp in first + halves:
            cp.start()

        def mine_load(u):
            k, j = units[u]
            h, cols = shapes[k]
            return pltpu.make_async_copy(_sub(g[k], kinds[k], j, c), mine.at[u % 2, pl.ds(0, h), pl.ds(0, cols)],
                                         load.at[u % 2])

        mine_load(0).start()
        for u, (k, j) in enumerate(units):
            h, cols = shapes[k]
            if u + 1 < len(units):
                mine_load(u + 1).start()
            mine_load(u).wait()
            halves[u].wait_recv()
            part[k][j] = (mine[u % 2, 0:h, 0:cols] + theirs[k][j]).astype(BF16)
            to_owner = _remote(part[k].at[j], recd[k].at[j0], send3.at[u], recv3.at[4 * k + j0], (j // 2, j % 2, c))

            @pl.when(j != j0)
            def _():
                to_owner.start()

            @pl.when(j == j0)
            def _():
                recd[k][j] = part[k][j]

        swaps = []
        for k in range(n):
            for j in range(N_CHIPS):
                arrived = _remote(part[k].at[j], recd[k].at[j], send3.at[4 * k + j], recv3.at[4 * k + j],
                                  (j // 2, j % 2, c))

                @pl.when(j != j0)
                def _():
                    arrived.wait_recv()

            r = recd[k]
            red[k][...] = ((r[0].astype(F32) + r[1].astype(F32)) + r[2].astype(F32)) + r[3].astype(F32)
            own = pltpu.make_async_copy(red[k], out[k].at[c], local.at[k])
            give = _remote(red[k], out[k].at[c], send5.at[k], recv5.at[k], sibling)
            take = _remote(red[k], out[k].at[1 - c], send5.at[k], recv5.at[k], sibling)
            own.start()
            give.start()
            swaps.append((own, give, take))

        for cp in small_in:
            cp.wait_recv()
            cp.wait_send()
        acc = got[0]
        for d in range(1, N_DEV):
            acc = acc + got[d]
        o_ref[...] = acc
        for u, (k, j) in enumerate(units):
            halves[u].wait_send()
            sent = _remote(part[k].at[j], recd[k].at[j0], send3.at[u], recv3.at[u], (j // 2, j % 2, c))

            @pl.when(j != j0)
            def _():
                sent.wait_send()
        for own, give, take in swaps:
            take.wait_recv()
            give.wait_send()
            own.wait()

    sems = [pltpu.SemaphoreType.DMA((len(units),))] * 4 + [pltpu.SemaphoreType.DMA((n,))] * 2 \
        + [pltpu.SemaphoreType.DMA((2,)), pltpu.SemaphoreType.DMA((n,))] + [pltpu.SemaphoreType.DMA((N_DEV - 1,))] * 2
    res = pl.pallas_call(
        body, name="reduce_grads",
        in_specs=[HBM] * n + [WHOLE], out_specs=[HBM] * n + [WHOLE],
        out_shape=[jax.ShapeDtypeStruct((2,) + s, F32) for s in shapes] + [jax.ShapeDtypeStruct(vec.shape, vec.dtype)],
        scratch_shapes=[pltpu.VMEM((N_CHIPS,) + s, F32) for s in shapes]
        + [pltpu.VMEM((N_CHIPS,) + s, BF16) for s in shapes] * 2
        + [pltpu.VMEM(s, F32) for s in shapes]
        + [pltpu.VMEM((2,) + big, F32), pltpu.VMEM((N_DEV,) + vec.shape, vec.dtype)] + sems,
        compiler_params=_comm_params(),
    )(*grads, vec)
    return res[:n], res[n]


def _adamw_math(w, g, m, v):
    m = ADAM_B1 * m + (1.0 - ADAM_B1) * g
    v = ADAM_B2 * v + (1.0 - ADAM_B2) * (g * g)
    m_hat = m / (1.0 - ADAM_B1 ** ADAM_STEP)
    v_hat = v / (1.0 - ADAM_B2 ** ADAM_STEP)
    return -ADAM_LR * (m_hat / (jnp.sqrt(v_hat) + ADAM_EPS) + ADAM_WD * w), m, v


def _adamw_tiled(w, g, m, v):
    rows, width = w.shape
    tm = rows // 4

    def body(w_ref, g_ref, m_ref, v_ref, go_ref, d_ref, mo_ref, vo_ref):
        g = g_ref[...]
        go_ref[...] = g
        d_ref[...], mo_ref[...], vo_ref[...] = _adamw_math(w_ref[...], g, m_ref[...], v_ref[...])

    spec = pl.BlockSpec((tm, width), lambda i: (i, 0))
    out = jax.ShapeDtypeStruct((rows, width), F32)
    return pl.pallas_call(
        body, grid=(rows // tm,), name="adamw_tiled",
        in_specs=[spec] * 4, out_specs=[spec] * 4, out_shape=[out] * 4,
        compiler_params=_params("parallel"),
    )(w, g, m, v)


def _adamw_many(ws, gs, ms, vs):
    n = len(ws)

    def body(*refs):
        for k in range(n):
            w_ref, g_ref, m_ref, v_ref = (refs[i * n + k] for i in range(4))
            go_ref, d_ref, mo_ref, vo_ref = (refs[(4 + i) * n + k] for i in range(4))
            g = g_ref[...]
            go_ref[...] = g
            d_ref[...], mo_ref[...], vo_ref[...] = _adamw_math(w_ref[...], g, m_ref[...], v_ref[...])

    outs = [jax.ShapeDtypeStruct(a.shape, F32) for a in ws]
    res = pl.pallas_call(
        body, name="adamw_many",
        in_specs=[WHOLE] * (4 * n), out_specs=[WHOLE] * (4 * n), out_shape=outs * 4,
        compiler_params=_comm_params(),
    )(*ws, *gs, *ms, *vs)
    return res[:n], res[n:2 * n], res[2 * n:3 * n], res[3 * n:]


def kernel(x, positions, a_norm, a_w_in, a_conv, a_w_out, kv_norm, w_dkv, ckv_norm, w_ukv, b_norm, b_w_in, b_q_norm, b_w_uq, b_w_out, final_norm, loss_target, m_a_norm, m_a_w_in, m_a_conv, m_a_w_out, m_kv_norm, m_w_dkv, m_ckv_norm, m_w_ukv, m_b_norm, m_b_w_in, m_b_q_norm, m_b_w_uq, m_b_w_out, m_final_norm, v_a_norm, v_a_w_in, v_a_conv, v_a_w_out, v_kv_norm, v_w_dkv, v_ckv_norm, v_w_ukv, v_b_norm, v_b_w_in, v_b_q_norm, v_b_w_uq, v_b_w_out, v_final_norm):
    w = dict(a_norm=a_norm, a_w_in=a_w_in, a_conv=a_conv, a_w_out=a_w_out, kv_norm=kv_norm, w_dkv=w_dkv,
             ckv_norm=ckv_norm, w_ukv=w_ukv, b_norm=b_norm, b_w_in=b_w_in, b_q_norm=b_q_norm, b_w_uq=b_w_uq,
             b_w_out=b_w_out, final_norm=final_norm)
    m = dict(a_norm=m_a_norm, a_w_in=m_a_w_in, a_conv=m_a_conv, a_w_out=m_a_w_out, kv_norm=m_kv_norm, w_dkv=m_w_dkv,
             ckv_norm=m_ckv_norm, w_ukv=m_w_ukv, b_norm=m_b_norm, b_w_in=m_b_w_in, b_q_norm=m_b_q_norm,
             b_w_uq=m_b_w_uq, b_w_out=m_b_w_out, final_norm=m_final_norm)
    v = dict(a_norm=v_a_norm, a_w_in=v_a_w_in, a_conv=v_a_conv, a_w_out=v_a_w_out, kv_norm=v_kv_norm, w_dkv=v_w_dkv,
             ckv_norm=v_ckv_norm, w_ukv=v_w_ukv, b_norm=v_b_norm, b_w_in=v_b_w_in, b_q_norm=v_b_q_norm,
             b_w_uq=v_b_w_uq, b_w_out=v_b_w_out, final_norm=v_final_norm)

    loss, dx, gmat, gsmall = _local_step(x, positions, loss_target, w)

    kinds = [GRAD_KIND[n] for n in GRAD_MATS]
    grads = [gmat[n] for n in GRAD_MATS]
    flat = jnp.concatenate([gsmall[n].reshape(-1) for n in SMALL] + [loss.reshape(1)])
    flat = jnp.pad(flat, (0, SMALL_ROWS * PACK_W - flat.shape[0])).reshape(SMALL_ROWS, PACK_W)
    mine, svec = _reduce_grads(grads, kinds, flat)
    loss = svec.reshape(-1)[LOSS_SLOT]
    g = _shard_grads({n: a.reshape(-1, a.shape[-1]) for n, a in zip(GRAD_MATS, mine)}, svec)

    def two_d(a):
        return a.reshape(-1, a.shape[-1])

    big = "a_w_in"
    rest = [n for n in WEIGHTS if n != big]
    res_big = _adamw_tiled(*(two_d(t[big]) for t in (w, g, m, v)))
    res_rest = _adamw_many(*([two_d(t[n]) for n in rest] for t in (w, g, m, v)))
    out = {kind: dict(zip(rest, res_rest[i])) for i, kind in enumerate("gdmv")}
    for i, kind in enumerate("gdmv"):
        out[kind][big] = res_big[i]
    return (loss, dx) + tuple(out[kind][n].reshape(SHARD_SHAPES[n]) for kind in "gdmv" for n in WEIGHTS)
```

```python
import functools
import math

import numpy as np
import jax
import jax.numpy as jnp
from jax import lax
from jax.experimental import pallas as pl
from jax.experimental.pallas import tpu as pltpu

F32 = jnp.float32
BF16 = jnp.bfloat16

D_MODEL = 1024
E_A = 1024
CONV_WIDTH = 3
N_HEADS = 8
QK_NOPE = 64
QK_ROPE = 32
V_HEAD = 64
KV_RANK = 256
Q_RANK = 384
E_B = N_HEADS * V_HEAD
ROPE_THETA = 10000.0
SOFTMAX_SCALE = 1.0 / math.sqrt(QK_NOPE + QK_ROPE)
LOG2E = math.log2(math.e)
LN2 = math.log(2.0)
Q_PRESCALE = SOFTMAX_SCALE * LOG2E
EPS = 1e-6
HEAD_PAD = 128
QK_PAD = N_HEADS * HEAD_PAD
ROPE_LO = QK_NOPE
ROPE_HALF = QK_ROPE // 2
KR_PAD = KV_RANK + HEAD_PAD

ADAM_LR = 0.001
ADAM_B1 = 0.9
ADAM_B2 = 0.999
ADAM_EPS = 1e-08
ADAM_WD = 0.01
ADAM_STEP = 10

VMEM_LIMIT = 56 * 1024 * 1024
ROW_TILE = 512
CONV_BWD_TILE = 256
ATT_TILE_FWD = 1024
ATT_TILE = 512
HEAD_CHAINS = 2
LANES = 128
PACK_W = 1024

N_CHIPS = 4
N_DEV = 8


def _dot(a, b):
    return jnp.dot(a, b, preferred_element_type=F32)


def _dot_nt(a, b):
    return lax.dot_general(a, b, (((1,), (1,)), ((), ())), preferred_element_type=F32)


def _dot_tn(a, b):
    return lax.dot_general(a, b, (((0,), (0,)), ((), ())), preferred_element_type=F32)


def _rms(x):
    r = lax.rsqrt(jnp.mean(x * x, axis=-1, keepdims=True) + EPS)
    return r, x * r


def _rms_bwd(dxh, xh, r):
    return r * (dxh - xh * jnp.mean(dxh * xh, axis=-1, keepdims=True))


def _rope_fwd(a, c, s1, s2):
    return a * c + pltpu.roll(a, HEAD_PAD - ROPE_HALF, 1) * s1 + pltpu.roll(a, ROPE_HALF, 1) * s2


def _rope_bwd(g, c, s1, s2):
    return g * c + pltpu.roll(g * s1, ROPE_HALF, 1) + pltpu.roll(g * s2, HEAD_PAD - ROPE_HALF, 1)


def _sigmoid(x):
    return 1.0 / (1.0 + jnp.exp(-x))


def _row_spec(tm, n):
    return pl.BlockSpec((tm, n), lambda i: (i, 0))


def _const_spec(shape):
    nd = len(shape)
    return pl.BlockSpec(shape, lambda i: (0,) * nd, pipeline_mode=pl.Buffered(1))


def _acc_spec(shape):
    nd = len(shape)
    return pl.BlockSpec(shape, lambda i: (0,) * nd, pipeline_mode=pl.Buffered(1))


def _params(*sem):
    return pltpu.CompilerParams(dimension_semantics=sem, vmem_limit_bytes=VMEM_LIMIT)


MESH = pl.DeviceIdType.MESH
HBM = pl.BlockSpec(memory_space=pl.ANY)
WHOLE = pl.BlockSpec(memory_space=pltpu.VMEM)
FLIPS = ((1, 0), (0, 1), (1, 1))


def _place():
    return lax.axis_index("x"), lax.axis_index("y"), lax.axis_index("c")


def _remote(src, dst, send, recv, peer):
    return pltpu.make_async_remote_copy(src_ref=src, dst_ref=dst, send_sem=send, recv_sem=recv, device_id=peer,
                                        device_id_type=MESH)


def _comm_params():
    return pltpu.CompilerParams(vmem_limit_bytes=VMEM_LIMIT)


def _rope_consts():
    lane = np.arange(HEAD_PAD)
    first = (lane >= ROPE_LO) & (lane < ROPE_LO + ROPE_HALF)
    second = (lane >= ROPE_LO + ROPE_HALF) & (lane < ROPE_LO + QK_ROPE)
    f = np.where(first, lane - ROPE_LO, np.where(second, lane - ROPE_LO - ROPE_HALF, 0))
    inv = np.float32(ROPE_THETA) ** (-(2 * f).astype(np.float32) / np.float32(QK_ROPE))
    out = np.zeros((8, HEAD_PAD), np.float32)
    out[0] = inv
    out[1] = first
    out[2] = second
    out[3] = lane < ROPE_LO
    return jnp.asarray(out)


def _gather_plan(w, sm, outs, osm, bf, sems):
    n = len(w)
    send_i, recv_i, send_d, recv_d, send_s, recv_s, local = sems
    x, y, c = _place()
    j0 = 2 * x + y
    sibling = (x, y, 1 - c)
    own = [pltpu.make_async_copy(bf[k], outs[k].at[j0], local.at[k]) for k in range(n)]
    if sm is not None:
        own.append(pltpu.make_async_copy(sm, osm.at[j0], local.at[n]))

    def half(k, cc):
        h = w[k].shape[0] // 2
        return pl.ds(pl.multiple_of(cc * h, 16), h)

    sends, arrivals, forwards, fwaits = [], [], [], []
    for i, (fx, fy) in enumerate(FLIPS):
        px, py = x ^ fx, y ^ fy
        pj = 2 * px + py
        for k in range(n):
            s = i * n + k
            sends.append(_remote(bf[k].at[half(k, c)], outs[k].at[j0, half(k, c)], send_i.at[s], recv_i.at[s],
                                 (px, py, c)))
            arrivals.append(_remote(bf[k].at[half(k, c)], outs[k].at[pj, half(k, c)], send_i.at[s], recv_i.at[s],
                                    (px, py, c)))
            forwards.append(_remote(outs[k].at[pj, half(k, c)], outs[k].at[pj, half(k, c)], send_d.at[s],
                                    recv_d.at[s], sibling))
            fwaits.append(_remote(outs[k].at[pj, half(k, 1 - c)], outs[k].at[pj, half(k, 1 - c)], send_d.at[s],
                                  recv_d.at[s], sibling))
        if sm is not None:
            sends.append(_remote(sm, osm.at[j0], send_s.at[i], recv_s.at[i], (px, py, c)))
            fwaits.append(_remote(sm, osm.at[pj], send_s.at[i], recv_s.at[i], (px, py, c)))
    return own, sends, arrivals, forwards, fwaits


def _gather_begin(w, bf, plan):
    own, sends, _, _, _ = plan
    for k in range(len(w)):
        bf[k][...] = w[k][...].astype(BF16)
    for cp in own + sends:
        cp.start()


def _gather_end(plan):
    own, sends, arrivals, forwards, fwaits = plan
    for cp, fwd in zip(arrivals, forwards):
        cp.wait_recv()
        fwd.start()
    for cp in fwaits:
        cp.wait_recv()
    for cp in sends + forwards:
        cp.wait_send()
    for cp in own:
        cp.wait()


def _gather_sems(n, with_small):
    return ([pltpu.SemaphoreType.DMA((3 * n,))] * 4 + [pltpu.SemaphoreType.DMA((3,))] * 2
            + [pltpu.SemaphoreType.DMA((n + (1 if with_small else 0),))])


def _rope_tables_gather(pos_col, w_in, small):
    t = pos_col.shape[0]
    tm = min(ROW_TILE, t)
    steps = t // tm

    def body(p_ref, k_ref, w_ref, sm_ref, c_ref, s1_ref, s2_ref, out_ref, osm_ref, bf_ref, *sems):
        plan = _gather_plan([w_ref], sm_ref, [out_ref], osm_ref, [bf_ref], sems)
        i = pl.program_id(0)

        @pl.when(i == 0)
        def _():
            _gather_begin([w_ref], [bf_ref], plan)

        inv, first, second, nope = k_ref[0:1, :], k_ref[1:2, :], k_ref[2:3, :], k_ref[3:4, :]
        ang = p_ref[...].astype(F32) * inv
        cs, sn = jnp.cos(ang), jnp.sin(ang)
        c_ref[...] = cs * (first + second) + nope
        s1_ref[...] = -sn * first
        s2_ref[...] = sn * second

        @pl.when(i == steps - 1)
        def _():
            _gather_end(plan)

    out = jax.ShapeDtypeStruct((t, HEAD_PAD), F32)
    return pl.pallas_call(
        body, grid=(steps,), name="rope_tables_gather",
        in_specs=[_row_spec(tm, 1), _const_spec((8, HEAD_PAD)), WHOLE, WHOLE],
        out_specs=[_row_spec(tm, HEAD_PAD)] * 3 + [HBM, HBM],
        out_shape=[out] * 3 + [jax.ShapeDtypeStruct((N_CHIPS,) + w_in.shape, BF16),
                               jax.ShapeDtypeStruct((N_CHIPS,) + small.shape, small.dtype)],
        scratch_shapes=[pltpu.VMEM(w_in.shape, BF16)] + _gather_sems(1, True),
        compiler_params=_params("arbitrary"),
    )(pos_col, _rope_consts(), w_in, small)


def _shift_down(v, prev, row):
    p1, p2 = prev[7:8, :], prev[6:7, :]
    v1 = jnp.where(row == 0, p1, pltpu.roll(v, 1, 0))
    v2 = jnp.where(row == 0, p2, jnp.where(row == 1, p1, pltpu.roll(v, 2, 0)))
    return v1, v2


def _conv_fwd(x, seq, ga, w_in4, cw, shards):
    t = x.shape[0]
    tm = min(ROW_TILE, seq)
    tiles_per_seq = seq // tm
    steps = t // tm
    n = len(shards)

    def body(x_ref, ga_ref, w_ref, cw_ref, *rest):
        sh, (b_ref, c_ref, u_ref, g_ref, ym_ref) = rest[:n], rest[n:n + 5]
        outs, carry_ref, bf, sems = rest[n + 5:2 * n + 5], rest[2 * n + 5], rest[2 * n + 6:3 * n + 6], rest[3 * n + 6:]
        plan = _gather_plan(sh, None, outs, None, bf, sems)
        i = pl.program_id(0)

        @pl.when(i == 0)
        def _():
            _gather_begin(sh, bf, plan)

        @pl.when(i % tiles_per_seq == 0)
        def _():
            carry_ref[...] = jnp.zeros_like(carry_ref)

        _, xh = _rms(x_ref[...])
        h = (xh * ga_ref[...]).astype(BF16)
        b, c, u, g = (_dot(h, w_ref[j]) for j in range(4))
        v = c * u
        row = lax.broadcasted_iota(jnp.int32, (tm, 1), 0)
        v1, v2 = _shift_down(v, carry_ref[...], row)
        carry_ref[...] = v[tm - 8:tm, :]
        cv = cw_ref[2:3, :] * v + cw_ref[1:2, :] * v1 + cw_ref[0:1, :] * v2
        b_ref[...] = b.astype(BF16)
        c_ref[...] = c.astype(BF16)
        u_ref[...] = u.astype(BF16)
        g_ref[...] = g.astype(BF16)
        ym_ref[...] = (g * _sigmoid(g) * b * cv).astype(BF16)

        @pl.when(i == steps - 1)
        def _():
            _gather_end(plan)

    out = jax.ShapeDtypeStruct((t, E_A), BF16)
    res = pl.pallas_call(
        body, grid=(steps,), name="conv_fwd",
        in_specs=[_row_spec(tm, D_MODEL), _const_spec((1, D_MODEL)), _const_spec((4, D_MODEL, E_A)),
                  _const_spec((8, E_A))] + [WHOLE] * n,
        out_specs=[_row_spec(tm, E_A)] * 5 + [HBM] * n,
        out_shape=[out] * 5 + [jax.ShapeDtypeStruct((N_CHIPS,) + a.shape, BF16) for a in shards],
        scratch_shapes=[pltpu.VMEM((8, E_A), F32)] + [pltpu.VMEM(a.shape, BF16) for a in shards]
        + _gather_sems(n, False),
        compiler_params=_params("arbitrary"),
    )(x, ga, w_in4, cw, *shards)
    return res[:5], res[5:]


def _mid_fwd(x, ym, w_out, gk, gb, w_dkv, gc, w_uk, w_uv, w_bin, gq, w_uq, rc, rs1, rs2):
    t = x.shape[0]
    tm = min(ROW_TILE, t)

    def body(x_ref, ym_ref, wo_ref, gk_ref, gb_ref, wd_ref, gc_ref, wk_ref, wv_ref, wi_ref, gq_ref, wq_ref,
             c_ref, s1_ref, s2_ref, x1_ref, q_ref, k_ref, v_ref, gate_ref, cq_ref, ckv_ref):
        cb, s1b, s2b = c_ref[...], s1_ref[...], s2_ref[...]
        x1 = x_ref[...] + _dot(ym_ref[...], wo_ref[...])
        x1_ref[...] = x1
        _, xh = _rms(x1)
        hk = (xh * gk_ref[...]).astype(BF16)
        h1 = (xh * gb_ref[...]).astype(BF16)

        pb = _dot(h1, wi_ref[...])
        cq = pb[:, :Q_RANK]
        cq_ref[...] = cq.astype(BF16)
        gate_ref[...] = pb[:, Q_RANK:].astype(BF16)
        _, cqh = _rms(cq)
        q = _dot((cqh * gq_ref[...]).astype(BF16), wq_ref[...])
        for h in range(N_HEADS):
            sl = slice(h * HEAD_PAD, (h + 1) * HEAD_PAD)
            q_ref[:, sl] = (_rope_fwd(q[:, sl], cb, s1b, s2b) * Q_PRESCALE).astype(BF16)

        ckr = _dot(hk, wd_ref[...])
        ckv_raw = ckr[:, :KV_RANK]
        ckv_ref[...] = ckv_raw.astype(BF16)
        _, ch = _rms(ckv_raw)
        ckv = (ch * gc_ref[...]).astype(BF16)
        kr = _rope_fwd(ckr[:, KV_RANK:], cb, s1b, s2b)
        kn = _dot(ckv, wk_ref[...])
        for h in range(N_HEADS):
            sl = slice(h * HEAD_PAD, (h + 1) * HEAD_PAD)
            k_ref[:, sl] = (kn[:, sl] + kr).astype(BF16)
        v_ref[...] = _dot(ckv, wv_ref[...]).astype(BF16)

    def sds(n, dt):
        return jax.ShapeDtypeStruct((t, n), dt)

    return pl.pallas_call(
        body, grid=(t // tm,), name="mid_fwd",
        in_specs=[_row_spec(tm, D_MODEL), _row_spec(tm, E_A), _const_spec((E_A, D_MODEL)),
                  _const_spec((1, D_MODEL)), _const_spec((1, D_MODEL)), _const_spec((D_MODEL, KR_PAD)),
                  _const_spec((1, KV_RANK)), _const_spec((KV_RANK, QK_PAD)), _const_spec((KV_RANK, E_B)),
                  _const_spec((D_MODEL, Q_RANK + E_B)), _const_spec((1, Q_RANK)), _const_spec((Q_RANK, QK_PAD)),
                  _row_spec(tm, HEAD_PAD), _row_spec(tm, HEAD_PAD), _row_spec(tm, HEAD_PAD)],
        out_specs=[_row_spec(tm, D_MODEL), _row_spec(tm, QK_PAD), _row_spec(tm, QK_PAD), _row_spec(tm, E_B),
                   _row_spec(tm, E_B), _row_spec(tm, Q_RANK), _row_spec(tm, KV_RANK)],
        out_shape=[sds(D_MODEL, F32), sds(QK_PAD, BF16), sds(QK_PAD, BF16), sds(E_B, BF16), sds(E_B, BF16),
                   sds(Q_RANK, BF16), sds(KV_RANK, BF16)],
        compiler_params=_params("parallel"),
    )(x, ym, w_out, gk, gb, w_dkv, gc, w_uk, w_uv, w_bin, gq, w_uq, rc, rs1, rs2)


def _pair_specs(seq):
    qk = pl.BlockSpec((seq, 2 * HEAD_PAD), lambda b, p: (b, p))
    vo = pl.BlockSpec((seq, 2 * V_HEAD), lambda b, p: (b, p))
    st = pl.BlockSpec((None, 2, seq), lambda b, p: (p, 0, b))
    return qk, vo, st


def _attn_fwd(q, k, v, seq):
    t = q.shape[0]
    tq = min(ATT_TILE_FWD, seq)
    nq = seq // tq

    def body(q_ref, k_ref, v_ref, o_ref, lse_ref, m_scr, l_scr, acc_scr):
        lane = lax.broadcasted_iota(jnp.int32, (tq, 2 * V_HEAD), 1)

        def q_step(qi, _):
            q0 = pl.multiple_of(qi * tq, tq)
            m_scr[...] = jnp.full(m_scr.shape, -jnp.inf, F32)
            l_scr[...] = jnp.zeros_like(l_scr)
            acc_scr[...] = jnp.zeros_like(acc_scr)

            def block(q_lo, q_n, k0, k_n, masked):
                rows = slice(q_lo, q_lo + q_n)
                vt = v_ref[pl.ds(k0, k_n), :]

                def score(hh):
                    hs = slice(hh * HEAD_PAD, (hh + 1) * HEAD_PAD)
                    return _dot_nt(q_ref[pl.ds(q0 + q_lo, q_n), hs], k_ref[pl.ds(k0, k_n), hs])

                early = [score(hh) for hh in range(2)] if masked else None
                for hh in range(2):
                    s = early[hh] if masked else score(hh)
                    if masked:
                        row = lax.broadcasted_iota(jnp.int32, (q_n, k_n), 0)
                        col = lax.broadcasted_iota(jnp.int32, (q_n, k_n), 1)
                        s = jnp.where(col <= row, s, -jnp.inf)
                    m_old = m_scr[hh, rows]
                    m_new = jnp.maximum(m_old, jnp.max(s, axis=-1, keepdims=True))
                    alpha = jnp.exp2(m_old - m_new)
                    ps = [jnp.exp2(s[:, j * LANES:(j + 1) * LANES] - m_new) for j in range(k_n // LANES)]
                    l_scr[hh, rows] = alpha * l_scr[hh, rows] + functools.reduce(lambda a, b: a + b, ps)
                    p = jnp.concatenate(ps, axis=-1).astype(BF16)
                    acc_scr[hh, rows] = alpha * acc_scr[hh, rows] + _dot(p, vt)
                    m_scr[hh, rows] = m_new

            def k_step(ki, _):
                block(0, tq, pl.multiple_of(ki * tq, tq), tq, False)
                return 0

            lax.fori_loop(0, qi, k_step, 0)
            half = tq // 2
            block(0, tq, q0, half, True)
            block(half, half, q0 + half, half, True)
            l0 = jnp.sum(l_scr[0], axis=-1, keepdims=True)
            l1 = jnp.sum(l_scr[1], axis=-1, keepdims=True)
            o_ref[pl.ds(q0, tq), :] = jnp.where(lane < V_HEAD, acc_scr[0] / l0, acc_scr[1] / l1).astype(BF16)
            stats = jnp.where(lane == 0, m_scr[0] + jnp.log2(l0), m_scr[1] + jnp.log2(l1)).T
            lse_ref[:, pl.ds(q0, tq)] = stats[0:2, :]
            return 0

        lax.fori_loop(0, nq, q_step, 0)

    qk, vo, st = _pair_specs(seq)
    return pl.pallas_call(
        body, grid=(t // seq, N_HEADS // 2), name="attn_fwd",
        in_specs=[qk, qk, vo], out_specs=[vo, st],
        out_shape=[jax.ShapeDtypeStruct((t, E_B), BF16), jax.ShapeDtypeStruct((N_HEADS // 2, 2, t), F32)],
        scratch_shapes=[pltpu.VMEM((2, tq, LANES), F32), pltpu.VMEM((2, tq, LANES), F32),
                        pltpu.VMEM((2, tq, 2 * V_HEAD), F32)],
        compiler_params=_params("parallel", "parallel"),
    )(q, k, v)


def _head_fwd_bwd(o, gate, x1, tgt, w_bout, gf):
    t = o.shape[0]
    tm = min(ROW_TILE, t)

    def body(o_ref, gate_ref, x1_ref, tgt_ref, w_ref, gf_ref,
             dx2_ref, do_ref, dgate_ref, dd_ref, loss_ref, dgf_ref, dw_ref):
        @pl.when(pl.program_id(0) == 0)
        def _():
            loss_ref[...] = jnp.zeros_like(loss_ref)
            dgf_ref[...] = jnp.zeros_like(dgf_ref)
            dw_ref[...] = jnp.zeros_like(dw_ref)

        hm = tm // HEAD_CHAINS
        gf = gf_ref[...]
        lane = lax.broadcasted_iota(jnp.int32, (hm, 2 * V_HEAD), 1)
        chains = []
        for ch in range(HEAD_CHAINS):
            rs = pl.ds(ch * hm, hm)
            o = o_ref[rs, :].astype(F32)
            gt = gate_ref[rs, :].astype(F32)
            sg = _sigmoid(gt)
            silu = gt * sg
            z = (o * silu).astype(BF16)
            chains.append((rs, o, gt, sg, silu, z, x1_ref[rs, :] + _dot(z, w_ref[...])))
        mids = []
        for rs, o, gt, sg, silu, z, x2 in chains:
            r2, xh2 = _rms(x2)
            err = xh2 * gf - tgt_ref[rs, :]
            loss_ref[...] += 0.5 * jnp.sum(jnp.mean(err * err, axis=-1, keepdims=True))
            dy = err * (1.0 / D_MODEL)
            dgf_ref[...] += jnp.sum(dy * xh2, axis=0, keepdims=True)
            dx2 = _rms_bwd(dy * gf, xh2, r2)
            dx2_ref[rs, :] = dx2
            dx2b = dx2.astype(BF16)
            dw_ref[...] += _dot_tn(z, dx2b)
            mids.append(_dot_nt(dx2b, w_ref[...]))
        for (rs, o, gt, sg, silu, z, x2), dz in zip(chains, mids):
            do = dz * silu
            do_ref[rs, :] = do.astype(BF16)
            dgate_ref[rs, :] = (dz * o * (sg * (1.0 + gt * (1.0 - sg)))).astype(BF16)
            prod = do * o
            cols = jnp.zeros((hm, LANES), F32)
            for p in range(N_HEADS // 2):
                blk = prod[:, p * 2 * V_HEAD:(p + 1) * 2 * V_HEAD]
                d0 = jnp.sum(jnp.where(lane < V_HEAD, blk, 0.0), axis=-1, keepdims=True)
                d1 = jnp.sum(jnp.where(lane < V_HEAD, 0.0, blk), axis=-1, keepdims=True)
                cols = jnp.where(lane == 2 * p, d0, jnp.where(lane == 2 * p + 1, d1, cols))
            rows = cols.T
            for h in range(N_HEADS):
                dd_ref[h // 2, h % 2:h % 2 + 1, rs] = rows[h:h + 1, :]

    return pl.pallas_call(
        body, grid=(t // tm,), name="head_fwd_bwd",
        in_specs=[_row_spec(tm, E_B), _row_spec(tm, E_B), _row_spec(tm, D_MODEL), _row_spec(tm, D_MODEL),
                  _const_spec((E_B, D_MODEL)), _const_spec((1, D_MODEL))],
        out_specs=[_row_spec(tm, D_MODEL), _row_spec(tm, E_B), _row_spec(tm, E_B),
                   pl.BlockSpec((N_HEADS // 2, 2, tm), lambda i: (0, 0, i)),
                   _acc_spec((1, 1)), _acc_spec((1, D_MODEL)), _acc_spec((E_B, D_MODEL))],
        out_shape=[jax.ShapeDtypeStruct((t, D_MODEL), F32), jax.ShapeDtypeStruct((t, E_B), BF16),
                   jax.ShapeDtypeStruct((t, E_B), BF16), jax.ShapeDtypeStruct((N_HEADS // 2, 2, t), F32),
                   jax.ShapeDtypeStruct((1, 1), F32), jax.ShapeDtypeStruct((1, D_MODEL), F32),
                   jax.ShapeDtypeStruct((E_B, D_MODEL), F32)],
        compiler_params=_params("arbitrary"),
    )(o, gate, x1, tgt, w_bout, gf)


def _attn_bwd(q, k, v, do, lse, dd, seq):
    t = q.shape[0]
    tq = min(ATT_TILE, seq)
    nq = seq // tq

    def body(q_ref, k_ref, v_ref, do_ref, lse_ref, dd_ref, dq_ref, dk_ref, dv_ref, dq_acc, dk_acc, dv_acc):
        dq_acc[...] = jnp.zeros_like(dq_acc)

        def k_step(ki, _):
            k0 = pl.multiple_of(ki * tq, tq)
            dk_acc[...] = jnp.zeros_like(dk_acc)
            dv_acc[...] = jnp.zeros_like(dv_acc)

            def block(k_lo, k_n, q0, q_n, masked):
                rows = slice(k_lo, k_lo + k_n)
                lane = lax.broadcasted_iota(jnp.int32, (q_n, 2 * V_HEAD), 1)
                vt = v_ref[pl.ds(k0 + k_lo, k_n), :]
                do_pair = do_ref[pl.ds(q0, q_n), :]

                def operands(hh):
                    hs = slice(hh * HEAD_PAD, (hh + 1) * HEAD_PAD)
                    kt = k_ref[pl.ds(k0 + k_lo, k_n), hs]
                    qt = q_ref[pl.ds(q0, q_n), hs]
                    mine = (lane < V_HEAD) if hh == 0 else (lane >= V_HEAD)
                    do_h = jnp.where(mine, do_pair, jnp.zeros((), BF16))
                    return hs, kt, qt, do_h, _dot_nt(kt, qt), _dot_nt(vt, do_h)

                early = [operands(hh) for hh in range(2)] if masked else None
                for hh in range(2):
                    hs, kt, qt, do_h, st, dpt = early[hh] if masked else operands(hh)
                    if masked:
                        krow = lax.broadcasted_iota(jnp.int32, (k_n, q_n), 0)
                        qcol = lax.broadcasted_iota(jnp.int32, (k_n, q_n), 1)
                        st = jnp.where(krow <= qcol, st, -jnp.inf)
                    pt = jnp.exp2(st - lse_ref[hh:hh + 1, pl.ds(q0, q_n)])
                    dst = (pt * (dpt - dd_ref[hh:hh + 1, pl.ds(q0, q_n)])).astype(BF16)
                    dv_acc[rows, :] += _dot(pt.astype(BF16), do_h)
                    dk_acc[rows, hs] += _dot(dst, qt)
                    dq_acc[pl.ds(q0, q_n), hs] += _dot_tn(dst, kt)

            def q_step(qi, _):
                block(0, tq, pl.multiple_of(qi * tq, tq), tq, False)
                return 0

            half = tq // 2
            block(0, half, k0, tq, True)
            block(half, half, pl.multiple_of(k0 + half, half), half, True)
            lax.fori_loop(ki + 1, nq, q_step, 0)
            dk_ref[pl.ds(k0, tq), :] = (dk_acc[...] * LN2).astype(BF16)
            dv_ref[pl.ds(k0, tq), :] = dv_acc[...].astype(BF16)
            return 0

        lax.fori_loop(0, nq, k_step, 0)
        dq_ref[...] = (dq_acc[...] * SOFTMAX_SCALE).astype(BF16)

    qk, vo, st = _pair_specs(seq)
    return pl.pallas_call(
        body, grid=(t // seq, N_HEADS // 2), name="attn_bwd",
        in_specs=[qk, qk, vo, vo, st, st], out_specs=[qk, qk, vo],
        out_shape=[jax.ShapeDtypeStruct((t, QK_PAD), BF16), jax.ShapeDtypeStruct((t, QK_PAD), BF16),
                   jax.ShapeDtypeStruct((t, E_B), BF16)],
        scratch_shapes=[pltpu.VMEM((seq, 2 * HEAD_PAD), F32), pltpu.VMEM((tq, 2 * HEAD_PAD), F32),
                        pltpu.VMEM((tq, 2 * V_HEAD), F32)],
        compiler_params=_params("parallel", "parallel"),
    )(q, k, v, do, lse, dd)


def _mid_bwd(dq, dk, dv, dgate, dx2, x1, cq, ckv, rc, rs1, rs2, w_uq, w_bin, w_uk, w_uv, w_dkv, gq, gc, gk, gb):
    t = dq.shape[0]
    tm = min(ROW_TILE, t)

    def body(dq_ref, dk_ref, dv_ref, dgate_ref, dx2_ref, x1_ref, cq_ref, ckv_ref, c_ref, s1_ref, s2_ref,
             wq_ref, wi_ref, wk_ref, wv_ref, wd_ref, gq_ref, gc_ref, gk_ref, gb_ref,
             dx1_ref, dwq_ref, dwi_ref, dwk_ref, dwv_ref, dwd_ref, dgq_ref, dgc_ref, dgk_ref, dgb_ref):
        @pl.when(pl.program_id(0) == 0)
        def _():
            for ref in (dwq_ref, dwi_ref, dwk_ref, dwv_ref, dwd_ref, dgq_ref, dgc_ref, dgk_ref, dgb_ref):
                ref[...] = jnp.zeros_like(ref)

        cb, s1b, s2b = c_ref[...], s1_ref[...], s2_ref[...]
        r1, xh = _rms(x1_ref[...])
        gk, gb, gq, gc = gk_ref[...], gb_ref[...], gq_ref[...], gc_ref[...]
        hk = (xh * gk).astype(BF16)
        h1 = (xh * gb).astype(BF16)

        rcv, ch = _rms(ckv_ref[...].astype(F32))
        ckvn = (ch * gc).astype(BF16)
        dkb, dvb = dk_ref[...], dv_ref[...]
        dwk_ref[...] += _dot_tn(ckvn, dkb)
        dwv_ref[...] += _dot_tn(ckvn, dvb)
        dckv = _dot_nt(dkb, wk_ref[...]) + _dot_nt(dvb, wv_ref[...])
        dgc_ref[...] += jnp.sum(dckv * ch, axis=0, keepdims=True)
        dckv_raw = _rms_bwd(dckv * gc, ch, rcv)
        dkr = dk_ref[:, 0:HEAD_PAD].astype(F32)
        for h in range(1, N_HEADS):
            dkr = dkr + dk_ref[:, h * HEAD_PAD:(h + 1) * HEAD_PAD].astype(F32)
        dkr = _rope_bwd(dkr, cb, s1b, s2b)
        dckr = jnp.concatenate([dckv_raw, dkr], axis=-1).astype(BF16)
        dwd_ref[...] += _dot_tn(hk, dckr)
        dhk = _dot_nt(dckr, wd_ref[...])

        dqs = [_rope_bwd(dq_ref[:, h * HEAD_PAD:(h + 1) * HEAD_PAD].astype(F32), cb, s1b, s2b)
               for h in range(N_HEADS)]
        dqb = jnp.concatenate(dqs, axis=-1).astype(BF16)
        rq, cqh = _rms(cq_ref[...].astype(F32))
        dwq_ref[...] += _dot_tn((cqh * gq).astype(BF16), dqb)
        dcqn = _dot_nt(dqb, wq_ref[...])
        dgq_ref[...] += jnp.sum(dcqn * cqh, axis=0, keepdims=True)
        dcq = _rms_bwd(dcqn * gq, cqh, rq)
        dpb = jnp.concatenate([dcq.astype(BF16), dgate_ref[...]], axis=-1)
        dwi_ref[...] += _dot_tn(h1, dpb)
        dh1 = _dot_nt(dpb, wi_ref[...])

        dgb_ref[...] += jnp.sum(dh1 * xh, axis=0, keepdims=True)
        dgk_ref[...] += jnp.sum(dhk * xh, axis=0, keepdims=True)
        dx1_ref[...] = dx2_ref[...] + _rms_bwd(dh1 * gb + dhk * gk, xh, r1)

    acc_shapes = [(Q_RANK, QK_PAD), (D_MODEL, Q_RANK + E_B), (KV_RANK, QK_PAD), (KV_RANK, E_B), (D_MODEL, KR_PAD),
                  (1, Q_RANK), (1, KV_RANK), (1, D_MODEL), (1, D_MODEL)]
    return pl.pallas_call(
        body, grid=(t // tm,), name="mid_bwd",
        in_specs=[_row_spec(tm, QK_PAD), _row_spec(tm, QK_PAD), _row_spec(tm, E_B), _row_spec(tm, E_B),
                  _row_spec(tm, D_MODEL), _row_spec(tm, D_MODEL), _row_spec(tm, Q_RANK), _row_spec(tm, KV_RANK),
                  _row_spec(tm, HEAD_PAD), _row_spec(tm, HEAD_PAD), _row_spec(tm, HEAD_PAD),
                  _const_spec((Q_RANK, QK_PAD)), _const_spec((D_MODEL, Q_RANK + E_B)),
                  _const_spec((KV_RANK, QK_PAD)), _const_spec((KV_RANK, E_B)), _const_spec((D_MODEL, KR_PAD)),
                  _const_spec((1, Q_RANK)), _const_spec((1, KV_RANK)), _const_spec((1, D_MODEL)),
                  _const_spec((1, D_MODEL))],
        out_specs=[_row_spec(tm, D_MODEL)] + [_acc_spec(s) for s in acc_shapes],
        out_shape=[jax.ShapeDtypeStruct((t, D_MODEL), F32)] + [jax.ShapeDtypeStruct(s, F32) for s in acc_shapes],
        compiler_params=_params("arbitrary"),
    )(dq, dk, dv, dgate, dx2, x1, cq, ckv, rc, rs1, rs2, w_uq, w_bin, w_uk, w_uv, w_dkv, gq, gc, gk, gb)


def _conv_bwd(dx1, x, b, c, u, g, seq, w_out, w_in4, ga, cw):
    t = x.shape[0]
    tm = min(CONV_BWD_TILE, seq)
    tiles_per_seq = seq // tm
    n = t // tm
    halo = tm // 8

    def tile(i):
        return n - 1 - i

    def rev(width):
        return pl.BlockSpec((tm, width), lambda i: (tile(i), 0))

    def prev8(width):
        return pl.BlockSpec((8, width), lambda i: (jnp.maximum(tile(i) * halo - 1, 0), 0))

    def body(dx1_ref, x_ref, b_ref, c_ref, u_ref, g_ref, cp_ref, up_ref, wo_ref, wi_ref, ga_ref, cw_ref,
             dx_ref, dwi_ref, dwo_ref, dcw_ref, dga_ref, carry_ref):
        i = pl.program_id(0)
        j = tile(i)

        @pl.when(i == 0)
        def _():
            for ref in (dwi_ref, dwo_ref, dcw_ref, dga_ref):
                ref[...] = jnp.zeros_like(ref)

        @pl.when(j % tiles_per_seq == tiles_per_seq - 1)
        def _():
            carry_ref[...] = jnp.zeros_like(carry_ref)

        dx1 = dx1_ref[...]
        dx1b = dx1.astype(BF16)
        b, c, u, g = (r[...].astype(F32) for r in (b_ref, c_ref, u_ref, g_ref))
        v = c * u
        first = (j % tiles_per_seq == 0).astype(F32)
        vprev = cp_ref[...].astype(F32) * up_ref[...].astype(F32) * (1.0 - first)
        row = lax.broadcasted_iota(jnp.int32, (tm, 1), 0)
        v1, v2 = _shift_down(v, vprev, row)
        w0, w1, w2 = cw_ref[0:1, :], cw_ref[1:2, :], cw_ref[2:3, :]
        cv = w2 * v + w1 * v1 + w0 * v2
        sg = _sigmoid(g)
        silu = g * sg
        ym = (silu * b * cv).astype(BF16)
        dwo_ref[...] += _dot_tn(ym, dx1b)
        dym = _dot_nt(dx1b, wo_ref[...])
        db = dym * silu * cv
        dcv = dym * silu * b
        dg = dym * b * cv * (sg * (1.0 + g * (1.0 - sg)))

        nxt = carry_ref[...]
        n0, n1 = nxt[0:1, :], nxt[1:2, :]
        d1 = jnp.where(row == tm - 1, n0, pltpu.roll(dcv, tm - 1, 0))
        d2 = jnp.where(row == tm - 1, n1, jnp.where(row == tm - 2, n0, pltpu.roll(dcv, tm - 2, 0)))
        carry_ref[...] = dcv[0:8, :]
        dv = w2 * dcv + w1 * d1 + w0 * d2
        dcw_ref[0:1, :] += jnp.sum(dcv * v2, axis=0, keepdims=True)
        dcw_ref[1:2, :] += jnp.sum(dcv * v1, axis=0, keepdims=True)
        dcw_ref[2:3, :] += jnp.sum(dcv * v, axis=0, keepdims=True)

        r0, xh = _rms(x_ref[...])
        ga = ga_ref[...]
        h = (xh * ga).astype(BF16)
        dh = jnp.zeros((tm, D_MODEL), F32)
        for idx, dpart in enumerate((db, dv * u, dv * c, dg)):
            dpb = dpart.astype(BF16)
            dwi_ref[idx] += _dot_tn(h, dpb)
            dh = dh + _dot_nt(dpb, wi_ref[idx])
        dga_ref[...] += jnp.sum(dh * xh, axis=0, keepdims=True)
        dx_ref[...] = dx1 + _rms_bwd(dh * ga, xh, r0)

    acc_shapes = [(4, D_MODEL, E_A), (E_A, D_MODEL), (8, E_A), (1, D_MODEL)]
    return pl.pallas_call(
        body, grid=(n,), name="conv_bwd",
        in_specs=[rev(D_MODEL), rev(D_MODEL), rev(E_A), rev(E_A), rev(E_A), rev(E_A), prev8(E_A), prev8(E_A),
                  _const_spec((E_A, D_MODEL)), _const_spec((4, D_MODEL, E_A)), _const_spec((1, D_MODEL)),
                  _const_spec((8, E_A))],
        out_specs=[rev(D_MODEL)] + [_acc_spec(s) for s in acc_shapes],
        out_shape=[jax.ShapeDtypeStruct((t, D_MODEL), F32)] + [jax.ShapeDtypeStruct(s, F32) for s in acc_shapes],
        scratch_shapes=[pltpu.VMEM((8, E_A), F32)],
        compiler_params=_params("arbitrary"),
    )(dx1, x, b, c, u, g, c, u, w_out, w_in4, ga, cw)


WEIGHTS = ("a_norm", "a_w_in", "a_conv", "a_w_out", "kv_norm", "w_dkv", "ckv_norm", "w_ukv", "b_norm", "b_w_in",
           "b_q_norm", "b_w_uq", "b_w_out", "final_norm")
SHARD_SHAPES = {
    "a_norm": (1, 256), "a_w_in": (1, 1024, 1024), "a_conv": (1, 3, 256), "a_w_out": (1, 256, 1024),
    "kv_norm": (1024,), "w_dkv": (256, 288), "ckv_norm": (256,), "w_ukv": (256, 256), "b_norm": (1, 1024),
    "b_w_in": (1, 256, 896), "b_q_norm": (1, 384), "b_w_uq": (1, 384, 192), "b_w_out": (1, 512, 256),
    "final_norm": (1024,),
}
MATS = ("a_w_in", "a_w_out", "w_dkv", "w_ukv", "b_w_in", "b_w_uq", "b_w_out")
SMALL = ("a_norm", "a_conv", "kv_norm", "ckv_norm", "b_norm", "b_q_norm", "final_norm")
SMALL_FULL = {"a_norm": 1024, "a_conv": 3072, "kv_norm": 1024, "ckv_norm": 256, "b_norm": 1024, "b_q_norm": 384,
              "final_norm": 1024}
SMALL_ROWS = 8
LOSS_SLOT = sum(SMALL_FULL.values())


def _mat2d(name, a):
    return a.reshape(SHARD_SHAPES[name][-2:])


def _prep_first(g_win, gsmall):
    sm = gsmall.reshape(N_CHIPS, -1)
    a_conv = jnp.transpose(sm[:, 256:1024].reshape(N_CHIPS, CONV_WIDTH, 256), (1, 0, 2)).reshape(CONV_WIDTH, -1)
    return {"w_in4": g_win, "ga": sm[:, :256].reshape(1, -1), "cw": jnp.pad(a_conv, ((0, 8 - CONV_WIDTH), (0, 0)))}


def _prep_rest(gath, w):
    def cols(a):
        return jnp.transpose(a, (1, 0, 2)).reshape(a.shape[1], -1)

    def pad_heads(a, width):
        a = a.reshape(a.shape[0], N_HEADS, width)
        return jnp.pad(a, ((0, 0), (0, 0), (0, HEAD_PAD - width))).reshape(a.shape[0], QK_PAD)

    row = lambda a: a.reshape(1, -1).astype(F32)
    w_dkv = gath["w_dkv"].reshape(D_MODEL, KV_RANK + QK_ROPE)
    w_ukv = cols(gath["w_ukv"]).reshape(KV_RANK, N_HEADS, 2, QK_NOPE)
    return {
        "w_out": gath["a_w_out"].reshape(E_A, D_MODEL),
        "w_dkv": jnp.concatenate([w_dkv[:, :KV_RANK], jnp.zeros((D_MODEL, ROPE_LO), BF16), w_dkv[:, KV_RANK:],
                                  jnp.zeros((D_MODEL, HEAD_PAD - ROPE_LO - QK_ROPE), BF16)], axis=1),
        "w_uk": pad_heads(w_ukv[:, :, 0, :].reshape(KV_RANK, N_HEADS * QK_NOPE), QK_NOPE),
        "w_uv": w_ukv[:, :, 1, :].reshape(KV_RANK, E_B),
        "w_bin": gath["b_w_in"].reshape(D_MODEL, Q_RANK + E_B),
        "w_uq": pad_heads(cols(gath["b_w_uq"]), QK_NOPE + QK_ROPE),
        "w_bout": cols(gath["b_w_out"]),
        "gk": row(w["kv_norm"]), "gc": row(w["ckv_norm"]), "gb": row(w["b_norm"]),
        "gq": row(w["b_q_norm"]), "gf": row(w["final_norm"]),
    }


GRAD_MATS = ("w_in4", "w_out", "w_dkv", "w_uk", "w_uv", "w_bin", "w_uq", "w_bout")
GRAD_KIND = {"w_in4": "lead", "w_out": "row", "w_dkv": "row", "w_uk": "col", "w_uv": "col", "w_bin": "row",
             "w_uq": "col", "w_bout": "col"}


def _local_step(x, positions, tgt, w):
    bsz, seq, _ = x.shape
    t = bsz * seq
    x2d = x.reshape(t, D_MODEL)
    small = jnp.concatenate([w["a_norm"].reshape(-1), w["a_conv"].reshape(-1)]).reshape(8, LANES)
    rc, rs1, rs2, g_win, gsmall = _rope_tables_gather(positions.reshape(t, 1), _mat2d(MATS[0], w[MATS[0]]), small)
    wk = _prep_first(g_win, gsmall)
    (b, c, u, g, ym), gathered = _conv_fwd(x2d, seq, wk["ga"], wk["w_in4"], wk["cw"],
                                           [_mat2d(n, w[n]) for n in MATS[1:]])
    wk.update(_prep_rest(dict(zip(MATS[1:], gathered)), w))
    x1, q, k, v, gate, cq, ckv = _mid_fwd(x2d, ym, wk["w_out"], wk["gk"], wk["gb"], wk["w_dkv"], wk["gc"], wk["w_uk"],
                                          wk["w_uv"], wk["w_bin"], wk["gq"], wk["w_uq"], rc, rs1, rs2)
    o, lse = _attn_fwd(q, k, v, seq)
    dx2, do, dgate, dd, loss, dgf, dw_bout = _head_fwd_bwd(o, gate, x1, tgt.reshape(t, D_MODEL), wk["w_bout"], wk["gf"])
    dq, dk, dv = _attn_bwd(q, k, v, do, lse, dd, seq)
    dx1, dwq, dw_bin, dwk, dwv, dwd, dgq, dgc, dgk, dgb = _mid_bwd(
        dq, dk, dv, dgate, dx2, x1, cq, ckv, rc, rs1, rs2, wk["w_uq"], wk["w_bin"], wk["w_uk"], wk["w_uv"], wk["w_dkv"],
        wk["gq"], wk["gc"], wk["gk"], wk["gb"])
    dx, dw_in4, dw_out, dcw, dga = _conv_bwd(dx1, x2d, b, c, u, g, seq, wk["w_out"], wk["w_in4"], wk["ga"], wk["cw"])
    mats = {"w_in4": dw_in4, "w_out": dw_out, "w_dkv": dwd, "w_uk": dwk, "w_uv": dwv, "w_bin": dw_bin, "w_uq": dwq,
            "w_bout": dw_bout}
    small = {"a_norm": dga, "a_conv": dcw[:CONV_WIDTH], "kv_norm": dgk, "ckv_norm": dgc, "b_norm": dgb,
             "b_q_norm": dgq, "final_norm": dgf}
    return loss[0, 0], dx.reshape(bsz, seq, D_MODEL), mats, small


def _shard_grads(sh, svec):
    j0 = 2 * lax.axis_index("x") + lax.axis_index("y")
    flat = svec.reshape(-1)
    off, small = 0, {}
    for n in SMALL:
        small[n] = flat[off:off + SMALL_FULL[n]]
        off += SMALL_FULL[n]
    dwd, dwk, dwv, dwq = sh["w_dkv"], sh["w_uk"], sh["w_uv"], sh["w_uq"]
    w_ukv = jnp.stack([dwk.reshape(KV_RANK, 2, HEAD_PAD)[:, :, :QK_NOPE], dwv.reshape(KV_RANK, 2, V_HEAD)], axis=2)
    return {
        "a_norm": lax.dynamic_slice(small["a_norm"], (j0 * 256,), (256,)),
        "a_conv": lax.dynamic_slice(small["a_conv"].reshape(CONV_WIDTH, E_A), (0, j0 * 256), (CONV_WIDTH, 256)),
        "kv_norm": small["kv_norm"], "ckv_norm": small["ckv_norm"], "b_norm": small["b_norm"],
        "b_q_norm": small["b_q_norm"], "final_norm": small["final_norm"],
        "a_w_in": sh["w_in4"], "a_w_out": sh["w_out"],
        "w_dkv": jnp.concatenate([dwd[:, :KV_RANK], dwd[:, KV_RANK + ROPE_LO:KV_RANK + ROPE_LO + QK_ROPE]], axis=1),
        "w_ukv": w_ukv.reshape(KV_RANK, 2 * (QK_NOPE + V_HEAD)),
        "b_w_in": sh["w_bin"],
        "b_w_uq": dwq.reshape(Q_RANK, 2, HEAD_PAD)[:, :, :QK_NOPE + QK_ROPE].reshape(Q_RANK, -1),
        "b_w_out": sh["w_bout"],
    }


def _sub(ref, kind, j, cc):
    if kind == "lead":
        h = ref.shape[1] // 2
        return ref.at[j, pl.ds(pl.multiple_of(cc * h, 8), h), :]
    if kind == "row":
        rows = ref.shape[0] // N_CHIPS
        h = rows // 2
        return ref.at[pl.ds(pl.multiple_of(j * rows + cc * h, 8), h), :]
    cols = ref.shape[1] // N_CHIPS
    h = ref.shape[0] // 2
    return ref.at[pl.ds(pl.multiple_of(cc * h, 8), h), pl.ds(j * cols, cols)]


def _sub_shape(shape, kind):
    if kind == "lead":
        return (shape[1] // 2, shape[2])
    if kind == "row":
        return (shape[0] // N_CHIPS // 2, shape[1])
    return (shape[0] // 2, shape[1] // N_CHIPS)


def _reduce_grads(grads, kinds, vec):
    n = len(grads)
    shapes = [_sub_shape(a.shape, kd) for a, kd in zip(grads, kinds)]
    units = [(k, j) for k in range(n) for j in range(N_CHIPS)]
    big = (max(s[0] for s in shapes), max(s[1] for s in shapes))

    def body(*refs):
        g, v_ref = refs[:n], refs[n]
        out, o_ref = refs[n + 1:2 * n + 1], refs[2 * n + 1]
        theirs, part, recd, red = (refs[(2 + i) * n + 2:(3 + i) * n + 2] for i in range(4))
        mine, got = refs[6 * n + 2], refs[6 * n + 3]
        send1, recv1, send3, recv3, send5, recv5, load, local, send_v, recv_v = refs[6 * n + 4:]
        x, y, c = _place()
        j0 = 2 * x + y
        me = 2 * j0 + c
        sibling = (x, y, 1 - c)

        got[me] = v_ref[...]
        first, small_in = [], []
        for d in range(1, N_DEV):
            px, py, pc = x ^ (d >> 2), y ^ ((d >> 1) & 1), c ^ (d & 1)
            first.append(_remote(v_ref, got.at[me], send_v.at[d - 1], recv_v.at[d - 1], (px, py, pc)))
            small_in.append(_remote(v_ref, got.at[4 * px + 2 * py + pc], send_v.at[d - 1], recv_v.at[d - 1],
                                    (px, py, pc)))
        halves = [_remote(_sub(g[k], kinds[k], j, 1 - c), theirs[k].at[j], send1.at[u], recv1.at[u], sibling)
                  for u, (k, j) in enumerate(units)]
        for cp in first + halves:
            cp.start()

        def mine_load(u):
            k, j = units[u]
            h, cols = shapes[k]
            return pltpu.make_async_copy(_sub(g[k], kinds[k], j, c), mine.at[u % 2, pl.ds(0, h), pl.ds(0, cols)],
                                         load.at[u % 2])

        mine_load(0).start()
        for u, (k, j) in enumerate(units):
            h, cols = shapes[k]
            if u + 1 < len(units):
                mine_load(u + 1).start()
            mine_load(u).wait()
            halves[u].wait_recv()
            part[k][j] = (mine[u % 2, 0:h, 0:cols] + theirs[k][j]).astype(BF16)
            to_owner = _remote(part[k].at[j], recd[k].at[j0], send3.at[u], recv3.at[4 * k + j0], (j // 2, j % 2, c))

            @pl.when(j != j0)
            def _():
                to_owner.start()

            @pl.when(j == j0)
            def _():
                recd[k][j] = part[k][j]

        swaps = []
        for k in range(n):
            for j in range(N_CHIPS):
                arrived = _remote(part[k].at[j], recd[k].at[j], send3.at[4 * k + j], recv3.at[4 * k + j],
                                  (j // 2, j % 2, c))

                @pl.when(j != j0)
                def _():
                    arrived.wait_recv()

            r = recd[k]
            red[k][...] = ((r[0].astype(F32) + r[1].astype(F32)) + r[2].astype(F32)) + r[3].astype(F32)
            own = pltpu.make_async_copy(red[k], out[k].at[c], local.at[k])
            give = _remote(red[k], out[k].at[c], send5.at[k], recv5.at[k], sibling)
            take = _remote(red[k], out[k].at[1 - c], send5.at[k], recv5.at[k], sibling)
            own.start()
            give.start()
            swaps.append((own, give, take))

        for cp in small_in:
            cp.wait_recv()
            cp.wait_send()
        acc = got[0]
        for d in range(1, N_DEV):
            acc = acc + got[d]
        o_ref[...] = acc
        for u, (k, j) in enumerate(units):
            halves[u].wait_send()
            sent = _remote(part[k].at[j], recd[k].at[j0], send3.at[u], recv3.at[u], (j // 2, j % 2, c))

            @pl.when(j != j0)
            def _():
                sent.wait_send()
        for own, give, take in swaps:
            take.wait_recv()
            give.wait_send()
            own.wait()

    sems = [pltpu.SemaphoreType.DMA((len(units),))] * 4 + [pltpu.SemaphoreType.DMA((n,))] * 2 \
        + [pltpu.SemaphoreType.DMA((2,)), pltpu.SemaphoreType.DMA((n,))] + [pltpu.SemaphoreType.DMA((N_DEV - 1,))] * 2
    res = pl.pallas_call(
        body, name="reduce_grads",
        in_specs=[HBM] * n + [WHOLE], out_specs=[HBM] * n + [WHOLE],
        out_shape=[jax.ShapeDtypeStruct((2,) + s, F32) for s in shapes] + [jax.ShapeDtypeStruct(vec.shape, vec.dtype)],
        scratch_shapes=[pltpu.VMEM((N_CHIPS,) + s, F32) for s in shapes]
        + [pltpu.VMEM((N_CHIPS,) + s, BF16) for s in shapes] * 2
        + [pltpu.VMEM(s, F32) for s in shapes]
        + [pltpu.VMEM((2,) + big, F32), pltpu.VMEM((N_DEV,) + vec.shape, vec.dtype)] + sems,
        compiler_params=_comm_params(),
    )(*grads, vec)
    return res[:n], res[n]


def _adamw_math(w, g, m, v):
    m = ADAM_B1 * m + (1.0 - ADAM_B1) * g
    v = ADAM_B2 * v + (1.0 - ADAM_B2) * (g * g)
    m_hat = m / (1.0 - ADAM_B1 ** ADAM_STEP)
    v_hat = v / (1.0 - ADAM_B2 ** ADAM_STEP)
    return -ADAM_LR * (m_hat / (jnp.sqrt(v_hat) + ADAM_EPS) + ADAM_WD * w), m, v


def _adamw_tiled(w, g, m, v):
    rows, width = w.shape
    tm = rows // 4

    def body(w_ref, g_ref, m_ref, v_ref, go_ref, d_ref, mo_ref, vo_ref):
        g = g_ref[...]
        go_ref[...] = g
        d_ref[...], mo_ref[...], vo_ref[...] = _adamw_math(w_ref[...], g, m_ref[...], v_ref[...])

    spec = pl.BlockSpec((tm, width), lambda i: (i, 0))
    out = jax.ShapeDtypeStruct((rows, width), F32)
    return pl.pallas_call(
        body, grid=(rows // tm,), name="adamw_tiled",
        in_specs=[spec] * 4, out_specs=[spec] * 4, out_shape=[out] * 4,
        compiler_params=_params("parallel"),
    )(w, g, m, v)


def _adamw_many(ws, gs, ms, vs):
    n = len(ws)

    def body(*refs):
        for k in range(n):
            w_ref, g_ref, m_ref, v_ref = (refs[i * n + k] for i in range(4))
            go_ref, d_ref, mo_ref, vo_ref = (refs[(4 + i) * n + k] for i in range(4))
            g = g_ref[...]
            go_ref[...] = g
            d_ref[...], mo_ref[...], vo_ref[...] = _adamw_math(w_ref[...], g, m_ref[...], v_ref[...])

    outs = [jax.ShapeDtypeStruct(a.shape, F32) for a in ws]
    res = pl.pallas_call(
        body, name="adamw_many",
        in_specs=[WHOLE] * (4 * n), out_specs=[WHOLE] * (4 * n), out_shape=outs * 4,
        compiler_params=_comm_params(),
    )(*ws, *gs, *ms, *vs)
    return res[:n], res[n:2 * n], res[2 * n:3 * n], res[3 * n:]


def kernel(x, positions, a_norm, a_w_in, a_conv, a_w_out, kv_norm, w_dkv, ckv_norm, w_ukv, b_norm, b_w_in, b_q_norm, b_w_uq, b_w_out, final_norm, loss_target, m_a_norm, m_a_w_in, m_a_conv, m_a_w_out, m_kv_norm, m_w_dkv, m_ckv_norm, m_w_ukv, m_b_norm, m_b_w_in, m_b_q_norm, m_b_w_uq, m_b_w_out, m_final_norm, v_a_norm, v_a_w_in, v_a_conv, v_a_w_out, v_kv_norm, v_w_dkv, v_ckv_norm, v_w_ukv, v_b_norm, v_b_w_in, v_b_q_norm, v_b_w_uq, v_b_w_out, v_final_norm):
    w = dict(a_norm=a_norm, a_w_in=a_w_in, a_conv=a_conv, a_w_out=a_w_out, kv_norm=kv_norm, w_dkv=w_dkv,
             ckv_norm=ckv_norm, w_ukv=w_ukv, b_norm=b_norm, b_w_in=b_w_in, b_q_norm=b_q_norm, b_w_uq=b_w_uq,
             b_w_out=b_w_out, final_norm=final_norm)
    m = dict(a_norm=m_a_norm, a_w_in=m_a_w_in, a_conv=m_a_conv, a_w_out=m_a_w_out, kv_norm=m_kv_norm, w_dkv=m_w_dkv,
             ckv_norm=m_ckv_norm, w_ukv=m_w_ukv, b_norm=m_b_norm, b_w_in=m_b_w_in, b_q_norm=m_b_q_norm,
             b_w_uq=m_b_w_uq, b_w_out=m_b_w_out, final_norm=m_final_norm)
    v = dict(a_norm=v_a_norm, a_w_in=v_a_w_in, a_conv=v_a_conv, a_w_out=v_a_w_out, kv_norm=v_kv_norm, w_dkv=v_w_dkv,
             ckv_norm=v_ckv_norm, w_ukv=v_w_ukv, b_norm=v_b_norm, b_w_in=v_b_w_in, b_q_norm=v_b_q_norm,
             b_w_uq=v_b_w_uq, b_w_out=v_b_w_out, final_norm=v_final_norm)

    loss, dx, gmat, gsmall = _local_step(x, positions, loss_target, w)

    kinds = [GRAD_KIND[n] for n in GRAD_MATS]
    grads = [gmat[n] for n in GRAD_MATS]
    flat = jnp.concatenate([gsmall[n].reshape(-1) for n in SMALL] + [loss.reshape(1)])
    flat = jnp.pad(flat, (0, SMALL_ROWS * PACK_W - flat.shape[0])).reshape(SMALL_ROWS, PACK_W)
    mine, svec = _reduce_grads(grads, kinds, flat)
    loss = svec.reshape(-1)[LOSS_SLOT]
    g = _shard_grads({n: a.reshape(-1, a.shape[-1]) for n, a in zip(GRAD_MATS, mine)}, svec)

    def two_d(a):
        return a.reshape(-1, a.shape[-1])

    big = "a_w_in"
    rest = [n for n in WEIGHTS if n != big]
    res_big = _adamw_tiled(*(two_d(t[big]) for t in (w, g, m, v)))
    res_rest = _adamw_many(*([two_d(t[n]) for n in rest] for t in (w, g, m, v)))
    out = {kind: dict(zip(rest, res_rest[i])) for i, kind in enumerate("gdmv")}
    for i, kind in enumerate("gdmv"):
        out[kind][big] = res_big[i]
    return (loss, dx) + tuple(out[kind][n].reshape(SHARD_SHAPES[n]) for kind in "gdmv" for n in WEIGHTS)
```

```python
import functools
import math

import numpy as np
import jax
import jax.numpy as jnp
from jax import lax
from jax.experimental import pallas as pl
from jax.experimental.pallas import tpu as pltpu

F32 = jnp.float32
BF16 = jnp.bfloat16

D_MODEL = 1024
E_A = 1024
CONV_WIDTH = 3
N_HEADS = 8
QK_NOPE = 64
QK_ROPE = 32
V_HEAD = 64
KV_RANK = 256
Q_RANK = 384
E_B = N_HEADS * V_HEAD
ROPE_THETA = 10000.0
SOFTMAX_SCALE = 1.0 / math.sqrt(QK_NOPE + QK_ROPE)
LOG2E = math.log2(math.e)
LN2 = math.log(2.0)
Q_PRESCALE = SOFTMAX_SCALE * LOG2E
EPS = 1e-6
HEAD_PAD = 128
QK_PAD = N_HEADS * HEAD_PAD
ROPE_LO = QK_NOPE
ROPE_HALF = QK_ROPE // 2
KR_PAD = KV_RANK + HEAD_PAD

ADAM_LR = 0.001
ADAM_B1 = 0.9
ADAM_B2 = 0.999
ADAM_EPS = 1e-08
ADAM_WD = 0.01
ADAM_STEP = 10

VMEM_LIMIT = 56 * 1024 * 1024
ROW_TILE = 512
CONV_BWD_TILE = 256
ATT_TILE_FWD = 1024
ATT_TILE = 512
HEAD_CHAINS = 2
LANES = 128
PACK_W = 1024

N_CHIPS = 4
N_DEV = 8


def _dot(a, b):
    return jnp.dot(a, b, preferred_element_type=F32)


def _dot_nt(a, b):
    return lax.dot_general(a, b, (((1,), (1,)), ((), ())), preferred_element_type=F32)


def _dot_tn(a, b):
    return lax.dot_general(a, b, (((0,), (0,)), ((), ())), preferred_element_type=F32)


def _rms(x):
    r = lax.rsqrt(jnp.mean(x * x, axis=-1, keepdims=True) + EPS)
    return r, x * r


def _rms_bwd(dxh, xh, r):
    return r * (dxh - xh * jnp.mean(dxh * xh, axis=-1, keepdims=True))


def _rope_fwd(a, c, s1, s2):
    return a * c + pltpu.roll(a, HEAD_PAD - ROPE_HALF, 1) * s1 + pltpu.roll(a, ROPE_HALF, 1) * s2


def _rope_bwd(g, c, s1, s2):
    return g * c + pltpu.roll(g * s1, ROPE_HALF, 1) + pltpu.roll(g * s2, HEAD_PAD - ROPE_HALF, 1)


def _sigmoid(x):
    return 1.0 / (1.0 + jnp.exp(-x))


def _row_spec(tm, n):
    return pl.BlockSpec((tm, n), lambda i: (i, 0))


def _const_spec(shape):
    nd = len(shape)
    return pl.BlockSpec(shape, lambda i: (0,) * nd, pipeline_mode=pl.Buffered(1))


def _acc_spec(shape):
    nd = len(shape)
    return pl.BlockSpec(shape, lambda i: (0,) * nd, pipeline_mode=pl.Buffered(1))


def _params(*sem):
    return pltpu.CompilerParams(dimension_semantics=sem, vmem_limit_bytes=VMEM_LIMIT)


MESH = pl.DeviceIdType.MESH
HBM = pl.BlockSpec(memory_space=pl.ANY)
WHOLE = pl.BlockSpec(memory_space=pltpu.VMEM)
FLIPS = ((1, 0), (0, 1), (1, 1))


def _place():
    return lax.axis_index("x"), lax.axis_index("y"), lax.axis_index("c")


def _remote(src, dst, send, recv, peer):
    return pltpu.make_async_remote_copy(src_ref=src, dst_ref=dst, send_sem=send, recv_sem=recv, device_id=peer,
                                        device_id_type=MESH)


def _comm_params():
    return pltpu.CompilerParams(vmem_limit_bytes=VMEM_LIMIT)


def _rope_consts():
    lane = np.arange(HEAD_PAD)
    first = (lane >= ROPE_LO) & (lane < ROPE_LO + ROPE_HALF)
    second = (lane >= ROPE_LO + ROPE_HALF) & (lane < ROPE_LO + QK_ROPE)
    f = np.where(first, lane - ROPE_LO, np.where(second, lane - ROPE_LO - ROPE_HALF, 0))
    inv = np.float32(ROPE_THETA) ** (-(2 * f).astype(np.float32) / np.float32(QK_ROPE))
    out = np.zeros((8, HEAD_PAD), np.float32)
    out[0] = inv
    out[1] = first
    out[2] = second
    out[3] = lane < ROPE_LO
    return jnp.asarray(out)


def _gather_plan(w, sm, outs, osm, bf, sems):
    n = len(w)
    send_i, recv_i, send_d, recv_d, send_s, recv_s, local = sems
    x, y, c = _place()
    j0 = 2 * x + y
    sibling = (x, y, 1 - c)
    own = [pltpu.make_async_copy(bf[k], outs[k].at[j0], local.at[k]) for k in range(n)]
    if sm is not None:
        own.append(pltpu.make_async_copy(sm, osm.at[j0], local.at[n]))

    def half(k, cc):
        h = w[k].shape[0] // 2
        return pl.ds(pl.multiple_of(cc * h, 16), h)

    sends, arrivals, forwards, fwaits = [], [], [], []
    for i, (fx, fy) in enumerate(FLIPS):
        px, py = x ^ fx, y ^ fy
        pj = 2 * px + py
        for k in range(n):
            s = i * n + k
            sends.append(_remote(bf[k].at[half(k, c)], outs[k].at[j0, half(k, c)], send_i.at[s], recv_i.at[s],
                                 (px, py, c)))
            arrivals.append(_remote(bf[k].at[half(k, c)], outs[k].at[pj, half(k, c)], send_i.at[s], recv_i.at[s],
                                    (px, py, c)))
            forwards.append(_remote(outs[k].at[pj, half(k, c)], outs[k].at[pj, half(k, c)], send_d.at[s],
                                    recv_d.at[s], sibling))
            fwaits.append(_remote(outs[k].at[pj, half(k, 1 - c)], outs[k].at[pj, half(k, 1 - c)], send_d.at[s],
                                  recv_d.at[s], sibling))
        if sm is not None:
            sends.append(_remote(sm, osm.at[j0], send_s.at[i], recv_s.at[i], (px, py, c)))
            fwaits.append(_remote(sm, osm.at[pj], send_s.at[i], recv_s.at[i], (px, py, c)))
    return own, sends, arrivals, forwards, fwaits


def _gather_begin(w, bf, plan):
    own, sends, _, _, _ = plan
    for k in range(len(w)):
        bf[k][...] = w[k][...].astype(BF16)
    for cp in own + sends:
        cp.start()


def _gather_end(plan):
    own, sends, arrivals, forwards, fwaits = plan
    for cp, fwd in zip(arrivals, forwards):
        cp.wait_recv()
        fwd.start()
    for cp in fwaits:
        cp.wait_recv()
    for cp in sends + forwards:
        cp.wait_send()
    for cp in own:
        cp.wait()


def _gather_sems(n, with_small):
    return ([pltpu.SemaphoreType.DMA((3 * n,))] * 4 + [pltpu.SemaphoreType.DMA((3,))] * 2
            + [pltpu.SemaphoreType.DMA((n + (1 if with_small else 0),))])


def _rope_tables_gather(pos_col, w_in, small):
    t = pos_col.shape[0]
    tm = min(ROW_TILE, t)
    steps = t // tm

    def body(p_ref, k_ref, w_ref, sm_ref, c_ref, s1_ref, s2_ref, out_ref, osm_ref, bf_ref, *sems):
        plan = _gather_plan([w_ref], sm_ref, [out_ref], osm_ref, [bf_ref], sems)
        i = pl.program_id(0)

        @pl.when(i == 0)
        def _():
            _gather_begin([w_ref], [bf_ref], plan)

        inv, first, second, nope = k_ref[0:1, :], k_ref[1:2, :], k_ref[2:3, :], k_ref[3:4, :]
        ang = p_ref[...].astype(F32) * inv
        cs, sn = jnp.cos(ang), jnp.sin(ang)
        c_ref[...] = cs * (first + second) + nope
        s1_ref[...] = -sn * first
        s2_ref[...] = sn * second

        @pl.when(i == steps - 1)
        def _():
            _gather_end(plan)

    out = jax.ShapeDtypeStruct((t, HEAD_PAD), F32)
    return pl.pallas_call(
        body, grid=(steps,), name="rope_tables_gather",
        in_specs=[_row_spec(tm, 1), _const_spec((8, HEAD_PAD)), WHOLE, WHOLE],
        out_specs=[_row_spec(tm, HEAD_PAD)] * 3 + [HBM, HBM],
        out_shape=[out] * 3 + [jax.ShapeDtypeStruct((N_CHIPS,) + w_in.shape, BF16),
                               jax.ShapeDtypeStruct((N_CHIPS,) + small.shape, small.dtype)],
        scratch_shapes=[pltpu.VMEM(w_in.shape, BF16)] + _gather_sems(1, True),
        compiler_params=_params("arbitrary"),
    )(pos_col, _rope_consts(), w_in, small)


def _shift_down(v, prev, row):
    p1, p2 = prev[7:8, :], prev[6:7, :]
    v1 = jnp.where(row == 0, p1, pltpu.roll(v, 1, 0))
    v2 = jnp.where(row == 0, p2, jnp.where(row == 1, p1, pltpu.roll(v, 2, 0)))
    return v1, v2


def _conv_fwd(x, seq, ga, w_in4, cw, shards):
    t = x.shape[0]
    tm = min(ROW_TILE, seq)
    tiles_per_seq = seq // tm
    steps = t // tm
    n = len(shards)

    def body(x_ref, ga_ref, w_ref, cw_ref, *rest):
        sh, (b_ref, c_ref, u_ref, g_ref, ym_ref) = rest[:n], rest[n:n + 5]
        outs, carry_ref, bf, sems = rest[n + 5:2 * n + 5], rest[2 * n + 5], rest[2 * n + 6:3 * n + 6], rest[3 * n + 6:]
        plan = _gather_plan(sh, None, outs, None, bf, sems)
        i = pl.program_id(0)

        @pl.when(i == 0)
        def _():
            _gather_begin(sh, bf, plan)

        @pl.when(i % tiles_per_seq == 0)
        def _():
            carry_ref[...] = jnp.zeros_like(carry_ref)

        _, xh = _rms(x_ref[...])
        h = (xh * ga_ref[...]).astype(BF16)
        b, c, u, g = (_dot(h, w_ref[j]) for j in range(4))
        v = c * u
        row = lax.broadcasted_iota(jnp.int32, (tm, 1), 0)
        v1, v2 = _shift_down(v, carry_ref[...], row)
        carry_ref[...] = v[tm - 8:tm, :]
        cv = cw_ref[2:3, :] * v + cw_ref[1:2, :] * v1 + cw_ref[0:1, :] * v2
        b_ref[...] = b.astype(BF16)
        c_ref[...] = c.astype(BF16)
        u_ref[...] = u.astype(BF16)
        g_ref[...] = g.astype(BF16)
        ym_ref[...] = (g * _sigmoid(g) * b * cv).astype(BF16)

        @pl.when(i == steps - 1)
        def _():
            _gather_end(plan)

    out = jax.ShapeDtypeStruct((t, E_A), BF16)
    res = pl.pallas_call(
        body, grid=(steps,), name="conv_fwd",
        in_specs=[_row_spec(tm, D_MODEL), _const_spec((1, D_MODEL)), _const_spec((4, D_MODEL, E_A)),
                  _const_spec((8, E_A))] + [WHOLE] * n,
        out_specs=[_row_spec(tm, E_A)] * 5 + [HBM] * n,
        out_shape=[out] * 5 + [jax.ShapeDtypeStruct((N_CHIPS,) + a.shape, BF16) for a in shards],
        scratch_shapes=[pltpu.VMEM((8, E_A), F32)] + [pltpu.VMEM(a.shape, BF16) for a in shards]
        + _gather_sems(n, False),
        compiler_params=_params("arbitrary"),
    )(x, ga, w_in4, cw, *shards)
    return res[:5], res[5:]


def _mid_fwd(x, ym, w_out, gk, gb, w_dkv, gc, w_uk, w_uv, w_bin, gq, w_uq, rc, rs1, rs2):
    t = x.shape[0]
    tm = min(ROW_TILE, t)

    def body(x_ref, ym_ref, wo_ref, gk_ref, gb_ref, wd_ref, gc_ref, wk_ref, wv_ref, wi_ref, gq_ref, wq_ref,
             c_ref, s1_ref, s2_ref, x1_ref, q_ref, k_ref, v_ref, gate_ref, cq_ref, ckv_ref):
        cb, s1b, s2b = c_ref[...], s1_ref[...], s2_ref[...]
        x1 = x_ref[...] + _dot(ym_ref[...], wo_ref[...])
        x1_ref[...] = x1
        _, xh = _rms(x1)
        hk = (xh * gk_ref[...]).astype(BF16)
        h1 = (xh * gb_ref[...]).astype(BF16)

        pb = _dot(h1, wi_ref[...])
        cq = pb[:, :Q_RANK]
        cq_ref[...] = cq.astype(BF16)
        gate_ref[...] = pb[:, Q_RANK:].astype(BF16)
        _, cqh = _rms(cq)
        q = _dot((cqh * gq_ref[...]).astype(BF16), wq_ref[...])
        for h in range(N_HEADS):
            sl = slice(h * HEAD_PAD, (h + 1) * HEAD_PAD)
            q_ref[:, sl] = (_rope_fwd(q[:, sl], cb, s1b, s2b) * Q_PRESCALE).astype(BF16)

        ckr = _dot(hk, wd_ref[...])
        ckv_raw = ckr[:, :KV_RANK]
        ckv_ref[...] = ckv_raw.astype(BF16)
        _, ch = _rms(ckv_raw)
        ckv = (ch * gc_ref[...]).astype(BF16)
        kr = _rope_fwd(ckr[:, KV_RANK:], cb, s1b, s2b)
        kn = _dot(ckv, wk_ref[...])
        for h in range(N_HEADS):
            sl = slice(h * HEAD_PAD, (h + 1) * HEAD_PAD)
            k_ref[:, sl] = (kn[:, sl] + kr).astype(BF16)
        v_ref[...] = _dot(ckv, wv_ref[...]).astype(BF16)

    def sds(n, dt):
        return jax.ShapeDtypeStruct((t, n), dt)

    return pl.pallas_call(
        body, grid=(t // tm,), name="mid_fwd",
        in_specs=[_row_spec(tm, D_MODEL), _row_spec(tm, E_A), _const_spec((E_A, D_MODEL)),
                  _const_spec((1, D_MODEL)), _const_spec((1, D_MODEL)), _const_spec((D_MODEL, KR_PAD)),
                  _const_spec((1, KV_RANK)), _const_spec((KV_RANK, QK_PAD)), _const_spec((KV_RANK, E_B)),
                  _const_spec((D_MODEL, Q_RANK + E_B)), _const_spec((1, Q_RANK)), _const_spec((Q_RANK, QK_PAD)),
                  _row_spec(tm, HEAD_PAD), _row_spec(tm, HEAD_PAD), _row_spec(tm, HEAD_PAD)],
        out_specs=[_row_spec(tm, D_MODEL), _row_spec(tm, QK_PAD), _row_spec(tm, QK_PAD), _row_spec(tm, E_B),
                   _row_spec(tm, E_B), _row_spec(tm, Q_RANK), _row_spec(tm, KV_RANK)],
        out_shape=[sds(D_MODEL, F32), sds(QK_PAD, BF16), sds(QK_PAD, BF16), sds(E_B, BF16), sds(E_B, BF16),
                   sds(Q_RANK, BF16), sds(KV_RANK, BF16)],
        compiler_params=_params("parallel"),
    )(x, ym, w_out, gk, gb, w_dkv, gc, w_uk, w_uv, w_bin, gq, w_uq, rc, rs1, rs2)


def _pair_specs(seq):
    qk = pl.BlockSpec((seq, 2 * HEAD_PAD), lambda b, p: (b, p))
    vo = pl.BlockSpec((seq, 2 * V_HEAD), lambda b, p: (b, p))
    st = pl.BlockSpec((None, 2, seq), lambda b, p: (p, 0, b))
    return qk, vo, st


def _attn_fwd(q, k, v, seq):
    t = q.shape[0]
    tq = min(ATT_TILE_FWD, seq)
    nq = seq // tq

    def body(q_ref, k_ref, v_ref, o_ref, lse_ref, m_scr, l_scr, acc_scr):
        lane = lax.broadcasted_iota(jnp.int32, (tq, 2 * V_HEAD), 1)

        def q_step(qi, _):
            q0 = pl.multiple_of(qi * tq, tq)
            m_scr[...] = jnp.full(m_scr.shape, -jnp.inf, F32)
            l_scr[...] = jnp.zeros_like(l_scr)
            acc_scr[...] = jnp.zeros_like(acc_scr)

            def block(q_lo, q_n, k0, k_n, masked):
                rows = slice(q_lo, q_lo + q_n)
                vt = v_ref[pl.ds(k0, k_n), :]

                def score(hh):
                    hs = slice(hh * HEAD_PAD, (hh + 1) * HEAD_PAD)
                    return _dot_nt(q_ref[pl.ds(q0 + q_lo, q_n), hs], k_ref[pl.ds(k0, k_n), hs])

                early = [score(hh) for hh in range(2)] if masked else None
                for hh in range(2):
                    s = early[hh] if masked else score(hh)
                    if masked:
                        row = lax.broadcasted_iota(jnp.int32, (q_n, k_n), 0)
                        col = lax.broadcasted_iota(jnp.int32, (q_n, k_n), 1)
                        s = jnp.where(col <= row, s, -jnp.inf)
                    m_old = m_scr[hh, rows]
                    m_new = jnp.maximum(m_old, jnp.max(s, axis=-1, keepdims=True))
                    alpha = jnp.exp2(m_old - m_new)
                    ps = [jnp.exp2(s[:, j * LANES:(j + 1) * LANES] - m_new) for j in range(k_n // LANES)]
                    l_scr[hh, rows] = alpha * l_scr[hh, rows] + functools.reduce(lambda a, b: a + b, ps)
                    p = jnp.concatenate(ps, axis=-1).astype(BF16)
                    acc_scr[hh, rows] = alpha * acc_scr[hh, rows] + _dot(p, vt)
                    m_scr[hh, rows] = m_new

            def k_step(ki, _):
                block(0, tq, pl.multiple_of(ki * tq, tq), tq, False)
                return 0

            lax.fori_loop(0, qi, k_step, 0)
            half = tq // 2
            block(0, tq, q0, half, True)
            block(half, half, q0 + half, half, True)
            l0 = jnp.sum(l_scr[0], axis=-1, keepdims=True)
            l1 = jnp.sum(l_scr[1], axis=-1, keepdims=True)
            o_ref[pl.ds(q0, tq), :] = jnp.where(lane < V_HEAD, acc_scr[0] / l0, acc_scr[1] / l1).astype(BF16)
            stats = jnp.where(lane == 0, m_scr[0] + jnp.log2(l0), m_scr[1] + jnp.log2(l1)).T
            lse_ref[:, pl.ds(q0, tq)] = stats[0:2, :]
            return 0

        lax.fori_loop(0, nq, q_step, 0)

    qk, vo, st = _pair_specs(seq)
    return pl.pallas_call(
        body, grid=(t // seq, N_HEADS // 2), name="attn_fwd",
        in_specs=[qk, qk, vo], out_specs=[vo, st],
        out_shape=[jax.ShapeDtypeStruct((t, E_B), BF16), jax.ShapeDtypeStruct((N_HEADS // 2, 2, t), F32)],
        scratch_shapes=[pltpu.VMEM((2, tq, LANES), F32), pltpu.VMEM((2, tq, LANES), F32),
                        pltpu.VMEM((2, tq, 2 * V_HEAD), F32)],
        compiler_params=_params("parallel", "parallel"),
    )(q, k, v)


def _head_fwd_bwd(o, gate, x1, tgt, w_bout, gf):
    t = o.shape[0]
    tm = min(ROW_TILE, t)

    def body(o_ref, gate_ref, x1_ref, tgt_ref, w_ref, gf_ref,
             dx2_ref, do_ref, dgate_ref, dd_ref, loss_ref, dgf_ref, dw_ref):
        @pl.when(pl.program_id(0) == 0)
        def _():
            loss_ref[...] = jnp.zeros_like(loss_ref)
            dgf_ref[...] = jnp.zeros_like(dgf_ref)
            dw_ref[...] = jnp.zeros_like(dw_ref)

        hm = tm // HEAD_CHAINS
        gf = gf_ref[...]
        lane = lax.broadcasted_iota(jnp.int32, (hm, 2 * V_HEAD), 1)
        chains = []
        for ch in range(HEAD_CHAINS):
            rs = pl.ds(ch * hm, hm)
            o = o_ref[rs, :].astype(F32)
            gt = gate_ref[rs, :].astype(F32)
            sg = _sigmoid(gt)
            silu = gt * sg
            z = (o * silu).astype(BF16)
            chains.append((rs, o, gt, sg, silu, z, x1_ref[rs, :] + _dot(z, w_ref[...])))
        mids = []
        for rs, o, gt, sg, silu, z, x2 in chains:
            r2, xh2 = _rms(x2)
            err = xh2 * gf - tgt_ref[rs, :]
            loss_ref[...] += 0.5 * jnp.sum(jnp.mean(err * err, axis=-1, keepdims=True))
            dy = err * (1.0 / D_MODEL)
            dgf_ref[...] += jnp.sum(dy * xh2, axis=0, keepdims=True)
            dx2 = _rms_bwd(dy * gf, xh2, r2)
            dx2_ref[rs, :] = dx2
            dx2b = dx2.astype(BF16)
            dw_ref[...] += _dot_tn(z, dx2b)
            mids.append(_dot_nt(dx2b, w_ref[...]))
        for (rs, o, gt, sg, silu, z, x2), dz in zip(chains, mids):
            do = dz * silu
            do_ref[rs, :] = do.astype(BF16)
            dgate_ref[rs, :] = (dz * o * (sg * (1.0 + gt * (1.0 - sg)))).astype(BF16)
            prod = do * o
            cols = jnp.zeros((hm, LANES), F32)
            for p in range(N_HEADS // 2):
                blk = prod[:, p * 2 * V_HEAD:(p + 1) * 2 * V_HEAD]
                d0 = jnp.sum(jnp.where(lane < V_HEAD, blk, 0.0), axis=-1, keepdims=True)
                d1 = jnp.sum(jnp.where(lane < V_HEAD, 0.0, blk), axis=-1, keepdims=True)
                cols = jnp.where(lane == 2 * p, d0, jnp.where(lane == 2 * p + 1, d1, cols))
            rows = cols.T
            for h in range(N_HEADS):
                dd_ref[h // 2, h % 2:h % 2 + 1, rs] = rows[h:h + 1, :]

    return pl.pallas_call(
        body, grid=(t // tm,), name="head_fwd_bwd",
        in_specs=[_row_spec(tm, E_B), _row_spec(tm, E_B), _row_spec(tm, D_MODEL), _row_spec(tm, D_MODEL),
                  _const_spec((E_B, D_MODEL)), _const_spec((1, D_MODEL))],
        out_specs=[_row_spec(tm, D_MODEL), _row_spec(tm, E_B), _row_spec(tm, E_B),
                   pl.BlockSpec((N_HEADS // 2, 2, tm), lambda i: (0, 0, i)),
                   _acc_spec((1, 1)), _acc_spec((1, D_MODEL)), _acc_spec((E_B, D_MODEL))],
        out_shape=[jax.ShapeDtypeStruct((t, D_MODEL), F32), jax.ShapeDtypeStruct((t, E_B), BF16),
                   jax.ShapeDtypeStruct((t, E_B), BF16), jax.ShapeDtypeStruct((N_HEADS // 2, 2, t), F32),
                   jax.ShapeDtypeStruct((1, 1), F32), jax.ShapeDtypeStruct((1, D_MODEL), F32),
                   jax.ShapeDtypeStruct((E_B, D_MODEL), F32)],
        compiler_params=_params("arbitrary"),
    )(o, gate, x1, tgt, w_bout, gf)


def _attn_bwd(q, k, v, do, lse, dd, seq):
    t = q.shape[0]
    tq = min(ATT_TILE, seq)
    nq = seq // tq

    def body(q_ref, k_ref, v_ref, do_ref, lse_ref, dd_ref, dq_ref, dk_ref, dv_ref, dq_acc, dk_acc, dv_acc):
        dq_acc[...] = jnp.zeros_like(dq_acc)

        def k_step(ki, _):
            k0 = pl.multiple_of(ki * tq, tq)
            dk_acc[...] = jnp.zeros_like(dk_acc)
            dv_acc[...] = jnp.zeros_like(dv_acc)

            def block(k_lo, k_n, q0, q_n, masked):
                rows = slice(k_lo, k_lo + k_n)
                lane = lax.broadcasted_iota(jnp.int32, (q_n, 2 * V_HEAD), 1)
                vt = v_ref[pl.ds(k0 + k_lo, k_n), :]
                do_pair = do_ref[pl.ds(q0, q_n), :]

                def operands(hh):
                    hs = slice(hh * HEAD_PAD, (hh + 1) * HEAD_PAD)
                    kt = k_ref[pl.ds(k0 + k_lo, k_n), hs]
                    qt = q_ref[pl.ds(q0, q_n), hs]
                    mine = (lane < V_HEAD) if hh == 0 else (lane >= V_HEAD)
                    do_h = jnp.where(mine, do_pair, jnp.zeros((), BF16))
                    return hs, kt, qt, do_h, _dot_nt(kt, qt), _dot_nt(vt, do_h)

                early = [operands(hh) for hh in range(2)] if masked else None
                for hh in range(2):
                    hs, kt, qt, do_h, st, dpt = early[hh] if masked else operands(hh)
                    if masked:
                        krow = lax.broadcasted_iota(jnp.int32, (k_n, q_n), 0)
                        qcol = lax.broadcasted_iota(jnp.int32, (k_n, q_n), 1)
                        st = jnp.where(krow <= qcol, st, -jnp.inf)
                    pt = jnp.exp2(st - lse_ref[hh:hh + 1, pl.ds(q0, q_n)])
                    dst = (pt * (dpt - dd_ref[hh:hh + 1, pl.ds(q0, q_n)])).astype(BF16)
                    dv_acc[rows, :] += _dot(pt.astype(BF16), do_h)
                    dk_acc[rows, hs] += _dot(dst, qt)
                    dq_acc[pl.ds(q0, q_n), hs] += _dot_tn(dst, kt)

            def q_step(qi, _):
                block(0, tq, pl.multiple_of(qi * tq, tq), tq, False)
                return 0

            half = tq // 2
            block(0, half, k0, tq, True)
            block(half, half, pl.multiple_of(k0 + half, half), half, True)
            lax.fori_loop(ki + 1, nq, q_step, 0)
            dk_ref[pl.ds(k0, tq), :] = (dk_acc[...] * LN2).astype(BF16)
            dv_ref[pl.ds(k0, tq), :] = dv_acc[...].astype(BF16)
            return 0

        lax.fori_loop(0, nq, k_step, 0)
        dq_ref[...] = (dq_acc[...] * SOFTMAX_SCALE).astype(BF16)

    qk, vo, st = _pair_specs(seq)
    return pl.pallas_call(
        body, grid=(t // seq, N_HEADS // 2), name="attn_bwd",
        in_specs=[qk, qk, vo, vo, st, st], out_specs=[qk, qk, vo],
        out_shape=[jax.ShapeDtypeStruct((t, QK_PAD), BF16), jax.ShapeDtypeStruct((t, QK_PAD), BF16),
                   jax.ShapeDtypeStruct((t, E_B), BF16)],
        scratch_shapes=[pltpu.VMEM((seq, 2 * HEAD_PAD), F32), pltpu.VMEM((tq, 2 * HEAD_PAD), F32),
                        pltpu.VMEM((tq, 2 * V_HEAD), F32)],
        compiler_params=_params("parallel", "parallel"),
    )(q, k, v, do, lse, dd)


def _mid_bwd(dq, dk, dv, dgate, dx2, x1, cq, ckv, rc, rs1, rs2, w_uq, w_bin, w_uk, w_uv, w_dkv, gq, gc, gk, gb):
    t = dq.shape[0]
    tm = min(ROW_TILE, t)

    def body(dq_ref, dk_ref, dv_ref, dgate_ref, dx2_ref, x1_ref, cq_ref, ckv_ref, c_ref, s1_ref, s2_ref,
             wq_ref, wi_ref, wk_ref, wv_ref, wd_ref, gq_ref, gc_ref, gk_ref, gb_ref,
             dx1_ref, dwq_ref, dwi_ref, dwk_ref, dwv_ref, dwd_ref, dgq_ref, dgc_ref, dgk_ref, dgb_ref):
        @pl.when(pl.program_id(0) == 0)
        def _():
            for ref in (dwq_ref, dwi_ref, dwk_ref, dwv_ref, dwd_ref, dgq_ref, dgc_ref, dgk_ref, dgb_ref):
                ref[...] = jnp.zeros_like(ref)

        cb, s1b, s2b = c_ref[...], s1_ref[...], s2_ref[...]
        r1, xh = _rms(x1_ref[...])
        gk, gb, gq, gc = gk_ref[...], gb_ref[...], gq_ref[...], gc_ref[...]
        hk = (xh * gk).astype(BF16)
        h1 = (xh * gb).astype(BF16)

        rcv, ch = _rms(ckv_ref[...].astype(F32))
        ckvn = (ch * gc).astype(BF16)
        dkb, dvb = dk_ref[...], dv_ref[...]
        dwk_ref[...] += _dot_tn(ckvn, dkb)
        dwv_ref[...] += _dot_tn(ckvn, dvb)
        dckv = _dot_nt(dkb, wk_ref[...]) + _dot_nt(dvb, wv_ref[...])
        dgc_ref[...] += jnp.sum(dckv * ch, axis=0, keepdims=True)
        dckv_raw = _rms_bwd(dckv * gc, ch, rcv)
        dkr = dk_ref[:, 0:HEAD_PAD].astype(F32)
        for h in range(1, N_HEADS):
            dkr = dkr + dk_ref[:, h * HEAD_PAD:(h + 1) * HEAD_PAD].astype(F32)
        dkr = _rope_bwd(dkr, cb, s1b, s2b)
        dckr = jnp.concatenate([dckv_raw, dkr], axis=-1).astype(BF16)
        dwd_ref[...] += _dot_tn(hk, dckr)
        dhk = _dot_nt(dckr, wd_ref[...])

        dqs = [_rope_bwd(dq_ref[:, h * HEAD_PAD:(h + 1) * HEAD_PAD].astype(F32), cb, s1b, s2b)
               for h in range(N_HEADS)]
        dqb = jnp.concatenate(dqs, axis=-1).astype(BF16)
        rq, cqh = _rms(cq_ref[...].astype(F32))
        dwq_ref[...] += _dot_tn((cqh * gq).astype(BF16), dqb)
        dcqn = _dot_nt(dqb, wq_ref[...])
        dgq_ref[...] += jnp.sum(dcqn * cqh, axis=0, keepdims=True)
        dcq = _rms_bwd(dcqn * gq, cqh, rq)
        dpb = jnp.concatenate([dcq.astype(BF16), dgate_ref[...]], axis=-1)
        dwi_ref[...] += _dot_tn(h1, dpb)
        dh1 = _dot_nt(dpb, wi_ref[...])

        dgb_ref[...] += jnp.sum(dh1 * xh, axis=0, keepdims=True)
        dgk_ref[...] += jnp.sum(dhk * xh, axis=0, keepdims=True)
        dx1_ref[...] = dx2_ref[...] + _rms_bwd(dh1 * gb + dhk * gk, xh, r1)

    acc_shapes = [(Q_RANK, QK_PAD), (D_MODEL, Q_RANK + E_B), (KV_RANK, QK_PAD), (KV_RANK, E_B), (D_MODEL, KR_PAD),
                  (1, Q_RANK), (1, KV_RANK), (1, D_MODEL), (1, D_MODEL)]
    return pl.pallas_call(
        body, grid=(t // tm,), name="mid_bwd",
        in_specs=[_row_spec(tm, QK_PAD), _row_spec(tm, QK_PAD), _row_spec(tm, E_B), _row_spec(tm, E_B),
                  _row_spec(tm, D_MODEL), _row_spec(tm, D_MODEL), _row_spec(tm, Q_RANK), _row_spec(tm, KV_RANK),
                  _row_spec(tm, HEAD_PAD), _row_spec(tm, HEAD_PAD), _row_spec(tm, HEAD_PAD),
                  _const_spec((Q_RANK, QK_PAD)), _const_spec((D_MODEL, Q_RANK + E_B)),
                  _const_spec((KV_RANK, QK_PAD)), _const_spec((KV_RANK, E_B)), _const_spec((D_MODEL, KR_PAD)),
                  _const_spec((1, Q_RANK)), _const_spec((1, KV_RANK)), _const_spec((1, D_MODEL)),
                  _const_spec((1, D_MODEL))],
        out_specs=[_row_spec(tm, D_MODEL)] + [_acc_spec(s) for s in acc_shapes],
        out_shape=[jax.ShapeDtypeStruct((t, D_MODEL), F32)] + [jax.ShapeDtypeStruct(s, F32) for s in acc_shapes],
        compiler_params=_params("arbitrary"),
    )(dq, dk, dv, dgate, dx2, x1, cq, ckv, rc, rs1, rs2, w_uq, w_bin, w_uk, w_uv, w_dkv, gq, gc, gk, gb)


def _conv_bwd(dx1, x, b, c, u, g, seq, w_out, w_in4, ga, cw):
    t = x.shape[0]
    tm = min(CONV_BWD_TILE, seq)
    tiles_per_seq = seq // tm
    n = t // tm
    halo = tm // 8

    def tile(i):
        return n - 1 - i

    def rev(width):
        return pl.BlockSpec((tm, width), lambda i: (tile(i), 0))

    def prev8(width):
        return pl.BlockSpec((8, width), lambda i: (jnp.maximum(tile(i) * halo - 1, 0), 0))

    def body(dx1_ref, x_ref, b_ref, c_ref, u_ref, g_ref, cp_ref, up_ref, wo_ref, wi_ref, ga_ref, cw_ref,
             dx_ref, dwi_ref, dwo_ref, dcw_ref, dga_ref, carry_ref):
        i = pl.program_id(0)
        j = tile(i)

        @pl.when(i == 0)
        def _():
            for ref in (dwi_ref, dwo_ref, dcw_ref, dga_ref):
                ref[...] = jnp.zeros_like(ref)

        @pl.when(j % tiles_per_seq == tiles_per_seq - 1)
        def _():
            carry_ref[...] = jnp.zeros_like(carry_ref)

        dx1 = dx1_ref[...]
        dx1b = dx1.astype(BF16)
        b, c, u, g = (r[...].astype(F32) for r in (b_ref, c_ref, u_ref, g_ref))
        v = c * u
        first = (j % tiles_per_seq == 0).astype(F32)
        vprev = cp_ref[...].astype(F32) * up_ref[...].astype(F32) * (1.0 - first)
        row = lax.broadcasted_iota(jnp.int32, (tm, 1), 0)
        v1, v2 = _shift_down(v, vprev, row)
        w0, w1, w2 = cw_ref[0:1, :], cw_ref[1:2, :], cw_ref[2:3, :]
        cv = w2 * v + w1 * v1 + w0 * v2
        sg = _sigmoid(g)
        silu = g * sg
        ym = (silu * b * cv).astype(BF16)
        dwo_ref[...] += _dot_tn(ym, dx1b)
        dym = _dot_nt(dx1b, wo_ref[...])
        db = dym * silu * cv
        dcv = dym * silu * b
        dg = dym * b * cv * (sg * (1.0 + g * (1.0 - sg)))

        nxt = carry_ref[...]
        n0, n1 = nxt[0:1, :], nxt[1:2, :]
        d1 = jnp.where(row == tm - 1, n0, pltpu.roll(dcv, tm - 1, 0))
        d2 = jnp.where(row == tm - 1, n1, jnp.where(row == tm - 2, n0, pltpu.roll(dcv, tm - 2, 0)))
        carry_ref[...] = dcv[0:8, :]
        dv = w2 * dcv + w1 * d1 + w0 * d2
        dcw_ref[0:1, :] += jnp.sum(dcv * v2, axis=0, keepdims=True)
        dcw_ref[1:2, :] += jnp.sum(dcv * v1, axis=0, keepdims=True)
        dcw_ref[2:3, :] += jnp.sum(dcv * v, axis=0, keepdims=True)

        r0, xh = _rms(x_ref[...])
        ga = ga_ref[...]
        h = (xh * ga).astype(BF16)
        dh = jnp.zeros((tm, D_MODEL), F32)
        for idx, dpart in enumerate((db, dv * u, dv * c, dg)):
            dpb = dpart.astype(BF16)
            dwi_ref[idx] += _dot_tn(h, dpb)
            dh = dh + _dot_nt(dpb, wi_ref[idx])
        dga_ref[...] += jnp.sum(dh * xh, axis=0, keepdims=True)
        dx_ref[...] = dx1 + _rms_bwd(dh * ga, xh, r0)

    acc_shapes = [(4, D_MODEL, E_A), (E_A, D_MODEL), (8, E_A), (1, D_MODEL)]
    return pl.pallas_call(
        body, grid=(n,), name="conv_bwd",
        in_specs=[rev(D_MODEL), rev(D_MODEL), rev(E_A), rev(E_A), rev(E_A), rev(E_A), prev8(E_A), prev8(E_A),
                  _const_spec((E_A, D_MODEL)), _const_spec((4, D_MODEL, E_A)), _const_spec((1, D_MODEL)),
                  _const_spec((8, E_A))],
        out_specs=[rev(D_MODEL)] + [_acc_spec(s) for s in acc_shapes],
        out_shape=[jax.ShapeDtypeStruct((t, D_MODEL), F32)] + [jax.ShapeDtypeStruct(s, F32) for s in acc_shapes],
        scratch_shapes=[pltpu.VMEM((8, E_A), F32)],
        compiler_params=_params("arbitrary"),
    )(dx1, x, b, c, u, g, c, u, w_out, w_in4, ga, cw)


WEIGHTS = ("a_norm", "a_w_in", "a_conv", "a_w_out", "kv_norm", "w_dkv", "ckv_norm", "w_ukv", "b_norm", "b_w_in",
           "b_q_norm", "b_w_uq", "b_w_out", "final_norm")
SHARD_SHAPES = {
    "a_norm": (1, 256), "a_w_in": (1, 1024, 1024), "a_conv": (1, 3, 256), "a_w_out": (1, 256, 1024),
    "kv_norm": (1024,), "w_dkv": (256, 288), "ckv_norm": (256,), "w_ukv": (256, 256), "b_norm": (1, 1024),
    "b_w_in": (1, 256, 896), "b_q_norm": (1, 384), "b_w_uq": (1, 384, 192), "b_w_out": (1, 512, 256),
    "final_norm": (1024,),
}
MATS = ("a_w_in", "a_w_out", "w_dkv", "w_ukv", "b_w_in", "b_w_uq", "b_w_out")
SMALL = ("a_norm", "a_conv", "kv_norm", "ckv_norm", "b_norm", "b_q_norm", "final_norm")
SMALL_FULL = {"a_norm": 1024, "a_conv": 3072, "kv_norm": 1024, "ckv_norm": 256, "b_norm": 1024, "b_q_norm": 384,
              "final_norm": 1024}
SMALL_ROWS = 8
LOSS_SLOT = sum(SMALL_FULL.values())


def _mat2d(name, a):
    return a.reshape(SHARD_SHAPES[name][-2:])


def _prep_first(g_win, gsmall):
    sm = gsmall.reshape(N_CHIPS, -1)
    a_conv = jnp.transpose(sm[:, 256:1024].reshape(N_CHIPS, CONV_WIDTH, 256), (1, 0, 2)).reshape(CONV_WIDTH, -1)
    return {"w_in4": g_win, "ga": sm[:, :256].reshape(1, -1), "cw": jnp.pad(a_conv, ((0, 8 - CONV_WIDTH), (0, 0)))}


def _prep_rest(gath, w):
    def cols(a):
        return jnp.transpose(a, (1, 0, 2)).reshape(a.shape[1], -1)

    def pad_heads(a, width):
        a = a.reshape(a.shape[0], N_HEADS, width)
        return jnp.pad(a, ((0, 0), (0, 0), (0, HEAD_PAD - width))).reshape(a.shape[0], QK_PAD)

    row = lambda a: a.reshape(1, -1).astype(F32)
    w_dkv = gath["w_dkv"].reshape(D_MODEL, KV_RANK + QK_ROPE)
    w_ukv = cols(gath["w_ukv"]).reshape(KV_RANK, N_HEADS, 2, QK_NOPE)
    return {
        "w_out": gath["a_w_out"].reshape(E_A, D_MODEL),
        "w_dkv": jnp.concatenate([w_dkv[:, :KV_RANK], jnp.zeros((D_MODEL, ROPE_LO), BF16), w_dkv[:, KV_RANK:],
                                  jnp.zeros((D_MODEL, HEAD_PAD - ROPE_LO - QK_ROPE), BF16)], axis=1),
        "w_uk": pad_heads(w_ukv[:, :, 0, :].reshape(KV_RANK, N_HEADS * QK_NOPE), QK_NOPE),
        "w_uv": w_ukv[:, :, 1, :].reshape(KV_RANK, E_B),
        "w_bin": gath["b_w_in"].reshape(D_MODEL, Q_RANK + E_B),
        "w_uq": pad_heads(cols(gath["b_w_uq"]), QK_NOPE + QK_ROPE),
        "w_bout": cols(gath["b_w_out"]),
        "gk": row(w["kv_norm"]), "gc": row(w["ckv_norm"]), "gb": row(w["b_norm"]),
        "gq": row(w["b_q_norm"]), "gf": row(w["final_norm"]),
    }


TRANSPOSED = ("w_dkv", "b_w_uq")
GRAD_MATS = ("w_in4", "w_out", "w_dkv", "w_uk", "w_uv", "w_bin", "w_uq", "w_bout")
GRAD_KIND = {"w_in4": "lead", "w_out": "row", "w_dkv": "row", "w_uk": "col", "w_uv": "col", "w_bin": "row",
             "w_uq": "col", "w_bout": "col"}


def _local_step(x, positions, tgt, w):
    bsz, seq, _ = x.shape
    t = bsz * seq
    x2d = x.reshape(t, D_MODEL)
    small = jnp.concatenate([w["a_norm"].reshape(-1), w["a_conv"].reshape(-1)]).reshape(8, LANES)
    rc, rs1, rs2, g_win, gsmall = _rope_tables_gather(positions.reshape(t, 1), _mat2d(MATS[0], w[MATS[0]]), small)
    wk = _prep_first(g_win, gsmall)
    (b, c, u, g, ym), gathered = _conv_fwd(x2d, seq, wk["ga"], wk["w_in4"], wk["cw"],
                                           [_mat2d(n, w[n]) for n in MATS[1:]])
    wk.update(_prep_rest(dict(zip(MATS[1:], gathered)), w))
    x1, q, k, v, gate, cq, ckv = _mid_fwd(x2d, ym, wk["w_out"], wk["gk"], wk["gb"], wk["w_dkv"], wk["gc"], wk["w_uk"],
                                          wk["w_uv"], wk["w_bin"], wk["gq"], wk["w_uq"], rc, rs1, rs2)
    o, lse = _attn_fwd(q, k, v, seq)
    dx2, do, dgate, dd, loss, dgf, dw_bout = _head_fwd_bwd(o, gate, x1, tgt.reshape(t, D_MODEL), wk["w_bout"], wk["gf"])
    dq, dk, dv = _attn_bwd(q, k, v, do, lse, dd, seq)
    dx1, dwq, dw_bin, dwk, dwv, dwd, dgq, dgc, dgk, dgb = _mid_bwd(
        dq, dk, dv, dgate, dx2, x1, cq, ckv, rc, rs1, rs2, wk["w_uq"], wk["w_bin"], wk["w_uk"], wk["w_uv"], wk["w_dkv"],
        wk["gq"], wk["gc"], wk["gk"], wk["gb"])
    dx, dw_in4, dw_out, dcw, dga = _conv_bwd(dx1, x2d, b, c, u, g, seq, wk["w_out"], wk["w_in4"], wk["ga"], wk["cw"])
    mats = {"w_in4": dw_in4, "w_out": dw_out, "w_dkv": dwd, "w_uk": dwk, "w_uv": dwv, "w_bin": dw_bin, "w_uq": dwq,
            "w_bout": dw_bout}
    small = {"a_norm": dga, "a_conv": dcw[:CONV_WIDTH], "kv_norm": dgk, "ckv_norm": dgc, "b_norm": dgb,
             "b_q_norm": dgq, "final_norm": dgf}
    return loss[0, 0], dx.reshape(bsz, seq, D_MODEL), mats, small


def _shard_grads(sh, svec):
    j0 = 2 * lax.axis_index("x") + lax.axis_index("y")
    flat = svec.reshape(-1)
    off, small = 0, {}
    for n in SMALL:
        small[n] = flat[off:off + SMALL_FULL[n]]
        off += SMALL_FULL[n]
    dwd, dwk, dwv, dwq = sh["w_dkv"], sh["w_uk"], sh["w_uv"], sh["w_uq"]
    w_ukv = jnp.stack([dwk.reshape(KV_RANK, 2, HEAD_PAD)[:, :, :QK_NOPE], dwv.reshape(KV_RANK, 2, V_HEAD)], axis=2)
    return {
        "a_norm": lax.dynamic_slice(small["a_norm"], (j0 * 256,), (256,)),
        "a_conv": lax.dynamic_slice(small["a_conv"].reshape(CONV_WIDTH, E_A), (0, j0 * 256), (CONV_WIDTH, 256)),
        "kv_norm": small["kv_norm"], "ckv_norm": small["ckv_norm"], "b_norm": small["b_norm"],
        "b_q_norm": small["b_q_norm"], "final_norm": small["final_norm"],
        "a_w_in": sh["w_in4"], "a_w_out": sh["w_out"],
        "w_dkv": jnp.concatenate([dwd[:, :KV_RANK], dwd[:, KV_RANK + ROPE_LO:KV_RANK + ROPE_LO + QK_ROPE]], axis=1),
        "w_ukv": w_ukv.reshape(KV_RANK, 2 * (QK_NOPE + V_HEAD)),
        "b_w_in": sh["w_bin"],
        "b_w_uq": dwq.reshape(Q_RANK, 2, HEAD_PAD)[:, :, :QK_NOPE + QK_ROPE].reshape(Q_RANK, -1),
        "b_w_out": sh["w_bout"],
    }


def _sub(ref, kind, j, cc):
    if kind == "lead":
        h = ref.shape[1] // 2
        return ref.at[j, pl.ds(pl.multiple_of(cc * h, 8), h), :]
    if kind == "row":
        rows = ref.shape[0] // N_CHIPS
        h = rows // 2
        return ref.at[pl.ds(pl.multiple_of(j * rows + cc * h, 8), h), :]
    cols = ref.shape[1] // N_CHIPS
    h = ref.shape[0] // 2
    return ref.at[pl.ds(pl.multiple_of(cc * h, 8), h), pl.ds(j * cols, cols)]


def _sub_shape(shape, kind):
    if kind == "lead":
        return (shape[1] // 2, shape[2])
    if kind == "row":
        return (shape[0] // N_CHIPS // 2, shape[1])
    return (shape[0] // 2, shape[1] // N_CHIPS)


def _reduce_grads(grads, kinds, vec):
    n = len(grads)
    shapes = [_sub_shape(a.shape, kd) for a, kd in zip(grads, kinds)]
    units = [(k, j) for k in range(n) for j in range(N_CHIPS)]
    big = (max(s[0] for s in shapes), max(s[1] for s in shapes))

    def body(*refs):
        g, v_ref = refs[:n], refs[n]
        out, o_ref = refs[n + 1:2 * n + 1], refs[2 * n + 1]
        theirs, part, recd, red = (refs[(2 + i) * n + 2:(3 + i) * n + 2] for i in range(4))
        mine, got = refs[6 * n + 2], refs[6 * n + 3]
        send1, recv1, send3, recv3, send5, recv5, load, local, send_v, recv_v = refs[6 * n + 4:]
        x, y, c = _place()
        j0 = 2 * x + y
        me = 2 * j0 + c
        sibling = (x, y, 1 - c)

        got[me] = v_ref[...]
        first, small_in = [], []
        for d in range(1, N_DEV):
            px, py, pc = x ^ (d >> 2), y ^ ((d >> 1) & 1), c ^ (d & 1)
            first.append(_remote(v_ref, got.at[me], send_v.at[d - 1], recv_v.at[d - 1], (px, py, pc)))
            small_in.append(_remote(v_ref, got.at[4 * px + 2 * py + pc], send_v.at[d - 1], recv_v.at[d - 1],
                                    (px, py, pc)))
        halves = [_remote(_sub(g[k], kinds[k], j, 1 - c), theirs[k].at[j], send1.at[u], recv1.at[u], sibling)
                  for u, (k, j) in enumerate(units)]
        for cp in first + halves:
            cp.start()

        def mine_load(u):
            k, j = units[u]
            h, cols = shapes[k]
            return pltpu.make_async_copy(_sub(g[k], kinds[k], j, c), mine.at[u % 2, pl.ds(0, h), pl.ds(0, cols)],
                                         load.at[u % 2])

        mine_load(0).start()
        for u, (k, j) in enumerate(units):
            h, cols = shapes[k]
            if u + 1 < len(units):
                mine_load(u + 1).start()
            mine_load(u).wait()
            halves[u].wait_recv()
            part[k][j] = (mine[u % 2, 0:h, 0:cols] + theirs[k][j]).astype(BF16)
            to_owner = _remote(part[k].at[j], recd[k].at[j0], send3.at[u], recv3.at[4 * k + j0], (j // 2, j % 2, c))

            @pl.when(j != j0)
            def _():
                to_owner.start()

            @pl.when(j == j0)
            def _():
                recd[k][j] = part[k][j]

        swaps = []
        for k in range(n):
            for j in range(N_CHIPS):
                arrived = _remote(part[k].at[j], recd[k].at[j], send3.at[4 * k + j], recv3.at[4 * k + j],
                                  (j // 2, j % 2, c))

                @pl.when(j != j0)
                def _():
                    arrived.wait_recv()

            r = recd[k]
            red[k][...] = ((r[0].astype(F32) + r[1].astype(F32)) + r[2].astype(F32)) + r[3].astype(F32)
            own = pltpu.make_async_copy(red[k], out[k].at[c], local.at[k])
            give = _remote(red[k], out[k].at[c], send5.at[k], recv5.at[k], sibling)
            take = _remote(red[k], out[k].at[1 - c], send5.at[k], recv5.at[k], sibling)
            own.start()
            give.start()
            swaps.append((own, give, take))

        for cp in small_in:
            cp.wait_recv()
            cp.wait_send()
        acc = got[0]
        for d in range(1, N_DEV):
            acc = acc + got[d]
        o_ref[...] = acc
        for u, (k, j) in enumerate(units):
            halves[u].wait_send()
            sent = _remote(part[k].at[j], recd[k].at[j0], send3.at[u], recv3.at[u], (j // 2, j % 2, c))

            @pl.when(j != j0)
            def _():
                sent.wait_send()
        for own, give, take in swaps:
            take.wait_recv()
            give.wait_send()
            own.wait()

    sems = [pltpu.SemaphoreType.DMA((len(units),))] * 4 + [pltpu.SemaphoreType.DMA((n,))] * 2 \
        + [pltpu.SemaphoreType.DMA((2,)), pltpu.SemaphoreType.DMA((n,))] + [pltpu.SemaphoreType.DMA((N_DEV - 1,))] * 2
    res = pl.pallas_call(
        body, name="reduce_grads",
        in_specs=[HBM] * n + [WHOLE], out_specs=[HBM] * n + [WHOLE],
        out_shape=[jax.ShapeDtypeStruct((2,) + s, F32) for s in shapes] + [jax.ShapeDtypeStruct(vec.shape, vec.dtype)],
        scratch_shapes=[pltpu.VMEM((N_CHIPS,) + s, F32) for s in shapes]
        + [pltpu.VMEM((N_CHIPS,) + s, BF16) for s in shapes] * 2
        + [pltpu.VMEM(s, F32) for s in shapes]
        + [pltpu.VMEM((2,) + big, F32), pltpu.VMEM((N_DEV,) + vec.shape, vec.dtype)] + sems,
        compiler_params=_comm_params(),
    )(*grads, vec)
    return res[:n], res[n]


def _adamw_math(w, g, m, v):
    m = ADAM_B1 * m + (1.0 - ADAM_B1) * g
    v = ADAM_B2 * v + (1.0 - ADAM_B2) * (g * g)
    m_hat = m / (1.0 - ADAM_B1 ** ADAM_STEP)
    v_hat = v / (1.0 - ADAM_B2 ** ADAM_STEP)
    return -ADAM_LR * (m_hat / (jnp.sqrt(v_hat) + ADAM_EPS) + ADAM_WD * w), m, v


def _adamw_tiled(w, g, m, v):
    rows, width = w.shape
    tm = rows // 4

    def body(w_ref, g_ref, m_ref, v_ref, go_ref, d_ref, mo_ref, vo_ref):
        g = g_ref[...]
        go_ref[...] = g
        d_ref[...], mo_ref[...], vo_ref[...] = _adamw_math(w_ref[...], g, m_ref[...], v_ref[...])

    spec = pl.BlockSpec((tm, width), lambda i: (i, 0))
    out = jax.ShapeDtypeStruct((rows, width), F32)
    return pl.pallas_call(
        body, grid=(rows // tm,), name="adamw_tiled",
        in_specs=[spec] * 4, out_specs=[spec] * 4, out_shape=[out] * 4,
        compiler_params=_params("parallel"),
    )(w, g, m, v)


def _adamw_many(ws, gs, ms, vs):
    n = len(ws)

    def body(*refs):
        for k in range(n):
            w_ref, g_ref, m_ref, v_ref = (refs[i * n + k] for i in range(4))
            go_ref, d_ref, mo_ref, vo_ref = (refs[(4 + i) * n + k] for i in range(4))
            g = g_ref[...]
            go_ref[...] = g
            d_ref[...], mo_ref[...], vo_ref[...] = _adamw_math(w_ref[...], g, m_ref[...], v_ref[...])

    outs = [jax.ShapeDtypeStruct(a.shape, F32) for a in ws]
    res = pl.pallas_call(
        body, name="adamw_many",
        in_specs=[WHOLE] * (4 * n), out_specs=[WHOLE] * (4 * n), out_shape=outs * 4,
        compiler_params=_comm_params(),
    )(*ws, *gs, *ms, *vs)
    return res[:n], res[n:2 * n], res[2 * n:3 * n], res[3 * n:]


def kernel(x, positions, a_norm, a_w_in, a_conv, a_w_out, kv_norm, w_dkv, ckv_norm, w_ukv, b_norm, b_w_in, b_q_norm, b_w_uq, b_w_out, final_norm, loss_target, m_a_norm, m_a_w_in, m_a_conv, m_a_w_out, m_kv_norm, m_w_dkv, m_ckv_norm, m_w_ukv, m_b_norm, m_b_w_in, m_b_q_norm, m_b_w_uq, m_b_w_out, m_final_norm, v_a_norm, v_a_w_in, v_a_conv, v_a_w_out, v_kv_norm, v_w_dkv, v_ckv_norm, v_w_ukv, v_b_norm, v_b_w_in, v_b_q_norm, v_b_w_uq, v_b_w_out, v_final_norm):
    w = dict(a_norm=a_norm, a_w_in=a_w_in, a_conv=a_conv, a_w_out=a_w_out, kv_norm=kv_norm, w_dkv=w_dkv,
             ckv_norm=ckv_norm, w_ukv=w_ukv, b_norm=b_norm, b_w_in=b_w_in, b_q_norm=b_q_norm, b_w_uq=b_w_uq,
             b_w_out=b_w_out, final_norm=final_norm)
    m = dict(a_norm=m_a_norm, a_w_in=m_a_w_in, a_conv=m_a_conv, a_w_out=m_a_w_out, kv_norm=m_kv_norm, w_dkv=m_w_dkv,
             ckv_norm=m_ckv_norm, w_ukv=m_w_ukv, b_norm=m_b_norm, b_w_in=m_b_w_in, b_q_norm=m_b_q_norm,
             b_w_uq=m_b_w_uq, b_w_out=m_b_w_out, final_norm=m_final_norm)
    v = dict(a_norm=v_a_norm, a_w_in=v_a_w_in, a_conv=v_a_conv, a_w_out=v_a_w_out, kv_norm=v_kv_norm, w_dkv=v_w_dkv,
             ckv_norm=v_ckv_norm, w_ukv=v_w_ukv, b_norm=v_b_norm, b_w_in=v_b_w_in, b_q_norm=v_b_q_norm,
             b_w_uq=v_b_w_uq, b_w_out=v_b_w_out, final_norm=v_final_norm)

    loss, dx, gmat, gsmall = _local_step(x, positions, loss_target, w)

    kinds = [GRAD_KIND[n] for n in GRAD_MATS]
    grads = [gmat[n] for n in GRAD_MATS]
    flat = jnp.concatenate([gsmall[n].reshape(-1) for n in SMALL] + [loss.reshape(1)])
    flat = jnp.pad(flat, (0, SMALL_ROWS * PACK_W - flat.shape[0])).reshape(SMALL_ROWS, PACK_W)
    mine, svec = _reduce_grads(grads, kinds, flat)
    loss = svec.reshape(-1)[LOSS_SLOT]
    g = _shard_grads({n: a.reshape(-1, a.shape[-1]) for n, a in zip(GRAD_MATS, mine)}, svec)

    def two_d(n, a):
        a = a.reshape(-1, a.shape[-1])
        return a.T if n in TRANSPOSED else a

    def back(n, a):
        return (a.T if n in TRANSPOSED else a).reshape(SHARD_SHAPES[n])

    big = "a_w_in"
    rest = [n for n in WEIGHTS if n != big]
    res_big = _adamw_tiled(*(two_d(big, t[big]) for t in (w, g, m, v)))
    res_rest = _adamw_many(*([two_d(n, t[n]) for n in rest] for t in (w, g, m, v)))
    out = {kind: dict(zip(rest, res_rest[i])) for i, kind in enumerate("gdmv")}
    for i, kind in enumerate("gdmv"):
        out[kind][big] = res_big[i]
    return (loss, dx) + tuple(back(n, out[kind][n]) for kind in "gdmv" for n in WEIGHTS)
```

```python
import functools
import math

import numpy as np
import jax
import jax.numpy as jnp
from jax import lax
from jax.experimental import pallas as pl
from jax.experimental.pallas import tpu as pltpu

F32 = jnp.float32
BF16 = jnp.bfloat16

D_MODEL = 1024
E_A = 1024
CONV_WIDTH = 3
N_HEADS = 8
QK_NOPE = 64
QK_ROPE = 32
V_HEAD = 64
KV_RANK = 256
Q_RANK = 384
E_B = N_HEADS * V_HEAD
ROPE_THETA = 10000.0
SOFTMAX_SCALE = 1.0 / math.sqrt(QK_NOPE + QK_ROPE)
LOG2E = math.log2(math.e)
LN2 = math.log(2.0)
Q_PRESCALE = SOFTMAX_SCALE * LOG2E
EPS = 1e-6
HEAD_PAD = 128
QK_PAD = N_HEADS * HEAD_PAD
ROPE_LO = QK_NOPE
ROPE_HALF = QK_ROPE // 2
KR_PAD = KV_RANK + HEAD_PAD

ADAM_LR = 0.001
ADAM_B1 = 0.9
ADAM_B2 = 0.999
ADAM_EPS = 1e-08
ADAM_WD = 0.01
ADAM_STEP = 10

VMEM_LIMIT = 56 * 1024 * 1024
ROW_TILE = 512
CONV_BWD_TILE = 256
ATT_TILE_FWD = 1024
ATT_TILE = 512
HEAD_CHAINS = 2
LANES = 128
PACK_W = 1024

N_CHIPS = 4
N_DEV = 8


def _dot(a, b):
    return jnp.dot(a, b, preferred_element_type=F32)


def _dot_nt(a, b):
    return lax.dot_general(a, b, (((1,), (1,)), ((), ())), preferred_element_type=F32)


def _dot_tn(a, b):
    return lax.dot_general(a, b, (((0,), (0,)), ((), ())), preferred_element_type=F32)


def _rms(x):
    r = lax.rsqrt(jnp.mean(x * x, axis=-1, keepdims=True) + EPS)
    return r, x * r


def _rms_bwd(dxh, xh, r):
    return r * (dxh - xh * jnp.mean(dxh * xh, axis=-1, keepdims=True))


def _rope_fwd(a, c, s1, s2):
    return a * c + pltpu.roll(a, HEAD_PAD - ROPE_HALF, 1) * s1 + pltpu.roll(a, ROPE_HALF, 1) * s2


def _rope_bwd(g, c, s1, s2):
    return g * c + pltpu.roll(g * s1, ROPE_HALF, 1) + pltpu.roll(g * s2, HEAD_PAD - ROPE_HALF, 1)


def _sigmoid(x):
    return 1.0 / (1.0 + jnp.exp(-x))


def _row_spec(tm, n):
    return pl.BlockSpec((tm, n), lambda i: (i, 0))


def _const_spec(shape):
    nd = len(shape)
    return pl.BlockSpec(shape, lambda i: (0,) * nd, pipeline_mode=pl.Buffered(1))


def _acc_spec(shape):
    nd = len(shape)
    return pl.BlockSpec(shape, lambda i: (0,) * nd, pipeline_mode=pl.Buffered(1))


def _params(*sem):
    return pltpu.CompilerParams(dimension_semantics=sem, vmem_limit_bytes=VMEM_LIMIT)


MESH = pl.DeviceIdType.MESH
HBM = pl.BlockSpec(memory_space=pl.ANY)
WHOLE = pl.BlockSpec(memory_space=pltpu.VMEM)
FLIPS = ((1, 0), (0, 1), (1, 1))


def _place():
    return lax.axis_index("x"), lax.axis_index("y"), lax.axis_index("c")


def _remote(src, dst, send, recv, peer):
    return pltpu.make_async_remote_copy(src_ref=src, dst_ref=dst, send_sem=send, recv_sem=recv, device_id=peer,
                                        device_id_type=MESH)


def _comm_params():
    return pltpu.CompilerParams(vmem_limit_bytes=VMEM_LIMIT)


def _rope_consts():
    lane = np.arange(HEAD_PAD)
    first = (lane >= ROPE_LO) & (lane < ROPE_LO + ROPE_HALF)
    second = (lane >= ROPE_LO + ROPE_HALF) & (lane < ROPE_LO + QK_ROPE)
    f = np.where(first, lane - ROPE_LO, np.where(second, lane - ROPE_LO - ROPE_HALF, 0))
    inv = np.float32(ROPE_THETA) ** (-(2 * f).astype(np.float32) / np.float32(QK_ROPE))
    out = np.zeros((8, HEAD_PAD), np.float32)
    out[0] = inv
    out[1] = first
    out[2] = second
    out[3] = lane < ROPE_LO
    return jnp.asarray(out)


def _gather_plan(w, sm, outs, osm, bf, sems):
    n = len(w)
    send_i, recv_i, send_d, recv_d, send_s, recv_s, local = sems
    x, y, c = _place()
    j0 = 2 * x + y
    sibling = (x, y, 1 - c)
    own = [pltpu.make_async_copy(bf[k], outs[k].at[j0], local.at[k]) for k in range(n)]
    if sm is not None:
        own.append(pltpu.make_async_copy(sm, osm.at[j0], local.at[n]))

    def half(k, cc):
        h = w[k].shape[0] // 2
        return pl.ds(pl.multiple_of(cc * h, 16), h)

    sends, arrivals, forwards, fwaits = [], [], [], []
    for i, (fx, fy) in enumerate(FLIPS):
        px, py = x ^ fx, y ^ fy
        pj = 2 * px + py
        for k in range(n):
            s = i * n + k
            sends.append(_remote(bf[k].at[half(k, c)], outs[k].at[j0, half(k, c)], send_i.at[s], recv_i.at[s],
                                 (px, py, c)))
            arrivals.append(_remote(bf[k].at[half(k, c)], outs[k].at[pj, half(k, c)], send_i.at[s], recv_i.at[s],
                                    (px, py, c)))
            forwards.append(_remote(outs[k].at[pj, half(k, c)], outs[k].at[pj, half(k, c)], send_d.at[s],
                                    recv_d.at[s], sibling))
            fwaits.append(_remote(outs[k].at[pj, half(k, 1 - c)], outs[k].at[pj, half(k, 1 - c)], send_d.at[s],
                                  recv_d.at[s], sibling))
        if sm is not None:
            sends.append(_remote(sm, osm.at[j0], send_s.at[i], recv_s.at[i], (px, py, c)))
            fwaits.append(_remote(sm, osm.at[pj], send_s.at[i], recv_s.at[i], (px, py, c)))
    return own, sends, arrivals, forwards, fwaits


def _gather_begin(w, bf, plan):
    own, sends, _, _, _ = plan
    for k in range(len(w)):
        bf[k][...] = w[k][...].astype(BF16)
    for cp in own + sends:
        cp.start()


def _gather_end(plan):
    own, sends, arrivals, forwards, fwaits = plan
    for cp, fwd in zip(arrivals, forwards):
        cp.wait_recv()
        fwd.start()
    for cp in fwaits:
        cp.wait_recv()
    for cp in sends + forwards:
        cp.wait_send()
    for cp in own:
        cp.wait()


def _gather_sems(n, with_small):
    return ([pltpu.SemaphoreType.DMA((3 * n,))] * 4 + [pltpu.SemaphoreType.DMA((3,))] * 2
            + [pltpu.SemaphoreType.DMA((n + (1 if with_small else 0),))])


def _rope_tables_gather(pos_col, w_in, small):
    t = pos_col.shape[0]
    tm = min(ROW_TILE, t)
    steps = t // tm

    def body(p_ref, k_ref, w_ref, sm_ref, c_ref, s1_ref, s2_ref, out_ref, osm_ref, bf_ref, *sems):
        plan = _gather_plan([w_ref], sm_ref, [out_ref], osm_ref, [bf_ref], sems)
        i = pl.program_id(0)

        @pl.when(i == 0)
        def _():
            _gather_begin([w_ref], [bf_ref], plan)

        inv, first, second, nope = k_ref[0:1, :], k_ref[1:2, :], k_ref[2:3, :], k_ref[3:4, :]
        ang = p_ref[...].astype(F32) * inv
        cs, sn = jnp.cos(ang), jnp.sin(ang)
        c_ref[...] = cs * (first + second) + nope
        s1_ref[...] = -sn * first
        s2_ref[...] = sn * second

        @pl.when(i == steps - 1)
        def _():
            _gather_end(plan)

    out = jax.ShapeDtypeStruct((t, HEAD_PAD), F32)
    return pl.pallas_call(
        body, grid=(steps,), name="rope_tables_gather",
        in_specs=[_row_spec(tm, 1), _const_spec((8, HEAD_PAD)), WHOLE, WHOLE],
        out_specs=[_row_spec(tm, HEAD_PAD)] * 3 + [HBM, HBM],
        out_shape=[out] * 3 + [jax.ShapeDtypeStruct((N_CHIPS,) + w_in.shape, BF16),
                               jax.ShapeDtypeStruct((N_CHIPS,) + small.shape, small.dtype)],
        scratch_shapes=[pltpu.VMEM(w_in.shape, BF16)] + _gather_sems(1, True),
        compiler_params=_params("arbitrary"),
    )(pos_col, _rope_consts(), w_in, small)


def _shift_down(v, prev, row):
    p1, p2 = prev[7:8, :], prev[6:7, :]
    v1 = jnp.where(row == 0, p1, pltpu.roll(v, 1, 0))
    v2 = jnp.where(row == 0, p2, jnp.where(row == 1, p1, pltpu.roll(v, 2, 0)))
    return v1, v2


def _conv_fwd(x, seq, ga, w_in4, cw, shards):
    t = x.shape[0]
    tm = min(ROW_TILE, seq)
    tiles_per_seq = seq // tm
    steps = t // tm
    n = len(shards)

    def body(x_ref, ga_ref, w_ref, cw_ref, *rest):
        sh, (b_ref, c_ref, u_ref, g_ref, ym_ref) = rest[:n], rest[n:n + 5]
        outs, carry_ref, bf, sems = rest[n + 5:2 * n + 5], rest[2 * n + 5], rest[2 * n + 6:3 * n + 6], rest[3 * n + 6:]
        plan = _gather_plan(sh, None, outs, None, bf, sems)
        i = pl.program_id(0)

        @pl.when(i == 0)
        def _():
            _gather_begin(sh, bf, plan)

        @pl.when(i % tiles_per_seq == 0)
        def _():
            carry_ref[...] = jnp.zeros_like(carry_ref)

        _, xh = _rms(x_ref[...])
        h = (xh * ga_ref[...]).astype(BF16)
        b, c, u, g = (_dot(h, w_ref[j]) for j in range(4))
        v = c * u
        row = lax.broadcasted_iota(jnp.int32, (tm, 1), 0)
        v1, v2 = _shift_down(v, carry_ref[...], row)
        carry_ref[...] = v[tm - 8:tm, :]
        cv = cw_ref[2:3, :] * v + cw_ref[1:2, :] * v1 + cw_ref[0:1, :] * v2
        b_ref[...] = b.astype(BF16)
        c_ref[...] = c.astype(BF16)
        u_ref[...] = u.astype(BF16)
        g_ref[...] = g.astype(BF16)
        ym_ref[...] = (g * _sigmoid(g) * b * cv).astype(BF16)

        @pl.when(i == steps - 1)
        def _():
            _gather_end(plan)

    out = jax.ShapeDtypeStruct((t, E_A), BF16)
    res = pl.pallas_call(
        body, grid=(steps,), name="conv_fwd",
        in_specs=[_row_spec(tm, D_MODEL), _const_spec((1, D_MODEL)), _const_spec((4, D_MODEL, E_A)),
                  _const_spec((8, E_A))] + [WHOLE] * n,
        out_specs=[_row_spec(tm, E_A)] * 5 + [HBM] * n,
        out_shape=[out] * 5 + [jax.ShapeDtypeStruct((N_CHIPS,) + a.shape, BF16) for a in shards],
        scratch_shapes=[pltpu.VMEM((8, E_A), F32)] + [pltpu.VMEM(a.shape, BF16) for a in shards]
        + _gather_sems(n, False),
        compiler_params=_params("arbitrary"),
    )(x, ga, w_in4, cw, *shards)
    return res[:5], res[5:]


def _mid_fwd(x, ym, w_out, gk, gb, w_dkv, gc, w_uk, w_uv, w_bin, gq, w_uq, rc, rs1, rs2):
    t = x.shape[0]
    tm = min(ROW_TILE, t)

    def body(x_ref, ym_ref, wo_ref, gk_ref, gb_ref, wd_ref, gc_ref, wk_ref, wv_ref, wi_ref, gq_ref, wq_ref,
             c_ref, s1_ref, s2_ref, x1_ref, q_ref, k_ref, v_ref, gate_ref, cq_ref, ckv_ref):
        cb, s1b, s2b = c_ref[...], s1_ref[...], s2_ref[...]
        x1 = x_ref[...] + _dot(ym_ref[...], wo_ref[...])
        x1_ref[...] = x1
        _, xh = _rms(x1)
        hk = (xh * gk_ref[...]).astype(BF16)
        h1 = (xh * gb_ref[...]).astype(BF16)

        pb = _dot(h1, wi_ref[...])
        cq = pb[:, :Q_RANK]
        cq_ref[...] = cq.astype(BF16)
        gate_ref[...] = pb[:, Q_RANK:].astype(BF16)
        _, cqh = _rms(cq)
        q = _dot((cqh * gq_ref[...]).astype(BF16), wq_ref[...])
        for h in range(N_HEADS):
            sl = slice(h * HEAD_PAD, (h + 1) * HEAD_PAD)
            q_ref[:, sl] = (_rope_fwd(q[:, sl], cb, s1b, s2b) * Q_PRESCALE).astype(BF16)

        ckr = _dot(hk, wd_ref[...])
        ckv_raw = ckr[:, :KV_RANK]
        ckv_ref[...] = ckv_raw.astype(BF16)
        _, ch = _rms(ckv_raw)
        ckv = (ch * gc_ref[...]).astype(BF16)
        kr = _rope_fwd(ckr[:, KV_RANK:], cb, s1b, s2b)
        kn = _dot(ckv, wk_ref[...])
        for h in range(N_HEADS):
            sl = slice(h * HEAD_PAD, (h + 1) * HEAD_PAD)
            k_ref[:, sl] = (kn[:, sl] + kr).astype(BF16)
        v_ref[...] = _dot(ckv, wv_ref[...]).astype(BF16)

    def sds(n, dt):
        return jax.ShapeDtypeStruct((t, n), dt)

    return pl.pallas_call(
        body, grid=(t // tm,), name="mid_fwd",
        in_specs=[_row_spec(tm, D_MODEL), _row_spec(tm, E_A), _const_spec((E_A, D_MODEL)),
                  _const_spec((1, D_MODEL)), _const_spec((1, D_MODEL)), _const_spec((D_MODEL, KR_PAD)),
                  _const_spec((1, KV_RANK)), _const_spec((KV_RANK, QK_PAD)), _const_spec((KV_RANK, E_B)),
                  _const_spec((D_MODEL, Q_RANK + E_B)), _const_spec((1, Q_RANK)), _const_spec((Q_RANK, QK_PAD)),
                  _row_spec(tm, HEAD_PAD), _row_spec(tm, HEAD_PAD), _row_spec(tm, HEAD_PAD)],
        out_specs=[_row_spec(tm, D_MODEL), _row_spec(tm, QK_PAD), _row_spec(tm, QK_PAD), _row_spec(tm, E_B),
                   _row_spec(tm, E_B), _row_spec(tm, Q_RANK), _row_spec(tm, KV_RANK)],
        out_shape=[sds(D_MODEL, F32), sds(QK_PAD, BF16), sds(QK_PAD, BF16), sds(E_B, BF16), sds(E_B, BF16),
                   sds(Q_RANK, BF16), sds(KV_RANK, BF16)],
        compiler_params=_params("parallel"),
    )(x, ym, w_out, gk, gb, w_dkv, gc, w_uk, w_uv, w_bin, gq, w_uq, rc, rs1, rs2)


def _pair_specs(seq):
    qk = pl.BlockSpec((seq, 2 * HEAD_PAD), lambda b, p: (b, p))
    vo = pl.BlockSpec((seq, 2 * V_HEAD), lambda b, p: (b, p))
    st = pl.BlockSpec((None, 2, seq), lambda b, p: (p, 0, b))
    return qk, vo, st


def _attn_fwd(q, k, v, seq):
    t = q.shape[0]
    tq = min(ATT_TILE_FWD, seq)
    nq = seq // tq

    def body(q_ref, k_ref, v_ref, o_ref, lse_ref, m_scr, l_scr, acc_scr):
        lane = lax.broadcasted_iota(jnp.int32, (tq, 2 * V_HEAD), 1)

        def q_step(qi, _):
            q0 = pl.multiple_of(qi * tq, tq)
            m_scr[...] = jnp.full(m_scr.shape, -jnp.inf, F32)
            l_scr[...] = jnp.zeros_like(l_scr)
            acc_scr[...] = jnp.zeros_like(acc_scr)

            def block(q_lo, q_n, k0, k_n, masked):
                rows = slice(q_lo, q_lo + q_n)
                vt = v_ref[pl.ds(k0, k_n), :]

                def score(hh):
                    hs = slice(hh * HEAD_PAD, (hh + 1) * HEAD_PAD)
                    return _dot_nt(q_ref[pl.ds(q0 + q_lo, q_n), hs], k_ref[pl.ds(k0, k_n), hs])

                early = [score(hh) for hh in range(2)] if masked else None
                for hh in range(2):
                    s = early[hh] if masked else score(hh)
                    if masked:
                        row = lax.broadcasted_iota(jnp.int32, (q_n, k_n), 0)
                        col = lax.broadcasted_iota(jnp.int32, (q_n, k_n), 1)
                        s = jnp.where(col <= row, s, -jnp.inf)
                    m_old = m_scr[hh, rows]
                    m_new = jnp.maximum(m_old, jnp.max(s, axis=-1, keepdims=True))
                    alpha = jnp.exp2(m_old - m_new)
                    ps = [jnp.exp2(s[:, j * LANES:(j + 1) * LANES] - m_new) for j in range(k_n // LANES)]
                    l_scr[hh, rows] = alpha * l_scr[hh, rows] + functools.reduce(lambda a, b: a + b, ps)
                    p = jnp.concatenate(ps, axis=-1).astype(BF16)
                    acc_scr[hh, rows] = alpha * acc_scr[hh, rows] + _dot(p, vt)
                    m_scr[hh, rows] = m_new

            def k_step(ki, _):
                block(0, tq, pl.multiple_of(ki * tq, tq), tq, False)
                return 0

            lax.fori_loop(0, qi, k_step, 0)
            half = tq // 2
            block(0, tq, q0, half, True)
            block(half, half, q0 + half, half, True)
            l0 = jnp.sum(l_scr[0], axis=-1, keepdims=True)
            l1 = jnp.sum(l_scr[1], axis=-1, keepdims=True)
            o_ref[pl.ds(q0, tq), :] = jnp.where(lane < V_HEAD, acc_scr[0] / l0, acc_scr[1] / l1).astype(BF16)
            stats = jnp.where(lane == 0, m_scr[0] + jnp.log2(l0), m_scr[1] + jnp.log2(l1)).T
            lse_ref[:, pl.ds(q0, tq)] = stats[0:2, :]
            return 0

        lax.fori_loop(0, nq, q_step, 0)

    qk, vo, st = _pair_specs(seq)
    return pl.pallas_call(
        body, grid=(t // seq, N_HEADS // 2), name="attn_fwd",
        in_specs=[qk, qk, vo], out_specs=[vo, st],
        out_shape=[jax.ShapeDtypeStruct((t, E_B), BF16), jax.ShapeDtypeStruct((N_HEADS // 2, 2, t), F32)],
        scratch_shapes=[pltpu.VMEM((2, tq, LANES), F32), pltpu.VMEM((2, tq, LANES), F32),
                        pltpu.VMEM((2, tq, 2 * V_HEAD), F32)],
        compiler_params=_params("parallel", "parallel"),
    )(q, k, v)


def _head_fwd_bwd(o, gate, x1, tgt, w_bout, gf):
    t = o.shape[0]
    tm = min(ROW_TILE, t)

    def body(o_ref, gate_ref, x1_ref, tgt_ref, w_ref, gf_ref,
             dx2_ref, do_ref, dgate_ref, dd_ref, loss_ref, dgf_ref, dw_ref):
        @pl.when(pl.program_id(0) == 0)
        def _():
            loss_ref[...] = jnp.zeros_like(loss_ref)
            dgf_ref[...] = jnp.zeros_like(dgf_ref)
            dw_ref[...] = jnp.zeros_like(dw_ref)

        hm = tm // HEAD_CHAINS
        gf = gf_ref[...]
        lane = lax.broadcasted_iota(jnp.int32, (hm, 2 * V_HEAD), 1)
        chains = []
        for ch in range(HEAD_CHAINS):
            rs = pl.ds(ch * hm, hm)
            o = o_ref[rs, :].astype(F32)
            gt = gate_ref[rs, :].astype(F32)
            sg = _sigmoid(gt)
            silu = gt * sg
            z = (o * silu).astype(BF16)
            chains.append((rs, o, gt, sg, silu, z, x1_ref[rs, :] + _dot(z, w_ref[...])))
        mids = []
        for rs, o, gt, sg, silu, z, x2 in chains:
            r2, xh2 = _rms(x2)
            err = xh2 * gf - tgt_ref[rs, :]
            loss_ref[...] += 0.5 * jnp.sum(jnp.mean(err * err, axis=-1, keepdims=True))
            dy = err * (1.0 / D_MODEL)
            dgf_ref[...] += jnp.sum(dy * xh2, axis=0, keepdims=True)
            dx2 = _rms_bwd(dy * gf, xh2, r2)
            dx2_ref[rs, :] = dx2
            dx2b = dx2.astype(BF16)
            mids.append(_dot_nt(dx2b, w_ref[...]))
            dw_ref[...] += _dot_tn(z, dx2b)
        for (rs, o, gt, sg, silu, z, x2), dz in zip(chains, mids):
            do = dz * silu
            do_ref[rs, :] = do.astype(BF16)
            dgate_ref[rs, :] = (dz * o * (sg * (1.0 + gt * (1.0 - sg)))).astype(BF16)
            prod = do * o
            cols = jnp.zeros((hm, LANES), F32)
            for p in range(N_HEADS // 2):
                blk = prod[:, p * 2 * V_HEAD:(p + 1) * 2 * V_HEAD]
                d0 = jnp.sum(jnp.where(lane < V_HEAD, blk, 0.0), axis=-1, keepdims=True)
                d1 = jnp.sum(jnp.where(lane < V_HEAD, 0.0, blk), axis=-1, keepdims=True)
                cols = jnp.where(lane == 2 * p, d0, jnp.where(lane == 2 * p + 1, d1, cols))
            rows = cols.T
            for h in range(N_HEADS):
                dd_ref[h // 2, h % 2:h % 2 + 1, rs] = rows[h:h + 1, :]

    return pl.pallas_call(
        body, grid=(t // tm,), name="head_fwd_bwd",
        in_specs=[_row_spec(tm, E_B), _row_spec(tm, E_B), _row_spec(tm, D_MODEL), _row_spec(tm, D_MODEL),
                  _const_spec((E_B, D_MODEL)), _const_spec((1, D_MODEL))],
        out_specs=[_row_spec(tm, D_MODEL), _row_spec(tm, E_B), _row_spec(tm, E_B),
                   pl.BlockSpec((N_HEADS // 2, 2, tm), lambda i: (0, 0, i)),
                   _acc_spec((1, 1)), _acc_spec((1, D_MODEL)), _acc_spec((E_B, D_MODEL))],
        out_shape=[jax.ShapeDtypeStruct((t, D_MODEL), F32), jax.ShapeDtypeStruct((t, E_B), BF16),
                   jax.ShapeDtypeStruct((t, E_B), BF16), jax.ShapeDtypeStruct((N_HEADS // 2, 2, t), F32),
                   jax.ShapeDtypeStruct((1, 1), F32), jax.ShapeDtypeStruct((1, D_MODEL), F32),
                   jax.ShapeDtypeStruct((E_B, D_MODEL), F32)],
        compiler_params=_params("arbitrary"),
    )(o, gate, x1, tgt, w_bout, gf)


def _attn_bwd(q, k, v, do, lse, dd, seq):
    t = q.shape[0]
    tq = min(ATT_TILE, seq)
    nq = seq // tq

    def body(q_ref, k_ref, v_ref, do_ref, lse_ref, dd_ref, dq_ref, dk_ref, dv_ref, dq_acc, dk_acc, dv_acc):
        dq_acc[...] = jnp.zeros_like(dq_acc)

        def k_step(ki, _):
            k0 = pl.multiple_of(ki * tq, tq)
            dk_acc[...] = jnp.zeros_like(dk_acc)
            dv_acc[...] = jnp.zeros_like(dv_acc)

            def block(k_lo, k_n, q0, q_n, masked):
                rows = slice(k_lo, k_lo + k_n)
                lane = lax.broadcasted_iota(jnp.int32, (q_n, 2 * V_HEAD), 1)
                vt = v_ref[pl.ds(k0 + k_lo, k_n), :]
                do_pair = do_ref[pl.ds(q0, q_n), :]

                def operands(hh):
                    hs = slice(hh * HEAD_PAD, (hh + 1) * HEAD_PAD)
                    kt = k_ref[pl.ds(k0 + k_lo, k_n), hs]
                    qt = q_ref[pl.ds(q0, q_n), hs]
                    mine = (lane < V_HEAD) if hh == 0 else (lane >= V_HEAD)
                    do_h = jnp.where(mine, do_pair, jnp.zeros((), BF16))
                    return hs, kt, qt, do_h, _dot_nt(kt, qt), _dot_nt(vt, do_h)

                early = [operands(hh) for hh in range(2)] if masked else None
                for hh in range(2):
                    hs, kt, qt, do_h, st, dpt = early[hh] if masked else operands(hh)
                    if masked:
                        krow = lax.broadcasted_iota(jnp.int32, (k_n, q_n), 0)
                        qcol = lax.broadcasted_iota(jnp.int32, (k_n, q_n), 1)
                        st = jnp.where(krow <= qcol, st, -jnp.inf)
                    pt = jnp.exp2(st - lse_ref[hh:hh + 1, pl.ds(q0, q_n)])
                    dst = (pt * (dpt - dd_ref[hh:hh + 1, pl.ds(q0, q_n)])).astype(BF16)
                    dv_acc[rows, :] += _dot(pt.astype(BF16), do_h)
                    dk_acc[rows, hs] += _dot(dst, qt)
                    dq_acc[pl.ds(q0, q_n), hs] += _dot_tn(dst, kt)

            def q_step(qi, _):
                block(0, tq, pl.multiple_of(qi * tq, tq), tq, False)
                return 0

            half = tq // 2
            block(0, half, k0, tq, True)
            block(half, half, pl.multiple_of(k0 + half, half), half, True)
            lax.fori_loop(ki + 1, nq, q_step, 0)
            dk_ref[pl.ds(k0, tq), :] = (dk_acc[...] * LN2).astype(BF16)
            dv_ref[pl.ds(k0, tq), :] = dv_acc[...].astype(BF16)
            return 0

        lax.fori_loop(0, nq, k_step, 0)
        dq_ref[...] = (dq_acc[...] * SOFTMAX_SCALE).astype(BF16)

    qk, vo, st = _pair_specs(seq)
    return pl.pallas_call(
        body, grid=(t // seq, N_HEADS // 2), name="attn_bwd",
        in_specs=[qk, qk, vo, vo, st, st], out_specs=[qk, qk, vo],
        out_shape=[jax.ShapeDtypeStruct((t, QK_PAD), BF16), jax.ShapeDtypeStruct((t, QK_PAD), BF16),
                   jax.ShapeDtypeStruct((t, E_B), BF16)],
        scratch_shapes=[pltpu.VMEM((seq, 2 * HEAD_PAD), F32), pltpu.VMEM((tq, 2 * HEAD_PAD), F32),
                        pltpu.VMEM((tq, 2 * V_HEAD), F32)],
        compiler_params=_params("parallel", "parallel"),
    )(q, k, v, do, lse, dd)


def _mid_bwd(dq, dk, dv, dgate, dx2, x1, cq, ckv, rc, rs1, rs2, w_uq, w_bin, w_uk, w_uv, w_dkv, gq, gc, gk, gb):
    t = dq.shape[0]
    tm = min(ROW_TILE, t)

    def body(dq_ref, dk_ref, dv_ref, dgate_ref, dx2_ref, x1_ref, cq_ref, ckv_ref, c_ref, s1_ref, s2_ref,
             wq_ref, wi_ref, wk_ref, wv_ref, wd_ref, gq_ref, gc_ref, gk_ref, gb_ref,
             dx1_ref, dwq_ref, dwi_ref, dwk_ref, dwv_ref, dwd_ref, dgq_ref, dgc_ref, dgk_ref, dgb_ref):
        @pl.when(pl.program_id(0) == 0)
        def _():
            for ref in (dwq_ref, dwi_ref, dwk_ref, dwv_ref, dwd_ref, dgq_ref, dgc_ref, dgk_ref, dgb_ref):
                ref[...] = jnp.zeros_like(ref)

        cb, s1b, s2b = c_ref[...], s1_ref[...], s2_ref[...]
        gk, gb, gq, gc = gk_ref[...], gb_ref[...], gq_ref[...], gc_ref[...]

        dkb, dvb = dk_ref[...], dv_ref[...]
        dckv = _dot_nt(dkb, wk_ref[...]) + _dot_nt(dvb, wv_ref[...])
        rcv, ch = _rms(ckv_ref[...].astype(F32))
        ckvn = (ch * gc).astype(BF16)
        dwk_ref[...] += _dot_tn(ckvn, dkb)
        dwv_ref[...] += _dot_tn(ckvn, dvb)

        dqs = [_rope_bwd(dq_ref[:, h * HEAD_PAD:(h + 1) * HEAD_PAD].astype(F32), cb, s1b, s2b)
               for h in range(N_HEADS)]
        dqb = jnp.concatenate(dqs, axis=-1).astype(BF16)
        rq, cqh = _rms(cq_ref[...].astype(F32))
        dcqn = _dot_nt(dqb, wq_ref[...])
        dwq_ref[...] += _dot_tn((cqh * gq).astype(BF16), dqb)

        dgc_ref[...] += jnp.sum(dckv * ch, axis=0, keepdims=True)
        dckv_raw = _rms_bwd(dckv * gc, ch, rcv)
        dkr = dk_ref[:, 0:HEAD_PAD].astype(F32)
        for h in range(1, N_HEADS):
            dkr = dkr + dk_ref[:, h * HEAD_PAD:(h + 1) * HEAD_PAD].astype(F32)
        dkr = _rope_bwd(dkr, cb, s1b, s2b)
        dckr = jnp.concatenate([dckv_raw, dkr], axis=-1).astype(BF16)
        r1, xh = _rms(x1_ref[...])
        dhk = _dot_nt(dckr, wd_ref[...])
        dwd_ref[...] += _dot_tn((xh * gk).astype(BF16), dckr)

        dgq_ref[...] += jnp.sum(dcqn * cqh, axis=0, keepdims=True)
        dcq = _rms_bwd(dcqn * gq, cqh, rq)
        dpb = jnp.concatenate([dcq.astype(BF16), dgate_ref[...]], axis=-1)
        dh1 = _dot_nt(dpb, wi_ref[...])
        dwi_ref[...] += _dot_tn((xh * gb).astype(BF16), dpb)

        dgb_ref[...] += jnp.sum(dh1 * xh, axis=0, keepdims=True)
        dgk_ref[...] += jnp.sum(dhk * xh, axis=0, keepdims=True)
        dx1_ref[...] = dx2_ref[...] + _rms_bwd(dh1 * gb + dhk * gk, xh, r1)

    acc_shapes = [(Q_RANK, QK_PAD), (D_MODEL, Q_RANK + E_B), (KV_RANK, QK_PAD), (KV_RANK, E_B), (D_MODEL, KR_PAD),
                  (1, Q_RANK), (1, KV_RANK), (1, D_MODEL), (1, D_MODEL)]
    return pl.pallas_call(
        body, grid=(t // tm,), name="mid_bwd",
        in_specs=[_row_spec(tm, QK_PAD), _row_spec(tm, QK_PAD), _row_spec(tm, E_B), _row_spec(tm, E_B),
                  _row_spec(tm, D_MODEL), _row_spec(tm, D_MODEL), _row_spec(tm, Q_RANK), _row_spec(tm, KV_RANK),
                  _row_spec(tm, HEAD_PAD), _row_spec(tm, HEAD_PAD), _row_spec(tm, HEAD_PAD),
                  _const_spec((Q_RANK, QK_PAD)), _const_spec((D_MODEL, Q_RANK + E_B)),
                  _const_spec((KV_RANK, QK_PAD)), _const_spec((KV_RANK, E_B)), _const_spec((D_MODEL, KR_PAD)),
                  _const_spec((1, Q_RANK)), _const_spec((1, KV_RANK)), _const_spec((1, D_MODEL)),
                  _const_spec((1, D_MODEL))],
        out_specs=[_row_spec(tm, D_MODEL)] + [_acc_spec(s) for s in acc_shapes],
        out_shape=[jax.ShapeDtypeStruct((t, D_MODEL), F32)] + [jax.ShapeDtypeStruct(s, F32) for s in acc_shapes],
        compiler_params=_params("arbitrary"),
    )(dq, dk, dv, dgate, dx2, x1, cq, ckv, rc, rs1, rs2, w_uq, w_bin, w_uk, w_uv, w_dkv, gq, gc, gk, gb)


def _conv_bwd(dx1, x, b, c, u, g, seq, w_out, w_in4, ga, cw):
    t = x.shape[0]
    tm = min(CONV_BWD_TILE, seq)
    tiles_per_seq = seq // tm
    n = t // tm
    halo = tm // 8

    def tile(i):
        return n - 1 - i

    def rev(width):
        return pl.BlockSpec((tm, width), lambda i: (tile(i), 0))

    def prev8(width):
        return pl.BlockSpec((8, width), lambda i: (jnp.maximum(tile(i) * halo - 1, 0), 0))

    def body(dx1_ref, x_ref, b_ref, c_ref, u_ref, g_ref, cp_ref, up_ref, wo_ref, wi_ref, ga_ref, cw_ref,
             dx_ref, dwi_ref, dwo_ref, dcw_ref, dga_ref, carry_ref):
        i = pl.program_id(0)
        j = tile(i)

        @pl.when(i == 0)
        def _():
            for ref in (dwi_ref, dwo_ref, dcw_ref, dga_ref):
                ref[...] = jnp.zeros_like(ref)

        @pl.when(j % tiles_per_seq == tiles_per_seq - 1)
        def _():
            carry_ref[...] = jnp.zeros_like(carry_ref)

        dx1 = dx1_ref[...]
        dx1b = dx1.astype(BF16)
        b, c, u, g = (r[...].astype(F32) for r in (b_ref, c_ref, u_ref, g_ref))
        v = c * u
        first = (j % tiles_per_seq == 0).astype(F32)
        vprev = cp_ref[...].astype(F32) * up_ref[...].astype(F32) * (1.0 - first)
        row = lax.broadcasted_iota(jnp.int32, (tm, 1), 0)
        v1, v2 = _shift_down(v, vprev, row)
        w0, w1, w2 = cw_ref[0:1, :], cw_ref[1:2, :], cw_ref[2:3, :]
        cv = w2 * v + w1 * v1 + w0 * v2
        sg = _sigmoid(g)
        silu = g * sg
        ym = (silu * b * cv).astype(BF16)
        dym = _dot_nt(dx1b, wo_ref[...])
        dwo_ref[...] += _dot_tn(ym, dx1b)
        db = dym * silu * cv
        dcv = dym * silu * b
        dg = dym * b * cv * (sg * (1.0 + g * (1.0 - sg)))

        nxt = carry_ref[...]
        n0, n1 = nxt[0:1, :], nxt[1:2, :]
        d1 = jnp.where(row == tm - 1, n0, pltpu.roll(dcv, tm - 1, 0))
        d2 = jnp.where(row == tm - 1, n1, jnp.where(row == tm - 2, n0, pltpu.roll(dcv, tm - 2, 0)))
        carry_ref[...] = dcv[0:8, :]
        dv = w2 * dcv + w1 * d1 + w0 * d2
        dcw_ref[0:1, :] += jnp.sum(dcv * v2, axis=0, keepdims=True)
        dcw_ref[1:2, :] += jnp.sum(dcv * v1, axis=0, keepdims=True)
        dcw_ref[2:3, :] += jnp.sum(dcv * v, axis=0, keepdims=True)

        r0, xh = _rms(x_ref[...])
        ga = ga_ref[...]
        h = (xh * ga).astype(BF16)
        dh = jnp.zeros((tm, D_MODEL), F32)
        for idx, dpart in enumerate((db, dv * u, dv * c, dg)):
            dpb = dpart.astype(BF16)
            dh = dh + _dot_nt(dpb, wi_ref[idx])
            dwi_ref[idx] += _dot_tn(h, dpb)
        dga_ref[...] += jnp.sum(dh * xh, axis=0, keepdims=True)
        dx_ref[...] = dx1 + _rms_bwd(dh * ga, xh, r0)

    acc_shapes = [(4, D_MODEL, E_A), (E_A, D_MODEL), (8, E_A), (1, D_MODEL)]
    return pl.pallas_call(
        body, grid=(n,), name="conv_bwd",
        in_specs=[rev(D_MODEL), rev(D_MODEL), rev(E_A), rev(E_A), rev(E_A), rev(E_A), prev8(E_A), prev8(E_A),
                  _const_spec((E_A, D_MODEL)), _const_spec((4, D_MODEL, E_A)), _const_spec((1, D_MODEL)),
                  _const_spec((8, E_A))],
        out_specs=[rev(D_MODEL)] + [_acc_spec(s) for s in acc_shapes],
        out_shape=[jax.ShapeDtypeStruct((t, D_MODEL), F32)] + [jax.ShapeDtypeStruct(s, F32) for s in acc_shapes],
        scratch_shapes=[pltpu.VMEM((8, E_A), F32)],
        compiler_params=_params("arbitrary"),
    )(dx1, x, b, c, u, g, c, u, w_out, w_in4, ga, cw)


WEIGHTS = ("a_norm", "a_w_in", "a_conv", "a_w_out", "kv_norm", "w_dkv", "ckv_norm", "w_ukv", "b_norm", "b_w_in",
           "b_q_norm", "b_w_uq", "b_w_out", "final_norm")
SHARD_SHAPES = {
    "a_norm": (1, 256), "a_w_in": (1, 1024, 1024), "a_conv": (1, 3, 256), "a_w_out": (1, 256, 1024),
    "kv_norm": (1024,), "w_dkv": (256, 288), "ckv_norm": (256,), "w_ukv": (256, 256), "b_norm": (1, 1024),
    "b_w_in": (1, 256, 896), "b_q_norm": (1, 384), "b_w_uq": (1, 384, 192), "b_w_out": (1, 512, 256),
    "final_norm": (1024,),
}
MATS = ("a_w_in", "a_w_out", "w_dkv", "w_ukv", "b_w_in", "b_w_uq", "b_w_out")
SMALL = ("a_norm", "a_conv", "kv_norm", "ckv_norm", "b_norm", "b_q_norm", "final_norm")
SMALL_FULL = {"a_norm": 1024, "a_conv": 3072, "kv_norm": 1024, "ckv_norm": 256, "b_norm": 1024, "b_q_norm": 384,
              "final_norm": 1024}
SMALL_ROWS = 8
LOSS_SLOT = sum(SMALL_FULL.values())


def _mat2d(name, a):
    return a.reshape(SHARD_SHAPES[name][-2:])


def _prep_first(g_win, gsmall):
    sm = gsmall.reshape(N_CHIPS, -1)
    a_conv = jnp.transpose(sm[:, 256:1024].reshape(N_CHIPS, CONV_WIDTH, 256), (1, 0, 2)).reshape(CONV_WIDTH, -1)
    return {"w_in4": g_win, "ga": sm[:, :256].reshape(1, -1), "cw": jnp.pad(a_conv, ((0, 8 - CONV_WIDTH), (0, 0)))}


def _prep_rest(gath, w):
    def cols(a):
        return jnp.transpose(a, (1, 0, 2)).reshape(a.shape[1], -1)

    def pad_heads(a, width):
        a = a.reshape(a.shape[0], N_HEADS, width)
        return jnp.pad(a, ((0, 0), (0, 0), (0, HEAD_PAD - width))).reshape(a.shape[0], QK_PAD)

    row = lambda a: a.reshape(1, -1).astype(F32)
    w_dkv = gath["w_dkv"].reshape(D_MODEL, KV_RANK + QK_ROPE)
    w_ukv = cols(gath["w_ukv"]).reshape(KV_RANK, N_HEADS, 2, QK_NOPE)
    return {
        "w_out": gath["a_w_out"].reshape(E_A, D_MODEL),
        "w_dkv": jnp.concatenate([w_dkv[:, :KV_RANK], jnp.zeros((D_MODEL, ROPE_LO), BF16), w_dkv[:, KV_RANK:],
                                  jnp.zeros((D_MODEL, HEAD_PAD - ROPE_LO - QK_ROPE), BF16)], axis=1),
        "w_uk": pad_heads(w_ukv[:, :, 0, :].reshape(KV_RANK, N_HEADS * QK_NOPE), QK_NOPE),
        "w_uv": w_ukv[:, :, 1, :].reshape(KV_RANK, E_B),
        "w_bin": gath["b_w_in"].reshape(D_MODEL, Q_RANK + E_B),
        "w_uq": pad_heads(cols(gath["b_w_uq"]), QK_NOPE + QK_ROPE),
        "w_bout": cols(gath["b_w_out"]),
        "gk": row(w["kv_norm"]), "gc": row(w["ckv_norm"]), "gb": row(w["b_norm"]),
        "gq": row(w["b_q_norm"]), "gf": row(w["final_norm"]),
    }


TRANSPOSED = ("w_dkv", "b_w_uq")
GRAD_MATS = ("w_in4", "w_out", "w_dkv", "w_uk", "w_uv", "w_bin", "w_uq", "w_bout")
GRAD_KIND = {"w_in4": "lead", "w_out": "row", "w_dkv": "row", "w_uk": "col", "w_uv": "col", "w_bin": "row",
             "w_uq": "col", "w_bout": "col"}


def _local_step(x, positions, tgt, w):
    bsz, seq, _ = x.shape
    t = bsz * seq
    x2d = x.reshape(t, D_MODEL)
    small = jnp.concatenate([w["a_norm"].reshape(-1), w["a_conv"].reshape(-1)]).reshape(8, LANES)
    rc, rs1, rs2, g_win, gsmall = _rope_tables_gather(positions.reshape(t, 1), _mat2d(MATS[0], w[MATS[0]]), small)
    wk = _prep_first(g_win, gsmall)
    (b, c, u, g, ym), gathered = _conv_fwd(x2d, seq, wk["ga"], wk["w_in4"], wk["cw"],
                                           [_mat2d(n, w[n]) for n in MATS[1:]])
    wk.update(_prep_rest(dict(zip(MATS[1:], gathered)), w))
    x1, q, k, v, gate, cq, ckv = _mid_fwd(x2d, ym, wk["w_out"], wk["gk"], wk["gb"], wk["w_dkv"], wk["gc"], wk["w_uk"],
                                          wk["w_uv"], wk["w_bin"], wk["gq"], wk["w_uq"], rc, rs1, rs2)
    o, lse = _attn_fwd(q, k, v, seq)
    dx2, do, dgate, dd, loss, dgf, dw_bout = _head_fwd_bwd(o, gate, x1, tgt.reshape(t, D_MODEL), wk["w_bout"], wk["gf"])
    dq, dk, dv = _attn_bwd(q, k, v, do, lse, dd, seq)
    dx1, dwq, dw_bin, dwk, dwv, dwd, dgq, dgc, dgk, dgb = _mid_bwd(
        dq, dk, dv, dgate, dx2, x1, cq, ckv, rc, rs1, rs2, wk["w_uq"], wk["w_bin"], wk["w_uk"], wk["w_uv"], wk["w_dkv"],
        wk["gq"], wk["gc"], wk["gk"], wk["gb"])
    dx, dw_in4, dw_out, dcw, dga = _conv_bwd(dx1, x2d, b, c, u, g, seq, wk["w_out"], wk["w_in4"], wk["ga"], wk["cw"])
    mats = {"w_in4": dw_in4, "w_out": dw_out, "w_dkv": dwd, "w_uk": dwk, "w_uv": dwv, "w_bin": dw_bin, "w_uq": dwq,
            "w_bout": dw_bout}
    small = {"a_norm": dga, "a_conv": dcw[:CONV_WIDTH], "kv_norm": dgk, "ckv_norm": dgc, "b_norm": dgb,
             "b_q_norm": dgq, "final_norm": dgf}
    return loss[0, 0], dx.reshape(bsz, seq, D_MODEL), mats, small


def _shard_grads(sh, svec):
    j0 = 2 * lax.axis_index("x") + lax.axis_index("y")
    flat = svec.reshape(-1)
    off, small = 0, {}
    for n in SMALL:
        small[n] = flat[off:off + SMALL_FULL[n]]
        off += SMALL_FULL[n]
    dwd, dwk, dwv, dwq = sh["w_dkv"], sh["w_uk"], sh["w_uv"], sh["w_uq"]
    w_ukv = jnp.stack([dwk.reshape(KV_RANK, 2, HEAD_PAD)[:, :, :QK_NOPE], dwv.reshape(KV_RANK, 2, V_HEAD)], axis=2)
    return {
        "a_norm": lax.dynamic_slice(small["a_norm"], (j0 * 256,), (256,)),
        "a_conv": lax.dynamic_slice(small["a_conv"].reshape(CONV_WIDTH, E_A), (0, j0 * 256), (CONV_WIDTH, 256)),
        "kv_norm": small["kv_norm"], "ckv_norm": small["ckv_norm"], "b_norm": small["b_norm"],
        "b_q_norm": small["b_q_norm"], "final_norm": small["final_norm"],
        "a_w_in": sh["w_in4"], "a_w_out": sh["w_out"],
        "w_dkv": jnp.concatenate([dwd[:, :KV_RANK], dwd[:, KV_RANK + ROPE_LO:KV_RANK + ROPE_LO + QK_ROPE]], axis=1),
        "w_ukv": w_ukv.reshape(KV_RANK, 2 * (QK_NOPE + V_HEAD)),
        "b_w_in": sh["w_bin"],
        "b_w_uq": dwq.reshape(Q_RANK, 2, HEAD_PAD)[:, :, :QK_NOPE + QK_ROPE].reshape(Q_RANK, -1),
        "b_w_out": sh["w_bout"],
    }


def _sub(ref, kind, j, cc):
    if kind == "lead":
        h = ref.shape[1] // 2
        return ref.at[j, pl.ds(pl.multiple_of(cc * h, 8), h), :]
    if kind == "row":
        rows = ref.shape[0] // N_CHIPS
        h = rows // 2
        return ref.at[pl.ds(pl.multiple_of(j * rows + cc * h, 8), h), :]
    cols = ref.shape[1] // N_CHIPS
    h = ref.shape[0] // 2
    return ref.at[pl.ds(pl.multiple_of(cc * h, 8), h), pl.ds(j * cols, cols)]


def _sub_shape(shape, kind):
    if kind == "lead":
        return (shape[1] // 2, shape[2])
    if kind == "row":
        return (shape[0] // N_CHIPS // 2, shape[1])
    return (shape[0] // 2, shape[1] // N_CHIPS)


def _reduce_grads(grads, kinds, vec):
    n = len(grads)
    shapes = [_sub_shape(a.shape, kd) for a, kd in zip(grads, kinds)]
    units = [(k, j) for k in range(n) for j in range(N_CHIPS)]
    big = (max(s[0] for s in shapes), max(s[1] for s in shapes))

    def body(*refs):
        g, v_ref = refs[:n], refs[n]
        out, o_ref = refs[n + 1:2 * n + 1], refs[2 * n + 1]
        theirs, part, recd, red = (refs[(2 + i) * n + 2:(3 + i) * n + 2] for i in range(4))
        mine, got = refs[6 * n + 2], refs[6 * n + 3]
        send1, recv1, send3, recv3, send5, recv5, load, local, send_v, recv_v = refs[6 * n + 4:]
        x, y, c = _place()
        j0 = 2 * x + y
        me = 2 * j0 + c
        sibling = (x, y, 1 - c)

        got[me] = v_ref[...]
        first, small_in = [], []
        for d in range(1, N_DEV):
            px, py, pc = x ^ (d >> 2), y ^ ((d >> 1) & 1), c ^ (d & 1)
            first.append(_remote(v_ref, got.at[me], send_v.at[d - 1], recv_v.at[d - 1], (px, py, pc)))
            small_in.append(_remote(v_ref, got.at[4 * px + 2 * py + pc], send_v.at[d - 1], recv_v.at[d - 1],
                                    (px, py, pc)))
        halves = [_remote(_sub(g[k], kinds[k], j, 1 - c), theirs[k].at[j], send1.at[u], recv1.at[u], sibling)
                  for u, (k, j) in enumerate(units)]
        for cp in first + halves:
            cp.start()

        def mine_load(u):
            k, j = units[u]
            h, cols = shapes[k]
            return pltpu.make_async_copy(_sub(g[k], kinds[k], j, c), mine.at[u % 2, pl.ds(0, h), pl.ds(0, cols)],
                                         load.at[u % 2])

        mine_load(0).start()
        for u, (k, j) in enumerate(units):
            h, cols = shapes[k]
            if u + 1 < len(units):
                mine_load(u + 1).start()
            mine_load(u).wait()
            halves[u].wait_recv()
            part[k][j] = (mine[u % 2, 0:h, 0:cols] + theirs[k][j]).astype(BF16)
            to_owner = _remote(part[k].at[j], recd[k].at[j0], send3.at[u], recv3.at[4 * k + j0], (j // 2, j % 2, c))

            @pl.when(j != j0)
            def _():
                to_owner.start()

            @pl.when(j == j0)
            def _():
                recd[k][j] = part[k][j]

        swaps = []
        for k in range(n):
            for j in range(N_CHIPS):
                arrived = _remote(part[k].at[j], recd[k].at[j], send3.at[4 * k + j], recv3.at[4 * k + j],
                                  (j // 2, j % 2, c))

                @pl.when(j != j0)
                def _():
                    arrived.wait_recv()

            r = recd[k]
            red[k][...] = ((r[0].astype(F32) + r[1].astype(F32)) + r[2].astype(F32)) + r[3].astype(F32)
            own = pltpu.make_async_copy(red[k], out[k].at[c], local.at[k])
            give = _remote(red[k], out[k].at[c], send5.at[k], recv5.at[k], sibling)
            take = _remote(red[k], out[k].at[1 - c], send5.at[k], recv5.at[k], sibling)
            own.start()
            give.start()
            swaps.append((own, give, take))

        for cp in small_in:
            cp.wait_recv()
            cp.wait_send()
        acc = got[0]
        for d in range(1, N_DEV):
            acc = acc + got[d]
        o_ref[...] = acc
        for u, (k, j) in enumerate(units):
            halves[u].wait_send()
            sent = _remote(part[k].at[j], recd[k].at[j0], send3.at[u], recv3.at[u], (j // 2, j % 2, c))

            @pl.when(j != j0)
            def _():
                sent.wait_send()
        for own, give, take in swaps:
            take.wait_recv()
            give.wait_send()
            own.wait()

    sems = [pltpu.SemaphoreType.DMA((len(units),))] * 4 + [pltpu.SemaphoreType.DMA((n,))] * 2 \
        + [pltpu.SemaphoreType.DMA((2,)), pltpu.SemaphoreType.DMA((n,))] + [pltpu.SemaphoreType.DMA((N_DEV - 1,))] * 2
    res = pl.pallas_call(
        body, name="reduce_grads",
        in_specs=[HBM] * n + [WHOLE], out_specs=[HBM] * n + [WHOLE],
        out_shape=[jax.ShapeDtypeStruct((2,) + s, F32) for s in shapes] + [jax.ShapeDtypeStruct(vec.shape, vec.dtype)],
        scratch_shapes=[pltpu.VMEM((N_CHIPS,) + s, F32) for s in shapes]
        + [pltpu.VMEM((N_CHIPS,) + s, BF16) for s in shapes] * 2
        + [pltpu.VMEM(s, F32) for s in shapes]
        + [pltpu.VMEM((2,) + big, F32), pltpu.VMEM((N_DEV,) + vec.shape, vec.dtype)] + sems,
        compiler_params=_comm_params(),
    )(*grads, vec)
    return res[:n], res[n]


def _adamw_math(w, g, m, v):
    m = ADAM_B1 * m + (1.0 - ADAM_B1) * g
    v = ADAM_B2 * v + (1.0 - ADAM_B2) * (g * g)
    m_hat = m / (1.0 - ADAM_B1 ** ADAM_STEP)
    v_hat = v / (1.0 - ADAM_B2 ** ADAM_STEP)
    return -ADAM_LR * (m_hat / (jnp.sqrt(v_hat) + ADAM_EPS) + ADAM_WD * w), m, v


def _adamw_tiled(w, g, m, v):
    rows, width = w.shape
    tm = rows // 4

    def body(w_ref, g_ref, m_ref, v_ref, go_ref, d_ref, mo_ref, vo_ref):
        g = g_ref[...]
        go_ref[...] = g
        d_ref[...], mo_ref[...], vo_ref[...] = _adamw_math(w_ref[...], g, m_ref[...], v_ref[...])

    spec = pl.BlockSpec((tm, width), lambda i: (i, 0))
    out = jax.ShapeDtypeStruct((rows, width), F32)
    return pl.pallas_call(
        body, grid=(rows // tm,), name="adamw_tiled",
        in_specs=[spec] * 4, out_specs=[spec] * 4, out_shape=[out] * 4,
        compiler_params=_params("parallel"),
    )(w, g, m, v)


def _adamw_many(ws, gs, ms, vs):
    n = len(ws)

    def body(*refs):
        for k in range(n):
            w_ref, g_ref, m_ref, v_ref = (refs[i * n + k] for i in range(4))
            go_ref, d_ref, mo_ref, vo_ref = (refs[(4 + i) * n + k] for i in range(4))
            g = g_ref[...]
            go_ref[...] = g
            d_ref[...], mo_ref[...], vo_ref[...] = _adamw_math(w_ref[...], g, m_ref[...], v_ref[...])

    outs = [jax.ShapeDtypeStruct(a.shape, F32) for a in ws]
    res = pl.pallas_call(
        body, name="adamw_many",
        in_specs=[WHOLE] * (4 * n), out_specs=[WHOLE] * (4 * n), out_shape=outs * 4,
        compiler_params=_comm_params(),
    )(*ws, *gs, *ms, *vs)
    return res[:n], res[n:2 * n], res[2 * n:3 * n], res[3 * n:]


def kernel(x, positions, a_norm, a_w_in, a_conv, a_w_out, kv_norm, w_dkv, ckv_norm, w_ukv, b_norm, b_w_in, b_q_norm, b_w_uq, b_w_out, final_norm, loss_target, m_a_norm, m_a_w_in, m_a_conv, m_a_w_out, m_kv_norm, m_w_dkv, m_ckv_norm, m_w_ukv, m_b_norm, m_b_w_in, m_b_q_norm, m_b_w_uq, m_b_w_out, m_final_norm, v_a_norm, v_a_w_in, v_a_conv, v_a_w_out, v_kv_norm, v_w_dkv, v_ckv_norm, v_w_ukv, v_b_norm, v_b_w_in, v_b_q_norm, v_b_w_uq, v_b_w_out, v_final_norm):
    w = dict(a_norm=a_norm, a_w_in=a_w_in, a_conv=a_conv, a_w_out=a_w_out, kv_norm=kv_norm, w_dkv=w_dkv,
             ckv_norm=ckv_norm, w_ukv=w_ukv, b_norm=b_norm, b_w_in=b_w_in, b_q_norm=b_q_norm, b_w_uq=b_w_uq,
             b_w_out=b_w_out, final_norm=final_norm)
    m = dict(a_norm=m_a_norm, a_w_in=m_a_w_in, a_conv=m_a_conv, a_w_out=m_a_w_out, kv_norm=m_kv_norm, w_dkv=m_w_dkv,
             ckv_norm=m_ckv_norm, w_ukv=m_w_ukv, b_norm=m_b_norm, b_w_in=m_b_w_in, b_q_norm=m_b_q_norm,
             b_w_uq=m_b_w_uq, b_w_out=m_b_w_out, final_norm=m_final_norm)
    v = dict(a_norm=v_a_norm, a_w_in=v_a_w_in, a_conv=v_a_conv, a_w_out=v_a_w_out, kv_norm=v_kv_norm, w_dkv=v_w_dkv,
             ckv_norm=v_ckv_norm, w_ukv=v_w_ukv, b_norm=v_b_norm, b_w_in=v_b_w_in, b_q_norm=v_b_q_norm,
             b_w_uq=v_b_w_uq, b_w_out=v_b_w_out, final_norm=v_final_norm)

    loss, dx, gmat, gsmall = _local_step(x, positions, loss_target, w)

    kinds = [GRAD_KIND[n] for n in GRAD_MATS]
    grads = [gmat[n] for n in GRAD_MATS]
    flat = jnp.concatenate([gsmall[n].reshape(-1) for n in SMALL] + [loss.reshape(1)])
    flat = jnp.pad(flat, (0, SMALL_ROWS * PACK_W - flat.shape[0])).reshape(SMALL_ROWS, PACK_W)
    mine, svec = _reduce_grads(grads, kinds, flat)
    loss = svec.reshape(-1)[LOSS_SLOT]
    g = _shard_grads({n: a.reshape(-1, a.shape[-1]) for n, a in zip(GRAD_MATS, mine)}, svec)

    def two_d(n, a):
        a = a.reshape(-1, a.shape[-1])
        return a.T if n in TRANSPOSED else a

    def back(n, a):
        return (a.T if n in TRANSPOSED else a).reshape(SHARD_SHAPES[n])

    big = "a_w_in"
    rest = [n for n in WEIGHTS if n != big]
    res_big = _adamw_tiled(*(two_d(big, t[big]) for t in (w, g, m, v)))
    res_rest = _adamw_many(*([two_d(n, t[n]) for n in rest] for t in (w, g, m, v)))
    out = {kind: dict(zip(rest, res_rest[i])) for i, kind in enumerate("gdmv")}
    for i, kind in enumerate("gdmv"):
        out[kind][big] = res_big[i]
    return (loss, dx) + tuple(back(n, out[kind][n]) for kind in "gdmv" for n in WEIGHTS)
```

```python
import functools
import math

import numpy as np
import jax
import jax.numpy as jnp
from jax import lax
from jax.experimental import pallas as pl
from jax.experimental.pallas import tpu as pltpu

F32 = jnp.float32
BF16 = jnp.bfloat16

D_MODEL = 1024
E_A = 1024
CONV_WIDTH = 3
N_HEADS = 8
QK_NOPE = 64
QK_ROPE = 32
V_HEAD = 64
KV_RANK = 256
Q_RANK = 384
E_B = N_HEADS * V_HEAD
ROPE_THETA = 10000.0
SOFTMAX_SCALE = 1.0 / math.sqrt(QK_NOPE + QK_ROPE)
LOG2E = math.log2(math.e)
LN2 = math.log(2.0)
Q_PRESCALE = SOFTMAX_SCALE * LOG2E
EPS = 1e-6
HEAD_PAD = 128
QK_PAD = N_HEADS * HEAD_PAD
ROPE_LO = QK_NOPE
ROPE_HALF = QK_ROPE // 2
KR_PAD = KV_RANK + HEAD_PAD

ADAM_LR = 0.001
ADAM_B1 = 0.9
ADAM_B2 = 0.999
ADAM_EPS = 1e-08
ADAM_WD = 0.01
ADAM_STEP = 10

VMEM_LIMIT = 56 * 1024 * 1024
ROW_TILE = 512
CONV_BWD_TILE = 256
ATT_TILE_FWD = 1024
ATT_TILE = 512
HEAD_CHAINS = 2
LANES = 128
PACK_W = 1024

N_CHIPS = 4
N_DEV = 8


def _dot(a, b):
    return jnp.dot(a, b, preferred_element_type=F32)


def _dot_nt(a, b):
    return lax.dot_general(a, b, (((1,), (1,)), ((), ())), preferred_element_type=F32)


def _dot_tn(a, b):
    return lax.dot_general(a, b, (((0,), (0,)), ((), ())), preferred_element_type=F32)


def _rms(x):
    r = lax.rsqrt(jnp.mean(x * x, axis=-1, keepdims=True) + EPS)
    return r, x * r


def _rms_bwd(dxh, xh, r):
    return r * (dxh - xh * jnp.mean(dxh * xh, axis=-1, keepdims=True))


def _rope_fwd(a, c, s1, s2):
    return a * c + pltpu.roll(a, HEAD_PAD - ROPE_HALF, 1) * s1 + pltpu.roll(a, ROPE_HALF, 1) * s2


def _rope_bwd(g, c, s1, s2):
    return g * c + pltpu.roll(g * s1, ROPE_HALF, 1) + pltpu.roll(g * s2, HEAD_PAD - ROPE_HALF, 1)


def _sigmoid(x):
    return 1.0 / (1.0 + jnp.exp(-x))


def _row_spec(tm, n):
    return pl.BlockSpec((tm, n), lambda i: (i, 0))


def _const_spec(shape):
    nd = len(shape)
    return pl.BlockSpec(shape, lambda i: (0,) * nd, pipeline_mode=pl.Buffered(1))


def _acc_spec(shape):
    nd = len(shape)
    return pl.BlockSpec(shape, lambda i: (0,) * nd, pipeline_mode=pl.Buffered(1))


def _params(*sem):
    return pltpu.CompilerParams(dimension_semantics=sem, vmem_limit_bytes=VMEM_LIMIT)


MESH = pl.DeviceIdType.MESH
HBM = pl.BlockSpec(memory_space=pl.ANY)
WHOLE = pl.BlockSpec(memory_space=pltpu.VMEM)
FLIPS = ((1, 0), (0, 1), (1, 1))


def _place():
    return lax.axis_index("x"), lax.axis_index("y"), lax.axis_index("c")


def _remote(src, dst, send, recv, peer):
    return pltpu.make_async_remote_copy(src_ref=src, dst_ref=dst, send_sem=send, recv_sem=recv, device_id=peer,
                                        device_id_type=MESH)


def _comm_params():
    return pltpu.CompilerParams(vmem_limit_bytes=VMEM_LIMIT)


def _rope_consts():
    lane = np.arange(HEAD_PAD)
    first = (lane >= ROPE_LO) & (lane < ROPE_LO + ROPE_HALF)
    second = (lane >= ROPE_LO + ROPE_HALF) & (lane < ROPE_LO + QK_ROPE)
    f = np.where(first, lane - ROPE_LO, np.where(second, lane - ROPE_LO - ROPE_HALF, 0))
    inv = np.float32(ROPE_THETA) ** (-(2 * f).astype(np.float32) / np.float32(QK_ROPE))
    out = np.zeros((8, HEAD_PAD), np.float32)
    out[0] = inv
    out[1] = first
    out[2] = second
    out[3] = lane < ROPE_LO
    return jnp.asarray(out)


def _gather_plan(w, sm, outs, osm, bf, sems):
    n = len(w)
    send_i, recv_i, send_d, recv_d, send_s, recv_s, local = sems
    x, y, c = _place()
    j0 = 2 * x + y
    sibling = (x, y, 1 - c)
    own = [pltpu.make_async_copy(bf[k], outs[k].at[j0], local.at[k]) for k in range(n)]
    if sm is not None:
        own.append(pltpu.make_async_copy(sm, osm.at[j0], local.at[n]))

    def half(k, cc):
        h = w[k].shape[0] // 2
        return pl.ds(pl.multiple_of(cc * h, 16), h)

    sends, arrivals, forwards, fwaits = [], [], [], []
    for i, (fx, fy) in enumerate(FLIPS):
        px, py = x ^ fx, y ^ fy
        pj = 2 * px + py
        for k in range(n):
            s = i * n + k
            sends.append(_remote(bf[k].at[half(k, c)], outs[k].at[j0, half(k, c)], send_i.at[s], recv_i.at[s],
                                 (px, py, c)))
            arrivals.append(_remote(bf[k].at[half(k, c)], outs[k].at[pj, half(k, c)], send_i.at[s], recv_i.at[s],
                                    (px, py, c)))
            forwards.append(_remote(outs[k].at[pj, half(k, c)], outs[k].at[pj, half(k, c)], send_d.at[s],
                                    recv_d.at[s], sibling))
            fwaits.append(_remote(outs[k].at[pj, half(k, 1 - c)], outs[k].at[pj, half(k, 1 - c)], send_d.at[s],
                                  recv_d.at[s], sibling))
        if sm is not None:
            sends.append(_remote(sm, osm.at[j0], send_s.at[i], recv_s.at[i], (px, py, c)))
            fwaits.append(_remote(sm, osm.at[pj], send_s.at[i], recv_s.at[i], (px, py, c)))
    return own, sends, arrivals, forwards, fwaits


def _gather_begin(w, bf, plan):
    own, sends, _, _, _ = plan
    for k in range(len(w)):
        bf[k][...] = w[k][...].astype(BF16)
    for cp in own + sends:
        cp.start()


def _gather_end(plan):
    own, sends, arrivals, forwards, fwaits = plan
    for cp, fwd in zip(arrivals, forwards):
        cp.wait_recv()
        fwd.start()
    for cp in fwaits:
        cp.wait_recv()
    for cp in sends + forwards:
        cp.wait_send()
    for cp in own:
        cp.wait()


def _gather_sems(n, with_small):
    return ([pltpu.SemaphoreType.DMA((3 * n,))] * 4 + [pltpu.SemaphoreType.DMA((3,))] * 2
            + [pltpu.SemaphoreType.DMA((n + (1 if with_small else 0),))])


def _rope_tables_gather(pos_col, w_in, small):
    t = pos_col.shape[0]
    tm = min(ROW_TILE, t)
    steps = t // tm

    def body(p_ref, k_ref, w_ref, sm_ref, c_ref, s1_ref, s2_ref, out_ref, osm_ref, bf_ref, *sems):
        plan = _gather_plan([w_ref], sm_ref, [out_ref], osm_ref, [bf_ref], sems)
        i = pl.program_id(0)

        @pl.when(i == 0)
        def _():
            _gather_begin([w_ref], [bf_ref], plan)

        inv, first, second, nope = k_ref[0:1, :], k_ref[1:2, :], k_ref[2:3, :], k_ref[3:4, :]
        ang = p_ref[...].astype(F32) * inv
        cs, sn = jnp.cos(ang), jnp.sin(ang)
        c_ref[...] = cs * (first + second) + nope
        s1_ref[...] = -sn * first
        s2_ref[...] = sn * second

        @pl.when(i == steps - 1)
        def _():
            _gather_end(plan)

    out = jax.ShapeDtypeStruct((t, HEAD_PAD), F32)
    return pl.pallas_call(
        body, grid=(steps,), name="rope_tables_gather",
        in_specs=[_row_spec(tm, 1), _const_spec((8, HEAD_PAD)), WHOLE, WHOLE],
        out_specs=[_row_spec(tm, HEAD_PAD)] * 3 + [HBM, HBM],
        out_shape=[out] * 3 + [jax.ShapeDtypeStruct((N_CHIPS,) + w_in.shape, BF16),
                               jax.ShapeDtypeStruct((N_CHIPS,) + small.shape, small.dtype)],
        scratch_shapes=[pltpu.VMEM(w_in.shape, BF16)] + _gather_sems(1, True),
        compiler_params=_params("arbitrary"),
    )(pos_col, _rope_consts(), w_in, small)


def _shift_down(v, prev, row):
    p1, p2 = prev[7:8, :], prev[6:7, :]
    v1 = jnp.where(row == 0, p1, pltpu.roll(v, 1, 0))
    v2 = jnp.where(row == 0, p2, jnp.where(row == 1, p1, pltpu.roll(v, 2, 0)))
    return v1, v2


def _conv_fwd(x, seq, ga, w_in4, cw, shards):
    t = x.shape[0]
    tm = min(ROW_TILE, seq)
    tiles_per_seq = seq // tm
    steps = t // tm
    n = len(shards)

    def body(x_ref, ga_ref, w_ref, cw_ref, *rest):
        sh, (b_ref, c_ref, u_ref, g_ref, ym_ref) = rest[:n], rest[n:n + 5]
        outs, carry_ref, bf, sems = rest[n + 5:2 * n + 5], rest[2 * n + 5], rest[2 * n + 6:3 * n + 6], rest[3 * n + 6:]
        plan = _gather_plan(sh, None, outs, None, bf, sems)
        i = pl.program_id(0)

        @pl.when(i == 0)
        def _():
            _gather_begin(sh, bf, plan)

        @pl.when(i % tiles_per_seq == 0)
        def _():
            carry_ref[...] = jnp.zeros_like(carry_ref)

        _, xh = _rms(x_ref[...])
        h = (xh * ga_ref[...]).astype(BF16)
        c, u = _dot(h, w_ref[1]), _dot(h, w_ref[2])
        g, b = _dot(h, w_ref[3]), _dot(h, w_ref[0])
        v = c * u
        row = lax.broadcasted_iota(jnp.int32, (tm, 1), 0)
        v1, v2 = _shift_down(v, carry_ref[...], row)
        carry_ref[...] = v[tm - 8:tm, :]
        cv = cw_ref[2:3, :] * v + cw_ref[1:2, :] * v1 + cw_ref[0:1, :] * v2
        b_ref[...] = b.astype(BF16)
        c_ref[...] = c.astype(BF16)
        u_ref[...] = u.astype(BF16)
        g_ref[...] = g.astype(BF16)
        ym_ref[...] = (g * _sigmoid(g) * b * cv).astype(BF16)

        @pl.when(i == steps - 1)
        def _():
            _gather_end(plan)

    out = jax.ShapeDtypeStruct((t, E_A), BF16)
    res = pl.pallas_call(
        body, grid=(steps,), name="conv_fwd",
        in_specs=[_row_spec(tm, D_MODEL), _const_spec((1, D_MODEL)), _const_spec((4, D_MODEL, E_A)),
                  _const_spec((8, E_A))] + [WHOLE] * n,
        out_specs=[_row_spec(tm, E_A)] * 5 + [HBM] * n,
        out_shape=[out] * 5 + [jax.ShapeDtypeStruct((N_CHIPS,) + a.shape, BF16) for a in shards],
        scratch_shapes=[pltpu.VMEM((8, E_A), F32)] + [pltpu.VMEM(a.shape, BF16) for a in shards]
        + _gather_sems(n, False),
        compiler_params=_params("arbitrary"),
    )(x, ga, w_in4, cw, *shards)
    return res[:5], res[5:]


def _mid_fwd(x, ym, w_out, gk, gb, w_dkv, gc, w_uk, w_uv, w_bin, gq, w_uq, rc, rs1, rs2):
    t = x.shape[0]
    tm = min(ROW_TILE, t)

    def body(x_ref, ym_ref, wo_ref, gk_ref, gb_ref, wd_ref, gc_ref, wk_ref, wv_ref, wi_ref, gq_ref, wq_ref,
             c_ref, s1_ref, s2_ref, x1_ref, q_ref, k_ref, v_ref, gate_ref, cq_ref, ckv_ref):
        cb, s1b, s2b = c_ref[...], s1_ref[...], s2_ref[...]
        x1 = x_ref[...] + _dot(ym_ref[...], wo_ref[...])
        x1_ref[...] = x1
        _, xh = _rms(x1)
        hk = (xh * gk_ref[...]).astype(BF16)
        h1 = (xh * gb_ref[...]).astype(BF16)

        pb = _dot(h1, wi_ref[...])
        cq = pb[:, :Q_RANK]
        cq_ref[...] = cq.astype(BF16)
        gate_ref[...] = pb[:, Q_RANK:].astype(BF16)
        _, cqh = _rms(cq)
        q = _dot((cqh * gq_ref[...]).astype(BF16), wq_ref[...])
        for h in range(N_HEADS):
            sl = slice(h * HEAD_PAD, (h + 1) * HEAD_PAD)
            q_ref[:, sl] = (_rope_fwd(q[:, sl], cb, s1b, s2b) * Q_PRESCALE).astype(BF16)

        ckr = _dot(hk, wd_ref[...])
        ckv_raw = ckr[:, :KV_RANK]
        ckv_ref[...] = ckv_raw.astype(BF16)
        _, ch = _rms(ckv_raw)
        ckv = (ch * gc_ref[...]).astype(BF16)
        kr = _rope_fwd(ckr[:, KV_RANK:], cb, s1b, s2b)
        kn = _dot(ckv, wk_ref[...])
        for h in range(N_HEADS):
            sl = slice(h * HEAD_PAD, (h + 1) * HEAD_PAD)
            k_ref[:, sl] = (kn[:, sl] + kr).astype(BF16)
        v_ref[...] = _dot(ckv, wv_ref[...]).astype(BF16)

    def sds(n, dt):
        return jax.ShapeDtypeStruct((t, n), dt)

    return pl.pallas_call(
        body, grid=(t // tm,), name="mid_fwd",
        in_specs=[_row_spec(tm, D_MODEL), _row_spec(tm, E_A), _const_spec((E_A, D_MODEL)),
                  _const_spec((1, D_MODEL)), _const_spec((1, D_MODEL)), _const_spec((D_MODEL, KR_PAD)),
                  _const_spec((1, KV_RANK)), _const_spec((KV_RANK, QK_PAD)), _const_spec((KV_RANK, E_B)),
                  _const_spec((D_MODEL, Q_RANK + E_B)), _const_spec((1, Q_RANK)), _const_spec((Q_RANK, QK_PAD)),
                  _row_spec(tm, HEAD_PAD), _row_spec(tm, HEAD_PAD), _row_spec(tm, HEAD_PAD)],
        out_specs=[_row_spec(tm, D_MODEL), _row_spec(tm, QK_PAD), _row_spec(tm, QK_PAD), _row_spec(tm, E_B),
                   _row_spec(tm, E_B), _row_spec(tm, Q_RANK), _row_spec(tm, KV_RANK)],
        out_shape=[sds(D_MODEL, F32), sds(QK_PAD, BF16), sds(QK_PAD, BF16), sds(E_B, BF16), sds(E_B, BF16),
                   sds(Q_RANK, BF16), sds(KV_RANK, BF16)],
        compiler_params=_params("parallel"),
    )(x, ym, w_out, gk, gb, w_dkv, gc, w_uk, w_uv, w_bin, gq, w_uq, rc, rs1, rs2)


def _pair_specs(seq):
    qk = pl.BlockSpec((seq, 2 * HEAD_PAD), lambda b, p: (b, p))
    vo = pl.BlockSpec((seq, 2 * V_HEAD), lambda b, p: (b, p))
    st = pl.BlockSpec((None, 2, seq), lambda b, p: (p, 0, b))
    return qk, vo, st


def _attn_fwd(q, k, v, seq):
    t = q.shape[0]
    tq = min(ATT_TILE_FWD, seq)
    nq = seq // tq

    def body(q_ref, k_ref, v_ref, o_ref, lse_ref, m_scr, l_scr, acc_scr):
        lane = lax.broadcasted_iota(jnp.int32, (tq, 2 * V_HEAD), 1)

        def q_step(qi, _):
            q0 = pl.multiple_of(qi * tq, tq)
            m_scr[...] = jnp.full(m_scr.shape, -jnp.inf, F32)
            l_scr[...] = jnp.zeros_like(l_scr)
            acc_scr[...] = jnp.zeros_like(acc_scr)

            def block(q_lo, q_n, k0, k_n, masked):
                rows = slice(q_lo, q_lo + q_n)
                vt = v_ref[pl.ds(k0, k_n), :]

                def score(hh):
                    hs = slice(hh * HEAD_PAD, (hh + 1) * HEAD_PAD)
                    return _dot_nt(q_ref[pl.ds(q0 + q_lo, q_n), hs], k_ref[pl.ds(k0, k_n), hs])

                early = [score(hh) for hh in range(2)] if masked else None
                for hh in range(2):
                    s = early[hh] if masked else score(hh)
                    if masked:
                        row = lax.broadcasted_iota(jnp.int32, (q_n, k_n), 0)
                        col = lax.broadcasted_iota(jnp.int32, (q_n, k_n), 1)
                        s = jnp.where(col <= row, s, -jnp.inf)
                    m_old = m_scr[hh, rows]
                    m_new = jnp.maximum(m_old, jnp.max(s, axis=-1, keepdims=True))
                    alpha = jnp.exp2(m_old - m_new)
                    ps = [jnp.exp2(s[:, j * LANES:(j + 1) * LANES] - m_new) for j in range(k_n // LANES)]
                    l_scr[hh, rows] = alpha * l_scr[hh, rows] + functools.reduce(lambda a, b: a + b, ps)
                    p = jnp.concatenate(ps, axis=-1).astype(BF16)
                    acc_scr[hh, rows] = alpha * acc_scr[hh, rows] + _dot(p, vt)
                    m_scr[hh, rows] = m_new

            def k_step(ki, _):
                block(0, tq, pl.multiple_of(ki * tq, tq), tq, False)
                return 0

            lax.fori_loop(0, qi, k_step, 0)
            half = tq // 2
            block(0, tq, q0, half, True)
            block(half, half, q0 + half, half, True)
            l0 = jnp.sum(l_scr[0], axis=-1, keepdims=True)
            l1 = jnp.sum(l_scr[1], axis=-1, keepdims=True)
            o_ref[pl.ds(q0, tq), :] = jnp.where(lane < V_HEAD, acc_scr[0] / l0, acc_scr[1] / l1).astype(BF16)
            stats = jnp.where(lane == 0, m_scr[0] + jnp.log2(l0), m_scr[1] + jnp.log2(l1)).T
            lse_ref[:, pl.ds(q0, tq)] = stats[0:2, :]
            return 0

        lax.fori_loop(0, nq, q_step, 0)

    qk, vo, st = _pair_specs(seq)
    return pl.pallas_call(
        body, grid=(t // seq, N_HEADS // 2), name="attn_fwd",
        in_specs=[qk, qk, vo], out_specs=[vo, st],
        out_shape=[jax.ShapeDtypeStruct((t, E_B), BF16), jax.ShapeDtypeStruct((N_HEADS // 2, 2, t), F32)],
        scratch_shapes=[pltpu.VMEM((2, tq, LANES), F32), pltpu.VMEM((2, tq, LANES), F32),
                        pltpu.VMEM((2, tq, 2 * V_HEAD), F32)],
        compiler_params=_params("parallel", "parallel"),
    )(q, k, v)


def _head_fwd_bwd(o, gate, x1, tgt, w_bout, gf):
    t = o.shape[0]
    tm = min(ROW_TILE, t)

    def body(o_ref, gate_ref, x1_ref, tgt_ref, w_ref, gf_ref,
             dx2_ref, do_ref, dgate_ref, dd_ref, loss_ref, dgf_ref, dw_ref):
        @pl.when(pl.program_id(0) == 0)
        def _():
            loss_ref[...] = jnp.zeros_like(loss_ref)
            dgf_ref[...] = jnp.zeros_like(dgf_ref)
            dw_ref[...] = jnp.zeros_like(dw_ref)

        hm = tm // HEAD_CHAINS
        gf = gf_ref[...]
        lane = lax.broadcasted_iota(jnp.int32, (hm, 2 * V_HEAD), 1)
        chains = []
        for ch in range(HEAD_CHAINS):
            rs = pl.ds(ch * hm, hm)
            o = o_ref[rs, :].astype(F32)
            gt = gate_ref[rs, :].astype(F32)
            sg = _sigmoid(gt)
            silu = gt * sg
            z = (o * silu).astype(BF16)
            chains.append((rs, o, gt, sg, silu, z, x1_ref[rs, :] + _dot(z, w_ref[...])))
        mids = []
        for rs, o, gt, sg, silu, z, x2 in chains:
            r2, xh2 = _rms(x2)
            err = xh2 * gf - tgt_ref[rs, :]
            loss_ref[...] += 0.5 * jnp.sum(jnp.mean(err * err, axis=-1, keepdims=True))
            dy = err * (1.0 / D_MODEL)
            dgf_ref[...] += jnp.sum(dy * xh2, axis=0, keepdims=True)
            dx2 = _rms_bwd(dy * gf, xh2, r2)
            dx2_ref[rs, :] = dx2
            dx2b = dx2.astype(BF16)
            mids.append(_dot_nt(dx2b, w_ref[...]))
            dw_ref[...] += _dot_tn(z, dx2b)
        for (rs, o, gt, sg, silu, z, x2), dz in zip(chains, mids):
            do = dz * silu
            do_ref[rs, :] = do.astype(BF16)
            dgate_ref[rs, :] = (dz * o * (sg * (1.0 + gt * (1.0 - sg)))).astype(BF16)
            prod = do * o
            cols = jnp.zeros((hm, LANES), F32)
            for p in range(N_HEADS // 2):
                blk = prod[:, p * 2 * V_HEAD:(p + 1) * 2 * V_HEAD]
                d0 = jnp.sum(jnp.where(lane < V_HEAD, blk, 0.0), axis=-1, keepdims=True)
                d1 = jnp.sum(jnp.where(lane < V_HEAD, 0.0, blk), axis=-1, keepdims=True)
                cols = jnp.where(lane == 2 * p, d0, jnp.where(lane == 2 * p + 1, d1, cols))
            rows = cols.T
            for h in range(N_HEADS):
                dd_ref[h // 2, h % 2:h % 2 + 1, rs] = rows[h:h + 1, :]

    return pl.pallas_call(
        body, grid=(t // tm,), name="head_fwd_bwd",
        in_specs=[_row_spec(tm, E_B), _row_spec(tm, E_B), _row_spec(tm, D_MODEL), _row_spec(tm, D_MODEL),
                  _const_spec((E_B, D_MODEL)), _const_spec((1, D_MODEL))],
        out_specs=[_row_spec(tm, D_MODEL), _row_spec(tm, E_B), _row_spec(tm, E_B),
                   pl.BlockSpec((N_HEADS // 2, 2, tm), lambda i: (0, 0, i)),
                   _acc_spec((1, 1)), _acc_spec((1, D_MODEL)), _acc_spec((E_B, D_MODEL))],
        out_shape=[jax.ShapeDtypeStruct((t, D_MODEL), F32), jax.ShapeDtypeStruct((t, E_B), BF16),
                   jax.ShapeDtypeStruct((t, E_B), BF16), jax.ShapeDtypeStruct((N_HEADS // 2, 2, t), F32),
                   jax.ShapeDtypeStruct((1, 1), F32), jax.ShapeDtypeStruct((1, D_MODEL), F32),
                   jax.ShapeDtypeStruct((E_B, D_MODEL), F32)],
        compiler_params=_params("arbitrary"),
    )(o, gate, x1, tgt, w_bout, gf)


def _attn_bwd(q, k, v, do, lse, dd, seq):
    t = q.shape[0]
    tq = min(ATT_TILE, seq)
    nq = seq // tq

    def body(q_ref, k_ref, v_ref, do_ref, lse_ref, dd_ref, dq_ref, dk_ref, dv_ref, dq_acc, dk_acc, dv_acc):
        dq_acc[...] = jnp.zeros_like(dq_acc)

        def k_step(ki, _):
            k0 = pl.multiple_of(ki * tq, tq)
            dk_acc[...] = jnp.zeros_like(dk_acc)
            dv_acc[...] = jnp.zeros_like(dv_acc)

            def block(k_lo, k_n, q0, q_n, masked):
                rows = slice(k_lo, k_lo + k_n)
                lane = lax.broadcasted_iota(jnp.int32, (q_n, 2 * V_HEAD), 1)
                vt = v_ref[pl.ds(k0 + k_lo, k_n), :]
                do_pair = do_ref[pl.ds(q0, q_n), :]

                def operands(hh):
                    hs = slice(hh * HEAD_PAD, (hh + 1) * HEAD_PAD)
                    kt = k_ref[pl.ds(k0 + k_lo, k_n), hs]
                    qt = q_ref[pl.ds(q0, q_n), hs]
                    mine = (lane < V_HEAD) if hh == 0 else (lane >= V_HEAD)
                    do_h = jnp.where(mine, do_pair, jnp.zeros((), BF16))
                    return hs, kt, qt, do_h, _dot_nt(kt, qt), _dot_nt(vt, do_h)

                early = [operands(hh) for hh in range(2)] if masked else None
                for hh in range(2):
                    hs, kt, qt, do_h, st, dpt = early[hh] if masked else operands(hh)
                    if masked:
                        krow = lax.broadcasted_iota(jnp.int32, (k_n, q_n), 0)
                        qcol = lax.broadcasted_iota(jnp.int32, (k_n, q_n), 1)
                        st = jnp.where(krow <= qcol, st, -jnp.inf)
                    pt = jnp.exp2(st - lse_ref[hh:hh + 1, pl.ds(q0, q_n)])
                    dst = (pt * (dpt - dd_ref[hh:hh + 1, pl.ds(q0, q_n)])).astype(BF16)
                    dv_acc[rows, :] += _dot(pt.astype(BF16), do_h)
                    dk_acc[rows, hs] += _dot(dst, qt)
                    dq_acc[pl.ds(q0, q_n), hs] += _dot_tn(dst, kt)

            def q_step(qi, _):
                block(0, tq, pl.multiple_of(qi * tq, tq), tq, False)
                return 0

            half = tq // 2
            block(0, half, k0, tq, True)
            block(half, half, pl.multiple_of(k0 + half, half), half, True)
            lax.fori_loop(ki + 1, nq, q_step, 0)
            dk_ref[pl.ds(k0, tq), :] = (dk_acc[...] * LN2).astype(BF16)
            dv_ref[pl.ds(k0, tq), :] = dv_acc[...].astype(BF16)
            return 0

        lax.fori_loop(0, nq, k_step, 0)
        dq_ref[...] = (dq_acc[...] * SOFTMAX_SCALE).astype(BF16)

    qk, vo, st = _pair_specs(seq)
    return pl.pallas_call(
        body, grid=(t // seq, N_HEADS // 2), name="attn_bwd",
        in_specs=[qk, qk, vo, vo, st, st], out_specs=[qk, qk, vo],
        out_shape=[jax.ShapeDtypeStruct((t, QK_PAD), BF16), jax.ShapeDtypeStruct((t, QK_PAD), BF16),
                   jax.ShapeDtypeStruct((t, E_B), BF16)],
        scratch_shapes=[pltpu.VMEM((seq, 2 * HEAD_PAD), F32), pltpu.VMEM((tq, 2 * HEAD_PAD), F32),
                        pltpu.VMEM((tq, 2 * V_HEAD), F32)],
        compiler_params=_params("parallel", "parallel"),
    )(q, k, v, do, lse, dd)


def _mid_bwd(dq, dk, dv, dgate, dx2, x1, cq, ckv, rc, rs1, rs2, w_uq, w_bin, w_uk, w_uv, w_dkv, gq, gc, gk, gb):
    t = dq.shape[0]
    tm = min(ROW_TILE, t)

    def body(dq_ref, dk_ref, dv_ref, dgate_ref, dx2_ref, x1_ref, cq_ref, ckv_ref, c_ref, s1_ref, s2_ref,
             wq_ref, wi_ref, wk_ref, wv_ref, wd_ref, gq_ref, gc_ref, gk_ref, gb_ref,
             dx1_ref, dwq_ref, dwi_ref, dwk_ref, dwv_ref, dwd_ref, dgq_ref, dgc_ref, dgk_ref, dgb_ref):
        @pl.when(pl.program_id(0) == 0)
        def _():
            for ref in (dwq_ref, dwi_ref, dwk_ref, dwv_ref, dwd_ref, dgq_ref, dgc_ref, dgk_ref, dgb_ref):
                ref[...] = jnp.zeros_like(ref)

        cb, s1b, s2b = c_ref[...], s1_ref[...], s2_ref[...]
        gk, gb, gq, gc = gk_ref[...], gb_ref[...], gq_ref[...], gc_ref[...]

        dkb, dvb = dk_ref[...], dv_ref[...]
        dckv = _dot_nt(dkb, wk_ref[...]) + _dot_nt(dvb, wv_ref[...])
        rcv, ch = _rms(ckv_ref[...].astype(F32))
        ckvn = (ch * gc).astype(BF16)
        dwk_ref[...] += _dot_tn(ckvn, dkb)
        dwv_ref[...] += _dot_tn(ckvn, dvb)

        dqs = [_rope_bwd(dq_ref[:, h * HEAD_PAD:(h + 1) * HEAD_PAD].astype(F32), cb, s1b, s2b)
               for h in range(N_HEADS)]
        dqb = jnp.concatenate(dqs, axis=-1).astype(BF16)
        rq, cqh = _rms(cq_ref[...].astype(F32))
        dcqn = _dot_nt(dqb, wq_ref[...])
        dwq_ref[...] += _dot_tn((cqh * gq).astype(BF16), dqb)

        dgc_ref[...] += jnp.sum(dckv * ch, axis=0, keepdims=True)
        dckv_raw = _rms_bwd(dckv * gc, ch, rcv)
        dkr = dk_ref[:, 0:HEAD_PAD].astype(F32)
        for h in range(1, N_HEADS):
            dkr = dkr + dk_ref[:, h * HEAD_PAD:(h + 1) * HEAD_PAD].astype(F32)
        dkr = _rope_bwd(dkr, cb, s1b, s2b)
        dckr = jnp.concatenate([dckv_raw, dkr], axis=-1).astype(BF16)
        r1, xh = _rms(x1_ref[...])
        dhk = _dot_nt(dckr, wd_ref[...])
        dwd_ref[...] += _dot_tn((xh * gk).astype(BF16), dckr)

        dgq_ref[...] += jnp.sum(dcqn * cqh, axis=0, keepdims=True)
        dcq = _rms_bwd(dcqn * gq, cqh, rq)
        dpb = jnp.concatenate([dcq.astype(BF16), dgate_ref[...]], axis=-1)
        dh1 = _dot_nt(dpb, wi_ref[...])
        dwi_ref[...] += _dot_tn((xh * gb).astype(BF16), dpb)

        dgb_ref[...] += jnp.sum(dh1 * xh, axis=0, keepdims=True)
        dgk_ref[...] += jnp.sum(dhk * xh, axis=0, keepdims=True)
        dx1_ref[...] = dx2_ref[...] + _rms_bwd(dh1 * gb + dhk * gk, xh, r1)

    acc_shapes = [(Q_RANK, QK_PAD), (D_MODEL, Q_RANK + E_B), (KV_RANK, QK_PAD), (KV_RANK, E_B), (D_MODEL, KR_PAD),
                  (1, Q_RANK), (1, KV_RANK), (1, D_MODEL), (1, D_MODEL)]
    return pl.pallas_call(
        body, grid=(t // tm,), name="mid_bwd",
        in_specs=[_row_spec(tm, QK_PAD), _row_spec(tm, QK_PAD), _row_spec(tm, E_B), _row_spec(tm, E_B),
                  _row_spec(tm, D_MODEL), _row_spec(tm, D_MODEL), _row_spec(tm, Q_RANK), _row_spec(tm, KV_RANK),
                  _row_spec(tm, HEAD_PAD), _row_spec(tm, HEAD_PAD), _row_spec(tm, HEAD_PAD),
                  _const_spec((Q_RANK, QK_PAD)), _const_spec((D_MODEL, Q_RANK + E_B)),
                  _const_spec((KV_RANK, QK_PAD)), _const_spec((KV_RANK, E_B)), _const_spec((D_MODEL, KR_PAD)),
                  _const_spec((1, Q_RANK)), _const_spec((1, KV_RANK)), _const_spec((1, D_MODEL)),
                  _const_spec((1, D_MODEL))],
        out_specs=[_row_spec(tm, D_MODEL)] + [_acc_spec(s) for s in acc_shapes],
        out_shape=[jax.ShapeDtypeStruct((t, D_MODEL), F32)] + [jax.ShapeDtypeStruct(s, F32) for s in acc_shapes],
        compiler_params=_params("arbitrary"),
    )(dq, dk, dv, dgate, dx2, x1, cq, ckv, rc, rs1, rs2, w_uq, w_bin, w_uk, w_uv, w_dkv, gq, gc, gk, gb)


def _conv_bwd(dx1, x, b, c, u, g, seq, w_out, w_in4, ga, cw):
    t = x.shape[0]
    tm = min(CONV_BWD_TILE, seq)
    tiles_per_seq = seq // tm
    n = t // tm
    halo = tm // 8

    def tile(i):
        return n - 1 - i

    def rev(width):
        return pl.BlockSpec((tm, width), lambda i: (tile(i), 0))

    def prev8(width):
        return pl.BlockSpec((8, width), lambda i: (jnp.maximum(tile(i) * halo - 1, 0), 0))

    def body(dx1_ref, x_ref, b_ref, c_ref, u_ref, g_ref, cp_ref, up_ref, wo_ref, wi_ref, ga_ref, cw_ref,
             dx_ref, dwi_ref, dwo_ref, dcw_ref, dga_ref, carry_ref):
        i = pl.program_id(0)
        j = tile(i)

        @pl.when(i == 0)
        def _():
            for ref in (dwi_ref, dwo_ref, dcw_ref, dga_ref):
                ref[...] = jnp.zeros_like(ref)

        @pl.when(j % tiles_per_seq == tiles_per_seq - 1)
        def _():
            carry_ref[...] = jnp.zeros_like(carry_ref)

        dx1 = dx1_ref[...]
        dx1b = dx1.astype(BF16)
        b, c, u, g = (r[...].astype(F32) for r in (b_ref, c_ref, u_ref, g_ref))
        v = c * u
        first = (j % tiles_per_seq == 0).astype(F32)
        vprev = cp_ref[...].astype(F32) * up_ref[...].astype(F32) * (1.0 - first)
        row = lax.broadcasted_iota(jnp.int32, (tm, 1), 0)
        v1, v2 = _shift_down(v, vprev, row)
        w0, w1, w2 = cw_ref[0:1, :], cw_ref[1:2, :], cw_ref[2:3, :]
        cv = w2 * v + w1 * v1 + w0 * v2
        sg = _sigmoid(g)
        silu = g * sg
        ym = (silu * b * cv).astype(BF16)
        dym = _dot_nt(dx1b, wo_ref[...])
        dwo_ref[...] += _dot_tn(ym, dx1b)
        db = dym * silu * cv
        dcv = dym * silu * b
        dg = dym * b * cv * (sg * (1.0 + g * (1.0 - sg)))

        nxt = carry_ref[...]
        n0, n1 = nxt[0:1, :], nxt[1:2, :]
        d1 = jnp.where(row == tm - 1, n0, pltpu.roll(dcv, tm - 1, 0))
        d2 = jnp.where(row == tm - 1, n1, jnp.where(row == tm - 2, n0, pltpu.roll(dcv, tm - 2, 0)))
        carry_ref[...] = dcv[0:8, :]
        dv = w2 * dcv + w1 * d1 + w0 * d2
        dcw_ref[0:1, :] += jnp.sum(dcv * v2, axis=0, keepdims=True)
        dcw_ref[1:2, :] += jnp.sum(dcv * v1, axis=0, keepdims=True)
        dcw_ref[2:3, :] += jnp.sum(dcv * v, axis=0, keepdims=True)

        r0, xh = _rms(x_ref[...])
        ga = ga_ref[...]
        h = (xh * ga).astype(BF16)
        dh = jnp.zeros((tm, D_MODEL), F32)
        for idx, dpart in enumerate((db, dv * u, dv * c, dg)):
            dpb = dpart.astype(BF16)
            dh = dh + _dot_nt(dpb, wi_ref[idx])
            dwi_ref[idx] += _dot_tn(h, dpb)
        dga_ref[...] += jnp.sum(dh * xh, axis=0, keepdims=True)
        dx_ref[...] = dx1 + _rms_bwd(dh * ga, xh, r0)

    acc_shapes = [(4, D_MODEL, E_A), (E_A, D_MODEL), (8, E_A), (1, D_MODEL)]
    return pl.pallas_call(
        body, grid=(n,), name="conv_bwd",
        in_specs=[rev(D_MODEL), rev(D_MODEL), rev(E_A), rev(E_A), rev(E_A), rev(E_A), prev8(E_A), prev8(E_A),
                  _const_spec((E_A, D_MODEL)), _const_spec((4, D_MODEL, E_A)), _const_spec((1, D_MODEL)),
                  _const_spec((8, E_A))],
        out_specs=[rev(D_MODEL)] + [_acc_spec(s) for s in acc_shapes],
        out_shape=[jax.ShapeDtypeStruct((t, D_MODEL), F32)] + [jax.ShapeDtypeStruct(s, F32) for s in acc_shapes],
        scratch_shapes=[pltpu.VMEM((8, E_A), F32)],
        compiler_params=_params("arbitrary"),
    )(dx1, x, b, c, u, g, c, u, w_out, w_in4, ga, cw)


WEIGHTS = ("a_norm", "a_w_in", "a_conv", "a_w_out", "kv_norm", "w_dkv", "ckv_norm", "w_ukv", "b_norm", "b_w_in",
           "b_q_norm", "b_w_uq", "b_w_out", "final_norm")
SHARD_SHAPES = {
    "a_norm": (1, 256), "a_w_in": (1, 1024, 1024), "a_conv": (1, 3, 256), "a_w_out": (1, 256, 1024),
    "kv_norm": (1024,), "w_dkv": (256, 288), "ckv_norm": (256,), "w_ukv": (256, 256), "b_norm": (1, 1024),
    "b_w_in": (1, 256, 896), "b_q_norm": (1, 384), "b_w_uq": (1, 384, 192), "b_w_out": (1, 512, 256),
    "final_norm": (1024,),
}
MATS = ("a_w_in", "a_w_out", "w_dkv", "w_ukv", "b_w_in", "b_w_uq", "b_w_out")
SMALL = ("a_norm", "a_conv", "kv_norm", "ckv_norm", "b_norm", "b_q_norm", "final_norm")
SMALL_FULL = {"a_norm": 1024, "a_conv": 3072, "kv_norm": 1024, "ckv_norm": 256, "b_norm": 1024, "b_q_norm": 384,
              "final_norm": 1024}
SMALL_ROWS = 8
LOSS_SLOT = sum(SMALL_FULL.values())


def _mat2d(name, a):
    return a.reshape(SHARD_SHAPES[name][-2:])


def _prep_first(g_win, gsmall):
    sm = gsmall.reshape(N_CHIPS, -1)
    a_conv = jnp.transpose(sm[:, 256:1024].reshape(N_CHIPS, CONV_WIDTH, 256), (1, 0, 2)).reshape(CONV_WIDTH, -1)
    return {"w_in4": g_win, "ga": sm[:, :256].reshape(1, -1), "cw": jnp.pad(a_conv, ((0, 8 - CONV_WIDTH), (0, 0)))}


def _prep_rest(gath, w):
    def cols(a):
        return jnp.transpose(a, (1, 0, 2)).reshape(a.shape[1], -1)

    def pad_heads(a, width):
        a = a.reshape(a.shape[0], N_HEADS, width)
        return jnp.pad(a, ((0, 0), (0, 0), (0, HEAD_PAD - width))).reshape(a.shape[0], QK_PAD)

    row = lambda a: a.reshape(1, -1).astype(F32)
    w_dkv = gath["w_dkv"].reshape(D_MODEL, KV_RANK + QK_ROPE)
    w_ukv = cols(gath["w_ukv"]).reshape(KV_RANK, N_HEADS, 2, QK_NOPE)
    return {
        "w_out": gath["a_w_out"].reshape(E_A, D_MODEL),
        "w_dkv": jnp.concatenate([w_dkv[:, :KV_RANK], jnp.zeros((D_MODEL, ROPE_LO), BF16), w_dkv[:, KV_RANK:],
                                  jnp.zeros((D_MODEL, HEAD_PAD - ROPE_LO - QK_ROPE), BF16)], axis=1),
        "w_uk": pad_heads(w_ukv[:, :, 0, :].reshape(KV_RANK, N_HEADS * QK_NOPE), QK_NOPE),
        "w_uv": w_ukv[:, :, 1, :].reshape(KV_RANK, E_B),
        "w_bin": gath["b_w_in"].reshape(D_MODEL, Q_RANK + E_B),
        "w_uq": pad_heads(cols(gath["b_w_uq"]), QK_NOPE + QK_ROPE),
        "w_bout": cols(gath["b_w_out"]),
        "gk": row(w["kv_norm"]), "gc": row(w["ckv_norm"]), "gb": row(w["b_norm"]),
        "gq": row(w["b_q_norm"]), "gf": row(w["final_norm"]),
    }


TRANSPOSED = ("w_dkv", "b_w_uq")
GRAD_MATS = ("w_in4", "w_out", "w_dkv", "w_uk", "w_uv", "w_bin", "w_uq", "w_bout")
GRAD_KIND = {"w_in4": "lead", "w_out": "row", "w_dkv": "row", "w_uk": "col", "w_uv": "col", "w_bin": "row",
             "w_uq": "col", "w_bout": "col"}


def _local_step(x, positions, tgt, w):
    bsz, seq, _ = x.shape
    t = bsz * seq
    x2d = x.reshape(t, D_MODEL)
    small = jnp.concatenate([w["a_norm"].reshape(-1), w["a_conv"].reshape(-1)]).reshape(8, LANES)
    rc, rs1, rs2, g_win, gsmall = _rope_tables_gather(positions.reshape(t, 1), _mat2d(MATS[0], w[MATS[0]]), small)
    wk = _prep_first(g_win, gsmall)
    (b, c, u, g, ym), gathered = _conv_fwd(x2d, seq, wk["ga"], wk["w_in4"], wk["cw"],
                                           [_mat2d(n, w[n]) for n in MATS[1:]])
    wk.update(_prep_rest(dict(zip(MATS[1:], gathered)), w))
    x1, q, k, v, gate, cq, ckv = _mid_fwd(x2d, ym, wk["w_out"], wk["gk"], wk["gb"], wk["w_dkv"], wk["gc"], wk["w_uk"],
                                          wk["w_uv"], wk["w_bin"], wk["gq"], wk["w_uq"], rc, rs1, rs2)
    o, lse = _attn_fwd(q, k, v, seq)
    dx2, do, dgate, dd, loss, dgf, dw_bout = _head_fwd_bwd(o, gate, x1, tgt.reshape(t, D_MODEL), wk["w_bout"], wk["gf"])
    dq, dk, dv = _attn_bwd(q, k, v, do, lse, dd, seq)
    dx1, dwq, dw_bin, dwk, dwv, dwd, dgq, dgc, dgk, dgb = _mid_bwd(
        dq, dk, dv, dgate, dx2, x1, cq, ckv, rc, rs1, rs2, wk["w_uq"], wk["w_bin"], wk["w_uk"], wk["w_uv"], wk["w_dkv"],
        wk["gq"], wk["gc"], wk["gk"], wk["gb"])
    dx, dw_in4, dw_out, dcw, dga = _conv_bwd(dx1, x2d, b, c, u, g, seq, wk["w_out"], wk["w_in4"], wk["ga"], wk["cw"])
    mats = {"w_in4": dw_in4, "w_out": dw_out, "w_dkv": dwd, "w_uk": dwk, "w_uv": dwv, "w_bin": dw_bin, "w_uq": dwq,
            "w_bout": dw_bout}
    small = {"a_norm": dga, "a_conv": dcw[:CONV_WIDTH], "kv_norm": dgk, "ckv_norm": dgc, "b_norm": dgb,
             "b_q_norm": dgq, "final_norm": dgf}
    return loss[0, 0], dx.reshape(bsz, seq, D_MODEL), mats, small


def _shard_grads(sh, svec):
    j0 = 2 * lax.axis_index("x") + lax.axis_index("y")
    flat = svec.reshape(-1)
    off, small = 0, {}
    for n in SMALL:
        small[n] = flat[off:off + SMALL_FULL[n]]
        off += SMALL_FULL[n]
    dwd, dwk, dwv, dwq = sh["w_dkv"], sh["w_uk"], sh["w_uv"], sh["w_uq"]
    w_ukv = jnp.stack([dwk.reshape(KV_RANK, 2, HEAD_PAD)[:, :, :QK_NOPE], dwv.reshape(KV_RANK, 2, V_HEAD)], axis=2)
    return {
        "a_norm": lax.dynamic_slice(small["a_norm"], (j0 * 256,), (256,)),
        "a_conv": lax.dynamic_slice(small["a_conv"].reshape(CONV_WIDTH, E_A), (0, j0 * 256), (CONV_WIDTH, 256)),
        "kv_norm": small["kv_norm"], "ckv_norm": small["ckv_norm"], "b_norm": small["b_norm"],
        "b_q_norm": small["b_q_norm"], "final_norm": small["final_norm"],
        "a_w_in": sh["w_in4"], "a_w_out": sh["w_out"],
        "w_dkv": jnp.concatenate([dwd[:, :KV_RANK], dwd[:, KV_RANK + ROPE_LO:KV_RANK + ROPE_LO + QK_ROPE]], axis=1),
        "w_ukv": w_ukv.reshape(KV_RANK, 2 * (QK_NOPE + V_HEAD)),
        "b_w_in": sh["w_bin"],
        "b_w_uq": dwq.reshape(Q_RANK, 2, HEAD_PAD)[:, :, :QK_NOPE + QK_ROPE].reshape(Q_RANK, -1),
        "b_w_out": sh["w_bout"],
    }


def _sub(ref, kind, j, cc):
    if kind == "lead":
        h = ref.shape[1] // 2
        return ref.at[j, pl.ds(pl.multiple_of(cc * h, 8), h), :]
    if kind == "row":
        rows = ref.shape[0] // N_CHIPS
        h = rows // 2
        return ref.at[pl.ds(pl.multiple_of(j * rows + cc * h, 8), h), :]
    cols = ref.shape[1] // N_CHIPS
    h = ref.shape[0] // 2
    return ref.at[pl.ds(pl.multiple_of(cc * h, 8), h), pl.ds(j * cols, cols)]


def _sub_shape(shape, kind):
    if kind == "lead":
        return (shape[1] // 2, shape[2])
    if kind == "row":
        return (shape[0] // N_CHIPS // 2, shape[1])
    return (shape[0] // 2, shape[1] // N_CHIPS)


def _reduce_grads(grads, kinds, vec):
    n = len(grads)
    shapes = [_sub_shape(a.shape, kd) for a, kd in zip(grads, kinds)]
    units = [(k, j) for k in range(n) for j in range(N_CHIPS)]
    big = (max(s[0] for s in shapes), max(s[1] for s in shapes))

    def body(*refs):
        g, v_ref = refs[:n], refs[n]
        out, o_ref = refs[n + 1:2 * n + 1], refs[2 * n + 1]
        theirs, part, recd, red = (refs[(2 + i) * n + 2:(3 + i) * n + 2] for i in range(4))
        mine, got = refs[6 * n + 2], refs[6 * n + 3]
        send1, recv1, send3, recv3, send5, recv5, load, local, send_v, recv_v = refs[6 * n + 4:]
        x, y, c = _place()
        j0 = 2 * x + y
        me = 2 * j0 + c
        sibling = (x, y, 1 - c)

        got[me] = v_ref[...]
        first, small_in = [], []
        for d in range(1, N_DEV):
            px, py, pc = x ^ (d >> 2), y ^ ((d >> 1) & 1), c ^ (d & 1)
            first.append(_remote(v_ref, got.at[me], send_v.at[d - 1], recv_v.at[d - 1], (px, py, pc)))
            small_in.append(_remote(v_ref, got.at[4 * px + 2 * py + pc], send_v.at[d - 1], recv_v.at[d - 1],
                                    (px, py, pc)))
        halves = [_remote(_sub(g[k], kinds[k], j, 1 - c), theirs[k].at[j], send1.at[u], recv1.at[u], sibling)
                  for u, (k, j) in enumerate(units)]
        for cp in first + halves:
            cp.start()

        def mine_load(u):
            k, j = units[u]
            h, cols = shapes[k]
            return pltpu.make_async_copy(_sub(g[k], kinds[k], j, c), mine.at[u % 2, pl.ds(0, h), pl.ds(0, cols)],
                                         load.at[u % 2])

        mine_load(0).start()
        for u, (k, j) in enumerate(units):
            h, cols = shapes[k]
            if u + 1 < len(units):
                mine_load(u + 1).start()
            mine_load(u).wait()
            halves[u].wait_recv()
            part[k][j] = (mine[u % 2, 0:h, 0:cols] + theirs[k][j]).astype(BF16)
            to_owner = _remote(part[k].at[j], recd[k].at[j0], send3.at[u], recv3.at[4 * k + j0], (j // 2, j % 2, c))

            @pl.when(j != j0)
            def _():
                to_owner.start()

            @pl.when(j == j0)
            def _():
                recd[k][j] = part[k][j]

        swaps = []
        for k in range(n):
            for j in range(N_CHIPS):
                arrived = _remote(part[k].at[j], recd[k].at[j], send3.at[4 * k + j], recv3.at[4 * k + j],
                                  (j // 2, j % 2, c))

                @pl.when(j != j0)
                def _():
                    arrived.wait_recv()

            r = recd[k]
            red[k][...] = ((r[0].astype(F32) + r[1].astype(F32)) + r[2].astype(F32)) + r[3].astype(F32)
            own = pltpu.make_async_copy(red[k], out[k].at[c], local.at[k])
            give = _remote(red[k], out[k].at[c], send5.at[k], recv5.at[k], sibling)
            take = _remote(red[k], out[k].at[1 - c], send5.at[k], recv5.at[k], sibling)
            own.start()
            give.start()
            swaps.append((own, give, take))

        for cp in small_in:
            cp.wait_recv()
            cp.wait_send()
        acc = got[0]
        for d in range(1, N_DEV):
            acc = acc + got[d]
        o_ref[...] = acc
        for u, (k, j) in enumerate(units):
            halves[u].wait_send()
            sent = _remote(part[k].at[j], recd[k].at[j0], send3.at[u], recv3.at[u], (j // 2, j % 2, c))

            @pl.when(j != j0)
            def _():
                sent.wait_send()
        for own, give, take in swaps:
            take.wait_recv()
            give.wait_send()
            own.wait()

    sems = [pltpu.SemaphoreType.DMA((len(units),))] * 4 + [pltpu.SemaphoreType.DMA((n,))] * 2 \
        + [pltpu.SemaphoreType.DMA((2,)), pltpu.SemaphoreType.DMA((n,))] + [pltpu.SemaphoreType.DMA((N_DEV - 1,))] * 2
    res = pl.pallas_call(
        body, name="reduce_grads",
        in_specs=[HBM] * n + [WHOLE], out_specs=[HBM] * n + [WHOLE],
        out_shape=[jax.ShapeDtypeStruct((2,) + s, F32) for s in shapes] + [jax.ShapeDtypeStruct(vec.shape, vec.dtype)],
        scratch_shapes=[pltpu.VMEM((N_CHIPS,) + s, F32) for s in shapes]
        + [pltpu.VMEM((N_CHIPS,) + s, BF16) for s in shapes] * 2
        + [pltpu.VMEM(s, F32) for s in shapes]
        + [pltpu.VMEM((2,) + big, F32), pltpu.VMEM((N_DEV,) + vec.shape, vec.dtype)] + sems,
        compiler_params=_comm_params(),
    )(*grads, vec)
    return res[:n], res[n]


def _adamw_math(w, g, m, v):
    m = ADAM_B1 * m + (1.0 - ADAM_B1) * g
    v = ADAM_B2 * v + (1.0 - ADAM_B2) * (g * g)
    m_hat = m / (1.0 - ADAM_B1 ** ADAM_STEP)
    v_hat = v / (1.0 - ADAM_B2 ** ADAM_STEP)
    return -ADAM_LR * (m_hat / (jnp.sqrt(v_hat) + ADAM_EPS) + ADAM_WD * w), m, v


def _adamw_tiled(w, g, m, v):
    rows, width = w.shape
    tm = rows // 4

    def body(w_ref, g_ref, m_ref, v_ref, go_ref, d_ref, mo_ref, vo_ref):
        g = g_ref[...]
        go_ref[...] = g
        d_ref[...], mo_ref[...], vo_ref[...] = _adamw_math(w_ref[...], g, m_ref[...], v_ref[...])

    spec = pl.BlockSpec((tm, width), lambda i: (i, 0))
    out = jax.ShapeDtypeStruct((rows, width), F32)
    return pl.pallas_call(
        body, grid=(rows // tm,), name="adamw_tiled",
        in_specs=[spec] * 4, out_specs=[spec] * 4, out_shape=[out] * 4,
        compiler_params=_params("parallel"),
    )(w, g, m, v)


def _adamw_many(ws, gs, ms, vs):
    n = len(ws)

    def body(*refs):
        for k in range(n):
            w_ref, g_ref, m_ref, v_ref = (refs[i * n + k] for i in range(4))
            go_ref, d_ref, mo_ref, vo_ref = (refs[(4 + i) * n + k] for i in range(4))
            g = g_ref[...]
            go_ref[...] = g
            d_ref[...], mo_ref[...], vo_ref[...] = _adamw_math(w_ref[...], g, m_ref[...], v_ref[...])

    outs = [jax.ShapeDtypeStruct(a.shape, F32) for a in ws]
    res = pl.pallas_call(
        body, name="adamw_many",
        in_specs=[WHOLE] * (4 * n), out_specs=[WHOLE] * (4 * n), out_shape=outs * 4,
        compiler_params=_comm_params(),
    )(*ws, *gs, *ms, *vs)
    return res[:n], res[n:2 * n], res[2 * n:3 * n], res[3 * n:]


def kernel(x, positions, a_norm, a_w_in, a_conv, a_w_out, kv_norm, w_dkv, ckv_norm, w_ukv, b_norm, b_w_in, b_q_norm, b_w_uq, b_w_out, final_norm, loss_target, m_a_norm, m_a_w_in, m_a_conv, m_a_w_out, m_kv_norm, m_w_dkv, m_ckv_norm, m_w_ukv, m_b_norm, m_b_w_in, m_b_q_norm, m_b_w_uq, m_b_w_out, m_final_norm, v_a_norm, v_a_w_in, v_a_conv, v_a_w_out, v_kv_norm, v_w_dkv, v_ckv_norm, v_w_ukv, v_b_norm, v_b_w_in, v_b_q_norm, v_b_w_uq, v_b_w_out, v_final_norm):
    w = dict(a_norm=a_norm, a_w_in=a_w_in, a_conv=a_conv, a_w_out=a_w_out, kv_norm=kv_norm, w_dkv=w_dkv,
             ckv_norm=ckv_norm, w_ukv=w_ukv, b_norm=b_norm, b_w_in=b_w_in, b_q_norm=b_q_norm, b_w_uq=b_w_uq,
             b_w_out=b_w_out, final_norm=final_norm)
    m = dict(a_norm=m_a_norm, a_w_in=m_a_w_in, a_conv=m_a_conv, a_w_out=m_a_w_out, kv_norm=m_kv_norm, w_dkv=m_w_dkv,
             ckv_norm=m_ckv_norm, w_ukv=m_w_ukv, b_norm=m_b_norm, b_w_in=m_b_w_in, b_q_norm=m_b_q_norm,
             b_w_uq=m_b_w_uq, b_w_out=m_b_w_out, final_norm=m_final_norm)
    v = dict(a_norm=v_a_norm, a_w_in=v_a_w_in, a_conv=v_a_conv, a_w_out=v_a_w_out, kv_norm=v_kv_norm, w_dkv=v_w_dkv,
             ckv_norm=v_ckv_norm, w_ukv=v_w_ukv, b_norm=v_b_norm, b_w_in=v_b_w_in, b_q_norm=v_b_q_norm,
             b_w_uq=v_b_w_uq, b_w_out=v_b_w_out, final_norm=v_final_norm)

    loss, dx, gmat, gsmall = _local_step(x, positions, loss_target, w)

    kinds = [GRAD_KIND[n] for n in GRAD_MATS]
    grads = [gmat[n] for n in GRAD_MATS]
    flat = jnp.concatenate([gsmall[n].reshape(-1) for n in SMALL] + [loss.reshape(1)])
    flat = jnp.pad(flat, (0, SMALL_ROWS * PACK_W - flat.shape[0])).reshape(SMALL_ROWS, PACK_W)
    mine, svec = _reduce_grads(grads, kinds, flat)
    loss = svec.reshape(-1)[LOSS_SLOT]
    g = _shard_grads({n: a.reshape(-1, a.shape[-1]) for n, a in zip(GRAD_MATS, mine)}, svec)

    def two_d(n, a):
        a = a.reshape(-1, a.shape[-1])
        return a.T if n in TRANSPOSED else a

    def back(n, a):
        return (a.T if n in TRANSPOSED else a).reshape(SHARD_SHAPES[n])

    big = "a_w_in"
    rest = [n for n in WEIGHTS if n != big]
    res_big = _adamw_tiled(*(two_d(big, t[big]) for t in (w, g, m, v)))
    res_rest = _adamw_many(*([two_d(n, t[n]) for n in rest] for t in (w, g, m, v)))
    out = {kind: dict(zip(rest, res_rest[i])) for i, kind in enumerate("gdmv")}
    for i, kind in enumerate("gdmv"):
        out[kind][big] = res_big[i]
    return (loss, dx) + tuple(back(n, out[kind][n]) for kind in "gdmv" for n in WEIGHTS)
```

```python
import functools
import math

import numpy as np
import jax
import jax.numpy as jnp
from jax import lax
from jax.experimental import pallas as pl
from jax.experimental.pallas import tpu as pltpu

F32 = jnp.float32
BF16 = jnp.bfloat16

D_MODEL = 1024
E_A = 1024
CONV_WIDTH = 3
N_HEADS = 8
QK_NOPE = 64
QK_ROPE = 32
V_HEAD = 64
KV_RANK = 256
Q_RANK = 384
E_B = N_HEADS * V_HEAD
ROPE_THETA = 10000.0
SOFTMAX_SCALE = 1.0 / math.sqrt(QK_NOPE + QK_ROPE)
LOG2E = math.log2(math.e)
LN2 = math.log(2.0)
Q_PRESCALE = SOFTMAX_SCALE * LOG2E
EPS = 1e-6
HEAD_PAD = 128
QK_PAD = N_HEADS * HEAD_PAD
ROPE_LO = QK_NOPE
ROPE_HALF = QK_ROPE // 2
KR_PAD = KV_RANK + HEAD_PAD

ADAM_LR = 0.001
ADAM_B1 = 0.9
ADAM_B2 = 0.999
ADAM_EPS = 1e-08
ADAM_WD = 0.01
ADAM_STEP = 10

VMEM_LIMIT = 56 * 1024 * 1024
ROW_TILE = 512
CONV_BWD_TILE = 256
ATT_TILE_FWD = 1024
ATT_TILE = 512
HEAD_CHAINS = 2
LANES = 128
PACK_W = 1024

N_CHIPS = 4
N_DEV = 8


def _dot(a, b):
    return jnp.dot(a, b, preferred_element_type=F32)


def _dot_nt(a, b):
    return lax.dot_general(a, b, (((1,), (1,)), ((), ())), preferred_element_type=F32)


def _dot_tn(a, b):
    return lax.dot_general(a, b, (((0,), (0,)), ((), ())), preferred_element_type=F32)


def _rms(x):
    r = lax.rsqrt(jnp.mean(x * x, axis=-1, keepdims=True) + EPS)
    return r, x * r


def _rms_bwd(dxh, xh, r):
    return r * (dxh - xh * jnp.mean(dxh * xh, axis=-1, keepdims=True))


def _rope_fwd(a, c, s1, s2):
    return a * c + pltpu.roll(a, HEAD_PAD - ROPE_HALF, 1) * s1 + pltpu.roll(a, ROPE_HALF, 1) * s2


def _rope_bwd(g, c, s1, s2):
    return g * c + pltpu.roll(g * s1, ROPE_HALF, 1) + pltpu.roll(g * s2, HEAD_PAD - ROPE_HALF, 1)


def _sigmoid(x):
    return 1.0 / (1.0 + jnp.exp(-x))


def _row_spec(tm, n):
    return pl.BlockSpec((tm, n), lambda i: (i, 0))


def _const_spec(shape):
    nd = len(shape)
    return pl.BlockSpec(shape, lambda i: (0,) * nd, pipeline_mode=pl.Buffered(1))


def _acc_spec(shape):
    nd = len(shape)
    return pl.BlockSpec(shape, lambda i: (0,) * nd, pipeline_mode=pl.Buffered(1))


def _params(*sem):
    return pltpu.CompilerParams(dimension_semantics=sem, vmem_limit_bytes=VMEM_LIMIT)


MESH = pl.DeviceIdType.MESH
HBM = pl.BlockSpec(memory_space=pl.ANY)
WHOLE = pl.BlockSpec(memory_space=pltpu.VMEM)
FLIPS = ((1, 0), (0, 1), (1, 1))


def _place():
    return lax.axis_index("x"), lax.axis_index("y"), lax.axis_index("c")


def _remote(src, dst, send, recv, peer):
    return pltpu.make_async_remote_copy(src_ref=src, dst_ref=dst, send_sem=send, recv_sem=recv, device_id=peer,
                                        device_id_type=MESH)


def _comm_params():
    return pltpu.CompilerParams(vmem_limit_bytes=VMEM_LIMIT)


def _rope_consts():
    lane = np.arange(HEAD_PAD)
    first = (lane >= ROPE_LO) & (lane < ROPE_LO + ROPE_HALF)
    second = (lane >= ROPE_LO + ROPE_HALF) & (lane < ROPE_LO + QK_ROPE)
    f = np.where(first, lane - ROPE_LO, np.where(second, lane - ROPE_LO - ROPE_HALF, 0))
    inv = np.float32(ROPE_THETA) ** (-(2 * f).astype(np.float32) / np.float32(QK_ROPE))
    out = np.zeros((8, HEAD_PAD), np.float32)
    out[0] = inv
    out[1] = first
    out[2] = second
    out[3] = lane < ROPE_LO
    return jnp.asarray(out)


def _gather_plan(w, sm, outs, osm, bf, sems):
    n = len(w)
    send_i, recv_i, send_d, recv_d, send_s, recv_s, local = sems
    x, y, c = _place()
    j0 = 2 * x + y
    sibling = (x, y, 1 - c)
    own = [pltpu.make_async_copy(bf[k], outs[k].at[j0], local.at[k]) for k in range(n)]
    if sm is not None:
        own.append(pltpu.make_async_copy(sm, osm.at[j0], local.at[n]))

    def half(k, cc):
        h = w[k].shape[0] // 2
        return pl.ds(pl.multiple_of(cc * h, 16), h)

    sends, arrivals, forwards, fwaits = [], [], [], []
    for i, (fx, fy) in enumerate(FLIPS):
        px, py = x ^ fx, y ^ fy
        pj = 2 * px + py
        for k in range(n):
            s = i * n + k
            sends.append(_remote(bf[k].at[half(k, c)], outs[k].at[j0, half(k, c)], send_i.at[s], recv_i.at[s],
                                 (px, py, c)))
            arrivals.append(_remote(bf[k].at[half(k, c)], outs[k].at[pj, half(k, c)], send_i.at[s], recv_i.at[s],
                                    (px, py, c)))
            forwards.append(_remote(outs[k].at[pj, half(k, c)], outs[k].at[pj, half(k, c)], send_d.at[s],
                                    recv_d.at[s], sibling))
            fwaits.append(_remote(outs[k].at[pj, half(k, 1 - c)], outs[k].at[pj, half(k, 1 - c)], send_d.at[s],
                                  recv_d.at[s], sibling))
        if sm is not None:
            sends.append(_remote(sm, osm.at[j0], send_s.at[i], recv_s.at[i], (px, py, c)))
            fwaits.append(_remote(sm, osm.at[pj], send_s.at[i], recv_s.at[i], (px, py, c)))
    return own, sends, arrivals, forwards, fwaits


def _gather_begin(w, bf, plan):
    own, sends, _, _, _ = plan
    for k in range(len(w)):
        bf[k][...] = w[k][...].astype(BF16)
    for cp in own + sends:
        cp.start()


def _gather_end(plan):
    own, sends, arrivals, forwards, fwaits = plan
    for cp, fwd in zip(arrivals, forwards):
        cp.wait_recv()
        fwd.start()
    for cp in fwaits:
        cp.wait_recv()
    for cp in sends + forwards:
        cp.wait_send()
    for cp in own:
        cp.wait()


def _gather_sems(n, with_small):
    return ([pltpu.SemaphoreType.DMA((3 * n,))] * 4 + [pltpu.SemaphoreType.DMA((3,))] * 2
            + [pltpu.SemaphoreType.DMA((n + (1 if with_small else 0),))])


def _rope_tables_gather(pos_col, w_in, small):
    t = pos_col.shape[0]
    tm = min(ROW_TILE, t)
    steps = t // tm

    def body(p_ref, k_ref, w_ref, sm_ref, c_ref, s1_ref, s2_ref, out_ref, osm_ref, bf_ref, *sems):
        plan = _gather_plan([w_ref], sm_ref, [out_ref], osm_ref, [bf_ref], sems)
        i = pl.program_id(0)

        @pl.when(i == 0)
        def _():
            _gather_begin([w_ref], [bf_ref], plan)

        inv, first, second, nope = k_ref[0:1, :], k_ref[1:2, :], k_ref[2:3, :], k_ref[3:4, :]
        ang = p_ref[...].astype(F32) * inv
        cs, sn = jnp.cos(ang), jnp.sin(ang)
        c_ref[...] = cs * (first + second) + nope
        s1_ref[...] = -sn * first
        s2_ref[...] = sn * second

        @pl.when(i == steps - 1)
        def _():
            _gather_end(plan)

    out = jax.ShapeDtypeStruct((t, HEAD_PAD), F32)
    return pl.pallas_call(
        body, grid=(steps,), name="rope_tables_gather",
        in_specs=[_row_spec(tm, 1), _const_spec((8, HEAD_PAD)), WHOLE, WHOLE],
        out_specs=[_row_spec(tm, HEAD_PAD)] * 3 + [HBM, HBM],
        out_shape=[out] * 3 + [jax.ShapeDtypeStruct((N_CHIPS,) + w_in.shape, BF16),
                               jax.ShapeDtypeStruct((N_CHIPS,) + small.shape, small.dtype)],
        scratch_shapes=[pltpu.VMEM(w_in.shape, BF16)] + _gather_sems(1, True),
        compiler_params=_params("arbitrary"),
    )(pos_col, _rope_consts(), w_in, small)


def _shift_down(v, prev, row):
    p1, p2 = prev[7:8, :], prev[6:7, :]
    v1 = jnp.where(row == 0, p1, pltpu.roll(v, 1, 0))
    v2 = jnp.where(row == 0, p2, jnp.where(row == 1, p1, pltpu.roll(v, 2, 0)))
    return v1, v2


def _conv_fwd(x, seq, ga, w_in4, cw, shards):
    t = x.shape[0]
    tm = min(ROW_TILE, seq)
    tiles_per_seq = seq // tm
    steps = t // tm
    n = len(shards)

    def body(x_ref, ga_ref, w_ref, cw_ref, *rest):
        sh, (c_ref, u_ref, v_ref, ym_ref, kb_ref, kcv_ref, kg_ref) = rest[:n], rest[n:n + 7]
        outs, carry_ref, bf, sems = rest[n + 7:2 * n + 7], rest[2 * n + 7], rest[2 * n + 8:3 * n + 8], rest[3 * n + 8:]
        plan = _gather_plan(sh, None, outs, None, bf, sems)
        i = pl.program_id(0)

        @pl.when(i == 0)
        def _():
            _gather_begin(sh, bf, plan)

        @pl.when(i % tiles_per_seq == 0)
        def _():
            carry_ref[...] = jnp.zeros_like(carry_ref)

        _, xh = _rms(x_ref[...])
        h = (xh * ga_ref[...]).astype(BF16)
        c, u = _dot(h, w_ref[1]), _dot(h, w_ref[2])
        g, b = _dot(h, w_ref[3]), _dot(h, w_ref[0])
        v = c * u
        row = lax.broadcasted_iota(jnp.int32, (tm, 1), 0)
        v1, v2 = _shift_down(v, carry_ref[...], row)
        carry_ref[...] = v[tm - 8:tm, :]
        cv = cw_ref[2:3, :] * v + cw_ref[1:2, :] * v1 + cw_ref[0:1, :] * v2
        sg = _sigmoid(g)
        silu = g * sg
        c_ref[...] = c.astype(BF16)
        u_ref[...] = u.astype(BF16)
        v_ref[...] = v.astype(BF16)
        ym_ref[...] = (silu * b * cv).astype(BF16)
        kb_ref[...] = (silu * cv).astype(BF16)
        kcv_ref[...] = (silu * b).astype(BF16)
        kg_ref[...] = (b * cv * (sg * (1.0 + g * (1.0 - sg)))).astype(BF16)

        @pl.when(i == steps - 1)
        def _():
            _gather_end(plan)

    out = jax.ShapeDtypeStruct((t, E_A), BF16)
    res = pl.pallas_call(
        body, grid=(steps,), name="conv_fwd",
        in_specs=[_row_spec(tm, D_MODEL), _const_spec((1, D_MODEL)), _const_spec((4, D_MODEL, E_A)),
                  _const_spec((8, E_A))] + [WHOLE] * n,
        out_specs=[_row_spec(tm, E_A)] * 7 + [HBM] * n,
        out_shape=[out] * 7 + [jax.ShapeDtypeStruct((N_CHIPS,) + a.shape, BF16) for a in shards],
        scratch_shapes=[pltpu.VMEM((8, E_A), F32)] + [pltpu.VMEM(a.shape, BF16) for a in shards]
        + _gather_sems(n, False),
        compiler_params=_params("arbitrary"),
    )(x, ga, w_in4, cw, *shards)
    return res[:7], res[7:]


def _mid_fwd(x, ym, w_out, gk, gb, w_dkv, gc, w_uk, w_uv, w_bin, gq, w_uq, rc, rs1, rs2):
    t = x.shape[0]
    tm = min(ROW_TILE, t)

    def body(x_ref, ym_ref, wo_ref, gk_ref, gb_ref, wd_ref, gc_ref, wk_ref, wv_ref, wi_ref, gq_ref, wq_ref,
             c_ref, s1_ref, s2_ref, x1_ref, q_ref, k_ref, v_ref, gate_ref, cq_ref, ckv_ref):
        cb, s1b, s2b = c_ref[...], s1_ref[...], s2_ref[...]
        x1 = x_ref[...] + _dot(ym_ref[...], wo_ref[...])
        x1_ref[...] = x1
        _, xh = _rms(x1)
        hk = (xh * gk_ref[...]).astype(BF16)
        h1 = (xh * gb_ref[...]).astype(BF16)

        pb = _dot(h1, wi_ref[...])
        cq = pb[:, :Q_RANK]
        cq_ref[...] = cq.astype(BF16)
        gate_ref[...] = pb[:, Q_RANK:].astype(BF16)
        _, cqh = _rms(cq)
        q = _dot((cqh * gq_ref[...]).astype(BF16), wq_ref[...])
        for h in range(N_HEADS):
            sl = slice(h * HEAD_PAD, (h + 1) * HEAD_PAD)
            q_ref[:, sl] = (_rope_fwd(q[:, sl], cb, s1b, s2b) * Q_PRESCALE).astype(BF16)

        ckr = _dot(hk, wd_ref[...])
        ckv_raw = ckr[:, :KV_RANK]
        ckv_ref[...] = ckv_raw.astype(BF16)
        _, ch = _rms(ckv_raw)
        ckv = (ch * gc_ref[...]).astype(BF16)
        kr = _rope_fwd(ckr[:, KV_RANK:], cb, s1b, s2b)
        kn = _dot(ckv, wk_ref[...])
        for h in range(N_HEADS):
            sl = slice(h * HEAD_PAD, (h + 1) * HEAD_PAD)
            k_ref[:, sl] = (kn[:, sl] + kr).astype(BF16)
        v_ref[...] = _dot(ckv, wv_ref[...]).astype(BF16)

    def sds(n, dt):
        return jax.ShapeDtypeStruct((t, n), dt)

    return pl.pallas_call(
        body, grid=(t // tm,), name="mid_fwd",
        in_specs=[_row_spec(tm, D_MODEL), _row_spec(tm, E_A), _const_spec((E_A, D_MODEL)),
                  _const_spec((1, D_MODEL)), _const_spec((1, D_MODEL)), _const_spec((D_MODEL, KR_PAD)),
                  _const_spec((1, KV_RANK)), _const_spec((KV_RANK, QK_PAD)), _const_spec((KV_RANK, E_B)),
                  _const_spec((D_MODEL, Q_RANK + E_B)), _const_spec((1, Q_RANK)), _const_spec((Q_RANK, QK_PAD)),
                  _row_spec(tm, HEAD_PAD), _row_spec(tm, HEAD_PAD), _row_spec(tm, HEAD_PAD)],
        out_specs=[_row_spec(tm, D_MODEL), _row_spec(tm, QK_PAD), _row_spec(tm, QK_PAD), _row_spec(tm, E_B),
                   _row_spec(tm, E_B), _row_spec(tm, Q_RANK), _row_spec(tm, KV_RANK)],
        out_shape=[sds(D_MODEL, F32), sds(QK_PAD, BF16), sds(QK_PAD, BF16), sds(E_B, BF16), sds(E_B, BF16),
                   sds(Q_RANK, BF16), sds(KV_RANK, BF16)],
        compiler_params=_params("parallel"),
    )(x, ym, w_out, gk, gb, w_dkv, gc, w_uk, w_uv, w_bin, gq, w_uq, rc, rs1, rs2)


def _pair_specs(seq):
    qk = pl.BlockSpec((seq, 2 * HEAD_PAD), lambda b, p: (b, p))
    vo = pl.BlockSpec((seq, 2 * V_HEAD), lambda b, p: (b, p))
    st = pl.BlockSpec((None, 2, seq), lambda b, p: (p, 0, b))
    return qk, vo, st


def _attn_fwd(q, k, v, seq):
    t = q.shape[0]
    tq = min(ATT_TILE_FWD, seq)
    nq = seq // tq

    def body(q_ref, k_ref, v_ref, o_ref, lse_ref, m_scr, l_scr, acc_scr):
        lane = lax.broadcasted_iota(jnp.int32, (tq, 2 * V_HEAD), 1)

        def q_step(qi, _):
            q0 = pl.multiple_of(qi * tq, tq)
            m_scr[...] = jnp.full(m_scr.shape, -jnp.inf, F32)
            l_scr[...] = jnp.zeros_like(l_scr)
            acc_scr[...] = jnp.zeros_like(acc_scr)

            def block(q_lo, q_n, k0, k_n, masked):
                rows = slice(q_lo, q_lo + q_n)
                vt = v_ref[pl.ds(k0, k_n), :]

                def score(hh):
                    hs = slice(hh * HEAD_PAD, (hh + 1) * HEAD_PAD)
                    return _dot_nt(q_ref[pl.ds(q0 + q_lo, q_n), hs], k_ref[pl.ds(k0, k_n), hs])

                early = [score(hh) for hh in range(2)] if masked else None
                for hh in range(2):
                    s = early[hh] if masked else score(hh)
                    if masked:
                        row = lax.broadcasted_iota(jnp.int32, (q_n, k_n), 0)
                        col = lax.broadcasted_iota(jnp.int32, (q_n, k_n), 1)
                        s = jnp.where(col <= row, s, -jnp.inf)
                    m_old = m_scr[hh, rows]
                    m_new = jnp.maximum(m_old, jnp.max(s, axis=-1, keepdims=True))
                    alpha = jnp.exp2(m_old - m_new)
                    ps = [jnp.exp2(s[:, j * LANES:(j + 1) * LANES] - m_new) for j in range(k_n // LANES)]
                    l_scr[hh, rows] = alpha * l_scr[hh, rows] + functools.reduce(lambda a, b: a + b, ps)
                    p = jnp.concatenate(ps, axis=-1).astype(BF16)
                    acc_scr[hh, rows] = alpha * acc_scr[hh, rows] + _dot(p, vt)
                    m_scr[hh, rows] = m_new

            def k_step(ki, _):
                block(0, tq, pl.multiple_of(ki * tq, tq), tq, False)
                return 0

            lax.fori_loop(0, qi, k_step, 0)
            half = tq // 2
            block(0, tq, q0, half, True)
            block(half, half, q0 + half, half, True)
            l0 = jnp.sum(l_scr[0], axis=-1, keepdims=True)
            l1 = jnp.sum(l_scr[1], axis=-1, keepdims=True)
            o_ref[pl.ds(q0, tq), :] = jnp.where(lane < V_HEAD, acc_scr[0] / l0, acc_scr[1] / l1).astype(BF16)
            stats = jnp.where(lane == 0, m_scr[0] + jnp.log2(l0), m_scr[1] + jnp.log2(l1)).T
            lse_ref[:, pl.ds(q0, tq)] = stats[0:2, :]
            return 0

        lax.fori_loop(0, nq, q_step, 0)

    qk, vo, st = _pair_specs(seq)
    return pl.pallas_call(
        body, grid=(t // seq, N_HEADS // 2), name="attn_fwd",
        in_specs=[qk, qk, vo], out_specs=[vo, st],
        out_shape=[jax.ShapeDtypeStruct((t, E_B), BF16), jax.ShapeDtypeStruct((N_HEADS // 2, 2, t), F32)],
        scratch_shapes=[pltpu.VMEM((2, tq, LANES), F32), pltpu.VMEM((2, tq, LANES), F32),
                        pltpu.VMEM((2, tq, 2 * V_HEAD), F32)],
        compiler_params=_params("parallel", "parallel"),
    )(q, k, v)


def _head_fwd_bwd(o, gate, x1, tgt, w_bout, gf):
    t = o.shape[0]
    tm = min(ROW_TILE, t)

    def body(o_ref, gate_ref, x1_ref, tgt_ref, w_ref, gf_ref,
             dx2_ref, do_ref, dgate_ref, dd_ref, loss_ref, dgf_ref, dw_ref):
        @pl.when(pl.program_id(0) == 0)
        def _():
            loss_ref[...] = jnp.zeros_like(loss_ref)
            dgf_ref[...] = jnp.zeros_like(dgf_ref)
            dw_ref[...] = jnp.zeros_like(dw_ref)

        hm = tm // HEAD_CHAINS
        gf = gf_ref[...]
        lane = lax.broadcasted_iota(jnp.int32, (hm, 2 * V_HEAD), 1)
        chains = []
        for ch in range(HEAD_CHAINS):
            rs = pl.ds(ch * hm, hm)
            o = o_ref[rs, :].astype(F32)
            gt = gate_ref[rs, :].astype(F32)
            sg = _sigmoid(gt)
            silu = gt * sg
            z = (o * silu).astype(BF16)
            chains.append((rs, o, gt, sg, silu, z, x1_ref[rs, :] + _dot(z, w_ref[...])))
        mids = []
        for rs, o, gt, sg, silu, z, x2 in chains:
            r2, xh2 = _rms(x2)
            err = xh2 * gf - tgt_ref[rs, :]
            loss_ref[...] += 0.5 * jnp.sum(jnp.mean(err * err, axis=-1, keepdims=True))
            dy = err * (1.0 / D_MODEL)
            dgf_ref[...] += jnp.sum(dy * xh2, axis=0, keepdims=True)
            dx2 = _rms_bwd(dy * gf, xh2, r2)
            dx2_ref[rs, :] = dx2
            dx2b = dx2.astype(BF16)
            mids.append(_dot_nt(dx2b, w_ref[...]))
            dw_ref[...] += _dot_tn(z, dx2b)
        for (rs, o, gt, sg, silu, z, x2), dz in zip(chains, mids):
            do = dz * silu
            do_ref[rs, :] = do.astype(BF16)
            dgate_ref[rs, :] = (dz * o * (sg * (1.0 + gt * (1.0 - sg)))).astype(BF16)
            prod = do * o
            cols = jnp.zeros((hm, LANES), F32)
            for p in range(N_HEADS // 2):
                blk = prod[:, p * 2 * V_HEAD:(p + 1) * 2 * V_HEAD]
                d0 = jnp.sum(jnp.where(lane < V_HEAD, blk, 0.0), axis=-1, keepdims=True)
                d1 = jnp.sum(jnp.where(lane < V_HEAD, 0.0, blk), axis=-1, keepdims=True)
                cols = jnp.where(lane == 2 * p, d0, jnp.where(lane == 2 * p + 1, d1, cols))
            rows = cols.T
            for h in range(N_HEADS):
                dd_ref[h // 2, h % 2:h % 2 + 1, rs] = rows[h:h + 1, :]

    return pl.pallas_call(
        body, grid=(t // tm,), name="head_fwd_bwd",
        in_specs=[_row_spec(tm, E_B), _row_spec(tm, E_B), _row_spec(tm, D_MODEL), _row_spec(tm, D_MODEL),
                  _const_spec((E_B, D_MODEL)), _const_spec((1, D_MODEL))],
        out_specs=[_row_spec(tm, D_MODEL), _row_spec(tm, E_B), _row_spec(tm, E_B),
                   pl.BlockSpec((N_HEADS // 2, 2, tm), lambda i: (0, 0, i)),
                   _acc_spec((1, 1)), _acc_spec((1, D_MODEL)), _acc_spec((E_B, D_MODEL))],
        out_shape=[jax.ShapeDtypeStruct((t, D_MODEL), F32), jax.ShapeDtypeStruct((t, E_B), BF16),
                   jax.ShapeDtypeStruct((t, E_B), BF16), jax.ShapeDtypeStruct((N_HEADS // 2, 2, t), F32),
                   jax.ShapeDtypeStruct((1, 1), F32), jax.ShapeDtypeStruct((1, D_MODEL), F32),
                   jax.ShapeDtypeStruct((E_B, D_MODEL), F32)],
        compiler_params=_params("arbitrary"),
    )(o, gate, x1, tgt, w_bout, gf)


def _attn_bwd(q, k, v, do, lse, dd, seq):
    t = q.shape[0]
    tq = min(ATT_TILE, seq)
    nq = seq // tq

    def body(q_ref, k_ref, v_ref, do_ref, lse_ref, dd_ref, dq_ref, dk_ref, dv_ref, dq_acc, dk_acc, dv_acc):
        dq_acc[...] = jnp.zeros_like(dq_acc)

        def k_step(ki, _):
            k0 = pl.multiple_of(ki * tq, tq)
            dk_acc[...] = jnp.zeros_like(dk_acc)
            dv_acc[...] = jnp.zeros_like(dv_acc)

            def block(k_lo, k_n, q0, q_n, masked):
                rows = slice(k_lo, k_lo + k_n)
                lane = lax.broadcasted_iota(jnp.int32, (q_n, 2 * V_HEAD), 1)
                vt = v_ref[pl.ds(k0 + k_lo, k_n), :]
                do_pair = do_ref[pl.ds(q0, q_n), :]

                def operands(hh):
                    hs = slice(hh * HEAD_PAD, (hh + 1) * HEAD_PAD)
                    kt = k_ref[pl.ds(k0 + k_lo, k_n), hs]
                    qt = q_ref[pl.ds(q0, q_n), hs]
                    mine = (lane < V_HEAD) if hh == 0 else (lane >= V_HEAD)
                    do_h = jnp.where(mine, do_pair, jnp.zeros((), BF16))
                    return hs, kt, qt, do_h, _dot_nt(kt, qt), _dot_nt(vt, do_h)

                early = [operands(hh) for hh in range(2)] if masked else None
                for hh in range(2):
                    hs, kt, qt, do_h, st, dpt = early[hh] if masked else operands(hh)
                    if masked:
                        krow = lax.broadcasted_iota(jnp.int32, (k_n, q_n), 0)
                        qcol = lax.broadcasted_iota(jnp.int32, (k_n, q_n), 1)
                        st = jnp.where(krow <= qcol, st, -jnp.inf)
                    pt = jnp.exp2(st - lse_ref[hh:hh + 1, pl.ds(q0, q_n)])
                    dst = (pt * (dpt - dd_ref[hh:hh + 1, pl.ds(q0, q_n)])).astype(BF16)
                    dv_acc[rows, :] += _dot(pt.astype(BF16), do_h)
                    dk_acc[rows, hs] += _dot(dst, qt)
                    dq_acc[pl.ds(q0, q_n), hs] += _dot_tn(dst, kt)

            def q_step(qi, _):
                block(0, tq, pl.multiple_of(qi * tq, tq), tq, False)
                return 0

            half = tq // 2
            block(0, half, k0, tq, True)
            block(half, half, pl.multiple_of(k0 + half, half), half, True)
            lax.fori_loop(ki + 1, nq, q_step, 0)
            dk_ref[pl.ds(k0, tq), :] = (dk_acc[...] * LN2).astype(BF16)
            dv_ref[pl.ds(k0, tq), :] = dv_acc[...].astype(BF16)
            return 0

        lax.fori_loop(0, nq, k_step, 0)
        dq_ref[...] = (dq_acc[...] * SOFTMAX_SCALE).astype(BF16)

    qk, vo, st = _pair_specs(seq)
    return pl.pallas_call(
        body, grid=(t // seq, N_HEADS // 2), name="attn_bwd",
        in_specs=[qk, qk, vo, vo, st, st], out_specs=[qk, qk, vo],
        out_shape=[jax.ShapeDtypeStruct((t, QK_PAD), BF16), jax.ShapeDtypeStruct((t, QK_PAD), BF16),
                   jax.ShapeDtypeStruct((t, E_B), BF16)],
        scratch_shapes=[pltpu.VMEM((seq, 2 * HEAD_PAD), F32), pltpu.VMEM((tq, 2 * HEAD_PAD), F32),
                        pltpu.VMEM((tq, 2 * V_HEAD), F32)],
        compiler_params=_params("parallel", "parallel"),
    )(q, k, v, do, lse, dd)


def _mid_bwd(dq, dk, dv, dgate, dx2, x1, cq, ckv, rc, rs1, rs2, w_uq, w_bin, w_uk, w_uv, w_dkv, gq, gc, gk, gb):
    t = dq.shape[0]
    tm = min(ROW_TILE, t)

    def body(dq_ref, dk_ref, dv_ref, dgate_ref, dx2_ref, x1_ref, cq_ref, ckv_ref, c_ref, s1_ref, s2_ref,
             wq_ref, wi_ref, wk_ref, wv_ref, wd_ref, gq_ref, gc_ref, gk_ref, gb_ref,
             dx1_ref, dwq_ref, dwi_ref, dwk_ref, dwv_ref, dwd_ref, dgq_ref, dgc_ref, dgk_ref, dgb_ref):
        @pl.when(pl.program_id(0) == 0)
        def _():
            for ref in (dwq_ref, dwi_ref, dwk_ref, dwv_ref, dwd_ref, dgq_ref, dgc_ref, dgk_ref, dgb_ref):
                ref[...] = jnp.zeros_like(ref)

        cb, s1b, s2b = c_ref[...], s1_ref[...], s2_ref[...]
        gk, gb, gq, gc = gk_ref[...], gb_ref[...], gq_ref[...], gc_ref[...]

        dkb, dvb = dk_ref[...], dv_ref[...]
        dckv = _dot_nt(dkb, wk_ref[...]) + _dot_nt(dvb, wv_ref[...])
        rcv, ch = _rms(ckv_ref[...].astype(F32))
        ckvn = (ch * gc).astype(BF16)
        dwk_ref[...] += _dot_tn(ckvn, dkb)
        dwv_ref[...] += _dot_tn(ckvn, dvb)

        dqs = [_rope_bwd(dq_ref[:, h * HEAD_PAD:(h + 1) * HEAD_PAD].astype(F32), cb, s1b, s2b)
               for h in range(N_HEADS)]
        dqb = jnp.concatenate(dqs, axis=-1).astype(BF16)
        rq, cqh = _rms(cq_ref[...].astype(F32))
        dcqn = _dot_nt(dqb, wq_ref[...])
        dwq_ref[...] += _dot_tn((cqh * gq).astype(BF16), dqb)

        dgc_ref[...] += jnp.sum(dckv * ch, axis=0, keepdims=True)
        dckv_raw = _rms_bwd(dckv * gc, ch, rcv)
        dkr = dk_ref[:, 0:HEAD_PAD].astype(F32)
        for h in range(1, N_HEADS):
            dkr = dkr + dk_ref[:, h * HEAD_PAD:(h + 1) * HEAD_PAD].astype(F32)
        dkr = _rope_bwd(dkr, cb, s1b, s2b)
        dckr = jnp.concatenate([dckv_raw, dkr], axis=-1).astype(BF16)
        r1, xh = _rms(x1_ref[...])
        dhk = _dot_nt(dckr, wd_ref[...])
        dwd_ref[...] += _dot_tn((xh * gk).astype(BF16), dckr)

        dgq_ref[...] += jnp.sum(dcqn * cqh, axis=0, keepdims=True)
        dcq = _rms_bwd(dcqn * gq, cqh, rq)
        dpb = jnp.concatenate([dcq.astype(BF16), dgate_ref[...]], axis=-1)
        dh1 = _dot_nt(dpb, wi_ref[...])
        dwi_ref[...] += _dot_tn((xh * gb).astype(BF16), dpb)

        dgb_ref[...] += jnp.sum(dh1 * xh, axis=0, keepdims=True)
        dgk_ref[...] += jnp.sum(dhk * xh, axis=0, keepdims=True)
        dx1_ref[...] = dx2_ref[...] + _rms_bwd(dh1 * gb + dhk * gk, xh, r1)

    acc_shapes = [(Q_RANK, QK_PAD), (D_MODEL, Q_RANK + E_B), (KV_RANK, QK_PAD), (KV_RANK, E_B), (D_MODEL, KR_PAD),
                  (1, Q_RANK), (1, KV_RANK), (1, D_MODEL), (1, D_MODEL)]
    return pl.pallas_call(
        body, grid=(t // tm,), name="mid_bwd",
        in_specs=[_row_spec(tm, QK_PAD), _row_spec(tm, QK_PAD), _row_spec(tm, E_B), _row_spec(tm, E_B),
                  _row_spec(tm, D_MODEL), _row_spec(tm, D_MODEL), _row_spec(tm, Q_RANK), _row_spec(tm, KV_RANK),
                  _row_spec(tm, HEAD_PAD), _row_spec(tm, HEAD_PAD), _row_spec(tm, HEAD_PAD),
                  _const_spec((Q_RANK, QK_PAD)), _const_spec((D_MODEL, Q_RANK + E_B)),
                  _const_spec((KV_RANK, QK_PAD)), _const_spec((KV_RANK, E_B)), _const_spec((D_MODEL, KR_PAD)),
                  _const_spec((1, Q_RANK)), _const_spec((1, KV_RANK)), _const_spec((1, D_MODEL)),
                  _const_spec((1, D_MODEL))],
        out_specs=[_row_spec(tm, D_MODEL)] + [_acc_spec(s) for s in acc_shapes],
        out_shape=[jax.ShapeDtypeStruct((t, D_MODEL), F32)] + [jax.ShapeDtypeStruct(s, F32) for s in acc_shapes],
        compiler_params=_params("arbitrary"),
    )(dq, dk, dv, dgate, dx2, x1, cq, ckv, rc, rs1, rs2, w_uq, w_bin, w_uk, w_uv, w_dkv, gq, gc, gk, gb)


def _conv_bwd(dx1, x, c, u, v, ym, kb, kcv, kg, seq, w_out, w_in4, ga, cw):
    t = x.shape[0]
    tm = min(CONV_BWD_TILE, seq)
    tiles_per_seq = seq // tm
    n = t // tm

    def tile(i):
        return n - 1 - i

    def rev(width):
        return pl.BlockSpec((tm, width), lambda i: (tile(i), 0))

    def body(dx1_ref, x_ref, c_ref, u_ref, v_ref, ym_ref, kb_ref, kcv_ref, kg_ref, wo_ref, wi_ref, ga_ref, cw_ref,
             dx_ref, dwi_ref, dwo_ref, dcw_ref, dga_ref, carry_ref):
        i = pl.program_id(0)
        j = tile(i)

        @pl.when(i == 0)
        def _():
            for ref in (dwi_ref, dwo_ref, dcw_ref, dga_ref):
                ref[...] = jnp.zeros_like(ref)

        @pl.when(j % tiles_per_seq == tiles_per_seq - 1)
        def _():
            carry_ref[...] = jnp.zeros_like(carry_ref)

        dx1 = dx1_ref[...]
        dx1b = dx1.astype(BF16)
        dym = _dot_nt(dx1b, wo_ref[...])
        dwo_ref[...] += _dot_tn(ym_ref[...], dx1b)
        db = dym * kb_ref[...].astype(F32)
        dcv = dym * kcv_ref[...].astype(F32)
        dg = dym * kg_ref[...].astype(F32)

        row = lax.broadcasted_iota(jnp.int32, (tm, 1), 0)
        w0, w1, w2 = cw_ref[0:1, :], cw_ref[1:2, :], cw_ref[2:3, :]
        nxt = carry_ref[...]
        n0, n1 = nxt[0:1, :], nxt[1:2, :]
        d1 = jnp.where(row == tm - 1, n0, pltpu.roll(dcv, tm - 1, 0))
        d2 = jnp.where(row == tm - 1, n1, jnp.where(row == tm - 2, n0, pltpu.roll(dcv, tm - 2, 0)))
        carry_ref[...] = dcv[0:8, :]
        dv = w2 * dcv + w1 * d1 + w0 * d2
        v = v_ref[...].astype(F32)
        dcw_ref[0:1, :] += jnp.sum(d2 * v, axis=0, keepdims=True)
        dcw_ref[1:2, :] += jnp.sum(d1 * v, axis=0, keepdims=True)
        dcw_ref[2:3, :] += jnp.sum(dcv * v, axis=0, keepdims=True)

        r0, xh = _rms(x_ref[...])
        ga = ga_ref[...]
        h = (xh * ga).astype(BF16)
        dh = jnp.zeros((tm, D_MODEL), F32)
        for idx, dpart in enumerate((db, dv * u_ref[...].astype(F32), dv * c_ref[...].astype(F32), dg)):
            dpb = dpart.astype(BF16)
            dh = dh + _dot_nt(dpb, wi_ref[idx])
            dwi_ref[idx] += _dot_tn(h, dpb)
        dga_ref[...] += jnp.sum(dh * xh, axis=0, keepdims=True)
        dx_ref[...] = dx1 + _rms_bwd(dh * ga, xh, r0)

    acc_shapes = [(4, D_MODEL, E_A), (E_A, D_MODEL), (8, E_A), (1, D_MODEL)]
    return pl.pallas_call(
        body, grid=(n,), name="conv_bwd",
        in_specs=[rev(D_MODEL), rev(D_MODEL)] + [rev(E_A)] * 7
        + [_const_spec((E_A, D_MODEL)), _const_spec((4, D_MODEL, E_A)), _const_spec((1, D_MODEL)),
           _const_spec((8, E_A))],
        out_specs=[rev(D_MODEL)] + [_acc_spec(s) for s in acc_shapes],
        out_shape=[jax.ShapeDtypeStruct((t, D_MODEL), F32)] + [jax.ShapeDtypeStruct(s, F32) for s in acc_shapes],
        scratch_shapes=[pltpu.VMEM((8, E_A), F32)],
        compiler_params=_params("arbitrary"),
    )(dx1, x, c, u, v, ym, kb, kcv, kg, w_out, w_in4, ga, cw)


WEIGHTS = ("a_norm", "a_w_in", "a_conv", "a_w_out", "kv_norm", "w_dkv", "ckv_norm", "w_ukv", "b_norm", "b_w_in",
           "b_q_norm", "b_w_uq", "b_w_out", "final_norm")
SHARD_SHAPES = {
    "a_norm": (1, 256), "a_w_in": (1, 1024, 1024), "a_conv": (1, 3, 256), "a_w_out": (1, 256, 1024),
    "kv_norm": (1024,), "w_dkv": (256, 288), "ckv_norm": (256,), "w_ukv": (256, 256), "b_norm": (1, 1024),
    "b_w_in": (1, 256, 896), "b_q_norm": (1, 384), "b_w_uq": (1, 384, 192), "b_w_out": (1, 512, 256),
    "final_norm": (1024,),
}
MATS = ("a_w_in", "a_w_out", "w_dkv", "w_ukv", "b_w_in", "b_w_uq", "b_w_out")
SMALL = ("a_norm", "a_conv", "kv_norm", "ckv_norm", "b_norm", "b_q_norm", "final_norm")
SMALL_FULL = {"a_norm": 1024, "a_conv": 3072, "kv_norm": 1024, "ckv_norm": 256, "b_norm": 1024, "b_q_norm": 384,
              "final_norm": 1024}
SMALL_ROWS = 8
LOSS_SLOT = sum(SMALL_FULL.values())


def _mat2d(name, a):
    return a.reshape(SHARD_SHAPES[name][-2:])


def _prep_first(g_win, gsmall):
    sm = gsmall.reshape(N_CHIPS, -1)
    a_conv = jnp.transpose(sm[:, 256:1024].reshape(N_CHIPS, CONV_WIDTH, 256), (1, 0, 2)).reshape(CONV_WIDTH, -1)
    return {"w_in4": g_win, "ga": sm[:, :256].reshape(1, -1), "cw": jnp.pad(a_conv, ((0, 8 - CONV_WIDTH), (0, 0)))}


def _prep_rest(gath, w):
    def cols(a):
        return jnp.transpose(a, (1, 0, 2)).reshape(a.shape[1], -1)

    def pad_heads(a, width):
        a = a.reshape(a.shape[0], N_HEADS, width)
        return jnp.pad(a, ((0, 0), (0, 0), (0, HEAD_PAD - width))).reshape(a.shape[0], QK_PAD)

    row = lambda a: a.reshape(1, -1).astype(F32)
    w_dkv = gath["w_dkv"].reshape(D_MODEL, KV_RANK + QK_ROPE)
    w_ukv = cols(gath["w_ukv"]).reshape(KV_RANK, N_HEADS, 2, QK_NOPE)
    return {
        "w_out": gath["a_w_out"].reshape(E_A, D_MODEL),
        "w_dkv": jnp.concatenate([w_dkv[:, :KV_RANK], jnp.zeros((D_MODEL, ROPE_LO), BF16), w_dkv[:, KV_RANK:],
                                  jnp.zeros((D_MODEL, HEAD_PAD - ROPE_LO - QK_ROPE), BF16)], axis=1),
        "w_uk": pad_heads(w_ukv[:, :, 0, :].reshape(KV_RANK, N_HEADS * QK_NOPE), QK_NOPE),
        "w_uv": w_ukv[:, :, 1, :].reshape(KV_RANK, E_B),
        "w_bin": gath["b_w_in"].reshape(D_MODEL, Q_RANK + E_B),
        "w_uq": pad_heads(cols(gath["b_w_uq"]), QK_NOPE + QK_ROPE),
        "w_bout": cols(gath["b_w_out"]),
        "gk": row(w["kv_norm"]), "gc": row(w["ckv_norm"]), "gb": row(w["b_norm"]),
        "gq": row(w["b_q_norm"]), "gf": row(w["final_norm"]),
    }


TRANSPOSED = ("w_dkv", "b_w_uq")
GRAD_MATS = ("w_in4", "w_out", "w_dkv", "w_uk", "w_uv", "w_bin", "w_uq", "w_bout")
GRAD_KIND = {"w_in4": "lead", "w_out": "row", "w_dkv": "row", "w_uk": "col", "w_uv": "col", "w_bin": "row",
             "w_uq": "col", "w_bout": "col"}


def _local_step(x, positions, tgt, w):
    bsz, seq, _ = x.shape
    t = bsz * seq
    x2d = x.reshape(t, D_MODEL)
    small = jnp.concatenate([w["a_norm"].reshape(-1), w["a_conv"].reshape(-1)]).reshape(8, LANES)
    rc, rs1, rs2, g_win, gsmall = _rope_tables_gather(positions.reshape(t, 1), _mat2d(MATS[0], w[MATS[0]]), small)
    wk = _prep_first(g_win, gsmall)
    (c, u, v, ym, kb, kcv, kg), gathered = _conv_fwd(x2d, seq, wk["ga"], wk["w_in4"], wk["cw"],
                                           [_mat2d(n, w[n]) for n in MATS[1:]])
    wk.update(_prep_rest(dict(zip(MATS[1:], gathered)), w))
    x1, q, k, vv, gate, cq, ckv = _mid_fwd(x2d, ym, wk["w_out"], wk["gk"], wk["gb"], wk["w_dkv"], wk["gc"], wk["w_uk"],
                                          wk["w_uv"], wk["w_bin"], wk["gq"], wk["w_uq"], rc, rs1, rs2)
    o, lse = _attn_fwd(q, k, vv, seq)
    dx2, do, dgate, dd, loss, dgf, dw_bout = _head_fwd_bwd(o, gate, x1, tgt.reshape(t, D_MODEL), wk["w_bout"], wk["gf"])
    dq, dk, dv = _attn_bwd(q, k, vv, do, lse, dd, seq)
    dx1, dwq, dw_bin, dwk, dwv, dwd, dgq, dgc, dgk, dgb = _mid_bwd(
        dq, dk, dv, dgate, dx2, x1, cq, ckv, rc, rs1, rs2, wk["w_uq"], wk["w_bin"], wk["w_uk"], wk["w_uv"], wk["w_dkv"],
        wk["gq"], wk["gc"], wk["gk"], wk["gb"])
    dx, dw_in4, dw_out, dcw, dga = _conv_bwd(dx1, x2d, c, u, v, ym, kb, kcv, kg, seq, wk["w_out"], wk["w_in4"], wk["ga"], wk["cw"])
    mats = {"w_in4": dw_in4, "w_out": dw_out, "w_dkv": dwd, "w_uk": dwk, "w_uv": dwv, "w_bin": dw_bin, "w_uq": dwq,
            "w_bout": dw_bout}
    small = {"a_norm": dga, "a_conv": dcw[:CONV_WIDTH], "kv_norm": dgk, "ckv_norm": dgc, "b_norm": dgb,
             "b_q_norm": dgq, "final_norm": dgf}
    return loss[0, 0], dx.reshape(bsz, seq, D_MODEL), mats, small


def _shard_grads(sh, svec):
    j0 = 2 * lax.axis_index("x") + lax.axis_index("y")
    flat = svec.reshape(-1)
    off, small = 0, {}
    for n in SMALL:
        small[n] = flat[off:off + SMALL_FULL[n]]
        off += SMALL_FULL[n]
    dwd, dwk, dwv, dwq = sh["w_dkv"], sh["w_uk"], sh["w_uv"], sh["w_uq"]
    w_ukv = jnp.stack([dwk.reshape(KV_RANK, 2, HEAD_PAD)[:, :, :QK_NOPE], dwv.reshape(KV_RANK, 2, V_HEAD)], axis=2)
    return {
        "a_norm": lax.dynamic_slice(small["a_norm"], (j0 * 256,), (256,)),
        "a_conv": lax.dynamic_slice(small["a_conv"].reshape(CONV_WIDTH, E_A), (0, j0 * 256), (CONV_WIDTH, 256)),
        "kv_norm": small["kv_norm"], "ckv_norm": small["ckv_norm"], "b_norm": small["b_norm"],
        "b_q_norm": small["b_q_norm"], "final_norm": small["final_norm"],
        "a_w_in": sh["w_in4"], "a_w_out": sh["w_out"],
        "w_dkv": jnp.concatenate([dwd[:, :KV_RANK], dwd[:, KV_RANK + ROPE_LO:KV_RANK + ROPE_LO + QK_ROPE]], axis=1),
        "w_ukv": w_ukv.reshape(KV_RANK, 2 * (QK_NOPE + V_HEAD)),
        "b_w_in": sh["w_bin"],
        "b_w_uq": dwq.reshape(Q_RANK, 2, HEAD_PAD)[:, :, :QK_NOPE + QK_ROPE].reshape(Q_RANK, -1),
        "b_w_out": sh["w_bout"],
    }


def _sub(ref, kind, j, cc):
    if kind == "lead":
        h = ref.shape[1] // 2
        return ref.at[j, pl.ds(pl.multiple_of(cc * h, 8), h), :]
    if kind == "row":
        rows = ref.shape[0] // N_CHIPS
        h = rows // 2
        return ref.at[pl.ds(pl.multiple_of(j * rows + cc * h, 8), h), :]
    cols = ref.shape[1] // N_CHIPS
    h = ref.shape[0] // 2
    return ref.at[pl.ds(pl.multiple_of(cc * h, 8), h), pl.ds(j * cols, cols)]


def _sub_shape(shape, kind):
    if kind == "lead":
        return (shape[1] // 2, shape[2])
    if kind == "row":
        return (shape[0] // N_CHIPS // 2, shape[1])
    return (shape[0] // 2, shape[1] // N_CHIPS)


def _reduce_grads(grads, kinds, vec):
    n = len(grads)
    shapes = [_sub_shape(a.shape, kd) for a, kd in zip(grads, kinds)]
    units = [(k, j) for k in range(n) for j in range(N_CHIPS)]
    big = (max(s[0] for s in shapes), max(s[1] for s in shapes))

    def body(*refs):
        g, v_ref = refs[:n], refs[n]
        out, o_ref = refs[n + 1:2 * n + 1], refs[2 * n + 1]
        theirs, part, recd, red = (refs[(2 + i) * n + 2:(3 + i) * n + 2] for i in range(4))
        mine, got = refs[6 * n + 2], refs[6 * n + 3]
        send1, recv1, send3, recv3, send5, recv5, load, local, send_v, recv_v = refs[6 * n + 4:]
        x, y, c = _place()
        j0 = 2 * x + y
        me = 2 * j0 + c
        sibling = (x, y, 1 - c)

        got[me] = v_ref[...]
        first, small_in = [], []
        for d in range(1, N_DEV):
            px, py, pc = x ^ (d >> 2), y ^ ((d >> 1) & 1), c ^ (d & 1)
            first.append(_remote(v_ref, got.at[me], send_v.at[d - 1], recv_v.at[d - 1], (px, py, pc)))
            small_in.append(_remote(v_ref, got.at[4 * px + 2 * py + pc], send_v.at[d - 1], recv_v.at[d - 1],
                                    (px, py, pc)))
        halves = [_remote(_sub(g[k], kinds[k], j, 1 - c), theirs[k].at[j], send1.at[u], recv1.at[u], sibling)
                  for u, (k, j) in enumerate(units)]
        for cp in first + halves:
            cp.start()

        def mine_load(u):
            k, j = units[u]
            h, cols = shapes[k]
            return pltpu.make_async_copy(_sub(g[k], kinds[k], j, c), mine.at[u % 2, pl.ds(0, h), pl.ds(0, cols)],
                                         load.at[u % 2])

        mine_load(0).start()
        for u, (k, j) in enumerate(units):
            h, cols = shapes[k]
            if u + 1 < len(units):
                mine_load(u + 1).start()
            mine_load(u).wait()
            halves[u].wait_recv()
            part[k][j] = (mine[u % 2, 0:h, 0:cols] + theirs[k][j]).astype(BF16)
            to_owner = _remote(part[k].at[j], recd[k].at[j0], send3.at[u], recv3.at[4 * k + j0], (j // 2, j % 2, c))

            @pl.when(j != j0)
            def _():
                to_owner.start()

            @pl.when(j == j0)
            def _():
                recd[k][j] = part[k][j]

        swaps = []
        for k in range(n):
            for j in range(N_CHIPS):
                arrived = _remote(part[k].at[j], recd[k].at[j], send3.at[4 * k + j], recv3.at[4 * k + j],
                                  (j // 2, j % 2, c))

                @pl.when(j != j0)
                def _():
                    arrived.wait_recv()

            r = recd[k]
            red[k][...] = ((r[0].astype(F32) + r[1].astype(F32)) + r[2].astype(F32)) + r[3].astype(F32)
            own = pltpu.make_async_copy(red[k], out[k].at[c], local.at[k])
            give = _remote(red[k], out[k].at[c], send5.at[k], recv5.at[k], sibling)
            take = _remote(red[k], out[k].at[1 - c], send5.at[k], recv5.at[k], sibling)
            own.start()
            give.start()
            swaps.append((own, give, take))

        for cp in small_in:
            cp.wait_recv()
            cp.wait_send()
        acc = got[0]
        for d in range(1, N_DEV):
            acc = acc + got[d]
        o_ref[...] = acc
        for u, (k, j) in enumerate(units):
            halves[u].wait_send()
            sent = _remote(part[k].at[j], recd[k].at[j0], send3.at[u], recv3.at[u], (j // 2, j % 2, c))

            @pl.when(j != j0)
            def _():
                sent.wait_send()
        for own, give, take in swaps:
            take.wait_recv()
            give.wait_send()
            own.wait()

    sems = [pltpu.SemaphoreType.DMA((len(units),))] * 4 + [pltpu.SemaphoreType.DMA((n,))] * 2 \
        + [pltpu.SemaphoreType.DMA((2,)), pltpu.SemaphoreType.DMA((n,))] + [pltpu.SemaphoreType.DMA((N_DEV - 1,))] * 2
    res = pl.pallas_call(
        body, name="reduce_grads",
        in_specs=[HBM] * n + [WHOLE], out_specs=[HBM] * n + [WHOLE],
        out_shape=[jax.ShapeDtypeStruct((2,) + s, F32) for s in shapes] + [jax.ShapeDtypeStruct(vec.shape, vec.dtype)],
        scratch_shapes=[pltpu.VMEM((N_CHIPS,) + s, F32) for s in shapes]
        + [pltpu.VMEM((N_CHIPS,) + s, BF16) for s in shapes] * 2
        + [pltpu.VMEM(s, F32) for s in shapes]
        + [pltpu.VMEM((2,) + big, F32), pltpu.VMEM((N_DEV,) + vec.shape, vec.dtype)] + sems,
        compiler_params=_comm_params(),
    )(*grads, vec)
    return res[:n], res[n]


def _adamw_math(w, g, m, v):
    m = ADAM_B1 * m + (1.0 - ADAM_B1) * g
    v = ADAM_B2 * v + (1.0 - ADAM_B2) * (g * g)
    m_hat = m / (1.0 - ADAM_B1 ** ADAM_STEP)
    v_hat = v / (1.0 - ADAM_B2 ** ADAM_STEP)
    return -ADAM_LR * (m_hat / (jnp.sqrt(v_hat) + ADAM_EPS) + ADAM_WD * w), m, v


def _adamw_tiled(w, g, m, v):
    rows, width = w.shape
    tm = rows // 4

    def body(w_ref, g_ref, m_ref, v_ref, go_ref, d_ref, mo_ref, vo_ref):
        g = g_ref[...]
        go_ref[...] = g
        d_ref[...], mo_ref[...], vo_ref[...] = _adamw_math(w_ref[...], g, m_ref[...], v_ref[...])

    spec = pl.BlockSpec((tm, width), lambda i: (i, 0))
    out = jax.ShapeDtypeStruct((rows, width), F32)
    return pl.pallas_call(
        body, grid=(rows // tm,), name="adamw_tiled",
        in_specs=[spec] * 4, out_specs=[spec] * 4, out_shape=[out] * 4,
        compiler_params=_params("parallel"),
    )(w, g, m, v)


def _adamw_many(ws, gs, ms, vs):
    n = len(ws)

    def body(*refs):
        for k in range(n):
            w_ref, g_ref, m_ref, v_ref = (refs[i * n + k] for i in range(4))
            go_ref, d_ref, mo_ref, vo_ref = (refs[(4 + i) * n + k] for i in range(4))
            g = g_ref[...]
            go_ref[...] = g
            d_ref[...], mo_ref[...], vo_ref[...] = _adamw_math(w_ref[...], g, m_ref[...], v_ref[...])

    outs = [jax.ShapeDtypeStruct(a.shape, F32) for a in ws]
    res = pl.pallas_call(
        body, name="adamw_many",
        in_specs=[WHOLE] * (4 * n), out_specs=[WHOLE] * (4 * n), out_shape=outs * 4,
        compiler_params=_comm_params(),
    )(*ws, *gs, *ms, *vs)
    return res[:n], res[n:2 * n], res[2 * n:3 * n], res[3 * n:]


def kernel(x, positions, a_norm, a_w_in, a_conv, a_w_out, kv_norm, w_dkv, ckv_norm, w_ukv, b_norm, b_w_in, b_q_norm, b_w_uq, b_w_out, final_norm, loss_target, m_a_norm, m_a_w_in, m_a_conv, m_a_w_out, m_kv_norm, m_w_dkv, m_ckv_norm, m_w_ukv, m_b_norm, m_b_w_in, m_b_q_norm, m_b_w_uq, m_b_w_out, m_final_norm, v_a_norm, v_a_w_in, v_a_conv, v_a_w_out, v_kv_norm, v_w_dkv, v_ckv_norm, v_w_ukv, v_b_norm, v_b_w_in, v_b_q_norm, v_b_w_uq, v_b_w_out, v_final_norm):
    w = dict(a_norm=a_norm, a_w_in=a_w_in, a_conv=a_conv, a_w_out=a_w_out, kv_norm=kv_norm, w_dkv=w_dkv,
             ckv_norm=ckv_norm, w_ukv=w_ukv, b_norm=b_norm, b_w_in=b_w_in, b_q_norm=b_q_norm, b_w_uq=b_w_uq,
             b_w_out=b_w_out, final_norm=final_norm)
    m = dict(a_norm=m_a_norm, a_w_in=m_a_w_in, a_conv=m_a_conv, a_w_out=m_a_w_out, kv_norm=m_kv_norm, w_dkv=m_w_dkv,
             ckv_norm=m_ckv_norm, w_ukv=m_w_ukv, b_norm=m_b_norm, b_w_in=m_b_w_in, b_q_norm=m_b_q_norm,
             b_w_uq=m_b_w_uq, b_w_out=m_b_w_out, final_norm=m_final_norm)
    v = dict(a_norm=v_a_norm, a_w_in=v_a_w_in, a_conv=v_a_conv, a_w_out=v_a_w_out, kv_norm=v_kv_norm, w_dkv=v_w_dkv,
             ckv_norm=v_ckv_norm, w_ukv=v_w_ukv, b_norm=v_b_norm, b_w_in=v_b_w_in, b_q_norm=v_b_q_norm,
             b_w_uq=v_b_w_uq, b_w_out=v_b_w_out, final_norm=v_final_norm)

    loss, dx, gmat, gsmall = _local_step(x, positions, loss_target, w)

    kinds = [GRAD_KIND[n] for n in GRAD_MATS]
    grads = [gmat[n] for n in GRAD_MATS]
    flat = jnp.concatenate([gsmall[n].reshape(-1) for n in SMALL] + [loss.reshape(1)])
    flat = jnp.pad(flat, (0, SMALL_ROWS * PACK_W - flat.shape[0])).reshape(SMALL_ROWS, PACK_W)
    mine, svec = _reduce_grads(grads, kinds, flat)
    loss = svec.reshape(-1)[LOSS_SLOT]
    g = _shard_grads({n: a.reshape(-1, a.shape[-1]) for n, a in zip(GRAD_MATS, mine)}, svec)

    def two_d(n, a):
        a = a.reshape(-1, a.shape[-1])
        return a.T if n in TRANSPOSED else a

    def back(n, a):
        return (a.T if n in TRANSPOSED else a).reshape(SHARD_SHAPES[n])

    big = "a_w_in"
    rest = [n for n in WEIGHTS if n != big]
    res_big = _adamw_tiled(*(two_d(big, t[big]) for t in (w, g, m, v)))
    res_rest = _adamw_many(*([two_d(n, t[n]) for n in rest] for t in (w, g, m, v)))
    out = {kind: dict(zip(rest, res_rest[i])) for i, kind in enumerate("gdmv")}
    for i, kind in enumerate("gdmv"):
        out[kind][big] = res_big[i]
    return (loss, dx) + tuple(back(n, out[kind][n]) for kind in "gdmv" for n in WEIGHTS)
```

```python
import functools
import math

import numpy as np
import jax
import jax.numpy as jnp
from jax import lax
from jax.experimental import pallas as pl
from jax.experimental.pallas import tpu as pltpu

F32 = jnp.float32
BF16 = jnp.bfloat16

D_MODEL = 1024
E_A = 1024
CONV_WIDTH = 3
N_HEADS = 8
QK_NOPE = 64
QK_ROPE = 32
V_HEAD = 64
KV_RANK = 256
Q_RANK = 384
E_B = N_HEADS * V_HEAD
ROPE_THETA = 10000.0
SOFTMAX_SCALE = 1.0 / math.sqrt(QK_NOPE + QK_ROPE)
LOG2E = math.log2(math.e)
LN2 = math.log(2.0)
Q_PRESCALE = SOFTMAX_SCALE * LOG2E
EPS = 1e-6
HEAD_PAD = 128
QK_PAD = N_HEADS * HEAD_PAD
ROPE_LO = QK_NOPE
ROPE_HALF = QK_ROPE // 2
KR_PAD = KV_RANK + HEAD_PAD

ADAM_LR = 0.001
ADAM_B1 = 0.9
ADAM_B2 = 0.999
ADAM_EPS = 1e-08
ADAM_WD = 0.01
ADAM_STEP = 10

VMEM_LIMIT = 56 * 1024 * 1024
ROW_TILE = 512
CONV_BWD_TILE = 256
ATT_TILE_FWD = 1024
ATT_TILE = 512
HEAD_CHAINS = 2
LANES = 128
PACK_W = 1024

N_CHIPS = 4
N_DEV = 8


def _dot(a, b):
    return jnp.dot(a, b, preferred_element_type=F32)


def _dot_nt(a, b):
    return lax.dot_general(a, b, (((1,), (1,)), ((), ())), preferred_element_type=F32)


def _dot_tn(a, b):
    return lax.dot_general(a, b, (((0,), (0,)), ((), ())), preferred_element_type=F32)


def _rms(x):
    r = lax.rsqrt(jnp.mean(x * x, axis=-1, keepdims=True) + EPS)
    return r, x * r


def _rms_bwd(dxh, xh, r):
    return r * (dxh - xh * jnp.mean(dxh * xh, axis=-1, keepdims=True))


def _rope_fwd(a, c, s1, s2):
    return a * c + pltpu.roll(a, HEAD_PAD - ROPE_HALF, 1) * s1 + pltpu.roll(a, ROPE_HALF, 1) * s2


def _rope_bwd(g, c, s1, s2):
    return g * c + pltpu.roll(g * s1, ROPE_HALF, 1) + pltpu.roll(g * s2, HEAD_PAD - ROPE_HALF, 1)


def _sigmoid(x):
    return 1.0 / (1.0 + jnp.exp(-x))


def _row_spec(tm, n):
    return pl.BlockSpec((tm, n), lambda i: (i, 0))


def _const_spec(shape):
    nd = len(shape)
    return pl.BlockSpec(shape, lambda i: (0,) * nd, pipeline_mode=pl.Buffered(1))


def _acc_spec(shape):
    nd = len(shape)
    return pl.BlockSpec(shape, lambda i: (0,) * nd, pipeline_mode=pl.Buffered(1))


def _params(*sem):
    return pltpu.CompilerParams(dimension_semantics=sem, vmem_limit_bytes=VMEM_LIMIT)


MESH = pl.DeviceIdType.MESH
HBM = pl.BlockSpec(memory_space=pl.ANY)
WHOLE = pl.BlockSpec(memory_space=pltpu.VMEM)
FLIPS = ((1, 0), (0, 1), (1, 1))


def _place():
    return lax.axis_index("x"), lax.axis_index("y"), lax.axis_index("c")


def _remote(src, dst, send, recv, peer):
    return pltpu.make_async_remote_copy(src_ref=src, dst_ref=dst, send_sem=send, recv_sem=recv, device_id=peer,
                                        device_id_type=MESH)


def _comm_params():
    return pltpu.CompilerParams(vmem_limit_bytes=VMEM_LIMIT)


def _rope_consts():
    lane = np.arange(HEAD_PAD)
    first = (lane >= ROPE_LO) & (lane < ROPE_LO + ROPE_HALF)
    second = (lane >= ROPE_LO + ROPE_HALF) & (lane < ROPE_LO + QK_ROPE)
    f = np.where(first, lane - ROPE_LO, np.where(second, lane - ROPE_LO - ROPE_HALF, 0))
    inv = np.float32(ROPE_THETA) ** (-(2 * f).astype(np.float32) / np.float32(QK_ROPE))
    out = np.zeros((8, HEAD_PAD), np.float32)
    out[0] = inv
    out[1] = first
    out[2] = second
    out[3] = lane < ROPE_LO
    return jnp.asarray(out)


def _gather_plan(w, sm, outs, osm, bf, sems):
    n = len(w)
    send_i, recv_i, send_d, recv_d, send_s, recv_s, local = sems
    x, y, c = _place()
    j0 = 2 * x + y
    sibling = (x, y, 1 - c)
    own = [pltpu.make_async_copy(bf[k], outs[k].at[j0], local.at[k]) for k in range(n)]
    if sm is not None:
        own.append(pltpu.make_async_copy(sm, osm.at[j0], local.at[n]))

    def half(k, cc):
        h = w[k].shape[0] // 2
        return pl.ds(pl.multiple_of(cc * h, 16), h)

    sends, arrivals, forwards, fwaits = [], [], [], []
    for i, (fx, fy) in enumerate(FLIPS):
        px, py = x ^ fx, y ^ fy
        pj = 2 * px + py
        for k in range(n):
            s = i * n + k
            sends.append(_remote(bf[k].at[half(k, c)], outs[k].at[j0, half(k, c)], send_i.at[s], recv_i.at[s],
                                 (px, py, c)))
            arrivals.append(_remote(bf[k].at[half(k, c)], outs[k].at[pj, half(k, c)], send_i.at[s], recv_i.at[s],
                                    (px, py, c)))
            forwards.append(_remote(outs[k].at[pj, half(k, c)], outs[k].at[pj, half(k, c)], send_d.at[s],
                                    recv_d.at[s], sibling))
            fwaits.append(_remote(outs[k].at[pj, half(k, 1 - c)], outs[k].at[pj, half(k, 1 - c)], send_d.at[s],
                                  recv_d.at[s], sibling))
        if sm is not None:
            sends.append(_remote(sm, osm.at[j0], send_s.at[i], recv_s.at[i], (px, py, c)))
            fwaits.append(_remote(sm, osm.at[pj], send_s.at[i], recv_s.at[i], (px, py, c)))
    return own, sends, arrivals, forwards, fwaits


def _gather_begin(w, bf, plan):
    own, sends, _, _, _ = plan
    for k in range(len(w)):
        bf[k][...] = w[k][...].astype(BF16)
    for cp in own + sends:
        cp.start()


def _gather_end(plan):
    own, sends, arrivals, forwards, fwaits = plan
    for cp, fwd in zip(arrivals, forwards):
        cp.wait_recv()
        fwd.start()
    for cp in fwaits:
        cp.wait_recv()
    for cp in sends + forwards:
        cp.wait_send()
    for cp in own:
        cp.wait()


def _gather_sems(n, with_small):
    return ([pltpu.SemaphoreType.DMA((3 * n,))] * 4 + [pltpu.SemaphoreType.DMA((3,))] * 2
            + [pltpu.SemaphoreType.DMA((n + (1 if with_small else 0),))])


def _rope_tables_gather(pos_col, w_in, small):
    t = pos_col.shape[0]
    tm = min(ROW_TILE, t)
    steps = t // tm

    def body(p_ref, k_ref, w_ref, sm_ref, c_ref, s1_ref, s2_ref, out_ref, osm_ref, bf_ref, *sems):
        plan = _gather_plan([w_ref], sm_ref, [out_ref], osm_ref, [bf_ref], sems)
        i = pl.program_id(0)

        @pl.when(i == 0)
        def _():
            _gather_begin([w_ref], [bf_ref], plan)

        inv, first, second, nope = k_ref[0:1, :], k_ref[1:2, :], k_ref[2:3, :], k_ref[3:4, :]
        ang = p_ref[...].astype(F32) * inv
        cs, sn = jnp.cos(ang), jnp.sin(ang)
        c_ref[...] = cs * (first + second) + nope
        s1_ref[...] = -sn * first
        s2_ref[...] = sn * second

        @pl.when(i == steps - 1)
        def _():
            _gather_end(plan)

    out = jax.ShapeDtypeStruct((t, HEAD_PAD), F32)
    return pl.pallas_call(
        body, grid=(steps,), name="rope_tables_gather",
        in_specs=[_row_spec(tm, 1), _const_spec((8, HEAD_PAD)), WHOLE, WHOLE],
        out_specs=[_row_spec(tm, HEAD_PAD)] * 3 + [HBM, HBM],
        out_shape=[out] * 3 + [jax.ShapeDtypeStruct((N_CHIPS,) + w_in.shape, BF16),
                               jax.ShapeDtypeStruct((N_CHIPS,) + small.shape, small.dtype)],
        scratch_shapes=[pltpu.VMEM(w_in.shape, BF16)] + _gather_sems(1, True),
        compiler_params=_params("arbitrary"),
    )(pos_col, _rope_consts(), w_in, small)


def _shift_down(v, prev, row):
    p1, p2 = prev[7:8, :], prev[6:7, :]
    v1 = jnp.where(row == 0, p1, pltpu.roll(v, 1, 0))
    v2 = jnp.where(row == 0, p2, jnp.where(row == 1, p1, pltpu.roll(v, 2, 0)))
    return v1, v2


def _conv_fwd(x, seq, ga, w_in4, cw, shards):
    t = x.shape[0]
    tm = min(ROW_TILE, seq)
    tiles_per_seq = seq // tm
    steps = t // tm
    n = len(shards)

    def body(x_ref, ga_ref, w_ref, cw_ref, *rest):
        sh, (c_ref, u_ref, v_ref, ym_ref, kb_ref, kcv_ref, kg_ref) = rest[:n], rest[n:n + 7]
        outs, carry_ref, bf, sems = rest[n + 7:2 * n + 7], rest[2 * n + 7], rest[2 * n + 8:3 * n + 8], rest[3 * n + 8:]
        plan = _gather_plan(sh, None, outs, None, bf, sems)
        i = pl.program_id(0)

        @pl.when(i == 0)
        def _():
            _gather_begin(sh, bf, plan)

        @pl.when(i % tiles_per_seq == 0)
        def _():
            carry_ref[...] = jnp.zeros_like(carry_ref)

        _, xh = _rms(x_ref[...])
        h = (xh * ga_ref[...]).astype(BF16)
        c, u = _dot(h, w_ref[1]), _dot(h, w_ref[2])
        g, b = _dot(h, w_ref[3]), _dot(h, w_ref[0])
        v = c * u
        row = lax.broadcasted_iota(jnp.int32, (tm, 1), 0)
        v1, v2 = _shift_down(v, carry_ref[...], row)
        carry_ref[...] = v[tm - 8:tm, :]
        cv = cw_ref[2:3, :] * v + cw_ref[1:2, :] * v1 + cw_ref[0:1, :] * v2
        sg = _sigmoid(g)
        silu = g * sg
        c_ref[...] = c.astype(BF16)
        u_ref[...] = u.astype(BF16)
        v_ref[...] = v.astype(BF16)
        ym_ref[...] = (silu * b * cv).astype(BF16)
        kb_ref[...] = (silu * cv).astype(BF16)
        kcv_ref[...] = (silu * b).astype(BF16)
        kg_ref[...] = (b * cv * (sg * (1.0 + g * (1.0 - sg)))).astype(BF16)

        @pl.when(i == steps - 1)
        def _():
            _gather_end(plan)

    out = jax.ShapeDtypeStruct((t, E_A), BF16)
    res = pl.pallas_call(
        body, grid=(steps,), name="conv_fwd",
        in_specs=[_row_spec(tm, D_MODEL), _const_spec((1, D_MODEL)), _const_spec((4, D_MODEL, E_A)),
                  _const_spec((8, E_A))] + [WHOLE] * n,
        out_specs=[_row_spec(tm, E_A)] * 7 + [HBM] * n,
        out_shape=[out] * 7 + [jax.ShapeDtypeStruct((N_CHIPS,) + a.shape, BF16) for a in shards],
        scratch_shapes=[pltpu.VMEM((8, E_A), F32)] + [pltpu.VMEM(a.shape, BF16) for a in shards]
        + _gather_sems(n, False),
        compiler_params=_params("arbitrary"),
    )(x, ga, w_in4, cw, *shards)
    return res[:7], res[7:]


def _mid_fwd(x, ym, w_out, gk, gb, w_dkv, gc, w_uk, w_uv, w_bin, gq, w_uq, rc, rs1, rs2):
    t = x.shape[0]
    tm = min(ROW_TILE, t)

    def body(x_ref, ym_ref, wo_ref, gk_ref, gb_ref, wd_ref, gc_ref, wk_ref, wv_ref, wi_ref, gq_ref, wq_ref,
             c_ref, s1_ref, s2_ref, x1_ref, q_ref, k_ref, v_ref, gate_ref, cq_ref, ckv_ref):
        cb, s1b, s2b = c_ref[...], s1_ref[...], s2_ref[...]
        x1 = x_ref[...] + _dot(ym_ref[...], wo_ref[...])
        x1_ref[...] = x1
        _, xh = _rms(x1)
        hk = (xh * gk_ref[...]).astype(BF16)
        h1 = (xh * gb_ref[...]).astype(BF16)

        pb = _dot(h1, wi_ref[...])
        cq = pb[:, :Q_RANK]
        cq_ref[...] = cq.astype(BF16)
        gate_ref[...] = pb[:, Q_RANK:].astype(BF16)
        _, cqh = _rms(cq)
        q = _dot((cqh * gq_ref[...]).astype(BF16), wq_ref[...])
        for h in range(N_HEADS):
            sl = slice(h * HEAD_PAD, (h + 1) * HEAD_PAD)
            q_ref[:, sl] = (_rope_fwd(q[:, sl], cb, s1b, s2b) * Q_PRESCALE).astype(BF16)

        ckr = _dot(hk, wd_ref[...])
        ckv_raw = ckr[:, :KV_RANK]
        ckv_ref[...] = ckv_raw.astype(BF16)
        _, ch = _rms(ckv_raw)
        ckv = (ch * gc_ref[...]).astype(BF16)
        kr = _rope_fwd(ckr[:, KV_RANK:], cb, s1b, s2b)
        kn = _dot(ckv, wk_ref[...])
        for h in range(N_HEADS):
            sl = slice(h * HEAD_PAD, (h + 1) * HEAD_PAD)
            k_ref[:, sl] = (kn[:, sl] + kr).astype(BF16)
        v_ref[...] = _dot(ckv, wv_ref[...]).astype(BF16)

    def sds(n, dt):
        return jax.ShapeDtypeStruct((t, n), dt)

    return pl.pallas_call(
        body, grid=(t // tm,), name="mid_fwd",
        in_specs=[_row_spec(tm, D_MODEL), _row_spec(tm, E_A), _const_spec((E_A, D_MODEL)),
                  _const_spec((1, D_MODEL)), _const_spec((1, D_MODEL)), _const_spec((D_MODEL, KR_PAD)),
                  _const_spec((1, KV_RANK)), _const_spec((KV_RANK, QK_PAD)), _const_spec((KV_RANK, E_B)),
                  _const_spec((D_MODEL, Q_RANK + E_B)), _const_spec((1, Q_RANK)), _const_spec((Q_RANK, QK_PAD)),
                  _row_spec(tm, HEAD_PAD), _row_spec(tm, HEAD_PAD), _row_spec(tm, HEAD_PAD)],
        out_specs=[_row_spec(tm, D_MODEL), _row_spec(tm, QK_PAD), _row_spec(tm, QK_PAD), _row_spec(tm, E_B),
                   _row_spec(tm, E_B), _row_spec(tm, Q_RANK), _row_spec(tm, KV_RANK)],
        out_shape=[sds(D_MODEL, F32), sds(QK_PAD, BF16), sds(QK_PAD, BF16), sds(E_B, BF16), sds(E_B, BF16),
                   sds(Q_RANK, BF16), sds(KV_RANK, BF16)],
        compiler_params=_params("parallel"),
    )(x, ym, w_out, gk, gb, w_dkv, gc, w_uk, w_uv, w_bin, gq, w_uq, rc, rs1, rs2)


def _pair_specs(seq):
    qk = pl.BlockSpec((seq, 2 * HEAD_PAD), lambda b, p: (b, p))
    vo = pl.BlockSpec((seq, 2 * V_HEAD), lambda b, p: (b, p))
    st = pl.BlockSpec((None, 2, seq), lambda b, p: (p, 0, b))
    return qk, vo, st


def _attn_fwd(q, k, v, seq):
    t = q.shape[0]
    tq = min(ATT_TILE_FWD, seq)
    nq = seq // tq

    def body(q_ref, k_ref, v_ref, o_ref, lse_ref, m_scr, l_scr, acc_scr):
        lane = lax.broadcasted_iota(jnp.int32, (tq, 2 * V_HEAD), 1)

        def q_step(qi, _):
            q0 = pl.multiple_of(qi * tq, tq)
            m_scr[...] = jnp.full(m_scr.shape, -jnp.inf, F32)
            l_scr[...] = jnp.zeros_like(l_scr)
            acc_scr[...] = jnp.zeros_like(acc_scr)

            def block(q_lo, q_n, k0, k_n, masked):
                rows = slice(q_lo, q_lo + q_n)
                vt = v_ref[pl.ds(k0, k_n), :]

                def score(hh):
                    hs = slice(hh * HEAD_PAD, (hh + 1) * HEAD_PAD)
                    return _dot_nt(q_ref[pl.ds(q0 + q_lo, q_n), hs], k_ref[pl.ds(k0, k_n), hs])

                early = [score(hh) for hh in range(2)] if masked else None
                for hh in range(2):
                    s = early[hh] if masked else score(hh)
                    if masked:
                        row = lax.broadcasted_iota(jnp.int32, (q_n, k_n), 0)
                        col = lax.broadcasted_iota(jnp.int32, (q_n, k_n), 1)
                        s = jnp.where(col <= row, s, -jnp.inf)
                    m_old = m_scr[hh, rows]
                    m_new = jnp.maximum(m_old, jnp.max(s, axis=-1, keepdims=True))
                    alpha = jnp.exp2(m_old - m_new)
                    ps = [jnp.exp2(s[:, j * LANES:(j + 1) * LANES] - m_new) for j in range(k_n // LANES)]
                    l_scr[hh, rows] = alpha * l_scr[hh, rows] + functools.reduce(lambda a, b: a + b, ps)
                    p = jnp.concatenate(ps, axis=-1).astype(BF16)
                    acc_scr[hh, rows] = alpha * acc_scr[hh, rows] + _dot(p, vt)
                    m_scr[hh, rows] = m_new

            def k_step(ki, _):
                block(0, tq, pl.multiple_of(ki * tq, tq), tq, False)
                return 0

            lax.fori_loop(0, qi, k_step, 0)
            half = tq // 2
            block(0, tq, q0, half, True)
            block(half, half, q0 + half, half, True)
            l0 = jnp.sum(l_scr[0], axis=-1, keepdims=True)
            l1 = jnp.sum(l_scr[1], axis=-1, keepdims=True)
            o_ref[pl.ds(q0, tq), :] = jnp.where(lane < V_HEAD, acc_scr[0] / l0, acc_scr[1] / l1).astype(BF16)
            stats = jnp.where(lane == 0, m_scr[0] + jnp.log2(l0), m_scr[1] + jnp.log2(l1)).T
            lse_ref[:, pl.ds(q0, tq)] = stats[0:2, :]
            return 0

        lax.fori_loop(0, nq, q_step, 0)

    qk, vo, st = _pair_specs(seq)
    return pl.pallas_call(
        body, grid=(t // seq, N_HEADS // 2), name="attn_fwd",
        in_specs=[qk, qk, vo], out_specs=[vo, st],
        out_shape=[jax.ShapeDtypeStruct((t, E_B), BF16), jax.ShapeDtypeStruct((N_HEADS // 2, 2, t), F32)],
        scratch_shapes=[pltpu.VMEM((2, tq, LANES), F32), pltpu.VMEM((2, tq, LANES), F32),
                        pltpu.VMEM((2, tq, 2 * V_HEAD), F32)],
        compiler_params=_params("parallel", "parallel"),
    )(q, k, v)


def _head_fwd_bwd(o, gate, x1, tgt, w_bout, gf):
    t = o.shape[0]
    tm = min(ROW_TILE, t)

    def body(o_ref, gate_ref, x1_ref, tgt_ref, w_ref, gf_ref,
             dx2_ref, do_ref, dgate_ref, dd_ref, loss_ref, dgf_ref, dw_ref):
        @pl.when(pl.program_id(0) == 0)
        def _():
            loss_ref[...] = jnp.zeros_like(loss_ref)
            dgf_ref[...] = jnp.zeros_like(dgf_ref)
            dw_ref[...] = jnp.zeros_like(dw_ref)

        hm = tm // HEAD_CHAINS
        gf = gf_ref[...]
        lane = lax.broadcasted_iota(jnp.int32, (hm, 2 * V_HEAD), 1)
        chains = []
        for ch in range(HEAD_CHAINS):
            rs = pl.ds(ch * hm, hm)
            o = o_ref[rs, :].astype(F32)
            gt = gate_ref[rs, :].astype(F32)
            sg = _sigmoid(gt)
            silu = gt * sg
            z = (o * silu).astype(BF16)
            chains.append((rs, o, gt, sg, silu, z, x1_ref[rs, :] + _dot(z, w_ref[...])))
        mids = []
        for rs, o, gt, sg, silu, z, x2 in chains:
            r2, xh2 = _rms(x2)
            err = xh2 * gf - tgt_ref[rs, :]
            loss_ref[...] += 0.5 * jnp.sum(jnp.mean(err * err, axis=-1, keepdims=True))
            dy = err * (1.0 / D_MODEL)
            dgf_ref[...] += jnp.sum(dy * xh2, axis=0, keepdims=True)
            dx2 = _rms_bwd(dy * gf, xh2, r2)
            dx2_ref[rs, :] = dx2
            dx2b = dx2.astype(BF16)
            mids.append(_dot_nt(dx2b, w_ref[...]))
            dw_ref[...] += _dot_tn(z, dx2b)
        for (rs, o, gt, sg, silu, z, x2), dz in zip(chains, mids):
            do = dz * silu
            do_ref[rs, :] = do.astype(BF16)
            dgate_ref[rs, :] = (dz * o * (sg * (1.0 + gt * (1.0 - sg)))).astype(BF16)
            prod = do * o
            cols = jnp.zeros((hm, LANES), F32)
            for p in range(N_HEADS // 2):
                blk = prod[:, p * 2 * V_HEAD:(p + 1) * 2 * V_HEAD]
                d0 = jnp.sum(jnp.where(lane < V_HEAD, blk, 0.0), axis=-1, keepdims=True)
                d1 = jnp.sum(jnp.where(lane < V_HEAD, 0.0, blk), axis=-1, keepdims=True)
                cols = jnp.where(lane == 2 * p, d0, jnp.where(lane == 2 * p + 1, d1, cols))
            rows = cols.T
            for h in range(N_HEADS):
                dd_ref[h // 2, h % 2:h % 2 + 1, rs] = rows[h:h + 1, :]

    return pl.pallas_call(
        body, grid=(t // tm,), name="head_fwd_bwd",
        in_specs=[_row_spec(tm, E_B), _row_spec(tm, E_B), _row_spec(tm, D_MODEL), _row_spec(tm, D_MODEL),
                  _const_spec((E_B, D_MODEL)), _const_spec((1, D_MODEL))],
        out_specs=[_row_spec(tm, D_MODEL), _row_spec(tm, E_B), _row_spec(tm, E_B),
                   pl.BlockSpec((N_HEADS // 2, 2, tm), lambda i: (0, 0, i)),
                   _acc_spec((1, 1)), _acc_spec((1, D_MODEL)), _acc_spec((E_B, D_MODEL))],
        out_shape=[jax.ShapeDtypeStruct((t, D_MODEL), F32), jax.ShapeDtypeStruct((t, E_B), BF16),
                   jax.ShapeDtypeStruct((t, E_B), BF16), jax.ShapeDtypeStruct((N_HEADS // 2, 2, t), F32),
                   jax.ShapeDtypeStruct((1, 1), F32), jax.ShapeDtypeStruct((1, D_MODEL), F32),
                   jax.ShapeDtypeStruct((E_B, D_MODEL), F32)],
        compiler_params=_params("arbitrary"),
    )(o, gate, x1, tgt, w_bout, gf)


def _attn_bwd(q, k, v, do, lse, dd, seq):
    t = q.shape[0]
    tq = min(ATT_TILE, seq)
    nq = seq // tq
    assert nq % 2 == 0, "full tiles are taken in pairs"

    def body(q_ref, k_ref, v_ref, do_ref, lse_ref, dd_ref, dq_ref, dk_ref, dv_ref, dq_acc, dk_acc, dv_acc):
        dq_acc[...] = jnp.zeros_like(dq_acc)

        def k_step(ki, _):
            k0 = pl.multiple_of(ki * tq, tq)
            dk_acc[...] = jnp.zeros_like(dk_acc)
            dv_acc[...] = jnp.zeros_like(dv_acc)

            def block(k_lo, k_n, q0, q_n, masked):
                rows = slice(k_lo, k_lo + k_n)
                lane = lax.broadcasted_iota(jnp.int32, (q_n, 2 * V_HEAD), 1)
                vt = v_ref[pl.ds(k0 + k_lo, k_n), :]
                do_pair = do_ref[pl.ds(q0, q_n), :]

                def operands(hh):
                    hs = slice(hh * HEAD_PAD, (hh + 1) * HEAD_PAD)
                    kt = k_ref[pl.ds(k0 + k_lo, k_n), hs]
                    qt = q_ref[pl.ds(q0, q_n), hs]
                    mine = (lane < V_HEAD) if hh == 0 else (lane >= V_HEAD)
                    do_h = jnp.where(mine, do_pair, jnp.zeros((), BF16))
                    return hs, kt, qt, do_h, _dot_nt(kt, qt), _dot_nt(vt, do_h)

                early = [operands(hh) for hh in range(2)] if masked else None
                for hh in range(2):
                    hs, kt, qt, do_h, st, dpt = early[hh] if masked else operands(hh)
                    if masked:
                        krow = lax.broadcasted_iota(jnp.int32, (k_n, q_n), 0)
                        qcol = lax.broadcasted_iota(jnp.int32, (k_n, q_n), 1)
                        st = jnp.where(krow <= qcol, st, -jnp.inf)
                    pt = jnp.exp2(st - lse_ref[hh:hh + 1, pl.ds(q0, q_n)])
                    dst = (pt * (dpt - dd_ref[hh:hh + 1, pl.ds(q0, q_n)])).astype(BF16)
                    dv_acc[rows, :] += _dot(pt.astype(BF16), do_h)
                    dk_acc[rows, hs] += _dot(dst, qt)
                    dq_acc[pl.ds(q0, q_n), hs] += _dot_tn(dst, kt)

            wide = 2 * tq

            def q_step(qj, _):
                block(0, tq, pl.multiple_of(qj * wide, wide), wide, False)
                return 0

            half = tq // 2
            block(0, half, k0, tq, True)
            block(half, half, pl.multiple_of(k0 + half, half), half, True)

            @pl.when(ki % 2 == 0)
            def _():
                block(0, tq, pl.multiple_of(k0 + tq, tq), tq, False)

            lax.fori_loop(ki // 2 + 1, nq // 2, q_step, 0)
            dk_ref[pl.ds(k0, tq), :] = (dk_acc[...] * LN2).astype(BF16)
            dv_ref[pl.ds(k0, tq), :] = dv_acc[...].astype(BF16)
            return 0

        lax.fori_loop(0, nq, k_step, 0)
        dq_ref[...] = (dq_acc[...] * SOFTMAX_SCALE).astype(BF16)

    qk, vo, st = _pair_specs(seq)
    return pl.pallas_call(
        body, grid=(t // seq, N_HEADS // 2), name="attn_bwd",
        in_specs=[qk, qk, vo, vo, st, st], out_specs=[qk, qk, vo],
        out_shape=[jax.ShapeDtypeStruct((t, QK_PAD), BF16), jax.ShapeDtypeStruct((t, QK_PAD), BF16),
                   jax.ShapeDtypeStruct((t, E_B), BF16)],
        scratch_shapes=[pltpu.VMEM((seq, 2 * HEAD_PAD), F32), pltpu.VMEM((tq, 2 * HEAD_PAD), F32),
                        pltpu.VMEM((tq, 2 * V_HEAD), F32)],
        compiler_params=_params("parallel", "parallel"),
    )(q, k, v, do, lse, dd)


def _mid_bwd(dq, dk, dv, dgate, dx2, x1, cq, ckv, rc, rs1, rs2, w_uq, w_bin, w_uk, w_uv, w_dkv, gq, gc, gk, gb):
    t = dq.shape[0]
    tm = min(ROW_TILE, t)

    def body(dq_ref, dk_ref, dv_ref, dgate_ref, dx2_ref, x1_ref, cq_ref, ckv_ref, c_ref, s1_ref, s2_ref,
             wq_ref, wi_ref, wk_ref, wv_ref, wd_ref, gq_ref, gc_ref, gk_ref, gb_ref,
             dx1_ref, dwq_ref, dwi_ref, dwk_ref, dwv_ref, dwd_ref, dgq_ref, dgc_ref, dgk_ref, dgb_ref):
        @pl.when(pl.program_id(0) == 0)
        def _():
            for ref in (dwq_ref, dwi_ref, dwk_ref, dwv_ref, dwd_ref, dgq_ref, dgc_ref, dgk_ref, dgb_ref):
                ref[...] = jnp.zeros_like(ref)

        cb, s1b, s2b = c_ref[...], s1_ref[...], s2_ref[...]
        gk, gb, gq, gc = gk_ref[...], gb_ref[...], gq_ref[...], gc_ref[...]

        dkb, dvb = dk_ref[...], dv_ref[...]
        dckv = _dot_nt(dkb, wk_ref[...]) + _dot_nt(dvb, wv_ref[...])
        rcv, ch = _rms(ckv_ref[...].astype(F32))
        ckvn = (ch * gc).astype(BF16)
        dwk_ref[...] += _dot_tn(ckvn, dkb)
        dwv_ref[...] += _dot_tn(ckvn, dvb)

        dqs = [_rope_bwd(dq_ref[:, h * HEAD_PAD:(h + 1) * HEAD_PAD].astype(F32), cb, s1b, s2b)
               for h in range(N_HEADS)]
        dqb = jnp.concatenate(dqs, axis=-1).astype(BF16)
        rq, cqh = _rms(cq_ref[...].astype(F32))
        dcqn = _dot_nt(dqb, wq_ref[...])
        dwq_ref[...] += _dot_tn((cqh * gq).astype(BF16), dqb)

        dgc_ref[...] += jnp.sum(dckv * ch, axis=0, keepdims=True)
        dckv_raw = _rms_bwd(dckv * gc, ch, rcv)
        dkr = dk_ref[:, 0:HEAD_PAD].astype(F32)
        for h in range(1, N_HEADS):
            dkr = dkr + dk_ref[:, h * HEAD_PAD:(h + 1) * HEAD_PAD].astype(F32)
        dkr = _rope_bwd(dkr, cb, s1b, s2b)
        dckr = jnp.concatenate([dckv_raw, dkr], axis=-1).astype(BF16)
        r1, xh = _rms(x1_ref[...])
        dhk = _dot_nt(dckr, wd_ref[...])
        dwd_ref[...] += _dot_tn((xh * gk).astype(BF16), dckr)

        dgq_ref[...] += jnp.sum(dcqn * cqh, axis=0, keepdims=True)
        dcq = _rms_bwd(dcqn * gq, cqh, rq)
        dpb = jnp.concatenate([dcq.astype(BF16), dgate_ref[...]], axis=-1)
        dh1 = _dot_nt(dpb, wi_ref[...])
        dwi_ref[...] += _dot_tn((xh * gb).astype(BF16), dpb)

        dgb_ref[...] += jnp.sum(dh1 * xh, axis=0, keepdims=True)
        dgk_ref[...] += jnp.sum(dhk * xh, axis=0, keepdims=True)
        dx1_ref[...] = dx2_ref[...] + _rms_bwd(dh1 * gb + dhk * gk, xh, r1)

    acc_shapes = [(Q_RANK, QK_PAD), (D_MODEL, Q_RANK + E_B), (KV_RANK, QK_PAD), (KV_RANK, E_B), (D_MODEL, KR_PAD),
                  (1, Q_RANK), (1, KV_RANK), (1, D_MODEL), (1, D_MODEL)]
    return pl.pallas_call(
        body, grid=(t // tm,), name="mid_bwd",
        in_specs=[_row_spec(tm, QK_PAD), _row_spec(tm, QK_PAD), _row_spec(tm, E_B), _row_spec(tm, E_B),
                  _row_spec(tm, D_MODEL), _row_spec(tm, D_MODEL), _row_spec(tm, Q_RANK), _row_spec(tm, KV_RANK),
                  _row_spec(tm, HEAD_PAD), _row_spec(tm, HEAD_PAD), _row_spec(tm, HEAD_PAD),
                  _const_spec((Q_RANK, QK_PAD)), _const_spec((D_MODEL, Q_RANK + E_B)),
                  _const_spec((KV_RANK, QK_PAD)), _const_spec((KV_RANK, E_B)), _const_spec((D_MODEL, KR_PAD)),
                  _const_spec((1, Q_RANK)), _const_spec((1, KV_RANK)), _const_spec((1, D_MODEL)),
                  _const_spec((1, D_MODEL))],
        out_specs=[_row_spec(tm, D_MODEL)] + [_acc_spec(s) for s in acc_shapes],
        out_shape=[jax.ShapeDtypeStruct((t, D_MODEL), F32)] + [jax.ShapeDtypeStruct(s, F32) for s in acc_shapes],
        compiler_params=_params("arbitrary"),
    )(dq, dk, dv, dgate, dx2, x1, cq, ckv, rc, rs1, rs2, w_uq, w_bin, w_uk, w_uv, w_dkv, gq, gc, gk, gb)


def _conv_bwd(dx1, x, c, u, v, ym, kb, kcv, kg, seq, w_out, w_in4, ga, cw):
    t = x.shape[0]
    tm = min(CONV_BWD_TILE, seq)
    tiles_per_seq = seq // tm
    n = t // tm

    def tile(i):
        return n - 1 - i

    def rev(width):
        return pl.BlockSpec((tm, width), lambda i: (tile(i), 0))

    def body(dx1_ref, x_ref, c_ref, u_ref, v_ref, ym_ref, kb_ref, kcv_ref, kg_ref, wo_ref, wi_ref, ga_ref, cw_ref,
             dx_ref, dwi_ref, dwo_ref, dcw_ref, dga_ref, carry_ref):
        i = pl.program_id(0)
        j = tile(i)

        @pl.when(i == 0)
        def _():
            for ref in (dwi_ref, dwo_ref, dcw_ref, dga_ref):
                ref[...] = jnp.zeros_like(ref)

        @pl.when(j % tiles_per_seq == tiles_per_seq - 1)
        def _():
            carry_ref[...] = jnp.zeros_like(carry_ref)

        dx1 = dx1_ref[...]
        dx1b = dx1.astype(BF16)
        dym = _dot_nt(dx1b, wo_ref[...])
        dwo_ref[...] += _dot_tn(ym_ref[...], dx1b)
        db = dym * kb_ref[...].astype(F32)
        dcv = dym * kcv_ref[...].astype(F32)
        dg = dym * kg_ref[...].astype(F32)

        row = lax.broadcasted_iota(jnp.int32, (tm, 1), 0)
        w0, w1, w2 = cw_ref[0:1, :], cw_ref[1:2, :], cw_ref[2:3, :]
        nxt = carry_ref[...]
        n0, n1 = nxt[0:1, :], nxt[1:2, :]
        d1 = jnp.where(row == tm - 1, n0, pltpu.roll(dcv, tm - 1, 0))
        d2 = jnp.where(row == tm - 1, n1, jnp.where(row == tm - 2, n0, pltpu.roll(dcv, tm - 2, 0)))
        carry_ref[...] = dcv[0:8, :]
        dv = w2 * dcv + w1 * d1 + w0 * d2
        v = v_ref[...].astype(F32)
        dcw_ref[0:1, :] += jnp.sum(d2 * v, axis=0, keepdims=True)
        dcw_ref[1:2, :] += jnp.sum(d1 * v, axis=0, keepdims=True)
        dcw_ref[2:3, :] += jnp.sum(dcv * v, axis=0, keepdims=True)

        r0, xh = _rms(x_ref[...])
        ga = ga_ref[...]
        h = (xh * ga).astype(BF16)
        dh = jnp.zeros((tm, D_MODEL), F32)
        for idx, dpart in enumerate((db, dv * u_ref[...].astype(F32), dv * c_ref[...].astype(F32), dg)):
            dpb = dpart.astype(BF16)
            dh = dh + _dot_nt(dpb, wi_ref[idx])
            dwi_ref[idx] += _dot_tn(h, dpb)
        dga_ref[...] += jnp.sum(dh * xh, axis=0, keepdims=True)
        dx_ref[...] = dx1 + _rms_bwd(dh * ga, xh, r0)

    acc_shapes = [(4, D_MODEL, E_A), (E_A, D_MODEL), (8, E_A), (1, D_MODEL)]
    return pl.pallas_call(
        body, grid=(n,), name="conv_bwd",
        in_specs=[rev(D_MODEL), rev(D_MODEL)] + [rev(E_A)] * 7
        + [_const_spec((E_A, D_MODEL)), _const_spec((4, D_MODEL, E_A)), _const_spec((1, D_MODEL)),
           _const_spec((8, E_A))],
        out_specs=[rev(D_MODEL)] + [_acc_spec(s) for s in acc_shapes],
        out_shape=[jax.ShapeDtypeStruct((t, D_MODEL), F32)] + [jax.ShapeDtypeStruct(s, F32) for s in acc_shapes],
        scratch_shapes=[pltpu.VMEM((8, E_A), F32)],
        compiler_params=_params("arbitrary"),
    )(dx1, x, c, u, v, ym, kb, kcv, kg, w_out, w_in4, ga, cw)


WEIGHTS = ("a_norm", "a_w_in", "a_conv", "a_w_out", "kv_norm", "w_dkv", "ckv_norm", "w_ukv", "b_norm", "b_w_in",
           "b_q_norm", "b_w_uq", "b_w_out", "final_norm")
SHARD_SHAPES = {
    "a_norm": (1, 256), "a_w_in": (1, 1024, 1024), "a_conv": (1, 3, 256), "a_w_out": (1, 256, 1024),
    "kv_norm": (1024,), "w_dkv": (256, 288), "ckv_norm": (256,), "w_ukv": (256, 256), "b_norm": (1, 1024),
    "b_w_in": (1, 256, 896), "b_q_norm": (1, 384), "b_w_uq": (1, 384, 192), "b_w_out": (1, 512, 256),
    "final_norm": (1024,),
}
MATS = ("a_w_in", "a_w_out", "w_dkv", "w_ukv", "b_w_in", "b_w_uq", "b_w_out")
SMALL = ("a_norm", "a_conv", "kv_norm", "ckv_norm", "b_norm", "b_q_norm", "final_norm")
SMALL_FULL = {"a_norm": 1024, "a_conv": 3072, "kv_norm": 1024, "ckv_norm": 256, "b_norm": 1024, "b_q_norm": 384,
              "final_norm": 1024}
SMALL_ROWS = 8
LOSS_SLOT = sum(SMALL_FULL.values())


def _mat2d(name, a):
    return a.reshape(SHARD_SHAPES[name][-2:])


def _prep_first(g_win, gsmall):
    sm = gsmall.reshape(N_CHIPS, -1)
    a_conv = jnp.transpose(sm[:, 256:1024].reshape(N_CHIPS, CONV_WIDTH, 256), (1, 0, 2)).reshape(CONV_WIDTH, -1)
    return {"w_in4": g_win, "ga": sm[:, :256].reshape(1, -1), "cw": jnp.pad(a_conv, ((0, 8 - CONV_WIDTH), (0, 0)))}


def _prep_rest(gath, w):
    def cols(a):
        return jnp.transpose(a, (1, 0, 2)).reshape(a.shape[1], -1)

    def pad_heads(a, width):
        a = a.reshape(a.shape[0], N_HEADS, width)
        return jnp.pad(a, ((0, 0), (0, 0), (0, HEAD_PAD - width))).reshape(a.shape[0], QK_PAD)

    row = lambda a: a.reshape(1, -1).astype(F32)
    w_dkv = gath["w_dkv"].reshape(D_MODEL, KV_RANK + QK_ROPE)
    w_ukv = cols(gath["w_ukv"]).reshape(KV_RANK, N_HEADS, 2, QK_NOPE)
    return {
        "w_out": gath["a_w_out"].reshape(E_A, D_MODEL),
        "w_dkv": jnp.concatenate([w_dkv[:, :KV_RANK], jnp.zeros((D_MODEL, ROPE_LO), BF16), w_dkv[:, KV_RANK:],
                                  jnp.zeros((D_MODEL, HEAD_PAD - ROPE_LO - QK_ROPE), BF16)], axis=1),
        "w_uk": pad_heads(w_ukv[:, :, 0, :].reshape(KV_RANK, N_HEADS * QK_NOPE), QK_NOPE),
        "w_uv": w_ukv[:, :, 1, :].reshape(KV_RANK, E_B),
        "w_bin": gath["b_w_in"].reshape(D_MODEL, Q_RANK + E_B),
        "w_uq": pad_heads(cols(gath["b_w_uq"]), QK_NOPE + QK_ROPE),
        "w_bout": cols(gath["b_w_out"]),
        "gk": row(w["kv_norm"]), "gc": row(w["ckv_norm"]), "gb": row(w["b_norm"]),
        "gq": row(w["b_q_norm"]), "gf": row(w["final_norm"]),
    }


TRANSPOSED = ("w_dkv", "b_w_uq")
GRAD_MATS = ("w_in4", "w_out", "w_dkv", "w_uk", "w_uv", "w_bin", "w_uq", "w_bout")
GRAD_KIND = {"w_in4": "lead", "w_out": "row", "w_dkv": "row", "w_uk": "col", "w_uv": "col", "w_bin": "row",
             "w_uq": "col", "w_bout": "col"}


def _local_step(x, positions, tgt, w):
    bsz, seq, _ = x.shape
    t = bsz * seq
    x2d = x.reshape(t, D_MODEL)
    small = jnp.concatenate([w["a_norm"].reshape(-1), w["a_conv"].reshape(-1)]).reshape(8, LANES)
    rc, rs1, rs2, g_win, gsmall = _rope_tables_gather(positions.reshape(t, 1), _mat2d(MATS[0], w[MATS[0]]), small)
    wk = _prep_first(g_win, gsmall)
    (c, u, v, ym, kb, kcv, kg), gathered = _conv_fwd(x2d, seq, wk["ga"], wk["w_in4"], wk["cw"],
                                           [_mat2d(n, w[n]) for n in MATS[1:]])
    wk.update(_prep_rest(dict(zip(MATS[1:], gathered)), w))
    x1, q, k, vv, gate, cq, ckv = _mid_fwd(x2d, ym, wk["w_out"], wk["gk"], wk["gb"], wk["w_dkv"], wk["gc"], wk["w_uk"],
                                          wk["w_uv"], wk["w_bin"], wk["gq"], wk["w_uq"], rc, rs1, rs2)
    o, lse = _attn_fwd(q, k, vv, seq)
    dx2, do, dgate, dd, loss, dgf, dw_bout = _head_fwd_bwd(o, gate, x1, tgt.reshape(t, D_MODEL), wk["w_bout"], wk["gf"])
    dq, dk, dv = _attn_bwd(q, k, vv, do, lse, dd, seq)
    dx1, dwq, dw_bin, dwk, dwv, dwd, dgq, dgc, dgk, dgb = _mid_bwd(
        dq, dk, dv, dgate, dx2, x1, cq, ckv, rc, rs1, rs2, wk["w_uq"], wk["w_bin"], wk["w_uk"], wk["w_uv"], wk["w_dkv"],
        wk["gq"], wk["gc"], wk["gk"], wk["gb"])
    dx, dw_in4, dw_out, dcw, dga = _conv_bwd(dx1, x2d, c, u, v, ym, kb, kcv, kg, seq, wk["w_out"], wk["w_in4"], wk["ga"], wk["cw"])
    mats = {"w_in4": dw_in4, "w_out": dw_out, "w_dkv": dwd, "w_uk": dwk, "w_uv": dwv, "w_bin": dw_bin, "w_uq": dwq,
            "w_bout": dw_bout}
    small = {"a_norm": dga, "a_conv": dcw[:CONV_WIDTH], "kv_norm": dgk, "ckv_norm": dgc, "b_norm": dgb,
             "b_q_norm": dgq, "final_norm": dgf}
    return loss[0, 0], dx.reshape(bsz, seq, D_MODEL), mats, small


def _shard_grads(sh, svec):
    j0 = 2 * lax.axis_index("x") + lax.axis_index("y")
    flat = svec.reshape(-1)
    off, small = 0, {}
    for n in SMALL:
        small[n] = flat[off:off + SMALL_FULL[n]]
        off += SMALL_FULL[n]
    dwd, dwk, dwv, dwq = sh["w_dkv"], sh["w_uk"], sh["w_uv"], sh["w_uq"]
    w_ukv = jnp.stack([dwk.reshape(KV_RANK, 2, HEAD_PAD)[:, :, :QK_NOPE], dwv.reshape(KV_RANK, 2, V_HEAD)], axis=2)
    return {
        "a_norm": lax.dynamic_slice(small["a_norm"], (j0 * 256,), (256,)),
        "a_conv": lax.dynamic_slice(small["a_conv"].reshape(CONV_WIDTH, E_A), (0, j0 * 256), (CONV_WIDTH, 256)),
        "kv_norm": small["kv_norm"], "ckv_norm": small["ckv_norm"], "b_norm": small["b_norm"],
        "b_q_norm": small["b_q_norm"], "final_norm": small["final_norm"],
        "a_w_in": sh["w_in4"], "a_w_out": sh["w_out"],
        "w_dkv": jnp.concatenate([dwd[:, :KV_RANK], dwd[:, KV_RANK + ROPE_LO:KV_RANK + ROPE_LO + QK_ROPE]], axis=1),
        "w_ukv": w_ukv.reshape(KV_RANK, 2 * (QK_NOPE + V_HEAD)),
        "b_w_in": sh["w_bin"],
        "b_w_uq": dwq.reshape(Q_RANK, 2, HEAD_PAD)[:, :, :QK_NOPE + QK_ROPE].reshape(Q_RANK, -1),
        "b_w_out": sh["w_bout"],
    }


def _sub(ref, kind, j, cc):
    if kind == "lead":
        h = ref.shape[1] // 2
        return ref.at[j, pl.ds(pl.multiple_of(cc * h, 8), h), :]
    if kind == "row":
        rows = ref.shape[0] // N_CHIPS
        h = rows // 2
        return ref.at[pl.ds(pl.multiple_of(j * rows + cc * h, 8), h), :]
    cols = ref.shape[1] // N_CHIPS
    h = ref.shape[0] // 2
    return ref.at[pl.ds(pl.multiple_of(cc * h, 8), h), pl.ds(j * cols, cols)]


def _sub_shape(shape, kind):
    if kind == "lead":
        return (shape[1] // 2, shape[2])
    if kind == "row":
        return (shape[0] // N_CHIPS // 2, shape[1])
    return (shape[0] // 2, shape[1] // N_CHIPS)


def _reduce_grads(grads, kinds, vec):
    n = len(grads)
    shapes = [_sub_shape(a.shape, kd) for a, kd in zip(grads, kinds)]
    units = [(k, j) for k in range(n) for j in range(N_CHIPS)]
    big = (max(s[0] for s in shapes), max(s[1] for s in shapes))

    def body(*refs):
        g, v_ref = refs[:n], refs[n]
        out, o_ref = refs[n + 1:2 * n + 1], refs[2 * n + 1]
        theirs, part, recd, red = (refs[(2 + i) * n + 2:(3 + i) * n + 2] for i in range(4))
        mine, got = refs[6 * n + 2], refs[6 * n + 3]
        send1, recv1, send3, recv3, send5, recv5, load, local, send_v, recv_v = refs[6 * n + 4:]
        x, y, c = _place()
        j0 = 2 * x + y
        me = 2 * j0 + c
        sibling = (x, y, 1 - c)

        got[me] = v_ref[...]
        first, small_in = [], []
        for d in range(1, N_DEV):
            px, py, pc = x ^ (d >> 2), y ^ ((d >> 1) & 1), c ^ (d & 1)
            first.append(_remote(v_ref, got.at[me], send_v.at[d - 1], recv_v.at[d - 1], (px, py, pc)))
            small_in.append(_remote(v_ref, got.at[4 * px + 2 * py + pc], send_v.at[d - 1], recv_v.at[d - 1],
                                    (px, py, pc)))
        halves = [_remote(_sub(g[k], kinds[k], j, 1 - c), theirs[k].at[j], send1.at[u], recv1.at[u], sibling)
                  for u, (k, j) in enumerate(units)]
        for cp in first + halves:
            cp.start()

        def mine_load(u):
            k, j = units[u]
            h, cols = shapes[k]
            return pltpu.make_async_copy(_sub(g[k], kinds[k], j, c), mine.at[u % 2, pl.ds(0, h), pl.ds(0, cols)],
                                         load.at[u % 2])

        mine_load(0).start()
        for u, (k, j) in enumerate(units):
            h, cols = shapes[k]
            if u + 1 < len(units):
                mine_load(u + 1).start()
            mine_load(u).wait()
            halves[u].wait_recv()
            part[k][j] = (mine[u % 2, 0:h, 0:cols] + theirs[k][j]).astype(BF16)
            to_owner = _remote(part[k].at[j], recd[k].at[j0], send3.at[u], recv3.at[4 * k + j0], (j // 2, j % 2, c))

            @pl.when(j != j0)
            def _():
                to_owner.start()

            @pl.when(j == j0)
            def _():
                recd[k][j] = part[k][j]

        swaps = []
        for k in range(n):
            for j in range(N_CHIPS):
                arrived = _remote(part[k].at[j], recd[k].at[j], send3.at[4 * k + j], recv3.at[4 * k + j],
                                  (j // 2, j % 2, c))

                @pl.when(j != j0)
                def _():
                    arrived.wait_recv()

            r = recd[k]
            red[k][...] = ((r[0].astype(F32) + r[1].astype(F32)) + r[2].astype(F32)) + r[3].astype(F32)
            own = pltpu.make_async_copy(red[k], out[k].at[c], local.at[k])
            give = _remote(red[k], out[k].at[c], send5.at[k], recv5.at[k], sibling)
            take = _remote(red[k], out[k].at[1 - c], send5.at[k], recv5.at[k], sibling)
            own.start()
            give.start()
            swaps.append((own, give, take))

        for cp in small_in:
            cp.wait_recv()
            cp.wait_send()
        acc = got[0]
        for d in range(1, N_DEV):
            acc = acc + got[d]
        o_ref[...] = acc
        for u, (k, j) in enumerate(units):
            halves[u].wait_send()
            sent = _remote(part[k].at[j], recd[k].at[j0], send3.at[u], recv3.at[u], (j // 2, j % 2, c))

            @pl.when(j != j0)
            def _():
                sent.wait_send()
        for own, give, take in swaps:
            take.wait_recv()
            give.wait_send()
            own.wait()

    sems = [pltpu.SemaphoreType.DMA((len(units),))] * 4 + [pltpu.SemaphoreType.DMA((n,))] * 2 \
        + [pltpu.SemaphoreType.DMA((2,)), pltpu.SemaphoreType.DMA((n,))] + [pltpu.SemaphoreType.DMA((N_DEV - 1,))] * 2
    res = pl.pallas_call(
        body, name="reduce_grads",
        in_specs=[HBM] * n + [WHOLE], out_specs=[HBM] * n + [WHOLE],
        out_shape=[jax.ShapeDtypeStruct((2,) + s, F32) for s in shapes] + [jax.ShapeDtypeStruct(vec.shape, vec.dtype)],
        scratch_shapes=[pltpu.VMEM((N_CHIPS,) + s, F32) for s in shapes]
        + [pltpu.VMEM((N_CHIPS,) + s, BF16) for s in shapes] * 2
        + [pltpu.VMEM(s, F32) for s in shapes]
        + [pltpu.VMEM((2,) + big, F32), pltpu.VMEM((N_DEV,) + vec.shape, vec.dtype)] + sems,
        compiler_params=_comm_params(),
    )(*grads, vec)
    return res[:n], res[n]


def _adamw_math(w, g, m, v):
    m = ADAM_B1 * m + (1.0 - ADAM_B1) * g
    v = ADAM_B2 * v + (1.0 - ADAM_B2) * (g * g)
    m_hat = m / (1.0 - ADAM_B1 ** ADAM_STEP)
    v_hat = v / (1.0 - ADAM_B2 ** ADAM_STEP)
    return -ADAM_LR * (m_hat / (jnp.sqrt(v_hat) + ADAM_EPS) + ADAM_WD * w), m, v


def _adamw_tiled(w, g, m, v):
    rows, width = w.shape
    tm = rows // 4

    def body(w_ref, g_ref, m_ref, v_ref, go_ref, d_ref, mo_ref, vo_ref):
        g = g_ref[...]
        go_ref[...] = g
        d_ref[...], mo_ref[...], vo_ref[...] = _adamw_math(w_ref[...], g, m_ref[...], v_ref[...])

    spec = pl.BlockSpec((tm, width), lambda i: (i, 0))
    out = jax.ShapeDtypeStruct((rows, width), F32)
    return pl.pallas_call(
        body, grid=(rows // tm,), name="adamw_tiled",
        in_specs=[spec] * 4, out_specs=[spec] * 4, out_shape=[out] * 4,
        compiler_params=_params("parallel"),
    )(w, g, m, v)


def _adamw_many(ws, gs, ms, vs):
    n = len(ws)

    def body(*refs):
        for k in range(n):
            w_ref, g_ref, m_ref, v_ref = (refs[i * n + k] for i in range(4))
            go_ref, d_ref, mo_ref, vo_ref = (refs[(4 + i) * n + k] for i in range(4))
            g = g_ref[...]
            go_ref[...] = g
            d_ref[...], mo_ref[...], vo_ref[...] = _adamw_math(w_ref[...], g, m_ref[...], v_ref[...])

    outs = [jax.ShapeDtypeStruct(a.shape, F32) for a in ws]
    res = pl.pallas_call(
        body, name="adamw_many",
        in_specs=[WHOLE] * (4 * n), out_specs=[WHOLE] * (4 * n), out_shape=outs * 4,
        compiler_params=_comm_params(),
    )(*ws, *gs, *ms, *vs)
    return res[:n], res[n:2 * n], res[2 * n:3 * n], res[3 * n:]


def kernel(x, positions, a_norm, a_w_in, a_conv, a_w_out, kv_norm, w_dkv, ckv_norm, w_ukv, b_norm, b_w_in, b_q_norm, b_w_uq, b_w_out, final_norm, loss_target, m_a_norm, m_a_w_in, m_a_conv, m_a_w_out, m_kv_norm, m_w_dkv, m_ckv_norm, m_w_ukv, m_b_norm, m_b_w_in, m_b_q_norm, m_b_w_uq, m_b_w_out, m_final_norm, v_a_norm, v_a_w_in, v_a_conv, v_a_w_out, v_kv_norm, v_w_dkv, v_ckv_norm, v_w_ukv, v_b_norm, v_b_w_in, v_b_q_norm, v_b_w_uq, v_b_w_out, v_final_norm):
    w = dict(a_norm=a_norm, a_w_in=a_w_in, a_conv=a_conv, a_w_out=a_w_out, kv_norm=kv_norm, w_dkv=w_dkv,
             ckv_norm=ckv_norm, w_ukv=w_ukv, b_norm=b_norm, b_w_in=b_w_in, b_q_norm=b_q_norm, b_w_uq=b_w_uq,
             b_w_out=b_w_out, final_norm=final_norm)
    m = dict(a_norm=m_a_norm, a_w_in=m_a_w_in, a_conv=m_a_conv, a_w_out=m_a_w_out, kv_norm=m_kv_norm, w_dkv=m_w_dkv,
             ckv_norm=m_ckv_norm, w_ukv=m_w_ukv, b_norm=m_b_norm, b_w_in=m_b_w_in, b_q_norm=m_b_q_norm,
             b_w_uq=m_b_w_uq, b_w_out=m_b_w_out, final_norm=m_final_norm)
    v = dict(a_norm=v_a_norm, a_w_in=v_a_w_in, a_conv=v_a_conv, a_w_out=v_a_w_out, kv_norm=v_kv_norm, w_dkv=v_w_dkv,
             ckv_norm=v_ckv_norm, w_ukv=v_w_ukv, b_norm=v_b_norm, b_w_in=v_b_w_in, b_q_norm=v_b_q_norm,
             b_w_uq=v_b_w_uq, b_w_out=v_b_w_out, final_norm=v_final_norm)

    loss, dx, gmat, gsmall = _local_step(x, positions, loss_target, w)

    kinds = [GRAD_KIND[n] for n in GRAD_MATS]
    grads = [gmat[n] for n in GRAD_MATS]
    flat = jnp.concatenate([gsmall[n].reshape(-1) for n in SMALL] + [loss.reshape(1)])
    flat = jnp.pad(flat, (0, SMALL_ROWS * PACK_W - flat.shape[0])).reshape(SMALL_ROWS, PACK_W)
    mine, svec = _reduce_grads(grads, kinds, flat)
    loss = svec.reshape(-1)[LOSS_SLOT]
    g = _shard_grads({n: a.reshape(-1, a.shape[-1]) for n, a in zip(GRAD_MATS, mine)}, svec)

    def two_d(n, a):
        a = a.reshape(-1, a.shape[-1])
        return a.T if n in TRANSPOSED else a

    def back(n, a):
        return (a.T if n in TRANSPOSED else a).reshape(SHARD_SHAPES[n])

    big = "a_w_in"
    rest = [n for n in WEIGHTS if n != big]
    res_big = _adamw_tiled(*(two_d(big, t[big]) for t in (w, g, m, v)))
    res_rest = _adamw_many(*([two_d(n, t[n]) for n in rest] for t in (w, g, m, v)))
    out = {kind: dict(zip(rest, res_rest[i])) for i, kind in enumerate("gdmv")}
    for i, kind in enumerate("gdmv"):
        out[kind][big] = res_big[i]
    return (loss, dx) + tuple(back(n, out[kind][n]) for kind in "gdmv" for n in WEIGHTS)
```

```python
import functools
import math

import numpy as np
import jax
import jax.numpy as jnp
from jax import lax
from jax.experimental import pallas as pl
from jax.experimental.pallas import tpu as pltpu

F32 = jnp.float32
BF16 = jnp.bfloat16

D_MODEL = 1024
E_A = 1024
CONV_WIDTH = 3
N_HEADS = 8
QK_NOPE = 64
QK_ROPE = 32
V_HEAD = 64
KV_RANK = 256
Q_RANK = 384
E_B = N_HEADS * V_HEAD
ROPE_THETA = 10000.0
SOFTMAX_SCALE = 1.0 / math.sqrt(QK_NOPE + QK_ROPE)
LOG2E = math.log2(math.e)
LN2 = math.log(2.0)
Q_PRESCALE = SOFTMAX_SCALE * LOG2E
EPS = 1e-6
HEAD_PAD = 128
QK_PAD = N_HEADS * HEAD_PAD
ROPE_LO = QK_NOPE
ROPE_HALF = QK_ROPE // 2
KR_PAD = KV_RANK + HEAD_PAD

ADAM_LR = 0.001
ADAM_B1 = 0.9
ADAM_B2 = 0.999
ADAM_EPS = 1e-08
ADAM_WD = 0.01
ADAM_STEP = 10

VMEM_LIMIT = 56 * 1024 * 1024
ROW_TILE = 512
CONV_BWD_TILE = 256
ATT_TILE_FWD = 1024
ATT_TILE = 512
HEAD_CHAINS = 2
LANES = 128
PACK_W = 1024

N_CHIPS = 4
N_DEV = 8


def _dot(a, b):
    return jnp.dot(a, b, preferred_element_type=F32)


def _dot_nt(a, b):
    return lax.dot_general(a, b, (((1,), (1,)), ((), ())), preferred_element_type=F32)


def _dot_tn(a, b):
    return lax.dot_general(a, b, (((0,), (0,)), ((), ())), preferred_element_type=F32)


def _rms(x):
    r = lax.rsqrt(jnp.mean(x * x, axis=-1, keepdims=True) + EPS)
    return r, x * r


def _rms_bwd(dxh, xh, r):
    return r * (dxh - xh * jnp.mean(dxh * xh, axis=-1, keepdims=True))


def _rope_fwd(a, c, s1, s2):
    return a * c + pltpu.roll(a, HEAD_PAD - ROPE_HALF, 1) * s1 + pltpu.roll(a, ROPE_HALF, 1) * s2


def _rope_bwd(g, c, s1, s2):
    return g * c + pltpu.roll(g * s1, ROPE_HALF, 1) + pltpu.roll(g * s2, HEAD_PAD - ROPE_HALF, 1)


def _sigmoid(x):
    return 1.0 / (1.0 + jnp.exp(-x))


def _row_spec(tm, n):
    return pl.BlockSpec((tm, n), lambda i: (i, 0))


def _const_spec(shape):
    nd = len(shape)
    return pl.BlockSpec(shape, lambda i: (0,) * nd, pipeline_mode=pl.Buffered(1))


def _acc_spec(shape):
    nd = len(shape)
    return pl.BlockSpec(shape, lambda i: (0,) * nd, pipeline_mode=pl.Buffered(1))


def _params(*sem):
    return pltpu.CompilerParams(dimension_semantics=sem, vmem_limit_bytes=VMEM_LIMIT)


MESH = pl.DeviceIdType.MESH
HBM = pl.BlockSpec(memory_space=pl.ANY)
WHOLE = pl.BlockSpec(memory_space=pltpu.VMEM)
FLIPS = ((1, 0), (0, 1), (1, 1))


def _place():
    return lax.axis_index("x"), lax.axis_index("y"), lax.axis_index("c")


def _remote(src, dst, send, recv, peer):
    return pltpu.make_async_remote_copy(src_ref=src, dst_ref=dst, send_sem=send, recv_sem=recv, device_id=peer,
                                        device_id_type=MESH)


def _comm_params():
    return pltpu.CompilerParams(vmem_limit_bytes=VMEM_LIMIT)


def _rope_consts():
    lane = np.arange(HEAD_PAD)
    first = (lane >= ROPE_LO) & (lane < ROPE_LO + ROPE_HALF)
    second = (lane >= ROPE_LO + ROPE_HALF) & (lane < ROPE_LO + QK_ROPE)
    f = np.where(first, lane - ROPE_LO, np.where(second, lane - ROPE_LO - ROPE_HALF, 0))
    inv = np.float32(ROPE_THETA) ** (-(2 * f).astype(np.float32) / np.float32(QK_ROPE))
    out = np.zeros((8, HEAD_PAD), np.float32)
    out[0] = inv
    out[1] = first
    out[2] = second
    out[3] = lane < ROPE_LO
    return jnp.asarray(out)


def _gather_plan(w, sm, outs, osm, bf, sems):
    n = len(w)
    send_i, recv_i, send_d, recv_d, send_s, recv_s, local = sems
    x, y, c = _place()
    j0 = 2 * x + y
    sibling = (x, y, 1 - c)
    own = [pltpu.make_async_copy(bf[k], outs[k].at[j0], local.at[k]) for k in range(n)]
    if sm is not None:
        own.append(pltpu.make_async_copy(sm, osm.at[j0], local.at[n]))

    def half(k, cc):
        h = w[k].shape[0] // 2
        return pl.ds(pl.multiple_of(cc * h, 16), h)

    sends, arrivals, forwards, fwaits = [], [], [], []
    for i, (fx, fy) in enumerate(FLIPS):
        px, py = x ^ fx, y ^ fy
        pj = 2 * px + py
        for k in range(n):
            s = i * n + k
            sends.append(_remote(bf[k].at[half(k, c)], outs[k].at[j0, half(k, c)], send_i.at[s], recv_i.at[s],
                                 (px, py, c)))
            arrivals.append(_remote(bf[k].at[half(k, c)], outs[k].at[pj, half(k, c)], send_i.at[s], recv_i.at[s],
                                    (px, py, c)))
            forwards.append(_remote(outs[k].at[pj, half(k, c)], outs[k].at[pj, half(k, c)], send_d.at[s],
                                    recv_d.at[s], sibling))
            fwaits.append(_remote(outs[k].at[pj, half(k, 1 - c)], outs[k].at[pj, half(k, 1 - c)], send_d.at[s],
                                  recv_d.at[s], sibling))
        if sm is not None:
            sends.append(_remote(sm, osm.at[j0], send_s.at[i], recv_s.at[i], (px, py, c)))
            fwaits.append(_remote(sm, osm.at[pj], send_s.at[i], recv_s.at[i], (px, py, c)))
    return own, sends, arrivals, forwards, fwaits


def _gather_begin(w, bf, plan):
    own, sends, _, _, _ = plan
    for k in range(len(w)):
        bf[k][...] = w[k][...].astype(BF16)
    for cp in own + sends:
        cp.start()


def _gather_end(plan):
    own, sends, arrivals, forwards, fwaits = plan
    for cp, fwd in zip(arrivals, forwards):
        cp.wait_recv()
        fwd.start()
    for cp in fwaits:
        cp.wait_recv()
    for cp in sends + forwards:
        cp.wait_send()
    for cp in own:
        cp.wait()


def _gather_sems(n, with_small):
    return ([pltpu.SemaphoreType.DMA((3 * n,))] * 4 + [pltpu.SemaphoreType.DMA((3,))] * 2
            + [pltpu.SemaphoreType.DMA((n + (1 if with_small else 0),))])


def _rope_tables_gather(pos_col, w_in, small):
    t = pos_col.shape[0]
    tm = min(ROW_TILE, t)
    steps = t // tm

    def body(p_ref, k_ref, w_ref, sm_ref, c_ref, s1_ref, s2_ref, out_ref, osm_ref, bf_ref, *sems):
        plan = _gather_plan([w_ref], sm_ref, [out_ref], osm_ref, [bf_ref], sems)
        i = pl.program_id(0)

        @pl.when(i == 0)
        def _():
            _gather_begin([w_ref], [bf_ref], plan)

        inv, first, second, nope = k_ref[0:1, :], k_ref[1:2, :], k_ref[2:3, :], k_ref[3:4, :]
        ang = p_ref[...].astype(F32) * inv
        cs, sn = jnp.cos(ang), jnp.sin(ang)
        c_ref[...] = cs * (first + second) + nope
        s1_ref[...] = -sn * first
        s2_ref[...] = sn * second

        @pl.when(i == steps - 1)
        def _():
            _gather_end(plan)

    out = jax.ShapeDtypeStruct((t, HEAD_PAD), F32)
    return pl.pallas_call(
        body, grid=(steps,), name="rope_tables_gather",
        in_specs=[_row_spec(tm, 1), _const_spec((8, HEAD_PAD)), WHOLE, WHOLE],
        out_specs=[_row_spec(tm, HEAD_PAD)] * 3 + [HBM, HBM],
        out_shape=[out] * 3 + [jax.ShapeDtypeStruct((N_CHIPS,) + w_in.shape, BF16),
                               jax.ShapeDtypeStruct((N_CHIPS,) + small.shape, small.dtype)],
        scratch_shapes=[pltpu.VMEM(w_in.shape, BF16)] + _gather_sems(1, True),
        compiler_params=_params("arbitrary"),
    )(pos_col, _rope_consts(), w_in, small)


def _shift_down(v, prev, row):
    p1, p2 = prev[7:8, :], prev[6:7, :]
    v1 = jnp.where(row == 0, p1, pltpu.roll(v, 1, 0))
    v2 = jnp.where(row == 0, p2, jnp.where(row == 1, p1, pltpu.roll(v, 2, 0)))
    return v1, v2


def _conv_fwd(x, seq, ga, w_in4, cw, shards):
    t = x.shape[0]
    tm = min(ROW_TILE, seq)
    tiles_per_seq = seq // tm
    steps = t // tm
    n = len(shards)

    def body(x_ref, ga_ref, w_ref, cw_ref, *rest):
        sh, (c_ref, u_ref, v_ref, ym_ref, kb_ref, kcv_ref, kg_ref) = rest[:n], rest[n:n + 7]
        outs, carry_ref, bf, sems = rest[n + 7:2 * n + 7], rest[2 * n + 7], rest[2 * n + 8:3 * n + 8], rest[3 * n + 8:]
        plan = _gather_plan(sh, None, outs, None, bf, sems)
        i = pl.program_id(0)

        @pl.when(i == 0)
        def _():
            _gather_begin(sh, bf, plan)

        @pl.when(i % tiles_per_seq == 0)
        def _():
            carry_ref[...] = jnp.zeros_like(carry_ref)

        _, xh = _rms(x_ref[...])
        h = (xh * ga_ref[...]).astype(BF16)
        c, u = _dot(h, w_ref[1]), _dot(h, w_ref[2])
        g, b = _dot(h, w_ref[3]), _dot(h, w_ref[0])
        v = c * u
        row = lax.broadcasted_iota(jnp.int32, (tm, 1), 0)
        v1, v2 = _shift_down(v, carry_ref[...], row)
        carry_ref[...] = v[tm - 8:tm, :]
        cv = cw_ref[2:3, :] * v + cw_ref[1:2, :] * v1 + cw_ref[0:1, :] * v2
        sg = _sigmoid(g)
        silu = g * sg
        c_ref[...] = c.astype(BF16)
        u_ref[...] = u.astype(BF16)
        v_ref[...] = v.astype(BF16)
        ym_ref[...] = (silu * b * cv).astype(BF16)
        kb_ref[...] = (silu * cv).astype(BF16)
        kcv_ref[...] = (silu * b).astype(BF16)
        kg_ref[...] = (b * cv * (sg * (1.0 + g * (1.0 - sg)))).astype(BF16)

        @pl.when(i == steps - 1)
        def _():
            _gather_end(plan)

    out = jax.ShapeDtypeStruct((t, E_A), BF16)
    res = pl.pallas_call(
        body, grid=(steps,), name="conv_fwd",
        in_specs=[_row_spec(tm, D_MODEL), _const_spec((1, D_MODEL)), _const_spec((4, D_MODEL, E_A)),
                  _const_spec((8, E_A))] + [WHOLE] * n,
        out_specs=[_row_spec(tm, E_A)] * 7 + [HBM] * n,
        out_shape=[out] * 7 + [jax.ShapeDtypeStruct((N_CHIPS,) + a.shape, BF16) for a in shards],
        scratch_shapes=[pltpu.VMEM((8, E_A), F32)] + [pltpu.VMEM(a.shape, BF16) for a in shards]
        + _gather_sems(n, False),
        compiler_params=_params("arbitrary"),
    )(x, ga, w_in4, cw, *shards)
    return res[:7], res[7:]


def _mid_fwd(x, ym, w_out, gk, gb, w_dkv, gc, w_uk, w_uv, w_bin, gq, w_uq, rc, rs1, rs2):
    t = x.shape[0]
    tm = min(ROW_TILE, t)

    def body(x_ref, ym_ref, wo_ref, gk_ref, gb_ref, wd_ref, gc_ref, wk_ref, wv_ref, wi_ref, gq_ref, wq_ref,
             c_ref, s1_ref, s2_ref, x1_ref, q_ref, k_ref, v_ref, gate_ref, cq_ref, ckv_ref):
        cb, s1b, s2b = c_ref[...], s1_ref[...], s2_ref[...]
        x1 = x_ref[...] + _dot(ym_ref[...], wo_ref[...])
        x1_ref[...] = x1
        _, xh = _rms(x1)
        hk = (xh * gk_ref[...]).astype(BF16)
        h1 = (xh * gb_ref[...]).astype(BF16)

        pb = _dot(h1, wi_ref[...])
        cq = pb[:, :Q_RANK]
        cq_ref[...] = cq.astype(BF16)
        gate_ref[...] = pb[:, Q_RANK:].astype(BF16)
        _, cqh = _rms(cq)
        q = _dot((cqh * gq_ref[...]).astype(BF16), wq_ref[...])
        for h in range(N_HEADS):
            sl = slice(h * HEAD_PAD, (h + 1) * HEAD_PAD)
            q_ref[:, sl] = (_rope_fwd(q[:, sl], cb, s1b, s2b) * Q_PRESCALE).astype(BF16)

        ckr = _dot(hk, wd_ref[...])
        ckv_raw = ckr[:, :KV_RANK]
        ckv_ref[...] = ckv_raw.astype(BF16)
        _, ch = _rms(ckv_raw)
        ckv = (ch * gc_ref[...]).astype(BF16)
        kr = _rope_fwd(ckr[:, KV_RANK:], cb, s1b, s2b)
        kn = _dot(ckv, wk_ref[...])
        for h in range(N_HEADS):
            sl = slice(h * HEAD_PAD, (h + 1) * HEAD_PAD)
            k_ref[:, sl] = (kn[:, sl] + kr).astype(BF16)
        v_ref[...] = _dot(ckv, wv_ref[...]).astype(BF16)

    def sds(n, dt):
        return jax.ShapeDtypeStruct((t, n), dt)

    return pl.pallas_call(
        body, grid=(t // tm,), name="mid_fwd",
        in_specs=[_row_spec(tm, D_MODEL), _row_spec(tm, E_A), _const_spec((E_A, D_MODEL)),
                  _const_spec((1, D_MODEL)), _const_spec((1, D_MODEL)), _const_spec((D_MODEL, KR_PAD)),
                  _const_spec((1, KV_RANK)), _const_spec((KV_RANK, QK_PAD)), _const_spec((KV_RANK, E_B)),
                  _const_spec((D_MODEL, Q_RANK + E_B)), _const_spec((1, Q_RANK)), _const_spec((Q_RANK, QK_PAD)),
                  _row_spec(tm, HEAD_PAD), _row_spec(tm, HEAD_PAD), _row_spec(tm, HEAD_PAD)],
        out_specs=[_row_spec(tm, D_MODEL), _row_spec(tm, QK_PAD), _row_spec(tm, QK_PAD), _row_spec(tm, E_B),
                   _row_spec(tm, E_B), _row_spec(tm, Q_RANK), _row_spec(tm, KV_RANK)],
        out_shape=[sds(D_MODEL, F32), sds(QK_PAD, BF16), sds(QK_PAD, BF16), sds(E_B, BF16), sds(E_B, BF16),
                   sds(Q_RANK, BF16), sds(KV_RANK, BF16)],
        compiler_params=_params("parallel"),
    )(x, ym, w_out, gk, gb, w_dkv, gc, w_uk, w_uv, w_bin, gq, w_uq, rc, rs1, rs2)


def _pair_specs(seq):
    qk = pl.BlockSpec((seq, 2 * HEAD_PAD), lambda b, p: (b, p))
    vo = pl.BlockSpec((seq, 2 * V_HEAD), lambda b, p: (b, p))
    st = pl.BlockSpec((None, 2, seq), lambda b, p: (p, 0, b))
    return qk, vo, st


def _attn_fwd(q, k, v, seq):
    t = q.shape[0]
    tq = min(ATT_TILE_FWD, seq)
    nq = seq // tq

    def body(q_ref, k_ref, v_ref, o_ref, lse_ref, m_scr, l_scr, acc_scr):
        lane = lax.broadcasted_iota(jnp.int32, (tq, 2 * V_HEAD), 1)

        def q_step(qi, _):
            q0 = pl.multiple_of(qi * tq, tq)
            m_scr[...] = jnp.full(m_scr.shape, -jnp.inf, F32)
            l_scr[...] = jnp.zeros_like(l_scr)
            acc_scr[...] = jnp.zeros_like(acc_scr)

            def block(q_lo, q_n, k0, k_n, masked):
                rows = slice(q_lo, q_lo + q_n)
                vt = v_ref[pl.ds(k0, k_n), :]

                def score(hh):
                    hs = slice(hh * HEAD_PAD, (hh + 1) * HEAD_PAD)
                    return _dot_nt(q_ref[pl.ds(q0 + q_lo, q_n), hs], k_ref[pl.ds(k0, k_n), hs])

                early = [score(hh) for hh in range(2)] if masked else None
                for hh in range(2):
                    s = early[hh] if masked else score(hh)
                    if masked:
                        row = lax.broadcasted_iota(jnp.int32, (q_n, k_n), 0)
                        col = lax.broadcasted_iota(jnp.int32, (q_n, k_n), 1)
                        s = jnp.where(col <= row, s, -jnp.inf)
                    m_old = m_scr[hh, rows]
                    m_new = jnp.maximum(m_old, jnp.max(s, axis=-1, keepdims=True))
                    alpha = jnp.exp2(m_old - m_new)
                    ps = [jnp.exp2(s[:, j * LANES:(j + 1) * LANES] - m_new) for j in range(k_n // LANES)]
                    l_scr[hh, rows] = alpha * l_scr[hh, rows] + functools.reduce(lambda a, b: a + b, ps)
                    p = jnp.concatenate(ps, axis=-1).astype(BF16)
                    acc_scr[hh, rows] = alpha * acc_scr[hh, rows] + _dot(p, vt)
                    m_scr[hh, rows] = m_new

            def k_step(ki, _):
                block(0, tq, pl.multiple_of(ki * tq, tq), tq, False)
                return 0

            lax.fori_loop(0, qi, k_step, 0)
            half = tq // 2
            block(0, tq, q0, half, True)
            block(half, half, q0 + half, half, True)
            l0 = jnp.sum(l_scr[0], axis=-1, keepdims=True)
            l1 = jnp.sum(l_scr[1], axis=-1, keepdims=True)
            o_ref[pl.ds(q0, tq), :] = jnp.where(lane < V_HEAD, acc_scr[0] / l0, acc_scr[1] / l1).astype(BF16)
            stats = jnp.where(lane == 0, m_scr[0] + jnp.log2(l0), m_scr[1] + jnp.log2(l1)).T
            lse_ref[:, pl.ds(q0, tq)] = stats[0:2, :]
            return 0

        lax.fori_loop(0, nq, q_step, 0)

    qk, vo, st = _pair_specs(seq)
    return pl.pallas_call(
        body, grid=(t // seq, N_HEADS // 2), name="attn_fwd",
        in_specs=[qk, qk, vo], out_specs=[vo, st],
        out_shape=[jax.ShapeDtypeStruct((t, E_B), BF16), jax.ShapeDtypeStruct((N_HEADS // 2, 2, t), F32)],
        scratch_shapes=[pltpu.VMEM((2, tq, LANES), F32), pltpu.VMEM((2, tq, LANES), F32),
                        pltpu.VMEM((2, tq, 2 * V_HEAD), F32)],
        compiler_params=_params("parallel", "parallel"),
    )(q, k, v)


def _head_fwd_bwd(o, gate, x1, tgt, w_bout, gf):
    t = o.shape[0]
    tm = min(ROW_TILE, t)

    def body(o_ref, gate_ref, x1_ref, tgt_ref, w_ref, gf_ref,
             dx2_ref, do_ref, dgate_ref, dd_ref, loss_ref, dgf_ref, dw_ref):
        @pl.when(pl.program_id(0) == 0)
        def _():
            loss_ref[...] = jnp.zeros_like(loss_ref)
            dgf_ref[...] = jnp.zeros_like(dgf_ref)
            dw_ref[...] = jnp.zeros_like(dw_ref)

        hm = tm // HEAD_CHAINS
        gf = gf_ref[...]
        lane = lax.broadcasted_iota(jnp.int32, (hm, 2 * V_HEAD), 1)
        chains = []
        for ch in range(HEAD_CHAINS):
            rs = pl.ds(ch * hm, hm)
            o = o_ref[rs, :].astype(F32)
            gt = gate_ref[rs, :].astype(F32)
            sg = _sigmoid(gt)
            silu = gt * sg
            z = (o * silu).astype(BF16)
            chains.append((rs, o, gt, sg, silu, z, x1_ref[rs, :] + _dot(z, w_ref[...])))
        mids = []
        for rs, o, gt, sg, silu, z, x2 in chains:
            r2, xh2 = _rms(x2)
            err = xh2 * gf - tgt_ref[rs, :]
            loss_ref[...] += 0.5 * jnp.sum(jnp.mean(err * err, axis=-1, keepdims=True))
            dy = err * (1.0 / D_MODEL)
            dgf_ref[...] += jnp.sum(dy * xh2, axis=0, keepdims=True)
            dx2 = _rms_bwd(dy * gf, xh2, r2)
            dx2_ref[rs, :] = dx2
            dx2b = dx2.astype(BF16)
            mids.append(_dot_nt(dx2b, w_ref[...]))
            dw_ref[...] += _dot_tn(z, dx2b)
        for (rs, o, gt, sg, silu, z, x2), dz in zip(chains, mids):
            do = dz * silu
            do_ref[rs, :] = do.astype(BF16)
            dgate_ref[rs, :] = (dz * o * (sg * (1.0 + gt * (1.0 - sg)))).astype(BF16)
            prod = do * o
            cols = jnp.zeros((hm, LANES), F32)
            for p in range(N_HEADS // 2):
                blk = prod[:, p * 2 * V_HEAD:(p + 1) * 2 * V_HEAD]
                d0 = jnp.sum(jnp.where(lane < V_HEAD, blk, 0.0), axis=-1, keepdims=True)
                d1 = jnp.sum(jnp.where(lane < V_HEAD, 0.0, blk), axis=-1, keepdims=True)
                cols = jnp.where(lane == 2 * p, d0, jnp.where(lane == 2 * p + 1, d1, cols))
            rows = cols.T
            for h in range(N_HEADS):
                dd_ref[h // 2, h % 2:h % 2 + 1, rs] = rows[h:h + 1, :]

    return pl.pallas_call(
        body, grid=(t // tm,), name="head_fwd_bwd",
        in_specs=[_row_spec(tm, E_B), _row_spec(tm, E_B), _row_spec(tm, D_MODEL), _row_spec(tm, D_MODEL),
                  _const_spec((E_B, D_MODEL)), _const_spec((1, D_MODEL))],
        out_specs=[_row_spec(tm, D_MODEL), _row_spec(tm, E_B), _row_spec(tm, E_B),
                   pl.BlockSpec((N_HEADS // 2, 2, tm), lambda i: (0, 0, i)),
                   _acc_spec((1, 1)), _acc_spec((1, D_MODEL)), _acc_spec((E_B, D_MODEL))],
        out_shape=[jax.ShapeDtypeStruct((t, D_MODEL), F32), jax.ShapeDtypeStruct((t, E_B), BF16),
                   jax.ShapeDtypeStruct((t, E_B), BF16), jax.ShapeDtypeStruct((N_HEADS // 2, 2, t), F32),
                   jax.ShapeDtypeStruct((1, 1), F32), jax.ShapeDtypeStruct((1, D_MODEL), F32),
                   jax.ShapeDtypeStruct((E_B, D_MODEL), F32)],
        compiler_params=_params("arbitrary"),
    )(o, gate, x1, tgt, w_bout, gf)


def _attn_bwd(q, k, v, do, lse, dd, seq):
    t = q.shape[0]
    tq = min(ATT_TILE, seq)
    nq = seq // tq
    assert nq % 2 == 0, "full tiles are taken in pairs"

    def body(q_ref, k_ref, v_ref, do_ref, lse_ref, dd_ref, dq_ref, dk_ref, dv_ref, dq_acc, dk_acc, dv_acc):
        dq_acc[...] = jnp.zeros_like(dq_acc)

        def k_step(ki, _):
            k0 = pl.multiple_of(ki * tq, tq)
            dk_acc[...] = jnp.zeros_like(dk_acc)
            dv_acc[...] = jnp.zeros_like(dv_acc)

            def block(k_lo, k_n, q0, q_n, masked):
                rows = slice(k_lo, k_lo + k_n)
                lane = lax.broadcasted_iota(jnp.int32, (q_n, 2 * V_HEAD), 1)
                vt = v_ref[pl.ds(k0 + k_lo, k_n), :]
                do_pair = do_ref[pl.ds(q0, q_n), :]

                def operands(hh):
                    hs = slice(hh * HEAD_PAD, (hh + 1) * HEAD_PAD)
                    kt = k_ref[pl.ds(k0 + k_lo, k_n), hs]
                    qt = q_ref[pl.ds(q0, q_n), hs]
                    mine = (lane < V_HEAD) if hh == 0 else (lane >= V_HEAD)
                    do_h = jnp.where(mine, do_pair, jnp.zeros((), BF16))
                    return hs, kt, qt, do_h, _dot_nt(kt, qt), _dot_nt(vt, do_h)

                early = [operands(hh) for hh in range(2)] if masked else None
                for hh in range(2):
                    hs, kt, qt, do_h, st, dpt = early[hh] if masked else operands(hh)
                    if masked:
                        krow = lax.broadcasted_iota(jnp.int32, (k_n, q_n), 0)
                        qcol = lax.broadcasted_iota(jnp.int32, (k_n, q_n), 1)
                        st = jnp.where(krow <= qcol, st, -jnp.inf)
                    pt = jnp.exp2(st - lse_ref[hh:hh + 1, pl.ds(q0, q_n)])
                    dst = (pt * (dpt - dd_ref[hh:hh + 1, pl.ds(q0, q_n)])).astype(BF16)
                    dv_acc[rows, :] += _dot(pt.astype(BF16), do_h)
                    dk_acc[rows, hs] += _dot(dst, qt)
                    dq_acc[pl.ds(q0, q_n), hs] += _dot_tn(dst, kt)

            wide = 2 * tq

            def q_step(qj, _):
                block(0, tq, pl.multiple_of(qj * wide, wide), wide, False)
                return 0

            half = tq // 2
            block(0, half, k0, tq, True)
            block(half, half, pl.multiple_of(k0 + half, half), half, True)

            @pl.when(ki % 2 == 0)
            def _():
                block(0, tq, pl.multiple_of(k0 + tq, tq), tq, False)

            lax.fori_loop(ki // 2 + 1, nq // 2, q_step, 0)
            dk_ref[pl.ds(k0, tq), :] = (dk_acc[...] * LN2).astype(BF16)
            dv_ref[pl.ds(k0, tq), :] = dv_acc[...].astype(BF16)
            return 0

        lax.fori_loop(0, nq, k_step, 0)
        dq_ref[...] = (dq_acc[...] * SOFTMAX_SCALE).astype(BF16)

    qk, vo, st = _pair_specs(seq)
    return pl.pallas_call(
        body, grid=(t // seq, N_HEADS // 2), name="attn_bwd",
        in_specs=[qk, qk, vo, vo, st, st], out_specs=[qk, qk, vo],
        out_shape=[jax.ShapeDtypeStruct((t, QK_PAD), BF16), jax.ShapeDtypeStruct((t, QK_PAD), BF16),
                   jax.ShapeDtypeStruct((t, E_B), BF16)],
        scratch_shapes=[pltpu.VMEM((seq, 2 * HEAD_PAD), F32), pltpu.VMEM((tq, 2 * HEAD_PAD), F32),
                        pltpu.VMEM((tq, 2 * V_HEAD), F32)],
        compiler_params=_params("parallel", "parallel"),
    )(q, k, v, do, lse, dd)


def _mid_bwd(dq, dk, dv, dgate, dx2, x1, cq, ckv, rc, rs1, rs2, w_uq, w_bin, w_uk, w_uv, w_dkv, gq, gc, gk, gb):
    t = dq.shape[0]
    tm = min(ROW_TILE, t)

    def body(dq_ref, dk_ref, dv_ref, dgate_ref, dx2_ref, x1_ref, cq_ref, ckv_ref, c_ref, s1_ref, s2_ref,
             wq_ref, wi_ref, wk_ref, wv_ref, wd_ref, gq_ref, gc_ref, gk_ref, gb_ref,
             dx1_ref, dwq_ref, dwi_ref, dwk_ref, dwv_ref, dwd_ref, dgq_ref, dgc_ref, dgk_ref, dgb_ref):
        @pl.when(pl.program_id(0) == 0)
        def _():
            for ref in (dwq_ref, dwi_ref, dwk_ref, dwv_ref, dwd_ref, dgq_ref, dgc_ref, dgk_ref, dgb_ref):
                ref[...] = jnp.zeros_like(ref)

        cb, s1b, s2b = c_ref[...], s1_ref[...], s2_ref[...]
        gk, gb, gq, gc = gk_ref[...], gb_ref[...], gq_ref[...], gc_ref[...]

        dkb, dvb = dk_ref[...], dv_ref[...]
        dckv = _dot_nt(dkb, wk_ref[...]) + _dot_nt(dvb, wv_ref[...])
        rcv, ch = _rms(ckv_ref[...].astype(F32))
        ckvn = (ch * gc).astype(BF16)
        dwk_ref[...] += _dot_tn(ckvn, dkb)
        dwv_ref[...] += _dot_tn(ckvn, dvb)

        dqs = [_rope_bwd(dq_ref[:, h * HEAD_PAD:(h + 1) * HEAD_PAD].astype(F32), cb, s1b, s2b)
               for h in range(N_HEADS)]
        dqb = jnp.concatenate(dqs, axis=-1).astype(BF16)
        rq, cqh = _rms(cq_ref[...].astype(F32))
        dcqn = _dot_nt(dqb, wq_ref[...])
        dwq_ref[...] += _dot_tn((cqh * gq).astype(BF16), dqb)

        dgc_ref[...] += jnp.sum(dckv * ch, axis=0, keepdims=True)
        dckv_raw = _rms_bwd(dckv * gc, ch, rcv)
        dkr = dk_ref[:, 0:HEAD_PAD].astype(F32)
        for h in range(1, N_HEADS):
            dkr = dkr + dk_ref[:, h * HEAD_PAD:(h + 1) * HEAD_PAD].astype(F32)
        dkr = _rope_bwd(dkr, cb, s1b, s2b)
        dckr = jnp.concatenate([dckv_raw, dkr], axis=-1).astype(BF16)
        r1, xh = _rms(x1_ref[...])
        dhk = _dot_nt(dckr, wd_ref[...])
        dwd_ref[...] += _dot_tn((xh * gk).astype(BF16), dckr)

        dgq_ref[...] += jnp.sum(dcqn * cqh, axis=0, keepdims=True)
        dcq = _rms_bwd(dcqn * gq, cqh, rq)
        dpb = jnp.concatenate([dcq.astype(BF16), dgate_ref[...]], axis=-1)
        dh1 = _dot_nt(dpb, wi_ref[...])
        dwi_ref[...] += _dot_tn((xh * gb).astype(BF16), dpb)

        dgb_ref[...] += jnp.sum(dh1 * xh, axis=0, keepdims=True)
        dgk_ref[...] += jnp.sum(dhk * xh, axis=0, keepdims=True)
        dx1_ref[...] = dx2_ref[...] + _rms_bwd(dh1 * gb + dhk * gk, xh, r1)

    acc_shapes = [(Q_RANK, QK_PAD), (D_MODEL, Q_RANK + E_B), (KV_RANK, QK_PAD), (KV_RANK, E_B), (D_MODEL, KR_PAD),
                  (1, Q_RANK), (1, KV_RANK), (1, D_MODEL), (1, D_MODEL)]
    return pl.pallas_call(
        body, grid=(t // tm,), name="mid_bwd",
        in_specs=[_row_spec(tm, QK_PAD), _row_spec(tm, QK_PAD), _row_spec(tm, E_B), _row_spec(tm, E_B),
                  _row_spec(tm, D_MODEL), _row_spec(tm, D_MODEL), _row_spec(tm, Q_RANK), _row_spec(tm, KV_RANK),
                  _row_spec(tm, HEAD_PAD), _row_spec(tm, HEAD_PAD), _row_spec(tm, HEAD_PAD),
                  _const_spec((Q_RANK, QK_PAD)), _const_spec((D_MODEL, Q_RANK + E_B)),
                  _const_spec((KV_RANK, QK_PAD)), _const_spec((KV_RANK, E_B)), _const_spec((D_MODEL, KR_PAD)),
                  _const_spec((1, Q_RANK)), _const_spec((1, KV_RANK)), _const_spec((1, D_MODEL)),
                  _const_spec((1, D_MODEL))],
        out_specs=[_row_spec(tm, D_MODEL)] + [_acc_spec(s) for s in acc_shapes],
        out_shape=[jax.ShapeDtypeStruct((t, D_MODEL), F32)] + [jax.ShapeDtypeStruct(s, F32) for s in acc_shapes],
        compiler_params=_params("arbitrary"),
    )(dq, dk, dv, dgate, dx2, x1, cq, ckv, rc, rs1, rs2, w_uq, w_bin, w_uk, w_uv, w_dkv, gq, gc, gk, gb)


def _chip_sums(phase, g, kinds, theirs, part, recd, mine, other, sems):
    send1, recv1, send3, recv3, load, local = sems
    n = len(g)
    shapes = [_sub_shape(a.shape, kd) for a, kd in zip(g, kinds)]
    units = [(k, j) for k in range(n) for j in range(N_CHIPS)]
    x, y, c = _place()
    j0 = 2 * x + y
    sibling = (x, y, 1 - c)
    halves = [_remote(_sub(g[k], kinds[k], j, 1 - c), theirs[k].at[j], send1.at[u], recv1.at[u], sibling)
              for u, (k, j) in enumerate(units)]

    def to_owner(u):
        k, j = units[u]
        return _remote(part[k].at[j], recd[k].at[j0], send3.at[u], recv3.at[4 * k + j0], (j // 2, j % 2, c))

    def keep(u):
        k, j = units[u]
        return pltpu.make_async_copy(part[k].at[j], recd[k].at[j], local.at[k])

    if phase == "begin":
        for cp in halves:
            cp.start()

        def loads(u):
            k, j = units[u]
            h, cols = shapes[k]
            return (pltpu.make_async_copy(_sub(g[k], kinds[k], j, c), mine.at[u % 2, pl.ds(0, h), pl.ds(0, cols)],
                                          load.at[u % 2]),
                    pltpu.make_async_copy(theirs[k].at[j], other.at[u % 2, pl.ds(0, h), pl.ds(0, cols)],
                                          load.at[2 + u % 2]))

        loads(0)[0].start()
        for u, (k, j) in enumerate(units):
            h, cols = shapes[k]
            if u + 1 < len(units):
                loads(u + 1)[0].start()
            halves[u].wait_recv()
            loads(u)[1].start()
            loads(u)[0].wait()
            loads(u)[1].wait()
            part[k][j] = (mine[u % 2, 0:h, 0:cols] + other[u % 2, 0:h, 0:cols]).astype(BF16)

            @pl.when(j != j0)
            def _():
                to_owner(u).start()

            @pl.when(j == j0)
            def _():
                keep(u).start()
    else:
        for u, (k, j) in enumerate(units):
            halves[u].wait_send()
            arrived = _remote(part[k].at[j], recd[k].at[j], send3.at[4 * k + j], recv3.at[4 * k + j], (j // 2, j % 2, c))

            @pl.when(j != j0)
            def _():
                arrived.wait_recv()
                to_owner(u).wait_send()

            @pl.when(j == j0)
            def _():
                keep(u).wait()


def _conv_bwd(dx1, x, c, u, v, ym, kb, kcv, kg, seq, w_out, w_in4, ga, cw, early, early_kinds):
    t = x.shape[0]
    tm = min(CONV_BWD_TILE, seq)
    tiles_per_seq = seq // tm
    n = t // tm

    def tile(i):
        return n - 1 - i

    def rev(width):
        return pl.BlockSpec((tm, width), lambda i: (tile(i), 0))

    ne = len(early)
    eshapes = [_sub_shape(a.shape, kd) for a, kd in zip(early, early_kinds)]
    ebig = (max(sh[0] for sh in eshapes), max(sh[1] for sh in eshapes))

    def body(dx1_ref, x_ref, c_ref, u_ref, v_ref, ym_ref, kb_ref, kcv_ref, kg_ref, wo_ref, wi_ref, ga_ref, cw_ref, *rest):
        eg, (dx_ref, dwi_ref, dwo_ref, dcw_ref, dga_ref) = rest[:ne], rest[ne:ne + 5]
        theirs, recd = rest[ne + 5:2 * ne + 5], rest[2 * ne + 5:3 * ne + 5]
        carry_ref, part = rest[3 * ne + 5], rest[3 * ne + 6:4 * ne + 6]
        mine, other = rest[4 * ne + 6], rest[4 * ne + 7]
        sems = rest[4 * ne + 8:]
        i = pl.program_id(0)
        j = tile(i)

        @pl.when(i == 0)
        def _():
            _chip_sums("begin", eg, early_kinds, theirs, part, recd, mine, other, sems)

        @pl.when(i == n - 1)
        def _():
            _chip_sums("end", eg, early_kinds, theirs, part, recd, mine, other, sems)

        @pl.when(i == 0)
        def _():
            for ref in (dwi_ref, dwo_ref, dcw_ref, dga_ref):
                ref[...] = jnp.zeros_like(ref)

        @pl.when(j % tiles_per_seq == tiles_per_seq - 1)
        def _():
            carry_ref[...] = jnp.zeros_like(carry_ref)

        dx1 = dx1_ref[...]
        dx1b = dx1.astype(BF16)
        dym = _dot_nt(dx1b, wo_ref[...])
        dwo_ref[...] += _dot_tn(ym_ref[...], dx1b)
        db = dym * kb_ref[...].astype(F32)
        dcv = dym * kcv_ref[...].astype(F32)
        dg = dym * kg_ref[...].astype(F32)

        row = lax.broadcasted_iota(jnp.int32, (tm, 1), 0)
        w0, w1, w2 = cw_ref[0:1, :], cw_ref[1:2, :], cw_ref[2:3, :]
        nxt = carry_ref[...]
        n0, n1 = nxt[0:1, :], nxt[1:2, :]
        d1 = jnp.where(row == tm - 1, n0, pltpu.roll(dcv, tm - 1, 0))
        d2 = jnp.where(row == tm - 1, n1, jnp.where(row == tm - 2, n0, pltpu.roll(dcv, tm - 2, 0)))
        carry_ref[...] = dcv[0:8, :]
        dv = w2 * dcv + w1 * d1 + w0 * d2
        v = v_ref[...].astype(F32)
        dcw_ref[0:1, :] += jnp.sum(d2 * v, axis=0, keepdims=True)
        dcw_ref[1:2, :] += jnp.sum(d1 * v, axis=0, keepdims=True)
        dcw_ref[2:3, :] += jnp.sum(dcv * v, axis=0, keepdims=True)

        r0, xh = _rms(x_ref[...])
        ga = ga_ref[...]
        h = (xh * ga).astype(BF16)
        dh = jnp.zeros((tm, D_MODEL), F32)
        for idx, dpart in enumerate((db, dv * u_ref[...].astype(F32), dv * c_ref[...].astype(F32), dg)):
            dpb = dpart.astype(BF16)
            dh = dh + _dot_nt(dpb, wi_ref[idx])
            dwi_ref[idx] += _dot_tn(h, dpb)
        dga_ref[...] += jnp.sum(dh * xh, axis=0, keepdims=True)
        dx_ref[...] = dx1 + _rms_bwd(dh * ga, xh, r0)

    acc_shapes = [(4, D_MODEL, E_A), (E_A, D_MODEL), (8, E_A), (1, D_MODEL)]
    res = pl.pallas_call(
        body, grid=(n,), name="conv_bwd",
        in_specs=[rev(D_MODEL), rev(D_MODEL)] + [rev(E_A)] * 7
        + [_const_spec((E_A, D_MODEL)), _const_spec((4, D_MODEL, E_A)), _const_spec((1, D_MODEL)),
           _const_spec((8, E_A))] + [HBM] * ne,
        out_specs=[rev(D_MODEL)] + [_acc_spec(sh) for sh in acc_shapes] + [HBM] * (2 * ne),
        out_shape=[jax.ShapeDtypeStruct((t, D_MODEL), F32)] + [jax.ShapeDtypeStruct(sh, F32) for sh in acc_shapes]
        + [jax.ShapeDtypeStruct((N_CHIPS,) + sh, F32) for sh in eshapes]
        + [jax.ShapeDtypeStruct((N_CHIPS,) + sh, BF16) for sh in eshapes],
        scratch_shapes=[pltpu.VMEM((8, E_A), F32)] + [pltpu.VMEM((N_CHIPS,) + sh, BF16) for sh in eshapes]
        + [pltpu.VMEM((2,) + ebig, F32)] * 2
        + [pltpu.SemaphoreType.DMA((4 * ne,))] * 4 + [pltpu.SemaphoreType.DMA((4,)), pltpu.SemaphoreType.DMA((ne,))],
        compiler_params=_params("arbitrary"),
    )(dx1, x, c, u, v, ym, kb, kcv, kg, w_out, w_in4, ga, cw, *early)
    return res[:5], res[5 + ne:]


WEIGHTS = ("a_norm", "a_w_in", "a_conv", "a_w_out", "kv_norm", "w_dkv", "ckv_norm", "w_ukv", "b_norm", "b_w_in",
           "b_q_norm", "b_w_uq", "b_w_out", "final_norm")
SHARD_SHAPES = {
    "a_norm": (1, 256), "a_w_in": (1, 1024, 1024), "a_conv": (1, 3, 256), "a_w_out": (1, 256, 1024),
    "kv_norm": (1024,), "w_dkv": (256, 288), "ckv_norm": (256,), "w_ukv": (256, 256), "b_norm": (1, 1024),
    "b_w_in": (1, 256, 896), "b_q_norm": (1, 384), "b_w_uq": (1, 384, 192), "b_w_out": (1, 512, 256),
    "final_norm": (1024,),
}
MATS = ("a_w_in", "a_w_out", "w_dkv", "w_ukv", "b_w_in", "b_w_uq", "b_w_out")
SMALL = ("a_norm", "a_conv", "kv_norm", "ckv_norm", "b_norm", "b_q_norm", "final_norm")
SMALL_FULL = {"a_norm": 1024, "a_conv": 3072, "kv_norm": 1024, "ckv_norm": 256, "b_norm": 1024, "b_q_norm": 384,
              "final_norm": 1024}
SMALL_ROWS = 8
LOSS_SLOT = sum(SMALL_FULL.values())


def _mat2d(name, a):
    return a.reshape(SHARD_SHAPES[name][-2:])


def _prep_first(g_win, gsmall):
    sm = gsmall.reshape(N_CHIPS, -1)
    a_conv = jnp.transpose(sm[:, 256:1024].reshape(N_CHIPS, CONV_WIDTH, 256), (1, 0, 2)).reshape(CONV_WIDTH, -1)
    return {"w_in4": g_win, "ga": sm[:, :256].reshape(1, -1), "cw": jnp.pad(a_conv, ((0, 8 - CONV_WIDTH), (0, 0)))}


def _prep_rest(gath, w):
    def cols(a):
        return jnp.transpose(a, (1, 0, 2)).reshape(a.shape[1], -1)

    def pad_heads(a, width):
        a = a.reshape(a.shape[0], N_HEADS, width)
        return jnp.pad(a, ((0, 0), (0, 0), (0, HEAD_PAD - width))).reshape(a.shape[0], QK_PAD)

    row = lambda a: a.reshape(1, -1).astype(F32)
    w_dkv = gath["w_dkv"].reshape(D_MODEL, KV_RANK + QK_ROPE)
    w_ukv = cols(gath["w_ukv"]).reshape(KV_RANK, N_HEADS, 2, QK_NOPE)
    return {
        "w_out": gath["a_w_out"].reshape(E_A, D_MODEL),
        "w_dkv": jnp.concatenate([w_dkv[:, :KV_RANK], jnp.zeros((D_MODEL, ROPE_LO), BF16), w_dkv[:, KV_RANK:],
                                  jnp.zeros((D_MODEL, HEAD_PAD - ROPE_LO - QK_ROPE), BF16)], axis=1),
        "w_uk": pad_heads(w_ukv[:, :, 0, :].reshape(KV_RANK, N_HEADS * QK_NOPE), QK_NOPE),
        "w_uv": w_ukv[:, :, 1, :].reshape(KV_RANK, E_B),
        "w_bin": gath["b_w_in"].reshape(D_MODEL, Q_RANK + E_B),
        "w_uq": pad_heads(cols(gath["b_w_uq"]), QK_NOPE + QK_ROPE),
        "w_bout": cols(gath["b_w_out"]),
        "gk": row(w["kv_norm"]), "gc": row(w["ckv_norm"]), "gb": row(w["b_norm"]),
        "gq": row(w["b_q_norm"]), "gf": row(w["final_norm"]),
    }


TRANSPOSED = ("w_dkv", "b_w_uq")
EARLY_MATS = ("w_dkv", "w_uk", "w_uv", "w_bin", "w_uq", "w_bout")
LATE_MATS = ("w_in4", "w_out")
GRAD_MATS = ("w_in4", "w_out", "w_dkv", "w_uk", "w_uv", "w_bin", "w_uq", "w_bout")
GRAD_KIND = {"w_in4": "lead", "w_out": "row", "w_dkv": "row", "w_uk": "col", "w_uv": "col", "w_bin": "row",
             "w_uq": "col", "w_bout": "col"}


def _local_step(x, positions, tgt, w):
    bsz, seq, _ = x.shape
    t = bsz * seq
    x2d = x.reshape(t, D_MODEL)
    small = jnp.concatenate([w["a_norm"].reshape(-1), w["a_conv"].reshape(-1)]).reshape(8, LANES)
    rc, rs1, rs2, g_win, gsmall = _rope_tables_gather(positions.reshape(t, 1), _mat2d(MATS[0], w[MATS[0]]), small)
    wk = _prep_first(g_win, gsmall)
    (c, u, v, ym, kb, kcv, kg), gathered = _conv_fwd(x2d, seq, wk["ga"], wk["w_in4"], wk["cw"],
                                           [_mat2d(n, w[n]) for n in MATS[1:]])
    wk.update(_prep_rest(dict(zip(MATS[1:], gathered)), w))
    x1, q, k, vv, gate, cq, ckv = _mid_fwd(x2d, ym, wk["w_out"], wk["gk"], wk["gb"], wk["w_dkv"], wk["gc"], wk["w_uk"],
                                          wk["w_uv"], wk["w_bin"], wk["gq"], wk["w_uq"], rc, rs1, rs2)
    o, lse = _attn_fwd(q, k, vv, seq)
    dx2, do, dgate, dd, loss, dgf, dw_bout = _head_fwd_bwd(o, gate, x1, tgt.reshape(t, D_MODEL), wk["w_bout"], wk["gf"])
    dq, dk, dv = _attn_bwd(q, k, vv, do, lse, dd, seq)
    dx1, dwq, dw_bin, dwk, dwv, dwd, dgq, dgc, dgk, dgb = _mid_bwd(
        dq, dk, dv, dgate, dx2, x1, cq, ckv, rc, rs1, rs2, wk["w_uq"], wk["w_bin"], wk["w_uk"], wk["w_uv"], wk["w_dkv"],
        wk["gq"], wk["gc"], wk["gk"], wk["gb"])
    early = {"w_dkv": dwd, "w_uk": dwk, "w_uv": dwv, "w_bin": dw_bin, "w_uq": dwq, "w_bout": dw_bout}
    (dx, dw_in4, dw_out, dcw, dga), summed = _conv_bwd(
        dx1, x2d, c, u, v, ym, kb, kcv, kg, seq, wk["w_out"], wk["w_in4"], wk["ga"], wk["cw"],
        [early[n] for n in EARLY_MATS], [GRAD_KIND[n] for n in EARLY_MATS])
    mats = {"w_in4": dw_in4, "w_out": dw_out, **dict(zip(EARLY_MATS, summed))}
    small = {"a_norm": dga, "a_conv": dcw[:CONV_WIDTH], "kv_norm": dgk, "ckv_norm": dgc, "b_norm": dgb,
             "b_q_norm": dgq, "final_norm": dgf}
    return loss[0, 0], dx.reshape(bsz, seq, D_MODEL), mats, small


def _shard_grads(sh, svec):
    j0 = 2 * lax.axis_index("x") + lax.axis_index("y")
    flat = svec.reshape(-1)
    off, small = 0, {}
    for n in SMALL:
        small[n] = flat[off:off + SMALL_FULL[n]]
        off += SMALL_FULL[n]
    dwd, dwk, dwv, dwq = sh["w_dkv"], sh["w_uk"], sh["w_uv"], sh["w_uq"]
    w_ukv = jnp.stack([dwk.reshape(KV_RANK, 2, HEAD_PAD)[:, :, :QK_NOPE], dwv.reshape(KV_RANK, 2, V_HEAD)], axis=2)
    return {
        "a_norm": lax.dynamic_slice(small["a_norm"], (j0 * 256,), (256,)),
        "a_conv": lax.dynamic_slice(small["a_conv"].reshape(CONV_WIDTH, E_A), (0, j0 * 256), (CONV_WIDTH, 256)),
        "kv_norm": small["kv_norm"], "ckv_norm": small["ckv_norm"], "b_norm": small["b_norm"],
        "b_q_norm": small["b_q_norm"], "final_norm": small["final_norm"],
        "a_w_in": sh["w_in4"], "a_w_out": sh["w_out"],
        "w_dkv": jnp.concatenate([dwd[:, :KV_RANK], dwd[:, KV_RANK + ROPE_LO:KV_RANK + ROPE_LO + QK_ROPE]], axis=1),
        "w_ukv": w_ukv.reshape(KV_RANK, 2 * (QK_NOPE + V_HEAD)),
        "b_w_in": sh["w_bin"],
        "b_w_uq": dwq.reshape(Q_RANK, 2, HEAD_PAD)[:, :, :QK_NOPE + QK_ROPE].reshape(Q_RANK, -1),
        "b_w_out": sh["w_bout"],
    }


def _sub(ref, kind, j, cc):
    if kind == "lead":
        h = ref.shape[1] // 2
        return ref.at[j, pl.ds(pl.multiple_of(cc * h, 8), h), :]
    if kind == "row":
        rows = ref.shape[0] // N_CHIPS
        h = rows // 2
        return ref.at[pl.ds(pl.multiple_of(j * rows + cc * h, 8), h), :]
    cols = ref.shape[1] // N_CHIPS
    h = ref.shape[0] // 2
    return ref.at[pl.ds(pl.multiple_of(cc * h, 8), h), pl.ds(j * cols, cols)]


def _sub_shape(shape, kind):
    if kind == "lead":
        return (shape[1] // 2, shape[2])
    if kind == "row":
        return (shape[0] // N_CHIPS // 2, shape[1])
    return (shape[0] // 2, shape[1] // N_CHIPS)


def _reduce_grads(grads, kinds, vec, pre):
    n = len(grads)
    npre = len(pre)
    shapes = [_sub_shape(a.shape, kd) for a, kd in zip(grads, kinds)]
    units = [(k, j) for k in range(n) for j in range(N_CHIPS)]
    big = (max(s[0] for s in shapes), max(s[1] for s in shapes))

    def body(*refs):
        refs = list(refs)

        def pop(count):
            head = refs[:count]
            del refs[:count]
            return head

        g, (v_ref,), pre_in = pop(n), pop(1), pop(npre)
        out, (o_ref,), pre_out = pop(n), pop(1), pop(npre)
        theirs, part, recd, red = pop(n), pop(n), pop(n), pop(n)
        (mine, got), pre_red = pop(2), pop(npre)
        send1, recv1, send3, recv3, send5, recv5, load, local, send_v, recv_v = refs
        x, y, c = _place()
        j0 = 2 * x + y
        me = 2 * j0 + c
        sibling = (x, y, 1 - c)

        got[me] = v_ref[...]
        first, small_in = [], []
        for d in range(1, N_DEV):
            px, py, pc = x ^ (d >> 2), y ^ ((d >> 1) & 1), c ^ (d & 1)
            first.append(_remote(v_ref, got.at[me], send_v.at[d - 1], recv_v.at[d - 1], (px, py, pc)))
            small_in.append(_remote(v_ref, got.at[4 * px + 2 * py + pc], send_v.at[d - 1], recv_v.at[d - 1],
                                    (px, py, pc)))
        halves = [_remote(_sub(g[k], kinds[k], j, 1 - c), theirs[k].at[j], send1.at[u], recv1.at[u], sibling)
                  for u, (k, j) in enumerate(units)]
        for cp in first + halves:
            cp.start()

        def mine_load(u):
            k, j = units[u]
            h, cols = shapes[k]
            return pltpu.make_async_copy(_sub(g[k], kinds[k], j, c), mine.at[u % 2, pl.ds(0, h), pl.ds(0, cols)],
                                         load.at[u % 2])

        mine_load(0).start()
        for u, (k, j) in enumerate(units):
            h, cols = shapes[k]
            if u + 1 < len(units):
                mine_load(u + 1).start()
            mine_load(u).wait()
            halves[u].wait_recv()
            part[k][j] = (mine[u % 2, 0:h, 0:cols] + theirs[k][j]).astype(BF16)
            to_owner = _remote(part[k].at[j], recd[k].at[j0], send3.at[u], recv3.at[4 * k + j0], (j // 2, j % 2, c))

            @pl.when(j != j0)
            def _():
                to_owner.start()

            @pl.when(j == j0)
            def _():
                recd[k][j] = part[k][j]

        swaps = []
        for k in range(n):
            for j in range(N_CHIPS):
                arrived = _remote(part[k].at[j], recd[k].at[j], send3.at[4 * k + j], recv3.at[4 * k + j],
                                  (j // 2, j % 2, c))

                @pl.when(j != j0)
                def _():
                    arrived.wait_recv()

            r = recd[k]
            red[k][...] = ((r[0].astype(F32) + r[1].astype(F32)) + r[2].astype(F32)) + r[3].astype(F32)
            own = pltpu.make_async_copy(red[k], out[k].at[c], local.at[k])
            give = _remote(red[k], out[k].at[c], send5.at[k], recv5.at[k], sibling)
            take = _remote(red[k], out[k].at[1 - c], send5.at[k], recv5.at[k], sibling)
            own.start()
            give.start()
            swaps.append((own, give, take))

        for k in range(npre):
            r = pre_in[k]
            pre_red[k][...] = ((r[0].astype(F32) + r[1].astype(F32)) + r[2].astype(F32)) + r[3].astype(F32)
            own = pltpu.make_async_copy(pre_red[k], pre_out[k].at[c], local.at[n + k])
            give = _remote(pre_red[k], pre_out[k].at[c], send5.at[n + k], recv5.at[n + k], sibling)
            take = _remote(pre_red[k], pre_out[k].at[1 - c], send5.at[n + k], recv5.at[n + k], sibling)
            own.start()
            give.start()
            swaps.append((own, give, take))

        for cp in small_in:
            cp.wait_recv()
            cp.wait_send()
        acc = got[0]
        for d in range(1, N_DEV):
            acc = acc + got[d]
        o_ref[...] = acc
        for u, (k, j) in enumerate(units):
            halves[u].wait_send()
            sent = _remote(part[k].at[j], recd[k].at[j0], send3.at[u], recv3.at[u], (j // 2, j % 2, c))

            @pl.when(j != j0)
            def _():
                sent.wait_send()
        for own, give, take in swaps:
            take.wait_recv()
            give.wait_send()
            own.wait()

    sems = [pltpu.SemaphoreType.DMA((len(units),))] * 4 + [pltpu.SemaphoreType.DMA((n + npre,))] * 2 \
        + [pltpu.SemaphoreType.DMA((2,)), pltpu.SemaphoreType.DMA((n + npre,))] + [pltpu.SemaphoreType.DMA((N_DEV - 1,))] * 2
    res = pl.pallas_call(
        body, name="reduce_grads",
        in_specs=[HBM] * n + [WHOLE] * (1 + npre), out_specs=[HBM] * n + [WHOLE] + [HBM] * npre,
        out_shape=[jax.ShapeDtypeStruct((2,) + sh, F32) for sh in shapes] + [jax.ShapeDtypeStruct(vec.shape, vec.dtype)]
        + [jax.ShapeDtypeStruct((2,) + a.shape[1:], F32) for a in pre],
        scratch_shapes=[pltpu.VMEM((N_CHIPS,) + sh, F32) for sh in shapes]
        + [pltpu.VMEM((N_CHIPS,) + sh, BF16) for sh in shapes] * 2
        + [pltpu.VMEM(sh, F32) for sh in shapes]
        + [pltpu.VMEM((2,) + big, F32), pltpu.VMEM((N_DEV,) + vec.shape, vec.dtype)]
        + [pltpu.VMEM(a.shape[1:], F32) for a in pre] + sems,
        compiler_params=_comm_params(),
    )(*grads, vec, *pre)
    return res[:n], res[n], res[n + 1:]


def _adamw_math(w, g, m, v):
    m = ADAM_B1 * m + (1.0 - ADAM_B1) * g
    v = ADAM_B2 * v + (1.0 - ADAM_B2) * (g * g)
    m_hat = m / (1.0 - ADAM_B1 ** ADAM_STEP)
    v_hat = v / (1.0 - ADAM_B2 ** ADAM_STEP)
    return -ADAM_LR * (m_hat / (jnp.sqrt(v_hat) + ADAM_EPS) + ADAM_WD * w), m, v


def _adamw_tiled(w, g, m, v):
    rows, width = w.shape
    tm = rows // 4

    def body(w_ref, g_ref, m_ref, v_ref, go_ref, d_ref, mo_ref, vo_ref):
        g = g_ref[...]
        go_ref[...] = g
        d_ref[...], mo_ref[...], vo_ref[...] = _adamw_math(w_ref[...], g, m_ref[...], v_ref[...])

    spec = pl.BlockSpec((tm, width), lambda i: (i, 0))
    out = jax.ShapeDtypeStruct((rows, width), F32)
    return pl.pallas_call(
        body, grid=(rows // tm,), name="adamw_tiled",
        in_specs=[spec] * 4, out_specs=[spec] * 4, out_shape=[out] * 4,
        compiler_params=_params("parallel"),
    )(w, g, m, v)


def _adamw_many(ws, gs, ms, vs):
    n = len(ws)

    def body(*refs):
        for k in range(n):
            w_ref, g_ref, m_ref, v_ref = (refs[i * n + k] for i in range(4))
            go_ref, d_ref, mo_ref, vo_ref = (refs[(4 + i) * n + k] for i in range(4))
            g = g_ref[...]
            go_ref[...] = g
            d_ref[...], mo_ref[...], vo_ref[...] = _adamw_math(w_ref[...], g, m_ref[...], v_ref[...])

    outs = [jax.ShapeDtypeStruct(a.shape, F32) for a in ws]
    res = pl.pallas_call(
        body, name="adamw_many",
        in_specs=[WHOLE] * (4 * n), out_specs=[WHOLE] * (4 * n), out_shape=outs * 4,
        compiler_params=_comm_params(),
    )(*ws, *gs, *ms, *vs)
    return res[:n], res[n:2 * n], res[2 * n:3 * n], res[3 * n:]


def kernel(x, positions, a_norm, a_w_in, a_conv, a_w_out, kv_norm, w_dkv, ckv_norm, w_ukv, b_norm, b_w_in, b_q_norm, b_w_uq, b_w_out, final_norm, loss_target, m_a_norm, m_a_w_in, m_a_conv, m_a_w_out, m_kv_norm, m_w_dkv, m_ckv_norm, m_w_ukv, m_b_norm, m_b_w_in, m_b_q_norm, m_b_w_uq, m_b_w_out, m_final_norm, v_a_norm, v_a_w_in, v_a_conv, v_a_w_out, v_kv_norm, v_w_dkv, v_ckv_norm, v_w_ukv, v_b_norm, v_b_w_in, v_b_q_norm, v_b_w_uq, v_b_w_out, v_final_norm):
    w = dict(a_norm=a_norm, a_w_in=a_w_in, a_conv=a_conv, a_w_out=a_w_out, kv_norm=kv_norm, w_dkv=w_dkv,
             ckv_norm=ckv_norm, w_ukv=w_ukv, b_norm=b_norm, b_w_in=b_w_in, b_q_norm=b_q_norm, b_w_uq=b_w_uq,
             b_w_out=b_w_out, final_norm=final_norm)
    m = dict(a_norm=m_a_norm, a_w_in=m_a_w_in, a_conv=m_a_conv, a_w_out=m_a_w_out, kv_norm=m_kv_norm, w_dkv=m_w_dkv,
             ckv_norm=m_ckv_norm, w_ukv=m_w_ukv, b_norm=m_b_norm, b_w_in=m_b_w_in, b_q_norm=m_b_q_norm,
             b_w_uq=m_b_w_uq, b_w_out=m_b_w_out, final_norm=m_final_norm)
    v = dict(a_norm=v_a_norm, a_w_in=v_a_w_in, a_conv=v_a_conv, a_w_out=v_a_w_out, kv_norm=v_kv_norm, w_dkv=v_w_dkv,
             ckv_norm=v_ckv_norm, w_ukv=v_w_ukv, b_norm=v_b_norm, b_w_in=v_b_w_in, b_q_norm=v_b_q_norm,
             b_w_uq=v_b_w_uq, b_w_out=v_b_w_out, final_norm=v_final_norm)

    loss, dx, gmat, gsmall = _local_step(x, positions, loss_target, w)

    kinds = [GRAD_KIND[n] for n in LATE_MATS]
    grads = [gmat[n] for n in LATE_MATS]
    flat = jnp.concatenate([gsmall[n].reshape(-1) for n in SMALL] + [loss.reshape(1)])
    flat = jnp.pad(flat, (0, SMALL_ROWS * PACK_W - flat.shape[0])).reshape(SMALL_ROWS, PACK_W)
    late, svec, early = _reduce_grads(grads, kinds, flat, [gmat[n] for n in EARLY_MATS])
    mine = dict(zip(LATE_MATS + EARLY_MATS, tuple(late) + tuple(early)))
    loss = svec.reshape(-1)[LOSS_SLOT]
    g = _shard_grads({n: a.reshape(-1, a.shape[-1]) for n, a in mine.items()}, svec)

    def two_d(n, a):
        a = a.reshape(-1, a.shape[-1])
        return a.T if n in TRANSPOSED else a

    def back(n, a):
        return (a.T if n in TRANSPOSED else a).reshape(SHARD_SHAPES[n])

    big = "a_w_in"
    rest = [n for n in WEIGHTS if n != big]
    res_big = _adamw_tiled(*(two_d(big, t[big]) for t in (w, g, m, v)))
    res_rest = _adamw_many(*([two_d(n, t[n]) for n in rest] for t in (w, g, m, v)))
    out = {kind: dict(zip(rest, res_rest[i])) for i, kind in enumerate("gdmv")}
    for i, kind in enumerate("gdmv"):
        out[kind][big] = res_big[i]
    return (loss, dx) + tuple(back(n, out[kind][n]) for kind in "gdmv" for n in WEIGHTS)
```

```python
import functools
import math

import numpy as np
import jax
import jax.numpy as jnp
from jax import lax
from jax.experimental import pallas as pl
from jax.experimental.pallas import tpu as pltpu

F32 = jnp.float32
BF16 = jnp.bfloat16

D_MODEL = 1024
E_A = 1024
CONV_WIDTH = 3
N_HEADS = 8
QK_NOPE = 64
QK_ROPE = 32
V_HEAD = 64
KV_RANK = 256
Q_RANK = 384
E_B = N_HEADS * V_HEAD
ROPE_THETA = 10000.0
SOFTMAX_SCALE = 1.0 / math.sqrt(QK_NOPE + QK_ROPE)
LOG2E = math.log2(math.e)
LN2 = math.log(2.0)
Q_PRESCALE = SOFTMAX_SCALE * LOG2E
EPS = 1e-6
HEAD_PAD = 128
QK_PAD = N_HEADS * HEAD_PAD
ROPE_LO = QK_NOPE
ROPE_HALF = QK_ROPE // 2
KR_PAD = KV_RANK + HEAD_PAD

ADAM_LR = 0.001
ADAM_B1 = 0.9
ADAM_B2 = 0.999
ADAM_EPS = 1e-08
ADAM_WD = 0.01
ADAM_STEP = 10

VMEM_LIMIT = 56 * 1024 * 1024
BIG_VMEM_LIMIT = 62 * 1024 * 1024
ROW_TILE = 512
CONV_BWD_TILE = 256
ATT_TILE_FWD = 1024
ATT_TILE = 512
HEAD_CHAINS = 2
LANES = 128
PACK_W = 1024

N_CHIPS = 4
N_DEV = 8


def _dot(a, b):
    return jnp.dot(a, b, preferred_element_type=F32)


def _dot_nt(a, b):
    return lax.dot_general(a, b, (((1,), (1,)), ((), ())), preferred_element_type=F32)


def _dot_tn(a, b):
    return lax.dot_general(a, b, (((0,), (0,)), ((), ())), preferred_element_type=F32)


def _rms(x):
    r = lax.rsqrt(jnp.mean(x * x, axis=-1, keepdims=True) + EPS)
    return r, x * r


def _rms_bwd(dxh, xh, r):
    return r * (dxh - xh * jnp.mean(dxh * xh, axis=-1, keepdims=True))


def _rope_fwd(a, c, s1, s2):
    return a * c + pltpu.roll(a, HEAD_PAD - ROPE_HALF, 1) * s1 + pltpu.roll(a, ROPE_HALF, 1) * s2


def _rope_bwd(g, c, s1, s2):
    return g * c + pltpu.roll(g * s1, ROPE_HALF, 1) + pltpu.roll(g * s2, HEAD_PAD - ROPE_HALF, 1)


def _sigmoid(x):
    return 1.0 / (1.0 + jnp.exp(-x))


def _row_spec(tm, n):
    return pl.BlockSpec((tm, n), lambda i: (i, 0))


def _const_spec(shape):
    nd = len(shape)
    return pl.BlockSpec(shape, lambda i: (0,) * nd, pipeline_mode=pl.Buffered(1))


def _acc_spec(shape):
    nd = len(shape)
    return pl.BlockSpec(shape, lambda i: (0,) * nd, pipeline_mode=pl.Buffered(1))


def _params(*sem):
    return pltpu.CompilerParams(dimension_semantics=sem, vmem_limit_bytes=VMEM_LIMIT)


MESH = pl.DeviceIdType.MESH
HBM = pl.BlockSpec(memory_space=pl.ANY)
WHOLE = pl.BlockSpec(memory_space=pltpu.VMEM)
FLIPS = ((1, 0), (0, 1), (1, 1))


def _place():
    return lax.axis_index("x"), lax.axis_index("y"), lax.axis_index("c")


def _remote(src, dst, send, recv, peer):
    return pltpu.make_async_remote_copy(src_ref=src, dst_ref=dst, send_sem=send, recv_sem=recv, device_id=peer,
                                        device_id_type=MESH)


def _comm_params():
    return pltpu.CompilerParams(vmem_limit_bytes=VMEM_LIMIT)


def _rope_consts():
    lane = np.arange(HEAD_PAD)
    first = (lane >= ROPE_LO) & (lane < ROPE_LO + ROPE_HALF)
    second = (lane >= ROPE_LO + ROPE_HALF) & (lane < ROPE_LO + QK_ROPE)
    f = np.where(first, lane - ROPE_LO, np.where(second, lane - ROPE_LO - ROPE_HALF, 0))
    inv = np.float32(ROPE_THETA) ** (-(2 * f).astype(np.float32) / np.float32(QK_ROPE))
    out = np.zeros((8, HEAD_PAD), np.float32)
    out[0] = inv
    out[1] = first
    out[2] = second
    out[3] = lane < ROPE_LO
    return jnp.asarray(out)


def _gather_plan(w, sm, outs, osm, bf, sems):
    n = len(w)
    send_i, recv_i, send_d, recv_d, send_s, recv_s, local = sems
    x, y, c = _place()
    j0 = 2 * x + y
    sibling = (x, y, 1 - c)
    own = [pltpu.make_async_copy(bf[k], outs[k].at[j0], local.at[k]) for k in range(n)]
    if sm is not None:
        own.append(pltpu.make_async_copy(sm, osm.at[j0], local.at[n]))

    def half(k, cc):
        h = w[k].shape[0] // 2
        return pl.ds(pl.multiple_of(cc * h, 16), h)

    sends, arrivals, forwards, fwaits = [], [], [], []
    for i, (fx, fy) in enumerate(FLIPS):
        px, py = x ^ fx, y ^ fy
        pj = 2 * px + py
        for k in range(n):
            s = i * n + k
            sends.append(_remote(bf[k].at[half(k, c)], outs[k].at[j0, half(k, c)], send_i.at[s], recv_i.at[s],
                                 (px, py, c)))
            arrivals.append(_remote(bf[k].at[half(k, c)], outs[k].at[pj, half(k, c)], send_i.at[s], recv_i.at[s],
                                    (px, py, c)))
            forwards.append(_remote(outs[k].at[pj, half(k, c)], outs[k].at[pj, half(k, c)], send_d.at[s],
                                    recv_d.at[s], sibling))
            fwaits.append(_remote(outs[k].at[pj, half(k, 1 - c)], outs[k].at[pj, half(k, 1 - c)], send_d.at[s],
                                  recv_d.at[s], sibling))
        if sm is not None:
            sends.append(_remote(sm, osm.at[j0], send_s.at[i], recv_s.at[i], (px, py, c)))
            fwaits.append(_remote(sm, osm.at[pj], send_s.at[i], recv_s.at[i], (px, py, c)))
    return own, sends, arrivals, forwards, fwaits


def _gather_begin(w, bf, plan):
    own, sends, _, _, _ = plan
    for k in range(len(w)):
        bf[k][...] = w[k][...].astype(BF16)
    for cp in own + sends:
        cp.start()


def _gather_end(plan):
    own, sends, arrivals, forwards, fwaits = plan
    for cp, fwd in zip(arrivals, forwards):
        cp.wait_recv()
        fwd.start()
    for cp in fwaits:
        cp.wait_recv()
    for cp in sends + forwards:
        cp.wait_send()
    for cp in own:
        cp.wait()


def _gather_sems(n, with_small):
    return ([pltpu.SemaphoreType.DMA((3 * n,))] * 4 + [pltpu.SemaphoreType.DMA((3,))] * 2
            + [pltpu.SemaphoreType.DMA((n + (1 if with_small else 0),))])


def _rope_tables_gather(pos_col, w_in, small):
    t = pos_col.shape[0]
    tm = min(ROW_TILE, t)
    steps = t // tm

    def body(p_ref, k_ref, w_ref, sm_ref, c_ref, s1_ref, s2_ref, out_ref, osm_ref, bf_ref, *sems):
        plan = _gather_plan([w_ref], sm_ref, [out_ref], osm_ref, [bf_ref], sems)
        i = pl.program_id(0)

        @pl.when(i == 0)
        def _():
            _gather_begin([w_ref], [bf_ref], plan)

        inv, first, second, nope = k_ref[0:1, :], k_ref[1:2, :], k_ref[2:3, :], k_ref[3:4, :]
        ang = p_ref[...].astype(F32) * inv
        cs, sn = jnp.cos(ang), jnp.sin(ang)
        c_ref[...] = cs * (first + second) + nope
        s1_ref[...] = -sn * first
        s2_ref[...] = sn * second

        @pl.when(i == steps - 1)
        def _():
            _gather_end(plan)

    out = jax.ShapeDtypeStruct((t, HEAD_PAD), F32)
    return pl.pallas_call(
        body, grid=(steps,), name="rope_tables_gather",
        in_specs=[_row_spec(tm, 1), _const_spec((8, HEAD_PAD)), WHOLE, WHOLE],
        out_specs=[_row_spec(tm, HEAD_PAD)] * 3 + [HBM, HBM],
        out_shape=[out] * 3 + [jax.ShapeDtypeStruct((N_CHIPS,) + w_in.shape, BF16),
                               jax.ShapeDtypeStruct((N_CHIPS,) + small.shape, small.dtype)],
        scratch_shapes=[pltpu.VMEM(w_in.shape, BF16)] + _gather_sems(1, True),
        compiler_params=_params("arbitrary"),
    )(pos_col, _rope_consts(), w_in, small)


def _shift_down(v, prev, row):
    p1, p2 = prev[7:8, :], prev[6:7, :]
    v1 = jnp.where(row == 0, p1, pltpu.roll(v, 1, 0))
    v2 = jnp.where(row == 0, p2, jnp.where(row == 1, p1, pltpu.roll(v, 2, 0)))
    return v1, v2


def _conv_fwd(x, seq, ga, w_in4, cw, shards):
    t = x.shape[0]
    tm = min(2 * ROW_TILE, seq)
    tiles_per_seq = seq // tm
    steps = t // tm
    n = len(shards)

    def body(x_ref, ga_ref, w_ref, cw_ref, *rest):
        sh, (c_ref, u_ref, ym_ref, kb_ref, kcv_ref, kg_ref) = rest[:n], rest[n:n + 6]
        outs, carry_ref, bf, sems = rest[n + 6:2 * n + 6], rest[2 * n + 6], rest[2 * n + 7:3 * n + 7], rest[3 * n + 7:]
        plan = _gather_plan(sh, None, outs, None, bf, sems)
        i = pl.program_id(0)

        @pl.when(i == 0)
        def _():
            _gather_begin(sh, bf, plan)

        @pl.when(i % tiles_per_seq == 0)
        def _():
            carry_ref[...] = jnp.zeros_like(carry_ref)

        _, xh = _rms(x_ref[...])
        h = (xh * ga_ref[...]).astype(BF16)
        c, u = _dot(h, w_ref[1]), _dot(h, w_ref[2])
        g, b = _dot(h, w_ref[3]), _dot(h, w_ref[0])
        v = c * u
        row = lax.broadcasted_iota(jnp.int32, (tm, 1), 0)
        v1, v2 = _shift_down(v, carry_ref[...], row)
        carry_ref[...] = v[tm - 8:tm, :]
        cv = cw_ref[2:3, :] * v + cw_ref[1:2, :] * v1 + cw_ref[0:1, :] * v2
        sg = _sigmoid(g)
        silu = g * sg
        c_ref[...] = c.astype(BF16)
        u_ref[...] = u.astype(BF16)
        ym_ref[...] = (silu * b * cv).astype(BF16)
        kb_ref[...] = (silu * cv).astype(BF16)
        kcv_ref[...] = (silu * b).astype(BF16)
        kg_ref[...] = (b * cv * (sg * (1.0 + g * (1.0 - sg)))).astype(BF16)

        @pl.when(i == steps - 1)
        def _():
            _gather_end(plan)

    out = jax.ShapeDtypeStruct((t, E_A), BF16)
    res = pl.pallas_call(
        body, grid=(steps,), name="conv_fwd",
        in_specs=[_row_spec(tm, D_MODEL), _const_spec((1, D_MODEL)), _const_spec((4, D_MODEL, E_A)),
                  _const_spec((8, E_A))] + [WHOLE] * n,
        out_specs=[_row_spec(tm, E_A)] * 6 + [HBM] * n,
        out_shape=[out] * 6 + [jax.ShapeDtypeStruct((N_CHIPS,) + a.shape, BF16) for a in shards],
        scratch_shapes=[pltpu.VMEM((8, E_A), F32)] + [pltpu.VMEM(a.shape, BF16) for a in shards]
        + _gather_sems(n, False),
        compiler_params=pltpu.CompilerParams(dimension_semantics=("arbitrary",), vmem_limit_bytes=BIG_VMEM_LIMIT),
    )(x, ga, w_in4, cw, *shards)
    return res[:6], res[6:]


def _mid_fwd(x, ym, w_out, gk, gb, w_dkv, gc, w_uk, w_uv, w_bin, gq, w_uq, rc, rs1, rs2):
    t = x.shape[0]
    tm = min(2 * ROW_TILE, t)

    def body(x_ref, ym_ref, wo_ref, gk_ref, gb_ref, wd_ref, gc_ref, wk_ref, wv_ref, wi_ref, gq_ref, wq_ref,
             c_ref, s1_ref, s2_ref, x1_ref, q_ref, k_ref, v_ref, gate_ref, cq_ref, ckv_ref):
        cb, s1b, s2b = c_ref[...], s1_ref[...], s2_ref[...]
        x1 = x_ref[...] + _dot(ym_ref[...], wo_ref[...])
        x1_ref[...] = x1
        _, xh = _rms(x1)
        hk = (xh * gk_ref[...]).astype(BF16)
        h1 = (xh * gb_ref[...]).astype(BF16)

        pb = _dot(h1, wi_ref[...])
        cq = pb[:, :Q_RANK]
        cq_ref[...] = cq.astype(BF16)
        gate_ref[...] = pb[:, Q_RANK:].astype(BF16)
        _, cqh = _rms(cq)
        q = _dot((cqh * gq_ref[...]).astype(BF16), wq_ref[...])
        for h in range(N_HEADS):
            sl = slice(h * HEAD_PAD, (h + 1) * HEAD_PAD)
            q_ref[:, sl] = (_rope_fwd(q[:, sl], cb, s1b, s2b) * Q_PRESCALE).astype(BF16)

        ckr = _dot(hk, wd_ref[...])
        ckv_raw = ckr[:, :KV_RANK]
        ckv_ref[...] = ckv_raw.astype(BF16)
        _, ch = _rms(ckv_raw)
        ckv = (ch * gc_ref[...]).astype(BF16)
        kr = _rope_fwd(ckr[:, KV_RANK:], cb, s1b, s2b)
        kn = _dot(ckv, wk_ref[...])
        for h in range(N_HEADS):
            sl = slice(h * HEAD_PAD, (h + 1) * HEAD_PAD)
            k_ref[:, sl] = (kn[:, sl] + kr).astype(BF16)
        v_ref[...] = _dot(ckv, wv_ref[...]).astype(BF16)

    def sds(n, dt):
        return jax.ShapeDtypeStruct((t, n), dt)

    return pl.pallas_call(
        body, grid=(t // tm,), name="mid_fwd",
        in_specs=[_row_spec(tm, D_MODEL), _row_spec(tm, E_A), _const_spec((E_A, D_MODEL)),
                  _const_spec((1, D_MODEL)), _const_spec((1, D_MODEL)), _const_spec((D_MODEL, KR_PAD)),
                  _const_spec((1, KV_RANK)), _const_spec((KV_RANK, QK_PAD)), _const_spec((KV_RANK, E_B)),
                  _const_spec((D_MODEL, Q_RANK + E_B)), _const_spec((1, Q_RANK)), _const_spec((Q_RANK, QK_PAD)),
                  _row_spec(tm, HEAD_PAD), _row_spec(tm, HEAD_PAD), _row_spec(tm, HEAD_PAD)],
        out_specs=[_row_spec(tm, D_MODEL), _row_spec(tm, QK_PAD), _row_spec(tm, QK_PAD), _row_spec(tm, E_B),
                   _row_spec(tm, E_B), _row_spec(tm, Q_RANK), _row_spec(tm, KV_RANK)],
        out_shape=[sds(D_MODEL, F32), sds(QK_PAD, BF16), sds(QK_PAD, BF16), sds(E_B, BF16), sds(E_B, BF16),
                   sds(Q_RANK, BF16), sds(KV_RANK, BF16)],
        compiler_params=_params("parallel"),
    )(x, ym, w_out, gk, gb, w_dkv, gc, w_uk, w_uv, w_bin, gq, w_uq, rc, rs1, rs2)


def _pair_specs(seq):
    qk = pl.BlockSpec((seq, 2 * HEAD_PAD), lambda b, p: (b, p))
    vo = pl.BlockSpec((seq, 2 * V_HEAD), lambda b, p: (b, p))
    st = pl.BlockSpec((None, 2, seq), lambda b, p: (p, 0, b))
    return qk, vo, st


def _attn_fwd(q, k, v, seq):
    t = q.shape[0]
    tq = min(ATT_TILE_FWD, seq)
    nq = seq // tq

    def body(q_ref, k_ref, v_ref, o_ref, lse_ref, m_scr, l_scr, acc_scr):
        lane = lax.broadcasted_iota(jnp.int32, (tq, 2 * V_HEAD), 1)

        def q_step(qi, _):
            q0 = pl.multiple_of(qi * tq, tq)
            m_scr[...] = jnp.full(m_scr.shape, -jnp.inf, F32)
            l_scr[...] = jnp.zeros_like(l_scr)
            acc_scr[...] = jnp.zeros_like(acc_scr)

            def block(q_lo, q_n, k0, k_n, masked):
                rows = slice(q_lo, q_lo + q_n)
                vt = v_ref[pl.ds(k0, k_n), :]

                def score(hh):
                    hs = slice(hh * HEAD_PAD, (hh + 1) * HEAD_PAD)
                    return _dot_nt(q_ref[pl.ds(q0 + q_lo, q_n), hs], k_ref[pl.ds(k0, k_n), hs])

                early = [score(hh) for hh in range(2)] if masked else None
                for hh in range(2):
                    s = early[hh] if masked else score(hh)
                    if masked:
                        row = lax.broadcasted_iota(jnp.int32, (q_n, k_n), 0)
                        col = lax.broadcasted_iota(jnp.int32, (q_n, k_n), 1)
                        s = jnp.where(col <= row, s, -jnp.inf)
                    m_old = m_scr[hh, rows]
                    m_new = jnp.maximum(m_old, jnp.max(s, axis=-1, keepdims=True))
                    alpha = jnp.exp2(m_old - m_new)
                    ps = [jnp.exp2(s[:, j * LANES:(j + 1) * LANES] - m_new) for j in range(k_n // LANES)]
                    l_scr[hh, rows] = alpha * l_scr[hh, rows] + functools.reduce(lambda a, b: a + b, ps)
                    p = jnp.concatenate(ps, axis=-1).astype(BF16)
                    acc_scr[hh, rows] = alpha * acc_scr[hh, rows] + _dot(p, vt)
                    m_scr[hh, rows] = m_new

            def k_step(ki, _):
                block(0, tq, pl.multiple_of(ki * tq, tq), tq, False)
                return 0

            lax.fori_loop(0, qi, k_step, 0)
            half = tq // 2
            block(0, tq, q0, half, True)
            block(half, half, q0 + half, half, True)
            l0 = jnp.sum(l_scr[0], axis=-1, keepdims=True)
            l1 = jnp.sum(l_scr[1], axis=-1, keepdims=True)
            o_ref[pl.ds(q0, tq), :] = jnp.where(lane < V_HEAD, acc_scr[0] / l0, acc_scr[1] / l1).astype(BF16)
            stats = jnp.where(lane == 0, m_scr[0] + jnp.log2(l0), m_scr[1] + jnp.log2(l1)).T
            lse_ref[:, pl.ds(q0, tq)] = stats[0:2, :]
            return 0

        lax.fori_loop(0, nq, q_step, 0)

    qk, vo, st = _pair_specs(seq)
    return pl.pallas_call(
        body, grid=(t // seq, N_HEADS // 2), name="attn_fwd",
        in_specs=[qk, qk, vo], out_specs=[vo, st],
        out_shape=[jax.ShapeDtypeStruct((t, E_B), BF16), jax.ShapeDtypeStruct((N_HEADS // 2, 2, t), F32)],
        scratch_shapes=[pltpu.VMEM((2, tq, LANES), F32), pltpu.VMEM((2, tq, LANES), F32),
                        pltpu.VMEM((2, tq, 2 * V_HEAD), F32)],
        compiler_params=_params("parallel", "parallel"),
    )(q, k, v)


def _head_fwd_bwd(o, gate, x1, tgt, w_bout, gf):
    t = o.shape[0]
    tm = min(2 * ROW_TILE, t)

    def body(o_ref, gate_ref, x1_ref, tgt_ref, w_ref, gf_ref,
             dx2_ref, do_ref, dgate_ref, dd_ref, loss_ref, dgf_ref, dw_ref):
        @pl.when(pl.program_id(0) == 0)
        def _():
            loss_ref[...] = jnp.zeros_like(loss_ref)
            dgf_ref[...] = jnp.zeros_like(dgf_ref)
            dw_ref[...] = jnp.zeros_like(dw_ref)

        hm = tm // HEAD_CHAINS
        gf = gf_ref[...]
        lane = lax.broadcasted_iota(jnp.int32, (hm, 2 * V_HEAD), 1)
        chains = []
        for ch in range(HEAD_CHAINS):
            rs = pl.ds(ch * hm, hm)
            o = o_ref[rs, :].astype(F32)
            gt = gate_ref[rs, :].astype(F32)
            sg = _sigmoid(gt)
            silu = gt * sg
            z = (o * silu).astype(BF16)
            chains.append((rs, o, gt, sg, silu, z, x1_ref[rs, :] + _dot(z, w_ref[...])))
        mids = []
        for rs, o, gt, sg, silu, z, x2 in chains:
            r2, xh2 = _rms(x2)
            err = xh2 * gf - tgt_ref[rs, :]
            loss_ref[...] += 0.5 * jnp.sum(jnp.mean(err * err, axis=-1, keepdims=True))
            dy = err * (1.0 / D_MODEL)
            dgf_ref[...] += jnp.sum(dy * xh2, axis=0, keepdims=True)
            dx2 = _rms_bwd(dy * gf, xh2, r2)
            dx2_ref[rs, :] = dx2
            dx2b = dx2.astype(BF16)
            mids.append(_dot_nt(dx2b, w_ref[...]))
            dw_ref[...] += _dot_tn(z, dx2b)
        for (rs, o, gt, sg, silu, z, x2), dz in zip(chains, mids):
            do = dz * silu
            do_ref[rs, :] = do.astype(BF16)
            dgate_ref[rs, :] = (dz * o * (sg * (1.0 + gt * (1.0 - sg)))).astype(BF16)
            prod = do * o
            cols = jnp.zeros((hm, LANES), F32)
            for p in range(N_HEADS // 2):
                blk = prod[:, p * 2 * V_HEAD:(p + 1) * 2 * V_HEAD]
                d0 = jnp.sum(jnp.where(lane < V_HEAD, blk, 0.0), axis=-1, keepdims=True)
                d1 = jnp.sum(jnp.where(lane < V_HEAD, 0.0, blk), axis=-1, keepdims=True)
                cols = jnp.where(lane == 2 * p, d0, jnp.where(lane == 2 * p + 1, d1, cols))
            rows = cols.T
            for h in range(N_HEADS):
                dd_ref[h // 2, h % 2:h % 2 + 1, rs] = rows[h:h + 1, :]

    return pl.pallas_call(
        body, grid=(t // tm,), name="head_fwd_bwd",
        in_specs=[_row_spec(tm, E_B), _row_spec(tm, E_B), _row_spec(tm, D_MODEL), _row_spec(tm, D_MODEL),
                  _const_spec((E_B, D_MODEL)), _const_spec((1, D_MODEL))],
        out_specs=[_row_spec(tm, D_MODEL), _row_spec(tm, E_B), _row_spec(tm, E_B),
                   pl.BlockSpec((N_HEADS // 2, 2, tm), lambda i: (0, 0, i)),
                   _acc_spec((1, 1)), _acc_spec((1, D_MODEL)), _acc_spec((E_B, D_MODEL))],
        out_shape=[jax.ShapeDtypeStruct((t, D_MODEL), F32), jax.ShapeDtypeStruct((t, E_B), BF16),
                   jax.ShapeDtypeStruct((t, E_B), BF16), jax.ShapeDtypeStruct((N_HEADS // 2, 2, t), F32),
                   jax.ShapeDtypeStruct((1, 1), F32), jax.ShapeDtypeStruct((1, D_MODEL), F32),
                   jax.ShapeDtypeStruct((E_B, D_MODEL), F32)],
        compiler_params=_params("arbitrary"),
    )(o, gate, x1, tgt, w_bout, gf)


def _attn_bwd(q, k, v, do, lse, dd, seq):
    t = q.shape[0]
    tq = min(ATT_TILE, seq)
    nq = seq // tq
    assert nq % 2 == 0, "full tiles are taken in pairs"

    def body(q_ref, k_ref, v_ref, do_ref, lse_ref, dd_ref, dq_ref, dk_ref, dv_ref, dq_acc, dk_acc, dv_acc):
        dq_acc[...] = jnp.zeros_like(dq_acc)

        def k_step(ki, _):
            k0 = pl.multiple_of(ki * tq, tq)
            dk_acc[...] = jnp.zeros_like(dk_acc)
            dv_acc[...] = jnp.zeros_like(dv_acc)

            def block(k_lo, k_n, q0, q_n, masked):
                rows = slice(k_lo, k_lo + k_n)
                lane = lax.broadcasted_iota(jnp.int32, (q_n, 2 * V_HEAD), 1)
                vt = v_ref[pl.ds(k0 + k_lo, k_n), :]
                do_pair = do_ref[pl.ds(q0, q_n), :]

                def operands(hh):
                    hs = slice(hh * HEAD_PAD, (hh + 1) * HEAD_PAD)
                    kt = k_ref[pl.ds(k0 + k_lo, k_n), hs]
                    qt = q_ref[pl.ds(q0, q_n), hs]
                    mine = (lane < V_HEAD) if hh == 0 else (lane >= V_HEAD)
                    do_h = jnp.where(mine, do_pair, jnp.zeros((), BF16))
                    return hs, kt, qt, do_h, _dot_nt(kt, qt), _dot_nt(vt, do_h)

                early = [operands(hh) for hh in range(2)] if masked else None
                for hh in range(2):
                    hs, kt, qt, do_h, st, dpt = early[hh] if masked else operands(hh)
                    if masked:
                        krow = lax.broadcasted_iota(jnp.int32, (k_n, q_n), 0)
                        qcol = lax.broadcasted_iota(jnp.int32, (k_n, q_n), 1)
                        st = jnp.where(krow <= qcol, st, -jnp.inf)
                    pt = jnp.exp2(st - lse_ref[hh:hh + 1, pl.ds(q0, q_n)])
                    dst = (pt * (dpt - dd_ref[hh:hh + 1, pl.ds(q0, q_n)])).astype(BF16)
                    dv_acc[rows, :] += _dot(pt.astype(BF16), do_h)
                    dk_acc[rows, hs] += _dot(dst, qt)
                    dq_acc[pl.ds(q0, q_n), hs] += _dot_tn(dst, kt)

            wide = 2 * tq

            def q_step(qj, _):
                block(0, tq, pl.multiple_of(qj * wide, wide), wide, False)
                return 0

            half = tq // 2
            block(0, half, k0, tq, True)
            block(half, half, pl.multiple_of(k0 + half, half), half, True)

            @pl.when(ki % 2 == 0)
            def _():
                block(0, tq, pl.multiple_of(k0 + tq, tq), tq, False)

            lax.fori_loop(ki // 2 + 1, nq // 2, q_step, 0)
            dk_ref[pl.ds(k0, tq), :] = (dk_acc[...] * LN2).astype(BF16)
            dv_ref[pl.ds(k0, tq), :] = dv_acc[...].astype(BF16)
            return 0

        lax.fori_loop(0, nq, k_step, 0)
        dq_ref[...] = (dq_acc[...] * SOFTMAX_SCALE).astype(BF16)

    qk, vo, st = _pair_specs(seq)
    return pl.pallas_call(
        body, grid=(t // seq, N_HEADS // 2), name="attn_bwd",
        in_specs=[qk, qk, vo, vo, st, st], out_specs=[qk, qk, vo],
        out_shape=[jax.ShapeDtypeStruct((t, QK_PAD), BF16), jax.ShapeDtypeStruct((t, QK_PAD), BF16),
                   jax.ShapeDtypeStruct((t, E_B), BF16)],
        scratch_shapes=[pltpu.VMEM((seq, 2 * HEAD_PAD), F32), pltpu.VMEM((tq, 2 * HEAD_PAD), F32),
                        pltpu.VMEM((tq, 2 * V_HEAD), F32)],
        compiler_params=_params("parallel", "parallel"),
    )(q, k, v, do, lse, dd)


def _mid_bwd(dq, dk, dv, dgate, dx2, x1, cq, ckv, rc, rs1, rs2, w_uq, w_bin, w_uk, w_uv, w_dkv, gq, gc, gk, gb):
    t = dq.shape[0]
    tm = min(ROW_TILE, t)

    def body(dq_ref, dk_ref, dv_ref, dgate_ref, dx2_ref, x1_ref, cq_ref, ckv_ref, c_ref, s1_ref, s2_ref,
             wq_ref, wi_ref, wk_ref, wv_ref, wd_ref, gq_ref, gc_ref, gk_ref, gb_ref,
             dx1_ref, dwq_ref, dwi_ref, dwk_ref, dwv_ref, dwd_ref, dgq_ref, dgc_ref, dgk_ref, dgb_ref):
        @pl.when(pl.program_id(0) == 0)
        def _():
            for ref in (dwq_ref, dwi_ref, dwk_ref, dwv_ref, dwd_ref, dgq_ref, dgc_ref, dgk_ref, dgb_ref):
                ref[...] = jnp.zeros_like(ref)

        cb, s1b, s2b = c_ref[...], s1_ref[...], s2_ref[...]
        gk, gb, gq, gc = gk_ref[...], gb_ref[...], gq_ref[...], gc_ref[...]

        dkb, dvb = dk_ref[...], dv_ref[...]
        dckv = _dot_nt(dkb, wk_ref[...]) + _dot_nt(dvb, wv_ref[...])
        rcv, ch = _rms(ckv_ref[...].astype(F32))
        ckvn = (ch * gc).astype(BF16)
        dwk_ref[...] += _dot_tn(ckvn, dkb)
        dwv_ref[...] += _dot_tn(ckvn, dvb)

        dqs = [_rope_bwd(dq_ref[:, h * HEAD_PAD:(h + 1) * HEAD_PAD].astype(F32), cb, s1b, s2b)
               for h in range(N_HEADS)]
        dqb = jnp.concatenate(dqs, axis=-1).astype(BF16)
        rq, cqh = _rms(cq_ref[...].astype(F32))
        dcqn = _dot_nt(dqb, wq_ref[...])
        dwq_ref[...] += _dot_tn((cqh * gq).astype(BF16), dqb)

        dgc_ref[...] += jnp.sum(dckv * ch, axis=0, keepdims=True)
        dckv_raw = _rms_bwd(dckv * gc, ch, rcv)
        dkr = dk_ref[:, 0:HEAD_PAD].astype(F32)
        for h in range(1, N_HEADS):
            dkr = dkr + dk_ref[:, h * HEAD_PAD:(h + 1) * HEAD_PAD].astype(F32)
        dkr = _rope_bwd(dkr, cb, s1b, s2b)
        dckr = jnp.concatenate([dckv_raw, dkr], axis=-1).astype(BF16)
        r1, xh = _rms(x1_ref[...])
        dhk = _dot_nt(dckr, wd_ref[...])
        dwd_ref[...] += _dot_tn((xh * gk).astype(BF16), dckr)

        dgq_ref[...] += jnp.sum(dcqn * cqh, axis=0, keepdims=True)
        dcq = _rms_bwd(dcqn * gq, cqh, rq)
        dpb = jnp.concatenate([dcq.astype(BF16), dgate_ref[...]], axis=-1)
        dh1 = _dot_nt(dpb, wi_ref[...])
        dwi_ref[...] += _dot_tn((xh * gb).astype(BF16), dpb)

        dgb_ref[...] += jnp.sum(dh1 * xh, axis=0, keepdims=True)
        dgk_ref[...] += jnp.sum(dhk * xh, axis=0, keepdims=True)
        dx1_ref[...] = dx2_ref[...] + _rms_bwd(dh1 * gb + dhk * gk, xh, r1)

    acc_shapes = [(Q_RANK, QK_PAD), (D_MODEL, Q_RANK + E_B), (KV_RANK, QK_PAD), (KV_RANK, E_B), (D_MODEL, KR_PAD),
                  (1, Q_RANK), (1, KV_RANK), (1, D_MODEL), (1, D_MODEL)]
    return pl.pallas_call(
        body, grid=(t // tm,), name="mid_bwd",
        in_specs=[_row_spec(tm, QK_PAD), _row_spec(tm, QK_PAD), _row_spec(tm, E_B), _row_spec(tm, E_B),
                  _row_spec(tm, D_MODEL), _row_spec(tm, D_MODEL), _row_spec(tm, Q_RANK), _row_spec(tm, KV_RANK),
                  _row_spec(tm, HEAD_PAD), _row_spec(tm, HEAD_PAD), _row_spec(tm, HEAD_PAD),
                  _const_spec((Q_RANK, QK_PAD)), _const_spec((D_MODEL, Q_RANK + E_B)),
                  _const_spec((KV_RANK, QK_PAD)), _const_spec((KV_RANK, E_B)), _const_spec((D_MODEL, KR_PAD)),
                  _const_spec((1, Q_RANK)), _const_spec((1, KV_RANK)), _const_spec((1, D_MODEL)),
                  _const_spec((1, D_MODEL))],
        out_specs=[_row_spec(tm, D_MODEL)] + [_acc_spec(s) for s in acc_shapes],
        out_shape=[jax.ShapeDtypeStruct((t, D_MODEL), F32)] + [jax.ShapeDtypeStruct(s, F32) for s in acc_shapes],
        compiler_params=_params("arbitrary"),
    )(dq, dk, dv, dgate, dx2, x1, cq, ckv, rc, rs1, rs2, w_uq, w_bin, w_uk, w_uv, w_dkv, gq, gc, gk, gb)


def _conv_bwd(dx1, x, c, u, ym, kb, kcv, kg, seq, w_out, w_in4, ga, cw):
    t = x.shape[0]
    tm = min(CONV_BWD_TILE, seq)
    tiles_per_seq = seq // tm
    n = t // tm

    def tile(i):
        return n - 1 - i

    def rev(width):
        return pl.BlockSpec((tm, width), lambda i: (tile(i), 0))

    def body(dx1_ref, x_ref, c_ref, u_ref, ym_ref, kb_ref, kcv_ref, kg_ref, wo_ref, wi_ref, ga_ref, cw_ref,
             dx_ref, dwi_ref, dwo_ref, dcw_ref, dga_ref, carry_ref):
        i = pl.program_id(0)
        j = tile(i)

        @pl.when(i == 0)
        def _():
            for ref in (dwi_ref, dwo_ref, dcw_ref, dga_ref):
                ref[...] = jnp.zeros_like(ref)

        @pl.when(j % tiles_per_seq == tiles_per_seq - 1)
        def _():
            carry_ref[...] = jnp.zeros_like(carry_ref)

        dx1 = dx1_ref[...]
        dx1b = dx1.astype(BF16)
        dym = _dot_nt(dx1b, wo_ref[...])
        dwo_ref[...] += _dot_tn(ym_ref[...], dx1b)
        db = dym * kb_ref[...].astype(F32)
        dcv = dym * kcv_ref[...].astype(F32)
        dg = dym * kg_ref[...].astype(F32)

        row = lax.broadcasted_iota(jnp.int32, (tm, 1), 0)
        w0, w1, w2 = cw_ref[0:1, :], cw_ref[1:2, :], cw_ref[2:3, :]
        nxt = carry_ref[...]
        n0, n1 = nxt[0:1, :], nxt[1:2, :]
        d1 = jnp.where(row == tm - 1, n0, pltpu.roll(dcv, tm - 1, 0))
        d2 = jnp.where(row == tm - 1, n1, jnp.where(row == tm - 2, n0, pltpu.roll(dcv, tm - 2, 0)))
        carry_ref[...] = dcv[0:8, :]
        dv = w2 * dcv + w1 * d1 + w0 * d2
        cc, uu = c_ref[...].astype(F32), u_ref[...].astype(F32)
        v = cc * uu
        dcw_ref[0:1, :] += jnp.sum(d2 * v, axis=0, keepdims=True)
        dcw_ref[1:2, :] += jnp.sum(d1 * v, axis=0, keepdims=True)
        dcw_ref[2:3, :] += jnp.sum(dcv * v, axis=0, keepdims=True)

        r0, xh = _rms(x_ref[...])
        ga = ga_ref[...]
        h = (xh * ga).astype(BF16)
        dh = jnp.zeros((tm, D_MODEL), F32)
        for idx, dpart in enumerate((db, dv * uu, dv * cc, dg)):
            dpb = dpart.astype(BF16)
            dh = dh + _dot_nt(dpb, wi_ref[idx])
            dwi_ref[idx] += _dot_tn(h, dpb)
        dga_ref[...] += jnp.sum(dh * xh, axis=0, keepdims=True)
        dx_ref[...] = dx1 + _rms_bwd(dh * ga, xh, r0)

    acc_shapes = [(4, D_MODEL, E_A), (E_A, D_MODEL), (8, E_A), (1, D_MODEL)]
    return pl.pallas_call(
        body, grid=(n,), name="conv_bwd",
        in_specs=[rev(D_MODEL), rev(D_MODEL)] + [rev(E_A)] * 6
        + [_const_spec((E_A, D_MODEL)), _const_spec((4, D_MODEL, E_A)), _const_spec((1, D_MODEL)),
           _const_spec((8, E_A))],
        out_specs=[rev(D_MODEL)] + [_acc_spec(s) for s in acc_shapes],
        out_shape=[jax.ShapeDtypeStruct((t, D_MODEL), F32)] + [jax.ShapeDtypeStruct(s, F32) for s in acc_shapes],
        scratch_shapes=[pltpu.VMEM((8, E_A), F32)],
        compiler_params=_params("arbitrary"),
    )(dx1, x, c, u, ym, kb, kcv, kg, w_out, w_in4, ga, cw)


WEIGHTS = ("a_norm", "a_w_in", "a_conv", "a_w_out", "kv_norm", "w_dkv", "ckv_norm", "w_ukv", "b_norm", "b_w_in",
           "b_q_norm", "b_w_uq", "b_w_out", "final_norm")
SHARD_SHAPES = {
    "a_norm": (1, 256), "a_w_in": (1, 1024, 1024), "a_conv": (1, 3, 256), "a_w_out": (1, 256, 1024),
    "kv_norm": (1024,), "w_dkv": (256, 288), "ckv_norm": (256,), "w_ukv": (256, 256), "b_norm": (1, 1024),
    "b_w_in": (1, 256, 896), "b_q_norm": (1, 384), "b_w_uq": (1, 384, 192), "b_w_out": (1, 512, 256),
    "final_norm": (1024,),
}
MATS = ("a_w_in", "a_w_out", "w_dkv", "w_ukv", "b_w_in", "b_w_uq", "b_w_out")
SMALL = ("a_norm", "a_conv", "kv_norm", "ckv_norm", "b_norm", "b_q_norm", "final_norm")
SMALL_FULL = {"a_norm": 1024, "a_conv": 3072, "kv_norm": 1024, "ckv_norm": 256, "b_norm": 1024, "b_q_norm": 384,
              "final_norm": 1024}
SMALL_ROWS = 8
LOSS_SLOT = sum(SMALL_FULL.values())


def _mat2d(name, a):
    return a.reshape(SHARD_SHAPES[name][-2:])


def _prep_first(g_win, gsmall):
    sm = gsmall.reshape(N_CHIPS, -1)
    a_conv = jnp.transpose(sm[:, 256:1024].reshape(N_CHIPS, CONV_WIDTH, 256), (1, 0, 2)).reshape(CONV_WIDTH, -1)
    return {"w_in4": g_win, "ga": sm[:, :256].reshape(1, -1), "cw": jnp.pad(a_conv, ((0, 8 - CONV_WIDTH), (0, 0)))}


def _prep_rest(gath, w):
    def cols(a):
        return jnp.transpose(a, (1, 0, 2)).reshape(a.shape[1], -1)

    def pad_heads(a, width):
        a = a.reshape(a.shape[0], N_HEADS, width)
        return jnp.pad(a, ((0, 0), (0, 0), (0, HEAD_PAD - width))).reshape(a.shape[0], QK_PAD)

    row = lambda a: a.reshape(1, -1).astype(F32)
    w_dkv = gath["w_dkv"].reshape(D_MODEL, KV_RANK + QK_ROPE)
    w_ukv = cols(gath["w_ukv"]).reshape(KV_RANK, N_HEADS, 2, QK_NOPE)
    return {
        "w_out": gath["a_w_out"].reshape(E_A, D_MODEL),
        "w_dkv": jnp.concatenate([w_dkv[:, :KV_RANK], jnp.zeros((D_MODEL, ROPE_LO), BF16), w_dkv[:, KV_RANK:],
                                  jnp.zeros((D_MODEL, HEAD_PAD - ROPE_LO - QK_ROPE), BF16)], axis=1),
        "w_uk": pad_heads(w_ukv[:, :, 0, :].reshape(KV_RANK, N_HEADS * QK_NOPE), QK_NOPE),
        "w_uv": w_ukv[:, :, 1, :].reshape(KV_RANK, E_B),
        "w_bin": gath["b_w_in"].reshape(D_MODEL, Q_RANK + E_B),
        "w_uq": pad_heads(cols(gath["b_w_uq"]), QK_NOPE + QK_ROPE),
        "w_bout": cols(gath["b_w_out"]),
        "gk": row(w["kv_norm"]), "gc": row(w["ckv_norm"]), "gb": row(w["b_norm"]),
        "gq": row(w["b_q_norm"]), "gf": row(w["final_norm"]),
    }


TRANSPOSED = ("w_dkv", "b_w_uq")
GRAD_MATS = ("w_in4", "w_out", "w_dkv", "w_uk", "w_uv", "w_bin", "w_uq", "w_bout")
GRAD_KIND = {"w_in4": "lead", "w_out": "row", "w_dkv": "row", "w_uk": "col", "w_uv": "col", "w_bin": "row",
             "w_uq": "col", "w_bout": "col"}


def _local_step(x, positions, tgt, w):
    bsz, seq, _ = x.shape
    t = bsz * seq
    x2d = x.reshape(t, D_MODEL)
    small = jnp.concatenate([w["a_norm"].reshape(-1), w["a_conv"].reshape(-1)]).reshape(8, LANES)
    rc, rs1, rs2, g_win, gsmall = _rope_tables_gather(positions.reshape(t, 1), _mat2d(MATS[0], w[MATS[0]]), small)
    wk = _prep_first(g_win, gsmall)
    (c, u, ym, kb, kcv, kg), gathered = _conv_fwd(x2d, seq, wk["ga"], wk["w_in4"], wk["cw"],
                                           [_mat2d(n, w[n]) for n in MATS[1:]])
    wk.update(_prep_rest(dict(zip(MATS[1:], gathered)), w))
    x1, q, k, vv, gate, cq, ckv = _mid_fwd(x2d, ym, wk["w_out"], wk["gk"], wk["gb"], wk["w_dkv"], wk["gc"], wk["w_uk"],
                                          wk["w_uv"], wk["w_bin"], wk["gq"], wk["w_uq"], rc, rs1, rs2)
    o, lse = _attn_fwd(q, k, vv, seq)
    dx2, do, dgate, dd, loss, dgf, dw_bout = _head_fwd_bwd(o, gate, x1, tgt.reshape(t, D_MODEL), wk["w_bout"], wk["gf"])
    dq, dk, dv = _attn_bwd(q, k, vv, do, lse, dd, seq)
    dx1, dwq, dw_bin, dwk, dwv, dwd, dgq, dgc, dgk, dgb = _mid_bwd(
        dq, dk, dv, dgate, dx2, x1, cq, ckv, rc, rs1, rs2, wk["w_uq"], wk["w_bin"], wk["w_uk"], wk["w_uv"], wk["w_dkv"],
        wk["gq"], wk["gc"], wk["gk"], wk["gb"])
    dx, dw_in4, dw_out, dcw, dga = _conv_bwd(dx1, x2d, c, u, ym, kb, kcv, kg, seq, wk["w_out"], wk["w_in4"], wk["ga"], wk["cw"])
    mats = {"w_in4": dw_in4, "w_out": dw_out, "w_dkv": dwd, "w_uk": dwk, "w_uv": dwv, "w_bin": dw_bin, "w_uq": dwq,
            "w_bout": dw_bout}
    small = {"a_norm": dga, "a_conv": dcw[:CONV_WIDTH], "kv_norm": dgk, "ckv_norm": dgc, "b_norm": dgb,
             "b_q_norm": dgq, "final_norm": dgf}
    return loss[0, 0], dx.reshape(bsz, seq, D_MODEL), mats, small


def _shard_grads(sh, svec):
    j0 = 2 * lax.axis_index("x") + lax.axis_index("y")
    flat = svec.reshape(-1)
    off, small = 0, {}
    for n in SMALL:
        small[n] = flat[off:off + SMALL_FULL[n]]
        off += SMALL_FULL[n]
    dwd, dwk, dwv, dwq = sh["w_dkv"], sh["w_uk"], sh["w_uv"], sh["w_uq"]
    w_ukv = jnp.stack([dwk.reshape(KV_RANK, 2, HEAD_PAD)[:, :, :QK_NOPE], dwv.reshape(KV_RANK, 2, V_HEAD)], axis=2)
    return {
        "a_norm": lax.dynamic_slice(small["a_norm"], (j0 * 256,), (256,)),
        "a_conv": lax.dynamic_slice(small["a_conv"].reshape(CONV_WIDTH, E_A), (0, j0 * 256), (CONV_WIDTH, 256)),
        "kv_norm": small["kv_norm"], "ckv_norm": small["ckv_norm"], "b_norm": small["b_norm"],
        "b_q_norm": small["b_q_norm"], "final_norm": small["final_norm"],
        "a_w_in": sh["w_in4"], "a_w_out": sh["w_out"],
        "w_dkv": jnp.concatenate([dwd[:, :KV_RANK], dwd[:, KV_RANK + ROPE_LO:KV_RANK + ROPE_LO + QK_ROPE]], axis=1),
        "w_ukv": w_ukv.reshape(KV_RANK, 2 * (QK_NOPE + V_HEAD)),
        "b_w_in": sh["w_bin"],
        "b_w_uq": dwq.reshape(Q_RANK, 2, HEAD_PAD)[:, :, :QK_NOPE + QK_ROPE].reshape(Q_RANK, -1),
        "b_w_out": sh["w_bout"],
    }


def _sub(ref, kind, j, cc):
    if kind == "lead":
        h = ref.shape[1] // 2
        return ref.at[j, pl.ds(pl.multiple_of(cc * h, 8), h), :]
    if kind == "row":
        rows = ref.shape[0] // N_CHIPS
        h = rows // 2
        return ref.at[pl.ds(pl.multiple_of(j * rows + cc * h, 8), h), :]
    cols = ref.shape[1] // N_CHIPS
    h = ref.shape[0] // 2
    return ref.at[pl.ds(pl.multiple_of(cc * h, 8), h), pl.ds(j * cols, cols)]


def _sub_shape(shape, kind):
    if kind == "lead":
        return (shape[1] // 2, shape[2])
    if kind == "row":
        return (shape[0] // N_CHIPS // 2, shape[1])
    return (shape[0] // 2, shape[1] // N_CHIPS)


def _reduce_grads(grads, kinds, vec):
    n = len(grads)
    shapes = [_sub_shape(a.shape, kd) for a, kd in zip(grads, kinds)]
    units = [(k, j) for k in range(n) for j in range(N_CHIPS)]
    big = (max(s[0] for s in shapes), max(s[1] for s in shapes))

    def body(*refs):
        g, v_ref = refs[:n], refs[n]
        out, o_ref = refs[n + 1:2 * n + 1], refs[2 * n + 1]
        theirs, part, recd, red = (refs[(2 + i) * n + 2:(3 + i) * n + 2] for i in range(4))
        mine, got = refs[6 * n + 2], refs[6 * n + 3]
        send1, recv1, send3, recv3, send5, recv5, load, local, send_v, recv_v = refs[6 * n + 4:]
        x, y, c = _place()
        j0 = 2 * x + y
        me = 2 * j0 + c
        sibling = (x, y, 1 - c)

        got[me] = v_ref[...]
        first, small_in = [], []
        for d in range(1, N_DEV):
            px, py, pc = x ^ (d >> 2), y ^ ((d >> 1) & 1), c ^ (d & 1)
            first.append(_remote(v_ref, got.at[me], send_v.at[d - 1], recv_v.at[d - 1], (px, py, pc)))
            small_in.append(_remote(v_ref, got.at[4 * px + 2 * py + pc], send_v.at[d - 1], recv_v.at[d - 1],
                                    (px, py, pc)))
        halves = [_remote(_sub(g[k], kinds[k], j, 1 - c), theirs[k].at[j], send1.at[u], recv1.at[u], sibling)
                  for u, (k, j) in enumerate(units)]
        for cp in first + halves:
            cp.start()

        def mine_load(u):
            k, j = units[u]
            h, cols = shapes[k]
            return pltpu.make_async_copy(_sub(g[k], kinds[k], j, c), mine.at[u % 2, pl.ds(0, h), pl.ds(0, cols)],
                                         load.at[u % 2])

        mine_load(0).start()
        for u, (k, j) in enumerate(units):
            h, cols = shapes[k]
            if u + 1 < len(units):
                mine_load(u + 1).start()
            mine_load(u).wait()
            halves[u].wait_recv()
            part[k][j] = (mine[u % 2, 0:h, 0:cols] + theirs[k][j]).astype(BF16)
            to_owner = _remote(part[k].at[j], recd[k].at[j0], send3.at[u], recv3.at[4 * k + j0], (j // 2, j % 2, c))

            @pl.when(j != j0)
            def _():
                to_owner.start()

            @pl.when(j == j0)
            def _():
                recd[k][j] = part[k][j]

        swaps = []
        for k in range(n):
            for j in range(N_CHIPS):
                arrived = _remote(part[k].at[j], recd[k].at[j], send3.at[4 * k + j], recv3.at[4 * k + j],
                                  (j // 2, j % 2, c))

                @pl.when(j != j0)
                def _():
                    arrived.wait_recv()

            r = recd[k]
            red[k][...] = ((r[0].astype(F32) + r[1].astype(F32)) + r[2].astype(F32)) + r[3].astype(F32)
            own = pltpu.make_async_copy(red[k], out[k].at[c], local.at[k])
            give = _remote(red[k], out[k].at[c], send5.at[k], recv5.at[k], sibling)
            take = _remote(red[k], out[k].at[1 - c], send5.at[k], recv5.at[k], sibling)
            own.start()
            give.start()
            swaps.append((own, give, take))

        for cp in small_in:
            cp.wait_recv()
            cp.wait_send()
        acc = got[0]
        for d in range(1, N_DEV):
            acc = acc + got[d]
        o_ref[...] = acc
        for u, (k, j) in enumerate(units):
            halves[u].wait_send()
            sent = _remote(part[k].at[j], recd[k].at[j0], send3.at[u], recv3.at[u], (j // 2, j % 2, c))

            @pl.when(j != j0)
            def _():
                sent.wait_send()
        for own, give, take in swaps:
            take.wait_recv()
            give.wait_send()
            own.wait()

    sems = [pltpu.SemaphoreType.DMA((len(units),))] * 4 + [pltpu.SemaphoreType.DMA((n,))] * 2 \
        + [pltpu.SemaphoreType.DMA((2,)), pltpu.SemaphoreType.DMA((n,))] + [pltpu.SemaphoreType.DMA((N_DEV - 1,))] * 2
    res = pl.pallas_call(
        body, name="reduce_grads",
        in_specs=[HBM] * n + [WHOLE], out_specs=[HBM] * n + [WHOLE],
        out_shape=[jax.ShapeDtypeStruct((2,) + s, F32) for s in shapes] + [jax.ShapeDtypeStruct(vec.shape, vec.dtype)],
        scratch_shapes=[pltpu.VMEM((N_CHIPS,) + s, F32) for s in shapes]
        + [pltpu.VMEM((N_CHIPS,) + s, BF16) for s in shapes] * 2
        + [pltpu.VMEM(s, F32) for s in shapes]
        + [pltpu.VMEM((2,) + big, F32), pltpu.VMEM((N_DEV,) + vec.shape, vec.dtype)] + sems,
        compiler_params=_comm_params(),
    )(*grads, vec)
    return res[:n], res[n]


def _adamw_math(w, g, m, v):
    m = ADAM_B1 * m + (1.0 - ADAM_B1) * g
    v = ADAM_B2 * v + (1.0 - ADAM_B2) * (g * g)
    m_hat = m / (1.0 - ADAM_B1 ** ADAM_STEP)
    v_hat = v / (1.0 - ADAM_B2 ** ADAM_STEP)
    return -ADAM_LR * (m_hat / (jnp.sqrt(v_hat) + ADAM_EPS) + ADAM_WD * w), m, v


def _adamw_tiled(w, g, m, v):
    rows, width = w.shape
    tm = rows // 4

    def body(w_ref, g_ref, m_ref, v_ref, go_ref, d_ref, mo_ref, vo_ref):
        g = g_ref[...]
        go_ref[...] = g
        d_ref[...], mo_ref[...], vo_ref[...] = _adamw_math(w_ref[...], g, m_ref[...], v_ref[...])

    spec = pl.BlockSpec((tm, width), lambda i: (i, 0))
    out = jax.ShapeDtypeStruct((rows, width), F32)
    return pl.pallas_call(
        body, grid=(rows // tm,), name="adamw_tiled",
        in_specs=[spec] * 4, out_specs=[spec] * 4, out_shape=[out] * 4,
        compiler_params=_params("parallel"),
    )(w, g, m, v)


def _adamw_many(ws, gs, ms, vs):
    n = len(ws)

    def body(*refs):
        for k in range(n):
            w_ref, g_ref, m_ref, v_ref = (refs[i * n + k] for i in range(4))
            go_ref, d_ref, mo_ref, vo_ref = (refs[(4 + i) * n + k] for i in range(4))
            g = g_ref[...]
            go_ref[...] = g
            d_ref[...], mo_ref[...], vo_ref[...] = _adamw_math(w_ref[...], g, m_ref[...], v_ref[...])

    outs = [jax.ShapeDtypeStruct(a.shape, F32) for a in ws]
    res = pl.pallas_call(
        body, name="adamw_many",
        in_specs=[WHOLE] * (4 * n), out_specs=[WHOLE] * (4 * n), out_shape=outs * 4,
        compiler_params=_comm_params(),
    )(*ws, *gs, *ms, *vs)
    return res[:n], res[n:2 * n], res[2 * n:3 * n], res[3 * n:]


def kernel(x, positions, a_norm, a_w_in, a_conv, a_w_out, kv_norm, w_dkv, ckv_norm, w_ukv, b_norm, b_w_in, b_q_norm, b_w_uq, b_w_out, final_norm, loss_target, m_a_norm, m_a_w_in, m_a_conv, m_a_w_out, m_kv_norm, m_w_dkv, m_ckv_norm, m_w_ukv, m_b_norm, m_b_w_in, m_b_q_norm, m_b_w_uq, m_b_w_out, m_final_norm, v_a_norm, v_a_w_in, v_a_conv, v_a_w_out, v_kv_norm, v_w_dkv, v_ckv_norm, v_w_ukv, v_b_norm, v_b_w_in, v_b_q_norm, v_b_w_uq, v_b_w_out, v_final_norm):
    w = dict(a_norm=a_norm, a_w_in=a_w_in, a_conv=a_conv, a_w_out=a_w_out, kv_norm=kv_norm, w_dkv=w_dkv,
             ckv_norm=ckv_norm, w_ukv=w_ukv, b_norm=b_norm, b_w_in=b_w_in, b_q_norm=b_q_norm, b_w_uq=b_w_uq,
             b_w_out=b_w_out, final_norm=final_norm)
    m = dict(a_norm=m_a_norm, a_w_in=m_a_w_in, a_conv=m_a_conv, a_w_out=m_a_w_out, kv_norm=m_kv_norm, w_dkv=m_w_dkv,
             ckv_norm=m_ckv_norm, w_ukv=m_w_ukv, b_norm=m_b_norm, b_w_in=m_b_w_in, b_q_norm=m_b_q_norm,
             b_w_uq=m_b_w_uq, b_w_out=m_b_w_out, final_norm=m_final_norm)
    v = dict(a_norm=v_a_norm, a_w_in=v_a_w_in, a_conv=v_a_conv, a_w_out=v_a_w_out, kv_norm=v_kv_norm, w_dkv=v_w_dkv,
             ckv_norm=v_ckv_norm, w_ukv=v_w_ukv, b_norm=v_b_norm, b_w_in=v_b_w_in, b_q_norm=v_b_q_norm,
             b_w_uq=v_b_w_uq, b_w_out=v_b_w_out, final_norm=v_final_norm)

    loss, dx, gmat, gsmall = _local_step(x, positions, loss_target, w)

    kinds = [GRAD_KIND[n] for n in GRAD_MATS]
    grads = [gmat[n] for n in GRAD_MATS]
    flat = jnp.concatenate([gsmall[n].reshape(-1) for n in SMALL] + [loss.reshape(1)])
    flat = jnp.pad(flat, (0, SMALL_ROWS * PACK_W - flat.shape[0])).reshape(SMALL_ROWS, PACK_W)
    mine, svec = _reduce_grads(grads, kinds, flat)
    loss = svec.reshape(-1)[LOSS_SLOT]
    g = _shard_grads({n: a.reshape(-1, a.shape[-1]) for n, a in zip(GRAD_MATS, mine)}, svec)

    def two_d(n, a):
        a = a.reshape(-1, a.shape[-1])
        return a.T if n in TRANSPOSED else a

    def back(n, a):
        return (a.T if n in TRANSPOSED else a).reshape(SHARD_SHAPES[n])

    big = "a_w_in"
    rest = [n for n in WEIGHTS if n != big]
    res_big = _adamw_tiled(*(two_d(big, t[big]) for t in (w, g, m, v)))
    res_rest = _adamw_many(*([two_d(n, t[n]) for n in rest] for t in (w, g, m, v)))
    out = {kind: dict(zip(rest, res_rest[i])) for i, kind in enumerate("gdmv")}
    for i, kind in enumerate("gdmv"):
        out[kind][big] = res_big[i]
    return (loss, dx) + tuple(back(n, out[kind][n]) for kind in "gdmv" for n in WEIGHTS)
```

```python
import functools
import math

import numpy as np
import jax
import jax.numpy as jnp
from jax import lax
from jax.experimental import pallas as pl
from jax.experimental.pallas import tpu as pltpu

F32 = jnp.float32
BF16 = jnp.bfloat16

D_MODEL = 1024
E_A = 1024
CONV_WIDTH = 3
N_HEADS = 8
QK_NOPE = 64
QK_ROPE = 32
V_HEAD = 64
KV_RANK = 256
Q_RANK = 384
E_B = N_HEADS * V_HEAD
ROPE_THETA = 10000.0
SOFTMAX_SCALE = 1.0 / math.sqrt(QK_NOPE + QK_ROPE)
LOG2E = math.log2(math.e)
LN2 = math.log(2.0)
Q_PRESCALE = SOFTMAX_SCALE * LOG2E
EPS = 1e-6
HEAD_PAD = 128
QK_PAD = N_HEADS * HEAD_PAD
ROPE_LO = QK_NOPE
ROPE_HALF = QK_ROPE // 2
KR_PAD = KV_RANK + HEAD_PAD

ADAM_LR = 0.001
ADAM_B1 = 0.9
ADAM_B2 = 0.999
ADAM_EPS = 1e-08
ADAM_WD = 0.01
ADAM_STEP = 10

VMEM_LIMIT = 56 * 1024 * 1024
ROW_TILE = 512
CONV_BWD_TILE = 256
ATT_TILE_FWD = 1024
ATT_TILE = 512
HEAD_CHAINS = 2
LANES = 128
PACK_W = 1024

N_CHIPS = 4
N_DEV = 8


def _dot(a, b):
    return jnp.dot(a, b, preferred_element_type=F32)


def _dot_nt(a, b):
    return lax.dot_general(a, b, (((1,), (1,)), ((), ())), preferred_element_type=F32)


def _dot_tn(a, b):
    return lax.dot_general(a, b, (((0,), (0,)), ((), ())), preferred_element_type=F32)


def _rms(x):
    r = lax.rsqrt(jnp.mean(x * x, axis=-1, keepdims=True) + EPS)
    return r, x * r


def _rms_bwd(dxh, xh, r):
    return r * (dxh - xh * jnp.mean(dxh * xh, axis=-1, keepdims=True))


def _rope_fwd(a, c, s1, s2):
    return a * c + pltpu.roll(a, HEAD_PAD - ROPE_HALF, 1) * s1 + pltpu.roll(a, ROPE_HALF, 1) * s2


def _rope_bwd(g, c, s1, s2):
    return g * c + pltpu.roll(g * s1, ROPE_HALF, 1) + pltpu.roll(g * s2, HEAD_PAD - ROPE_HALF, 1)


def _sigmoid(x):
    return 1.0 / (1.0 + jnp.exp(-x))


def _row_spec(tm, n):
    return pl.BlockSpec((tm, n), lambda i: (i, 0))


def _const_spec(shape):
    nd = len(shape)
    return pl.BlockSpec(shape, lambda i: (0,) * nd, pipeline_mode=pl.Buffered(1))


def _acc_spec(shape):
    nd = len(shape)
    return pl.BlockSpec(shape, lambda i: (0,) * nd, pipeline_mode=pl.Buffered(1))


def _params(*sem):
    return pltpu.CompilerParams(dimension_semantics=sem, vmem_limit_bytes=VMEM_LIMIT)


MESH = pl.DeviceIdType.MESH
HBM = pl.BlockSpec(memory_space=pl.ANY)
WHOLE = pl.BlockSpec(memory_space=pltpu.VMEM)
FLIPS = ((1, 0), (0, 1), (1, 1))


def _place():
    return lax.axis_index("x"), lax.axis_index("y"), lax.axis_index("c")


def _remote(src, dst, send, recv, peer):
    return pltpu.make_async_remote_copy(src_ref=src, dst_ref=dst, send_sem=send, recv_sem=recv, device_id=peer,
                                        device_id_type=MESH)


def _comm_params():
    return pltpu.CompilerParams(vmem_limit_bytes=VMEM_LIMIT)


def _rope_consts():
    lane = np.arange(HEAD_PAD)
    first = (lane >= ROPE_LO) & (lane < ROPE_LO + ROPE_HALF)
    second = (lane >= ROPE_LO + ROPE_HALF) & (lane < ROPE_LO + QK_ROPE)
    f = np.where(first, lane - ROPE_LO, np.where(second, lane - ROPE_LO - ROPE_HALF, 0))
    inv = np.float32(ROPE_THETA) ** (-(2 * f).astype(np.float32) / np.float32(QK_ROPE))
    out = np.zeros((8, HEAD_PAD), np.float32)
    out[0] = inv
    out[1] = first
    out[2] = second
    out[3] = lane < ROPE_LO
    return jnp.asarray(out)


def _gather_plan(w, sm, outs, osm, bf, sems):
    n = len(w)
    send_i, recv_i, send_d, recv_d, send_s, recv_s, local = sems
    x, y, c = _place()
    j0 = 2 * x + y
    sibling = (x, y, 1 - c)
    own = [pltpu.make_async_copy(bf[k], outs[k].at[j0], local.at[k]) for k in range(n)]
    if sm is not None:
        own.append(pltpu.make_async_copy(sm, osm.at[j0], local.at[n]))

    def half(k, cc):
        h = w[k].shape[0] // 2
        return pl.ds(pl.multiple_of(cc * h, 16), h)

    sends, arrivals, forwards, fwaits = [], [], [], []
    for i, (fx, fy) in enumerate(FLIPS):
        px, py = x ^ fx, y ^ fy
        pj = 2 * px + py
        for k in range(n):
            s = i * n + k
            sends.append(_remote(bf[k].at[half(k, c)], outs[k].at[j0, half(k, c)], send_i.at[s], recv_i.at[s],
                                 (px, py, c)))
            arrivals.append(_remote(bf[k].at[half(k, c)], outs[k].at[pj, half(k, c)], send_i.at[s], recv_i.at[s],
                                    (px, py, c)))
            forwards.append(_remote(outs[k].at[pj, half(k, c)], outs[k].at[pj, half(k, c)], send_d.at[s],
                                    recv_d.at[s], sibling))
            fwaits.append(_remote(outs[k].at[pj, half(k, 1 - c)], outs[k].at[pj, half(k, 1 - c)], send_d.at[s],
                                  recv_d.at[s], sibling))
        if sm is not None:
            sends.append(_remote(sm, osm.at[j0], send_s.at[i], recv_s.at[i], (px, py, c)))
            fwaits.append(_remote(sm, osm.at[pj], send_s.at[i], recv_s.at[i], (px, py, c)))
    return own, sends, arrivals, forwards, fwaits


def _gather_begin(w, bf, plan):
    own, sends, _, _, _ = plan
    for k in range(len(w)):
        bf[k][...] = w[k][...].astype(BF16)
    for cp in own + sends:
        cp.start()


def _gather_end(plan):
    own, sends, arrivals, forwards, fwaits = plan
    for cp, fwd in zip(arrivals, forwards):
        cp.wait_recv()
        fwd.start()
    for cp in fwaits:
        cp.wait_recv()
    for cp in sends + forwards:
        cp.wait_send()
    for cp in own:
        cp.wait()


def _gather_sems(n, with_small):
    return ([pltpu.SemaphoreType.DMA((3 * n,))] * 4 + [pltpu.SemaphoreType.DMA((3,))] * 2
            + [pltpu.SemaphoreType.DMA((n + (1 if with_small else 0),))])


def _rope_tables_gather(pos_col, w_in, small):
    t = pos_col.shape[0]
    tm = min(ROW_TILE, t)
    steps = t // tm

    def body(p_ref, k_ref, w_ref, sm_ref, c_ref, s1_ref, s2_ref, out_ref, osm_ref, bf_ref, *sems):
        plan = _gather_plan([w_ref], sm_ref, [out_ref], osm_ref, [bf_ref], sems)
        i = pl.program_id(0)

        @pl.when(i == 0)
        def _():
            _gather_begin([w_ref], [bf_ref], plan)

        inv, first, second, nope = k_ref[0:1, :], k_ref[1:2, :], k_ref[2:3, :], k_ref[3:4, :]
        ang = p_ref[...].astype(F32) * inv
        cs, sn = jnp.cos(ang), jnp.sin(ang)
        c_ref[...] = cs * (first + second) + nope
        s1_ref[...] = -sn * first
        s2_ref[...] = sn * second

        @pl.when(i == steps - 1)
        def _():
            _gather_end(plan)

    out = jax.ShapeDtypeStruct((t, HEAD_PAD), F32)
    return pl.pallas_call(
        body, grid=(steps,), name="rope_tables_gather",
        in_specs=[_row_spec(tm, 1), _const_spec((8, HEAD_PAD)), WHOLE, WHOLE],
        out_specs=[_row_spec(tm, HEAD_PAD)] * 3 + [HBM, HBM],
        out_shape=[out] * 3 + [jax.ShapeDtypeStruct((N_CHIPS,) + w_in.shape, BF16),
                               jax.ShapeDtypeStruct((N_CHIPS,) + small.shape, small.dtype)],
        scratch_shapes=[pltpu.VMEM(w_in.shape, BF16)] + _gather_sems(1, True),
        compiler_params=_params("arbitrary"),
    )(pos_col, _rope_consts(), w_in, small)


def _shift_down(v, prev, row):
    p1, p2 = prev[7:8, :], prev[6:7, :]
    v1 = jnp.where(row == 0, p1, pltpu.roll(v, 1, 0))
    v2 = jnp.where(row == 0, p2, jnp.where(row == 1, p1, pltpu.roll(v, 2, 0)))
    return v1, v2


def _conv_fwd(x, seq, ga, w_in4, cw, shards):
    t = x.shape[0]
    tm = min(ROW_TILE, seq)
    tiles_per_seq = seq // tm
    steps = t // tm
    n = len(shards)

    def body(x_ref, ga_ref, w_ref, cw_ref, *rest):
        sh, (c_ref, u_ref, v_ref, ym_ref, kb_ref, kcv_ref, kg_ref) = rest[:n], rest[n:n + 7]
        outs, carry_ref, bf, sems = rest[n + 7:2 * n + 7], rest[2 * n + 7], rest[2 * n + 8:3 * n + 8], rest[3 * n + 8:]
        plan = _gather_plan(sh, None, outs, None, bf, sems)
        i = pl.program_id(0)

        @pl.when(i == 0)
        def _():
            _gather_begin(sh, bf, plan)

        @pl.when(i % tiles_per_seq == 0)
        def _():
            carry_ref[...] = jnp.zeros_like(carry_ref)

        _, xh = _rms(x_ref[...])
        h = (xh * ga_ref[...]).astype(BF16)
        c, u = _dot(h, w_ref[1]), _dot(h, w_ref[2])
        g, b = _dot(h, w_ref[3]), _dot(h, w_ref[0])
        v = c * u
        row = lax.broadcasted_iota(jnp.int32, (tm, 1), 0)
        v1, v2 = _shift_down(v, carry_ref[...], row)
        carry_ref[...] = v[tm - 8:tm, :]
        cv = cw_ref[2:3, :] * v + cw_ref[1:2, :] * v1 + cw_ref[0:1, :] * v2
        sg = _sigmoid(g)
        silu = g * sg
        c_ref[...] = c.astype(BF16)
        u_ref[...] = u.astype(BF16)
        v_ref[...] = v.astype(BF16)
        ym_ref[...] = (silu * b * cv).astype(BF16)
        kb_ref[...] = (silu * cv).astype(BF16)
        kcv_ref[...] = (silu * b).astype(BF16)
        kg_ref[...] = (b * cv * (sg * (1.0 + g * (1.0 - sg)))).astype(BF16)

        @pl.when(i == steps - 1)
        def _():
            _gather_end(plan)

    out = jax.ShapeDtypeStruct((t, E_A), BF16)
    res = pl.pallas_call(
        body, grid=(steps,), name="conv_fwd",
        in_specs=[_row_spec(tm, D_MODEL), _const_spec((1, D_MODEL)), _const_spec((4, D_MODEL, E_A)),
                  _const_spec((8, E_A))] + [WHOLE] * n,
        out_specs=[_row_spec(tm, E_A)] * 7 + [HBM] * n,
        out_shape=[out] * 7 + [jax.ShapeDtypeStruct((N_CHIPS,) + a.shape, BF16) for a in shards],
        scratch_shapes=[pltpu.VMEM((8, E_A), F32)] + [pltpu.VMEM(a.shape, BF16) for a in shards]
        + _gather_sems(n, False),
        compiler_params=_params("arbitrary"),
    )(x, ga, w_in4, cw, *shards)
    return res[:7], res[7:]


def _mid_fwd(x, ym, w_out, gk, gb, w_dkv, gc, w_uk, w_uv, w_bin, gq, w_uq, rc, rs1, rs2):
    t = x.shape[0]
    tm = min(2 * ROW_TILE, t)

    def body(x_ref, ym_ref, wo_ref, gk_ref, gb_ref, wd_ref, gc_ref, wk_ref, wv_ref, wi_ref, gq_ref, wq_ref,
             c_ref, s1_ref, s2_ref, x1_ref, q_ref, k_ref, v_ref, gate_ref, cq_ref, ckv_ref):
        cb, s1b, s2b = c_ref[...], s1_ref[...], s2_ref[...]
        x1 = x_ref[...] + _dot(ym_ref[...], wo_ref[...])
        x1_ref[...] = x1
        _, xh = _rms(x1)
        hk = (xh * gk_ref[...]).astype(BF16)
        h1 = (xh * gb_ref[...]).astype(BF16)

        pb = _dot(h1, wi_ref[...])
        cq = pb[:, :Q_RANK]
        cq_ref[...] = cq.astype(BF16)
        gate_ref[...] = pb[:, Q_RANK:].astype(BF16)
        _, cqh = _rms(cq)
        q = _dot((cqh * gq_ref[...]).astype(BF16), wq_ref[...])
        for h in range(N_HEADS):
            sl = slice(h * HEAD_PAD, (h + 1) * HEAD_PAD)
            q_ref[:, sl] = (_rope_fwd(q[:, sl], cb, s1b, s2b) * Q_PRESCALE).astype(BF16)

        ckr = _dot(hk, wd_ref[...])
        ckv_raw = ckr[:, :KV_RANK]
        ckv_ref[...] = ckv_raw.astype(BF16)
        _, ch = _rms(ckv_raw)
        ckv = (ch * gc_ref[...]).astype(BF16)
        kr = _rope_fwd(ckr[:, KV_RANK:], cb, s1b, s2b)
        kn = _dot(ckv, wk_ref[...])
        for h in range(N_HEADS):
            sl = slice(h * HEAD_PAD, (h + 1) * HEAD_PAD)
            k_ref[:, sl] = (kn[:, sl] + kr).astype(BF16)
        v_ref[...] = _dot(ckv, wv_ref[...]).astype(BF16)

    def sds(n, dt):
        return jax.ShapeDtypeStruct((t, n), dt)

    return pl.pallas_call(
        body, grid=(t // tm,), name="mid_fwd",
        in_specs=[_row_spec(tm, D_MODEL), _row_spec(tm, E_A), _const_spec((E_A, D_MODEL)),
                  _const_spec((1, D_MODEL)), _const_spec((1, D_MODEL)), _const_spec((D_MODEL, KR_PAD)),
                  _const_spec((1, KV_RANK)), _const_spec((KV_RANK, QK_PAD)), _const_spec((KV_RANK, E_B)),
                  _const_spec((D_MODEL, Q_RANK + E_B)), _const_spec((1, Q_RANK)), _const_spec((Q_RANK, QK_PAD)),
                  _row_spec(tm, HEAD_PAD), _row_spec(tm, HEAD_PAD), _row_spec(tm, HEAD_PAD)],
        out_specs=[_row_spec(tm, D_MODEL), _row_spec(tm, QK_PAD), _row_spec(tm, QK_PAD), _row_spec(tm, E_B),
                   _row_spec(tm, E_B), _row_spec(tm, Q_RANK), _row_spec(tm, KV_RANK)],
        out_shape=[sds(D_MODEL, F32), sds(QK_PAD, BF16), sds(QK_PAD, BF16), sds(E_B, BF16), sds(E_B, BF16),
                   sds(Q_RANK, BF16), sds(KV_RANK, BF16)],
        compiler_params=_params("parallel"),
    )(x, ym, w_out, gk, gb, w_dkv, gc, w_uk, w_uv, w_bin, gq, w_uq, rc, rs1, rs2)


def _pair_specs(seq):
    qk = pl.BlockSpec((seq, 2 * HEAD_PAD), lambda b, p: (b, p))
    vo = pl.BlockSpec((seq, 2 * V_HEAD), lambda b, p: (b, p))
    st = pl.BlockSpec((None, 2, seq), lambda b, p: (p, 0, b))
    return qk, vo, st


def _attn_fwd(q, k, v, seq):
    t = q.shape[0]
    tq = min(ATT_TILE_FWD, seq)
    nq = seq // tq

    def body(q_ref, k_ref, v_ref, o_ref, lse_ref, m_scr, l_scr, acc_scr):
        lane = lax.broadcasted_iota(jnp.int32, (tq, 2 * V_HEAD), 1)

        def q_step(qi, _):
            q0 = pl.multiple_of(qi * tq, tq)
            m_scr[...] = jnp.full(m_scr.shape, -jnp.inf, F32)
            l_scr[...] = jnp.zeros_like(l_scr)
            acc_scr[...] = jnp.zeros_like(acc_scr)

            def block(q_lo, q_n, k0, k_n, masked):
                rows = slice(q_lo, q_lo + q_n)
                vt = v_ref[pl.ds(k0, k_n), :]

                def score(hh):
                    hs = slice(hh * HEAD_PAD, (hh + 1) * HEAD_PAD)
                    return _dot_nt(q_ref[pl.ds(q0 + q_lo, q_n), hs], k_ref[pl.ds(k0, k_n), hs])

                early = [score(hh) for hh in range(2)] if masked else None
                for hh in range(2):
                    s = early[hh] if masked else score(hh)
                    if masked:
                        row = lax.broadcasted_iota(jnp.int32, (q_n, k_n), 0)
                        col = lax.broadcasted_iota(jnp.int32, (q_n, k_n), 1)
                        s = jnp.where(col <= row, s, -jnp.inf)
                    m_old = m_scr[hh, rows]
                    m_new = jnp.maximum(m_old, jnp.max(s, axis=-1, keepdims=True))
                    alpha = jnp.exp2(m_old - m_new)
                    ps = [jnp.exp2(s[:, j * LANES:(j + 1) * LANES] - m_new) for j in range(k_n // LANES)]
                    l_scr[hh, rows] = alpha * l_scr[hh, rows] + functools.reduce(lambda a, b: a + b, ps)
                    p = jnp.concatenate(ps, axis=-1).astype(BF16)
                    acc_scr[hh, rows] = alpha * acc_scr[hh, rows] + _dot(p, vt)
                    m_scr[hh, rows] = m_new

            def k_step(ki, _):
                block(0, tq, pl.multiple_of(ki * tq, tq), tq, False)
                return 0

            lax.fori_loop(0, qi, k_step, 0)
            half = tq // 2
            block(0, tq, q0, half, True)
            block(half, half, q0 + half, half, True)
            l0 = jnp.sum(l_scr[0], axis=-1, keepdims=True)
            l1 = jnp.sum(l_scr[1], axis=-1, keepdims=True)
            o_ref[pl.ds(q0, tq), :] = jnp.where(lane < V_HEAD, acc_scr[0] / l0, acc_scr[1] / l1).astype(BF16)
            stats = jnp.where(lane == 0, m_scr[0] + jnp.log2(l0), m_scr[1] + jnp.log2(l1)).T
            lse_ref[:, pl.ds(q0, tq)] = stats[0:2, :]
            return 0

        lax.fori_loop(0, nq, q_step, 0)

    qk, vo, st = _pair_specs(seq)
    return pl.pallas_call(
        body, grid=(t // seq, N_HEADS // 2), name="attn_fwd",
        in_specs=[qk, qk, vo], out_specs=[vo, st],
        out_shape=[jax.ShapeDtypeStruct((t, E_B), BF16), jax.ShapeDtypeStruct((N_HEADS // 2, 2, t), F32)],
        scratch_shapes=[pltpu.VMEM((2, tq, LANES), F32), pltpu.VMEM((2, tq, LANES), F32),
                        pltpu.VMEM((2, tq, 2 * V_HEAD), F32)],
        compiler_params=_params("parallel", "parallel"),
    )(q, k, v)


def _head_fwd_bwd(o, gate, x1, tgt, w_bout, gf):
    t = o.shape[0]
    tm = min(2 * ROW_TILE, t)

    def body(o_ref, gate_ref, x1_ref, tgt_ref, w_ref, gf_ref,
             dx2_ref, do_ref, dgate_ref, dd_ref, loss_ref, dgf_ref, dw_ref):
        @pl.when(pl.program_id(0) == 0)
        def _():
            loss_ref[...] = jnp.zeros_like(loss_ref)
            dgf_ref[...] = jnp.zeros_like(dgf_ref)
            dw_ref[...] = jnp.zeros_like(dw_ref)

        hm = tm // HEAD_CHAINS
        gf = gf_ref[...]
        lane = lax.broadcasted_iota(jnp.int32, (hm, 2 * V_HEAD), 1)
        chains = []
        for ch in range(HEAD_CHAINS):
            rs = pl.ds(ch * hm, hm)
            o = o_ref[rs, :].astype(F32)
            gt = gate_ref[rs, :].astype(F32)
            sg = _sigmoid(gt)
            silu = gt * sg
            z = (o * silu).astype(BF16)
            chains.append((rs, o, gt, sg, silu, z, x1_ref[rs, :] + _dot(z, w_ref[...])))
        mids = []
        for rs, o, gt, sg, silu, z, x2 in chains:
            r2, xh2 = _rms(x2)
            err = xh2 * gf - tgt_ref[rs, :]
            loss_ref[...] += 0.5 * jnp.sum(jnp.mean(err * err, axis=-1, keepdims=True))
            dy = err * (1.0 / D_MODEL)
            dgf_ref[...] += jnp.sum(dy * xh2, axis=0, keepdims=True)
            dx2 = _rms_bwd(dy * gf, xh2, r2)
            dx2_ref[rs, :] = dx2
            dx2b = dx2.astype(BF16)
            mids.append(_dot_nt(dx2b, w_ref[...]))
            dw_ref[...] += _dot_tn(z, dx2b)
        for (rs, o, gt, sg, silu, z, x2), dz in zip(chains, mids):
            do = dz * silu
            do_ref[rs, :] = do.astype(BF16)
            dgate_ref[rs, :] = (dz * o * (sg * (1.0 + gt * (1.0 - sg)))).astype(BF16)
            prod = do * o
            cols = jnp.zeros((hm, LANES), F32)
            for p in range(N_HEADS // 2):
                blk = prod[:, p * 2 * V_HEAD:(p + 1) * 2 * V_HEAD]
                d0 = jnp.sum(jnp.where(lane < V_HEAD, blk, 0.0), axis=-1, keepdims=True)
                d1 = jnp.sum(jnp.where(lane < V_HEAD, 0.0, blk), axis=-1, keepdims=True)
                cols = jnp.where(lane == 2 * p, d0, jnp.where(lane == 2 * p + 1, d1, cols))
            rows = cols.T
            for h in range(N_HEADS):
                dd_ref[h // 2, h % 2:h % 2 + 1, rs] = rows[h:h + 1, :]

    return pl.pallas_call(
        body, grid=(t // tm,), name="head_fwd_bwd",
        in_specs=[_row_spec(tm, E_B), _row_spec(tm, E_B), _row_spec(tm, D_MODEL), _row_spec(tm, D_MODEL),
                  _const_spec((E_B, D_MODEL)), _const_spec((1, D_MODEL))],
        out_specs=[_row_spec(tm, D_MODEL), _row_spec(tm, E_B), _row_spec(tm, E_B),
                   pl.BlockSpec((N_HEADS // 2, 2, tm), lambda i: (0, 0, i)),
                   _acc_spec((1, 1)), _acc_spec((1, D_MODEL)), _acc_spec((E_B, D_MODEL))],
        out_shape=[jax.ShapeDtypeStruct((t, D_MODEL), F32), jax.ShapeDtypeStruct((t, E_B), BF16),
                   jax.ShapeDtypeStruct((t, E_B), BF16), jax.ShapeDtypeStruct((N_HEADS // 2, 2, t), F32),
                   jax.ShapeDtypeStruct((1, 1), F32), jax.ShapeDtypeStruct((1, D_MODEL), F32),
                   jax.ShapeDtypeStruct((E_B, D_MODEL), F32)],
        compiler_params=_params("arbitrary"),
    )(o, gate, x1, tgt, w_bout, gf)


def _attn_bwd(q, k, v, do, lse, dd, seq):
    t = q.shape[0]
    tq = min(ATT_TILE, seq)
    nq = seq // tq
    assert nq % 2 == 0, "full tiles are taken in pairs"

    def body(q_ref, k_ref, v_ref, do_ref, lse_ref, dd_ref, dq_ref, dk_ref, dv_ref, dq_acc, dk_acc, dv_acc):
        dq_acc[...] = jnp.zeros_like(dq_acc)

        def k_step(ki, _):
            k0 = pl.multiple_of(ki * tq, tq)
            dk_acc[...] = jnp.zeros_like(dk_acc)
            dv_acc[...] = jnp.zeros_like(dv_acc)

            def block(k_lo, k_n, q0, q_n, masked):
                rows = slice(k_lo, k_lo + k_n)
                lane = lax.broadcasted_iota(jnp.int32, (q_n, 2 * V_HEAD), 1)
                vt = v_ref[pl.ds(k0 + k_lo, k_n), :]
                do_pair = do_ref[pl.ds(q0, q_n), :]

                def operands(hh):
                    hs = slice(hh * HEAD_PAD, (hh + 1) * HEAD_PAD)
                    kt = k_ref[pl.ds(k0 + k_lo, k_n), hs]
                    qt = q_ref[pl.ds(q0, q_n), hs]
                    mine = (lane < V_HEAD) if hh == 0 else (lane >= V_HEAD)
                    do_h = jnp.where(mine, do_pair, jnp.zeros((), BF16))
                    return hs, kt, qt, do_h, _dot_nt(kt, qt), _dot_nt(vt, do_h)

                early = [operands(hh) for hh in range(2)] if masked else None
                for hh in range(2):
                    hs, kt, qt, do_h, st, dpt = early[hh] if masked else operands(hh)
                    if masked:
                        krow = lax.broadcasted_iota(jnp.int32, (k_n, q_n), 0)
                        qcol = lax.broadcasted_iota(jnp.int32, (k_n, q_n), 1)
                        st = jnp.where(krow <= qcol, st, -jnp.inf)
                    pt = jnp.exp2(st - lse_ref[hh:hh + 1, pl.ds(q0, q_n)])
                    dst = (pt * (dpt - dd_ref[hh:hh + 1, pl.ds(q0, q_n)])).astype(BF16)
                    dv_acc[rows, :] += _dot(pt.astype(BF16), do_h)
                    dk_acc[rows, hs] += _dot(dst, qt)
                    dq_acc[pl.ds(q0, q_n), hs] += _dot_tn(dst, kt)

            wide = 2 * tq

            def q_step(qj, _):
                block(0, tq, pl.multiple_of(qj * wide, wide), wide, False)
                return 0

            half = tq // 2
            block(0, half, k0, tq, True)
            block(half, half, pl.multiple_of(k0 + half, half), half, True)

            @pl.when(ki % 2 == 0)
            def _():
                block(0, tq, pl.multiple_of(k0 + tq, tq), tq, False)

            lax.fori_loop(ki // 2 + 1, nq // 2, q_step, 0)
            dk_ref[pl.ds(k0, tq), :] = (dk_acc[...] * LN2).astype(BF16)
            dv_ref[pl.ds(k0, tq), :] = dv_acc[...].astype(BF16)
            return 0

        lax.fori_loop(0, nq, k_step, 0)
        dq_ref[...] = (dq_acc[...] * SOFTMAX_SCALE).astype(BF16)

    qk, vo, st = _pair_specs(seq)
    return pl.pallas_call(
        body, grid=(t // seq, N_HEADS // 2), name="attn_bwd",
        in_specs=[qk, qk, vo, vo, st, st], out_specs=[qk, qk, vo],
        out_shape=[jax.ShapeDtypeStruct((t, QK_PAD), BF16), jax.ShapeDtypeStruct((t, QK_PAD), BF16),
                   jax.ShapeDtypeStruct((t, E_B), BF16)],
        scratch_shapes=[pltpu.VMEM((seq, 2 * HEAD_PAD), F32), pltpu.VMEM((tq, 2 * HEAD_PAD), F32),
                        pltpu.VMEM((tq, 2 * V_HEAD), F32)],
        compiler_params=_params("parallel", "parallel"),
    )(q, k, v, do, lse, dd)


def _mid_bwd(dq, dk, dv, dgate, dx2, x1, cq, ckv, rc, rs1, rs2, w_uq, w_bin, w_uk, w_uv, w_dkv, gq, gc, gk, gb):
    t = dq.shape[0]
    tm = min(ROW_TILE, t)

    def body(dq_ref, dk_ref, dv_ref, dgate_ref, dx2_ref, x1_ref, cq_ref, ckv_ref, c_ref, s1_ref, s2_ref,
             wq_ref, wi_ref, wk_ref, wv_ref, wd_ref, gq_ref, gc_ref, gk_ref, gb_ref,
             dx1_ref, dwq_ref, dwi_ref, dwk_ref, dwv_ref, dwd_ref, dgq_ref, dgc_ref, dgk_ref, dgb_ref):
        @pl.when(pl.program_id(0) == 0)
        def _():
            for ref in (dwq_ref, dwi_ref, dwk_ref, dwv_ref, dwd_ref, dgq_ref, dgc_ref, dgk_ref, dgb_ref):
                ref[...] = jnp.zeros_like(ref)

        cb, s1b, s2b = c_ref[...], s1_ref[...], s2_ref[...]
        gk, gb, gq, gc = gk_ref[...], gb_ref[...], gq_ref[...], gc_ref[...]

        dkb, dvb = dk_ref[...], dv_ref[...]
        dckv = _dot_nt(dkb, wk_ref[...]) + _dot_nt(dvb, wv_ref[...])
        rcv, ch = _rms(ckv_ref[...].astype(F32))
        ckvn = (ch * gc).astype(BF16)
        dwk_ref[...] += _dot_tn(ckvn, dkb)
        dwv_ref[...] += _dot_tn(ckvn, dvb)

        dqs = [_rope_bwd(dq_ref[:, h * HEAD_PAD:(h + 1) * HEAD_PAD].astype(F32), cb, s1b, s2b)
               for h in range(N_HEADS)]
        dqb = jnp.concatenate(dqs, axis=-1).astype(BF16)
        rq, cqh = _rms(cq_ref[...].astype(F32))
        dcqn = _dot_nt(dqb, wq_ref[...])
        dwq_ref[...] += _dot_tn((cqh * gq).astype(BF16), dqb)

        dgc_ref[...] += jnp.sum(dckv * ch, axis=0, keepdims=True)
        dckv_raw = _rms_bwd(dckv * gc, ch, rcv)
        dkr = dk_ref[:, 0:HEAD_PAD].astype(F32)
        for h in range(1, N_HEADS):
            dkr = dkr + dk_ref[:, h * HEAD_PAD:(h + 1) * HEAD_PAD].astype(F32)
        dkr = _rope_bwd(dkr, cb, s1b, s2b)
        dckr = jnp.concatenate([dckv_raw, dkr], axis=-1).astype(BF16)
        r1, xh = _rms(x1_ref[...])
        dhk = _dot_nt(dckr, wd_ref[...])
        dwd_ref[...] += _dot_tn((xh * gk).astype(BF16), dckr)

        dgq_ref[...] += jnp.sum(dcqn * cqh, axis=0, keepdims=True)
        dcq = _rms_bwd(dcqn * gq, cqh, rq)
        dpb = jnp.concatenate([dcq.astype(BF16), dgate_ref[...]], axis=-1)
        dh1 = _dot_nt(dpb, wi_ref[...])
        dwi_ref[...] += _dot_tn((xh * gb).astype(BF16), dpb)

        dgb_ref[...] += jnp.sum(dh1 * xh, axis=0, keepdims=True)
        dgk_ref[...] += jnp.sum(dhk * xh, axis=0, keepdims=True)
        dx1_ref[...] = dx2_ref[...] + _rms_bwd(dh1 * gb + dhk * gk, xh, r1)

    acc_shapes = [(Q_RANK, QK_PAD), (D_MODEL, Q_RANK + E_B), (KV_RANK, QK_PAD), (KV_RANK, E_B), (D_MODEL, KR_PAD),
                  (1, Q_RANK), (1, KV_RANK), (1, D_MODEL), (1, D_MODEL)]
    return pl.pallas_call(
        body, grid=(t // tm,), name="mid_bwd",
        in_specs=[_row_spec(tm, QK_PAD), _row_spec(tm, QK_PAD), _row_spec(tm, E_B), _row_spec(tm, E_B),
                  _row_spec(tm, D_MODEL), _row_spec(tm, D_MODEL), _row_spec(tm, Q_RANK), _row_spec(tm, KV_RANK),
                  _row_spec(tm, HEAD_PAD), _row_spec(tm, HEAD_PAD), _row_spec(tm, HEAD_PAD),
                  _const_spec((Q_RANK, QK_PAD)), _const_spec((D_MODEL, Q_RANK + E_B)),
                  _const_spec((KV_RANK, QK_PAD)), _const_spec((KV_RANK, E_B)), _const_spec((D_MODEL, KR_PAD)),
                  _const_spec((1, Q_RANK)), _const_spec((1, KV_RANK)), _const_spec((1, D_MODEL)),
                  _const_spec((1, D_MODEL))],
        out_specs=[_row_spec(tm, D_MODEL)] + [_acc_spec(s) for s in acc_shapes],
        out_shape=[jax.ShapeDtypeStruct((t, D_MODEL), F32)] + [jax.ShapeDtypeStruct(s, F32) for s in acc_shapes],
        compiler_params=_params("arbitrary"),
    )(dq, dk, dv, dgate, dx2, x1, cq, ckv, rc, rs1, rs2, w_uq, w_bin, w_uk, w_uv, w_dkv, gq, gc, gk, gb)


def _conv_bwd(dx1, x, c, u, v, ym, kb, kcv, kg, seq, w_out, w_in4, ga, cw):
    t = x.shape[0]
    tm = min(CONV_BWD_TILE, seq)
    tiles_per_seq = seq // tm
    n = t // tm

    def tile(i):
        return n - 1 - i

    def rev(width):
        return pl.BlockSpec((tm, width), lambda i: (tile(i), 0))

    def body(dx1_ref, x_ref, c_ref, u_ref, v_ref, ym_ref, kb_ref, kcv_ref, kg_ref, wo_ref, wi_ref, ga_ref, cw_ref,
             dx_ref, dwi_ref, dwo_ref, dcw_ref, dga_ref, carry_ref):
        i = pl.program_id(0)
        j = tile(i)

        @pl.when(i == 0)
        def _():
            for ref in (dwi_ref, dwo_ref, dcw_ref, dga_ref):
                ref[...] = jnp.zeros_like(ref)

        @pl.when(j % tiles_per_seq == tiles_per_seq - 1)
        def _():
            carry_ref[...] = jnp.zeros_like(carry_ref)

        dx1 = dx1_ref[...]
        dx1b = dx1.astype(BF16)
        dym = _dot_nt(dx1b, wo_ref[...])
        dwo_ref[...] += _dot_tn(ym_ref[...], dx1b)
        db = dym * kb_ref[...].astype(F32)
        dcv = dym * kcv_ref[...].astype(F32)
        dg = dym * kg_ref[...].astype(F32)

        row = lax.broadcasted_iota(jnp.int32, (tm, 1), 0)
        w0, w1, w2 = cw_ref[0:1, :], cw_ref[1:2, :], cw_ref[2:3, :]
        nxt = carry_ref[...]
        n0, n1 = nxt[0:1, :], nxt[1:2, :]
        d1 = jnp.where(row == tm - 1, n0, pltpu.roll(dcv, tm - 1, 0))
        d2 = jnp.where(row == tm - 1, n1, jnp.where(row == tm - 2, n0, pltpu.roll(dcv, tm - 2, 0)))
        carry_ref[...] = dcv[0:8, :]
        dv = w2 * dcv + w1 * d1 + w0 * d2
        v = v_ref[...].astype(F32)
        dcw_ref[0:1, :] += jnp.sum(d2 * v, axis=0, keepdims=True)
        dcw_ref[1:2, :] += jnp.sum(d1 * v, axis=0, keepdims=True)
        dcw_ref[2:3, :] += jnp.sum(dcv * v, axis=0, keepdims=True)

        r0, xh = _rms(x_ref[...])
        ga = ga_ref[...]
        h = (xh * ga).astype(BF16)
        dh = jnp.zeros((tm, D_MODEL), F32)
        for idx, dpart in enumerate((db, dv * u_ref[...].astype(F32), dv * c_ref[...].astype(F32), dg)):
            dpb = dpart.astype(BF16)
            dh = dh + _dot_nt(dpb, wi_ref[idx])
            dwi_ref[idx] += _dot_tn(h, dpb)
        dga_ref[...] += jnp.sum(dh * xh, axis=0, keepdims=True)
        dx_ref[...] = dx1 + _rms_bwd(dh * ga, xh, r0)

    acc_shapes = [(4, D_MODEL, E_A), (E_A, D_MODEL), (8, E_A), (1, D_MODEL)]
    return pl.pallas_call(
        body, grid=(n,), name="conv_bwd",
        in_specs=[rev(D_MODEL), rev(D_MODEL)] + [rev(E_A)] * 7
        + [_const_spec((E_A, D_MODEL)), _const_spec((4, D_MODEL, E_A)), _const_spec((1, D_MODEL)),
           _const_spec((8, E_A))],
        out_specs=[rev(D_MODEL)] + [_acc_spec(s) for s in acc_shapes],
        out_shape=[jax.ShapeDtypeStruct((t, D_MODEL), F32)] + [jax.ShapeDtypeStruct(s, F32) for s in acc_shapes],
        scratch_shapes=[pltpu.VMEM((8, E_A), F32)],
        compiler_params=_params("arbitrary"),
    )(dx1, x, c, u, v, ym, kb, kcv, kg, w_out, w_in4, ga, cw)


WEIGHTS = ("a_norm", "a_w_in", "a_conv", "a_w_out", "kv_norm", "w_dkv", "ckv_norm", "w_ukv", "b_norm", "b_w_in",
           "b_q_norm", "b_w_uq", "b_w_out", "final_norm")
SHARD_SHAPES = {
    "a_norm": (1, 256), "a_w_in": (1, 1024, 1024), "a_conv": (1, 3, 256), "a_w_out": (1, 256, 1024),
    "kv_norm": (1024,), "w_dkv": (256, 288), "ckv_norm": (256,), "w_ukv": (256, 256), "b_norm": (1, 1024),
    "b_w_in": (1, 256, 896), "b_q_norm": (1, 384), "b_w_uq": (1, 384, 192), "b_w_out": (1, 512, 256),
    "final_norm": (1024,),
}
MATS = ("a_w_in", "a_w_out", "w_dkv", "w_ukv", "b_w_in", "b_w_uq", "b_w_out")
SMALL = ("a_norm", "a_conv", "kv_norm", "ckv_norm", "b_norm", "b_q_norm", "final_norm")
SMALL_FULL = {"a_norm": 1024, "a_conv": 3072, "kv_norm": 1024, "ckv_norm": 256, "b_norm": 1024, "b_q_norm": 384,
              "final_norm": 1024}
SMALL_ROWS = 8
LOSS_SLOT = sum(SMALL_FULL.values())


def _mat2d(name, a):
    return a.reshape(SHARD_SHAPES[name][-2:])


def _prep_first(g_win, gsmall):
    sm = gsmall.reshape(N_CHIPS, -1)
    a_conv = jnp.transpose(sm[:, 256:1024].reshape(N_CHIPS, CONV_WIDTH, 256), (1, 0, 2)).reshape(CONV_WIDTH, -1)
    return {"w_in4": g_win, "ga": sm[:, :256].reshape(1, -1), "cw": jnp.pad(a_conv, ((0, 8 - CONV_WIDTH), (0, 0)))}


def _prep_rest(gath, w):
    def cols(a):
        return jnp.transpose(a, (1, 0, 2)).reshape(a.shape[1], -1)

    def pad_heads(a, width):
        a = a.reshape(a.shape[0], N_HEADS, width)
        return jnp.pad(a, ((0, 0), (0, 0), (0, HEAD_PAD - width))).reshape(a.shape[0], QK_PAD)

    row = lambda a: a.reshape(1, -1).astype(F32)
    w_dkv = gath["w_dkv"].reshape(D_MODEL, KV_RANK + QK_ROPE)
    w_ukv = cols(gath["w_ukv"]).reshape(KV_RANK, N_HEADS, 2, QK_NOPE)
    return {
        "w_out": gath["a_w_out"].reshape(E_A, D_MODEL),
        "w_dkv": jnp.concatenate([w_dkv[:, :KV_RANK], jnp.zeros((D_MODEL, ROPE_LO), BF16), w_dkv[:, KV_RANK:],
                                  jnp.zeros((D_MODEL, HEAD_PAD - ROPE_LO - QK_ROPE), BF16)], axis=1),
        "w_uk": pad_heads(w_ukv[:, :, 0, :].reshape(KV_RANK, N_HEADS * QK_NOPE), QK_NOPE),
        "w_uv": w_ukv[:, :, 1, :].reshape(KV_RANK, E_B),
        "w_bin": gath["b_w_in"].reshape(D_MODEL, Q_RANK + E_B),
        "w_uq": pad_heads(cols(gath["b_w_uq"]), QK_NOPE + QK_ROPE),
        "w_bout": cols(gath["b_w_out"]),
        "gk": row(w["kv_norm"]), "gc": row(w["ckv_norm"]), "gb": row(w["b_norm"]),
        "gq": row(w["b_q_norm"]), "gf": row(w["final_norm"]),
    }


TRANSPOSED = ("w_dkv", "b_w_uq")
GRAD_MATS = ("w_in4", "w_out", "w_dkv", "w_uk", "w_uv", "w_bin", "w_uq", "w_bout")
GRAD_KIND = {"w_in4": "lead", "w_out": "row", "w_dkv": "row", "w_uk": "col", "w_uv": "col", "w_bin": "row",
             "w_uq": "col", "w_bout": "col"}


def _local_step(x, positions, tgt, w):
    bsz, seq, _ = x.shape
    t = bsz * seq
    x2d = x.reshape(t, D_MODEL)
    small = jnp.concatenate([w["a_norm"].reshape(-1), w["a_conv"].reshape(-1)]).reshape(8, LANES)
    rc, rs1, rs2, g_win, gsmall = _rope_tables_gather(positions.reshape(t, 1), _mat2d(MATS[0], w[MATS[0]]), small)
    wk = _prep_first(g_win, gsmall)
    (c, u, v, ym, kb, kcv, kg), gathered = _conv_fwd(x2d, seq, wk["ga"], wk["w_in4"], wk["cw"],
                                           [_mat2d(n, w[n]) for n in MATS[1:]])
    wk.update(_prep_rest(dict(zip(MATS[1:], gathered)), w))
    x1, q, k, vv, gate, cq, ckv = _mid_fwd(x2d, ym, wk["w_out"], wk["gk"], wk["gb"], wk["w_dkv"], wk["gc"], wk["w_uk"],
                                          wk["w_uv"], wk["w_bin"], wk["gq"], wk["w_uq"], rc, rs1, rs2)
    o, lse = _attn_fwd(q, k, vv, seq)
    dx2, do, dgate, dd, loss, dgf, dw_bout = _head_fwd_bwd(o, gate, x1, tgt.reshape(t, D_MODEL), wk["w_bout"], wk["gf"])
    dq, dk, dv = _attn_bwd(q, k, vv, do, lse, dd, seq)
    dx1, dwq, dw_bin, dwk, dwv, dwd, dgq, dgc, dgk, dgb = _mid_bwd(
        dq, dk, dv, dgate, dx2, x1, cq, ckv, rc, rs1, rs2, wk["w_uq"], wk["w_bin"], wk["w_uk"], wk["w_uv"], wk["w_dkv"],
        wk["gq"], wk["gc"], wk["gk"], wk["gb"])
    dx, dw_in4, dw_out, dcw, dga = _conv_bwd(dx1, x2d, c, u, v, ym, kb, kcv, kg, seq, wk["w_out"], wk["w_in4"], wk["ga"], wk["cw"])
    mats = {"w_in4": dw_in4, "w_out": dw_out, "w_dkv": dwd, "w_uk": dwk, "w_uv": dwv, "w_bin": dw_bin, "w_uq": dwq,
            "w_bout": dw_bout}
    small = {"a_norm": dga, "a_conv": dcw[:CONV_WIDTH], "kv_norm": dgk, "ckv_norm": dgc, "b_norm": dgb,
             "b_q_norm": dgq, "final_norm": dgf}
    return loss[0, 0], dx.reshape(bsz, seq, D_MODEL), mats, small


def _shard_grads(sh, svec):
    j0 = 2 * lax.axis_index("x") + lax.axis_index("y")
    flat = svec.reshape(-1)
    off, small = 0, {}
    for n in SMALL:
        small[n] = flat[off:off + SMALL_FULL[n]]
        off += SMALL_FULL[n]
    dwd, dwk, dwv, dwq = sh["w_dkv"], sh["w_uk"], sh["w_uv"], sh["w_uq"]
    w_ukv = jnp.stack([dwk.reshape(KV_RANK, 2, HEAD_PAD)[:, :, :QK_NOPE], dwv.reshape(KV_RANK, 2, V_HEAD)], axis=2)
    return {
        "a_norm": lax.dynamic_slice(small["a_norm"], (j0 * 256,), (256,)),
        "a_conv": lax.dynamic_slice(small["a_conv"].reshape(CONV_WIDTH, E_A), (0, j0 * 256), (CONV_WIDTH, 256)),
        "kv_norm": small["kv_norm"], "ckv_norm": small["ckv_norm"], "b_norm": small["b_norm"],
        "b_q_norm": small["b_q_norm"], "final_norm": small["final_norm"],
        "a_w_in": sh["w_in4"], "a_w_out": sh["w_out"],
        "w_dkv": jnp.concatenate([dwd[:, :KV_RANK], dwd[:, KV_RANK + ROPE_LO:KV_RANK + ROPE_LO + QK_ROPE]], axis=1),
        "w_ukv": w_ukv.reshape(KV_RANK, 2 * (QK_NOPE + V_HEAD)),
        "b_w_in": sh["w_bin"],
        "b_w_uq": dwq.reshape(Q_RANK, 2, HEAD_PAD)[:, :, :QK_NOPE + QK_ROPE].reshape(Q_RANK, -1),
        "b_w_out": sh["w_bout"],
    }


def _sub(ref, kind, j, cc):
    if kind == "lead":
        h = ref.shape[1] // 2
        return ref.at[j, pl.ds(pl.multiple_of(cc * h, 8), h), :]
    if kind == "row":
        rows = ref.shape[0] // N_CHIPS
        h = rows // 2
        return ref.at[pl.ds(pl.multiple_of(j * rows + cc * h, 8), h), :]
    cols = ref.shape[1] // N_CHIPS
    h = ref.shape[0] // 2
    return ref.at[pl.ds(pl.multiple_of(cc * h, 8), h), pl.ds(j * cols, cols)]


def _sub_shape(shape, kind):
    if kind == "lead":
        return (shape[1] // 2, shape[2])
    if kind == "row":
        return (shape[0] // N_CHIPS // 2, shape[1])
    return (shape[0] // 2, shape[1] // N_CHIPS)


def _reduce_grads(grads, kinds, vec):
    n = len(grads)
    shapes = [_sub_shape(a.shape, kd) for a, kd in zip(grads, kinds)]
    units = [(k, j) for k in range(n) for j in range(N_CHIPS)]
    big = (max(s[0] for s in shapes), max(s[1] for s in shapes))

    def body(*refs):
        g, v_ref = refs[:n], refs[n]
        out, o_ref = refs[n + 1:2 * n + 1], refs[2 * n + 1]
        theirs, part, recd, red = (refs[(2 + i) * n + 2:(3 + i) * n + 2] for i in range(4))
        mine, got = refs[6 * n + 2], refs[6 * n + 3]
        send1, recv1, send3, recv3, send5, recv5, load, local, send_v, recv_v = refs[6 * n + 4:]
        x, y, c = _place()
        j0 = 2 * x + y
        me = 2 * j0 + c
        sibling = (x, y, 1 - c)

        got[me] = v_ref[...]
        first, small_in = [], []
        for d in range(1, N_DEV):
            px, py, pc = x ^ (d >> 2), y ^ ((d >> 1) & 1), c ^ (d & 1)
            first.append(_remote(v_ref, got.at[me], send_v.at[d - 1], recv_v.at[d - 1], (px, py, pc)))
            small_in.append(_remote(v_ref, got.at[4 * px + 2 * py + pc], send_v.at[d - 1], recv_v.at[d - 1],
                                    (px, py, pc)))
        halves = [_remote(_sub(g[k], kinds[k], j, 1 - c), theirs[k].at[j], send1.at[u], recv1.at[u], sibling)
                  for u, (k, j) in enumerate(units)]
        for cp in first + halves:
            cp.start()

        def mine_load(u):
            k, j = units[u]
            h, cols = shapes[k]
            return pltpu.make_async_copy(_sub(g[k], kinds[k], j, c), mine.at[u % 2, pl.ds(0, h), pl.ds(0, cols)],
                                         load.at[u % 2])

        mine_load(0).start()
        for u, (k, j) in enumerate(units):
            h, cols = shapes[k]
            if u + 1 < len(units):
                mine_load(u + 1).start()
            mine_load(u).wait()
            halves[u].wait_recv()
            part[k][j] = (mine[u % 2, 0:h, 0:cols] + theirs[k][j]).astype(BF16)
            to_owner = _remote(part[k].at[j], recd[k].at[j0], send3.at[u], recv3.at[4 * k + j0], (j // 2, j % 2, c))

            @pl.when(j != j0)
            def _():
                to_owner.start()

            @pl.when(j == j0)
            def _():
                recd[k][j] = part[k][j]

        swaps = []
        for k in range(n):
            for j in range(N_CHIPS):
                arrived = _remote(part[k].at[j], recd[k].at[j], send3.at[4 * k + j], recv3.at[4 * k + j],
                                  (j // 2, j % 2, c))

                @pl.when(j != j0)
                def _():
                    arrived.wait_recv()

            r = recd[k]
            red[k][...] = ((r[0].astype(F32) + r[1].astype(F32)) + r[2].astype(F32)) + r[3].astype(F32)
            own = pltpu.make_async_copy(red[k], out[k].at[c], local.at[k])
            give = _remote(red[k], out[k].at[c], send5.at[k], recv5.at[k], sibling)
            take = _remote(red[k], out[k].at[1 - c], send5.at[k], recv5.at[k], sibling)
            own.start()
            give.start()
            swaps.append((own, give, take))

        for cp in small_in:
            cp.wait_recv()
            cp.wait_send()
        acc = got[0]
        for d in range(1, N_DEV):
            acc = acc + got[d]
        o_ref[...] = acc
        for u, (k, j) in enumerate(units):
            halves[u].wait_send()
            sent = _remote(part[k].at[j], recd[k].at[j0], send3.at[u], recv3.at[u], (j // 2, j % 2, c))

            @pl.when(j != j0)
            def _():
                sent.wait_send()
        for own, give, take in swaps:
            take.wait_recv()
            give.wait_send()
            own.wait()

    sems = [pltpu.SemaphoreType.DMA((len(units),))] * 4 + [pltpu.SemaphoreType.DMA((n,))] * 2 \
        + [pltpu.SemaphoreType.DMA((2,)), pltpu.SemaphoreType.DMA((n,))] + [pltpu.SemaphoreType.DMA((N_DEV - 1,))] * 2
    res = pl.pallas_call(
        body, name="reduce_grads",
        in_specs=[HBM] * n + [WHOLE], out_specs=[HBM] * n + [WHOLE],
        out_shape=[jax.ShapeDtypeStruct((2,) + s, F32) for s in shapes] + [jax.ShapeDtypeStruct(vec.shape, vec.dtype)],
        scratch_shapes=[pltpu.VMEM((N_CHIPS,) + s, F32) for s in shapes]
        + [pltpu.VMEM((N_CHIPS,) + s, BF16) for s in shapes] * 2
        + [pltpu.VMEM(s, F32) for s in shapes]
        + [pltpu.VMEM((2,) + big, F32), pltpu.VMEM((N_DEV,) + vec.shape, vec.dtype)] + sems,
        compiler_params=_comm_params(),
    )(*grads, vec)
    return res[:n], res[n]


def _adamw_math(w, g, m, v):
    m = ADAM_B1 * m + (1.0 - ADAM_B1) * g
    v = ADAM_B2 * v + (1.0 - ADAM_B2) * (g * g)
    m_hat = m / (1.0 - ADAM_B1 ** ADAM_STEP)
    v_hat = v / (1.0 - ADAM_B2 ** ADAM_STEP)
    return -ADAM_LR * (m_hat / (jnp.sqrt(v_hat) + ADAM_EPS) + ADAM_WD * w), m, v


def _adamw_tiled(w, g, m, v):
    rows, width = w.shape
    tm = rows // 4

    def body(w_ref, g_ref, m_ref, v_ref, go_ref, d_ref, mo_ref, vo_ref):
        g = g_ref[...]
        go_ref[...] = g
        d_ref[...], mo_ref[...], vo_ref[...] = _adamw_math(w_ref[...], g, m_ref[...], v_ref[...])

    spec = pl.BlockSpec((tm, width), lambda i: (i, 0))
    out = jax.ShapeDtypeStruct((rows, width), F32)
    return pl.pallas_call(
        body, grid=(rows // tm,), name="adamw_tiled",
        in_specs=[spec] * 4, out_specs=[spec] * 4, out_shape=[out] * 4,
        compiler_params=_params("parallel"),
    )(w, g, m, v)


def _adamw_many(ws, gs, ms, vs):
    n = len(ws)

    def body(*refs):
        for k in range(n):
            w_ref, g_ref, m_ref, v_ref = (refs[i * n + k] for i in range(4))
            go_ref, d_ref, mo_ref, vo_ref = (refs[(4 + i) * n + k] for i in range(4))
            g = g_ref[...]
            go_ref[...] = g
            d_ref[...], mo_ref[...], vo_ref[...] = _adamw_math(w_ref[...], g, m_ref[...], v_ref[...])

    outs = [jax.ShapeDtypeStruct(a.shape, F32) for a in ws]
    res = pl.pallas_call(
        body, name="adamw_many",
        in_specs=[WHOLE] * (4 * n), out_specs=[WHOLE] * (4 * n), out_shape=outs * 4,
        compiler_params=_comm_params(),
    )(*ws, *gs, *ms, *vs)
    return res[:n], res[n:2 * n], res[2 * n:3 * n], res[3 * n:]


def kernel(x, positions, a_norm, a_w_in, a_conv, a_w_out, kv_norm, w_dkv, ckv_norm, w_ukv, b_norm, b_w_in, b_q_norm, b_w_uq, b_w_out, final_norm, loss_target, m_a_norm, m_a_w_in, m_a_conv, m_a_w_out, m_kv_norm, m_w_dkv, m_ckv_norm, m_w_ukv, m_b_norm, m_b_w_in, m_b_q_norm, m_b_w_uq, m_b_w_out, m_final_norm, v_a_norm, v_a_w_in, v_a_conv, v_a_w_out, v_kv_norm, v_w_dkv, v_ckv_norm, v_w_ukv, v_b_norm, v_b_w_in, v_b_q_norm, v_b_w_uq, v_b_w_out, v_final_norm):
    w = dict(a_norm=a_norm, a_w_in=a_w_in, a_conv=a_conv, a_w_out=a_w_out, kv_norm=kv_norm, w_dkv=w_dkv,
             ckv_norm=ckv_norm, w_ukv=w_ukv, b_norm=b_norm, b_w_in=b_w_in, b_q_norm=b_q_norm, b_w_uq=b_w_uq,
             b_w_out=b_w_out, final_norm=final_norm)
    m = dict(a_norm=m_a_norm, a_w_in=m_a_w_in, a_conv=m_a_conv, a_w_out=m_a_w_out, kv_norm=m_kv_norm, w_dkv=m_w_dkv,
             ckv_norm=m_ckv_norm, w_ukv=m_w_ukv, b_norm=m_b_norm, b_w_in=m_b_w_in, b_q_norm=m_b_q_norm,
             b_w_uq=m_b_w_uq, b_w_out=m_b_w_out, final_norm=m_final_norm)
    v = dict(a_norm=v_a_norm, a_w_in=v_a_w_in, a_conv=v_a_conv, a_w_out=v_a_w_out, kv_norm=v_kv_norm, w_dkv=v_w_dkv,
             ckv_norm=v_ckv_norm, w_ukv=v_w_ukv, b_norm=v_b_norm, b_w_in=v_b_w_in, b_q_norm=v_b_q_norm,
             b_w_uq=v_b_w_uq, b_w_out=v_b_w_out, final_norm=v_final_norm)

    loss, dx, gmat, gsmall = _local_step(x, positions, loss_target, w)

    kinds = [GRAD_KIND[n] for n in GRAD_MATS]
    grads = [gmat[n] for n in GRAD_MATS]
    flat = jnp.concatenate([gsmall[n].reshape(-1) for n in SMALL] + [loss.reshape(1)])
    flat = jnp.pad(flat, (0, SMALL_ROWS * PACK_W - flat.shape[0])).reshape(SMALL_ROWS, PACK_W)
    mine, svec = _reduce_grads(grads, kinds, flat)
    loss = svec.reshape(-1)[LOSS_SLOT]
    g = _shard_grads({n: a.reshape(-1, a.shape[-1]) for n, a in zip(GRAD_MATS, mine)}, svec)

    def two_d(n, a):
        a = a.reshape(-1, a.shape[-1])
        return a.T if n in TRANSPOSED else a

    def back(n, a):
        return (a.T if n in TRANSPOSED else a).reshape(SHARD_SHAPES[n])

    big = "a_w_in"
    rest = [n for n in WEIGHTS if n != big]
    res_big = _adamw_tiled(*(two_d(big, t[big]) for t in (w, g, m, v)))
    res_rest = _adamw_many(*([two_d(n, t[n]) for n in rest] for t in (w, g, m, v)))
    out = {kind: dict(zip(rest, res_rest[i])) for i, kind in enumerate("gdmv")}
    for i, kind in enumerate("gdmv"):
        out[kind][big] = res_big[i]
    return (loss, dx) + tuple(back(n, out[kind][n]) for kind in "gdmv" for n in WEIGHTS)
```

```python
import functools
import math

import numpy as np
import jax
import jax.numpy as jnp
from jax import lax
from jax.experimental import pallas as pl
from jax.experimental.pallas import tpu as pltpu

F32 = jnp.float32
BF16 = jnp.bfloat16

D_MODEL = 1024
E_A = 1024
CONV_WIDTH = 3
N_HEADS = 8
QK_NOPE = 64
QK_ROPE = 32
V_HEAD = 64
KV_RANK = 256
Q_RANK = 384
E_B = N_HEADS * V_HEAD
ROPE_THETA = 10000.0
SOFTMAX_SCALE = 1.0 / math.sqrt(QK_NOPE + QK_ROPE)
LOG2E = math.log2(math.e)
LN2 = math.log(2.0)
Q_PRESCALE = SOFTMAX_SCALE * LOG2E
EPS = 1e-6
HEAD_PAD = 128
QK_PAD = N_HEADS * HEAD_PAD
ROPE_LO = QK_NOPE
ROPE_HALF = QK_ROPE // 2
KR_PAD = KV_RANK + HEAD_PAD

ADAM_LR = 0.001
ADAM_B1 = 0.9
ADAM_B2 = 0.999
ADAM_EPS = 1e-08
ADAM_WD = 0.01
ADAM_STEP = 10

VMEM_LIMIT = 56 * 1024 * 1024
ROW_TILE = 512
ROW_TILE_WIDE = 2 * ROW_TILE
CONV_BWD_TILE = 256
ATT_TILE_FWD = 1024
ATT_TILE = 512
HEAD_CHAINS = 4
LANES = 128
PACK_W = 1024

N_CHIPS = 4
N_DEV = 8


def _dot(a, b):
    return jnp.dot(a, b, preferred_element_type=F32)


def _dot_nt(a, b):
    return lax.dot_general(a, b, (((1,), (1,)), ((), ())), preferred_element_type=F32)


def _dot_tn(a, b):
    return lax.dot_general(a, b, (((0,), (0,)), ((), ())), preferred_element_type=F32)


def _rms(x):
    r = lax.rsqrt(jnp.mean(x * x, axis=-1, keepdims=True) + EPS)
    return r, x * r


def _rms_bwd(dxh, xh, r):
    return r * (dxh - xh * jnp.mean(dxh * xh, axis=-1, keepdims=True))


def _rope_fwd(a, c, s1, s2):
    return a * c + pltpu.roll(a, HEAD_PAD - ROPE_HALF, 1) * s1 + pltpu.roll(a, ROPE_HALF, 1) * s2


def _rope_bwd(g, c, s1, s2):
    return g * c + pltpu.roll(g * s1, ROPE_HALF, 1) + pltpu.roll(g * s2, HEAD_PAD - ROPE_HALF, 1)


def _sigmoid(x):
    return 1.0 / (1.0 + jnp.exp(-x))


def _row_spec(tm, n):
    return pl.BlockSpec((tm, n), lambda i: (i, 0))


def _const_spec(shape):
    nd = len(shape)
    return pl.BlockSpec(shape, lambda i: (0,) * nd, pipeline_mode=pl.Buffered(1))


def _acc_spec(shape):
    nd = len(shape)
    return pl.BlockSpec(shape, lambda i: (0,) * nd, pipeline_mode=pl.Buffered(1))


def _params(*sem):
    return pltpu.CompilerParams(dimension_semantics=sem, vmem_limit_bytes=VMEM_LIMIT)


MESH = pl.DeviceIdType.MESH
HBM = pl.BlockSpec(memory_space=pl.ANY)
WHOLE = pl.BlockSpec(memory_space=pltpu.VMEM)
FLIPS = ((1, 0), (0, 1), (1, 1))


def _place():
    return lax.axis_index("x"), lax.axis_index("y"), lax.axis_index("c")


def _remote(src, dst, send, recv, peer):
    return pltpu.make_async_remote_copy(src_ref=src, dst_ref=dst, send_sem=send, recv_sem=recv, device_id=peer,
                                        device_id_type=MESH)


def _comm_params():
    return pltpu.CompilerParams(vmem_limit_bytes=VMEM_LIMIT)


def _rope_consts():
    lane = np.arange(HEAD_PAD)
    first = (lane >= ROPE_LO) & (lane < ROPE_LO + ROPE_HALF)
    second = (lane >= ROPE_LO + ROPE_HALF) & (lane < ROPE_LO + QK_ROPE)
    f = np.where(first, lane - ROPE_LO, np.where(second, lane - ROPE_LO - ROPE_HALF, 0))
    inv = np.float32(ROPE_THETA) ** (-(2 * f).astype(np.float32) / np.float32(QK_ROPE))
    out = np.zeros((8, HEAD_PAD), np.float32)
    out[0] = inv
    out[1] = first
    out[2] = second
    out[3] = lane < ROPE_LO
    return jnp.asarray(out)


def _gather_plan(w, sm, outs, osm, bf, sems):
    n = len(w)
    send_i, recv_i, send_d, recv_d, send_s, recv_s, local = sems
    x, y, c = _place()
    j0 = 2 * x + y
    sibling = (x, y, 1 - c)
    own = [pltpu.make_async_copy(bf[k], outs[k].at[j0], local.at[k]) for k in range(n)]
    if sm is not None:
        own.append(pltpu.make_async_copy(sm, osm.at[j0], local.at[n]))

    def half(k, cc):
        h = w[k].shape[0] // 2
        return pl.ds(pl.multiple_of(cc * h, 16), h)

    sends, arrivals, forwards, fwaits = [], [], [], []
    for i, (fx, fy) in enumerate(FLIPS):
        px, py = x ^ fx, y ^ fy
        pj = 2 * px + py
        for k in range(n):
            s = i * n + k
            sends.append(_remote(bf[k].at[half(k, c)], outs[k].at[j0, half(k, c)], send_i.at[s], recv_i.at[s],
                                 (px, py, c)))
            arrivals.append(_remote(bf[k].at[half(k, c)], outs[k].at[pj, half(k, c)], send_i.at[s], recv_i.at[s],
                                    (px, py, c)))
            forwards.append(_remote(outs[k].at[pj, half(k, c)], outs[k].at[pj, half(k, c)], send_d.at[s],
                                    recv_d.at[s], sibling))
            fwaits.append(_remote(outs[k].at[pj, half(k, 1 - c)], outs[k].at[pj, half(k, 1 - c)], send_d.at[s],
                                  recv_d.at[s], sibling))
        if sm is not None:
            sends.append(_remote(sm, osm.at[j0], send_s.at[i], recv_s.at[i], (px, py, c)))
            fwaits.append(_remote(sm, osm.at[pj], send_s.at[i], recv_s.at[i], (px, py, c)))
    return own, sends, arrivals, forwards, fwaits


def _gather_begin(w, bf, plan):
    own, sends, _, _, _ = plan
    for k in range(len(w)):
        bf[k][...] = w[k][...].astype(BF16)
    for cp in own + sends:
        cp.start()


def _gather_end(plan):
    own, sends, arrivals, forwards, fwaits = plan
    for cp, fwd in zip(arrivals, forwards):
        cp.wait_recv()
        fwd.start()
    for cp in fwaits:
        cp.wait_recv()
    for cp in sends + forwards:
        cp.wait_send()
    for cp in own:
        cp.wait()


def _gather_sems(n, with_small):
    return ([pltpu.SemaphoreType.DMA((3 * n,))] * 4 + [pltpu.SemaphoreType.DMA((3,))] * 2
            + [pltpu.SemaphoreType.DMA((n + (1 if with_small else 0),))])


def _rope_tables_gather(pos_col, w_in, small):
    t = pos_col.shape[0]
    tm = min(ROW_TILE, t)
    steps = t // tm

    def body(p_ref, k_ref, w_ref, sm_ref, c_ref, s1_ref, s2_ref, out_ref, osm_ref, bf_ref, *sems):
        plan = _gather_plan([w_ref], sm_ref, [out_ref], osm_ref, [bf_ref], sems)
        i = pl.program_id(0)

        @pl.when(i == 0)
        def _():
            _gather_begin([w_ref], [bf_ref], plan)

        inv, first, second, nope = k_ref[0:1, :], k_ref[1:2, :], k_ref[2:3, :], k_ref[3:4, :]
        ang = p_ref[...].astype(F32) * inv
        cs, sn = jnp.cos(ang), jnp.sin(ang)
        c_ref[...] = cs * (first + second) + nope
        s1_ref[...] = -sn * first
        s2_ref[...] = sn * second

        @pl.when(i == steps - 1)
        def _():
            _gather_end(plan)

    out = jax.ShapeDtypeStruct((t, HEAD_PAD), F32)
    return pl.pallas_call(
        body, grid=(steps,), name="rope_tables_gather",
        in_specs=[_row_spec(tm, 1), _const_spec((8, HEAD_PAD)), WHOLE, WHOLE],
        out_specs=[_row_spec(tm, HEAD_PAD)] * 3 + [HBM, HBM],
        out_shape=[out] * 3 + [jax.ShapeDtypeStruct((N_CHIPS,) + w_in.shape, BF16),
                               jax.ShapeDtypeStruct((N_CHIPS,) + small.shape, small.dtype)],
        scratch_shapes=[pltpu.VMEM(w_in.shape, BF16)] + _gather_sems(1, True),
        compiler_params=_params("arbitrary"),
    )(pos_col, _rope_consts(), w_in, small)


def _shift_down(v, prev, row):
    p1, p2 = prev[7:8, :], prev[6:7, :]
    v1 = jnp.where(row == 0, p1, pltpu.roll(v, 1, 0))
    v2 = jnp.where(row == 0, p2, jnp.where(row == 1, p1, pltpu.roll(v, 2, 0)))
    return v1, v2


def _conv_fwd(x, seq, ga, w_in4, cw, shards):
    t = x.shape[0]
    tm = min(ROW_TILE, seq)
    tiles_per_seq = seq // tm
    steps = t // tm
    n = len(shards)

    def body(x_ref, ga_ref, w_ref, cw_ref, *rest):
        sh, (c_ref, u_ref, v_ref, ym_ref, kb_ref, kcv_ref, kg_ref) = rest[:n], rest[n:n + 7]
        outs, carry_ref, bf, sems = rest[n + 7:2 * n + 7], rest[2 * n + 7], rest[2 * n + 8:3 * n + 8], rest[3 * n + 8:]
        plan = _gather_plan(sh, None, outs, None, bf, sems)
        i = pl.program_id(0)

        @pl.when(i == 0)
        def _():
            _gather_begin(sh, bf, plan)

        @pl.when(i % tiles_per_seq == 0)
        def _():
            carry_ref[...] = jnp.zeros_like(carry_ref)

        _, xh = _rms(x_ref[...])
        h = (xh * ga_ref[...]).astype(BF16)
        c, u = _dot(h, w_ref[1]), _dot(h, w_ref[2])
        g, b = _dot(h, w_ref[3]), _dot(h, w_ref[0])
        v = c * u
        row = lax.broadcasted_iota(jnp.int32, (tm, 1), 0)
        v1, v2 = _shift_down(v, carry_ref[...], row)
        carry_ref[...] = v[tm - 8:tm, :]
        cv = cw_ref[2:3, :] * v + cw_ref[1:2, :] * v1 + cw_ref[0:1, :] * v2
        sg = _sigmoid(g)
        silu = g * sg
        c_ref[...] = c.astype(BF16)
        u_ref[...] = u.astype(BF16)
        v_ref[...] = v.astype(BF16)
        ym_ref[...] = (silu * b * cv).astype(BF16)
        kb_ref[...] = (silu * cv).astype(BF16)
        kcv_ref[...] = (silu * b).astype(BF16)
        kg_ref[...] = (b * cv * (sg * (1.0 + g * (1.0 - sg)))).astype(BF16)

        @pl.when(i == steps - 1)
        def _():
            _gather_end(plan)

    out = jax.ShapeDtypeStruct((t, E_A), BF16)
    res = pl.pallas_call(
        body, grid=(steps,), name="conv_fwd",
        in_specs=[_row_spec(tm, D_MODEL), _const_spec((1, D_MODEL)), _const_spec((4, D_MODEL, E_A)),
                  _const_spec((8, E_A))] + [WHOLE] * n,
        out_specs=[_row_spec(tm, E_A)] * 7 + [HBM] * n,
        out_shape=[out] * 7 + [jax.ShapeDtypeStruct((N_CHIPS,) + a.shape, BF16) for a in shards],
        scratch_shapes=[pltpu.VMEM((8, E_A), F32)] + [pltpu.VMEM(a.shape, BF16) for a in shards]
        + _gather_sems(n, False),
        compiler_params=_params("arbitrary"),
    )(x, ga, w_in4, cw, *shards)
    return res[:7], res[7:]


def _mid_fwd(x, ym, w_out, gk, gb, w_dkv, gc, w_uk, w_uv, w_bin, gq, w_uq, rc, rs1, rs2):
    t = x.shape[0]
    tm = min(ROW_TILE_WIDE, t)

    def body(x_ref, ym_ref, wo_ref, gk_ref, gb_ref, wd_ref, gc_ref, wk_ref, wv_ref, wi_ref, gq_ref, wq_ref,
             c_ref, s1_ref, s2_ref, x1_ref, q_ref, k_ref, v_ref, gate_ref, cq_ref, ckv_ref):
        cb, s1b, s2b = c_ref[...], s1_ref[...], s2_ref[...]
        x1 = x_ref[...] + _dot(ym_ref[...], wo_ref[...])
        x1_ref[...] = x1
        _, xh = _rms(x1)
        hk = (xh * gk_ref[...]).astype(BF16)
        h1 = (xh * gb_ref[...]).astype(BF16)

        pb = _dot(h1, wi_ref[...])
        cq = pb[:, :Q_RANK]
        cq_ref[...] = cq.astype(BF16)
        gate_ref[...] = pb[:, Q_RANK:].astype(BF16)
        _, cqh = _rms(cq)
        q = _dot((cqh * gq_ref[...]).astype(BF16), wq_ref[...])
        for h in range(N_HEADS):
            sl = slice(h * HEAD_PAD, (h + 1) * HEAD_PAD)
            q_ref[:, sl] = (_rope_fwd(q[:, sl], cb, s1b, s2b) * Q_PRESCALE).astype(BF16)

        ckr = _dot(hk, wd_ref[...])
        ckv_raw = ckr[:, :KV_RANK]
        ckv_ref[...] = ckv_raw.astype(BF16)
        _, ch = _rms(ckv_raw)
        ckv = (ch * gc_ref[...]).astype(BF16)
        kr = _rope_fwd(ckr[:, KV_RANK:], cb, s1b, s2b)
        kn = _dot(ckv, wk_ref[...])
        for h in range(N_HEADS):
            sl = slice(h * HEAD_PAD, (h + 1) * HEAD_PAD)
            k_ref[:, sl] = (kn[:, sl] + kr).astype(BF16)
        v_ref[...] = _dot(ckv, wv_ref[...]).astype(BF16)

    def sds(n, dt):
        return jax.ShapeDtypeStruct((t, n), dt)

    return pl.pallas_call(
        body, grid=(t // tm,), name="mid_fwd",
        in_specs=[_row_spec(tm, D_MODEL), _row_spec(tm, E_A), _const_spec((E_A, D_MODEL)),
                  _const_spec((1, D_MODEL)), _const_spec((1, D_MODEL)), _const_spec((D_MODEL, KR_PAD)),
                  _const_spec((1, KV_RANK)), _const_spec((KV_RANK, QK_PAD)), _const_spec((KV_RANK, E_B)),
                  _const_spec((D_MODEL, Q_RANK + E_B)), _const_spec((1, Q_RANK)), _const_spec((Q_RANK, QK_PAD)),
                  _row_spec(tm, HEAD_PAD), _row_spec(tm, HEAD_PAD), _row_spec(tm, HEAD_PAD)],
        out_specs=[_row_spec(tm, D_MODEL), _row_spec(tm, QK_PAD), _row_spec(tm, QK_PAD), _row_spec(tm, E_B),
                   _row_spec(tm, E_B), _row_spec(tm, Q_RANK), _row_spec(tm, KV_RANK)],
        out_shape=[sds(D_MODEL, F32), sds(QK_PAD, BF16), sds(QK_PAD, BF16), sds(E_B, BF16), sds(E_B, BF16),
                   sds(Q_RANK, BF16), sds(KV_RANK, BF16)],
        compiler_params=_params("parallel"),
    )(x, ym, w_out, gk, gb, w_dkv, gc, w_uk, w_uv, w_bin, gq, w_uq, rc, rs1, rs2)


def _pair_specs(seq):
    qk = pl.BlockSpec((seq, 2 * HEAD_PAD), lambda b, p: (b, p))
    vo = pl.BlockSpec((seq, 2 * V_HEAD), lambda b, p: (b, p))
    st = pl.BlockSpec((None, 2, seq), lambda b, p: (p, 0, b))
    return qk, vo, st


def _attn_fwd(q, k, v, seq):
    t = q.shape[0]
    tq = min(ATT_TILE_FWD, seq)
    nq = seq // tq

    def body(q_ref, k_ref, v_ref, o_ref, lse_ref, m_scr, l_scr, acc_scr):
        lane = lax.broadcasted_iota(jnp.int32, (tq, 2 * V_HEAD), 1)

        def q_step(qi, _):
            q0 = pl.multiple_of(qi * tq, tq)
            m_scr[...] = jnp.full(m_scr.shape, -jnp.inf, F32)
            l_scr[...] = jnp.zeros_like(l_scr)
            acc_scr[...] = jnp.zeros_like(acc_scr)

            def block(q_lo, q_n, k0, k_n, masked):
                rows = slice(q_lo, q_lo + q_n)
                vt = v_ref[pl.ds(k0, k_n), :]

                def score(hh):
                    hs = slice(hh * HEAD_PAD, (hh + 1) * HEAD_PAD)
                    return _dot_nt(q_ref[pl.ds(q0 + q_lo, q_n), hs], k_ref[pl.ds(k0, k_n), hs])

                early = [score(hh) for hh in range(2)] if masked else None
                for hh in range(2):
                    s = early[hh] if masked else score(hh)
                    if masked:
                        row = lax.broadcasted_iota(jnp.int32, (q_n, k_n), 0)
                        col = lax.broadcasted_iota(jnp.int32, (q_n, k_n), 1)
                        s = jnp.where(col <= row, s, -jnp.inf)
                    m_old = m_scr[hh, rows]
                    m_new = jnp.maximum(m_old, jnp.max(s, axis=-1, keepdims=True))
                    alpha = jnp.exp2(m_old - m_new)
                    ps = [jnp.exp2(s[:, j * LANES:(j + 1) * LANES] - m_new) for j in range(k_n // LANES)]
                    l_scr[hh, rows] = alpha * l_scr[hh, rows] + functools.reduce(lambda a, b: a + b, ps)
                    p = jnp.concatenate(ps, axis=-1).astype(BF16)
                    acc_scr[hh, rows] = alpha * acc_scr[hh, rows] + _dot(p, vt)
                    m_scr[hh, rows] = m_new

            def k_step(ki, _):
                block(0, tq, pl.multiple_of(ki * tq, tq), tq, False)
                return 0

            lax.fori_loop(0, qi, k_step, 0)
            half = tq // 2
            block(0, tq, q0, half, True)
            block(half, half, q0 + half, half, True)
            l0 = jnp.sum(l_scr[0], axis=-1, keepdims=True)
            l1 = jnp.sum(l_scr[1], axis=-1, keepdims=True)
            o_ref[pl.ds(q0, tq), :] = jnp.where(lane < V_HEAD, acc_scr[0] / l0, acc_scr[1] / l1).astype(BF16)
            stats = jnp.where(lane == 0, m_scr[0] + jnp.log2(l0), m_scr[1] + jnp.log2(l1)).T
            lse_ref[:, pl.ds(q0, tq)] = stats[0:2, :]
            return 0

        lax.fori_loop(0, nq, q_step, 0)

    qk, vo, st = _pair_specs(seq)
    return pl.pallas_call(
        body, grid=(t // seq, N_HEADS // 2), name="attn_fwd",
        in_specs=[qk, qk, vo], out_specs=[vo, st],
        out_shape=[jax.ShapeDtypeStruct((t, E_B), BF16), jax.ShapeDtypeStruct((N_HEADS // 2, 2, t), F32)],
        scratch_shapes=[pltpu.VMEM((2, tq, LANES), F32), pltpu.VMEM((2, tq, LANES), F32),
                        pltpu.VMEM((2, tq, 2 * V_HEAD), F32)],
        compiler_params=_params("parallel", "parallel"),
    )(q, k, v)


def _head_fwd_bwd(o, gate, x1, tgt, w_bout, gf):
    t = o.shape[0]
    tm = min(ROW_TILE_WIDE, t)

    def body(o_ref, gate_ref, x1_ref, tgt_ref, w_ref, gf_ref,
             dx2_ref, do_ref, dgate_ref, dd_ref, loss_ref, dgf_ref, dw_ref):
        @pl.when(pl.program_id(0) == 0)
        def _():
            loss_ref[...] = jnp.zeros_like(loss_ref)
            dgf_ref[...] = jnp.zeros_like(dgf_ref)
            dw_ref[...] = jnp.zeros_like(dw_ref)

        hm = tm // HEAD_CHAINS
        gf = gf_ref[...]
        lane = lax.broadcasted_iota(jnp.int32, (hm, 2 * V_HEAD), 1)
        chains = []
        for ch in range(HEAD_CHAINS):
            rs = pl.ds(ch * hm, hm)
            o = o_ref[rs, :].astype(F32)
            gt = gate_ref[rs, :].astype(F32)
            sg = _sigmoid(gt)
            silu = gt * sg
            z = (o * silu).astype(BF16)
            chains.append((rs, o, gt, sg, silu, z, x1_ref[rs, :] + _dot(z, w_ref[...])))
        mids = []
        for rs, o, gt, sg, silu, z, x2 in chains:
            r2, xh2 = _rms(x2)
            err = xh2 * gf - tgt_ref[rs, :]
            loss_ref[...] += 0.5 * jnp.sum(jnp.mean(err * err, axis=-1, keepdims=True))
            dy = err * (1.0 / D_MODEL)
            dgf_ref[...] += jnp.sum(dy * xh2, axis=0, keepdims=True)
            dx2 = _rms_bwd(dy * gf, xh2, r2)
            dx2_ref[rs, :] = dx2
            dx2b = dx2.astype(BF16)
            mids.append(_dot_nt(dx2b, w_ref[...]))
            dw_ref[...] += _dot_tn(z, dx2b)
        for (rs, o, gt, sg, silu, z, x2), dz in zip(chains, mids):
            do = dz * silu
            do_ref[rs, :] = do.astype(BF16)
            dgate_ref[rs, :] = (dz * o * (sg * (1.0 + gt * (1.0 - sg)))).astype(BF16)
            prod = do * o
            cols = jnp.zeros((hm, LANES), F32)
            for p in range(N_HEADS // 2):
                blk = prod[:, p * 2 * V_HEAD:(p + 1) * 2 * V_HEAD]
                d0 = jnp.sum(jnp.where(lane < V_HEAD, blk, 0.0), axis=-1, keepdims=True)
                d1 = jnp.sum(jnp.where(lane < V_HEAD, 0.0, blk), axis=-1, keepdims=True)
                cols = jnp.where(lane == 2 * p, d0, jnp.where(lane == 2 * p + 1, d1, cols))
            rows = cols.T
            for h in range(N_HEADS):
                dd_ref[h // 2, h % 2:h % 2 + 1, rs] = rows[h:h + 1, :]

    return pl.pallas_call(
        body, grid=(t // tm,), name="head_fwd_bwd",
        in_specs=[_row_spec(tm, E_B), _row_spec(tm, E_B), _row_spec(tm, D_MODEL), _row_spec(tm, D_MODEL),
                  _const_spec((E_B, D_MODEL)), _const_spec((1, D_MODEL))],
        out_specs=[_row_spec(tm, D_MODEL), _row_spec(tm, E_B), _row_spec(tm, E_B),
                   pl.BlockSpec((N_HEADS // 2, 2, tm), lambda i: (0, 0, i)),
                   _acc_spec((1, 1)), _acc_spec((1, D_MODEL)), _acc_spec((E_B, D_MODEL))],
        out_shape=[jax.ShapeDtypeStruct((t, D_MODEL), F32), jax.ShapeDtypeStruct((t, E_B), BF16),
                   jax.ShapeDtypeStruct((t, E_B), BF16), jax.ShapeDtypeStruct((N_HEADS // 2, 2, t), F32),
                   jax.ShapeDtypeStruct((1, 1), F32), jax.ShapeDtypeStruct((1, D_MODEL), F32),
                   jax.ShapeDtypeStruct((E_B, D_MODEL), F32)],
        compiler_params=_params("arbitrary"),
    )(o, gate, x1, tgt, w_bout, gf)


def _attn_bwd(q, k, v, do, lse, dd, seq):
    t = q.shape[0]
    tq = min(ATT_TILE, seq)
    nq = seq // tq
    assert nq % 2 == 0, "full tiles are taken in pairs"

    def body(q_ref, k_ref, v_ref, do_ref, lse_ref, dd_ref, dq_ref, dk_ref, dv_ref, dq_acc, dk_acc, dv_acc):
        dq_acc[...] = jnp.zeros_like(dq_acc)

        def k_step(ki, _):
            k0 = pl.multiple_of(ki * tq, tq)
            dk_acc[...] = jnp.zeros_like(dk_acc)
            dv_acc[...] = jnp.zeros_like(dv_acc)

            def block(k_lo, k_n, q0, q_n, masked):
                rows = slice(k_lo, k_lo + k_n)
                lane = lax.broadcasted_iota(jnp.int32, (q_n, 2 * V_HEAD), 1)
                vt = v_ref[pl.ds(k0 + k_lo, k_n), :]
                do_pair = do_ref[pl.ds(q0, q_n), :]

                def operands(hh):
                    hs = slice(hh * HEAD_PAD, (hh + 1) * HEAD_PAD)
                    kt = k_ref[pl.ds(k0 + k_lo, k_n), hs]
                    qt = q_ref[pl.ds(q0, q_n), hs]
                    mine = (lane < V_HEAD) if hh == 0 else (lane >= V_HEAD)
                    do_h = jnp.where(mine, do_pair, jnp.zeros((), BF16))
                    return hs, kt, qt, do_h, _dot_nt(kt, qt), _dot_nt(vt, do_h)

                early = [operands(hh) for hh in range(2)] if masked else None
                for hh in range(2):
                    hs, kt, qt, do_h, st, dpt = early[hh] if masked else operands(hh)
                    if masked:
                        krow = lax.broadcasted_iota(jnp.int32, (k_n, q_n), 0)
                        qcol = lax.broadcasted_iota(jnp.int32, (k_n, q_n), 1)
                        st = jnp.where(krow <= qcol, st, -jnp.inf)
                    pt = jnp.exp2(st - lse_ref[hh:hh + 1, pl.ds(q0, q_n)])
                    dst = (pt * (dpt - dd_ref[hh:hh + 1, pl.ds(q0, q_n)])).astype(BF16)
                    dv_acc[rows, :] += _dot(pt.astype(BF16), do_h)
                    dk_acc[rows, hs] += _dot(dst, qt)
                    dq_acc[pl.ds(q0, q_n), hs] += _dot_tn(dst, kt)

            wide = 2 * tq

            def q_step(qj, _):
                block(0, tq, pl.multiple_of(qj * wide, wide), wide, False)
                return 0

            half = tq // 2
            block(0, half, k0, tq, True)
            block(half, half, pl.multiple_of(k0 + half, half), half, True)

            @pl.when(ki % 2 == 0)
            def _():
                block(0, tq, pl.multiple_of(k0 + tq, tq), tq, False)

            lax.fori_loop(ki // 2 + 1, nq // 2, q_step, 0)
            dk_ref[pl.ds(k0, tq), :] = (dk_acc[...] * LN2).astype(BF16)
            dv_ref[pl.ds(k0, tq), :] = dv_acc[...].astype(BF16)
            return 0

        lax.fori_loop(0, nq, k_step, 0)
        dq_ref[...] = (dq_acc[...] * SOFTMAX_SCALE).astype(BF16)

    qk, vo, st = _pair_specs(seq)
    return pl.pallas_call(
        body, grid=(t // seq, N_HEADS // 2), name="attn_bwd",
        in_specs=[qk, qk, vo, vo, st, st], out_specs=[qk, qk, vo],
        out_shape=[jax.ShapeDtypeStruct((t, QK_PAD), BF16), jax.ShapeDtypeStruct((t, QK_PAD), BF16),
                   jax.ShapeDtypeStruct((t, E_B), BF16)],
        scratch_shapes=[pltpu.VMEM((seq, 2 * HEAD_PAD), F32), pltpu.VMEM((tq, 2 * HEAD_PAD), F32),
                        pltpu.VMEM((tq, 2 * V_HEAD), F32)],
        compiler_params=_params("parallel", "parallel"),
    )(q, k, v, do, lse, dd)


def _mid_bwd(dq, dk, dv, dgate, dx2, x1, cq, ckv, rc, rs1, rs2, w_uq, w_bin, w_uk, w_uv, w_dkv, gq, gc, gk, gb):
    t = dq.shape[0]
    tm = min(ROW_TILE, t)

    def body(dq_ref, dk_ref, dv_ref, dgate_ref, dx2_ref, x1_ref, cq_ref, ckv_ref, c_ref, s1_ref, s2_ref,
             wq_ref, wi_ref, wk_ref, wv_ref, wd_ref, gq_ref, gc_ref, gk_ref, gb_ref,
             dx1_ref, dwq_ref, dwi_ref, dwk_ref, dwv_ref, dwd_ref, dgq_ref, dgc_ref, dgk_ref, dgb_ref):
        @pl.when(pl.program_id(0) == 0)
        def _():
            for ref in (dwq_ref, dwi_ref, dwk_ref, dwv_ref, dwd_ref, dgq_ref, dgc_ref, dgk_ref, dgb_ref):
                ref[...] = jnp.zeros_like(ref)

        cb, s1b, s2b = c_ref[...], s1_ref[...], s2_ref[...]
        gk, gb, gq, gc = gk_ref[...], gb_ref[...], gq_ref[...], gc_ref[...]

        dkb, dvb = dk_ref[...], dv_ref[...]
        dckv = _dot_nt(dkb, wk_ref[...]) + _dot_nt(dvb, wv_ref[...])
        rcv, ch = _rms(ckv_ref[...].astype(F32))
        ckvn = (ch * gc).astype(BF16)
        dwk_ref[...] += _dot_tn(ckvn, dkb)
        dwv_ref[...] += _dot_tn(ckvn, dvb)

        dqs = [_rope_bwd(dq_ref[:, h * HEAD_PAD:(h + 1) * HEAD_PAD].astype(F32), cb, s1b, s2b)
               for h in range(N_HEADS)]
        dqb = jnp.concatenate(dqs, axis=-1).astype(BF16)
        rq, cqh = _rms(cq_ref[...].astype(F32))
        dcqn = _dot_nt(dqb, wq_ref[...])
        dwq_ref[...] += _dot_tn((cqh * gq).astype(BF16), dqb)

        dgc_ref[...] += jnp.sum(dckv * ch, axis=0, keepdims=True)
        dckv_raw = _rms_bwd(dckv * gc, ch, rcv)
        dkr = dk_ref[:, 0:HEAD_PAD].astype(F32)
        for h in range(1, N_HEADS):
            dkr = dkr + dk_ref[:, h * HEAD_PAD:(h + 1) * HEAD_PAD].astype(F32)
        dkr = _rope_bwd(dkr, cb, s1b, s2b)
        dckr = jnp.concatenate([dckv_raw, dkr], axis=-1).astype(BF16)
        r1, xh = _rms(x1_ref[...])
        dhk = _dot_nt(dckr, wd_ref[...])
        dwd_ref[...] += _dot_tn((xh * gk).astype(BF16), dckr)

        dgq_ref[...] += jnp.sum(dcqn * cqh, axis=0, keepdims=True)
        dcq = _rms_bwd(dcqn * gq, cqh, rq)
        dpb = jnp.concatenate([dcq.astype(BF16), dgate_ref[...]], axis=-1)
        dh1 = _dot_nt(dpb, wi_ref[...])
        dwi_ref[...] += _dot_tn((xh * gb).astype(BF16), dpb)

        dgb_ref[...] += jnp.sum(dh1 * xh, axis=0, keepdims=True)
        dgk_ref[...] += jnp.sum(dhk * xh, axis=0, keepdims=True)
        dx1_ref[...] = dx2_ref[...] + _rms_bwd(dh1 * gb + dhk * gk, xh, r1)

    acc_shapes = [(Q_RANK, QK_PAD), (D_MODEL, Q_RANK + E_B), (KV_RANK, QK_PAD), (KV_RANK, E_B), (D_MODEL, KR_PAD),
                  (1, Q_RANK), (1, KV_RANK), (1, D_MODEL), (1, D_MODEL)]
    return pl.pallas_call(
        body, grid=(t // tm,), name="mid_bwd",
        in_specs=[_row_spec(tm, QK_PAD), _row_spec(tm, QK_PAD), _row_spec(tm, E_B), _row_spec(tm, E_B),
                  _row_spec(tm, D_MODEL), _row_spec(tm, D_MODEL), _row_spec(tm, Q_RANK), _row_spec(tm, KV_RANK),
                  _row_spec(tm, HEAD_PAD), _row_spec(tm, HEAD_PAD), _row_spec(tm, HEAD_PAD),
                  _const_spec((Q_RANK, QK_PAD)), _const_spec((D_MODEL, Q_RANK + E_B)),
                  _const_spec((KV_RANK, QK_PAD)), _const_spec((KV_RANK, E_B)), _const_spec((D_MODEL, KR_PAD)),
                  _const_spec((1, Q_RANK)), _const_spec((1, KV_RANK)), _const_spec((1, D_MODEL)),
                  _const_spec((1, D_MODEL))],
        out_specs=[_row_spec(tm, D_MODEL)] + [_acc_spec(s) for s in acc_shapes],
        out_shape=[jax.ShapeDtypeStruct((t, D_MODEL), F32)] + [jax.ShapeDtypeStruct(s, F32) for s in acc_shapes],
        compiler_params=_params("arbitrary"),
    )(dq, dk, dv, dgate, dx2, x1, cq, ckv, rc, rs1, rs2, w_uq, w_bin, w_uk, w_uv, w_dkv, gq, gc, gk, gb)


def _conv_bwd(dx1, x, c, u, v, ym, kb, kcv, kg, seq, w_out, w_in4, ga, cw):
    t = x.shape[0]
    tm = min(CONV_BWD_TILE, seq)
    tiles_per_seq = seq // tm
    n = t // tm

    def tile(i):
        return n - 1 - i

    def rev(width):
        return pl.BlockSpec((tm, width), lambda i: (tile(i), 0))

    def body(dx1_ref, x_ref, c_ref, u_ref, v_ref, ym_ref, kb_ref, kcv_ref, kg_ref, wo_ref, wi_ref, ga_ref, cw_ref,
             dx_ref, dwi_ref, dwo_ref, dcw_ref, dga_ref, carry_ref):
        i = pl.program_id(0)
        j = tile(i)

        @pl.when(i == 0)
        def _():
            for ref in (dwi_ref, dwo_ref, dcw_ref, dga_ref):
                ref[...] = jnp.zeros_like(ref)

        @pl.when(j % tiles_per_seq == tiles_per_seq - 1)
        def _():
            carry_ref[...] = jnp.zeros_like(carry_ref)

        dx1 = dx1_ref[...]
        dx1b = dx1.astype(BF16)
        dym = _dot_nt(dx1b, wo_ref[...])
        dwo_ref[...] += _dot_tn(ym_ref[...], dx1b)
        db = dym * kb_ref[...].astype(F32)
        dcv = dym * kcv_ref[...].astype(F32)
        dg = dym * kg_ref[...].astype(F32)

        row = lax.broadcasted_iota(jnp.int32, (tm, 1), 0)
        w0, w1, w2 = cw_ref[0:1, :], cw_ref[1:2, :], cw_ref[2:3, :]
        nxt = carry_ref[...]
        n0, n1 = nxt[0:1, :], nxt[1:2, :]
        d1 = jnp.where(row == tm - 1, n0, pltpu.roll(dcv, tm - 1, 0))
        d2 = jnp.where(row == tm - 1, n1, jnp.where(row == tm - 2, n0, pltpu.roll(dcv, tm - 2, 0)))
        carry_ref[...] = dcv[0:8, :]
        dv = w2 * dcv + w1 * d1 + w0 * d2
        v = v_ref[...].astype(F32)
        dcw_ref[0:1, :] += jnp.sum(d2 * v, axis=0, keepdims=True)
        dcw_ref[1:2, :] += jnp.sum(d1 * v, axis=0, keepdims=True)
        dcw_ref[2:3, :] += jnp.sum(dcv * v, axis=0, keepdims=True)

        r0, xh = _rms(x_ref[...])
        ga = ga_ref[...]
        h = (xh * ga).astype(BF16)
        dh = jnp.zeros((tm, D_MODEL), F32)
        for idx, dpart in enumerate((db, dv * u_ref[...].astype(F32), dv * c_ref[...].astype(F32), dg)):
            dpb = dpart.astype(BF16)
            dh = dh + _dot_nt(dpb, wi_ref[idx])
            dwi_ref[idx] += _dot_tn(h, dpb)
        dga_ref[...] += jnp.sum(dh * xh, axis=0, keepdims=True)
        dx_ref[...] = dx1 + _rms_bwd(dh * ga, xh, r0)

    acc_shapes = [(4, D_MODEL, E_A), (E_A, D_MODEL), (8, E_A), (1, D_MODEL)]
    return pl.pallas_call(
        body, grid=(n,), name="conv_bwd",
        in_specs=[rev(D_MODEL), rev(D_MODEL)] + [rev(E_A)] * 7
        + [_const_spec((E_A, D_MODEL)), _const_spec((4, D_MODEL, E_A)), _const_spec((1, D_MODEL)),
           _const_spec((8, E_A))],
        out_specs=[rev(D_MODEL)] + [_acc_spec(s) for s in acc_shapes],
        out_shape=[jax.ShapeDtypeStruct((t, D_MODEL), F32)] + [jax.ShapeDtypeStruct(s, F32) for s in acc_shapes],
        scratch_shapes=[pltpu.VMEM((8, E_A), F32)],
        compiler_params=_params("arbitrary"),
    )(dx1, x, c, u, v, ym, kb, kcv, kg, w_out, w_in4, ga, cw)


WEIGHTS = ("a_norm", "a_w_in", "a_conv", "a_w_out", "kv_norm", "w_dkv", "ckv_norm", "w_ukv", "b_norm", "b_w_in",
           "b_q_norm", "b_w_uq", "b_w_out", "final_norm")
SHARD_SHAPES = {
    "a_norm": (1, 256), "a_w_in": (1, 1024, 1024), "a_conv": (1, 3, 256), "a_w_out": (1, 256, 1024),
    "kv_norm": (1024,), "w_dkv": (256, 288), "ckv_norm": (256,), "w_ukv": (256, 256), "b_norm": (1, 1024),
    "b_w_in": (1, 256, 896), "b_q_norm": (1, 384), "b_w_uq": (1, 384, 192), "b_w_out": (1, 512, 256),
    "final_norm": (1024,),
}
MATS = ("a_w_in", "a_w_out", "w_dkv", "w_ukv", "b_w_in", "b_w_uq", "b_w_out")
SMALL = ("a_norm", "a_conv", "kv_norm", "ckv_norm", "b_norm", "b_q_norm", "final_norm")
SMALL_FULL = {"a_norm": 1024, "a_conv": 3072, "kv_norm": 1024, "ckv_norm": 256, "b_norm": 1024, "b_q_norm": 384,
              "final_norm": 1024}
SMALL_ROWS = 8
LOSS_SLOT = sum(SMALL_FULL.values())


def _mat2d(name, a):
    return a.reshape(SHARD_SHAPES[name][-2:])


def _prep_first(g_win, gsmall):
    sm = gsmall.reshape(N_CHIPS, -1)
    a_conv = jnp.transpose(sm[:, 256:1024].reshape(N_CHIPS, CONV_WIDTH, 256), (1, 0, 2)).reshape(CONV_WIDTH, -1)
    return {"w_in4": g_win, "ga": sm[:, :256].reshape(1, -1), "cw": jnp.pad(a_conv, ((0, 8 - CONV_WIDTH), (0, 0)))}


def _prep_rest(gath, w):
    def cols(a):
        return jnp.transpose(a, (1, 0, 2)).reshape(a.shape[1], -1)

    def pad_heads(a, width):
        a = a.reshape(a.shape[0], N_HEADS, width)
        return jnp.pad(a, ((0, 0), (0, 0), (0, HEAD_PAD - width))).reshape(a.shape[0], QK_PAD)

    row = lambda a: a.reshape(1, -1).astype(F32)
    w_dkv = gath["w_dkv"].reshape(D_MODEL, KV_RANK + QK_ROPE)
    w_ukv = cols(gath["w_ukv"]).reshape(KV_RANK, N_HEADS, 2, QK_NOPE)
    return {
        "w_out": gath["a_w_out"].reshape(E_A, D_MODEL),
        "w_dkv": jnp.concatenate([w_dkv[:, :KV_RANK], jnp.zeros((D_MODEL, ROPE_LO), BF16), w_dkv[:, KV_RANK:],
                                  jnp.zeros((D_MODEL, HEAD_PAD - ROPE_LO - QK_ROPE), BF16)], axis=1),
        "w_uk": pad_heads(w_ukv[:, :, 0, :].reshape(KV_RANK, N_HEADS * QK_NOPE), QK_NOPE),
        "w_uv": w_ukv[:, :, 1, :].reshape(KV_RANK, E_B),
        "w_bin": gath["b_w_in"].reshape(D_MODEL, Q_RANK + E_B),
        "w_uq": pad_heads(cols(gath["b_w_uq"]), QK_NOPE + QK_ROPE),
        "w_bout": cols(gath["b_w_out"]),
        "gk": row(w["kv_norm"]), "gc": row(w["ckv_norm"]), "gb": row(w["b_norm"]),
        "gq": row(w["b_q_norm"]), "gf": row(w["final_norm"]),
    }


TRANSPOSED = ("w_dkv", "b_w_uq")
GRAD_MATS = ("w_in4", "w_out", "w_dkv", "w_uk", "w_uv", "w_bin", "w_uq", "w_bout")
GRAD_KIND = {"w_in4": "lead", "w_out": "row", "w_dkv": "row", "w_uk": "col", "w_uv": "col", "w_bin": "row",
             "w_uq": "col", "w_bout": "col"}


def _local_step(x, positions, tgt, w):
    bsz, seq, _ = x.shape
    t = bsz * seq
    x2d = x.reshape(t, D_MODEL)
    small = jnp.concatenate([w["a_norm"].reshape(-1), w["a_conv"].reshape(-1)]).reshape(8, LANES)
    rc, rs1, rs2, g_win, gsmall = _rope_tables_gather(positions.reshape(t, 1), _mat2d(MATS[0], w[MATS[0]]), small)
    wk = _prep_first(g_win, gsmall)
    (c, u, v, ym, kb, kcv, kg), gathered = _conv_fwd(x2d, seq, wk["ga"], wk["w_in4"], wk["cw"],
                                           [_mat2d(n, w[n]) for n in MATS[1:]])
    wk.update(_prep_rest(dict(zip(MATS[1:], gathered)), w))
    x1, q, k, vv, gate, cq, ckv = _mid_fwd(x2d, ym, wk["w_out"], wk["gk"], wk["gb"], wk["w_dkv"], wk["gc"], wk["w_uk"],
                                          wk["w_uv"], wk["w_bin"], wk["gq"], wk["w_uq"], rc, rs1, rs2)
    o, lse = _attn_fwd(q, k, vv, seq)
    dx2, do, dgate, dd, loss, dgf, dw_bout = _head_fwd_bwd(o, gate, x1, tgt.reshape(t, D_MODEL), wk["w_bout"], wk["gf"])
    dq, dk, dv = _attn_bwd(q, k, vv, do, lse, dd, seq)
    dx1, dwq, dw_bin, dwk, dwv, dwd, dgq, dgc, dgk, dgb = _mid_bwd(
        dq, dk, dv, dgate, dx2, x1, cq, ckv, rc, rs1, rs2, wk["w_uq"], wk["w_bin"], wk["w_uk"], wk["w_uv"], wk["w_dkv"],
        wk["gq"], wk["gc"], wk["gk"], wk["gb"])
    dx, dw_in4, dw_out, dcw, dga = _conv_bwd(dx1, x2d, c, u, v, ym, kb, kcv, kg, seq, wk["w_out"], wk["w_in4"], wk["ga"], wk["cw"])
    mats = {"w_in4": dw_in4, "w_out": dw_out, "w_dkv": dwd, "w_uk": dwk, "w_uv": dwv, "w_bin": dw_bin, "w_uq": dwq,
            "w_bout": dw_bout}
    small = {"a_norm": dga, "a_conv": dcw[:CONV_WIDTH], "kv_norm": dgk, "ckv_norm": dgc, "b_norm": dgb,
             "b_q_norm": dgq, "final_norm": dgf}
    return loss[0, 0], dx.reshape(bsz, seq, D_MODEL), mats, small


def _shard_grads(sh, svec):
    j0 = 2 * lax.axis_index("x") + lax.axis_index("y")
    flat = svec.reshape(-1)
    off, small = 0, {}
    for n in SMALL:
        small[n] = flat[off:off + SMALL_FULL[n]]
        off += SMALL_FULL[n]
    dwd, dwk, dwv, dwq = sh["w_dkv"], sh["w_uk"], sh["w_uv"], sh["w_uq"]
    w_ukv = jnp.stack([dwk.reshape(KV_RANK, 2, HEAD_PAD)[:, :, :QK_NOPE], dwv.reshape(KV_RANK, 2, V_HEAD)], axis=2)
    return {
        "a_norm": lax.dynamic_slice(small["a_norm"], (j0 * 256,), (256,)),
        "a_conv": lax.dynamic_slice(small["a_conv"].reshape(CONV_WIDTH, E_A), (0, j0 * 256), (CONV_WIDTH, 256)),
        "kv_norm": small["kv_norm"], "ckv_norm": small["ckv_norm"], "b_norm": small["b_norm"],
        "b_q_norm": small["b_q_norm"], "final_norm": small["final_norm"],
        "a_w_in": sh["w_in4"], "a_w_out": sh["w_out"],
        "w_dkv": jnp.concatenate([dwd[:, :KV_RANK], dwd[:, KV_RANK + ROPE_LO:KV_RANK + ROPE_LO + QK_ROPE]], axis=1),
        "w_ukv": w_ukv.reshape(KV_RANK, 2 * (QK_NOPE + V_HEAD)),
        "b_w_in": sh["w_bin"],
        "b_w_uq": dwq.reshape(Q_RANK, 2, HEAD_PAD)[:, :, :QK_NOPE + QK_ROPE].reshape(Q_RANK, -1),
        "b_w_out": sh["w_bout"],
    }


def _sub(ref, kind, j, cc):
    if kind == "lead":
        h = ref.shape[1] // 2
        return ref.at[j, pl.ds(pl.multiple_of(cc * h, 8), h), :]
    if kind == "row":
        rows = ref.shape[0] // N_CHIPS
        h = rows // 2
        return ref.at[pl.ds(pl.multiple_of(j * rows + cc * h, 8), h), :]
    cols = ref.shape[1] // N_CHIPS
    h = ref.shape[0] // 2
    return ref.at[pl.ds(pl.multiple_of(cc * h, 8), h), pl.ds(j * cols, cols)]


def _sub_shape(shape, kind):
    if kind == "lead":
        return (shape[1] // 2, shape[2])
    if kind == "row":
        return (shape[0] // N_CHIPS // 2, shape[1])
    return (shape[0] // 2, shape[1] // N_CHIPS)


def _reduce_grads(grads, kinds, vec):
    n = len(grads)
    shapes = [_sub_shape(a.shape, kd) for a, kd in zip(grads, kinds)]
    units = [(k, j) for k in range(n) for j in range(N_CHIPS)]
    big = (max(s[0] for s in shapes), max(s[1] for s in shapes))

    def body(*refs):
        g, v_ref = refs[:n], refs[n]
        out, o_ref = refs[n + 1:2 * n + 1], refs[2 * n + 1]
        theirs, part, recd, red = (refs[(2 + i) * n + 2:(3 + i) * n + 2] for i in range(4))
        mine, got = refs[6 * n + 2], refs[6 * n + 3]
        send1, recv1, send3, recv3, send5, recv5, load, local, send_v, recv_v = refs[6 * n + 4:]
        x, y, c = _place()
        j0 = 2 * x + y
        me = 2 * j0 + c
        sibling = (x, y, 1 - c)

        got[me] = v_ref[...]
        first, small_in = [], []
        for d in range(1, N_DEV):
            px, py, pc = x ^ (d >> 2), y ^ ((d >> 1) & 1), c ^ (d & 1)
            first.append(_remote(v_ref, got.at[me], send_v.at[d - 1], recv_v.at[d - 1], (px, py, pc)))
            small_in.append(_remote(v_ref, got.at[4 * px + 2 * py + pc], send_v.at[d - 1], recv_v.at[d - 1],
                                    (px, py, pc)))
        halves = [_remote(_sub(g[k], kinds[k], j, 1 - c), theirs[k].at[j], send1.at[u], recv1.at[u], sibling)
                  for u, (k, j) in enumerate(units)]
        for cp in first + halves:
            cp.start()

        def mine_load(u):
            k, j = units[u]
            h, cols = shapes[k]
            return pltpu.make_async_copy(_sub(g[k], kinds[k], j, c), mine.at[u % 2, pl.ds(0, h), pl.ds(0, cols)],
                                         load.at[u % 2])

        mine_load(0).start()
        for u, (k, j) in enumerate(units):
            h, cols = shapes[k]
            if u + 1 < len(units):
                mine_load(u + 1).start()
            mine_load(u).wait()
            halves[u].wait_recv()
            part[k][j] = (mine[u % 2, 0:h, 0:cols] + theirs[k][j]).astype(BF16)
            to_owner = _remote(part[k].at[j], recd[k].at[j0], send3.at[u], recv3.at[4 * k + j0], (j // 2, j % 2, c))

            @pl.when(j != j0)
            def _():
                to_owner.start()

            @pl.when(j == j0)
            def _():
                recd[k][j] = part[k][j]

        swaps = []
        for k in range(n):
            for j in range(N_CHIPS):
                arrived = _remote(part[k].at[j], recd[k].at[j], send3.at[4 * k + j], recv3.at[4 * k + j],
                                  (j // 2, j % 2, c))

                @pl.when(j != j0)
                def _():
                    arrived.wait_recv()

            r = recd[k]
            red[k][...] = ((r[0].astype(F32) + r[1].astype(F32)) + r[2].astype(F32)) + r[3].astype(F32)
            own = pltpu.make_async_copy(red[k], out[k].at[c], local.at[k])
            give = _remote(red[k], out[k].at[c], send5.at[k], recv5.at[k], sibling)
            take = _remote(red[k], out[k].at[1 - c], send5.at[k], recv5.at[k], sibling)
            own.start()
            give.start()
            swaps.append((own, give, take))

        for cp in small_in:
            cp.wait_recv()
            cp.wait_send()
        acc = got[0]
        for d in range(1, N_DEV):
            acc = acc + got[d]
        o_ref[...] = acc
        for u, (k, j) in enumerate(units):
            halves[u].wait_send()
            sent = _remote(part[k].at[j], recd[k].at[j0], send3.at[u], recv3.at[u], (j // 2, j % 2, c))

            @pl.when(j != j0)
            def _():
                sent.wait_send()
        for own, give, take in swaps:
            take.wait_recv()
            give.wait_send()
            own.wait()

    sems = [pltpu.SemaphoreType.DMA((len(units),))] * 4 + [pltpu.SemaphoreType.DMA((n,))] * 2 \
        + [pltpu.SemaphoreType.DMA((2,)), pltpu.SemaphoreType.DMA((n,))] + [pltpu.SemaphoreType.DMA((N_DEV - 1,))] * 2
    res = pl.pallas_call(
        body, name="reduce_grads",
        in_specs=[HBM] * n + [WHOLE], out_specs=[HBM] * n + [WHOLE],
        out_shape=[jax.ShapeDtypeStruct((2,) + s, F32) for s in shapes] + [jax.ShapeDtypeStruct(vec.shape, vec.dtype)],
        scratch_shapes=[pltpu.VMEM((N_CHIPS,) + s, F32) for s in shapes]
        + [pltpu.VMEM((N_CHIPS,) + s, BF16) for s in shapes] * 2
        + [pltpu.VMEM(s, F32) for s in shapes]
        + [pltpu.VMEM((2,) + big, F32), pltpu.VMEM((N_DEV,) + vec.shape, vec.dtype)] + sems,
        compiler_params=_comm_params(),
    )(*grads, vec)
    return res[:n], res[n]


def _adamw_math(w, g, m, v):
    m = ADAM_B1 * m + (1.0 - ADAM_B1) * g
    v = ADAM_B2 * v + (1.0 - ADAM_B2) * (g * g)
    m_hat = m / (1.0 - ADAM_B1 ** ADAM_STEP)
    v_hat = v / (1.0 - ADAM_B2 ** ADAM_STEP)
    return -ADAM_LR * (m_hat / (jnp.sqrt(v_hat) + ADAM_EPS) + ADAM_WD * w), m, v


def _adamw_tiled(w, g, m, v):
    rows, width = w.shape
    tm = rows // 4

    def body(w_ref, g_ref, m_ref, v_ref, go_ref, d_ref, mo_ref, vo_ref):
        g = g_ref[...]
        go_ref[...] = g
        d_ref[...], mo_ref[...], vo_ref[...] = _adamw_math(w_ref[...], g, m_ref[...], v_ref[...])

    spec = pl.BlockSpec((tm, width), lambda i: (i, 0))
    out = jax.ShapeDtypeStruct((rows, width), F32)
    return pl.pallas_call(
        body, grid=(rows // tm,), name="adamw_tiled",
        in_specs=[spec] * 4, out_specs=[spec] * 4, out_shape=[out] * 4,
        compiler_params=_params("parallel"),
    )(w, g, m, v)


def _adamw_many(ws, gs, ms, vs):
    n = len(ws)

    def body(*refs):
        for k in range(n):
            w_ref, g_ref, m_ref, v_ref = (refs[i * n + k] for i in range(4))
            go_ref, d_ref, mo_ref, vo_ref = (refs[(4 + i) * n + k] for i in range(4))
            g = g_ref[...]
            go_ref[...] = g
            d_ref[...], mo_ref[...], vo_ref[...] = _adamw_math(w_ref[...], g, m_ref[...], v_ref[...])

    outs = [jax.ShapeDtypeStruct(a.shape, F32) for a in ws]
    res = pl.pallas_call(
        body, name="adamw_many",
        in_specs=[WHOLE] * (4 * n), out_specs=[WHOLE] * (4 * n), out_shape=outs * 4,
        compiler_params=_comm_params(),
    )(*ws, *gs, *ms, *vs)
    return res[:n], res[n:2 * n], res[2 * n:3 * n], res[3 * n:]


def kernel(x, positions, a_norm, a_w_in, a_conv, a_w_out, kv_norm, w_dkv, ckv_norm, w_ukv, b_norm, b_w_in, b_q_norm, b_w_uq, b_w_out, final_norm, loss_target, m_a_norm, m_a_w_in, m_a_conv, m_a_w_out, m_kv_norm, m_w_dkv, m_ckv_norm, m_w_ukv, m_b_norm, m_b_w_in, m_b_q_norm, m_b_w_uq, m_b_w_out, m_final_norm, v_a_norm, v_a_w_in, v_a_conv, v_a_w_out, v_kv_norm, v_w_dkv, v_ckv_norm, v_w_ukv, v_b_norm, v_b_w_in, v_b_q_norm, v_b_w_uq, v_b_w_out, v_final_norm):
    w = dict(a_norm=a_norm, a_w_in=a_w_in, a_conv=a_conv, a_w_out=a_w_out, kv_norm=kv_norm, w_dkv=w_dkv,
             ckv_norm=ckv_norm, w_ukv=w_ukv, b_norm=b_norm, b_w_in=b_w_in, b_q_norm=b_q_norm, b_w_uq=b_w_uq,
             b_w_out=b_w_out, final_norm=final_norm)
    m = dict(a_norm=m_a_norm, a_w_in=m_a_w_in, a_conv=m_a_conv, a_w_out=m_a_w_out, kv_norm=m_kv_norm, w_dkv=m_w_dkv,
             ckv_norm=m_ckv_norm, w_ukv=m_w_ukv, b_norm=m_b_norm, b_w_in=m_b_w_in, b_q_norm=m_b_q_norm,
             b_w_uq=m_b_w_uq, b_w_out=m_b_w_out, final_norm=m_final_norm)
    v = dict(a_norm=v_a_norm, a_w_in=v_a_w_in, a_conv=v_a_conv, a_w_out=v_a_w_out, kv_norm=v_kv_norm, w_dkv=v_w_dkv,
             ckv_norm=v_ckv_norm, w_ukv=v_w_ukv, b_norm=v_b_norm, b_w_in=v_b_w_in, b_q_norm=v_b_q_norm,
             b_w_uq=v_b_w_uq, b_w_out=v_b_w_out, final_norm=v_final_norm)

    loss, dx, gmat, gsmall = _local_step(x, positions, loss_target, w)

    kinds = [GRAD_KIND[n] for n in GRAD_MATS]
    grads = [gmat[n] for n in GRAD_MATS]
    flat = jnp.concatenate([gsmall[n].reshape(-1) for n in SMALL] + [loss.reshape(1)])
    flat = jnp.pad(flat, (0, SMALL_ROWS * PACK_W - flat.shape[0])).reshape(SMALL_ROWS, PACK_W)
    mine, svec = _reduce_grads(grads, kinds, flat)
    loss = svec.reshape(-1)[LOSS_SLOT]
    g = _shard_grads({n: a.reshape(-1, a.shape[-1]) for n, a in zip(GRAD_MATS, mine)}, svec)

    def two_d(n, a):
        a = a.reshape(-1, a.shape[-1])
        return a.T if n in TRANSPOSED else a

    def back(n, a):
        return (a.T if n in TRANSPOSED else a).reshape(SHARD_SHAPES[n])

    big = "a_w_in"
    rest = [n for n in WEIGHTS if n != big]
    res_big = _adamw_tiled(*(two_d(big, t[big]) for t in (w, g, m, v)))
    res_rest = _adamw_many(*([two_d(n, t[n]) for n in rest] for t in (w, g, m, v)))
    out = {kind: dict(zip(rest, res_rest[i])) for i, kind in enumerate("gdmv")}
    for i, kind in enumerate("gdmv"):
        out[kind][big] = res_big[i]
    return (loss, dx) + tuple(back(n, out[kind][n]) for kind in "gdmv" for n in WEIGHTS)
```

```python
import functools
import math

import numpy as np
import jax
import jax.numpy as jnp
from jax import lax
from jax.experimental import pallas as pl
from jax.experimental.pallas import tpu as pltpu

F32 = jnp.float32
BF16 = jnp.bfloat16

D_MODEL = 1024
E_A = 1024
CONV_WIDTH = 3
N_HEADS = 8
QK_NOPE = 64
QK_ROPE = 32
V_HEAD = 64
KV_RANK = 256
Q_RANK = 384
E_B = N_HEADS * V_HEAD
ROPE_THETA = 10000.0
SOFTMAX_SCALE = 1.0 / math.sqrt(QK_NOPE + QK_ROPE)
LOG2E = math.log2(math.e)
LN2 = math.log(2.0)
Q_PRESCALE = SOFTMAX_SCALE * LOG2E
EPS = 1e-6
HEAD_PAD = 128
QK_PAD = N_HEADS * HEAD_PAD
ROPE_LO = QK_NOPE
ROPE_HALF = QK_ROPE // 2
KR_PAD = KV_RANK + HEAD_PAD

ADAM_LR = 0.001
ADAM_B1 = 0.9
ADAM_B2 = 0.999
ADAM_EPS = 1e-08
ADAM_WD = 0.01
ADAM_STEP = 10

VMEM_LIMIT = 56 * 1024 * 1024
ROW_TILE = 512
CONV_BWD_TILE = 256
ATT_TILE_FWD = 1024
ATT_TILE = 512
HEAD_CHAINS = 2
LANES = 128
PACK_W = 1024

N_CHIPS = 4
N_DEV = 8


def _dot(a, b):
    return jnp.dot(a, b, preferred_element_type=F32)


def _dot_nt(a, b):
    return lax.dot_general(a, b, (((1,), (1,)), ((), ())), preferred_element_type=F32)


def _dot_tn(a, b):
    return lax.dot_general(a, b, (((0,), (0,)), ((), ())), preferred_element_type=F32)


def _rms(x):
    r = lax.rsqrt(jnp.mean(x * x, axis=-1, keepdims=True) + EPS)
    return r, x * r


def _rms_bwd(dxh, xh, r):
    return r * (dxh - xh * jnp.mean(dxh * xh, axis=-1, keepdims=True))


def _rope_fwd(a, c, s1, s2):
    return a * c + pltpu.roll(a, HEAD_PAD - ROPE_HALF, 1) * s1 + pltpu.roll(a, ROPE_HALF, 1) * s2


def _rope_bwd(g, c, s1, s2):
    return g * c + pltpu.roll(g * s1, ROPE_HALF, 1) + pltpu.roll(g * s2, HEAD_PAD - ROPE_HALF, 1)


def _sigmoid(x):
    return 1.0 / (1.0 + jnp.exp(-x))


def _row_spec(tm, n):
    return pl.BlockSpec((tm, n), lambda i: (i, 0))


def _const_spec(shape):
    nd = len(shape)
    return pl.BlockSpec(shape, lambda i: (0,) * nd, pipeline_mode=pl.Buffered(1))


def _acc_spec(shape):
    nd = len(shape)
    return pl.BlockSpec(shape, lambda i: (0,) * nd, pipeline_mode=pl.Buffered(1))


def _params(*sem):
    return pltpu.CompilerParams(dimension_semantics=sem, vmem_limit_bytes=VMEM_LIMIT)


MESH = pl.DeviceIdType.MESH
HBM = pl.BlockSpec(memory_space=pl.ANY)
WHOLE = pl.BlockSpec(memory_space=pltpu.VMEM)
FLIPS = ((1, 0), (0, 1), (1, 1))


def _place():
    return lax.axis_index("x"), lax.axis_index("y"), lax.axis_index("c")


def _remote(src, dst, send, recv, peer):
    return pltpu.make_async_remote_copy(src_ref=src, dst_ref=dst, send_sem=send, recv_sem=recv, device_id=peer,
                                        device_id_type=MESH)


def _comm_params():
    return pltpu.CompilerParams(vmem_limit_bytes=VMEM_LIMIT)


def _rope_consts():
    lane = np.arange(HEAD_PAD)
    first = (lane >= ROPE_LO) & (lane < ROPE_LO + ROPE_HALF)
    second = (lane >= ROPE_LO + ROPE_HALF) & (lane < ROPE_LO + QK_ROPE)
    f = np.where(first, lane - ROPE_LO, np.where(second, lane - ROPE_LO - ROPE_HALF, 0))
    inv = np.float32(ROPE_THETA) ** (-(2 * f).astype(np.float32) / np.float32(QK_ROPE))
    out = np.zeros((8, HEAD_PAD), np.float32)
    out[0] = inv
    out[1] = first
    out[2] = second
    out[3] = lane < ROPE_LO
    return jnp.asarray(out)


def _gather_plan(w, sm, outs, osm, bf, sems):
    n = len(w)
    send_i, recv_i, send_d, recv_d, send_s, recv_s, local = sems
    x, y, c = _place()
    j0 = 2 * x + y
    sibling = (x, y, 1 - c)
    own = [pltpu.make_async_copy(bf[k], outs[k].at[j0], local.at[k]) for k in range(n)]
    if sm is not None:
        own.append(pltpu.make_async_copy(sm, osm.at[j0], local.at[n]))

    def half(k, cc):
        h = w[k].shape[0] // 2
        return pl.ds(pl.multiple_of(cc * h, 16), h)

    sends, arrivals, forwards, fwaits = [], [], [], []
    for i, (fx, fy) in enumerate(FLIPS):
        px, py = x ^ fx, y ^ fy
        pj = 2 * px + py
        for k in range(n):
            s = i * n + k
            sends.append(_remote(bf[k].at[half(k, c)], outs[k].at[j0, half(k, c)], send_i.at[s], recv_i.at[s],
                                 (px, py, c)))
            arrivals.append(_remote(bf[k].at[half(k, c)], outs[k].at[pj, half(k, c)], send_i.at[s], recv_i.at[s],
                                    (px, py, c)))
            forwards.append(_remote(outs[k].at[pj, half(k, c)], outs[k].at[pj, half(k, c)], send_d.at[s],
                                    recv_d.at[s], sibling))
            fwaits.append(_remote(outs[k].at[pj, half(k, 1 - c)], outs[k].at[pj, half(k, 1 - c)], send_d.at[s],
                                  recv_d.at[s], sibling))
        if sm is not None:
            sends.append(_remote(sm, osm.at[j0], send_s.at[i], recv_s.at[i], (px, py, c)))
            fwaits.append(_remote(sm, osm.at[pj], send_s.at[i], recv_s.at[i], (px, py, c)))
    return own, sends, arrivals, forwards, fwaits


def _gather_begin(w, bf, plan):
    own, sends, _, _, _ = plan
    for k in range(len(w)):
        bf[k][...] = w[k][...].astype(BF16)
    for cp in own + sends:
        cp.start()


def _gather_pass_on(plan):
    _, _, arrivals, forwards, _ = plan
    for cp, fwd in zip(arrivals, forwards):
        cp.wait_recv()
        fwd.start()


def _gather_end(plan):
    own, sends, _, forwards, fwaits = plan
    for cp in fwaits:
        cp.wait_recv()
    for cp in sends + forwards:
        cp.wait_send()
    for cp in own:
        cp.wait()


def _gather_sems(n, with_small):
    return ([pltpu.SemaphoreType.DMA((3 * n,))] * 4 + [pltpu.SemaphoreType.DMA((3,))] * 2
            + [pltpu.SemaphoreType.DMA((n + (1 if with_small else 0),))])


def _rope_tables_gather(pos_col, w_in, small):
    t = pos_col.shape[0]
    tm = min(ROW_TILE, t)
    steps = t // tm

    def body(p_ref, k_ref, w_ref, sm_ref, c_ref, s1_ref, s2_ref, out_ref, osm_ref, bf_ref, *sems):
        plan = _gather_plan([w_ref], sm_ref, [out_ref], osm_ref, [bf_ref], sems)
        i = pl.program_id(0)

        @pl.when(i == 0)
        def _():
            _gather_begin([w_ref], [bf_ref], plan)

        inv, first, second, nope = k_ref[0:1, :], k_ref[1:2, :], k_ref[2:3, :], k_ref[3:4, :]
        ang = p_ref[...].astype(F32) * inv
        cs, sn = jnp.cos(ang), jnp.sin(ang)
        c_ref[...] = cs * (first + second) + nope
        s1_ref[...] = -sn * first
        s2_ref[...] = sn * second

        @pl.when(i == steps - 1)
        def _():
            _gather_pass_on(plan)
            _gather_end(plan)

    out = jax.ShapeDtypeStruct((t, HEAD_PAD), F32)
    return pl.pallas_call(
        body, grid=(steps,), name="rope_tables_gather",
        in_specs=[_row_spec(tm, 1), _const_spec((8, HEAD_PAD)), WHOLE, WHOLE],
        out_specs=[_row_spec(tm, HEAD_PAD)] * 3 + [HBM, HBM],
        out_shape=[out] * 3 + [jax.ShapeDtypeStruct((N_CHIPS,) + w_in.shape, BF16),
                               jax.ShapeDtypeStruct((N_CHIPS,) + small.shape, small.dtype)],
        scratch_shapes=[pltpu.VMEM(w_in.shape, BF16)] + _gather_sems(1, True),
        compiler_params=_params("arbitrary"),
    )(pos_col, _rope_consts(), w_in, small)


def _shift_down(v, prev, row):
    p1, p2 = prev[7:8, :], prev[6:7, :]
    v1 = jnp.where(row == 0, p1, pltpu.roll(v, 1, 0))
    v2 = jnp.where(row == 0, p2, jnp.where(row == 1, p1, pltpu.roll(v, 2, 0)))
    return v1, v2


def _conv_fwd(x, seq, ga, w_in4, cw, shards):
    t = x.shape[0]
    tm = min(ROW_TILE, seq)
    tiles_per_seq = seq // tm
    steps = t // tm
    n = len(shards)

    def body(x_ref, ga_ref, w_ref, cw_ref, *rest):
        sh, (c_ref, u_ref, v_ref, ym_ref, kb_ref, kcv_ref, kg_ref) = rest[:n], rest[n:n + 7]
        outs, carry_ref, bf, sems = rest[n + 7:2 * n + 7], rest[2 * n + 7], rest[2 * n + 8:3 * n + 8], rest[3 * n + 8:]
        plan = _gather_plan(sh, None, outs, None, bf, sems)
        i = pl.program_id(0)

        @pl.when(i == 0)
        def _():
            _gather_begin(sh, bf, plan)

        @pl.when(i % tiles_per_seq == 0)
        def _():
            carry_ref[...] = jnp.zeros_like(carry_ref)

        _, xh = _rms(x_ref[...])
        h = (xh * ga_ref[...]).astype(BF16)
        c, u = _dot(h, w_ref[1]), _dot(h, w_ref[2])
        g, b = _dot(h, w_ref[3]), _dot(h, w_ref[0])
        v = c * u
        row = lax.broadcasted_iota(jnp.int32, (tm, 1), 0)
        v1, v2 = _shift_down(v, carry_ref[...], row)
        carry_ref[...] = v[tm - 8:tm, :]
        cv = cw_ref[2:3, :] * v + cw_ref[1:2, :] * v1 + cw_ref[0:1, :] * v2
        sg = _sigmoid(g)
        silu = g * sg
        c_ref[...] = c.astype(BF16)
        u_ref[...] = u.astype(BF16)
        v_ref[...] = v.astype(BF16)
        ym_ref[...] = (silu * b * cv).astype(BF16)
        kb_ref[...] = (silu * cv).astype(BF16)
        kcv_ref[...] = (silu * b).astype(BF16)
        kg_ref[...] = (b * cv * (sg * (1.0 + g * (1.0 - sg)))).astype(BF16)

        @pl.when(i == max(steps - 2, 0))
        def _():
            _gather_pass_on(plan)

        @pl.when(i == steps - 1)
        def _():
            _gather_end(plan)

    out = jax.ShapeDtypeStruct((t, E_A), BF16)
    res = pl.pallas_call(
        body, grid=(steps,), name="conv_fwd",
        in_specs=[_row_spec(tm, D_MODEL), _const_spec((1, D_MODEL)), _const_spec((4, D_MODEL, E_A)),
                  _const_spec((8, E_A))] + [WHOLE] * n,
        out_specs=[_row_spec(tm, E_A)] * 7 + [HBM] * n,
        out_shape=[out] * 7 + [jax.ShapeDtypeStruct((N_CHIPS,) + a.shape, BF16) for a in shards],
        scratch_shapes=[pltpu.VMEM((8, E_A), F32)] + [pltpu.VMEM(a.shape, BF16) for a in shards]
        + _gather_sems(n, False),
        compiler_params=_params("arbitrary"),
    )(x, ga, w_in4, cw, *shards)
    return res[:7], res[7:]


def _mid_fwd(x, ym, w_out, gk, gb, w_dkv, gc, w_uk, w_uv, w_bin, gq, w_uq, rc, rs1, rs2):
    t = x.shape[0]
    tm = min(2 * ROW_TILE, t)

    def body(x_ref, ym_ref, wo_ref, gk_ref, gb_ref, wd_ref, gc_ref, wk_ref, wv_ref, wi_ref, gq_ref, wq_ref,
             c_ref, s1_ref, s2_ref, x1_ref, q_ref, k_ref, v_ref, gate_ref, cq_ref, ckv_ref):
        cb, s1b, s2b = c_ref[...], s1_ref[...], s2_ref[...]
        x1 = x_ref[...] + _dot(ym_ref[...], wo_ref[...])
        x1_ref[...] = x1
        _, xh = _rms(x1)
        hk = (xh * gk_ref[...]).astype(BF16)
        h1 = (xh * gb_ref[...]).astype(BF16)

        pb = _dot(h1, wi_ref[...])
        cq = pb[:, :Q_RANK]
        cq_ref[...] = cq.astype(BF16)
        gate_ref[...] = pb[:, Q_RANK:].astype(BF16)
        _, cqh = _rms(cq)
        q = _dot((cqh * gq_ref[...]).astype(BF16), wq_ref[...])
        for h in range(N_HEADS):
            sl = slice(h * HEAD_PAD, (h + 1) * HEAD_PAD)
            q_ref[:, sl] = (_rope_fwd(q[:, sl], cb, s1b, s2b) * Q_PRESCALE).astype(BF16)

        ckr = _dot(hk, wd_ref[...])
        ckv_raw = ckr[:, :KV_RANK]
        ckv_ref[...] = ckv_raw.astype(BF16)
        _, ch = _rms(ckv_raw)
        ckv = (ch * gc_ref[...]).astype(BF16)
        kr = _rope_fwd(ckr[:, KV_RANK:], cb, s1b, s2b)
        kn = _dot(ckv, wk_ref[...])
        for h in range(N_HEADS):
            sl = slice(h * HEAD_PAD, (h + 1) * HEAD_PAD)
            k_ref[:, sl] = (kn[:, sl] + kr).astype(BF16)
        v_ref[...] = _dot(ckv, wv_ref[...]).astype(BF16)

    def sds(n, dt):
        return jax.ShapeDtypeStruct((t, n), dt)

    return pl.pallas_call(
        body, grid=(t // tm,), name="mid_fwd",
        in_specs=[_row_spec(tm, D_MODEL), _row_spec(tm, E_A), _const_spec((E_A, D_MODEL)),
                  _const_spec((1, D_MODEL)), _const_spec((1, D_MODEL)), _const_spec((D_MODEL, KR_PAD)),
                  _const_spec((1, KV_RANK)), _const_spec((KV_RANK, QK_PAD)), _const_spec((KV_RANK, E_B)),
                  _const_spec((D_MODEL, Q_RANK + E_B)), _const_spec((1, Q_RANK)), _const_spec((Q_RANK, QK_PAD)),
                  _row_spec(tm, HEAD_PAD), _row_spec(tm, HEAD_PAD), _row_spec(tm, HEAD_PAD)],
        out_specs=[_row_spec(tm, D_MODEL), _row_spec(tm, QK_PAD), _row_spec(tm, QK_PAD), _row_spec(tm, E_B),
                   _row_spec(tm, E_B), _row_spec(tm, Q_RANK), _row_spec(tm, KV_RANK)],
        out_shape=[sds(D_MODEL, F32), sds(QK_PAD, BF16), sds(QK_PAD, BF16), sds(E_B, BF16), sds(E_B, BF16),
                   sds(Q_RANK, BF16), sds(KV_RANK, BF16)],
        compiler_params=_params("parallel"),
    )(x, ym, w_out, gk, gb, w_dkv, gc, w_uk, w_uv, w_bin, gq, w_uq, rc, rs1, rs2)


def _pair_specs(seq):
    qk = pl.BlockSpec((seq, 2 * HEAD_PAD), lambda b, p: (b, p))
    vo = pl.BlockSpec((seq, 2 * V_HEAD), lambda b, p: (b, p))
    st = pl.BlockSpec((None, 2, seq), lambda b, p: (p, 0, b))
    return qk, vo, st


def _attn_fwd(q, k, v, seq):
    t = q.shape[0]
    tq = min(ATT_TILE_FWD, seq)
    nq = seq // tq

    def body(q_ref, k_ref, v_ref, o_ref, lse_ref, m_scr, l_scr, acc_scr):
        lane = lax.broadcasted_iota(jnp.int32, (tq, 2 * V_HEAD), 1)

        def q_step(qi, _):
            q0 = pl.multiple_of(qi * tq, tq)
            m_scr[...] = jnp.full(m_scr.shape, -jnp.inf, F32)
            l_scr[...] = jnp.zeros_like(l_scr)
            acc_scr[...] = jnp.zeros_like(acc_scr)

            def block(q_lo, q_n, k0, k_n, masked):
                rows = slice(q_lo, q_lo + q_n)
                vt = v_ref[pl.ds(k0, k_n), :]

                def score(hh):
                    hs = slice(hh * HEAD_PAD, (hh + 1) * HEAD_PAD)
                    return _dot_nt(q_ref[pl.ds(q0 + q_lo, q_n), hs], k_ref[pl.ds(k0, k_n), hs])

                early = [score(hh) for hh in range(2)] if masked else None
                for hh in range(2):
                    s = early[hh] if masked else score(hh)
                    if masked:
                        row = lax.broadcasted_iota(jnp.int32, (q_n, k_n), 0)
                        col = lax.broadcasted_iota(jnp.int32, (q_n, k_n), 1)
                        s = jnp.where(col <= row, s, -jnp.inf)
                    m_old = m_scr[hh, rows]
                    m_new = jnp.maximum(m_old, jnp.max(s, axis=-1, keepdims=True))
                    alpha = jnp.exp2(m_old - m_new)
                    ps = [jnp.exp2(s[:, j * LANES:(j + 1) * LANES] - m_new) for j in range(k_n // LANES)]
                    l_scr[hh, rows] = alpha * l_scr[hh, rows] + functools.reduce(lambda a, b: a + b, ps)
                    p = jnp.concatenate(ps, axis=-1).astype(BF16)
                    acc_scr[hh, rows] = alpha * acc_scr[hh, rows] + _dot(p, vt)
                    m_scr[hh, rows] = m_new

            def k_step(ki, _):
                block(0, tq, pl.multiple_of(ki * tq, tq), tq, False)
                return 0

            lax.fori_loop(0, qi, k_step, 0)
            half = tq // 2
            block(0, tq, q0, half, True)
            block(half, half, q0 + half, half, True)
            l0 = jnp.sum(l_scr[0], axis=-1, keepdims=True)
            l1 = jnp.sum(l_scr[1], axis=-1, keepdims=True)
            o_ref[pl.ds(q0, tq), :] = jnp.where(lane < V_HEAD, acc_scr[0] / l0, acc_scr[1] / l1).astype(BF16)
            stats = jnp.where(lane == 0, m_scr[0] + jnp.log2(l0), m_scr[1] + jnp.log2(l1)).T
            lse_ref[:, pl.ds(q0, tq)] = stats[0:2, :]
            return 0

        lax.fori_loop(0, nq, q_step, 0)

    qk, vo, st = _pair_specs(seq)
    return pl.pallas_call(
        body, grid=(t // seq, N_HEADS // 2), name="attn_fwd",
        in_specs=[qk, qk, vo], out_specs=[vo, st],
        out_shape=[jax.ShapeDtypeStruct((t, E_B), BF16), jax.ShapeDtypeStruct((N_HEADS // 2, 2, t), F32)],
        scratch_shapes=[pltpu.VMEM((2, tq, LANES), F32), pltpu.VMEM((2, tq, LANES), F32),
                        pltpu.VMEM((2, tq, 2 * V_HEAD), F32)],
        compiler_params=_params("parallel", "parallel"),
    )(q, k, v)


def _head_fwd_bwd(o, gate, x1, tgt, w_bout, gf):
    t = o.shape[0]
    tm = min(2 * ROW_TILE, t)

    def body(o_ref, gate_ref, x1_ref, tgt_ref, w_ref, gf_ref,
             dx2_ref, do_ref, dgate_ref, dd_ref, loss_ref, dgf_ref, dw_ref):
        @pl.when(pl.program_id(0) == 0)
        def _():
            loss_ref[...] = jnp.zeros_like(loss_ref)
            dgf_ref[...] = jnp.zeros_like(dgf_ref)
            dw_ref[...] = jnp.zeros_like(dw_ref)

        hm = tm // HEAD_CHAINS
        gf = gf_ref[...]
        lane = lax.broadcasted_iota(jnp.int32, (hm, 2 * V_HEAD), 1)
        chains = []
        for ch in range(HEAD_CHAINS):
            rs = pl.ds(ch * hm, hm)
            o = o_ref[rs, :].astype(F32)
            gt = gate_ref[rs, :].astype(F32)
            sg = _sigmoid(gt)
            silu = gt * sg
            z = (o * silu).astype(BF16)
            chains.append((rs, o, gt, sg, silu, z, x1_ref[rs, :] + _dot(z, w_ref[...])))
        mids = []
        for rs, o, gt, sg, silu, z, x2 in chains:
            r2, xh2 = _rms(x2)
            err = xh2 * gf - tgt_ref[rs, :]
            loss_ref[...] += 0.5 * jnp.sum(jnp.mean(err * err, axis=-1, keepdims=True))
            dy = err * (1.0 / D_MODEL)
            dgf_ref[...] += jnp.sum(dy * xh2, axis=0, keepdims=True)
            dx2 = _rms_bwd(dy * gf, xh2, r2)
            dx2_ref[rs, :] = dx2
            dx2b = dx2.astype(BF16)
            mids.append(_dot_nt(dx2b, w_ref[...]))
            dw_ref[...] += _dot_tn(z, dx2b)
        for (rs, o, gt, sg, silu, z, x2), dz in zip(chains, mids):
            do = dz * silu
            do_ref[rs, :] = do.astype(BF16)
            dgate_ref[rs, :] = (dz * o * (sg * (1.0 + gt * (1.0 - sg)))).astype(BF16)
            prod = do * o
            cols = jnp.zeros((hm, LANES), F32)
            for p in range(N_HEADS // 2):
                blk = prod[:, p * 2 * V_HEAD:(p + 1) * 2 * V_HEAD]
                d0 = jnp.sum(jnp.where(lane < V_HEAD, blk, 0.0), axis=-1, keepdims=True)
                d1 = jnp.sum(jnp.where(lane < V_HEAD, 0.0, blk), axis=-1, keepdims=True)
                cols = jnp.where(lane == 2 * p, d0, jnp.where(lane == 2 * p + 1, d1, cols))
            rows = cols.T
            for h in range(N_HEADS):
                dd_ref[h // 2, h % 2:h % 2 + 1, rs] = rows[h:h + 1, :]

    return pl.pallas_call(
        body, grid=(t // tm,), name="head_fwd_bwd",
        in_specs=[_row_spec(tm, E_B), _row_spec(tm, E_B), _row_spec(tm, D_MODEL), _row_spec(tm, D_MODEL),
                  _const_spec((E_B, D_MODEL)), _const_spec((1, D_MODEL))],
        out_specs=[_row_spec(tm, D_MODEL), _row_spec(tm, E_B), _row_spec(tm, E_B),
                   pl.BlockSpec((N_HEADS // 2, 2, tm), lambda i: (0, 0, i)),
                   _acc_spec((1, 1)), _acc_spec((1, D_MODEL)), _acc_spec((E_B, D_MODEL))],
        out_shape=[jax.ShapeDtypeStruct((t, D_MODEL), F32), jax.ShapeDtypeStruct((t, E_B), BF16),
                   jax.ShapeDtypeStruct((t, E_B), BF16), jax.ShapeDtypeStruct((N_HEADS // 2, 2, t), F32),
                   jax.ShapeDtypeStruct((1, 1), F32), jax.ShapeDtypeStruct((1, D_MODEL), F32),
                   jax.ShapeDtypeStruct((E_B, D_MODEL), F32)],
        compiler_params=_params("arbitrary"),
    )(o, gate, x1, tgt, w_bout, gf)


def _attn_bwd(q, k, v, do, lse, dd, seq):
    t = q.shape[0]
    tq = min(ATT_TILE, seq)
    nq = seq // tq
    assert nq % 2 == 0, "full tiles are taken in pairs"

    def body(q_ref, k_ref, v_ref, do_ref, lse_ref, dd_ref, dq_ref, dk_ref, dv_ref, dq_acc, dk_acc, dv_acc):
        dq_acc[...] = jnp.zeros_like(dq_acc)

        def k_step(ki, _):
            k0 = pl.multiple_of(ki * tq, tq)
            dk_acc[...] = jnp.zeros_like(dk_acc)
            dv_acc[...] = jnp.zeros_like(dv_acc)

            def block(k_lo, k_n, q0, q_n, masked):
                rows = slice(k_lo, k_lo + k_n)
                lane = lax.broadcasted_iota(jnp.int32, (q_n, 2 * V_HEAD), 1)
                vt = v_ref[pl.ds(k0 + k_lo, k_n), :]
                do_pair = do_ref[pl.ds(q0, q_n), :]

                def operands(hh):
                    hs = slice(hh * HEAD_PAD, (hh + 1) * HEAD_PAD)
                    kt = k_ref[pl.ds(k0 + k_lo, k_n), hs]
                    qt = q_ref[pl.ds(q0, q_n), hs]
                    mine = (lane < V_HEAD) if hh == 0 else (lane >= V_HEAD)
                    do_h = jnp.where(mine, do_pair, jnp.zeros((), BF16))
                    return hs, kt, qt, do_h, _dot_nt(kt, qt), _dot_nt(vt, do_h)

                early = [operands(hh) for hh in range(2)] if masked else None
                for hh in range(2):
                    hs, kt, qt, do_h, st, dpt = early[hh] if masked else operands(hh)
                    if masked:
                        krow = lax.broadcasted_iota(jnp.int32, (k_n, q_n), 0)
                        qcol = lax.broadcasted_iota(jnp.int32, (k_n, q_n), 1)
                        st = jnp.where(krow <= qcol, st, -jnp.inf)
                    pt = jnp.exp2(st - lse_ref[hh:hh + 1, pl.ds(q0, q_n)])
                    dst = (pt * (dpt - dd_ref[hh:hh + 1, pl.ds(q0, q_n)])).astype(BF16)
                    dv_acc[rows, :] += _dot(pt.astype(BF16), do_h)
                    dk_acc[rows, hs] += _dot(dst, qt)
                    dq_acc[pl.ds(q0, q_n), hs] += _dot_tn(dst, kt)

            wide = 2 * tq

            def q_step(qj, _):
                block(0, tq, pl.multiple_of(qj * wide, wide), wide, False)
                return 0

            half = tq // 2
            block(0, half, k0, tq, True)
            block(half, half, pl.multiple_of(k0 + half, half), half, True)

            @pl.when(ki % 2 == 0)
            def _():
                block(0, tq, pl.multiple_of(k0 + tq, tq), tq, False)

            lax.fori_loop(ki // 2 + 1, nq // 2, q_step, 0)
            dk_ref[pl.ds(k0, tq), :] = (dk_acc[...] * LN2).astype(BF16)
            dv_ref[pl.ds(k0, tq), :] = dv_acc[...].astype(BF16)
            return 0

        lax.fori_loop(0, nq, k_step, 0)
        dq_ref[...] = (dq_acc[...] * SOFTMAX_SCALE).astype(BF16)

    qk, vo, st = _pair_specs(seq)
    return pl.pallas_call(
        body, grid=(t // seq, N_HEADS // 2), name="attn_bwd",
        in_specs=[qk, qk, vo, vo, st, st], out_specs=[qk, qk, vo],
        out_shape=[jax.ShapeDtypeStruct((t, QK_PAD), BF16), jax.ShapeDtypeStruct((t, QK_PAD), BF16),
                   jax.ShapeDtypeStruct((t, E_B), BF16)],
        scratch_shapes=[pltpu.VMEM((seq, 2 * HEAD_PAD), F32), pltpu.VMEM((tq, 2 * HEAD_PAD), F32),
                        pltpu.VMEM((tq, 2 * V_HEAD), F32)],
        compiler_params=_params("parallel", "parallel"),
    )(q, k, v, do, lse, dd)


def _mid_bwd(dq, dk, dv, dgate, dx2, x1, cq, ckv, rc, rs1, rs2, w_uq, w_bin, w_uk, w_uv, w_dkv, gq, gc, gk, gb):
    t = dq.shape[0]
    tm = min(ROW_TILE, t)

    def body(dq_ref, dk_ref, dv_ref, dgate_ref, dx2_ref, x1_ref, cq_ref, ckv_ref, c_ref, s1_ref, s2_ref,
             wq_ref, wi_ref, wk_ref, wv_ref, wd_ref, gq_ref, gc_ref, gk_ref, gb_ref,
             dx1_ref, dwq_ref, dwi_ref, dwk_ref, dwv_ref, dwd_ref, dgq_ref, dgc_ref, dgk_ref, dgb_ref):
        @pl.when(pl.program_id(0) == 0)
        def _():
            for ref in (dwq_ref, dwi_ref, dwk_ref, dwv_ref, dwd_ref, dgq_ref, dgc_ref, dgk_ref, dgb_ref):
                ref[...] = jnp.zeros_like(ref)

        cb, s1b, s2b = c_ref[...], s1_ref[...], s2_ref[...]
        gk, gb, gq, gc = gk_ref[...], gb_ref[...], gq_ref[...], gc_ref[...]

        dkb, dvb = dk_ref[...], dv_ref[...]
        dckv = _dot_nt(dkb, wk_ref[...]) + _dot_nt(dvb, wv_ref[...])
        rcv, ch = _rms(ckv_ref[...].astype(F32))
        ckvn = (ch * gc).astype(BF16)
        dwk_ref[...] += _dot_tn(ckvn, dkb)
        dwv_ref[...] += _dot_tn(ckvn, dvb)

        dqs = [_rope_bwd(dq_ref[:, h * HEAD_PAD:(h + 1) * HEAD_PAD].astype(F32), cb, s1b, s2b)
               for h in range(N_HEADS)]
        dqb = jnp.concatenate(dqs, axis=-1).astype(BF16)
        rq, cqh = _rms(cq_ref[...].astype(F32))
        dcqn = _dot_nt(dqb, wq_ref[...])
        dwq_ref[...] += _dot_tn((cqh * gq).astype(BF16), dqb)

        dgc_ref[...] += jnp.sum(dckv * ch, axis=0, keepdims=True)
        dckv_raw = _rms_bwd(dckv * gc, ch, rcv)
        dkr = dk_ref[:, 0:HEAD_PAD].astype(F32)
        for h in range(1, N_HEADS):
            dkr = dkr + dk_ref[:, h * HEAD_PAD:(h + 1) * HEAD_PAD].astype(F32)
        dkr = _rope_bwd(dkr, cb, s1b, s2b)
        dckr = jnp.concatenate([dckv_raw, dkr], axis=-1).astype(BF16)
        r1, xh = _rms(x1_ref[...])
        dhk = _dot_nt(dckr, wd_ref[...])
        dwd_ref[...] += _dot_tn((xh * gk).astype(BF16), dckr)

        dgq_ref[...] += jnp.sum(dcqn * cqh, axis=0, keepdims=True)
        dcq = _rms_bwd(dcqn * gq, cqh, rq)
        dpb = jnp.concatenate([dcq.astype(BF16), dgate_ref[...]], axis=-1)
        dh1 = _dot_nt(dpb, wi_ref[...])
        dwi_ref[...] += _dot_tn((xh * gb).astype(BF16), dpb)

        dgb_ref[...] += jnp.sum(dh1 * xh, axis=0, keepdims=True)
        dgk_ref[...] += jnp.sum(dhk * xh, axis=0, keepdims=True)
        dx1_ref[...] = dx2_ref[...] + _rms_bwd(dh1 * gb + dhk * gk, xh, r1)

    acc_shapes = [(Q_RANK, QK_PAD), (D_MODEL, Q_RANK + E_B), (KV_RANK, QK_PAD), (KV_RANK, E_B), (D_MODEL, KR_PAD),
                  (1, Q_RANK), (1, KV_RANK), (1, D_MODEL), (1, D_MODEL)]
    return pl.pallas_call(
        body, grid=(t // tm,), name="mid_bwd",
        in_specs=[_row_spec(tm, QK_PAD), _row_spec(tm, QK_PAD), _row_spec(tm, E_B), _row_spec(tm, E_B),
                  _row_spec(tm, D_MODEL), _row_spec(tm, D_MODEL), _row_spec(tm, Q_RANK), _row_spec(tm, KV_RANK),
                  _row_spec(tm, HEAD_PAD), _row_spec(tm, HEAD_PAD), _row_spec(tm, HEAD_PAD),
                  _const_spec((Q_RANK, QK_PAD)), _const_spec((D_MODEL, Q_RANK + E_B)),
                  _const_spec((KV_RANK, QK_PAD)), _const_spec((KV_RANK, E_B)), _const_spec((D_MODEL, KR_PAD)),
                  _const_spec((1, Q_RANK)), _const_spec((1, KV_RANK)), _const_spec((1, D_MODEL)),
                  _const_spec((1, D_MODEL))],
        out_specs=[_row_spec(tm, D_MODEL)] + [_acc_spec(s) for s in acc_shapes],
        out_shape=[jax.ShapeDtypeStruct((t, D_MODEL), F32)] + [jax.ShapeDtypeStruct(s, F32) for s in acc_shapes],
        compiler_params=_params("arbitrary"),
    )(dq, dk, dv, dgate, dx2, x1, cq, ckv, rc, rs1, rs2, w_uq, w_bin, w_uk, w_uv, w_dkv, gq, gc, gk, gb)


def _conv_bwd(dx1, x, c, u, v, ym, kb, kcv, kg, seq, w_out, w_in4, ga, cw):
    t = x.shape[0]
    tm = min(CONV_BWD_TILE, seq)
    tiles_per_seq = seq // tm
    n = t // tm

    def tile(i):
        return n - 1 - i

    def rev(width):
        return pl.BlockSpec((tm, width), lambda i: (tile(i), 0))

    def body(dx1_ref, x_ref, c_ref, u_ref, v_ref, ym_ref, kb_ref, kcv_ref, kg_ref, wo_ref, wi_ref, ga_ref, cw_ref,
             dx_ref, dwi_ref, dwo_ref, dcw_ref, dga_ref, carry_ref):
        i = pl.program_id(0)
        j = tile(i)

        @pl.when(i == 0)
        def _():
            for ref in (dwi_ref, dwo_ref, dcw_ref, dga_ref):
                ref[...] = jnp.zeros_like(ref)

        @pl.when(j % tiles_per_seq == tiles_per_seq - 1)
        def _():
            carry_ref[...] = jnp.zeros_like(carry_ref)

        dx1 = dx1_ref[...]
        dx1b = dx1.astype(BF16)
        dym = _dot_nt(dx1b, wo_ref[...])
        dwo_ref[...] += _dot_tn(ym_ref[...], dx1b)
        db = dym * kb_ref[...].astype(F32)
        dcv = dym * kcv_ref[...].astype(F32)
        dg = dym * kg_ref[...].astype(F32)

        row = lax.broadcasted_iota(jnp.int32, (tm, 1), 0)
        w0, w1, w2 = cw_ref[0:1, :], cw_ref[1:2, :], cw_ref[2:3, :]
        nxt = carry_ref[...]
        n0, n1 = nxt[0:1, :], nxt[1:2, :]
        d1 = jnp.where(row == tm - 1, n0, pltpu.roll(dcv, tm - 1, 0))
        d2 = jnp.where(row == tm - 1, n1, jnp.where(row == tm - 2, n0, pltpu.roll(dcv, tm - 2, 0)))
        carry_ref[...] = dcv[0:8, :]
        dv = w2 * dcv + w1 * d1 + w0 * d2
        v = v_ref[...].astype(F32)
        dcw_ref[0:1, :] += jnp.sum(d2 * v, axis=0, keepdims=True)
        dcw_ref[1:2, :] += jnp.sum(d1 * v, axis=0, keepdims=True)
        dcw_ref[2:3, :] += jnp.sum(dcv * v, axis=0, keepdims=True)

        r0, xh = _rms(x_ref[...])
        ga = ga_ref[...]
        h = (xh * ga).astype(BF16)
        dh = jnp.zeros((tm, D_MODEL), F32)
        for idx, dpart in enumerate((db, dv * u_ref[...].astype(F32), dv * c_ref[...].astype(F32), dg)):
            dpb = dpart.astype(BF16)
            dh = dh + _dot_nt(dpb, wi_ref[idx])
            dwi_ref[idx] += _dot_tn(h, dpb)
        dga_ref[...] += jnp.sum(dh * xh, axis=0, keepdims=True)
        dx_ref[...] = dx1 + _rms_bwd(dh * ga, xh, r0)

    acc_shapes = [(4, D_MODEL, E_A), (E_A, D_MODEL), (8, E_A), (1, D_MODEL)]
    return pl.pallas_call(
        body, grid=(n,), name="conv_bwd",
        in_specs=[rev(D_MODEL), rev(D_MODEL)] + [rev(E_A)] * 7
        + [_const_spec((E_A, D_MODEL)), _const_spec((4, D_MODEL, E_A)), _const_spec((1, D_MODEL)),
           _const_spec((8, E_A))],
        out_specs=[rev(D_MODEL)] + [_acc_spec(s) for s in acc_shapes],
        out_shape=[jax.ShapeDtypeStruct((t, D_MODEL), F32)] + [jax.ShapeDtypeStruct(s, F32) for s in acc_shapes],
        scratch_shapes=[pltpu.VMEM((8, E_A), F32)],
        compiler_params=_params("arbitrary"),
    )(dx1, x, c, u, v, ym, kb, kcv, kg, w_out, w_in4, ga, cw)


WEIGHTS = ("a_norm", "a_w_in", "a_conv", "a_w_out", "kv_norm", "w_dkv", "ckv_norm", "w_ukv", "b_norm", "b_w_in",
           "b_q_norm", "b_w_uq", "b_w_out", "final_norm")
SHARD_SHAPES = {
    "a_norm": (1, 256), "a_w_in": (1, 1024, 1024), "a_conv": (1, 3, 256), "a_w_out": (1, 256, 1024),
    "kv_norm": (1024,), "w_dkv": (256, 288), "ckv_norm": (256,), "w_ukv": (256, 256), "b_norm": (1, 1024),
    "b_w_in": (1, 256, 896), "b_q_norm": (1, 384), "b_w_uq": (1, 384, 192), "b_w_out": (1, 512, 256),
    "final_norm": (1024,),
}
MATS = ("a_w_in", "a_w_out", "w_dkv", "w_ukv", "b_w_in", "b_w_uq", "b_w_out")
SMALL = ("a_norm", "a_conv", "kv_norm", "ckv_norm", "b_norm", "b_q_norm", "final_norm")
SMALL_FULL = {"a_norm": 1024, "a_conv": 3072, "kv_norm": 1024, "ckv_norm": 256, "b_norm": 1024, "b_q_norm": 384,
              "final_norm": 1024}
SMALL_ROWS = 8
LOSS_SLOT = sum(SMALL_FULL.values())


def _mat2d(name, a):
    return a.reshape(SHARD_SHAPES[name][-2:])


def _prep_first(g_win, gsmall):
    sm = gsmall.reshape(N_CHIPS, -1)
    a_conv = jnp.transpose(sm[:, 256:1024].reshape(N_CHIPS, CONV_WIDTH, 256), (1, 0, 2)).reshape(CONV_WIDTH, -1)
    return {"w_in4": g_win, "ga": sm[:, :256].reshape(1, -1), "cw": jnp.pad(a_conv, ((0, 8 - CONV_WIDTH), (0, 0)))}


def _prep_rest(gath, w):
    def cols(a):
        return jnp.transpose(a, (1, 0, 2)).reshape(a.shape[1], -1)

    def pad_heads(a, width):
        a = a.reshape(a.shape[0], N_HEADS, width)
        return jnp.pad(a, ((0, 0), (0, 0), (0, HEAD_PAD - width))).reshape(a.shape[0], QK_PAD)

    row = lambda a: a.reshape(1, -1).astype(F32)
    w_dkv = gath["w_dkv"].reshape(D_MODEL, KV_RANK + QK_ROPE)
    w_ukv = cols(gath["w_ukv"]).reshape(KV_RANK, N_HEADS, 2, QK_NOPE)
    return {
        "w_out": gath["a_w_out"].reshape(E_A, D_MODEL),
        "w_dkv": jnp.concatenate([w_dkv[:, :KV_RANK], jnp.zeros((D_MODEL, ROPE_LO), BF16), w_dkv[:, KV_RANK:],
                                  jnp.zeros((D_MODEL, HEAD_PAD - ROPE_LO - QK_ROPE), BF16)], axis=1),
        "w_uk": pad_heads(w_ukv[:, :, 0, :].reshape(KV_RANK, N_HEADS * QK_NOPE), QK_NOPE),
        "w_uv": w_ukv[:, :, 1, :].reshape(KV_RANK, E_B),
        "w_bin": gath["b_w_in"].reshape(D_MODEL, Q_RANK + E_B),
        "w_uq": pad_heads(cols(gath["b_w_uq"]), QK_NOPE + QK_ROPE),
        "w_bout": cols(gath["b_w_out"]),
        "gk": row(w["kv_norm"]), "gc": row(w["ckv_norm"]), "gb": row(w["b_norm"]),
        "gq": row(w["b_q_norm"]), "gf": row(w["final_norm"]),
    }


TRANSPOSED = ("w_dkv", "b_w_uq")
GRAD_MATS = ("w_in4", "w_out", "w_dkv", "w_uk", "w_uv", "w_bin", "w_uq", "w_bout")
GRAD_KIND = {"w_in4": "lead", "w_out": "row", "w_dkv": "row", "w_uk": "col", "w_uv": "col", "w_bin": "row",
             "w_uq": "col", "w_bout": "col"}


def _local_step(x, positions, tgt, w):
    bsz, seq, _ = x.shape
    t = bsz * seq
    x2d = x.reshape(t, D_MODEL)
    small = jnp.concatenate([w["a_norm"].reshape(-1), w["a_conv"].reshape(-1)]).reshape(8, LANES)
    rc, rs1, rs2, g_win, gsmall = _rope_tables_gather(positions.reshape(t, 1), _mat2d(MATS[0], w[MATS[0]]), small)
    wk = _prep_first(g_win, gsmall)
    (c, u, v, ym, kb, kcv, kg), gathered = _conv_fwd(x2d, seq, wk["ga"], wk["w_in4"], wk["cw"],
                                           [_mat2d(n, w[n]) for n in MATS[1:]])
    wk.update(_prep_rest(dict(zip(MATS[1:], gathered)), w))
    x1, q, k, vv, gate, cq, ckv = _mid_fwd(x2d, ym, wk["w_out"], wk["gk"], wk["gb"], wk["w_dkv"], wk["gc"], wk["w_uk"],
                                          wk["w_uv"], wk["w_bin"], wk["gq"], wk["w_uq"], rc, rs1, rs2)
    o, lse = _attn_fwd(q, k, vv, seq)
    dx2, do, dgate, dd, loss, dgf, dw_bout = _head_fwd_bwd(o, gate, x1, tgt.reshape(t, D_MODEL), wk["w_bout"], wk["gf"])
    dq, dk, dv = _attn_bwd(q, k, vv, do, lse, dd, seq)
    dx1, dwq, dw_bin, dwk, dwv, dwd, dgq, dgc, dgk, dgb = _mid_bwd(
        dq, dk, dv, dgate, dx2, x1, cq, ckv, rc, rs1, rs2, wk["w_uq"], wk["w_bin"], wk["w_uk"], wk["w_uv"], wk["w_dkv"],
        wk["gq"], wk["gc"], wk["gk"], wk["gb"])
    dx, dw_in4, dw_out, dcw, dga = _conv_bwd(dx1, x2d, c, u, v, ym, kb, kcv, kg, seq, wk["w_out"], wk["w_in4"], wk["ga"], wk["cw"])
    mats = {"w_in4": dw_in4, "w_out": dw_out, "w_dkv": dwd, "w_uk": dwk, "w_uv": dwv, "w_bin": dw_bin, "w_uq": dwq,
            "w_bout": dw_bout}
    small = {"a_norm": dga, "a_conv": dcw[:CONV_WIDTH], "kv_norm": dgk, "ckv_norm": dgc, "b_norm": dgb,
             "b_q_norm": dgq, "final_norm": dgf}
    return loss[0, 0], dx.reshape(bsz, seq, D_MODEL), mats, small


def _shard_grads(sh, svec):
    j0 = 2 * lax.axis_index("x") + lax.axis_index("y")
    flat = svec.reshape(-1)
    off, small = 0, {}
    for n in SMALL:
        small[n] = flat[off:off + SMALL_FULL[n]]
        off += SMALL_FULL[n]
    dwd, dwk, dwv, dwq = sh["w_dkv"], sh["w_uk"], sh["w_uv"], sh["w_uq"]
    w_ukv = jnp.stack([dwk.reshape(KV_RANK, 2, HEAD_PAD)[:, :, :QK_NOPE], dwv.reshape(KV_RANK, 2, V_HEAD)], axis=2)
    return {
        "a_norm": lax.dynamic_slice(small["a_norm"], (j0 * 256,), (256,)),
        "a_conv": lax.dynamic_slice(small["a_conv"].reshape(CONV_WIDTH, E_A), (0, j0 * 256), (CONV_WIDTH, 256)),
        "kv_norm": small["kv_norm"], "ckv_norm": small["ckv_norm"], "b_norm": small["b_norm"],
        "b_q_norm": small["b_q_norm"], "final_norm": small["final_norm"],
        "a_w_in": sh["w_in4"], "a_w_out": sh["w_out"],
        "w_dkv": jnp.concatenate([dwd[:, :KV_RANK], dwd[:, KV_RANK + ROPE_LO:KV_RANK + ROPE_LO + QK_ROPE]], axis=1),
        "w_ukv": w_ukv.reshape(KV_RANK, 2 * (QK_NOPE + V_HEAD)),
        "b_w_in": sh["w_bin"],
        "b_w_uq": dwq.reshape(Q_RANK, 2, HEAD_PAD)[:, :, :QK_NOPE + QK_ROPE].reshape(Q_RANK, -1),
        "b_w_out": sh["w_bout"],
    }


def _sub(ref, kind, j, cc):
    if kind == "lead":
        h = ref.shape[1] // 2
        return ref.at[j, pl.ds(pl.multiple_of(cc * h, 8), h), :]
    if kind == "row":
        rows = ref.shape[0] // N_CHIPS
        h = rows // 2
        return ref.at[pl.ds(pl.multiple_of(j * rows + cc * h, 8), h), :]
    cols = ref.shape[1] // N_CHIPS
    h = ref.shape[0] // 2
    return ref.at[pl.ds(pl.multiple_of(cc * h, 8), h), pl.ds(j * cols, cols)]


def _sub_shape(shape, kind):
    if kind == "lead":
        return (shape[1] // 2, shape[2])
    if kind == "row":
        return (shape[0] // N_CHIPS // 2, shape[1])
    return (shape[0] // 2, shape[1] // N_CHIPS)


def _reduce_grads(grads, kinds, vec):
    n = len(grads)
    shapes = [_sub_shape(a.shape, kd) for a, kd in zip(grads, kinds)]
    units = [(k, j) for k in range(n) for j in range(N_CHIPS)]
    big = (max(s[0] for s in shapes), max(s[1] for s in shapes))

    def body(*refs):
        g, v_ref = refs[:n], refs[n]
        out, o_ref = refs[n + 1:2 * n + 1], refs[2 * n + 1]
        theirs, part, recd, red = (refs[(2 + i) * n + 2:(3 + i) * n + 2] for i in range(4))
        mine, got = refs[6 * n + 2], refs[6 * n + 3]
        send1, recv1, send3, recv3, send5, recv5, load, local, send_v, recv_v = refs[6 * n + 4:]
        x, y, c = _place()
        j0 = 2 * x + y
        me = 2 * j0 + c
        sibling = (x, y, 1 - c)

        got[me] = v_ref[...]
        first, small_in = [], []
        for d in range(1, N_DEV):
            px, py, pc = x ^ (d >> 2), y ^ ((d >> 1) & 1), c ^ (d & 1)
            first.append(_remote(v_ref, got.at[me], send_v.at[d - 1], recv_v.at[d - 1], (px, py, pc)))
            small_in.append(_remote(v_ref, got.at[4 * px + 2 * py + pc], send_v.at[d - 1], recv_v.at[d - 1],
                                    (px, py, pc)))
        halves = [_remote(_sub(g[k], kinds[k], j, 1 - c), theirs[k].at[j], send1.at[u], recv1.at[u], sibling)
                  for u, (k, j) in enumerate(units)]
        for cp in first + halves:
            cp.start()

        def mine_load(u):
            k, j = units[u]
            h, cols = shapes[k]
            return pltpu.make_async_copy(_sub(g[k], kinds[k], j, c), mine.at[u % 2, pl.ds(0, h), pl.ds(0, cols)],
                                         load.at[u % 2])

        mine_load(0).start()
        for u, (k, j) in enumerate(units):
            h, cols = shapes[k]
            if u + 1 < len(units):
                mine_load(u + 1).start()
            mine_load(u).wait()
            halves[u].wait_recv()
            part[k][j] = (mine[u % 2, 0:h, 0:cols] + theirs[k][j]).astype(BF16)
            to_owner = _remote(part[k].at[j], recd[k].at[j0], send3.at[u], recv3.at[4 * k + j0], (j // 2, j % 2, c))

            @pl.when(j != j0)
            def _():
                to_owner.start()

            @pl.when(j == j0)
            def _():
                recd[k][j] = part[k][j]

        swaps = []
        for k in range(n):
            for j in range(N_CHIPS):
                arrived = _remote(part[k].at[j], recd[k].at[j], send3.at[4 * k + j], recv3.at[4 * k + j],
                                  (j // 2, j % 2, c))

                @pl.when(j != j0)
                def _():
                    arrived.wait_recv()

            r = recd[k]
            red[k][...] = ((r[0].astype(F32) + r[1].astype(F32)) + r[2].astype(F32)) + r[3].astype(F32)
            own = pltpu.make_async_copy(red[k], out[k].at[c], local.at[k])
            give = _remote(red[k], out[k].at[c], send5.at[k], recv5.at[k], sibling)
            take = _remote(red[k], out[k].at[1 - c], send5.at[k], recv5.at[k], sibling)
            own.start()
            give.start()
            swaps.append((own, give, take))

        for cp in small_in:
            cp.wait_recv()
            cp.wait_send()
        acc = got[0]
        for d in range(1, N_DEV):
            acc = acc + got[d]
        o_ref[...] = acc
        for u, (k, j) in enumerate(units):
            halves[u].wait_send()
            sent = _remote(part[k].at[j], recd[k].at[j0], send3.at[u], recv3.at[u], (j // 2, j % 2, c))

            @pl.when(j != j0)
            def _():
                sent.wait_send()
        for own, give, take in swaps:
            take.wait_recv()
            give.wait_send()
            own.wait()

    sems = [pltpu.SemaphoreType.DMA((len(units),))] * 4 + [pltpu.SemaphoreType.DMA((n,))] * 2 \
        + [pltpu.SemaphoreType.DMA((2,)), pltpu.SemaphoreType.DMA((n,))] + [pltpu.SemaphoreType.DMA((N_DEV - 1,))] * 2
    res = pl.pallas_call(
        body, name="reduce_grads",
        in_specs=[HBM] * n + [WHOLE], out_specs=[HBM] * n + [WHOLE],
        out_shape=[jax.ShapeDtypeStruct((2,) + s, F32) for s in shapes] + [jax.ShapeDtypeStruct(vec.shape, vec.dtype)],
        scratch_shapes=[pltpu.VMEM((N_CHIPS,) + s, F32) for s in shapes]
        + [pltpu.VMEM((N_CHIPS,) + s, BF16) for s in shapes] * 2
        + [pltpu.VMEM(s, F32) for s in shapes]
        + [pltpu.VMEM((2,) + big, F32), pltpu.VMEM((N_DEV,) + vec.shape, vec.dtype)] + sems,
        compiler_params=_comm_params(),
    )(*grads, vec)
    return res[:n], res[n]


def _adamw_math(w, g, m, v):
    m = ADAM_B1 * m + (1.0 - ADAM_B1) * g
    v = ADAM_B2 * v + (1.0 - ADAM_B2) * (g * g)
    m_hat = m / (1.0 - ADAM_B1 ** ADAM_STEP)
    v_hat = v / (1.0 - ADAM_B2 ** ADAM_STEP)
    return -ADAM_LR * (m_hat / (jnp.sqrt(v_hat) + ADAM_EPS) + ADAM_WD * w), m, v


def _adamw_tiled(w, g, m, v):
    rows, width = w.shape
    tm = rows // 4

    def body(w_ref, g_ref, m_ref, v_ref, go_ref, d_ref, mo_ref, vo_ref):
        g = g_ref[...]
        go_ref[...] = g
        d_ref[...], mo_ref[...], vo_ref[...] = _adamw_math(w_ref[...], g, m_ref[...], v_ref[...])

    spec = pl.BlockSpec((tm, width), lambda i: (i, 0))
    out = jax.ShapeDtypeStruct((rows, width), F32)
    return pl.pallas_call(
        body, grid=(rows // tm,), name="adamw_tiled",
        in_specs=[spec] * 4, out_specs=[spec] * 4, out_shape=[out] * 4,
        compiler_params=_params("parallel"),
    )(w, g, m, v)


def _adamw_many(ws, gs, ms, vs):
    n = len(ws)

    def body(*refs):
        for k in range(n):
            w_ref, g_ref, m_ref, v_ref = (refs[i * n + k] for i in range(4))
            go_ref, d_ref, mo_ref, vo_ref = (refs[(4 + i) * n + k] for i in range(4))
            g = g_ref[...]
            go_ref[...] = g
            d_ref[...], mo_ref[...], vo_ref[...] = _adamw_math(w_ref[...], g, m_ref[...], v_ref[...])

    outs = [jax.ShapeDtypeStruct(a.shape, F32) for a in ws]
    res = pl.pallas_call(
        body, name="adamw_many",
        in_specs=[WHOLE] * (4 * n), out_specs=[WHOLE] * (4 * n), out_shape=outs * 4,
        compiler_params=_comm_params(),
    )(*ws, *gs, *ms, *vs)
    return res[:n], res[n:2 * n], res[2 * n:3 * n], res[3 * n:]


def kernel(x, positions, a_norm, a_w_in, a_conv, a_w_out, kv_norm, w_dkv, ckv_norm, w_ukv, b_norm, b_w_in, b_q_norm, b_w_uq, b_w_out, final_norm, loss_target, m_a_norm, m_a_w_in, m_a_conv, m_a_w_out, m_kv_norm, m_w_dkv, m_ckv_norm, m_w_ukv, m_b_norm, m_b_w_in, m_b_q_norm, m_b_w_uq, m_b_w_out, m_final_norm, v_a_norm, v_a_w_in, v_a_conv, v_a_w_out, v_kv_norm, v_w_dkv, v_ckv_norm, v_w_ukv, v_b_norm, v_b_w_in, v_b_q_norm, v_b_w_uq, v_b_w_out, v_final_norm):
    w = dict(a_norm=a_norm, a_w_in=a_w_in, a_conv=a_conv, a_w_out=a_w_out, kv_norm=kv_norm, w_dkv=w_dkv,
             ckv_norm=ckv_norm, w_ukv=w_ukv, b_norm=b_norm, b_w_in=b_w_in, b_q_norm=b_q_norm, b_w_uq=b_w_uq,
             b_w_out=b_w_out, final_norm=final_norm)
    m = dict(a_norm=m_a_norm, a_w_in=m_a_w_in, a_conv=m_a_conv, a_w_out=m_a_w_out, kv_norm=m_kv_norm, w_dkv=m_w_dkv,
             ckv_norm=m_ckv_norm, w_ukv=m_w_ukv, b_norm=m_b_norm, b_w_in=m_b_w_in, b_q_norm=m_b_q_norm,
             b_w_uq=m_b_w_uq, b_w_out=m_b_w_out, final_norm=m_final_norm)
    v = dict(a_norm=v_a_norm, a_w_in=v_a_w_in, a_conv=v_a_conv, a_w_out=v_a_w_out, kv_norm=v_kv_norm, w_dkv=v_w_dkv,
             ckv_norm=v_ckv_norm, w_ukv=v_w_ukv, b_norm=v_b_norm, b_w_in=v_b_w_in, b_q_norm=v_b_q_norm,
             b_w_uq=v_b_w_uq, b_w_out=v_b_w_out, final_norm=v_final_norm)

    loss, dx, gmat, gsmall = _local_step(x, positions, loss_target, w)

    kinds = [GRAD_KIND[n] for n in GRAD_MATS]
    grads = [gmat[n] for n in GRAD_MATS]
    flat = jnp.concatenate([gsmall[n].reshape(-1) for n in SMALL] + [loss.reshape(1)])
    flat = jnp.pad(flat, (0, SMALL_ROWS * PACK_W - flat.shape[0])).reshape(SMALL_ROWS, PACK_W)
    mine, svec = _reduce_grads(grads, kinds, flat)
    loss = svec.reshape(-1)[LOSS_SLOT]
    g = _shard_grads({n: a.reshape(-1, a.shape[-1]) for n, a in zip(GRAD_MATS, mine)}, svec)

    def two_d(n, a):
        a = a.reshape(-1, a.shape[-1])
        return a.T if n in TRANSPOSED else a

    def back(n, a):
        return (a.T if n in TRANSPOSED else a).reshape(SHARD_SHAPES[n])

    big = "a_w_in"
    rest = [n for n in WEIGHTS if n != big]
    res_big = _adamw_tiled(*(two_d(big, t[big]) for t in (w, g, m, v)))
    res_rest = _adamw_many(*([two_d(n, t[n]) for n in rest] for t in (w, g, m, v)))
    out = {kind: dict(zip(rest, res_rest[i])) for i, kind in enumerate("gdmv")}
    for i, kind in enumerate("gdmv"):
        out[kind][big] = res_big[i]
    return (loss, dx) + tuple(back(n, out[kind][n]) for kind in "gdmv" for n in WEIGHTS)
```

```python
import functools
import math

import numpy as np
import jax
import jax.numpy as jnp
from jax import lax
from jax.experimental import pallas as pl
from jax.experimental.pallas import tpu as pltpu

F32 = jnp.float32
BF16 = jnp.bfloat16

D_MODEL = 1024
E_A = 1024
CONV_WIDTH = 3
N_HEADS = 8
QK_NOPE = 64
QK_ROPE = 32
V_HEAD = 64
KV_RANK = 256
Q_RANK = 384
E_B = N_HEADS * V_HEAD
ROPE_THETA = 10000.0
SOFTMAX_SCALE = 1.0 / math.sqrt(QK_NOPE + QK_ROPE)
LOG2E = math.log2(math.e)
LN2 = math.log(2.0)
Q_PRESCALE = SOFTMAX_SCALE * LOG2E
EPS = 1e-6
HEAD_PAD = 128
QK_PAD = N_HEADS * HEAD_PAD
ROPE_LO = QK_NOPE
ROPE_HALF = QK_ROPE // 2
KR_PAD = KV_RANK + HEAD_PAD

ADAM_LR = 0.001
ADAM_B1 = 0.9
ADAM_B2 = 0.999
ADAM_EPS = 1e-08
ADAM_WD = 0.01
ADAM_STEP = 10

VMEM_LIMIT = 56 * 1024 * 1024
HEAD_VMEM_LIMIT = 62 * 1024 * 1024
ROW_TILE = 512
CONV_BWD_TILE = 256
ATT_TILE_FWD = 1024
ATT_TILE = 512
HEAD_CHAINS = 2
LANES = 128
PACK_W = 1024

N_CHIPS = 4
N_DEV = 8


def _dot(a, b):
    return jnp.dot(a, b, preferred_element_type=F32)


def _dot_nt(a, b):
    return lax.dot_general(a, b, (((1,), (1,)), ((), ())), preferred_element_type=F32)


def _dot_tn(a, b):
    return lax.dot_general(a, b, (((0,), (0,)), ((), ())), preferred_element_type=F32)


def _rms(x):
    r = lax.rsqrt(jnp.mean(x * x, axis=-1, keepdims=True) + EPS)
    return r, x * r


def _rms_bwd(dxh, xh, r):
    return r * (dxh - xh * jnp.mean(dxh * xh, axis=-1, keepdims=True))


def _rope_fwd(a, c, s1, s2):
    return a * c + pltpu.roll(a, HEAD_PAD - ROPE_HALF, 1) * s1 + pltpu.roll(a, ROPE_HALF, 1) * s2


def _rope_bwd(g, c, s1, s2):
    return g * c + pltpu.roll(g * s1, ROPE_HALF, 1) + pltpu.roll(g * s2, HEAD_PAD - ROPE_HALF, 1)


def _sigmoid(x):
    return 1.0 / (1.0 + jnp.exp(-x))


def _row_spec(tm, n):
    return pl.BlockSpec((tm, n), lambda i: (i, 0))


def _const_spec(shape):
    nd = len(shape)
    return pl.BlockSpec(shape, lambda i: (0,) * nd, pipeline_mode=pl.Buffered(1))


def _acc_spec(shape):
    nd = len(shape)
    return pl.BlockSpec(shape, lambda i: (0,) * nd, pipeline_mode=pl.Buffered(1))


def _params(*sem):
    return pltpu.CompilerParams(dimension_semantics=sem, vmem_limit_bytes=VMEM_LIMIT)


MESH = pl.DeviceIdType.MESH
HBM = pl.BlockSpec(memory_space=pl.ANY)
WHOLE = pl.BlockSpec(memory_space=pltpu.VMEM)
FLIPS = ((1, 0), (0, 1), (1, 1))


def _place():
    return lax.axis_index("x"), lax.axis_index("y"), lax.axis_index("c")


def _remote(src, dst, send, recv, peer):
    return pltpu.make_async_remote_copy(src_ref=src, dst_ref=dst, send_sem=send, recv_sem=recv, device_id=peer,
                                        device_id_type=MESH)


def _comm_params():
    return pltpu.CompilerParams(vmem_limit_bytes=VMEM_LIMIT)


def _rope_consts():
    lane = np.arange(HEAD_PAD)
    first = (lane >= ROPE_LO) & (lane < ROPE_LO + ROPE_HALF)
    second = (lane >= ROPE_LO + ROPE_HALF) & (lane < ROPE_LO + QK_ROPE)
    f = np.where(first, lane - ROPE_LO, np.where(second, lane - ROPE_LO - ROPE_HALF, 0))
    inv = np.float32(ROPE_THETA) ** (-(2 * f).astype(np.float32) / np.float32(QK_ROPE))
    out = np.zeros((8, HEAD_PAD), np.float32)
    out[0] = inv
    out[1] = first
    out[2] = second
    out[3] = lane < ROPE_LO
    return jnp.asarray(out)


def _gather_plan(w, sm, outs, osm, bf, sems):
    n = len(w)
    send_i, recv_i, send_d, recv_d, send_s, recv_s, local = sems
    x, y, c = _place()
    j0 = 2 * x + y
    sibling = (x, y, 1 - c)
    own = [pltpu.make_async_copy(bf[k], outs[k].at[j0], local.at[k]) for k in range(n)]
    if sm is not None:
        own.append(pltpu.make_async_copy(sm, osm.at[j0], local.at[n]))

    def half(k, cc):
        h = w[k].shape[0] // 2
        return pl.ds(pl.multiple_of(cc * h, 16), h)

    sends, arrivals, forwards, fwaits = [], [], [], []
    for i, (fx, fy) in enumerate(FLIPS):
        px, py = x ^ fx, y ^ fy
        pj = 2 * px + py
        for k in range(n):
            s = i * n + k
            sends.append(_remote(bf[k].at[half(k, c)], outs[k].at[j0, half(k, c)], send_i.at[s], recv_i.at[s],
                                 (px, py, c)))
            arrivals.append(_remote(bf[k].at[half(k, c)], outs[k].at[pj, half(k, c)], send_i.at[s], recv_i.at[s],
                                    (px, py, c)))
            forwards.append(_remote(outs[k].at[pj, half(k, c)], outs[k].at[pj, half(k, c)], send_d.at[s],
                                    recv_d.at[s], sibling))
            fwaits.append(_remote(outs[k].at[pj, half(k, 1 - c)], outs[k].at[pj, half(k, 1 - c)], send_d.at[s],
                                  recv_d.at[s], sibling))
        if sm is not None:
            sends.append(_remote(sm, osm.at[j0], send_s.at[i], recv_s.at[i], (px, py, c)))
            fwaits.append(_remote(sm, osm.at[pj], send_s.at[i], recv_s.at[i], (px, py, c)))
    return own, sends, arrivals, forwards, fwaits


def _gather_begin(w, bf, plan):
    own, sends, _, _, _ = plan
    for k in range(len(w)):
        bf[k][...] = w[k][...].astype(BF16)
    for cp in own + sends:
        cp.start()


def _gather_pass_on(plan):
    _, _, arrivals, forwards, _ = plan
    for cp, fwd in zip(arrivals, forwards):
        cp.wait_recv()
        fwd.start()


def _gather_end(plan):
    own, sends, _, forwards, fwaits = plan
    for cp in fwaits:
        cp.wait_recv()
    for cp in sends + forwards:
        cp.wait_send()
    for cp in own:
        cp.wait()


def _gather_sems(n, with_small):
    return ([pltpu.SemaphoreType.DMA((3 * n,))] * 4 + [pltpu.SemaphoreType.DMA((3,))] * 2
            + [pltpu.SemaphoreType.DMA((n + (1 if with_small else 0),))])


def _rope_tables_gather(pos_col, w_in, small):
    t = pos_col.shape[0]
    tm = min(ROW_TILE, t)
    steps = t // tm

    def body(p_ref, k_ref, w_ref, sm_ref, c_ref, s1_ref, s2_ref, out_ref, osm_ref, bf_ref, *sems):
        plan = _gather_plan([w_ref], sm_ref, [out_ref], osm_ref, [bf_ref], sems)
        i = pl.program_id(0)

        @pl.when(i == 0)
        def _():
            _gather_begin([w_ref], [bf_ref], plan)

        inv, first, second, nope = k_ref[0:1, :], k_ref[1:2, :], k_ref[2:3, :], k_ref[3:4, :]
        ang = p_ref[...].astype(F32) * inv
        cs, sn = jnp.cos(ang), jnp.sin(ang)
        c_ref[...] = cs * (first + second) + nope
        s1_ref[...] = -sn * first
        s2_ref[...] = sn * second

        @pl.when(i == steps - 1)
        def _():
            _gather_pass_on(plan)
            _gather_end(plan)

    out = jax.ShapeDtypeStruct((t, HEAD_PAD), F32)
    return pl.pallas_call(
        body, grid=(steps,), name="rope_tables_gather",
        in_specs=[_row_spec(tm, 1), _const_spec((8, HEAD_PAD)), WHOLE, WHOLE],
        out_specs=[_row_spec(tm, HEAD_PAD)] * 3 + [HBM, HBM],
        out_shape=[out] * 3 + [jax.ShapeDtypeStruct((N_CHIPS,) + w_in.shape, BF16),
                               jax.ShapeDtypeStruct((N_CHIPS,) + small.shape, small.dtype)],
        scratch_shapes=[pltpu.VMEM(w_in.shape, BF16)] + _gather_sems(1, True),
        compiler_params=_params("arbitrary"),
    )(pos_col, _rope_consts(), w_in, small)


def _shift_down(v, prev, row):
    p1, p2 = prev[7:8, :], prev[6:7, :]
    v1 = jnp.where(row == 0, p1, pltpu.roll(v, 1, 0))
    v2 = jnp.where(row == 0, p2, jnp.where(row == 1, p1, pltpu.roll(v, 2, 0)))
    return v1, v2


def _conv_fwd(x, seq, ga, w_in4, cw, shards):
    t = x.shape[0]
    tm = min(ROW_TILE, seq)
    tiles_per_seq = seq // tm
    steps = t // tm
    n = len(shards)

    def body(x_ref, ga_ref, w_ref, cw_ref, *rest):
        sh, (c_ref, u_ref, v_ref, ym_ref, kb_ref, kcv_ref, kg_ref) = rest[:n], rest[n:n + 7]
        outs, carry_ref, bf, sems = rest[n + 7:2 * n + 7], rest[2 * n + 7], rest[2 * n + 8:3 * n + 8], rest[3 * n + 8:]
        plan = _gather_plan(sh, None, outs, None, bf, sems)
        i = pl.program_id(0)

        @pl.when(i == 0)
        def _():
            _gather_begin(sh, bf, plan)

        @pl.when(i % tiles_per_seq == 0)
        def _():
            carry_ref[...] = jnp.zeros_like(carry_ref)

        _, xh = _rms(x_ref[...])
        h = (xh * ga_ref[...]).astype(BF16)
        c, u = _dot(h, w_ref[1]), _dot(h, w_ref[2])
        g, b = _dot(h, w_ref[3]), _dot(h, w_ref[0])
        v = c * u
        row = lax.broadcasted_iota(jnp.int32, (tm, 1), 0)
        v1, v2 = _shift_down(v, carry_ref[...], row)
        carry_ref[...] = v[tm - 8:tm, :]
        cv = cw_ref[2:3, :] * v + cw_ref[1:2, :] * v1 + cw_ref[0:1, :] * v2
        sg = _sigmoid(g)
        silu = g * sg
        c_ref[...] = c.astype(BF16)
        u_ref[...] = u.astype(BF16)
        v_ref[...] = v.astype(BF16)
        ym_ref[...] = (silu * b * cv).astype(BF16)
        kb_ref[...] = (silu * cv).astype(BF16)
        kcv_ref[...] = (silu * b).astype(BF16)
        kg_ref[...] = (b * cv * (sg * (1.0 + g * (1.0 - sg)))).astype(BF16)

        @pl.when(i == max(steps - 2, 0))
        def _():
            _gather_pass_on(plan)

        @pl.when(i == steps - 1)
        def _():
            _gather_end(plan)

    out = jax.ShapeDtypeStruct((t, E_A), BF16)
    res = pl.pallas_call(
        body, grid=(steps,), name="conv_fwd",
        in_specs=[_row_spec(tm, D_MODEL), _const_spec((1, D_MODEL)), _const_spec((4, D_MODEL, E_A)),
                  _const_spec((8, E_A))] + [WHOLE] * n,
        out_specs=[_row_spec(tm, E_A)] * 7 + [HBM] * n,
        out_shape=[out] * 7 + [jax.ShapeDtypeStruct((N_CHIPS,) + a.shape, BF16) for a in shards],
        scratch_shapes=[pltpu.VMEM((8, E_A), F32)] + [pltpu.VMEM(a.shape, BF16) for a in shards]
        + _gather_sems(n, False),
        compiler_params=_params("arbitrary"),
    )(x, ga, w_in4, cw, *shards)
    return res[:7], res[7:]


def _mid_fwd(x, ym, w_out, gk, gb, w_dkv, gc, w_uk, w_uv, w_bin, gq, w_uq, rc, rs1, rs2):
    t = x.shape[0]
    tm = min(2 * ROW_TILE, t)

    def body(x_ref, ym_ref, wo_ref, gk_ref, gb_ref, wd_ref, gc_ref, wk_ref, wv_ref, wi_ref, gq_ref, wq_ref,
             c_ref, s1_ref, s2_ref, x1_ref, q_ref, k_ref, v_ref, gate_ref, cq_ref, ckv_ref):
        cb, s1b, s2b = c_ref[...], s1_ref[...], s2_ref[...]
        x1 = x_ref[...] + _dot(ym_ref[...], wo_ref[...])
        x1_ref[...] = x1
        _, xh = _rms(x1)
        hk = (xh * gk_ref[...]).astype(BF16)
        h1 = (xh * gb_ref[...]).astype(BF16)

        pb = _dot(h1, wi_ref[...])
        cq = pb[:, :Q_RANK]
        cq_ref[...] = cq.astype(BF16)
        gate_ref[...] = pb[:, Q_RANK:].astype(BF16)
        _, cqh = _rms(cq)
        q = _dot((cqh * gq_ref[...]).astype(BF16), wq_ref[...])
        for h in range(N_HEADS):
            sl = slice(h * HEAD_PAD, (h + 1) * HEAD_PAD)
            q_ref[:, sl] = (_rope_fwd(q[:, sl], cb, s1b, s2b) * Q_PRESCALE).astype(BF16)

        ckr = _dot(hk, wd_ref[...])
        ckv_raw = ckr[:, :KV_RANK]
        ckv_ref[...] = ckv_raw.astype(BF16)
        _, ch = _rms(ckv_raw)
        ckv = (ch * gc_ref[...]).astype(BF16)
        kr = _rope_fwd(ckr[:, KV_RANK:], cb, s1b, s2b)
        kn = _dot(ckv, wk_ref[...])
        for h in range(N_HEADS):
            sl = slice(h * HEAD_PAD, (h + 1) * HEAD_PAD)
            k_ref[:, sl] = (kn[:, sl] + kr).astype(BF16)
        v_ref[...] = _dot(ckv, wv_ref[...]).astype(BF16)

    def sds(n, dt):
        return jax.ShapeDtypeStruct((t, n), dt)

    return pl.pallas_call(
        body, grid=(t // tm,), name="mid_fwd",
        in_specs=[_row_spec(tm, D_MODEL), _row_spec(tm, E_A), _const_spec((E_A, D_MODEL)),
                  _const_spec((1, D_MODEL)), _const_spec((1, D_MODEL)), _const_spec((D_MODEL, KR_PAD)),
                  _const_spec((1, KV_RANK)), _const_spec((KV_RANK, QK_PAD)), _const_spec((KV_RANK, E_B)),
                  _const_spec((D_MODEL, Q_RANK + E_B)), _const_spec((1, Q_RANK)), _const_spec((Q_RANK, QK_PAD)),
                  _row_spec(tm, HEAD_PAD), _row_spec(tm, HEAD_PAD), _row_spec(tm, HEAD_PAD)],
        out_specs=[_row_spec(tm, D_MODEL), _row_spec(tm, QK_PAD), _row_spec(tm, QK_PAD), _row_spec(tm, E_B),
                   _row_spec(tm, E_B), _row_spec(tm, Q_RANK), _row_spec(tm, KV_RANK)],
        out_shape=[sds(D_MODEL, F32), sds(QK_PAD, BF16), sds(QK_PAD, BF16), sds(E_B, BF16), sds(E_B, BF16),
                   sds(Q_RANK, BF16), sds(KV_RANK, BF16)],
        compiler_params=_params("parallel"),
    )(x, ym, w_out, gk, gb, w_dkv, gc, w_uk, w_uv, w_bin, gq, w_uq, rc, rs1, rs2)


def _pair_specs(seq):
    qk = pl.BlockSpec((seq, 2 * HEAD_PAD), lambda b, p: (b, p))
    vo = pl.BlockSpec((seq, 2 * V_HEAD), lambda b, p: (b, p))
    st = pl.BlockSpec((None, 2, seq), lambda b, p: (p, 0, b))
    return qk, vo, st


def _attn_fwd(q, k, v, seq):
    t = q.shape[0]
    tq = min(ATT_TILE_FWD, seq)
    nq = seq // tq

    def body(q_ref, k_ref, v_ref, o_ref, lse_ref, m_scr, l_scr, acc_scr):
        lane = lax.broadcasted_iota(jnp.int32, (tq, 2 * V_HEAD), 1)

        def q_step(qi, _):
            q0 = pl.multiple_of(qi * tq, tq)
            m_scr[...] = jnp.full(m_scr.shape, -jnp.inf, F32)
            l_scr[...] = jnp.zeros_like(l_scr)
            acc_scr[...] = jnp.zeros_like(acc_scr)

            def block(q_lo, q_n, k0, k_n, masked):
                rows = slice(q_lo, q_lo + q_n)
                vt = v_ref[pl.ds(k0, k_n), :]

                def score(hh):
                    hs = slice(hh * HEAD_PAD, (hh + 1) * HEAD_PAD)
                    return _dot_nt(q_ref[pl.ds(q0 + q_lo, q_n), hs], k_ref[pl.ds(k0, k_n), hs])

                early = [score(hh) for hh in range(2)] if masked else None
                for hh in range(2):
                    s = early[hh] if masked else score(hh)
                    if masked:
                        row = lax.broadcasted_iota(jnp.int32, (q_n, k_n), 0)
                        col = lax.broadcasted_iota(jnp.int32, (q_n, k_n), 1)
                        s = jnp.where(col <= row, s, -jnp.inf)
                    m_old = m_scr[hh, rows]
                    m_new = jnp.maximum(m_old, jnp.max(s, axis=-1, keepdims=True))
                    alpha = jnp.exp2(m_old - m_new)
                    ps = [jnp.exp2(s[:, j * LANES:(j + 1) * LANES] - m_new) for j in range(k_n // LANES)]
                    l_scr[hh, rows] = alpha * l_scr[hh, rows] + functools.reduce(lambda a, b: a + b, ps)
                    p = jnp.concatenate(ps, axis=-1).astype(BF16)
                    acc_scr[hh, rows] = alpha * acc_scr[hh, rows] + _dot(p, vt)
                    m_scr[hh, rows] = m_new

            def k_step(ki, _):
                block(0, tq, pl.multiple_of(ki * tq, tq), tq, False)
                return 0

            lax.fori_loop(0, qi, k_step, 0)
            half = tq // 2
            block(0, tq, q0, half, True)
            block(half, half, q0 + half, half, True)
            l0 = jnp.sum(l_scr[0], axis=-1, keepdims=True)
            l1 = jnp.sum(l_scr[1], axis=-1, keepdims=True)
            o_ref[pl.ds(q0, tq), :] = jnp.where(lane < V_HEAD, acc_scr[0] / l0, acc_scr[1] / l1).astype(BF16)
            stats = jnp.where(lane == 0, m_scr[0] + jnp.log2(l0), m_scr[1] + jnp.log2(l1)).T
            lse_ref[:, pl.ds(q0, tq)] = stats[0:2, :]
            return 0

        lax.fori_loop(0, nq, q_step, 0)

    qk, vo, st = _pair_specs(seq)
    return pl.pallas_call(
        body, grid=(t // seq, N_HEADS // 2), name="attn_fwd",
        in_specs=[qk, qk, vo], out_specs=[vo, st],
        out_shape=[jax.ShapeDtypeStruct((t, E_B), BF16), jax.ShapeDtypeStruct((N_HEADS // 2, 2, t), F32)],
        scratch_shapes=[pltpu.VMEM((2, tq, LANES), F32), pltpu.VMEM((2, tq, LANES), F32),
                        pltpu.VMEM((2, tq, 2 * V_HEAD), F32)],
        compiler_params=_params("parallel", "parallel"),
    )(q, k, v)


def _head_fwd_bwd(o, gate, x1, tgt, w_bout, gf):
    t = o.shape[0]
    tm = min(2 * ROW_TILE, t)

    steps = t // tm
    slots = 3

    def body(o_ref, gate_ref, x1_hbm, tgt_hbm, w_ref, gf_ref,
             dx2_ref, do_ref, dgate_ref, dd_ref, loss_ref, dgf_ref, dw_ref, ring, ring_sem):
        step = pl.program_id(0)

        def fetch(s, k, src):
            rows = pl.ds(s * tm if isinstance(s, int) else pl.multiple_of(s * tm, tm), tm)
            return pltpu.make_async_copy(src.at[rows, :], ring.at[k, s % slots], ring_sem.at[k, s % slots])

        @pl.when(step == 0)
        def _():
            loss_ref[...] = jnp.zeros_like(loss_ref)
            dgf_ref[...] = jnp.zeros_like(dgf_ref)
            dw_ref[...] = jnp.zeros_like(dw_ref)
            for s in range(min(slots - 1, steps)):
                fetch(s, 0, x1_hbm).start()
                fetch(s, 1, tgt_hbm).start()

        @pl.when(step + slots - 1 < steps)
        def _():
            fetch(step + slots - 1, 0, x1_hbm).start()
            fetch(step + slots - 1, 1, tgt_hbm).start()

        fetch(step, 0, x1_hbm).wait()
        fetch(step, 1, tgt_hbm).wait()
        x1_ref, tgt_ref = ring.at[0, step % slots], ring.at[1, step % slots]

        hm = tm // HEAD_CHAINS
        gf = gf_ref[...]
        lane = lax.broadcasted_iota(jnp.int32, (hm, 2 * V_HEAD), 1)
        chains = []
        for ch in range(HEAD_CHAINS):
            rs = pl.ds(ch * hm, hm)
            o = o_ref[rs, :].astype(F32)
            gt = gate_ref[rs, :].astype(F32)
            sg = _sigmoid(gt)
            silu = gt * sg
            z = (o * silu).astype(BF16)
            chains.append((rs, o, gt, sg, silu, z, x1_ref[rs, :] + _dot(z, w_ref[...])))
        mids = []
        for rs, o, gt, sg, silu, z, x2 in chains:
            r2, xh2 = _rms(x2)
            err = xh2 * gf - tgt_ref[rs, :]
            loss_ref[...] += 0.5 * jnp.sum(jnp.mean(err * err, axis=-1, keepdims=True))
            dy = err * (1.0 / D_MODEL)
            dgf_ref[...] += jnp.sum(dy * xh2, axis=0, keepdims=True)
            dx2 = _rms_bwd(dy * gf, xh2, r2)
            dx2_ref[rs, :] = dx2
            dx2b = dx2.astype(BF16)
            mids.append(_dot_nt(dx2b, w_ref[...]))
            dw_ref[...] += _dot_tn(z, dx2b)
        for (rs, o, gt, sg, silu, z, x2), dz in zip(chains, mids):
            do = dz * silu
            do_ref[rs, :] = do.astype(BF16)
            dgate_ref[rs, :] = (dz * o * (sg * (1.0 + gt * (1.0 - sg)))).astype(BF16)
            prod = do * o
            cols = jnp.zeros((hm, LANES), F32)
            for p in range(N_HEADS // 2):
                blk = prod[:, p * 2 * V_HEAD:(p + 1) * 2 * V_HEAD]
                d0 = jnp.sum(jnp.where(lane < V_HEAD, blk, 0.0), axis=-1, keepdims=True)
                d1 = jnp.sum(jnp.where(lane < V_HEAD, 0.0, blk), axis=-1, keepdims=True)
                cols = jnp.where(lane == 2 * p, d0, jnp.where(lane == 2 * p + 1, d1, cols))
            rows = cols.T
            for h in range(N_HEADS):
                dd_ref[h // 2, h % 2:h % 2 + 1, rs] = rows[h:h + 1, :]

    return pl.pallas_call(
        body, grid=(steps,), name="head_fwd_bwd",
        in_specs=[_row_spec(tm, E_B), _row_spec(tm, E_B), HBM, HBM,
                  _const_spec((E_B, D_MODEL)), _const_spec((1, D_MODEL))],
        out_specs=[_row_spec(tm, D_MODEL), _row_spec(tm, E_B), _row_spec(tm, E_B),
                   pl.BlockSpec((N_HEADS // 2, 2, tm), lambda i: (0, 0, i)),
                   _acc_spec((1, 1)), _acc_spec((1, D_MODEL)), _acc_spec((E_B, D_MODEL))],
        out_shape=[jax.ShapeDtypeStruct((t, D_MODEL), F32), jax.ShapeDtypeStruct((t, E_B), BF16),
                   jax.ShapeDtypeStruct((t, E_B), BF16), jax.ShapeDtypeStruct((N_HEADS // 2, 2, t), F32),
                   jax.ShapeDtypeStruct((1, 1), F32), jax.ShapeDtypeStruct((1, D_MODEL), F32),
                   jax.ShapeDtypeStruct((E_B, D_MODEL), F32)],
        scratch_shapes=[pltpu.VMEM((2, slots, tm, D_MODEL), F32), pltpu.SemaphoreType.DMA((2, slots))],
        compiler_params=pltpu.CompilerParams(dimension_semantics=("arbitrary",), vmem_limit_bytes=HEAD_VMEM_LIMIT),
    )(o, gate, x1, tgt, w_bout, gf)


def _attn_bwd(q, k, v, do, lse, dd, seq):
    t = q.shape[0]
    tq = min(ATT_TILE, seq)
    nq = seq // tq
    assert nq % 2 == 0, "full tiles are taken in pairs"

    def body(q_ref, k_ref, v_ref, do_ref, lse_ref, dd_ref, dq_ref, dk_ref, dv_ref, dq_acc, dk_acc, dv_acc):
        dq_acc[...] = jnp.zeros_like(dq_acc)

        def k_step(ki, _):
            k0 = pl.multiple_of(ki * tq, tq)
            dk_acc[...] = jnp.zeros_like(dk_acc)
            dv_acc[...] = jnp.zeros_like(dv_acc)

            def block(k_lo, k_n, q0, q_n, masked):
                rows = slice(k_lo, k_lo + k_n)
                lane = lax.broadcasted_iota(jnp.int32, (q_n, 2 * V_HEAD), 1)
                vt = v_ref[pl.ds(k0 + k_lo, k_n), :]
                do_pair = do_ref[pl.ds(q0, q_n), :]

                def operands(hh):
                    hs = slice(hh * HEAD_PAD, (hh + 1) * HEAD_PAD)
                    kt = k_ref[pl.ds(k0 + k_lo, k_n), hs]
                    qt = q_ref[pl.ds(q0, q_n), hs]
                    mine = (lane < V_HEAD) if hh == 0 else (lane >= V_HEAD)
                    do_h = jnp.where(mine, do_pair, jnp.zeros((), BF16))
                    return hs, kt, qt, do_h, _dot_nt(kt, qt), _dot_nt(vt, do_h)

                early = [operands(hh) for hh in range(2)] if masked else None
                for hh in range(2):
                    hs, kt, qt, do_h, st, dpt = early[hh] if masked else operands(hh)
                    if masked:
                        krow = lax.broadcasted_iota(jnp.int32, (k_n, q_n), 0)
                        qcol = lax.broadcasted_iota(jnp.int32, (k_n, q_n), 1)
                        st = jnp.where(krow <= qcol, st, -jnp.inf)
                    pt = jnp.exp2(st - lse_ref[hh:hh + 1, pl.ds(q0, q_n)])
                    dst = (pt * (dpt - dd_ref[hh:hh + 1, pl.ds(q0, q_n)])).astype(BF16)
                    dv_acc[rows, :] += _dot(pt.astype(BF16), do_h)
                    dk_acc[rows, hs] += _dot(dst, qt)
                    dq_acc[pl.ds(q0, q_n), hs] += _dot_tn(dst, kt)

            wide = 2 * tq

            def q_step(qj, _):
                block(0, tq, pl.multiple_of(qj * wide, wide), wide, False)
                return 0

            half = tq // 2
            block(0, half, k0, tq, True)
            block(half, half, pl.multiple_of(k0 + half, half), half, True)

            @pl.when(ki % 2 == 0)
            def _():
                block(0, tq, pl.multiple_of(k0 + tq, tq), tq, False)

            lax.fori_loop(ki // 2 + 1, nq // 2, q_step, 0)
            dk_ref[pl.ds(k0, tq), :] = (dk_acc[...] * LN2).astype(BF16)
            dv_ref[pl.ds(k0, tq), :] = dv_acc[...].astype(BF16)
            return 0

        lax.fori_loop(0, nq, k_step, 0)
        dq_ref[...] = (dq_acc[...] * SOFTMAX_SCALE).astype(BF16)

    qk, vo, st = _pair_specs(seq)
    return pl.pallas_call(
        body, grid=(t // seq, N_HEADS // 2), name="attn_bwd",
        in_specs=[qk, qk, vo, vo, st, st], out_specs=[qk, qk, vo],
        out_shape=[jax.ShapeDtypeStruct((t, QK_PAD), BF16), jax.ShapeDtypeStruct((t, QK_PAD), BF16),
                   jax.ShapeDtypeStruct((t, E_B), BF16)],
        scratch_shapes=[pltpu.VMEM((seq, 2 * HEAD_PAD), F32), pltpu.VMEM((tq, 2 * HEAD_PAD), F32),
                        pltpu.VMEM((tq, 2 * V_HEAD), F32)],
        compiler_params=_params("parallel", "parallel"),
    )(q, k, v, do, lse, dd)


def _mid_bwd(dq, dk, dv, dgate, dx2, x1, cq, ckv, rc, rs1, rs2, w_uq, w_bin, w_uk, w_uv, w_dkv, gq, gc, gk, gb):
    t = dq.shape[0]
    tm = min(ROW_TILE, t)

    def body(dq_ref, dk_ref, dv_ref, dgate_ref, dx2_ref, x1_ref, cq_ref, ckv_ref, c_ref, s1_ref, s2_ref,
             wq_ref, wi_ref, wk_ref, wv_ref, wd_ref, gq_ref, gc_ref, gk_ref, gb_ref,
             dx1_ref, dwq_ref, dwi_ref, dwk_ref, dwv_ref, dwd_ref, dgq_ref, dgc_ref, dgk_ref, dgb_ref):
        @pl.when(pl.program_id(0) == 0)
        def _():
            for ref in (dwq_ref, dwi_ref, dwk_ref, dwv_ref, dwd_ref, dgq_ref, dgc_ref, dgk_ref, dgb_ref):
                ref[...] = jnp.zeros_like(ref)

        cb, s1b, s2b = c_ref[...], s1_ref[...], s2_ref[...]
        gk, gb, gq, gc = gk_ref[...], gb_ref[...], gq_ref[...], gc_ref[...]

        dkb, dvb = dk_ref[...], dv_ref[...]
        dckv = _dot_nt(dkb, wk_ref[...]) + _dot_nt(dvb, wv_ref[...])
        rcv, ch = _rms(ckv_ref[...].astype(F32))
        ckvn = (ch * gc).astype(BF16)
        dwk_ref[...] += _dot_tn(ckvn, dkb)
        dwv_ref[...] += _dot_tn(ckvn, dvb)

        dqs = [_rope_bwd(dq_ref[:, h * HEAD_PAD:(h + 1) * HEAD_PAD].astype(F32), cb, s1b, s2b)
               for h in range(N_HEADS)]
        dqb = jnp.concatenate(dqs, axis=-1).astype(BF16)
        rq, cqh = _rms(cq_ref[...].astype(F32))
        dcqn = _dot_nt(dqb, wq_ref[...])
        dwq_ref[...] += _dot_tn((cqh * gq).astype(BF16), dqb)

        dgc_ref[...] += jnp.sum(dckv * ch, axis=0, keepdims=True)
        dckv_raw = _rms_bwd(dckv * gc, ch, rcv)
        dkr = dk_ref[:, 0:HEAD_PAD].astype(F32)
        for h in range(1, N_HEADS):
            dkr = dkr + dk_ref[:, h * HEAD_PAD:(h + 1) * HEAD_PAD].astype(F32)
        dkr = _rope_bwd(dkr, cb, s1b, s2b)
        dckr = jnp.concatenate([dckv_raw, dkr], axis=-1).astype(BF16)
        r1, xh = _rms(x1_ref[...])
        dhk = _dot_nt(dckr, wd_ref[...])
        dwd_ref[...] += _dot_tn((xh * gk).astype(BF16), dckr)

        dgq_ref[...] += jnp.sum(dcqn * cqh, axis=0, keepdims=True)
        dcq = _rms_bwd(dcqn * gq, cqh, rq)
        dpb = jnp.concatenate([dcq.astype(BF16), dgate_ref[...]], axis=-1)
        dh1 = _dot_nt(dpb, wi_ref[...])
        dwi_ref[...] += _dot_tn((xh * gb).astype(BF16), dpb)

        dgb_ref[...] += jnp.sum(dh1 * xh, axis=0, keepdims=True)
        dgk_ref[...] += jnp.sum(dhk * xh, axis=0, keepdims=True)
        dx1_ref[...] = dx2_ref[...] + _rms_bwd(dh1 * gb + dhk * gk, xh, r1)

    acc_shapes = [(Q_RANK, QK_PAD), (D_MODEL, Q_RANK + E_B), (KV_RANK, QK_PAD), (KV_RANK, E_B), (D_MODEL, KR_PAD),
                  (1, Q_RANK), (1, KV_RANK), (1, D_MODEL), (1, D_MODEL)]
    return pl.pallas_call(
        body, grid=(t // tm,), name="mid_bwd",
        in_specs=[_row_spec(tm, QK_PAD), _row_spec(tm, QK_PAD), _row_spec(tm, E_B), _row_spec(tm, E_B),
                  _row_spec(tm, D_MODEL), _row_spec(tm, D_MODEL), _row_spec(tm, Q_RANK), _row_spec(tm, KV_RANK),
                  _row_spec(tm, HEAD_PAD), _row_spec(tm, HEAD_PAD), _row_spec(tm, HEAD_PAD),
                  _const_spec((Q_RANK, QK_PAD)), _const_spec((D_MODEL, Q_RANK + E_B)),
                  _const_spec((KV_RANK, QK_PAD)), _const_spec((KV_RANK, E_B)), _const_spec((D_MODEL, KR_PAD)),
                  _const_spec((1, Q_RANK)), _const_spec((1, KV_RANK)), _const_spec((1, D_MODEL)),
                  _const_spec((1, D_MODEL))],
        out_specs=[_row_spec(tm, D_MODEL)] + [_acc_spec(s) for s in acc_shapes],
        out_shape=[jax.ShapeDtypeStruct((t, D_MODEL), F32)] + [jax.ShapeDtypeStruct(s, F32) for s in acc_shapes],
        compiler_params=_params("arbitrary"),
    )(dq, dk, dv, dgate, dx2, x1, cq, ckv, rc, rs1, rs2, w_uq, w_bin, w_uk, w_uv, w_dkv, gq, gc, gk, gb)


def _conv_bwd(dx1, x, c, u, v, ym, kb, kcv, kg, seq, w_out, w_in4, ga, cw):
    t = x.shape[0]
    tm = min(CONV_BWD_TILE, seq)
    tiles_per_seq = seq // tm
    n = t // tm

    def tile(i):
        return n - 1 - i

    def rev(width):
        return pl.BlockSpec((tm, width), lambda i: (tile(i), 0))

    def body(dx1_ref, x_ref, c_ref, u_ref, v_ref, ym_ref, kb_ref, kcv_ref, kg_ref, wo_ref, wi_ref, ga_ref, cw_ref,
             dx_ref, dwi_ref, dwo_ref, dcw_ref, dga_ref, carry_ref):
        i = pl.program_id(0)
        j = tile(i)

        @pl.when(i == 0)
        def _():
            for ref in (dwi_ref, dwo_ref, dcw_ref, dga_ref):
                ref[...] = jnp.zeros_like(ref)

        @pl.when(j % tiles_per_seq == tiles_per_seq - 1)
        def _():
            carry_ref[...] = jnp.zeros_like(carry_ref)

        dx1 = dx1_ref[...]
        dx1b = dx1.astype(BF16)
        dym = _dot_nt(dx1b, wo_ref[...])
        dwo_ref[...] += _dot_tn(ym_ref[...], dx1b)
        db = dym * kb_ref[...].astype(F32)
        dcv = dym * kcv_ref[...].astype(F32)
        dg = dym * kg_ref[...].astype(F32)

        row = lax.broadcasted_iota(jnp.int32, (tm, 1), 0)
        w0, w1, w2 = cw_ref[0:1, :], cw_ref[1:2, :], cw_ref[2:3, :]
        nxt = carry_ref[...]
        n0, n1 = nxt[0:1, :], nxt[1:2, :]
        d1 = jnp.where(row == tm - 1, n0, pltpu.roll(dcv, tm - 1, 0))
        d2 = jnp.where(row == tm - 1, n1, jnp.where(row == tm - 2, n0, pltpu.roll(dcv, tm - 2, 0)))
        carry_ref[...] = dcv[0:8, :]
        dv = w2 * dcv + w1 * d1 + w0 * d2
        v = v_ref[...].astype(F32)
        dcw_ref[0:1, :] += jnp.sum(d2 * v, axis=0, keepdims=True)
        dcw_ref[1:2, :] += jnp.sum(d1 * v, axis=0, keepdims=True)
        dcw_ref[2:3, :] += jnp.sum(dcv * v, axis=0, keepdims=True)

        r0, xh = _rms(x_ref[...])
        ga = ga_ref[...]
        h = (xh * ga).astype(BF16)
        dh = jnp.zeros((tm, D_MODEL), F32)
        for idx, dpart in enumerate((db, dv * u_ref[...].astype(F32), dv * c_ref[...].astype(F32), dg)):
            dpb = dpart.astype(BF16)
            dh = dh + _dot_nt(dpb, wi_ref[idx])
            dwi_ref[idx] += _dot_tn(h, dpb)
        dga_ref[...] += jnp.sum(dh * xh, axis=0, keepdims=True)
        dx_ref[...] = dx1 + _rms_bwd(dh * ga, xh, r0)

    acc_shapes = [(4, D_MODEL, E_A), (E_A, D_MODEL), (8, E_A), (1, D_MODEL)]
    return pl.pallas_call(
        body, grid=(n,), name="conv_bwd",
        in_specs=[rev(D_MODEL), rev(D_MODEL)] + [rev(E_A)] * 7
        + [_const_spec((E_A, D_MODEL)), _const_spec((4, D_MODEL, E_A)), _const_spec((1, D_MODEL)),
           _const_spec((8, E_A))],
        out_specs=[rev(D_MODEL)] + [_acc_spec(s) for s in acc_shapes],
        out_shape=[jax.ShapeDtypeStruct((t, D_MODEL), F32)] + [jax.ShapeDtypeStruct(s, F32) for s in acc_shapes],
        scratch_shapes=[pltpu.VMEM((8, E_A), F32)],
        compiler_params=_params("arbitrary"),
    )(dx1, x, c, u, v, ym, kb, kcv, kg, w_out, w_in4, ga, cw)


WEIGHTS = ("a_norm", "a_w_in", "a_conv", "a_w_out", "kv_norm", "w_dkv", "ckv_norm", "w_ukv", "b_norm", "b_w_in",
           "b_q_norm", "b_w_uq", "b_w_out", "final_norm")
SHARD_SHAPES = {
    "a_norm": (1, 256), "a_w_in": (1, 1024, 1024), "a_conv": (1, 3, 256), "a_w_out": (1, 256, 1024),
    "kv_norm": (1024,), "w_dkv": (256, 288), "ckv_norm": (256,), "w_ukv": (256, 256), "b_norm": (1, 1024),
    "b_w_in": (1, 256, 896), "b_q_norm": (1, 384), "b_w_uq": (1, 384, 192), "b_w_out": (1, 512, 256),
    "final_norm": (1024,),
}
MATS = ("a_w_in", "a_w_out", "w_dkv", "w_ukv", "b_w_in", "b_w_uq", "b_w_out")
SMALL = ("a_norm", "a_conv", "kv_norm", "ckv_norm", "b_norm", "b_q_norm", "final_norm")
SMALL_FULL = {"a_norm": 1024, "a_conv": 3072, "kv_norm": 1024, "ckv_norm": 256, "b_norm": 1024, "b_q_norm": 384,
              "final_norm": 1024}
SMALL_ROWS = 8
LOSS_SLOT = sum(SMALL_FULL.values())


def _mat2d(name, a):
    return a.reshape(SHARD_SHAPES[name][-2:])


def _prep_first(g_win, gsmall):
    sm = gsmall.reshape(N_CHIPS, -1)
    a_conv = jnp.transpose(sm[:, 256:1024].reshape(N_CHIPS, CONV_WIDTH, 256), (1, 0, 2)).reshape(CONV_WIDTH, -1)
    return {"w_in4": g_win, "ga": sm[:, :256].reshape(1, -1), "cw": jnp.pad(a_conv, ((0, 8 - CONV_WIDTH), (0, 0)))}


def _prep_rest(gath, w):
    def cols(a):
        return jnp.transpose(a, (1, 0, 2)).reshape(a.shape[1], -1)

    def pad_heads(a, width):
        a = a.reshape(a.shape[0], N_HEADS, width)
        return jnp.pad(a, ((0, 0), (0, 0), (0, HEAD_PAD - width))).reshape(a.shape[0], QK_PAD)

    row = lambda a: a.reshape(1, -1).astype(F32)
    w_dkv = gath["w_dkv"].reshape(D_MODEL, KV_RANK + QK_ROPE)
    w_ukv = cols(gath["w_ukv"]).reshape(KV_RANK, N_HEADS, 2, QK_NOPE)
    return {
        "w_out": gath["a_w_out"].reshape(E_A, D_MODEL),
        "w_dkv": jnp.concatenate([w_dkv[:, :KV_RANK], jnp.zeros((D_MODEL, ROPE_LO), BF16), w_dkv[:, KV_RANK:],
                                  jnp.zeros((D_MODEL, HEAD_PAD - ROPE_LO - QK_ROPE), BF16)], axis=1),
        "w_uk": pad_heads(w_ukv[:, :, 0, :].reshape(KV_RANK, N_HEADS * QK_NOPE), QK_NOPE),
        "w_uv": w_ukv[:, :, 1, :].reshape(KV_RANK, E_B),
        "w_bin": gath["b_w_in"].reshape(D_MODEL, Q_RANK + E_B),
        "w_uq": pad_heads(cols(gath["b_w_uq"]), QK_NOPE + QK_ROPE),
        "w_bout": cols(gath["b_w_out"]),
        "gk": row(w["kv_norm"]), "gc": row(w["ckv_norm"]), "gb": row(w["b_norm"]),
        "gq": row(w["b_q_norm"]), "gf": row(w["final_norm"]),
    }


TRANSPOSED = ("w_dkv", "b_w_uq")
GRAD_MATS = ("w_in4", "w_out", "w_dkv", "w_uk", "w_uv", "w_bin", "w_uq", "w_bout")
GRAD_KIND = {"w_in4": "lead", "w_out": "row", "w_dkv": "row", "w_uk": "col", "w_uv": "col", "w_bin": "row",
             "w_uq": "col", "w_bout": "col"}


def _local_step(x, positions, tgt, w):
    bsz, seq, _ = x.shape
    t = bsz * seq
    x2d = x.reshape(t, D_MODEL)
    small = jnp.concatenate([w["a_norm"].reshape(-1), w["a_conv"].reshape(-1)]).reshape(8, LANES)
    rc, rs1, rs2, g_win, gsmall = _rope_tables_gather(positions.reshape(t, 1), _mat2d(MATS[0], w[MATS[0]]), small)
    wk = _prep_first(g_win, gsmall)
    (c, u, v, ym, kb, kcv, kg), gathered = _conv_fwd(x2d, seq, wk["ga"], wk["w_in4"], wk["cw"],
                                           [_mat2d(n, w[n]) for n in MATS[1:]])
    wk.update(_prep_rest(dict(zip(MATS[1:], gathered)), w))
    x1, q, k, vv, gate, cq, ckv = _mid_fwd(x2d, ym, wk["w_out"], wk["gk"], wk["gb"], wk["w_dkv"], wk["gc"], wk["w_uk"],
                                          wk["w_uv"], wk["w_bin"], wk["gq"], wk["w_uq"], rc, rs1, rs2)
    o, lse = _attn_fwd(q, k, vv, seq)
    dx2, do, dgate, dd, loss, dgf, dw_bout = _head_fwd_bwd(o, gate, x1, tgt.reshape(t, D_MODEL), wk["w_bout"], wk["gf"])
    dq, dk, dv = _attn_bwd(q, k, vv, do, lse, dd, seq)
    dx1, dwq, dw_bin, dwk, dwv, dwd, dgq, dgc, dgk, dgb = _mid_bwd(
        dq, dk, dv, dgate, dx2, x1, cq, ckv, rc, rs1, rs2, wk["w_uq"], wk["w_bin"], wk["w_uk"], wk["w_uv"], wk["w_dkv"],
        wk["gq"], wk["gc"], wk["gk"], wk["gb"])
    dx, dw_in4, dw_out, dcw, dga = _conv_bwd(dx1, x2d, c, u, v, ym, kb, kcv, kg, seq, wk["w_out"], wk["w_in4"], wk["ga"], wk["cw"])
    mats = {"w_in4": dw_in4, "w_out": dw_out, "w_dkv": dwd, "w_uk": dwk, "w_uv": dwv, "w_bin": dw_bin, "w_uq": dwq,
            "w_bout": dw_bout}
    small = {"a_norm": dga, "a_conv": dcw[:CONV_WIDTH], "kv_norm": dgk, "ckv_norm": dgc, "b_norm": dgb,
             "b_q_norm": dgq, "final_norm": dgf}
    return loss[0, 0], dx.reshape(bsz, seq, D_MODEL), mats, small


def _shard_grads(sh, svec):
    j0 = 2 * lax.axis_index("x") + lax.axis_index("y")
    flat = svec.reshape(-1)
    off, small = 0, {}
    for n in SMALL:
        small[n] = flat[off:off + SMALL_FULL[n]]
        off += SMALL_FULL[n]
    dwd, dwk, dwv, dwq = sh["w_dkv"], sh["w_uk"], sh["w_uv"], sh["w_uq"]
    w_ukv = jnp.stack([dwk.reshape(KV_RANK, 2, HEAD_PAD)[:, :, :QK_NOPE], dwv.reshape(KV_RANK, 2, V_HEAD)], axis=2)
    return {
        "a_norm": lax.dynamic_slice(small["a_norm"], (j0 * 256,), (256,)),
        "a_conv": lax.dynamic_slice(small["a_conv"].reshape(CONV_WIDTH, E_A), (0, j0 * 256), (CONV_WIDTH, 256)),
        "kv_norm": small["kv_norm"], "ckv_norm": small["ckv_norm"], "b_norm": small["b_norm"],
        "b_q_norm": small["b_q_norm"], "final_norm": small["final_norm"],
        "a_w_in": sh["w_in4"], "a_w_out": sh["w_out"],
        "w_dkv": jnp.concatenate([dwd[:, :KV_RANK], dwd[:, KV_RANK + ROPE_LO:KV_RANK + ROPE_LO + QK_ROPE]], axis=1),
        "w_ukv": w_ukv.reshape(KV_RANK, 2 * (QK_NOPE + V_HEAD)),
        "b_w_in": sh["w_bin"],
        "b_w_uq": dwq.reshape(Q_RANK, 2, HEAD_PAD)[:, :, :QK_NOPE + QK_ROPE].reshape(Q_RANK, -1),
        "b_w_out": sh["w_bout"],
    }


def _sub(ref, kind, j, cc):
    if kind == "lead":
        h = ref.shape[1] // 2
        return ref.at[j, pl.ds(pl.multiple_of(cc * h, 8), h), :]
    if kind == "row":
        rows = ref.shape[0] // N_CHIPS
        h = rows // 2
        return ref.at[pl.ds(pl.multiple_of(j * rows + cc * h, 8), h), :]
    cols = ref.shape[1] // N_CHIPS
    h = ref.shape[0] // 2
    return ref.at[pl.ds(pl.multiple_of(cc * h, 8), h), pl.ds(j * cols, cols)]


def _sub_shape(shape, kind):
    if kind == "lead":
        return (shape[1] // 2, shape[2])
    if kind == "row":
        return (shape[0] // N_CHIPS // 2, shape[1])
    return (shape[0] // 2, shape[1] // N_CHIPS)


def _reduce_grads(grads, kinds, vec):
    n = len(grads)
    shapes = [_sub_shape(a.shape, kd) for a, kd in zip(grads, kinds)]
    units = [(k, j) for k in range(n) for j in range(N_CHIPS)]
    big = (max(s[0] for s in shapes), max(s[1] for s in shapes))

    def body(*refs):
        g, v_ref = refs[:n], refs[n]
        out, o_ref = refs[n + 1:2 * n + 1], refs[2 * n + 1]
        theirs, part, recd, red = (refs[(2 + i) * n + 2:(3 + i) * n + 2] for i in range(4))
        mine, got = refs[6 * n + 2], refs[6 * n + 3]
        send1, recv1, send3, recv3, send5, recv5, load, local, send_v, recv_v = refs[6 * n + 4:]
        x, y, c = _place()
        j0 = 2 * x + y
        me = 2 * j0 + c
        sibling = (x, y, 1 - c)

        got[me] = v_ref[...]
        first, small_in = [], []
        for d in range(1, N_DEV):
            px, py, pc = x ^ (d >> 2), y ^ ((d >> 1) & 1), c ^ (d & 1)
            first.append(_remote(v_ref, got.at[me], send_v.at[d - 1], recv_v.at[d - 1], (px, py, pc)))
            small_in.append(_remote(v_ref, got.at[4 * px + 2 * py + pc], send_v.at[d - 1], recv_v.at[d - 1],
                                    (px, py, pc)))
        halves = [_remote(_sub(g[k], kinds[k], j, 1 - c), theirs[k].at[j], send1.at[u], recv1.at[u], sibling)
                  for u, (k, j) in enumerate(units)]
        for cp in first + halves:
            cp.start()

        def mine_load(u):
            k, j = units[u]
            h, cols = shapes[k]
            return pltpu.make_async_copy(_sub(g[k], kinds[k], j, c), mine.at[u % 2, pl.ds(0, h), pl.ds(0, cols)],
                                         load.at[u % 2])

        mine_load(0).start()
        for u, (k, j) in enumerate(units):
            h, cols = shapes[k]
            if u + 1 < len(units):
                mine_load(u + 1).start()
            mine_load(u).wait()
            halves[u].wait_recv()
            part[k][j] = (mine[u % 2, 0:h, 0:cols] + theirs[k][j]).astype(BF16)
            to_owner = _remote(part[k].at[j], recd[k].at[j0], send3.at[u], recv3.at[4 * k + j0], (j // 2, j % 2, c))

            @pl.when(j != j0)
            def _():
                to_owner.start()

            @pl.when(j == j0)
            def _():
                recd[k][j] = part[k][j]

        swaps = []
        for k in range(n):
            for j in range(N_CHIPS):
                arrived = _remote(part[k].at[j], recd[k].at[j], send3.at[4 * k + j], recv3.at[4 * k + j],
                                  (j // 2, j % 2, c))

                @pl.when(j != j0)
                def _():
                    arrived.wait_recv()

            r = recd[k]
            red[k][...] = ((r[0].astype(F32) + r[1].astype(F32)) + r[2].astype(F32)) + r[3].astype(F32)
            own = pltpu.make_async_copy(red[k], out[k].at[c], local.at[k])
            give = _remote(red[k], out[k].at[c], send5.at[k], recv5.at[k], sibling)
            take = _remote(red[k], out[k].at[1 - c], send5.at[k], recv5.at[k], sibling)
            own.start()
            give.start()
            swaps.append((own, give, take))

        for cp in small_in:
            cp.wait_recv()
            cp.wait_send()
        acc = got[0]
        for d in range(1, N_DEV):
            acc = acc + got[d]
        o_ref[...] = acc
        for u, (k, j) in enumerate(units):
            halves[u].wait_send()
            sent = _remote(part[k].at[j], recd[k].at[j0], send3.at[u], recv3.at[u], (j // 2, j % 2, c))

            @pl.when(j != j0)
            def _():
                sent.wait_send()
        for own, give, take in swaps:
            take.wait_recv()
            give.wait_send()
            own.wait()

    sems = [pltpu.SemaphoreType.DMA((len(units),))] * 4 + [pltpu.SemaphoreType.DMA((n,))] * 2 \
        + [pltpu.SemaphoreType.DMA((2,)), pltpu.SemaphoreType.DMA((n,))] + [pltpu.SemaphoreType.DMA((N_DEV - 1,))] * 2
    res = pl.pallas_call(
        body, name="reduce_grads",
        in_specs=[HBM] * n + [WHOLE], out_specs=[HBM] * n + [WHOLE],
        out_shape=[jax.ShapeDtypeStruct((2,) + s, F32) for s in shapes] + [jax.ShapeDtypeStruct(vec.shape, vec.dtype)],
        scratch_shapes=[pltpu.VMEM((N_CHIPS,) + s, F32) for s in shapes]
        + [pltpu.VMEM((N_CHIPS,) + s, BF16) for s in shapes] * 2
        + [pltpu.VMEM(s, F32) for s in shapes]
        + [pltpu.VMEM((2,) + big, F32), pltpu.VMEM((N_DEV,) + vec.shape, vec.dtype)] + sems,
        compiler_params=_comm_params(),
    )(*grads, vec)
    return res[:n], res[n]


def _adamw_math(w, g, m, v):
    m = ADAM_B1 * m + (1.0 - ADAM_B1) * g
    v = ADAM_B2 * v + (1.0 - ADAM_B2) * (g * g)
    m_hat = m / (1.0 - ADAM_B1 ** ADAM_STEP)
    v_hat = v / (1.0 - ADAM_B2 ** ADAM_STEP)
    return -ADAM_LR * (m_hat / (jnp.sqrt(v_hat) + ADAM_EPS) + ADAM_WD * w), m, v


def _adamw_tiled(w, g, m, v):
    rows, width = w.shape
    tm = rows // 4

    def body(w_ref, g_ref, m_ref, v_ref, go_ref, d_ref, mo_ref, vo_ref):
        g = g_ref[...]
        go_ref[...] = g
        d_ref[...], mo_ref[...], vo_ref[...] = _adamw_math(w_ref[...], g, m_ref[...], v_ref[...])

    spec = pl.BlockSpec((tm, width), lambda i: (i, 0))
    out = jax.ShapeDtypeStruct((rows, width), F32)
    return pl.pallas_call(
        body, grid=(rows // tm,), name="adamw_tiled",
        in_specs=[spec] * 4, out_specs=[spec] * 4, out_shape=[out] * 4,
        compiler_params=_params("parallel"),
    )(w, g, m, v)


def _adamw_many(ws, gs, ms, vs):
    n = len(ws)

    def body(*refs):
        for k in range(n):
            w_ref, g_ref, m_ref, v_ref = (refs[i * n + k] for i in range(4))
            go_ref, d_ref, mo_ref, vo_ref = (refs[(4 + i) * n + k] for i in range(4))
            g = g_ref[...]
            go_ref[...] = g
            d_ref[...], mo_ref[...], vo_ref[...] = _adamw_math(w_ref[...], g, m_ref[...], v_ref[...])

    outs = [jax.ShapeDtypeStruct(a.shape, F32) for a in ws]
    res = pl.pallas_call(
        body, name="adamw_many",
        in_specs=[WHOLE] * (4 * n), out_specs=[WHOLE] * (4 * n), out_shape=outs * 4,
        compiler_params=_comm_params(),
    )(*ws, *gs, *ms, *vs)
    return res[:n], res[n:2 * n], res[2 * n:3 * n], res[3 * n:]


def kernel(x, positions, a_norm, a_w_in, a_conv, a_w_out, kv_norm, w_dkv, ckv_norm, w_ukv, b_norm, b_w_in, b_q_norm, b_w_uq, b_w_out, final_norm, loss_target, m_a_norm, m_a_w_in, m_a_conv, m_a_w_out, m_kv_norm, m_w_dkv, m_ckv_norm, m_w_ukv, m_b_norm, m_b_w_in, m_b_q_norm, m_b_w_uq, m_b_w_out, m_final_norm, v_a_norm, v_a_w_in, v_a_conv, v_a_w_out, v_kv_norm, v_w_dkv, v_ckv_norm, v_w_ukv, v_b_norm, v_b_w_in, v_b_q_norm, v_b_w_uq, v_b_w_out, v_final_norm):
    w = dict(a_norm=a_norm, a_w_in=a_w_in, a_conv=a_conv, a_w_out=a_w_out, kv_norm=kv_norm, w_dkv=w_dkv,
             ckv_norm=ckv_norm, w_ukv=w_ukv, b_norm=b_norm, b_w_in=b_w_in, b_q_norm=b_q_norm, b_w_uq=b_w_uq,
             b_w_out=b_w_out, final_norm=final_norm)
    m = dict(a_norm=m_a_norm, a_w_in=m_a_w_in, a_conv=m_a_conv, a_w_out=m_a_w_out, kv_norm=m_kv_norm, w_dkv=m_w_dkv,
             ckv_norm=m_ckv_norm, w_ukv=m_w_ukv, b_norm=m_b_norm, b_w_in=m_b_w_in, b_q_norm=m_b_q_norm,
             b_w_uq=m_b_w_uq, b_w_out=m_b_w_out, final_norm=m_final_norm)
    v = dict(a_norm=v_a_norm, a_w_in=v_a_w_in, a_conv=v_a_conv, a_w_out=v_a_w_out, kv_norm=v_kv_norm, w_dkv=v_w_dkv,
             ckv_norm=v_ckv_norm, w_ukv=v_w_ukv, b_norm=v_b_norm, b_w_in=v_b_w_in, b_q_norm=v_b_q_norm,
             b_w_uq=v_b_w_uq, b_w_out=v_b_w_out, final_norm=v_final_norm)

    loss, dx, gmat, gsmall = _local_step(x, positions, loss_target, w)

    kinds = [GRAD_KIND[n] for n in GRAD_MATS]
    grads = [gmat[n] for n in GRAD_MATS]
    flat = jnp.concatenate([gsmall[n].reshape(-1) for n in SMALL] + [loss.reshape(1)])
    flat = jnp.pad(flat, (0, SMALL_ROWS * PACK_W - flat.shape[0])).reshape(SMALL_ROWS, PACK_W)
    mine, svec = _reduce_grads(grads, kinds, flat)
    loss = svec.reshape(-1)[LOSS_SLOT]
    g = _shard_grads({n: a.reshape(-1, a.shape[-1]) for n, a in zip(GRAD_MATS, mine)}, svec)

    def two_d(n, a):
        a = a.reshape(-1, a.shape[-1])
        return a.T if n in TRANSPOSED else a

    def back(n, a):
        return (a.T if n in TRANSPOSED else a).reshape(SHARD_SHAPES[n])

    big = "a_w_in"
    rest = [n for n in WEIGHTS if n != big]
    res_big = _adamw_tiled(*(two_d(big, t[big]) for t in (w, g, m, v)))
    res_rest = _adamw_many(*([two_d(n, t[n]) for n in rest] for t in (w, g, m, v)))
    out = {kind: dict(zip(rest, res_rest[i])) for i, kind in enumerate("gdmv")}
    for i, kind in enumerate("gdmv"):
        out[kind][big] = res_big[i]
    return (loss, dx) + tuple(back(n, out[kind][n]) for kind in "gdmv" for n in WEIGHTS)
```

```python
import functools
import math

import numpy as np
import jax
import jax.numpy as jnp
from jax import lax
from jax.experimental import pallas as pl
from jax.experimental.pallas import tpu as pltpu

F32 = jnp.float32
BF16 = jnp.bfloat16

D_MODEL = 1024
E_A = 1024
CONV_WIDTH = 3
N_HEADS = 8
QK_NOPE = 64
QK_ROPE = 32
V_HEAD = 64
KV_RANK = 256
Q_RANK = 384
E_B = N_HEADS * V_HEAD
ROPE_THETA = 10000.0
SOFTMAX_SCALE = 1.0 / math.sqrt(QK_NOPE + QK_ROPE)
LOG2E = math.log2(math.e)
LN2 = math.log(2.0)
Q_PRESCALE = SOFTMAX_SCALE * LOG2E
EPS = 1e-6
HEAD_PAD = 128
QK_PAD = N_HEADS * HEAD_PAD
ROPE_LO = QK_NOPE
ROPE_HALF = QK_ROPE // 2
KR_PAD = KV_RANK + HEAD_PAD

ADAM_LR = 0.001
ADAM_B1 = 0.9
ADAM_B2 = 0.999
ADAM_EPS = 1e-08
ADAM_WD = 0.01
ADAM_STEP = 10

VMEM_LIMIT = 56 * 1024 * 1024
ROW_TILE = 512
CONV_BWD_TILE = 256
ATT_TILE_FWD = 1024
ATT_TILE = 512
HEAD_CHAINS = 2
LANES = 128
PACK_W = 1024

N_CHIPS = 4
N_DEV = 8


def _dot(a, b):
    return jnp.dot(a, b, preferred_element_type=F32)


def _dot_nt(a, b):
    return lax.dot_general(a, b, (((1,), (1,)), ((), ())), preferred_element_type=F32)


def _dot_tn(a, b):
    return lax.dot_general(a, b, (((0,), (0,)), ((), ())), preferred_element_type=F32)


def _rms(x):
    r = lax.rsqrt(jnp.mean(x * x, axis=-1, keepdims=True) + EPS)
    return r, x * r


def _rms_bwd(dxh, xh, r):
    return r * (dxh - xh * jnp.mean(dxh * xh, axis=-1, keepdims=True))


def _rope_fwd(a, c, s1, s2):
    return a * c + pltpu.roll(a, HEAD_PAD - ROPE_HALF, 1) * s1 + pltpu.roll(a, ROPE_HALF, 1) * s2


def _rope_bwd(g, c, s1, s2):
    return g * c + pltpu.roll(g * s1, ROPE_HALF, 1) + pltpu.roll(g * s2, HEAD_PAD - ROPE_HALF, 1)


def _sigmoid(x):
    return 1.0 / (1.0 + jnp.exp(-x))


def _row_spec(tm, n):
    return pl.BlockSpec((tm, n), lambda i: (i, 0))


def _const_spec(shape):
    nd = len(shape)
    return pl.BlockSpec(shape, lambda i: (0,) * nd, pipeline_mode=pl.Buffered(1))


def _acc_spec(shape):
    nd = len(shape)
    return pl.BlockSpec(shape, lambda i: (0,) * nd, pipeline_mode=pl.Buffered(1))


def _params(*sem):
    return pltpu.CompilerParams(dimension_semantics=sem, vmem_limit_bytes=VMEM_LIMIT)


MESH = pl.DeviceIdType.MESH
HBM = pl.BlockSpec(memory_space=pl.ANY)
WHOLE = pl.BlockSpec(memory_space=pltpu.VMEM)
FLIPS = ((1, 0), (0, 1), (1, 1))


def _place():
    return lax.axis_index("x"), lax.axis_index("y"), lax.axis_index("c")


def _remote(src, dst, send, recv, peer):
    return pltpu.make_async_remote_copy(src_ref=src, dst_ref=dst, send_sem=send, recv_sem=recv, device_id=peer,
                                        device_id_type=MESH)


def _comm_params():
    return pltpu.CompilerParams(vmem_limit_bytes=VMEM_LIMIT)


def _rope_consts():
    lane = np.arange(HEAD_PAD)
    first = (lane >= ROPE_LO) & (lane < ROPE_LO + ROPE_HALF)
    second = (lane >= ROPE_LO + ROPE_HALF) & (lane < ROPE_LO + QK_ROPE)
    f = np.where(first, lane - ROPE_LO, np.where(second, lane - ROPE_LO - ROPE_HALF, 0))
    inv = np.float32(ROPE_THETA) ** (-(2 * f).astype(np.float32) / np.float32(QK_ROPE))
    out = np.zeros((8, HEAD_PAD), np.float32)
    out[0] = inv
    out[1] = first
    out[2] = second
    out[3] = lane < ROPE_LO
    return jnp.asarray(out)


def _gather_plan(w, sm, outs, osm, bf, sems):
    n = len(w)
    send_i, recv_i, send_d, recv_d, send_s, recv_s, local = sems
    x, y, c = _place()
    j0 = 2 * x + y
    sibling = (x, y, 1 - c)
    own = [pltpu.make_async_copy(bf[k], outs[k].at[j0], local.at[k]) for k in range(n)]
    if sm is not None:
        own.append(pltpu.make_async_copy(sm, osm.at[j0], local.at[n]))

    def half(k, cc):
        h = w[k].shape[0] // 2
        return pl.ds(pl.multiple_of(cc * h, 16), h)

    sends, arrivals, forwards, fwaits = [], [], [], []
    for i, (fx, fy) in enumerate(FLIPS):
        px, py = x ^ fx, y ^ fy
        pj = 2 * px + py
        for k in range(n):
            s = i * n + k
            sends.append(_remote(bf[k].at[half(k, c)], outs[k].at[j0, half(k, c)], send_i.at[s], recv_i.at[s],
                                 (px, py, c)))
            arrivals.append(_remote(bf[k].at[half(k, c)], outs[k].at[pj, half(k, c)], send_i.at[s], recv_i.at[s],
                                    (px, py, c)))
            forwards.append(_remote(outs[k].at[pj, half(k, c)], outs[k].at[pj, half(k, c)], send_d.at[s],
                                    recv_d.at[s], sibling))
            fwaits.append(_remote(outs[k].at[pj, half(k, 1 - c)], outs[k].at[pj, half(k, 1 - c)], send_d.at[s],
                                  recv_d.at[s], sibling))
        if sm is not None:
            sends.append(_remote(sm, osm.at[j0], send_s.at[i], recv_s.at[i], (px, py, c)))
            fwaits.append(_remote(sm, osm.at[pj], send_s.at[i], recv_s.at[i], (px, py, c)))
    return own, sends, arrivals, forwards, fwaits


def _gather_begin(w, bf, plan):
    own, sends, _, _, _ = plan
    for k in range(len(w)):
        bf[k][...] = w[k][...].astype(BF16)
    for cp in own + sends:
        cp.start()


def _gather_pass_on(plan):
    _, _, arrivals, forwards, _ = plan
    for cp, fwd in zip(arrivals, forwards):
        cp.wait_recv()
        fwd.start()


def _gather_end(plan):
    own, sends, _, forwards, fwaits = plan
    for cp in fwaits:
        cp.wait_recv()
    for cp in sends + forwards:
        cp.wait_send()
    for cp in own:
        cp.wait()


def _gather_sems(n, with_small):
    return ([pltpu.SemaphoreType.DMA((3 * n,))] * 4 + [pltpu.SemaphoreType.DMA((3,))] * 2
            + [pltpu.SemaphoreType.DMA((n + (1 if with_small else 0),))])


def _rope_tables_gather(pos_col, w_in, small):
    t = pos_col.shape[0]
    tm = min(ROW_TILE, t)
    steps = t // tm

    def body(p_ref, k_ref, w_ref, sm_ref, c_ref, s1_ref, s2_ref, out_ref, osm_ref, bf_ref, *sems):
        plan = _gather_plan([w_ref], sm_ref, [out_ref], osm_ref, [bf_ref], sems)
        i = pl.program_id(0)

        @pl.when(i == 0)
        def _():
            _gather_begin([w_ref], [bf_ref], plan)

        inv, first, second, nope = k_ref[0:1, :], k_ref[1:2, :], k_ref[2:3, :], k_ref[3:4, :]
        ang = p_ref[...].astype(F32) * inv
        cs, sn = jnp.cos(ang), jnp.sin(ang)
        c_ref[...] = cs * (first + second) + nope
        s1_ref[...] = -sn * first
        s2_ref[...] = sn * second

        @pl.when(i == steps - 1)
        def _():
            _gather_pass_on(plan)
            _gather_end(plan)

    out = jax.ShapeDtypeStruct((t, HEAD_PAD), F32)
    return pl.pallas_call(
        body, grid=(steps,), name="rope_tables_gather",
        in_specs=[_row_spec(tm, 1), _const_spec((8, HEAD_PAD)), WHOLE, WHOLE],
        out_specs=[_row_spec(tm, HEAD_PAD)] * 3 + [HBM, HBM],
        out_shape=[out] * 3 + [jax.ShapeDtypeStruct((N_CHIPS,) + w_in.shape, BF16),
                               jax.ShapeDtypeStruct((N_CHIPS,) + small.shape, small.dtype)],
        scratch_shapes=[pltpu.VMEM(w_in.shape, BF16)] + _gather_sems(1, True),
        compiler_params=_params("arbitrary"),
    )(pos_col, _rope_consts(), w_in, small)


def _shift_down(v, prev, row):
    p1, p2 = prev[7:8, :], prev[6:7, :]
    v1 = jnp.where(row == 0, p1, pltpu.roll(v, 1, 0))
    v2 = jnp.where(row == 0, p2, jnp.where(row == 1, p1, pltpu.roll(v, 2, 0)))
    return v1, v2


def _conv_fwd(x, seq, ga, w_in4, cw, shards):
    t = x.shape[0]
    tm = min(ROW_TILE, seq)
    tiles_per_seq = seq // tm
    steps = t // tm
    n = len(shards)

    def body(x_ref, ga_ref, w_ref, cw_ref, *rest):
        sh, (c_ref, u_ref, v_ref, ym_ref, kb_ref, kcv_ref, kg_ref) = rest[:n], rest[n:n + 7]
        outs, carry_ref, bf, sems = rest[n + 7:2 * n + 7], rest[2 * n + 7], rest[2 * n + 8:3 * n + 8], rest[3 * n + 8:]
        plan = _gather_plan(sh, None, outs, None, bf, sems)
        i = pl.program_id(0)

        @pl.when(i == 0)
        def _():
            _gather_begin(sh, bf, plan)

        @pl.when(i % tiles_per_seq == 0)
        def _():
            carry_ref[...] = jnp.zeros_like(carry_ref)

        _, xh = _rms(x_ref[...])
        h = (xh * ga_ref[...]).astype(BF16)
        c, u = _dot(h, w_ref[1]), _dot(h, w_ref[2])
        g, b = _dot(h, w_ref[3]), _dot(h, w_ref[0])
        v = c * u
        row = lax.broadcasted_iota(jnp.int32, (tm, 1), 0)
        v1, v2 = _shift_down(v, carry_ref[...], row)
        carry_ref[...] = v[tm - 8:tm, :]
        cv = cw_ref[2:3, :] * v + cw_ref[1:2, :] * v1 + cw_ref[0:1, :] * v2
        sg = _sigmoid(g)
        silu = g * sg
        c_ref[...] = c.astype(BF16)
        u_ref[...] = u.astype(BF16)
        v_ref[...] = v.astype(BF16)
        ym_ref[...] = (silu * b * cv).astype(BF16)
        kb_ref[...] = (silu * cv).astype(BF16)
        kcv_ref[...] = (silu * b).astype(BF16)
        kg_ref[...] = (b * cv * (sg * (1.0 + g * (1.0 - sg)))).astype(BF16)

        @pl.when(i == max(steps - 2, 0))
        def _():
            _gather_pass_on(plan)

        @pl.when(i == steps - 1)
        def _():
            _gather_end(plan)

    out = jax.ShapeDtypeStruct((t, E_A), BF16)
    res = pl.pallas_call(
        body, grid=(steps,), name="conv_fwd",
        in_specs=[_row_spec(tm, D_MODEL), _const_spec((1, D_MODEL)), _const_spec((4, D_MODEL, E_A)),
                  _const_spec((8, E_A))] + [WHOLE] * n,
        out_specs=[_row_spec(tm, E_A)] * 7 + [HBM] * n,
        out_shape=[out] * 7 + [jax.ShapeDtypeStruct((N_CHIPS,) + a.shape, BF16) for a in shards],
        scratch_shapes=[pltpu.VMEM((8, E_A), F32)] + [pltpu.VMEM(a.shape, BF16) for a in shards]
        + _gather_sems(n, False),
        compiler_params=_params("arbitrary"),
    )(x, ga, w_in4, cw, *shards)
    return res[:7], res[7:]


def _mid_fwd(x, ym, w_out, gk, gb, w_dkv, gc, w_uk, w_uv, w_bin, gq, w_uq, rc, rs1, rs2):
    t = x.shape[0]
    tm = min(2 * ROW_TILE, t)

    def body(x_ref, ym_ref, wo_ref, gk_ref, gb_ref, wd_ref, gc_ref, wk_ref, wv_ref, wi_ref, gq_ref, wq_ref,
             c_ref, s1_ref, s2_ref, x1_ref, q_ref, k_ref, v_ref, gate_ref, cq_ref, ckv_ref):
        cb, s1b, s2b = c_ref[...], s1_ref[...], s2_ref[...]
        x1 = x_ref[...] + _dot(ym_ref[...], wo_ref[...])
        x1_ref[...] = x1
        _, xh = _rms(x1)
        hk = (xh * gk_ref[...]).astype(BF16)
        h1 = (xh * gb_ref[...]).astype(BF16)

        pb = _dot(h1, wi_ref[...])
        cq = pb[:, :Q_RANK]
        cq_ref[...] = cq.astype(BF16)
        gate_ref[...] = pb[:, Q_RANK:].astype(BF16)
        _, cqh = _rms(cq)
        q = _dot((cqh * gq_ref[...]).astype(BF16), wq_ref[...])
        for h in range(N_HEADS):
            sl = slice(h * HEAD_PAD, (h + 1) * HEAD_PAD)
            q_ref[:, sl] = (_rope_fwd(q[:, sl], cb, s1b, s2b) * Q_PRESCALE).astype(BF16)

        ckr = _dot(hk, wd_ref[...])
        ckv_raw = ckr[:, :KV_RANK]
        ckv_ref[...] = ckv_raw.astype(BF16)
        _, ch = _rms(ckv_raw)
        ckv = (ch * gc_ref[...]).astype(BF16)
        kr = _rope_fwd(ckr[:, KV_RANK:], cb, s1b, s2b)
        kn = _dot(ckv, wk_ref[...])
        for h in range(N_HEADS):
            sl = slice(h * HEAD_PAD, (h + 1) * HEAD_PAD)
            k_ref[:, sl] = (kn[:, sl] + kr).astype(BF16)
        v_ref[...] = _dot(ckv, wv_ref[...]).astype(BF16)

    def sds(n, dt):
        return jax.ShapeDtypeStruct((t, n), dt)

    return pl.pallas_call(
        body, grid=(t // tm,), name="mid_fwd",
        in_specs=[_row_spec(tm, D_MODEL), _row_spec(tm, E_A), _const_spec((E_A, D_MODEL)),
                  _const_spec((1, D_MODEL)), _const_spec((1, D_MODEL)), _const_spec((D_MODEL, KR_PAD)),
                  _const_spec((1, KV_RANK)), _const_spec((KV_RANK, QK_PAD)), _const_spec((KV_RANK, E_B)),
                  _const_spec((D_MODEL, Q_RANK + E_B)), _const_spec((1, Q_RANK)), _const_spec((Q_RANK, QK_PAD)),
                  _row_spec(tm, HEAD_PAD), _row_spec(tm, HEAD_PAD), _row_spec(tm, HEAD_PAD)],
        out_specs=[_row_spec(tm, D_MODEL), _row_spec(tm, QK_PAD), _row_spec(tm, QK_PAD), _row_spec(tm, E_B),
                   _row_spec(tm, E_B), _row_spec(tm, Q_RANK), _row_spec(tm, KV_RANK)],
        out_shape=[sds(D_MODEL, F32), sds(QK_PAD, BF16), sds(QK_PAD, BF16), sds(E_B, BF16), sds(E_B, BF16),
                   sds(Q_RANK, BF16), sds(KV_RANK, BF16)],
        compiler_params=_params("parallel"),
    )(x, ym, w_out, gk, gb, w_dkv, gc, w_uk, w_uv, w_bin, gq, w_uq, rc, rs1, rs2)


def _pair_specs(seq):
    qk = pl.BlockSpec((seq, 2 * HEAD_PAD), lambda b, p: (b, p))
    vo = pl.BlockSpec((seq, 2 * V_HEAD), lambda b, p: (b, p))
    st = pl.BlockSpec((None, 2, seq), lambda b, p: (p, 0, b))
    return qk, vo, st


def _attn_fwd(q, k, v, seq):
    t = q.shape[0]
    tq = min(ATT_TILE_FWD, seq)
    nq = seq // tq

    def body(q_ref, k_ref, v_ref, o_ref, lse_ref, m_scr, l_scr, acc_scr):
        lane = lax.broadcasted_iota(jnp.int32, (tq, 2 * V_HEAD), 1)

        def q_step(qi, _):
            q0 = pl.multiple_of(qi * tq, tq)
            m_scr[...] = jnp.full(m_scr.shape, -jnp.inf, F32)
            l_scr[...] = jnp.zeros_like(l_scr)
            acc_scr[...] = jnp.zeros_like(acc_scr)

            def block(q_lo, q_n, k0, k_n, masked):
                rows = slice(q_lo, q_lo + q_n)
                vt = v_ref[pl.ds(k0, k_n), :]

                def score(hh):
                    hs = slice(hh * HEAD_PAD, (hh + 1) * HEAD_PAD)
                    return _dot_nt(q_ref[pl.ds(q0 + q_lo, q_n), hs], k_ref[pl.ds(k0, k_n), hs])

                early = [score(hh) for hh in range(2)] if masked else None
                for hh in range(2):
                    s = early[hh] if masked else score(hh)
                    if masked:
                        row = lax.broadcasted_iota(jnp.int32, (q_n, k_n), 0)
                        col = lax.broadcasted_iota(jnp.int32, (q_n, k_n), 1)
                        s = jnp.where(col <= row, s, -jnp.inf)
                    m_old = m_scr[hh, rows]
                    m_new = jnp.maximum(m_old, jnp.max(s, axis=-1, keepdims=True))
                    alpha = jnp.exp2(m_old - m_new)
                    ps = [jnp.exp2(s[:, j * LANES:(j + 1) * LANES] - m_new) for j in range(k_n // LANES)]
                    l_scr[hh, rows] = alpha * l_scr[hh, rows] + functools.reduce(lambda a, b: a + b, ps)
                    p = jnp.concatenate(ps, axis=-1).astype(BF16)
                    acc_scr[hh, rows] = alpha * acc_scr[hh, rows] + _dot(p, vt)
                    m_scr[hh, rows] = m_new

            def k_step(ki, _):
                block(0, tq, pl.multiple_of(ki * tq, tq), tq, False)
                return 0

            lax.fori_loop(0, qi, k_step, 0)
            half = tq // 2
            block(0, tq, q0, half, True)
            block(half, half, q0 + half, half, True)
            l0 = jnp.sum(l_scr[0], axis=-1, keepdims=True)
            l1 = jnp.sum(l_scr[1], axis=-1, keepdims=True)
            o_ref[pl.ds(q0, tq), :] = jnp.where(lane < V_HEAD, acc_scr[0] / l0, acc_scr[1] / l1).astype(BF16)
            stats = jnp.where(lane == 0, m_scr[0] + jnp.log2(l0), m_scr[1] + jnp.log2(l1)).T
            lse_ref[:, pl.ds(q0, tq)] = stats[0:2, :]
            return 0

        lax.fori_loop(0, nq, q_step, 0)

    qk, vo, st = _pair_specs(seq)
    return pl.pallas_call(
        body, grid=(t // seq, N_HEADS // 2), name="attn_fwd",
        in_specs=[qk, qk, vo], out_specs=[vo, st],
        out_shape=[jax.ShapeDtypeStruct((t, E_B), BF16), jax.ShapeDtypeStruct((N_HEADS // 2, 2, t), F32)],
        scratch_shapes=[pltpu.VMEM((2, tq, LANES), F32), pltpu.VMEM((2, tq, LANES), F32),
                        pltpu.VMEM((2, tq, 2 * V_HEAD), F32)],
        compiler_params=_params("parallel", "parallel"),
    )(q, k, v)


def _head_fwd_bwd(o, gate, x1, tgt, w_bout, gf):
    t = o.shape[0]
    tm = min(2 * ROW_TILE, t)

    def body(o_ref, gate_ref, x1_ref, tgt_ref, w_ref, gf_ref,
             dx2_ref, do_ref, dgate_ref, dd_ref, loss_ref, dgf_ref, dw_ref):
        @pl.when(pl.program_id(0) == 0)
        def _():
            loss_ref[...] = jnp.zeros_like(loss_ref)
            dgf_ref[...] = jnp.zeros_like(dgf_ref)
            dw_ref[...] = jnp.zeros_like(dw_ref)

        hm = tm // HEAD_CHAINS
        gf = gf_ref[...]
        lane = lax.broadcasted_iota(jnp.int32, (hm, 2 * V_HEAD), 1)
        chains = []
        for ch in range(HEAD_CHAINS):
            rs = pl.ds(ch * hm, hm)
            o = o_ref[rs, :].astype(F32)
            gt = gate_ref[rs, :].astype(F32)
            sg = _sigmoid(gt)
            silu = gt * sg
            z = (o * silu).astype(BF16)
            chains.append((rs, o, gt, sg, silu, z, x1_ref[rs, :] + _dot(z, w_ref[...])))
        mids = []
        for rs, o, gt, sg, silu, z, x2 in chains:
            r2, xh2 = _rms(x2)
            err = xh2 * gf - tgt_ref[rs, :]
            loss_ref[...] += 0.5 * jnp.sum(jnp.mean(err * err, axis=-1, keepdims=True))
            dy = err * (1.0 / D_MODEL)
            dgf_ref[...] += jnp.sum(dy * xh2, axis=0, keepdims=True)
            dx2 = _rms_bwd(dy * gf, xh2, r2)
            dx2_ref[rs, :] = dx2
            dx2b = dx2.astype(BF16)
            mids.append(_dot_nt(dx2b, w_ref[...]))
            dw_ref[...] += _dot_tn(z, dx2b)
        for (rs, o, gt, sg, silu, z, x2), dz in zip(chains, mids):
            do = dz * silu
            do_ref[rs, :] = do.astype(BF16)
            dgate_ref[rs, :] = (dz * o * (sg * (1.0 + gt * (1.0 - sg)))).astype(BF16)
            prod = do * o
            cols = jnp.zeros((hm, LANES), F32)
            for p in range(N_HEADS // 2):
                blk = prod[:, p * 2 * V_HEAD:(p + 1) * 2 * V_HEAD]
                d0 = jnp.sum(jnp.where(lane < V_HEAD, blk, 0.0), axis=-1, keepdims=True)
                d1 = jnp.sum(jnp.where(lane < V_HEAD, 0.0, blk), axis=-1, keepdims=True)
                cols = jnp.where(lane == 2 * p, d0, jnp.where(lane == 2 * p + 1, d1, cols))
            rows = cols.T
            for h in range(N_HEADS):
                dd_ref[h // 2, h % 2:h % 2 + 1, rs] = rows[h:h + 1, :]

    return pl.pallas_call(
        body, grid=(t // tm,), name="head_fwd_bwd",
        in_specs=[_row_spec(tm, E_B), _row_spec(tm, E_B), _row_spec(tm, D_MODEL), _row_spec(tm, D_MODEL),
                  _const_spec((E_B, D_MODEL)), _const_spec((1, D_MODEL))],
        out_specs=[_row_spec(tm, D_MODEL), _row_spec(tm, E_B), _row_spec(tm, E_B),
                   pl.BlockSpec((N_HEADS // 2, 2, tm), lambda i: (0, 0, i)),
                   _acc_spec((1, 1)), _acc_spec((1, D_MODEL)), _acc_spec((E_B, D_MODEL))],
        out_shape=[jax.ShapeDtypeStruct((t, D_MODEL), F32), jax.ShapeDtypeStruct((t, E_B), BF16),
                   jax.ShapeDtypeStruct((t, E_B), BF16), jax.ShapeDtypeStruct((N_HEADS // 2, 2, t), F32),
                   jax.ShapeDtypeStruct((1, 1), F32), jax.ShapeDtypeStruct((1, D_MODEL), F32),
                   jax.ShapeDtypeStruct((E_B, D_MODEL), F32)],
        compiler_params=_params("arbitrary"),
    )(o, gate, x1, tgt, w_bout, gf)


def _attn_bwd(q, k, v, do, lse, dd, seq):
    t = q.shape[0]
    tq = min(ATT_TILE, seq)
    nq = seq // tq
    assert nq % 2 == 0, "full tiles are taken in pairs"

    def body(q_ref, k_ref, v_ref, do_ref, lse_ref, dd_ref, dq_ref, dk_ref, dv_ref, dq_acc, dk_acc, dv_acc):
        dq_acc[...] = jnp.zeros_like(dq_acc)

        def k_step(ki, _):
            k0 = pl.multiple_of(ki * tq, tq)
            dk_acc[...] = jnp.zeros_like(dk_acc)
            dv_acc[...] = jnp.zeros_like(dv_acc)

            def block(k_lo, k_n, q0, q_n, masked):
                rows = slice(k_lo, k_lo + k_n)
                lane = lax.broadcasted_iota(jnp.int32, (q_n, 2 * V_HEAD), 1)
                vt = v_ref[pl.ds(k0 + k_lo, k_n), :]
                do_pair = do_ref[pl.ds(q0, q_n), :]

                def operands(hh):
                    hs = slice(hh * HEAD_PAD, (hh + 1) * HEAD_PAD)
                    kt = k_ref[pl.ds(k0 + k_lo, k_n), hs]
                    qt = q_ref[pl.ds(q0, q_n), hs]
                    mine = (lane < V_HEAD) if hh == 0 else (lane >= V_HEAD)
                    do_h = jnp.where(mine, do_pair, jnp.zeros((), BF16))
                    return hs, kt, qt, do_h, _dot_nt(kt, qt), _dot_nt(vt, do_h)

                early = [operands(hh) for hh in range(2)] if masked else None
                for hh in range(2):
                    hs, kt, qt, do_h, st, dpt = early[hh] if masked else operands(hh)
                    if masked:
                        krow = lax.broadcasted_iota(jnp.int32, (k_n, q_n), 0)
                        qcol = lax.broadcasted_iota(jnp.int32, (k_n, q_n), 1)
                        st = jnp.where(krow <= qcol, st, -jnp.inf)
                    pt = jnp.exp2(st - lse_ref[hh:hh + 1, pl.ds(q0, q_n)])
                    dst = (pt * (dpt - dd_ref[hh:hh + 1, pl.ds(q0, q_n)])).astype(BF16)
                    dv_acc[rows, :] += _dot(pt.astype(BF16), do_h)
                    dk_acc[rows, hs] += _dot(dst, qt)
                    dq_acc[pl.ds(q0, q_n), hs] += _dot_tn(dst, kt)

            wide = 2 * tq

            def q_step(qj, _):
                block(0, tq, pl.multiple_of(qj * wide, wide), wide, False)
                return 0

            half = tq // 2
            block(0, half, k0, tq, True)
            block(half, half, pl.multiple_of(k0 + half, half), half, True)

            @pl.when(ki % 2 == 0)
            def _():
                block(0, tq, pl.multiple_of(k0 + tq, tq), tq, False)

            lax.fori_loop(ki // 2 + 1, nq // 2, q_step, 0)
            dk_ref[pl.ds(k0, tq), :] = (dk_acc[...] * LN2).astype(BF16)
            dv_ref[pl.ds(k0, tq), :] = dv_acc[...].astype(BF16)
            return 0

        lax.fori_loop(0, nq, k_step, 0)
        dq_ref[...] = (dq_acc[...] * SOFTMAX_SCALE).astype(BF16)

    qk, vo, st = _pair_specs(seq)
    return pl.pallas_call(
        body, grid=(t // seq, N_HEADS // 2), name="attn_bwd",
        in_specs=[qk, qk, vo, vo, st, st], out_specs=[qk, qk, vo],
        out_shape=[jax.ShapeDtypeStruct((t, QK_PAD), BF16), jax.ShapeDtypeStruct((t, QK_PAD), BF16),
                   jax.ShapeDtypeStruct((t, E_B), BF16)],
        scratch_shapes=[pltpu.VMEM((seq, 2 * HEAD_PAD), F32), pltpu.VMEM((tq, 2 * HEAD_PAD), F32),
                        pltpu.VMEM((tq, 2 * V_HEAD), F32)],
        compiler_params=_params("parallel", "parallel"),
    )(q, k, v, do, lse, dd)


def _mid_bwd(dq, dk, dv, dgate, dx2, x1, cq, ckv, rc, rs1, rs2, w_uq, w_bin, w_uk, w_uv, w_dkv, gq, gc, gk, gb):
    t = dq.shape[0]
    tm = min(ROW_TILE, t)

    def body(dq_ref, dk_ref, dv_ref, dgate_ref, dx2_ref, x1_ref, cq_ref, ckv_ref, c_ref, s1_ref, s2_ref,
             wq_ref, wi_ref, wk_ref, wv_ref, wd_ref, gq_ref, gc_ref, gk_ref, gb_ref,
             dx1_ref, dwq_ref, dwi_ref, dwk_ref, dwv_ref, dwd_ref, dgq_ref, dgc_ref, dgk_ref, dgb_ref):
        @pl.when(pl.program_id(0) == 0)
        def _():
            for ref in (dwq_ref, dwi_ref, dwk_ref, dwv_ref, dwd_ref, dgq_ref, dgc_ref, dgk_ref, dgb_ref):
                ref[...] = jnp.zeros_like(ref)

        cb, s1b, s2b = c_ref[...], s1_ref[...], s2_ref[...]
        gk, gb, gq, gc = gk_ref[...], gb_ref[...], gq_ref[...], gc_ref[...]

        dkb, dvb = dk_ref[...], dv_ref[...]
        dckv = _dot_nt(dkb, wk_ref[...]) + _dot_nt(dvb, wv_ref[...])
        rcv, ch = _rms(ckv_ref[...].astype(F32))
        ckvn = (ch * gc).astype(BF16)
        dwk_ref[...] += _dot_tn(ckvn, dkb)
        dwv_ref[...] += _dot_tn(ckvn, dvb)

        dqs = [_rope_bwd(dq_ref[:, h * HEAD_PAD:(h + 1) * HEAD_PAD].astype(F32), cb, s1b, s2b)
               for h in range(N_HEADS)]
        dqb = jnp.concatenate(dqs, axis=-1).astype(BF16)
        rq, cqh = _rms(cq_ref[...].astype(F32))
        dcqn = _dot_nt(dqb, wq_ref[...])
        dwq_ref[...] += _dot_tn((cqh * gq).astype(BF16), dqb)

        dgc_ref[...] += jnp.sum(dckv * ch, axis=0, keepdims=True)
        dckv_raw = _rms_bwd(dckv * gc, ch, rcv)
        dkr = dk_ref[:, 0:HEAD_PAD].astype(F32)
        for h in range(1, N_HEADS):
            dkr = dkr + dk_ref[:, h * HEAD_PAD:(h + 1) * HEAD_PAD].astype(F32)
        dkr = _rope_bwd(dkr, cb, s1b, s2b)
        dckr = jnp.concatenate([dckv_raw, dkr], axis=-1).astype(BF16)
        r1, xh = _rms(x1_ref[...])
        dhk = _dot_nt(dckr, wd_ref[...])
        dwd_ref[...] += _dot_tn((xh * gk).astype(BF16), dckr)

        dgq_ref[...] += jnp.sum(dcqn * cqh, axis=0, keepdims=True)
        dcq = _rms_bwd(dcqn * gq, cqh, rq)
        dpb = jnp.concatenate([dcq.astype(BF16), dgate_ref[...]], axis=-1)
        dh1 = _dot_nt(dpb, wi_ref[...])
        dwi_ref[...] += _dot_tn((xh * gb).astype(BF16), dpb)

        dgb_ref[...] += jnp.sum(dh1 * xh, axis=0, keepdims=True)
        dgk_ref[...] += jnp.sum(dhk * xh, axis=0, keepdims=True)
        dx1_ref[...] = dx2_ref[...] + _rms_bwd(dh1 * gb + dhk * gk, xh, r1)

    acc_shapes = [(Q_RANK, QK_PAD), (D_MODEL, Q_RANK + E_B), (KV_RANK, QK_PAD), (KV_RANK, E_B), (D_MODEL, KR_PAD),
                  (1, Q_RANK), (1, KV_RANK), (1, D_MODEL), (1, D_MODEL)]
    return pl.pallas_call(
        body, grid=(t // tm,), name="mid_bwd",
        in_specs=[_row_spec(tm, QK_PAD), _row_spec(tm, QK_PAD), _row_spec(tm, E_B), _row_spec(tm, E_B),
                  _row_spec(tm, D_MODEL), _row_spec(tm, D_MODEL), _row_spec(tm, Q_RANK), _row_spec(tm, KV_RANK),
                  _row_spec(tm, HEAD_PAD), _row_spec(tm, HEAD_PAD), _row_spec(tm, HEAD_PAD),
                  _const_spec((Q_RANK, QK_PAD)), _const_spec((D_MODEL, Q_RANK + E_B)),
                  _const_spec((KV_RANK, QK_PAD)), _const_spec((KV_RANK, E_B)), _const_spec((D_MODEL, KR_PAD)),
                  _const_spec((1, Q_RANK)), _const_spec((1, KV_RANK)), _const_spec((1, D_MODEL)),
                  _const_spec((1, D_MODEL))],
        out_specs=[_row_spec(tm, D_MODEL)] + [_acc_spec(s) for s in acc_shapes],
        out_shape=[jax.ShapeDtypeStruct((t, D_MODEL), F32)] + [jax.ShapeDtypeStruct(s, F32) for s in acc_shapes],
        compiler_params=_params("arbitrary"),
    )(dq, dk, dv, dgate, dx2, x1, cq, ckv, rc, rs1, rs2, w_uq, w_bin, w_uk, w_uv, w_dkv, gq, gc, gk, gb)


def _conv_bwd(dx1, x, c, u, v, ym, kb, kcv, kg, seq, w_out, w_in4, ga, cw):
    t = x.shape[0]
    tm = min(CONV_BWD_TILE, seq)
    tiles_per_seq = seq // tm
    n = t // tm

    def tile(i):
        return n - 1 - i

    def rev(width):
        return pl.BlockSpec((tm, width), lambda i: (tile(i), 0))

    def body(dx1_ref, x_ref, c_ref, u_ref, v_ref, ym_ref, kb_ref, kcv_ref, kg_ref, wo_ref, wi_ref, ga_ref, cw_ref,
             dx_ref, dwi_out, dwo_ref, dcw_ref, dga_ref, carry_ref, dwi_ref, dwi_sem):
        i = pl.program_id(0)
        j = tile(i)
        dwi_done = [pltpu.make_async_copy(dwi_ref.at[idx], dwi_out.at[idx], dwi_sem.at[idx]) for idx in range(4)]

        @pl.when(i == 0)
        def _():
            for ref in (dwi_ref, dwo_ref, dcw_ref, dga_ref):
                ref[...] = jnp.zeros_like(ref)

        @pl.when(j % tiles_per_seq == tiles_per_seq - 1)
        def _():
            carry_ref[...] = jnp.zeros_like(carry_ref)

        dx1 = dx1_ref[...]
        dx1b = dx1.astype(BF16)
        dym = _dot_nt(dx1b, wo_ref[...])
        dwo_ref[...] += _dot_tn(ym_ref[...], dx1b)
        db = dym * kb_ref[...].astype(F32)
        dcv = dym * kcv_ref[...].astype(F32)
        dg = dym * kg_ref[...].astype(F32)

        row = lax.broadcasted_iota(jnp.int32, (tm, 1), 0)
        w0, w1, w2 = cw_ref[0:1, :], cw_ref[1:2, :], cw_ref[2:3, :]
        nxt = carry_ref[...]
        n0, n1 = nxt[0:1, :], nxt[1:2, :]
        d1 = jnp.where(row == tm - 1, n0, pltpu.roll(dcv, tm - 1, 0))
        d2 = jnp.where(row == tm - 1, n1, jnp.where(row == tm - 2, n0, pltpu.roll(dcv, tm - 2, 0)))
        carry_ref[...] = dcv[0:8, :]
        dv = w2 * dcv + w1 * d1 + w0 * d2
        v = v_ref[...].astype(F32)
        dcw_ref[0:1, :] += jnp.sum(d2 * v, axis=0, keepdims=True)
        dcw_ref[1:2, :] += jnp.sum(d1 * v, axis=0, keepdims=True)
        dcw_ref[2:3, :] += jnp.sum(dcv * v, axis=0, keepdims=True)

        r0, xh = _rms(x_ref[...])
        ga = ga_ref[...]
        h = (xh * ga).astype(BF16)
        dh = jnp.zeros((tm, D_MODEL), F32)
        for idx, dpart in enumerate((db, dv * u_ref[...].astype(F32), dv * c_ref[...].astype(F32), dg)):
            dpb = dpart.astype(BF16)
            dh = dh + _dot_nt(dpb, wi_ref[idx])
            dwi_ref[idx] += _dot_tn(h, dpb)

            @pl.when(i == n - 1)
            def _():
                dwi_done[idx].start()
        dga_ref[...] += jnp.sum(dh * xh, axis=0, keepdims=True)
        dx_ref[...] = dx1 + _rms_bwd(dh * ga, xh, r0)

        @pl.when(i == n - 1)
        def _():
            for cp in dwi_done:
                cp.wait()

    acc_shapes = [(E_A, D_MODEL), (8, E_A), (1, D_MODEL)]
    return pl.pallas_call(
        body, grid=(n,), name="conv_bwd",
        in_specs=[rev(D_MODEL), rev(D_MODEL)] + [rev(E_A)] * 7
        + [_const_spec((E_A, D_MODEL)), _const_spec((4, D_MODEL, E_A)), _const_spec((1, D_MODEL)),
           _const_spec((8, E_A))],
        out_specs=[rev(D_MODEL), HBM] + [_acc_spec(s) for s in acc_shapes],
        out_shape=[jax.ShapeDtypeStruct((t, D_MODEL), F32), jax.ShapeDtypeStruct((4, D_MODEL, E_A), F32)]
        + [jax.ShapeDtypeStruct(s, F32) for s in acc_shapes],
        scratch_shapes=[pltpu.VMEM((8, E_A), F32), pltpu.VMEM((4, D_MODEL, E_A), F32), pltpu.SemaphoreType.DMA((4,))],
        compiler_params=_params("arbitrary"),
    )(dx1, x, c, u, v, ym, kb, kcv, kg, w_out, w_in4, ga, cw)


WEIGHTS = ("a_norm", "a_w_in", "a_conv", "a_w_out", "kv_norm", "w_dkv", "ckv_norm", "w_ukv", "b_norm", "b_w_in",
           "b_q_norm", "b_w_uq", "b_w_out", "final_norm")
SHARD_SHAPES = {
    "a_norm": (1, 256), "a_w_in": (1, 1024, 1024), "a_conv": (1, 3, 256), "a_w_out": (1, 256, 1024),
    "kv_norm": (1024,), "w_dkv": (256, 288), "ckv_norm": (256,), "w_ukv": (256, 256), "b_norm": (1, 1024),
    "b_w_in": (1, 256, 896), "b_q_norm": (1, 384), "b_w_uq": (1, 384, 192), "b_w_out": (1, 512, 256),
    "final_norm": (1024,),
}
MATS = ("a_w_in", "a_w_out", "w_dkv", "w_ukv", "b_w_in", "b_w_uq", "b_w_out")
SMALL = ("a_norm", "a_conv", "kv_norm", "ckv_norm", "b_norm", "b_q_norm", "final_norm")
SMALL_FULL = {"a_norm": 1024, "a_conv": 3072, "kv_norm": 1024, "ckv_norm": 256, "b_norm": 1024, "b_q_norm": 384,
              "final_norm": 1024}
SMALL_ROWS = 8
LOSS_SLOT = sum(SMALL_FULL.values())


def _mat2d(name, a):
    return a.reshape(SHARD_SHAPES[name][-2:])


def _prep_first(g_win, gsmall):
    sm = gsmall.reshape(N_CHIPS, -1)
    a_conv = jnp.transpose(sm[:, 256:1024].reshape(N_CHIPS, CONV_WIDTH, 256), (1, 0, 2)).reshape(CONV_WIDTH, -1)
    return {"w_in4": g_win, "ga": sm[:, :256].reshape(1, -1), "cw": jnp.pad(a_conv, ((0, 8 - CONV_WIDTH), (0, 0)))}


def _prep_rest(gath, w):
    def cols(a):
        return jnp.transpose(a, (1, 0, 2)).reshape(a.shape[1], -1)

    def pad_heads(a, width):
        a = a.reshape(a.shape[0], N_HEADS, width)
        return jnp.pad(a, ((0, 0), (0, 0), (0, HEAD_PAD - width))).reshape(a.shape[0], QK_PAD)

    row = lambda a: a.reshape(1, -1).astype(F32)
    w_dkv = gath["w_dkv"].reshape(D_MODEL, KV_RANK + QK_ROPE)
    w_ukv = cols(gath["w_ukv"]).reshape(KV_RANK, N_HEADS, 2, QK_NOPE)
    return {
        "w_out": gath["a_w_out"].reshape(E_A, D_MODEL),
        "w_dkv": jnp.concatenate([w_dkv[:, :KV_RANK], jnp.zeros((D_MODEL, ROPE_LO), BF16), w_dkv[:, KV_RANK:],
                                  jnp.zeros((D_MODEL, HEAD_PAD - ROPE_LO - QK_ROPE), BF16)], axis=1),
        "w_uk": pad_heads(w_ukv[:, :, 0, :].reshape(KV_RANK, N_HEADS * QK_NOPE), QK_NOPE),
        "w_uv": w_ukv[:, :, 1, :].reshape(KV_RANK, E_B),
        "w_bin": gath["b_w_in"].reshape(D_MODEL, Q_RANK + E_B),
        "w_uq": pad_heads(cols(gath["b_w_uq"]), QK_NOPE + QK_ROPE),
        "w_bout": cols(gath["b_w_out"]),
        "gk": row(w["kv_norm"]), "gc": row(w["ckv_norm"]), "gb": row(w["b_norm"]),
        "gq": row(w["b_q_norm"]), "gf": row(w["final_norm"]),
    }


TRANSPOSED = ("w_dkv", "b_w_uq")
GRAD_MATS = ("w_in4", "w_out", "w_dkv", "w_uk", "w_uv", "w_bin", "w_uq", "w_bout")
GRAD_KIND = {"w_in4": "lead", "w_out": "row", "w_dkv": "row", "w_uk": "col", "w_uv": "col", "w_bin": "row",
             "w_uq": "col", "w_bout": "col"}


def _local_step(x, positions, tgt, w):
    bsz, seq, _ = x.shape
    t = bsz * seq
    x2d = x.reshape(t, D_MODEL)
    small = jnp.concatenate([w["a_norm"].reshape(-1), w["a_conv"].reshape(-1)]).reshape(8, LANES)
    rc, rs1, rs2, g_win, gsmall = _rope_tables_gather(positions.reshape(t, 1), _mat2d(MATS[0], w[MATS[0]]), small)
    wk = _prep_first(g_win, gsmall)
    (c, u, v, ym, kb, kcv, kg), gathered = _conv_fwd(x2d, seq, wk["ga"], wk["w_in4"], wk["cw"],
                                           [_mat2d(n, w[n]) for n in MATS[1:]])
    wk.update(_prep_rest(dict(zip(MATS[1:], gathered)), w))
    x1, q, k, vv, gate, cq, ckv = _mid_fwd(x2d, ym, wk["w_out"], wk["gk"], wk["gb"], wk["w_dkv"], wk["gc"], wk["w_uk"],
                                          wk["w_uv"], wk["w_bin"], wk["gq"], wk["w_uq"], rc, rs1, rs2)
    o, lse = _attn_fwd(q, k, vv, seq)
    dx2, do, dgate, dd, loss, dgf, dw_bout = _head_fwd_bwd(o, gate, x1, tgt.reshape(t, D_MODEL), wk["w_bout"], wk["gf"])
    dq, dk, dv = _attn_bwd(q, k, vv, do, lse, dd, seq)
    dx1, dwq, dw_bin, dwk, dwv, dwd, dgq, dgc, dgk, dgb = _mid_bwd(
        dq, dk, dv, dgate, dx2, x1, cq, ckv, rc, rs1, rs2, wk["w_uq"], wk["w_bin"], wk["w_uk"], wk["w_uv"], wk["w_dkv"],
        wk["gq"], wk["gc"], wk["gk"], wk["gb"])
    dx, dw_in4, dw_out, dcw, dga = _conv_bwd(dx1, x2d, c, u, v, ym, kb, kcv, kg, seq, wk["w_out"], wk["w_in4"], wk["ga"], wk["cw"])
    mats = {"w_in4": dw_in4, "w_out": dw_out, "w_dkv": dwd, "w_uk": dwk, "w_uv": dwv, "w_bin": dw_bin, "w_uq": dwq,
            "w_bout": dw_bout}
    small = {"a_norm": dga, "a_conv": dcw[:CONV_WIDTH], "kv_norm": dgk, "ckv_norm": dgc, "b_norm": dgb,
             "b_q_norm": dgq, "final_norm": dgf}
    return loss[0, 0], dx.reshape(bsz, seq, D_MODEL), mats, small


def _shard_grads(sh, svec):
    j0 = 2 * lax.axis_index("x") + lax.axis_index("y")
    flat = svec.reshape(-1)
    off, small = 0, {}
    for n in SMALL:
        small[n] = flat[off:off + SMALL_FULL[n]]
        off += SMALL_FULL[n]
    dwd, dwk, dwv, dwq = sh["w_dkv"], sh["w_uk"], sh["w_uv"], sh["w_uq"]
    w_ukv = jnp.stack([dwk.reshape(KV_RANK, 2, HEAD_PAD)[:, :, :QK_NOPE], dwv.reshape(KV_RANK, 2, V_HEAD)], axis=2)
    return {
        "a_norm": lax.dynamic_slice(small["a_norm"], (j0 * 256,), (256,)),
        "a_conv": lax.dynamic_slice(small["a_conv"].reshape(CONV_WIDTH, E_A), (0, j0 * 256), (CONV_WIDTH, 256)),
        "kv_norm": small["kv_norm"], "ckv_norm": small["ckv_norm"], "b_norm": small["b_norm"],
        "b_q_norm": small["b_q_norm"], "final_norm": small["final_norm"],
        "a_w_in": sh["w_in4"], "a_w_out": sh["w_out"],
        "w_dkv": jnp.concatenate([dwd[:, :KV_RANK], dwd[:, KV_RANK + ROPE_LO:KV_RANK + ROPE_LO + QK_ROPE]], axis=1),
        "w_ukv": w_ukv.reshape(KV_RANK, 2 * (QK_NOPE + V_HEAD)),
        "b_w_in": sh["w_bin"],
        "b_w_uq": dwq.reshape(Q_RANK, 2, HEAD_PAD)[:, :, :QK_NOPE + QK_ROPE].reshape(Q_RANK, -1),
        "b_w_out": sh["w_bout"],
    }


def _sub(ref, kind, j, cc):
    if kind == "lead":
        h = ref.shape[1] // 2
        return ref.at[j, pl.ds(pl.multiple_of(cc * h, 8), h), :]
    if kind == "row":
        rows = ref.shape[0] // N_CHIPS
        h = rows // 2
        return ref.at[pl.ds(pl.multiple_of(j * rows + cc * h, 8), h), :]
    cols = ref.shape[1] // N_CHIPS
    h = ref.shape[0] // 2
    return ref.at[pl.ds(pl.multiple_of(cc * h, 8), h), pl.ds(j * cols, cols)]


def _sub_shape(shape, kind):
    if kind == "lead":
        return (shape[1] // 2, shape[2])
    if kind == "row":
        return (shape[0] // N_CHIPS // 2, shape[1])
    return (shape[0] // 2, shape[1] // N_CHIPS)


def _reduce_grads(grads, kinds, vec):
    n = len(grads)
    shapes = [_sub_shape(a.shape, kd) for a, kd in zip(grads, kinds)]
    units = [(k, j) for k in range(n) for j in range(N_CHIPS)]
    big = (max(s[0] for s in shapes), max(s[1] for s in shapes))

    def body(*refs):
        g, v_ref = refs[:n], refs[n]
        out, o_ref = refs[n + 1:2 * n + 1], refs[2 * n + 1]
        theirs, part, recd, red = (refs[(2 + i) * n + 2:(3 + i) * n + 2] for i in range(4))
        mine, got = refs[6 * n + 2], refs[6 * n + 3]
        send1, recv1, send3, recv3, send5, recv5, load, local, send_v, recv_v = refs[6 * n + 4:]
        x, y, c = _place()
        j0 = 2 * x + y
        me = 2 * j0 + c
        sibling = (x, y, 1 - c)

        got[me] = v_ref[...]
        first, small_in = [], []
        for d in range(1, N_DEV):
            px, py, pc = x ^ (d >> 2), y ^ ((d >> 1) & 1), c ^ (d & 1)
            first.append(_remote(v_ref, got.at[me], send_v.at[d - 1], recv_v.at[d - 1], (px, py, pc)))
            small_in.append(_remote(v_ref, got.at[4 * px + 2 * py + pc], send_v.at[d - 1], recv_v.at[d - 1],
                                    (px, py, pc)))
        halves = [_remote(_sub(g[k], kinds[k], j, 1 - c), theirs[k].at[j], send1.at[u], recv1.at[u], sibling)
                  for u, (k, j) in enumerate(units)]
        for cp in first + halves:
            cp.start()

        def mine_load(u):
            k, j = units[u]
            h, cols = shapes[k]
            return pltpu.make_async_copy(_sub(g[k], kinds[k], j, c), mine.at[u % 2, pl.ds(0, h), pl.ds(0, cols)],
                                         load.at[u % 2])

        mine_load(0).start()
        for u, (k, j) in enumerate(units):
            h, cols = shapes[k]
            if u + 1 < len(units):
                mine_load(u + 1).start()
            mine_load(u).wait()
            halves[u].wait_recv()
            part[k][j] = (mine[u % 2, 0:h, 0:cols] + theirs[k][j]).astype(BF16)
            to_owner = _remote(part[k].at[j], recd[k].at[j0], send3.at[u], recv3.at[4 * k + j0], (j // 2, j % 2, c))

            @pl.when(j != j0)
            def _():
                to_owner.start()

            @pl.when(j == j0)
            def _():
                recd[k][j] = part[k][j]

        swaps = []
        for k in range(n):
            for j in range(N_CHIPS):
                arrived = _remote(part[k].at[j], recd[k].at[j], send3.at[4 * k + j], recv3.at[4 * k + j],
                                  (j // 2, j % 2, c))

                @pl.when(j != j0)
                def _():
                    arrived.wait_recv()

            r = recd[k]
            red[k][...] = ((r[0].astype(F32) + r[1].astype(F32)) + r[2].astype(F32)) + r[3].astype(F32)
            own = pltpu.make_async_copy(red[k], out[k].at[c], local.at[k])
            give = _remote(red[k], out[k].at[c], send5.at[k], recv5.at[k], sibling)
            take = _remote(red[k], out[k].at[1 - c], send5.at[k], recv5.at[k], sibling)
            own.start()
            give.start()
            swaps.append((own, give, take))

        for cp in small_in:
            cp.wait_recv()
            cp.wait_send()
        acc = got[0]
        for d in range(1, N_DEV):
            acc = acc + got[d]
        o_ref[...] = acc
        for u, (k, j) in enumerate(units):
            halves[u].wait_send()
            sent = _remote(part[k].at[j], recd[k].at[j0], send3.at[u], recv3.at[u], (j // 2, j % 2, c))

            @pl.when(j != j0)
            def _():
                sent.wait_send()
        for own, give, take in swaps:
            take.wait_recv()
            give.wait_send()
            own.wait()

    sems = [pltpu.SemaphoreType.DMA((len(units),))] * 4 + [pltpu.SemaphoreType.DMA((n,))] * 2 \
        + [pltpu.SemaphoreType.DMA((2,)), pltpu.SemaphoreType.DMA((n,))] + [pltpu.SemaphoreType.DMA((N_DEV - 1,))] * 2
    res = pl.pallas_call(
        body, name="reduce_grads",
        in_specs=[HBM] * n + [WHOLE], out_specs=[HBM] * n + [WHOLE],
        out_shape=[jax.ShapeDtypeStruct((2,) + s, F32) for s in shapes] + [jax.ShapeDtypeStruct(vec.shape, vec.dtype)],
        scratch_shapes=[pltpu.VMEM((N_CHIPS,) + s, F32) for s in shapes]
        + [pltpu.VMEM((N_CHIPS,) + s, BF16) for s in shapes] * 2
        + [pltpu.VMEM(s, F32) for s in shapes]
        + [pltpu.VMEM((2,) + big, F32), pltpu.VMEM((N_DEV,) + vec.shape, vec.dtype)] + sems,
        compiler_params=_comm_params(),
    )(*grads, vec)
    return res[:n], res[n]


def _adamw_math(w, g, m, v):
    m = ADAM_B1 * m + (1.0 - ADAM_B1) * g
    v = ADAM_B2 * v + (1.0 - ADAM_B2) * (g * g)
    m_hat = m / (1.0 - ADAM_B1 ** ADAM_STEP)
    v_hat = v / (1.0 - ADAM_B2 ** ADAM_STEP)
    return -ADAM_LR * (m_hat / (jnp.sqrt(v_hat) + ADAM_EPS) + ADAM_WD * w), m, v


def _adamw_tiled(w, g, m, v):
    rows, width = w.shape
    tm = rows // 4

    def body(w_ref, g_ref, m_ref, v_ref, go_ref, d_ref, mo_ref, vo_ref):
        g = g_ref[...]
        go_ref[...] = g
        d_ref[...], mo_ref[...], vo_ref[...] = _adamw_math(w_ref[...], g, m_ref[...], v_ref[...])

    spec = pl.BlockSpec((tm, width), lambda i: (i, 0))
    out = jax.ShapeDtypeStruct((rows, width), F32)
    return pl.pallas_call(
        body, grid=(rows // tm,), name="adamw_tiled",
        in_specs=[spec] * 4, out_specs=[spec] * 4, out_shape=[out] * 4,
        compiler_params=_params("parallel"),
    )(w, g, m, v)


def _adamw_many(ws, gs, ms, vs):
    n = len(ws)

    def body(*refs):
        for k in range(n):
            w_ref, g_ref, m_ref, v_ref = (refs[i * n + k] for i in range(4))
            go_ref, d_ref, mo_ref, vo_ref = (refs[(4 + i) * n + k] for i in range(4))
            g = g_ref[...]
            go_ref[...] = g
            d_ref[...], mo_ref[...], vo_ref[...] = _adamw_math(w_ref[...], g, m_ref[...], v_ref[...])

    outs = [jax.ShapeDtypeStruct(a.shape, F32) for a in ws]
    res = pl.pallas_call(
        body, name="adamw_many",
        in_specs=[WHOLE] * (4 * n), out_specs=[WHOLE] * (4 * n), out_shape=outs * 4,
        compiler_params=_comm_params(),
    )(*ws, *gs, *ms, *vs)
    return res[:n], res[n:2 * n], res[2 * n:3 * n], res[3 * n:]


def kernel(x, positions, a_norm, a_w_in, a_conv, a_w_out, kv_norm, w_dkv, ckv_norm, w_ukv, b_norm, b_w_in, b_q_norm, b_w_uq, b_w_out, final_norm, loss_target, m_a_norm, m_a_w_in, m_a_conv, m_a_w_out, m_kv_norm, m_w_dkv, m_ckv_norm, m_w_ukv, m_b_norm, m_b_w_in, m_b_q_norm, m_b_w_uq, m_b_w_out, m_final_norm, v_a_norm, v_a_w_in, v_a_conv, v_a_w_out, v_kv_norm, v_w_dkv, v_ckv_norm, v_w_ukv, v_b_norm, v_b_w_in, v_b_q_norm, v_b_w_uq, v_b_w_out, v_final_norm):
    w = dict(a_norm=a_norm, a_w_in=a_w_in, a_conv=a_conv, a_w_out=a_w_out, kv_norm=kv_norm, w_dkv=w_dkv,
             ckv_norm=ckv_norm, w_ukv=w_ukv, b_norm=b_norm, b_w_in=b_w_in, b_q_norm=b_q_norm, b_w_uq=b_w_uq,
             b_w_out=b_w_out, final_norm=final_norm)
    m = dict(a_norm=m_a_norm, a_w_in=m_a_w_in, a_conv=m_a_conv, a_w_out=m_a_w_out, kv_norm=m_kv_norm, w_dkv=m_w_dkv,
             ckv_norm=m_ckv_norm, w_ukv=m_w_ukv, b_norm=m_b_norm, b_w_in=m_b_w_in, b_q_norm=m_b_q_norm,
             b_w_uq=m_b_w_uq, b_w_out=m_b_w_out, final_norm=m_final_norm)
    v = dict(a_norm=v_a_norm, a_w_in=v_a_w_in, a_conv=v_a_conv, a_w_out=v_a_w_out, kv_norm=v_kv_norm, w_dkv=v_w_dkv,
             ckv_norm=v_ckv_norm, w_ukv=v_w_ukv, b_norm=v_b_norm, b_w_in=v_b_w_in, b_q_norm=v_b_q_norm,
             b_w_uq=v_b_w_uq, b_w_out=v_b_w_out, final_norm=v_final_norm)

    loss, dx, gmat, gsmall = _local_step(x, positions, loss_target, w)

    kinds = [GRAD_KIND[n] for n in GRAD_MATS]
    grads = [gmat[n] for n in GRAD_MATS]
    flat = jnp.concatenate([gsmall[n].reshape(-1) for n in SMALL] + [loss.reshape(1)])
    flat = jnp.pad(flat, (0, SMALL_ROWS * PACK_W - flat.shape[0])).reshape(SMALL_ROWS, PACK_W)
    mine, svec = _reduce_grads(grads, kinds, flat)
    loss = svec.reshape(-1)[LOSS_SLOT]
    g = _shard_grads({n: a.reshape(-1, a.shape[-1]) for n, a in zip(GRAD_MATS, mine)}, svec)

    def two_d(n, a):
        a = a.reshape(-1, a.shape[-1])
        return a.T if n in TRANSPOSED else a

    def back(n, a):
        return (a.T if n in TRANSPOSED else a).reshape(SHARD_SHAPES[n])

    big = "a_w_in"
    rest = [n for n in WEIGHTS if n != big]
    res_big = _adamw_tiled(*(two_d(big, t[big]) for t in (w, g, m, v)))
    res_rest = _adamw_many(*([two_d(n, t[n]) for n in rest] for t in (w, g, m, v)))
    out = {kind: dict(zip(rest, res_rest[i])) for i, kind in enumerate("gdmv")}
    for i, kind in enumerate("gdmv"):
        out[kind][big] = res_big[i]
    return (loss, dx) + tuple(back(n, out[kind][n]) for kind in "gdmv" for n in WEIGHTS)
```
